```python
import math
import jax, jax.numpy as jnp
from jax import lax
import numpy as np

D_MODEL = 1024
BATCH = 8
SEQ = 2048
DEPTH = 1
DEC_BATCH = 128
DEC_SEQ = 4
PAST_LEN = 16384
PAGE_SIZE = 128

D_MIX = D_MODEL
D_SSD = D_MIX // 2
SSD_HEAD_DIM = 64
SSD_HEADS = D_SSD // SSD_HEAD_DIM
SSD_GROUPS = 2
SSD_STATE = 128
SSD_CONV_W = 4
SSD_CHUNK = 128
SSD_CONV_DIM = D_SSD + 2 * SSD_GROUPS * SSD_STATE
D_S5 = D_MIX - D_SSD
S5_CH = 16
S5_GROUPS = D_S5 // S5_CH
S5_STATE = 64
S5_DT_MIN = 0.001
S5_DT_MAX = 0.1
SSD_DT_MIN = 0.001
SSD_DT_MAX = 0.1
D_FF = -(-8 * D_MODEL // (3 * 256)) * 256
IN_PROJ = D_SSD + SSD_CONV_DIM + SSD_HEADS + D_S5
N_ADA = 6
EPS = 1e-6

kernel_name = "hymba_ssd_s5_adaln_decode_step"


def _rmsnorm(x, g):
    xf = x.astype(jnp.float32)
    y = xf * lax.rsqrt(jnp.mean(xf * xf, axis=-1, keepdims=True) + EPS)
    return (y * g.astype(jnp.float32)).astype(x.dtype)


def _segsum(a):
    T = a.shape[-1]
    aa = jnp.broadcast_to(a[..., None], a.shape + (T,))
    aa = jnp.where(jnp.tril(jnp.ones((T, T), bool), -1), aa, 0.0)
    ss = jnp.cumsum(aa, axis=-2)
    return jnp.where(jnp.tril(jnp.ones((T, T), bool)), ss, -jnp.inf)


def _ssd_chunked(X, dA, Bm, Cm, h0):
    b, L, H, P = X.shape
    N = Bm.shape[-1]
    T = min(SSD_CHUNK, L)
    nc = L // T
    X = X.reshape(b, nc, T, H, P)
    Bm = Bm.reshape(b, nc, T, H, N)
    Cm = Cm.reshape(b, nc, T, H, N)
    A = dA.reshape(b, nc, T, H).transpose(0, 3, 1, 2)
    A_cs = jnp.cumsum(A, axis=-1)
    Lmat = jnp.exp(_segsum(A))
    scores = jnp.einsum("bclhn,bcshn->bhcls", Cm, Bm) * Lmat
    y_diag = jnp.einsum("bhcls,bcshp->bclhp", scores, X)
    decay_states = jnp.exp(A_cs[..., -1:] - A_cs).transpose(0, 2, 3, 1)
    states = jnp.einsum("bclhn,bclhp->bchpn", Bm * decay_states[..., None], X)
    states = jnp.concatenate([h0[:, None], states], axis=1)
    chunk_tot = jnp.pad(A_cs[..., -1], ((0, 0), (0, 0), (1, 0)))
    decay_chunk = jnp.exp(_segsum(chunk_tot))
    new_states = jnp.einsum("bhzc,bchpn->bzhpn", decay_chunk, states)
    y_off = jnp.einsum("bclhn,bchpn->bclhp", Cm, new_states[:, :-1]) * \
        jnp.exp(A_cs).transpose(0, 2, 3, 1)[..., None]
    y = (y_diag + y_off).reshape(b, L, H, P)
    return y, new_states[:, -1]


def _ssd_mixer(z, xbc, dt_raw, conv_buf, h0, conv_w, conv_b, dt_bias, A_log, D_skip, norm_g):
    b, L, _ = xbc.shape
    xbc_full = jnp.concatenate([conv_buf.astype(xbc.dtype), xbc], axis=1)
    conv = conv_b + sum(xbc_full[:, k:k + L] * conv_w[k] for k in range(SSD_CONV_W))
    conv_new = xbc_full[:, L:]
    act = jax.nn.silu(conv.astype(jnp.float32))
    xs = act[..., :D_SSD].reshape(b, L, SSD_HEADS, SSD_HEAD_DIM)
    rep = SSD_HEADS // SSD_GROUPS
    Bs = jnp.repeat(act[..., D_SSD:D_SSD + SSD_GROUPS * SSD_STATE].reshape(b, L, SSD_GROUPS, SSD_STATE), rep, axis=2)
    Cs = jnp.repeat(act[..., D_SSD + SSD_GROUPS * SSD_STATE:].reshape(b, L, SSD_GROUPS, SSD_STATE), rep, axis=2)
    dt = jax.nn.softplus(dt_raw.astype(jnp.float32) + dt_bias.astype(jnp.float32))
    A = -jnp.exp(A_log.astype(jnp.float32))
    y, h_new = _ssd_chunked(xs * dt[..., None], dt * A, Bs, Cs, h0.astype(jnp.float32))
    y = y + D_skip.astype(jnp.float32)[:, None] * xs
    y = y.reshape(b, L, D_SSD) * jax.nn.silu(z.astype(jnp.float32))
    yg = y.reshape(b, L, SSD_GROUPS, D_SSD // SSD_GROUPS)
    yg = yg * lax.rsqrt(jnp.mean(yg * yg, axis=-1, keepdims=True) + EPS)
    y = yg.reshape(b, L, D_SSD) * norm_g.astype(jnp.float32)
    return y, h_new, conv_new


def _s5_combine(e1, e2):
    a1r, a1i, b1r, b1i = e1
    a2r, a2i, b2r, b2i = e2
    return (a2r * a1r - a2i * a1i,
            a2r * a1i + a2i * a1r,
            a2r * b1r - a2i * b1i + b2r,
            a2r * b1i + a2i * b1r + b2i)


def _s5_mixer(u5, h0_re, h0_im, A_re, A_im, log_step, B_re, B_im, C_re, C_im, D_skip, w_glu, b_glu):
    b, L, _ = u5.shape
    f32 = jnp.float32
    uc = u5.astype(f32).reshape(b, L, S5_GROUPS, S5_CH)
    lr, li = A_re.astype(f32), A_im.astype(f32)
    step = jnp.exp(log_step.astype(f32))[:, None]
    mag = jnp.exp(lr * step)
    ab_re, ab_im = mag * jnp.cos(li * step), mag * jnp.sin(li * step)
    nr, ni = ab_re - 1.0, ab_im
    den = lr * lr + li * li
    f_re = (nr * lr + ni * li) / den
    f_im = (ni * lr - nr * li) / den
    Br, Bi = B_re.astype(f32), B_im.astype(f32)
    Bb_re = f_re[..., None] * Br - f_im[..., None] * Bi
    Bb_im = f_re[..., None] * Bi + f_im[..., None] * Br
    bu_re = jnp.einsum("gpc,blgc->blgp", Bb_re, uc)
    bu_im = jnp.einsum("gpc,blgc->blgp", Bb_im, uc)
    a_re = jnp.broadcast_to(ab_re, bu_re.shape)
    a_im = jnp.broadcast_to(ab_im, bu_im.shape)
    Ar, Ai, Hr, Hi = lax.associative_scan(_s5_combine, (a_re, a_im, bu_re, bu_im), axis=1)
    r0, i0 = h0_re.astype(f32)[:, None], h0_im.astype(f32)[:, None]
    h_re = Hr + Ar * r0 - Ai * i0
    h_im = Hi + Ar * i0 + Ai * r0
    y = jnp.einsum("gcp,blgp->blgc", C_re.astype(f32), h_re) - \
        jnp.einsum("gcp,blgp->blgc", C_im.astype(f32), h_im)
    y = y.reshape(b, L, D_S5) + D_skip.astype(f32) * uc.reshape(b, L, D_S5)
    g = jax.nn.gelu(y, approximate=False)
    out = g * jax.nn.sigmoid(g @ w_glu.astype(f32) + b_glu.astype(f32))
    return out, h_re[:, -1], h_im[:, -1]


def _layer(x, c, conv_buf, h_ssd0, s5_re0, s5_im0,
           w_ada, b_ada, norm1_g, w_in, conv_w, conv_b, ssd_dt_bias, ssd_A_log, ssd_D, ssd_norm_g,
           s5_A_re, s5_A_im, s5_log_step, s5_B_re, s5_B_im, s5_C_re, s5_C_im, s5_D, w_glu, b_glu,
           w_out, norm2_g, w_ffn_gate, w_ffn_up, w_ffn_down):
    mod = (jax.nn.silu(c) @ w_ada + b_ada)[:, None, :]
    sh1, sc1, g1, sh2, sc2, g2 = jnp.split(mod, N_ADA, axis=-1)
    u = _rmsnorm(x, norm1_g) * (1 + sc1) + sh1
    proj = u @ w_in
    o1 = D_SSD
    o2 = o1 + SSD_CONV_DIM
    o3 = o2 + SSD_HEADS
    y_ssd, h_ssd, conv_new = _ssd_mixer(proj[..., :o1], proj[..., o1:o2], proj[..., o2:o3], conv_buf, h_ssd0,
                                        conv_w, conv_b, ssd_dt_bias, ssd_A_log, ssd_D, ssd_norm_g)
    y_s5, re_new, im_new = _s5_mixer(proj[..., o3:], s5_re0, s5_im0, s5_A_re, s5_A_im, s5_log_step,
                                     s5_B_re, s5_B_im, s5_C_re, s5_C_im, s5_D, w_glu, b_glu)
    mix = jnp.concatenate([y_ssd, y_s5], axis=-1).astype(x.dtype)
    x = x + g1 * (mix @ w_out)
    v = _rmsnorm(x, norm2_g) * (1 + sc2) + sh2
    ff = (jax.nn.silu(v @ w_ffn_gate) * (v @ w_ffn_up)) @ w_ffn_down
    x = x + g2 * ff
    return x, h_ssd, conv_new, re_new, im_new


def setup_inputs(seed: int = 0) -> dict:
    key = jax.random.key(seed)
    ks = iter(jax.random.split(key, 48))

    def nrm(shape, scale):
        return jax.random.normal(next(ks), shape, jnp.float32) * scale

    def unif(shape, lo, hi):
        return jax.random.uniform(next(ks), shape, jnp.float32, lo, hi)

    Dp = DEPTH
    d = {}
    d["x_prompt"] = nrm((BATCH, SEQ, D_MODEL), 1.0)
    d["x_sample"] = nrm((DEC_BATCH, DEC_SEQ, D_MODEL), 1.0)
    d["c_prompt"] = nrm((BATCH, D_MODEL), 1.0)
    d["c_sample"] = nrm((DEC_BATCH, D_MODEL), 1.0)
    d["state_ssd"] = nrm((Dp, DEC_BATCH, SSD_HEADS, SSD_HEAD_DIM, SSD_STATE), 0.1)
    d["state_conv"] = nrm((Dp, DEC_BATCH, SSD_CONV_W - 1, SSD_CONV_DIM), 1.0)
    d["state_s5_re"] = nrm((Dp, DEC_BATCH, S5_GROUPS, S5_STATE), 0.1)
    d["state_s5_im"] = nrm((Dp, DEC_BATCH, S5_GROUPS, S5_STATE), 0.1)
    d["w_ada"] = nrm((Dp, D_MODEL, N_ADA * D_MODEL), 0.5 * D_MODEL ** -0.5)
    d["b_ada"] = nrm((Dp, N_ADA * D_MODEL), 0.02)
    d["norm1_g"] = 1.0 + nrm((Dp, D_MODEL), 0.02)
    d["w_in"] = nrm((Dp, D_MODEL, IN_PROJ), D_MODEL ** -0.5)
    d["conv_w"] = nrm((Dp, SSD_CONV_W, SSD_CONV_DIM), SSD_CONV_W ** -0.5)
    d["conv_b"] = nrm((Dp, SSD_CONV_DIM), 0.02)
    dt0 = jnp.exp(unif((Dp, SSD_HEADS), math.log(SSD_DT_MIN), math.log(SSD_DT_MAX)))
    d["ssd_dt_bias"] = dt0 + jnp.log(-jnp.expm1(-dt0))
    d["ssd_A_log"] = jnp.log(unif((Dp, SSD_HEADS), 1.0, 16.0))
    d["ssd_D"] = 1.0 + nrm((Dp, SSD_HEADS), 0.02)
    d["ssd_norm_g"] = 1.0 + nrm((Dp, D_SSD), 0.02)
    n_idx = jnp.arange(S5_STATE, dtype=jnp.float32)
    d["s5_A_re"] = -0.5 + nrm((Dp, S5_GROUPS, S5_STATE), 0.01)
    d["s5_A_im"] = jnp.broadcast_to(math.pi * n_idx, (Dp, S5_GROUPS, S5_STATE)) + nrm((Dp, S5_GROUPS, S5_STATE), 0.01)
    d["s5_log_step"] = unif((Dp, S5_GROUPS), math.log(S5_DT_MIN), math.log(S5_DT_MAX))
    d["s5_B_re"] = nrm((Dp, S5_GROUPS, S5_STATE, S5_CH), (2.0 * S5_CH) ** -0.5)
    d["s5_B_im"] = nrm((Dp, S5_GROUPS, S5_STATE, S5_CH), (2.0 * S5_CH) ** -0.5)
    d["s5_C_re"] = nrm((Dp, S5_GROUPS, S5_CH, S5_STATE), (2.0 * S5_STATE) ** -0.5)
    d["s5_C_im"] = nrm((Dp, S5_GROUPS, S5_CH, S5_STATE), (2.0 * S5_STATE) ** -0.5)
    d["s5_D"] = nrm((Dp, D_S5), 1.0)
    d["w_glu"] = nrm((Dp, D_S5, D_S5), D_S5 ** -0.5)
    d["b_glu"] = nrm((Dp, D_S5), 0.02)
    d["w_out"] = nrm((Dp, D_MIX, D_MODEL), D_MIX ** -0.5)
    d["norm2_g"] = 1.0 + nrm((Dp, D_MODEL), 0.02)
    d["w_ffn_gate"] = nrm((Dp, D_MODEL, D_FF), D_MODEL ** -0.5)
    d["w_ffn_up"] = nrm((Dp, D_MODEL, D_FF), D_MODEL ** -0.5)
    d["w_ffn_down"] = nrm((Dp, D_FF, D_MODEL), D_FF ** -0.5)
    d["w_ada_f"] = nrm((D_MODEL, 2 * D_MODEL), 0.5 * D_MODEL ** -0.5)
    d["b_ada_f"] = nrm((2 * D_MODEL,), 0.02)
    d["normf_g"] = 1.0 + nrm((D_MODEL,), 0.02)
    return d


def reference(x_prompt, x_sample, c_prompt, c_sample, state_ssd, state_conv, state_s5_re, state_s5_im,
              w_ada, b_ada, norm1_g, w_in, conv_w, conv_b, ssd_dt_bias, ssd_A_log, ssd_D, ssd_norm_g,
              s5_A_re, s5_A_im, s5_log_step, s5_B_re, s5_B_im, s5_C_re, s5_C_im, s5_D, w_glu, b_glu,
              w_out, norm2_g, w_ffn_gate, w_ffn_up, w_ffn_down, w_ada_f, b_ada_f, normf_g):
    layer_w = (w_ada, b_ada, norm1_g, w_in, conv_w, conv_b, ssd_dt_bias, ssd_A_log, ssd_D, ssd_norm_g,
               s5_A_re, s5_A_im, s5_log_step, s5_B_re, s5_B_im, s5_C_re, s5_C_im, s5_D, w_glu, b_glu,
               w_out, norm2_g, w_ffn_gate, w_ffn_up, w_ffn_down)

    def run(x, c, ssd0, conv0, re0, im0):
        ssd_n, conv_n, re_n, im_n = [], [], [], []
        for l in range(DEPTH):
            x, h, cb, r, i = _layer(x, c, conv0[l], ssd0[l], re0[l], im0[l], *[w[l] for w in layer_w])
            ssd_n.append(h)
            conv_n.append(cb)
            re_n.append(r)
            im_n.append(i)
        mod = (jax.nn.silu(c) @ w_ada_f + b_ada_f)[:, None, :]
        shf, scf = jnp.split(mod, 2, axis=-1)
        y = _rmsnorm(x, normf_g) * (1 + scf) + shf
        return y, jnp.stack(ssd_n), jnp.stack(conv_n), jnp.stack(re_n), jnp.stack(im_n)

    bp = x_prompt.shape[0]
    f32 = jnp.float32
    y_prompt, ssd_p, conv_p, re_p, im_p = run(
        x_prompt, c_prompt,
        jnp.zeros((DEPTH, bp, SSD_HEADS, SSD_HEAD_DIM, SSD_STATE), f32),
        jnp.zeros((DEPTH, bp, SSD_CONV_W - 1, SSD_CONV_DIM), x_prompt.dtype),
        jnp.zeros((DEPTH, bp, S5_GROUPS, S5_STATE), f32),
        jnp.zeros((DEPTH, bp, S5_GROUPS, S5_STATE), f32))
    y_sample, ssd_s, conv_s, re_s, im_s = run(
        x_sample, c_sample, state_ssd, state_conv, state_s5_re, state_s5_im)
    return (y_prompt, y_sample, ssd_p, ssd_s, conv_p, conv_s, re_p, re_s, im_p, im_s)
```

```python
import functools

import jax
import jax.numpy as jnp
from jax import lax
from jax.experimental import pallas as pl
from jax.experimental.pallas import tpu as pltpu

f32 = jnp.float32
bf16 = jnp.bfloat16

D_MODEL = 1024
D_SSD = 512
HEAD_DIM = 64
HEADS = 8
GROUPS = 2
HEADS_PER_GROUP = HEADS // GROUPS
STATE = 128
CONV_W = 4
CONV_DIM = D_SSD + 2 * GROUPS * STATE
D_S5 = 512
S5_CH = 16
S5_GROUPS = 32
S5_STATE = 64
D_FF = 2816
N_ADA = 6
EPS = 1e-6

LANES = 128
SUBLANES = 8
SSD_CHUNK = 128
S5_SLABS = D_S5 // LANES
S5_SLAB_STATE = (S5_GROUPS // S5_SLABS) * S5_STATE
DT_PAD = LANES
IN_PROJ_PACKED = D_SSD + CONV_DIM + D_S5 + DT_PAD
VMEM_LIMIT = 56 * 1024 * 1024

NT_DIMS = (((1,), (1,)), ((), ()))
TN_DIMS = (((0,), (0,)), ((), ()))


def _silu(x):
    return x * jax.nn.sigmoid(x)


def _rms_mod(x, g, sc, sh):
    y = x * lax.rsqrt(jnp.mean(x * x, axis=-1, keepdims=True) + EPS)
    return (y * g) * (1.0 + sc) + sh


def _mod_row(mod_ref, i, per_row):
    return mod_ref[i] if per_row else mod_ref[0, i:i + 1, :]


def _const_spec(shape):
    nd = len(shape)
    return pl.BlockSpec(shape, lambda *_: (0,) * nd)


def _params(*sem):
    return pltpu.CompilerParams(dimension_semantics=sem, vmem_limit_bytes=VMEM_LIMIT)


def _ada_kernel(c_ref, w_ref, b_ref, o_ref):
    s = _silu(c_ref[...]).astype(bf16)
    o_ref[...] = jnp.dot(s, w_ref[...].astype(bf16), preferred_element_type=f32) + b_ref[...]


def _ada_mod(c, w, b, tn=512):
    m, k = c.shape
    n = w.shape[1]
    return pl.pallas_call(
        _ada_kernel,
        grid=(n // tn,),
        in_specs=[pl.BlockSpec((m, k), lambda j: (0, 0)),
                  pl.BlockSpec((k, tn), lambda j: (0, j)),
                  pl.BlockSpec((1, tn), lambda j: (0, j))],
        out_specs=pl.BlockSpec((m, tn), lambda j: (0, j)),
        out_shape=jax.ShapeDtypeStruct((m, n), f32),
        compiler_params=_params("parallel"),
        name="ada_mod",
    )(c, w, b.reshape(1, n))


def _inproj_kernel(x_ref, mod_ref, g_ref, w_ref, z_ref, xbc_ref, u5_ref, dt_ref, *, per_row):
    sh = _mod_row(mod_ref, 0, per_row)
    sc = _mod_row(mod_ref, 1, per_row)
    u = _rms_mod(x_ref[0], g_ref[...], sc, sh).astype(bf16)
    o = 0
    for ref, width in ((z_ref, D_SSD), (xbc_ref, CONV_DIM), (u5_ref, D_S5), (dt_ref, DT_PAD)):
        ref[0] = jnp.dot(u, w_ref[:, o:o + width], preferred_element_type=f32)
        o += width


def _inproj(x, mod, g, w, *, tm, per_row):
    nb, rows, d = x.shape
    nt = rows // tm
    if per_row:
        mod_spec = pl.BlockSpec((N_ADA, tm, d), lambda i, j: (0, j, 0))
    else:
        mod_spec = pl.BlockSpec((1, N_ADA, d), lambda i, j: (i, 0, 0))
    widths = (D_SSD, CONV_DIM, D_S5, DT_PAD)
    return pl.pallas_call(
        functools.partial(_inproj_kernel, per_row=per_row),
        grid=(nb, nt),
        in_specs=[pl.BlockSpec((1, tm, d), lambda i, j: (i, j, 0)),
                  mod_spec,
                  _const_spec((1, d)),
                  _const_spec((d, IN_PROJ_PACKED))],
        out_specs=[pl.BlockSpec((1, tm, wd), lambda i, j: (i, j, 0)) for wd in widths],
        out_shape=[jax.ShapeDtypeStruct((nb, rows, wd), f32) for wd in widths],
        compiler_params=_params("parallel", "parallel"),
        name="in_proj",
    )(x, mod, g, w)


def _gated_group_norm(y, z, ng):
    y = y * _silu(z)
    gw = D_SSD // GROUPS
    parts = []
    for g in range(GROUPS):
        yg = y[:, g * gw:(g + 1) * gw]
        parts.append(yg * lax.rsqrt(jnp.mean(yg * yg, axis=-1, keepdims=True) + EPS))
    return jnp.concatenate(parts, axis=-1) * ng


def _ssd_prompt_kernel(xbc_ref, z_ref, dt_ref, cw_ref, cb_ref, dtb_ref, alog_ref, dexp_ref, ng_ref,
                       y_ref, st_ref, cn_ref, ext_ref, h_ref):
    c = pl.program_id(1)
    T = SSD_CHUNK

    @pl.when(c == 0)
    def _():
        ext_ref[0:SUBLANES, :] = jnp.zeros((SUBLANES, CONV_DIM), f32)
        h_ref[...] = jnp.zeros_like(h_ref)

    ext_ref[SUBLANES:SUBLANES + T, :] = xbc_ref[0]
    conv = cb_ref[...]
    for k in range(CONV_W):
        o = SUBLANES - (CONV_W - 1) + k
        conv = conv + ext_ref[o:o + T, :] * cw_ref[k:k + 1, :]
    cn_ref[0] = ext_ref[SUBLANES + T - (CONV_W - 1):SUBLANES + T, :]
    ext_ref[0:SUBLANES, :] = ext_ref[T:T + SUBLANES, :]

    act = _silu(conv)
    xs = act[:, :D_SSD]
    bm = act[:, D_SSD:D_SSD + GROUPS * STATE]
    cm = act[:, D_SSD + GROUPS * STATE:]
    dtv = jax.nn.softplus(dt_ref[0] + dtb_ref[...])
    da = dtv * (-jnp.exp(alog_ref[...]))
    row = lax.broadcasted_iota(jnp.int32, (T, T), 0)
    col = lax.broadcasted_iota(jnp.int32, (T, T), 1)
    tri = row >= col
    a_cs = jnp.dot(tri.astype(f32), da, precision=lax.Precision.HIGHEST,
                   preferred_element_type=f32)
    a_cs_t = a_cs.T
    dx = xs * dexp_ref[...]

    ys = []
    for g in range(GROUPS):
        bg = bm[:, g * STATE:(g + 1) * STATE]
        cg16 = cm[:, g * STATE:(g + 1) * STATE].astype(bf16)
        gmat = lax.dot_general(cg16, bg.astype(bf16), NT_DIMS, preferred_element_type=f32)
        for hh in range(HEADS_PER_GROUP):
            h = g * HEADS_PER_GROUP + hh
            acs = a_cs[:, h:h + 1]
            seg = acs - a_cs_t[h:h + 1, :]
            lmat = jnp.exp(jnp.where(tri, seg, -jnp.inf))
            s16 = (gmat * lmat).astype(bf16)
            xh = xs[:, h * HEAD_DIM:(h + 1) * HEAD_DIM]
            x16 = (xh * dtv[:, h:h + 1]).astype(bf16)
            hp = h_ref[h]
            y = jnp.dot(s16, x16, preferred_element_type=f32)
            y_off = lax.dot_general(cg16, hp.astype(bf16), NT_DIMS, preferred_element_type=f32)
            y = y + y_off * jnp.exp(acs)
            a_last = a_cs[T - 1:T, h:h + 1]
            bd16 = (bg * jnp.exp(a_last - acs)).astype(bf16)
            upd = lax.dot_general(x16, bd16, TN_DIMS, preferred_element_type=f32)
            h_ref[h] = jnp.exp(a_last) * hp + upd
            ys.append(y)
    y = jnp.concatenate(ys, axis=-1) + dx
    y_ref[0] = _gated_group_norm(y, z_ref[0], ng_ref[...])

    @pl.when(c == pl.num_programs(1) - 1)
    def _():
        st_ref[0] = h_ref[...]


def _ssd_prompt(xbc, z, dt, cw, cb, dtb, alog, dexp, ng):
    nb, L, _ = xbc.shape
    T = SSD_CHUNK
    nc = L // T
    blk = lambda wd: pl.BlockSpec((1, T, wd), lambda b, c: (b, c, 0))
    return pl.pallas_call(
        _ssd_prompt_kernel,
        grid=(nb, nc),
        in_specs=[blk(CONV_DIM), blk(D_SSD), blk(DT_PAD),
                  _const_spec((CONV_W, CONV_DIM)), _const_spec((1, CONV_DIM)),
                  _const_spec((1, DT_PAD)), _const_spec((1, DT_PAD)),
                  _const_spec((1, D_SSD)), _const_spec((1, D_SSD))],
        out_specs=[blk(D_SSD),
                   pl.BlockSpec((1, HEADS, HEAD_DIM, STATE), lambda b, c: (b, 0, 0, 0)),
                   pl.BlockSpec((1, CONV_W - 1, CONV_DIM), lambda b, c: (b, 0, 0))],
        out_shape=[jax.ShapeDtypeStruct((nb, L, D_SSD), f32),
                   jax.ShapeDtypeStruct((nb, HEADS, HEAD_DIM, STATE), f32),
                   jax.ShapeDtypeStruct((nb, CONV_W - 1, CONV_DIM), f32)],
        scratch_shapes=[pltpu.VMEM((SUBLANES + T, CONV_DIM), f32),
                        pltpu.VMEM((HEADS, HEAD_DIM, STATE), f32)],
        compiler_params=_params("parallel", "arbitrary"),
        name="ssd_prompt",
    )(xbc, z, dt, cw, cb, dtb, alog, dexp, ng)


def _ssd_sample_kernel(xbc_ref, z_ref, dt_ref, cbuf_ref, st_ref, cw_ref, cb_ref, dtb_ref, alog_ref,
                       dexp_ref, ng_ref, y_ref, stn_ref, cn_ref, dtot_ref, *, L, nb):
    full = [cbuf_ref[j] for j in range(CONV_W - 1)] + [xbc_ref[t] for t in range(L)]
    for j in range(CONV_W - 1):
        cn_ref[j] = full[L + j]
    a_neg = -jnp.exp(alog_ref[...])
    xs, bm, cm, dtv, acs = [], [], [], [], []
    run = None
    for t in range(L):
        conv = cb_ref[...]
        for k in range(CONV_W):
            conv = conv + full[t + k] * cw_ref[k:k + 1, :]
        act = _silu(conv)
        xs.append(act[:, :D_SSD])
        bm.append(act[:, D_SSD:D_SSD + GROUPS * STATE])
        cm.append(act[:, D_SSD + GROUPS * STATE:])
        d = jax.nn.softplus(dt_ref[t] + dtb_ref[...])
        dtv.append(d)
        run = d * a_neg if run is None else run + d * a_neg
        acs.append(run)
    a_tot = acs[L - 1]
    dtot_ref[...] = jnp.exp(a_tot)

    y_heads = [[None] * HEADS for _ in range(L)]
    xd_stack = []
    for h in range(HEADS):
        g = h // HEADS_PER_GROUP
        sl = slice(h * HEAD_DIM, (h + 1) * HEAD_DIM)
        xh = [xs[t][:, sl] * dtv[t][:, h:h + 1] for t in range(L)]
        for t in range(L):
            acc = None
            for s in range(t + 1):
                cb_dot = jnp.sum(cm[t][:, g * STATE:(g + 1) * STATE] * bm[s][:, g * STATE:(g + 1) * STATE],
                                 axis=-1, keepdims=True)
                w = cb_dot * jnp.exp(acs[t][:, h:h + 1] - acs[s][:, h:h + 1])
                acc = w * xh[s] if acc is None else acc + w * xh[s]
            y_heads[t][h] = acc
        xd_stack.append(jnp.concatenate(
            [xh[t] * jnp.exp(a_tot[:, h:h + 1] - acs[t][:, h:h + 1]) for t in range(L)], axis=0))
    c_stack = [jnp.concatenate([cm[t][:, g * STATE:(g + 1) * STATE] for t in range(L)], axis=0).astype(bf16)
               for g in range(GROUPS)]
    b_stack = [jnp.concatenate([bm[t][:, g * STATE:(g + 1) * STATE] for t in range(L)], axis=0).astype(bf16)
               for g in range(GROUPS)]
    seq_of_row = lax.broadcasted_iota(jnp.int32, (L * nb, HEAD_DIM), 0) & (nb - 1)

    def per_seq(b, yoff):
        mine = seq_of_row == b
        drow = dtot_ref[pl.ds(b, 1), :]
        out = []
        for h in range(HEADS):
            g = h // HEADS_PER_GROUP
            h0 = st_ref[b, h]
            r = lax.dot_general(c_stack[g], h0.astype(bf16), NT_DIMS, preferred_element_type=f32)
            out.append(yoff[h] + jnp.where(mine, r, 0.0))
            xm = jnp.where(mine, xd_stack[h], 0.0).astype(bf16)
            upd = lax.dot_general(xm, b_stack[g], TN_DIMS, preferred_element_type=f32)
            stn_ref[b, h] = drow[:, h:h + 1] * h0 + upd
        return tuple(out)

    yoff = lax.fori_loop(0, nb, per_seq,
                         tuple(jnp.zeros((L * nb, HEAD_DIM), f32) for _ in range(HEADS)))

    for t in range(L):
        parts = [y_heads[t][h] + yoff[h][t * nb:(t + 1) * nb] * jnp.exp(acs[t][:, h:h + 1])
                 for h in range(HEADS)]
        y = jnp.concatenate(parts, axis=-1) + xs[t] * dexp_ref[...]
        y_ref[t] = _gated_group_norm(y, z_ref[t], ng_ref[...])


def _ssd_sample(xbc, z, dt, cbuf, st, cw, cb, dtb, alog, dexp, ng, *, nb=8):
    L, B, _ = xbc.shape
    assert nb & (nb - 1) == 0 and B % nb == 0
    tblk = lambda n, wd: pl.BlockSpec((n, nb, wd), lambda i: (0, i, 0))
    st_spec = pl.BlockSpec((nb, HEADS, HEAD_DIM, STATE), lambda i: (i, 0, 0, 0))
    return pl.pallas_call(
        functools.partial(_ssd_sample_kernel, L=L, nb=nb),
        grid=(B // nb,),
        in_specs=[tblk(L, CONV_DIM), tblk(L, D_SSD), tblk(L, DT_PAD), tblk(CONV_W - 1, CONV_DIM), st_spec,
                  _const_spec((CONV_W, CONV_DIM)), _const_spec((1, CONV_DIM)),
                  _const_spec((1, DT_PAD)), _const_spec((1, DT_PAD)),
                  _const_spec((1, D_SSD)), _const_spec((1, D_SSD))],
        out_specs=[tblk(L, D_SSD), st_spec, tblk(CONV_W - 1, CONV_DIM)],
        out_shape=[jax.ShapeDtypeStruct((L, B, D_SSD), f32),
                   jax.ShapeDtypeStruct((B, HEADS, HEAD_DIM, STATE), f32),
                   jax.ShapeDtypeStruct((CONV_W - 1, B, CONV_DIM), f32)],
        scratch_shapes=[pltpu.VMEM((nb, DT_PAD), f32)],
        compiler_params=_params("parallel"),
        name="ssd_sample",
    )(xbc, z, dt, cbuf, st, cw, cb, dtb, alog, dexp, ng)


def _s5_param_kernel(lr_ref, li_ref, ls_ref, br_ref, bi_ref, abr_ref, abi_ref, bbr_ref, bbi_ref):
    lr, li = lr_ref[...], li_ref[...]
    step = jnp.exp(ls_ref[...])
    mag = jnp.exp(lr * step)
    abr = mag * jnp.cos(li * step)
    abi = mag * jnp.sin(li * step)
    nr, ni = abr - 1.0, abi
    den = lr * lr + li * li
    fr = (nr * lr + ni * li) / den
    fi = (ni * lr - nr * li) / den
    abr_ref[...] = abr
    abi_ref[...] = abi
    br, bi = br_ref[...], bi_ref[...]
    bbr_ref[...] = fr * br - fi * bi
    bbi_ref[...] = fr * bi + fi * br


def _s5_params(a_re, a_im, log_step, b_re_t, b_im_t):
    g, p = a_re.shape
    vec = jax.ShapeDtypeStruct((g, 1, p), f32)
    mat = jax.ShapeDtypeStruct(b_re_t.shape, f32)
    return pl.pallas_call(_s5_param_kernel, out_shape=[vec, vec, mat, mat], name="s5_params")(
        a_re.reshape(g, 1, p), a_im.reshape(g, 1, p), log_step.reshape(g, 1, 1), b_re_t, b_im_t)


def _gelu(x):
    return 0.5 * x * (1.0 + lax.erf(x * (2.0 ** -0.5)))


def _s5_kernel(u_ref, re0_ref, im0_ref, ar_ref, ai_ref, bblk_ref, cblk_ref, dsk_ref, wglu_ref, bglu_ref,
               y_ref, ren_ref, imn_ref, ut_ref, hbuf_ref, hst_ref, g_ref, *, nb, tl, batch_major):
    step = pl.program_id(0)
    rows = nb * tl
    ns = S5_SLAB_STATE

    @pl.when(step == 0)
    def _():
        for s in range(S5_SLABS):
            hst_ref[s, :, :ns] = re0_ref[:, s * ns:(s + 1) * ns]
            hst_ref[s, :, ns:] = im0_ref[:, s * ns:(s + 1) * ns]

    if batch_major:
        ut_ref[...] = jnp.swapaxes(u_ref[...], 0, 1).reshape(rows, D_S5)
    else:
        ut_ref[...] = u_ref[...].reshape(rows, D_S5)

    for s in range(S5_SLABS):
        us = ut_ref[:, s * LANES:(s + 1) * LANES]
        hbuf_ref[0:nb, :] = hst_ref[s]
        hbuf_ref[nb:nb + rows, :] = jnp.dot(us.astype(bf16), bblk_ref[s], preferred_element_type=f32)
        ar = jnp.broadcast_to(ar_ref[s], (SUBLANES, ns))
        ai = jnp.broadcast_to(ai_ref[s], (SUBLANES, ns))

        def scan_step(i, carry):
            j = pl.multiple_of(nb + i * SUBLANES, SUBLANES)
            prev = hbuf_ref[pl.ds(j - nb, SUBLANES), :]
            cur = hbuf_ref[pl.ds(j, SUBLANES), :]
            pr, pi = prev[:, :ns], prev[:, ns:]
            hbuf_ref[pl.ds(j, SUBLANES), :ns] = ar * pr - ai * pi + cur[:, :ns]
            hbuf_ref[pl.ds(j, SUBLANES), ns:] = ar * pi + ai * pr + cur[:, ns:]
            return carry

        lax.fori_loop(0, rows // SUBLANES, scan_step, 0)
        hst_ref[s] = hbuf_ref[rows:rows + nb, :]
        ys = jnp.dot(hbuf_ref[nb:nb + rows, :].astype(bf16), cblk_ref[s], preferred_element_type=f32)
        ys = ys + dsk_ref[:, s * LANES:(s + 1) * LANES] * us
        g_ref[:, s * LANES:(s + 1) * LANES] = _gelu(ys)

    g = g_ref[...]
    gate = jnp.dot(g.astype(bf16), wglu_ref[...], preferred_element_type=f32) + bglu_ref[...]
    out = g * jax.nn.sigmoid(gate)
    if batch_major:
        y_ref[...] = jnp.swapaxes(out.reshape(tl, nb, D_S5), 0, 1)
    else:
        y_ref[...] = out.reshape(tl, nb, D_S5)

    @pl.when(step == pl.num_programs(0) - 1)
    def _():
        for s in range(S5_SLABS):
            ren_ref[:, s * ns:(s + 1) * ns] = hst_ref[s, :, :ns]
            imn_ref[:, s * ns:(s + 1) * ns] = hst_ref[s, :, ns:]


def _s5(u, re0, im0, ar, ai, bblk, cblk, dsk, wglu, bglu, *, tl, batch_major):
    if batch_major:
        nb, L, _ = u.shape
        u_spec = pl.BlockSpec((nb, tl, D_S5), lambda i: (0, i, 0))
    else:
        L, nb, _ = u.shape
        u_spec = pl.BlockSpec((tl, nb, D_S5), lambda i: (i, 0, 0))
    rows = nb * tl
    nstate = S5_GROUPS * S5_STATE
    st_spec = _const_spec((nb, nstate))
    return pl.pallas_call(
        functools.partial(_s5_kernel, nb=nb, tl=tl, batch_major=batch_major),
        grid=(L // tl,),
        in_specs=[u_spec, st_spec, st_spec,
                  _const_spec((S5_SLABS, 1, S5_SLAB_STATE)), _const_spec((S5_SLABS, 1, S5_SLAB_STATE)),
                  _const_spec((S5_SLABS, LANES, 2 * S5_SLAB_STATE)),
                  _const_spec((S5_SLABS, 2 * S5_SLAB_STATE, LANES)),
                  _const_spec((1, D_S5)), _const_spec((D_S5, D_S5)), _const_spec((1, D_S5))],
        out_specs=[u_spec, st_spec, st_spec],
        out_shape=[jax.ShapeDtypeStruct(u.shape, f32),
                   jax.ShapeDtypeStruct((nb, nstate), f32),
                   jax.ShapeDtypeStruct((nb, nstate), f32)],
        scratch_shapes=[pltpu.VMEM((rows, D_S5), f32),
                        pltpu.VMEM((nb + rows, 2 * S5_SLAB_STATE), f32),
                        pltpu.VMEM((S5_SLABS, nb, 2 * S5_SLAB_STATE), f32),
                        pltpu.VMEM((rows, D_S5), f32)],
        compiler_params=_params("arbitrary"),
        name="s5_mixer",
    )(u, re0, im0, ar, ai, bblk, cblk, dsk, wglu, bglu)


def _ffn_kernel(x_ref, ys_ref, y5_ref, mod_ref, modf_ref, n2g_ref, nfg_ref, wo_ref, wg_ref, wu_ref, wd_ref,
                o_ref, *, per_row, ff_chunk):
    g1 = _mod_row(mod_ref, 2, per_row)
    sh2 = _mod_row(mod_ref, 3, per_row)
    sc2 = _mod_row(mod_ref, 4, per_row)
    g2 = _mod_row(mod_ref, 5, per_row)
    shf = _mod_row(modf_ref, 0, per_row)
    scf = _mod_row(modf_ref, 1, per_row)
    att = jnp.dot(ys_ref[0].astype(bf16), wo_ref[:D_SSD, :], preferred_element_type=f32)
    att = att + jnp.dot(y5_ref[0].astype(bf16), wo_ref[D_SSD:, :], preferred_element_type=f32)
    x1 = x_ref[0] + g1 * att
    v = _rms_mod(x1, n2g_ref[...], sc2, sh2).astype(bf16)
    ff = None
    for o in range(0, D_FF, ff_chunk):
        gate = jnp.dot(v, wg_ref[:, o:o + ff_chunk], preferred_element_type=f32)
        up = jnp.dot(v, wu_ref[:, o:o + ff_chunk], preferred_element_type=f32)
        hid = (_silu(gate) * up).astype(bf16)
        part = jnp.dot(hid, wd_ref[o:o + ff_chunk, :], preferred_element_type=f32)
        ff = part if ff is None else ff + part
    x2 = x1 + g2 * ff
    o_ref[0] = _rms_mod(x2, nfg_ref[...], scf, shf)


def _ffn(x, ys, y5, mod, modf, n2g, nfg, wo, wg, wu, wd, *, tm, per_row, ff_chunk=1408):
    nb, rows, d = x.shape
    nt = rows // tm
    if per_row:
        mod_spec = pl.BlockSpec((N_ADA, tm, d), lambda i, j: (0, j, 0))
        modf_spec = pl.BlockSpec((2, tm, d), lambda i, j: (0, j, 0))
    else:
        mod_spec = pl.BlockSpec((1, N_ADA, d), lambda i, j: (i, 0, 0))
        modf_spec = pl.BlockSpec((1, 2, d), lambda i, j: (i, 0, 0))
    blk = lambda wd_: pl.BlockSpec((1, tm, wd_), lambda i, j: (i, j, 0))
    single = dict(pipeline_mode=pl.Buffered(1))
    wspec = lambda shape: pl.BlockSpec(shape, lambda i, j: (0, 0), **single)
    return pl.pallas_call(
        functools.partial(_ffn_kernel, per_row=per_row, ff_chunk=ff_chunk),
        grid=(nb, nt),
        in_specs=[blk(d), blk(D_SSD), blk(D_S5), mod_spec, modf_spec,
                  _const_spec((1, d)), _const_spec((1, d)),
                  wspec((d, d)), wspec((d, D_FF)), wspec((d, D_FF)), wspec((D_FF, d))],
        out_specs=blk(d),
        out_shape=jax.ShapeDtypeStruct((nb, rows, d), f32),
        compiler_params=_params("parallel", "parallel"),
        name="out_ffn",
    )(x, ys, y5, mod, modf, n2g, nfg, wo, wg, wu, wd)


def _block_diag(m):
    s, k, a, b = m.shape
    eye = jnp.eye(k, dtype=m.dtype)
    return (m[:, :, :, None, :] * eye[None, :, None, :, None]).reshape(s, k * a, k * b)


def kernel(x_prompt, x_sample, c_prompt, c_sample, state_ssd, state_conv, state_s5_re, state_s5_im, w_ada, b_ada, norm1_g, w_in, conv_w, conv_b, ssd_dt_bias, ssd_A_log, ssd_D, ssd_norm_g, s5_A_re, s5_A_im, s5_log_step, s5_B_re, s5_B_im, s5_C_re, s5_C_im, s5_D, w_glu, b_glu, w_out, norm2_g, w_ffn_gate, w_ffn_up, w_ffn_down, w_ada_f, b_ada_f, normf_g):
    assert w_ada.shape[0] == 1, "single-layer stack"
    bp, seq, d = x_prompt.shape
    bs, dseq, _ = x_sample.shape

    c_all = jnp.concatenate([c_prompt, c_sample], axis=0)
    mod = _ada_mod(c_all, w_ada[0], b_ada[0])
    modf = _ada_mod(c_all, w_ada_f, b_ada_f)
    mod_p = mod[:bp].reshape(bp, N_ADA, d)
    modf_p = modf[:bp].reshape(bp, 2, d)
    mod_s = mod[bp:].reshape(bs, N_ADA, d).transpose(1, 0, 2)
    modf_s = modf[bp:].reshape(bs, 2, d).transpose(1, 0, 2)

    o1, o2, o3 = D_SSD, D_SSD + CONV_DIM, D_SSD + CONV_DIM + HEADS
    wi = w_in[0]
    w_in_p = jnp.concatenate(
        [wi[:, :o2], wi[:, o3:], wi[:, o2:o3], jnp.zeros((d, DT_PAD - HEADS), f32)], axis=1).astype(bf16)
    pad_h = lambda v: jnp.concatenate([v, jnp.zeros((DT_PAD - HEADS,), f32)]).reshape(1, DT_PAD)
    dtb = pad_h(ssd_dt_bias[0])
    alog = pad_h(ssd_A_log[0])
    dexp = jnp.repeat(ssd_D[0], HEAD_DIM).reshape(1, D_SSD)
    ng = ssd_norm_g[0].reshape(1, D_SSD)
    cw, cb = conv_w[0], conv_b[0].reshape(1, CONV_DIM)

    abr, abi, bbr, bbi = _s5_params(s5_A_re[0], s5_A_im[0], s5_log_step[0],
                                    s5_B_re[0].transpose(0, 2, 1), s5_B_im[0].transpose(0, 2, 1))
    gps = S5_GROUPS // S5_SLABS
    ar = abr.reshape(S5_SLABS, 1, S5_SLAB_STATE)
    ai = abi.reshape(S5_SLABS, 1, S5_SLAB_STATE)
    slab = lambda m: m.reshape((S5_SLABS, gps) + m.shape[1:])
    bblk = jnp.concatenate([_block_diag(slab(bbr)), _block_diag(slab(bbi))], axis=2).astype(bf16)
    cblk = jnp.concatenate([_block_diag(slab(s5_C_re[0].transpose(0, 2, 1))),
                            _block_diag(slab(-s5_C_im[0].transpose(0, 2, 1)))], axis=1).astype(bf16)
    dsk = s5_D[0].reshape(1, D_S5)
    wglu = w_glu[0].astype(bf16)
    bglu = b_glu[0].reshape(1, D_S5)
    wo = w_out[0].astype(bf16)
    wg, wu, wd = w_ffn_gate[0].astype(bf16), w_ffn_up[0].astype(bf16), w_ffn_down[0].astype(bf16)
    n1g, n2g, nfg = norm1_g[0].reshape(1, d), norm2_g[0].reshape(1, d), normf_g.reshape(1, d)
    nstate = S5_GROUPS * S5_STATE

    z, xbc, u5, dt = _inproj(x_prompt, mod_p, n1g, w_in_p, tm=512, per_row=False)
    y_ssd, ssd_p, conv_p = _ssd_prompt(xbc, z, dt, cw, cb, dtb, alog, dexp, ng)
    zeros_st = jnp.zeros((bp, nstate), f32)
    y_s5, re_p, im_p = _s5(u5, zeros_st, zeros_st, ar, ai, bblk, cblk, dsk, wglu, bglu,
                           tl=128, batch_major=True)
    y_prompt = _ffn(x_prompt, y_ssd, y_s5, mod_p, modf_p, n2g, nfg, wo, wg, wu, wd, tm=512, per_row=False)

    xs_t = x_sample.transpose(1, 0, 2)
    z, xbc, u5, dt = _inproj(xs_t, mod_s, n1g, w_in_p, tm=bs, per_row=True)
    y_ssd, ssd_s, conv_s = _ssd_sample(xbc, z, dt, state_conv[0].transpose(1, 0, 2), state_ssd[0],
                                       cw, cb, dtb, alog, dexp, ng)
    y_s5, re_s, im_s = _s5(u5, state_s5_re[0].reshape(bs, nstate), state_s5_im[0].reshape(bs, nstate),
                           ar, ai, bblk, cblk, dsk, wglu, bglu, tl=dseq, batch_major=False)
    y_sample = _ffn(xs_t, y_ssd, y_s5, mod_s, modf_s, n2g, nfg, wo, wg, wu, wd, tm=bs, per_row=True)
    y_sample = y_sample.transpose(1, 0, 2)

    g5 = (S5_GROUPS, S5_STATE)
    return (y_prompt, y_sample,
            ssd_p[None], ssd_s[None],
            conv_p[None], conv_s.transpose(1, 0, 2)[None],
            re_p.reshape((1, bp) + g5), re_s.reshape((1, bs) + g5),
            im_p.reshape((1, bp) + g5), im_s.reshape((1, bs) + g5))
```

```python
import functools

import jax
import jax.numpy as jnp
from jax import lax
from jax.experimental import pallas as pl
from jax.experimental.pallas import tpu as pltpu

f32 = jnp.float32
bf16 = jnp.bfloat16

D_MODEL = 1024
D_SSD = 512
HEAD_DIM = 64
HEADS = 8
GROUPS = 2
HEADS_PER_GROUP = HEADS // GROUPS
STATE = 128
CONV_W = 4
CONV_DIM = D_SSD + 2 * GROUPS * STATE
D_S5 = 512
S5_CH = 16
S5_GROUPS = 32
S5_STATE = 64
D_FF = 2816
N_ADA = 6
EPS = 1e-6

LANES = 128
SUBLANES = 8
SSD_CHUNK = 128
S5_SLABS = D_S5 // LANES
S5_SLAB_STATE = (S5_GROUPS // S5_SLABS) * S5_STATE
DT_PAD = LANES
IN_PROJ_PACKED = D_SSD + CONV_DIM + D_S5 + DT_PAD
VMEM_LIMIT = 56 * 1024 * 1024

NT_DIMS = (((1,), (1,)), ((), ()))
TN_DIMS = (((0,), (0,)), ((), ()))


def _silu(x):
    return x * jax.nn.sigmoid(x)


def _rms_mod(x, g, sc, sh):
    y = x * lax.rsqrt(jnp.mean(x * x, axis=-1, keepdims=True) + EPS)
    return (y * g) * (1.0 + sc) + sh


def _mod_row(mod_ref, i, per_row):
    return mod_ref[i] if per_row else mod_ref[0, i:i + 1, :]


def _const_spec(shape):
    nd = len(shape)
    return pl.BlockSpec(shape, lambda *_: (0,) * nd)


def _params(*sem):
    return pltpu.CompilerParams(dimension_semantics=sem, vmem_limit_bytes=VMEM_LIMIT)


def _ada_kernel(c_ref, w_ref, b_ref, o_ref):
    s = _silu(c_ref[...]).astype(bf16)
    o_ref[...] = jnp.dot(s, w_ref[...].astype(bf16), preferred_element_type=f32) + b_ref[...]


def _ada_mod(c, w, b, tn=512):
    m, k = c.shape
    n = w.shape[1]
    return pl.pallas_call(
        _ada_kernel,
        grid=(n // tn,),
        in_specs=[pl.BlockSpec((m, k), lambda j: (0, 0)),
                  pl.BlockSpec((k, tn), lambda j: (0, j)),
                  pl.BlockSpec((1, tn), lambda j: (0, j))],
        out_specs=pl.BlockSpec((m, tn), lambda j: (0, j)),
        out_shape=jax.ShapeDtypeStruct((m, n), f32),
        compiler_params=_params("parallel"),
        name="ada_mod",
    )(c, w, b.reshape(1, n))


def _inproj_kernel(x_ref, mod_ref, g_ref, w_ref, z_ref, xbc_ref, u5_ref, dt_ref, *, per_row):
    sh = _mod_row(mod_ref, 0, per_row)
    sc = _mod_row(mod_ref, 1, per_row)
    u = _rms_mod(x_ref[0], g_ref[...], sc, sh).astype(bf16)
    o = 0
    for ref, width in ((z_ref, D_SSD), (xbc_ref, CONV_DIM), (u5_ref, D_S5), (dt_ref, DT_PAD)):
        ref[0] = jnp.dot(u, w_ref[:, o:o + width], preferred_element_type=f32)
        o += width


def _inproj(x, mod, g, w, *, tm, per_row):
    nb, rows, d = x.shape
    nt = rows // tm
    if per_row:
        mod_spec = pl.BlockSpec((N_ADA, tm, d), lambda i, j: (0, j, 0))
    else:
        mod_spec = pl.BlockSpec((1, N_ADA, d), lambda i, j: (i, 0, 0))
    widths = (D_SSD, CONV_DIM, D_S5, DT_PAD)
    return pl.pallas_call(
        functools.partial(_inproj_kernel, per_row=per_row),
        grid=(nb, nt),
        in_specs=[pl.BlockSpec((1, tm, d), lambda i, j: (i, j, 0)),
                  mod_spec,
                  _const_spec((1, d)),
                  _const_spec((d, IN_PROJ_PACKED))],
        out_specs=[pl.BlockSpec((1, tm, wd), lambda i, j: (i, j, 0)) for wd in widths],
        out_shape=[jax.ShapeDtypeStruct((nb, rows, wd), f32) for wd in widths],
        compiler_params=_params("parallel", "parallel"),
        name="in_proj",
    )(x, mod, g, w)


def _gated_group_norm(y, z, ng):
    y = y * _silu(z)
    gw = D_SSD // GROUPS
    parts = []
    for g in range(GROUPS):
        yg = y[:, g * gw:(g + 1) * gw]
        parts.append(yg * lax.rsqrt(jnp.mean(yg * yg, axis=-1, keepdims=True) + EPS))
    return jnp.concatenate(parts, axis=-1) * ng


def _ssd_prompt_kernel(xbc_ref, z_ref, dt_ref, cw_ref, cb_ref, dtb_ref, alog_ref, dexp_ref, ng_ref,
                       y_ref, st_ref, cn_ref, ext_ref, h_ref):
    c = pl.program_id(1)
    T = SSD_CHUNK

    @pl.when(c == 0)
    def _():
        ext_ref[0:SUBLANES, :] = jnp.zeros((SUBLANES, CONV_DIM), f32)
        h_ref[...] = jnp.zeros_like(h_ref)

    ext_ref[SUBLANES:SUBLANES + T, :] = xbc_ref[0]
    conv = cb_ref[...]
    for k in range(CONV_W):
        o = SUBLANES - (CONV_W - 1) + k
        conv = conv + ext_ref[o:o + T, :] * cw_ref[k:k + 1, :]
    cn_ref[0] = ext_ref[SUBLANES + T - (CONV_W - 1):SUBLANES + T, :]
    ext_ref[0:SUBLANES, :] = ext_ref[T:T + SUBLANES, :]

    act = _silu(conv)
    xs = act[:, :D_SSD]
    bm = act[:, D_SSD:D_SSD + GROUPS * STATE]
    cm = act[:, D_SSD + GROUPS * STATE:]
    dtv = jax.nn.softplus(dt_ref[0] + dtb_ref[...])
    da = dtv * (-jnp.exp(alog_ref[...]))
    row = lax.broadcasted_iota(jnp.int32, (T, T), 0)
    col = lax.broadcasted_iota(jnp.int32, (T, T), 1)
    tri = row >= col
    a_cs = jnp.dot(tri.astype(f32), da, precision=lax.Precision.HIGHEST,
                   preferred_element_type=f32)
    a_cs_t = a_cs.T
    dx = xs * dexp_ref[...]

    ys = []
    for g in range(GROUPS):
        bg = bm[:, g * STATE:(g + 1) * STATE]
        cg16 = cm[:, g * STATE:(g + 1) * STATE].astype(bf16)
        gmat = lax.dot_general(cg16, bg.astype(bf16), NT_DIMS, preferred_element_type=f32)
        for hh in range(HEADS_PER_GROUP):
            h = g * HEADS_PER_GROUP + hh
            acs = a_cs[:, h:h + 1]
            seg = acs - a_cs_t[h:h + 1, :]
            lmat = jnp.exp(jnp.where(tri, seg, -jnp.inf))
            s16 = (gmat * lmat).astype(bf16)
            xh = xs[:, h * HEAD_DIM:(h + 1) * HEAD_DIM]
            x16 = (xh * dtv[:, h:h + 1]).astype(bf16)
            hp = h_ref[h]
            y = jnp.dot(s16, x16, preferred_element_type=f32)
            y_off = lax.dot_general(cg16, hp.astype(bf16), NT_DIMS, preferred_element_type=f32)
            y = y + y_off * jnp.exp(acs)
            a_last = a_cs[T - 1:T, h:h + 1]
            bd16 = (bg * jnp.exp(a_last - acs)).astype(bf16)
            upd = lax.dot_general(x16, bd16, TN_DIMS, preferred_element_type=f32)
            h_ref[h] = jnp.exp(a_last) * hp + upd
            ys.append(y)
    y = jnp.concatenate(ys, axis=-1) + dx
    y_ref[0] = _gated_group_norm(y, z_ref[0], ng_ref[...])

    @pl.when(c == pl.num_programs(1) - 1)
    def _():
        st_ref[0] = h_ref[...]


def _ssd_prompt(xbc, z, dt, cw, cb, dtb, alog, dexp, ng):
    nb, L, _ = xbc.shape
    T = SSD_CHUNK
    nc = L // T
    blk = lambda wd: pl.BlockSpec((1, T, wd), lambda b, c: (b, c, 0))
    return pl.pallas_call(
        _ssd_prompt_kernel,
        grid=(nb, nc),
        in_specs=[blk(CONV_DIM), blk(D_SSD), blk(DT_PAD),
                  _const_spec((CONV_W, CONV_DIM)), _const_spec((1, CONV_DIM)),
                  _const_spec((1, DT_PAD)), _const_spec((1, DT_PAD)),
                  _const_spec((1, D_SSD)), _const_spec((1, D_SSD))],
        out_specs=[blk(D_SSD),
                   pl.BlockSpec((1, HEADS, HEAD_DIM, STATE), lambda b, c: (b, 0, 0, 0)),
                   pl.BlockSpec((1, CONV_W - 1, CONV_DIM), lambda b, c: (b, 0, 0))],
        out_shape=[jax.ShapeDtypeStruct((nb, L, D_SSD), f32),
                   jax.ShapeDtypeStruct((nb, HEADS, HEAD_DIM, STATE), f32),
                   jax.ShapeDtypeStruct((nb, CONV_W - 1, CONV_DIM), f32)],
        scratch_shapes=[pltpu.VMEM((SUBLANES + T, CONV_DIM), f32),
                        pltpu.VMEM((HEADS, HEAD_DIM, STATE), f32)],
        compiler_params=_params("parallel", "arbitrary"),
        name="ssd_prompt",
    )(xbc, z, dt, cw, cb, dtb, alog, dexp, ng)


def _ssd_sample_kernel(xbc_ref, z_ref, dt_ref, cbuf_ref, st_ref, cw_ref, cb_ref, dtb_ref, alog_ref,
                       dexp_ref, ng_ref, y_ref, stn_ref, cn_ref, dtot_ref, *, L, nb):
    full = [cbuf_ref[j] for j in range(CONV_W - 1)] + [xbc_ref[t] for t in range(L)]
    for j in range(CONV_W - 1):
        cn_ref[j] = full[L + j]
    a_neg = -jnp.exp(alog_ref[...])
    xs, bm, cm, dtv, acs = [], [], [], [], []
    run = None
    for t in range(L):
        conv = cb_ref[...]
        for k in range(CONV_W):
            conv = conv + full[t + k] * cw_ref[k:k + 1, :]
        act = _silu(conv)
        xs.append(act[:, :D_SSD])
        bm.append(act[:, D_SSD:D_SSD + GROUPS * STATE])
        cm.append(act[:, D_SSD + GROUPS * STATE:])
        d = jax.nn.softplus(dt_ref[t] + dtb_ref[...])
        dtv.append(d)
        run = d * a_neg if run is None else run + d * a_neg
        acs.append(run)
    a_tot = acs[L - 1]
    dtot_ref[...] = jnp.exp(a_tot)

    y_heads = [[None] * HEADS for _ in range(L)]
    xd_stack = []
    for h in range(HEADS):
        g = h // HEADS_PER_GROUP
        sl = slice(h * HEAD_DIM, (h + 1) * HEAD_DIM)
        xh = [xs[t][:, sl] * dtv[t][:, h:h + 1] for t in range(L)]
        for t in range(L):
            acc = None
            for s in range(t + 1):
                cb_dot = jnp.sum(cm[t][:, g * STATE:(g + 1) * STATE] * bm[s][:, g * STATE:(g + 1) * STATE],
                                 axis=-1, keepdims=True)
                w = cb_dot * jnp.exp(acs[t][:, h:h + 1] - acs[s][:, h:h + 1])
                acc = w * xh[s] if acc is None else acc + w * xh[s]
            y_heads[t][h] = acc
        xd_stack.append(jnp.concatenate(
            [xh[t] * jnp.exp(a_tot[:, h:h + 1] - acs[t][:, h:h + 1]) for t in range(L)], axis=0))
    c_stack = [jnp.concatenate([cm[t][:, g * STATE:(g + 1) * STATE] for t in range(L)], axis=0).astype(bf16)
               for g in range(GROUPS)]
    b_stack = [jnp.concatenate([bm[t][:, g * STATE:(g + 1) * STATE] for t in range(L)], axis=0).astype(bf16)
               for g in range(GROUPS)]
    seq_of_row = lax.broadcasted_iota(jnp.int32, (L * nb, HEAD_DIM), 0) & (nb - 1)

    def per_seq(b, yoff):
        mine = seq_of_row == b
        drow = dtot_ref[pl.ds(b, 1), :]
        out = []
        for h in range(HEADS):
            g = h // HEADS_PER_GROUP
            h0 = st_ref[b, h]
            r = lax.dot_general(c_stack[g], h0.astype(bf16), NT_DIMS, preferred_element_type=f32)
            out.append(yoff[h] + jnp.where(mine, r, 0.0))
            xm = jnp.where(mine, xd_stack[h], 0.0).astype(bf16)
            upd = lax.dot_general(xm, b_stack[g], TN_DIMS, preferred_element_type=f32)
            stn_ref[b, h] = drow[:, h:h + 1] * h0 + upd
        return tuple(out)

    yoff = lax.fori_loop(0, nb, per_seq,
                         tuple(jnp.zeros((L * nb, HEAD_DIM), f32) for _ in range(HEADS)))

    for t in range(L):
        parts = [y_heads[t][h] + yoff[h][t * nb:(t + 1) * nb] * jnp.exp(acs[t][:, h:h + 1])
                 for h in range(HEADS)]
        y = jnp.concatenate(parts, axis=-1) + xs[t] * dexp_ref[...]
        y_ref[t] = _gated_group_norm(y, z_ref[t], ng_ref[...])


def _ssd_sample(xbc, z, dt, cbuf, st, cw, cb, dtb, alog, dexp, ng, *, nb=8):
    L, B, _ = xbc.shape
    assert nb & (nb - 1) == 0 and B % nb == 0
    tblk = lambda n, wd: pl.BlockSpec((n, nb, wd), lambda i: (0, i, 0))
    st_spec = pl.BlockSpec((nb, HEADS, HEAD_DIM, STATE), lambda i: (i, 0, 0, 0))
    return pl.pallas_call(
        functools.partial(_ssd_sample_kernel, L=L, nb=nb),
        grid=(B // nb,),
        in_specs=[tblk(L, CONV_DIM), tblk(L, D_SSD), tblk(L, DT_PAD), tblk(CONV_W - 1, CONV_DIM), st_spec,
                  _const_spec((CONV_W, CONV_DIM)), _const_spec((1, CONV_DIM)),
                  _const_spec((1, DT_PAD)), _const_spec((1, DT_PAD)),
                  _const_spec((1, D_SSD)), _const_spec((1, D_SSD))],
        out_specs=[tblk(L, D_SSD), st_spec, tblk(CONV_W - 1, CONV_DIM)],
        out_shape=[jax.ShapeDtypeStruct((L, B, D_SSD), f32),
                   jax.ShapeDtypeStruct((B, HEADS, HEAD_DIM, STATE), f32),
                   jax.ShapeDtypeStruct((CONV_W - 1, B, CONV_DIM), f32)],
        scratch_shapes=[pltpu.VMEM((nb, DT_PAD), f32)],
        compiler_params=_params("parallel"),
        name="ssd_sample",
    )(xbc, z, dt, cbuf, st, cw, cb, dtb, alog, dexp, ng)


def _s5_param_kernel(lr_ref, li_ref, ls_ref, br_ref, bi_ref, abr_ref, abi_ref, bbr_ref, bbi_ref):
    lr, li = lr_ref[...], li_ref[...]
    step = jnp.exp(ls_ref[...])
    mag = jnp.exp(lr * step)
    abr = mag * jnp.cos(li * step)
    abi = mag * jnp.sin(li * step)
    nr, ni = abr - 1.0, abi
    den = lr * lr + li * li
    fr = (nr * lr + ni * li) / den
    fi = (ni * lr - nr * li) / den
    abr_ref[...] = abr
    abi_ref[...] = abi
    br, bi = br_ref[...], bi_ref[...]
    bbr_ref[...] = fr * br - fi * bi
    bbi_ref[...] = fr * bi + fi * br


def _s5_params(a_re, a_im, log_step, b_re_t, b_im_t):
    g, p = a_re.shape
    vec = jax.ShapeDtypeStruct((g, 1, p), f32)
    mat = jax.ShapeDtypeStruct(b_re_t.shape, f32)
    return pl.pallas_call(_s5_param_kernel, out_shape=[vec, vec, mat, mat], name="s5_params")(
        a_re.reshape(g, 1, p), a_im.reshape(g, 1, p), log_step.reshape(g, 1, 1), b_re_t, b_im_t)


def _gelu(x):
    return 0.5 * x * (1.0 + lax.erf(x * (2.0 ** -0.5)))


def _s5_scan_registers(hbuf_ref, hst_ref, ar_ref, ai_ref, s0, gs, tl):
    ns = S5_SLAB_STATE

    def body(t, carry):
        r = pl.multiple_of(SUBLANES + t * SUBLANES, SUBLANES)
        out = []
        for k in range(gs):
            pr, pi = carry[2 * k], carry[2 * k + 1]
            ar, ai = ar_ref[s0 + k], ai_ref[s0 + k]
            cur = hbuf_ref[k, pl.ds(r, SUBLANES), :]
            nr = ar * pr - ai * pi + cur[:, :ns]
            ni = ar * pi + ai * pr + cur[:, ns:]
            hbuf_ref[k, pl.ds(r, SUBLANES), :ns] = nr
            hbuf_ref[k, pl.ds(r, SUBLANES), ns:] = ni
            out += [nr, ni]
        return tuple(out)

    init = []
    for k in range(gs):
        init += [hst_ref[s0 + k, :, :ns], hst_ref[s0 + k, :, ns:]]
    fin = lax.fori_loop(0, tl, body, tuple(init), unroll=2)
    for k in range(gs):
        hst_ref[s0 + k, :, :ns] = fin[2 * k]
        hst_ref[s0 + k, :, ns:] = fin[2 * k + 1]


def _s5_scan_vmem(hbuf_ref, hst_ref, ar_ref, ai_ref, s, nb, rows):
    ns = S5_SLAB_STATE
    hbuf_ref[0, 0:nb, :] = hst_ref[s]
    ar, ai = ar_ref[s], ai_ref[s]

    def body(i, carry):
        j = pl.multiple_of(nb + i * SUBLANES, SUBLANES)
        prev = hbuf_ref[0, pl.ds(j - nb, SUBLANES), :]
        cur = hbuf_ref[0, pl.ds(j, SUBLANES), :]
        pr, pi = prev[:, :ns], prev[:, ns:]
        hbuf_ref[0, pl.ds(j, SUBLANES), :ns] = ar * pr - ai * pi + cur[:, :ns]
        hbuf_ref[0, pl.ds(j, SUBLANES), ns:] = ar * pi + ai * pr + cur[:, ns:]
        return carry

    lax.fori_loop(0, rows // SUBLANES, body, 0)
    hst_ref[s] = hbuf_ref[0, rows:rows + nb, :]


def _s5_kernel(u_ref, re0_ref, im0_ref, ar_ref, ai_ref, bblk_ref, cblk_ref, dsk_ref, wglu_ref, bglu_ref,
               y_ref, ren_ref, imn_ref, ut_ref, hbuf_ref, hst_ref, g_ref, *, nb, tl, gs, batch_major):
    step = pl.program_id(0)
    rows = nb * tl
    ns = S5_SLAB_STATE

    @pl.when(step == 0)
    def _():
        for s in range(S5_SLABS):
            hst_ref[s, :, :ns] = re0_ref[:, s * ns:(s + 1) * ns]
            hst_ref[s, :, ns:] = im0_ref[:, s * ns:(s + 1) * ns]

    if batch_major:
        ut_ref[...] = jnp.swapaxes(u_ref[...], 0, 1).reshape(rows, D_S5)
    else:
        ut_ref[...] = u_ref[...].reshape(rows, D_S5)

    for s0 in range(0, S5_SLABS, gs):
        for k in range(gs):
            us = ut_ref[:, (s0 + k) * LANES:(s0 + k + 1) * LANES]
            hbuf_ref[k, nb:nb + rows, :] = jnp.dot(us.astype(bf16), bblk_ref[s0 + k],
                                                   preferred_element_type=f32)
        if nb == SUBLANES:
            _s5_scan_registers(hbuf_ref, hst_ref, ar_ref, ai_ref, s0, gs, tl)
        else:
            _s5_scan_vmem(hbuf_ref, hst_ref, ar_ref, ai_ref, s0, nb, rows)
        for k in range(gs):
            sl = slice((s0 + k) * LANES, (s0 + k + 1) * LANES)
            ys = jnp.dot(hbuf_ref[k, nb:nb + rows, :].astype(bf16), cblk_ref[s0 + k],
                         preferred_element_type=f32)
            ys = ys + dsk_ref[:, sl] * ut_ref[:, sl]
            g_ref[:, sl] = _gelu(ys)

    g = g_ref[...]
    gate = jnp.dot(g.astype(bf16), wglu_ref[...], preferred_element_type=f32) + bglu_ref[...]
    out = g * jax.nn.sigmoid(gate)
    if batch_major:
        y_ref[...] = jnp.swapaxes(out.reshape(tl, nb, D_S5), 0, 1)
    else:
        y_ref[...] = out.reshape(tl, nb, D_S5)

    @pl.when(step == pl.num_programs(0) - 1)
    def _():
        for s in range(S5_SLABS):
            ren_ref[:, s * ns:(s + 1) * ns] = hst_ref[s, :, :ns]
            imn_ref[:, s * ns:(s + 1) * ns] = hst_ref[s, :, ns:]


def _s5(u, re0, im0, ar, ai, bblk, cblk, dsk, wglu, bglu, *, tl, batch_major):
    if batch_major:
        nb, L, _ = u.shape
        u_spec = pl.BlockSpec((nb, tl, D_S5), lambda i: (0, i, 0))
    else:
        L, nb, _ = u.shape
        u_spec = pl.BlockSpec((tl, nb, D_S5), lambda i: (i, 0, 0))
    rows = nb * tl
    nstate = S5_GROUPS * S5_STATE
    st_spec = _const_spec((nb, nstate))
    gs = 2 if nb == SUBLANES else 1
    return pl.pallas_call(
        functools.partial(_s5_kernel, nb=nb, tl=tl, gs=gs, batch_major=batch_major),
        grid=(L // tl,),
        in_specs=[u_spec, st_spec, st_spec,
                  _const_spec((S5_SLABS, SUBLANES, S5_SLAB_STATE)),
                  _const_spec((S5_SLABS, SUBLANES, S5_SLAB_STATE)),
                  _const_spec((S5_SLABS, LANES, 2 * S5_SLAB_STATE)),
                  _const_spec((S5_SLABS, 2 * S5_SLAB_STATE, LANES)),
                  _const_spec((1, D_S5)), _const_spec((D_S5, D_S5)), _const_spec((1, D_S5))],
        out_specs=[u_spec, st_spec, st_spec],
        out_shape=[jax.ShapeDtypeStruct(u.shape, f32),
                   jax.ShapeDtypeStruct((nb, nstate), f32),
                   jax.ShapeDtypeStruct((nb, nstate), f32)],
        scratch_shapes=[pltpu.VMEM((rows, D_S5), f32),
                        pltpu.VMEM((gs, nb + rows, 2 * S5_SLAB_STATE), f32),
                        pltpu.VMEM((S5_SLABS, nb, 2 * S5_SLAB_STATE), f32),
                        pltpu.VMEM((rows, D_S5), f32)],
        compiler_params=_params("arbitrary"),
        name="s5_mixer",
    )(u, re0, im0, ar, ai, bblk, cblk, dsk, wglu, bglu)


def _ffn_kernel(x_ref, ys_ref, y5_ref, mod_ref, modf_ref, n2g_ref, nfg_ref, wo_ref, wg_ref, wu_ref, wd_ref,
                o_ref, *, per_row, ff_chunk):
    g1 = _mod_row(mod_ref, 2, per_row)
    sh2 = _mod_row(mod_ref, 3, per_row)
    sc2 = _mod_row(mod_ref, 4, per_row)
    g2 = _mod_row(mod_ref, 5, per_row)
    shf = _mod_row(modf_ref, 0, per_row)
    scf = _mod_row(modf_ref, 1, per_row)
    att = jnp.dot(ys_ref[0].astype(bf16), wo_ref[:D_SSD, :], preferred_element_type=f32)
    att = att + jnp.dot(y5_ref[0].astype(bf16), wo_ref[D_SSD:, :], preferred_element_type=f32)
    x1 = x_ref[0] + g1 * att
    v = _rms_mod(x1, n2g_ref[...], sc2, sh2).astype(bf16)
    ff = None
    for o in range(0, D_FF, ff_chunk):
        gate = jnp.dot(v, wg_ref[:, o:o + ff_chunk], preferred_element_type=f32)
        up = jnp.dot(v, wu_ref[:, o:o + ff_chunk], preferred_element_type=f32)
        hid = (_silu(gate) * up).astype(bf16)
        part = jnp.dot(hid, wd_ref[o:o + ff_chunk, :], preferred_element_type=f32)
        ff = part if ff is None else ff + part
    x2 = x1 + g2 * ff
    o_ref[0] = _rms_mod(x2, nfg_ref[...], scf, shf)


def _ffn(x, ys, y5, mod, modf, n2g, nfg, wo, wg, wu, wd, *, tm, per_row, ff_chunk=256):
    nb, rows, d = x.shape
    nt = rows // tm
    if per_row:
        mod_spec = pl.BlockSpec((N_ADA, tm, d), lambda i, j: (0, j, 0))
        modf_spec = pl.BlockSpec((2, tm, d), lambda i, j: (0, j, 0))
    else:
        mod_spec = pl.BlockSpec((1, N_ADA, d), lambda i, j: (i, 0, 0))
        modf_spec = pl.BlockSpec((1, 2, d), lambda i, j: (i, 0, 0))
    blk = lambda wd_: pl.BlockSpec((1, tm, wd_), lambda i, j: (i, j, 0))
    single = dict(pipeline_mode=pl.Buffered(1))
    wspec = lambda shape: pl.BlockSpec(shape, lambda i, j: (0, 0), **single)
    return pl.pallas_call(
        functools.partial(_ffn_kernel, per_row=per_row, ff_chunk=ff_chunk),
        grid=(nb, nt),
        in_specs=[blk(d), blk(D_SSD), blk(D_S5), mod_spec, modf_spec,
                  _const_spec((1, d)), _const_spec((1, d)),
                  wspec((d, d)), wspec((d, D_FF)), wspec((d, D_FF)), wspec((D_FF, d))],
        out_specs=blk(d),
        out_shape=jax.ShapeDtypeStruct((nb, rows, d), f32),
        compiler_params=_params("parallel", "parallel"),
        name="out_ffn",
    )(x, ys, y5, mod, modf, n2g, nfg, wo, wg, wu, wd)


def _block_diag(m):
    s, k, a, b = m.shape
    eye = jnp.eye(k, dtype=m.dtype)
    return (m[:, :, :, None, :] * eye[None, :, None, :, None]).reshape(s, k * a, k * b)


def kernel(x_prompt, x_sample, c_prompt, c_sample, state_ssd, state_conv, state_s5_re, state_s5_im, w_ada, b_ada, norm1_g, w_in, conv_w, conv_b, ssd_dt_bias, ssd_A_log, ssd_D, ssd_norm_g, s5_A_re, s5_A_im, s5_log_step, s5_B_re, s5_B_im, s5_C_re, s5_C_im, s5_D, w_glu, b_glu, w_out, norm2_g, w_ffn_gate, w_ffn_up, w_ffn_down, w_ada_f, b_ada_f, normf_g):
    assert w_ada.shape[0] == 1, "single-layer stack"
    bp, seq, d = x_prompt.shape
    bs, dseq, _ = x_sample.shape

    c_all = jnp.concatenate([c_prompt, c_sample], axis=0)
    mod = _ada_mod(c_all, w_ada[0], b_ada[0])
    modf = _ada_mod(c_all, w_ada_f, b_ada_f)
    mod_p = mod[:bp].reshape(bp, N_ADA, d)
    modf_p = modf[:bp].reshape(bp, 2, d)
    mod_s = mod[bp:].reshape(bs, N_ADA, d).transpose(1, 0, 2)
    modf_s = modf[bp:].reshape(bs, 2, d).transpose(1, 0, 2)

    o1, o2, o3 = D_SSD, D_SSD + CONV_DIM, D_SSD + CONV_DIM + HEADS
    wi = w_in[0]
    w_in_p = jnp.concatenate(
        [wi[:, :o2], wi[:, o3:], wi[:, o2:o3], jnp.zeros((d, DT_PAD - HEADS), f32)], axis=1).astype(bf16)
    pad_h = lambda v: jnp.concatenate([v, jnp.zeros((DT_PAD - HEADS,), f32)]).reshape(1, DT_PAD)
    dtb = pad_h(ssd_dt_bias[0])
    alog = pad_h(ssd_A_log[0])
    dexp = jnp.repeat(ssd_D[0], HEAD_DIM).reshape(1, D_SSD)
    ng = ssd_norm_g[0].reshape(1, D_SSD)
    cw, cb = conv_w[0], conv_b[0].reshape(1, CONV_DIM)

    abr, abi, bbr, bbi = _s5_params(s5_A_re[0], s5_A_im[0], s5_log_step[0],
                                    s5_B_re[0].transpose(0, 2, 1), s5_B_im[0].transpose(0, 2, 1))
    gps = S5_GROUPS // S5_SLABS
    tile_rows = lambda v: jnp.broadcast_to(v.reshape(S5_SLABS, 1, S5_SLAB_STATE),
                                           (S5_SLABS, SUBLANES, S5_SLAB_STATE))
    ar, ai = tile_rows(abr), tile_rows(abi)
    slab = lambda m: m.reshape((S5_SLABS, gps) + m.shape[1:])
    bblk = jnp.concatenate([_block_diag(slab(bbr)), _block_diag(slab(bbi))], axis=2).astype(bf16)
    cblk = jnp.concatenate([_block_diag(slab(s5_C_re[0].transpose(0, 2, 1))),
                            _block_diag(slab(-s5_C_im[0].transpose(0, 2, 1)))], axis=1).astype(bf16)
    dsk = s5_D[0].reshape(1, D_S5)
    wglu = w_glu[0].astype(bf16)
    bglu = b_glu[0].reshape(1, D_S5)
    wo = w_out[0].astype(bf16)
    wg, wu, wd = w_ffn_gate[0].astype(bf16), w_ffn_up[0].astype(bf16), w_ffn_down[0].astype(bf16)
    n1g, n2g, nfg = norm1_g[0].reshape(1, d), norm2_g[0].reshape(1, d), normf_g.reshape(1, d)
    nstate = S5_GROUPS * S5_STATE

    z, xbc, u5, dt = _inproj(x_prompt, mod_p, n1g, w_in_p, tm=512, per_row=False)
    y_ssd, ssd_p, conv_p = _ssd_prompt(xbc, z, dt, cw, cb, dtb, alog, dexp, ng)
    zeros_st = jnp.zeros((bp, nstate), f32)
    y_s5, re_p, im_p = _s5(u5, zeros_st, zeros_st, ar, ai, bblk, cblk, dsk, wglu, bglu,
                           tl=128, batch_major=True)
    y_prompt = _ffn(x_prompt, y_ssd, y_s5, mod_p, modf_p, n2g, nfg, wo, wg, wu, wd, tm=512, per_row=False)

    xs_t = x_sample.transpose(1, 0, 2)
    z, xbc, u5, dt = _inproj(xs_t, mod_s, n1g, w_in_p, tm=bs, per_row=True)
    y_ssd, ssd_s, conv_s = _ssd_sample(xbc, z, dt, state_conv[0].transpose(1, 0, 2), state_ssd[0],
                                       cw, cb, dtb, alog, dexp, ng)
    y_s5, re_s, im_s = _s5(u5, state_s5_re[0].reshape(bs, nstate), state_s5_im[0].reshape(bs, nstate),
                           ar, ai, bblk, cblk, dsk, wglu, bglu, tl=dseq, batch_major=False)
    y_sample = _ffn(xs_t, y_ssd, y_s5, mod_s, modf_s, n2g, nfg, wo, wg, wu, wd, tm=bs, per_row=True)
    y_sample = y_sample.transpose(1, 0, 2)

    g5 = (S5_GROUPS, S5_STATE)
    return (y_prompt, y_sample,
            ssd_p[None], ssd_s[None],
            conv_p[None], conv_s.transpose(1, 0, 2)[None],
            re_p.reshape((1, bp) + g5), re_s.reshape((1, bs) + g5),
            im_p.reshape((1, bp) + g5), im_s.reshape((1, bs) + g5))
```

```python
import functools

import jax
import jax.numpy as jnp
from jax import lax
from jax.experimental import pallas as pl
from jax.experimental.pallas import tpu as pltpu

f32 = jnp.float32
bf16 = jnp.bfloat16

D_MODEL = 1024
D_SSD = 512
HEAD_DIM = 64
HEADS = 8
GROUPS = 2
HEADS_PER_GROUP = HEADS // GROUPS
STATE = 128
CONV_W = 4
CONV_DIM = D_SSD + 2 * GROUPS * STATE
D_S5 = 512
S5_CH = 16
S5_GROUPS = 32
S5_STATE = 64
D_FF = 2816
N_ADA = 6
EPS = 1e-6

LANES = 128
SUBLANES = 8
SSD_CHUNK = 128
S5_SLABS = D_S5 // LANES
S5_SLAB_STATE = (S5_GROUPS // S5_SLABS) * S5_STATE
DT_PAD = LANES
IN_PROJ_PACKED = D_SSD + CONV_DIM + D_S5 + DT_PAD
VMEM_LIMIT = 56 * 1024 * 1024

NT_DIMS = (((1,), (1,)), ((), ()))
TN_DIMS = (((0,), (0,)), ((), ()))


def _silu(x):
    return x * jax.nn.sigmoid(x)


def _by_row(fn, v, *ms):
    r, d = v.shape
    m_rows = ms[0].shape[0]
    if m_rows in (1, r):
        return fn(v, *ms)
    out = fn(v.reshape(r // m_rows, m_rows, d), *[m[None] for m in ms])
    return out.reshape(r, d)


def _rms_mod(x, g, sc, sh):
    y = x * lax.rsqrt(jnp.mean(x * x, axis=-1, keepdims=True) + EPS)
    return _by_row(lambda v, s, t: v * (1.0 + s) + t, y * g, sc, sh)


def _mod_row(mod_ref, i, per_row):
    return mod_ref[i] if per_row else mod_ref[0, i:i + 1, :]


def _const_spec(shape):
    nd = len(shape)
    return pl.BlockSpec(shape, lambda *_: (0,) * nd)


def _params(*sem):
    return pltpu.CompilerParams(dimension_semantics=sem, vmem_limit_bytes=VMEM_LIMIT)


def _ada_kernel(c_ref, w_ref, b_ref, o_ref):
    s = _silu(c_ref[...]).astype(bf16)
    o_ref[...] = jnp.dot(s, w_ref[...].astype(bf16), preferred_element_type=f32) + b_ref[...]


def _ada_mod(c, w, b, tn=512):
    m, k = c.shape
    n = w.shape[1]
    return pl.pallas_call(
        _ada_kernel,
        grid=(n // tn,),
        in_specs=[pl.BlockSpec((m, k), lambda j: (0, 0)),
                  pl.BlockSpec((k, tn), lambda j: (0, j)),
                  pl.BlockSpec((1, tn), lambda j: (0, j))],
        out_specs=pl.BlockSpec((m, tn), lambda j: (0, j)),
        out_shape=jax.ShapeDtypeStruct((m, n), f32),
        compiler_params=_params("parallel"),
        name="ada_mod",
    )(c, w, b.reshape(1, n))


def _inproj_kernel(x_ref, mod_ref, g_ref, w_ref, z_ref, xbc_ref, u5_ref, dt_ref, *, per_row):
    sh = _mod_row(mod_ref, 0, per_row)
    sc = _mod_row(mod_ref, 1, per_row)
    u = _rms_mod(x_ref[0], g_ref[...], sc, sh).astype(bf16)
    o = 0
    for ref, width in ((z_ref, D_SSD), (xbc_ref, CONV_DIM), (u5_ref, D_S5), (dt_ref, DT_PAD)):
        ref[0] = jnp.dot(u, w_ref[:, o:o + width], preferred_element_type=f32)
        o += width


def _inproj(x, mod, g, w, *, tm, per_row):
    nb, rows, d = x.shape
    nt = rows // tm
    if per_row:
        mod_spec = _const_spec(mod.shape)
    else:
        mod_spec = pl.BlockSpec((1, N_ADA, d), lambda i, j: (i, 0, 0))
    widths = (D_SSD, CONV_DIM, D_S5, DT_PAD)
    return pl.pallas_call(
        functools.partial(_inproj_kernel, per_row=per_row),
        grid=(nb, nt),
        in_specs=[pl.BlockSpec((1, tm, d), lambda i, j: (i, j, 0)),
                  mod_spec,
                  _const_spec((1, d)),
                  _const_spec((d, IN_PROJ_PACKED))],
        out_specs=[pl.BlockSpec((1, tm, wd), lambda i, j: (i, j, 0)) for wd in widths],
        out_shape=[jax.ShapeDtypeStruct((nb, rows, wd), f32) for wd in widths],
        compiler_params=_params("parallel", "parallel"),
        name="in_proj",
    )(x, mod, g, w)


def _split3(x):
    hi = x.astype(bf16)
    r1 = x - hi.astype(f32)
    mid = r1.astype(bf16)
    lo = (r1 - mid.astype(f32)).astype(bf16)
    return hi, mid, lo


def _dot_sel_lhs(sel16, x):
    return sum(jnp.dot(sel16, p, preferred_element_type=f32) for p in _split3(x))


def _dot_sel_rhs(x, sel16):
    return sum(jnp.dot(p, sel16, preferred_element_type=f32) for p in _split3(x))


def _gated_group_norm(y, z, ng):
    y = y * _silu(z)
    gw = D_SSD // GROUPS
    parts = []
    for g in range(GROUPS):
        yg = y[:, g * gw:(g + 1) * gw]
        parts.append(yg * lax.rsqrt(jnp.mean(yg * yg, axis=-1, keepdims=True) + EPS))
    return jnp.concatenate(parts, axis=-1) * ng


def _ssd_prompt_kernel(xbc_ref, z_ref, dt_ref, cw_ref, cb_ref, dtb_ref, alog_ref, dexp_ref, ng_ref,
                       y_ref, st_ref, cn_ref, ext_ref, act_ref, h_ref, *, cps):
    T = SSD_CHUNK
    gw = HEADS_PER_GROUP * HEAD_DIM

    @pl.when(pl.program_id(1) == 0)
    def _():
        ext_ref[0:SUBLANES, :] = jnp.zeros((SUBLANES, CONV_DIM), f32)
        h_ref[...] = jnp.zeros_like(h_ref)

    row = lax.broadcasted_iota(jnp.int32, (T, T), 0)
    col = lax.broadcasted_iota(jnp.int32, (T, T), 1)
    tri = row >= col
    tri16 = jnp.where(tri, 1.0, 0.0).astype(bf16)
    low_half = lax.broadcasted_iota(jnp.int32, (T, LANES), 1) < HEAD_DIM
    a_neg = -jnp.exp(alog_ref[...])

    R = cps * T
    ext_ref[SUBLANES:SUBLANES + R, :] = xbc_ref[0]
    for ci in range(cps):
        ext = ext_ref[ci * T:ci * T + SUBLANES + T, :]
        conv = cb_ref[...] + ext[SUBLANES:, :] * cw_ref[CONV_W - 1:CONV_W, :]
        for m in range(1, CONV_W):
            k = CONV_W - 1 - m
            conv = conv + pltpu.roll(ext, m, 0)[SUBLANES:, :] * cw_ref[k:k + 1, :]
        act_ref[ci * T:(ci + 1) * T, :] = _silu(conv)
    cn_ref[0] = ext_ref[SUBLANES + R - (CONV_W - 1):SUBLANES + R, :]
    ext_ref[0:SUBLANES, :] = ext_ref[R:R + SUBLANES, :]

    lane_head = lax.broadcasted_iota(jnp.int32, (DT_PAD, D_SSD), 1) // HEAD_DIM
    sel16 = jnp.where(lax.broadcasted_iota(jnp.int32, (DT_PAD, D_SSD), 0) == lane_head, 1.0, 0.0).astype(bf16)

    def spread(q):
        return _dot_sel_rhs(q, sel16)

    def state_free(ci):
        r0 = ci * T
        act = act_ref[r0:r0 + T, :]
        xs = act[:, :D_SSD]
        bm = act[:, D_SSD:D_SSD + GROUPS * STATE]
        cm = act[:, D_SSD + GROUPS * STATE:]
        dtv = jax.nn.softplus(dt_ref[0, r0:r0 + T, :] + dtb_ref[...])
        a_cs = _dot_sel_lhs(tri16, dtv * a_neg)
        bg16s = [bm[:, g * STATE:(g + 1) * STATE].astype(bf16) for g in range(GROUPS)]
        cg16s = [cm[:, g * STATE:(g + 1) * STATE].astype(bf16) for g in range(GROUPS)]
        gmats = [lax.dot_general(cg16s[g], bg16s[g], NT_DIMS, preferred_element_type=f32)
                 for g in range(GROUPS)]
        yield
        a_last = a_cs[T - 1:T, :]
        a_cs_t = a_cs.T
        rep = spread(jnp.concatenate([dtv, jnp.exp(a_last - a_cs), jnp.exp(a_cs)], axis=0))
        yield
        x = xs * rep[:T]
        x16 = x.astype(bf16)
        xd16 = (x * rep[T:2 * T]).astype(bf16)
        e_cs = rep[2 * T:]
        yield
        y_part = []
        for g in range(GROUPS):
            for pr in range(HEADS_PER_GROUP // 2):
                s_pair = []
                for q in range(2):
                    h = g * HEADS_PER_GROUP + 2 * pr + q
                    seg = a_cs[:, h:h + 1] - a_cs_t[h:h + 1, :]
                    lmat = jnp.exp(jnp.where(tri, seg, -jnp.inf))
                    s_pair.append((gmats[g] * lmat).astype(bf16))
                lanes = slice(g * gw + pr * LANES, g * gw + (pr + 1) * LANES)
                both = jnp.dot(jnp.concatenate(s_pair, axis=0), x16[:, lanes], preferred_element_type=f32)
                yield
                y_part.append(jnp.where(low_half, both[:T], both[T:]))
        y_free = jnp.concatenate(y_part, axis=-1) + xs * dexp_ref[...]
        return y_free, e_cs, jnp.exp(a_last), xd16, bg16s, cg16s

    def state_step(ci, y_free, e_cs, e_last, xd16, bg16s, cg16s):
        r0 = ci * T
        y_offs = []
        for g in range(GROUPS):
            hp = h_ref[g]
            y_offs.append(lax.dot_general(cg16s[g], hp.astype(bf16), NT_DIMS, preferred_element_type=f32))
            upd = lax.dot_general(xd16[:, g * gw:(g + 1) * gw], bg16s[g], TN_DIMS, preferred_element_type=f32)
            for hh in range(HEADS_PER_GROUP):
                h = g * HEADS_PER_GROUP + hh
                rows = slice(hh * HEAD_DIM, (hh + 1) * HEAD_DIM)
                h_ref[g, rows, :] = e_last[:, h:h + 1] * hp[rows] + upd[rows]
        y = y_free + jnp.concatenate(y_offs, axis=-1) * e_cs
        y_ref[0, r0:r0 + T, :] = _gated_group_norm(y, z_ref[0, r0:r0 + T, :], ng_ref[...])

    gens = [state_free(ci) for ci in range(cps)]
    free = [None] * cps
    while any(f is None for f in free):
        for ci in range(cps):
            if free[ci] is None:
                try:
                    next(gens[ci])
                except StopIteration as done:
                    free[ci] = done.value
    for ci in range(cps):
        state_step(ci, *free[ci])

    @pl.when(pl.program_id(1) == pl.num_programs(1) - 1)
    def _():
        st_ref[0] = h_ref[...].reshape(HEADS, HEAD_DIM, STATE)


def _ssd_prompt(xbc, z, dt, cw, cb, dtb, alog, dexp, ng, *, cps=8):
    nb, L, _ = xbc.shape
    T = SSD_CHUNK
    rows = cps * T
    blk = lambda wd: pl.BlockSpec((1, rows, wd), lambda b, c: (b, c, 0))
    return pl.pallas_call(
        functools.partial(_ssd_prompt_kernel, cps=cps),
        grid=(nb, L // rows),
        in_specs=[blk(CONV_DIM), blk(D_SSD), blk(DT_PAD),
                  _const_spec((CONV_W, CONV_DIM)), _const_spec((1, CONV_DIM)),
                  _const_spec((1, DT_PAD)), _const_spec((1, DT_PAD)),
                  _const_spec((1, D_SSD)), _const_spec((1, D_SSD))],
        out_specs=[blk(D_SSD),
                   pl.BlockSpec((1, HEADS, HEAD_DIM, STATE), lambda b, c: (b, 0, 0, 0)),
                   pl.BlockSpec((1, CONV_W - 1, CONV_DIM), lambda b, c: (b, 0, 0))],
        out_shape=[jax.ShapeDtypeStruct((nb, L, D_SSD), f32),
                   jax.ShapeDtypeStruct((nb, HEADS, HEAD_DIM, STATE), f32),
                   jax.ShapeDtypeStruct((nb, CONV_W - 1, CONV_DIM), f32)],
        scratch_shapes=[pltpu.VMEM((SUBLANES + rows, CONV_DIM), f32),
                        pltpu.VMEM((rows, CONV_DIM), f32),
                        pltpu.VMEM((GROUPS, HEADS_PER_GROUP * HEAD_DIM, STATE), f32)],
        compiler_params=_params("parallel", "arbitrary"),
        name="ssd_prompt",
    )(xbc, z, dt, cw, cb, dtb, alog, dexp, ng)


def _ssd_sample_kernel(xbc_ref, z_ref, dt_ref, cbuf_ref, st_ref, cw_ref, cb_ref, dtb_ref, alog_ref,
                       dexp_ref, ng_ref, y_ref, stn_ref, cn_ref, dtot_ref, *, L, nb):
    full = [cbuf_ref[j] for j in range(CONV_W - 1)] + [xbc_ref[t] for t in range(L)]
    for j in range(CONV_W - 1):
        cn_ref[j] = full[L + j]
    a_neg = -jnp.exp(alog_ref[...])
    xs, bm, cm, dtv, acs = [], [], [], [], []
    run = None
    for t in range(L):
        conv = cb_ref[...]
        for k in range(CONV_W):
            conv = conv + full[t + k] * cw_ref[k:k + 1, :]
        act = _silu(conv)
        xs.append(act[:, :D_SSD])
        bm.append(act[:, D_SSD:D_SSD + GROUPS * STATE])
        cm.append(act[:, D_SSD + GROUPS * STATE:])
        d = jax.nn.softplus(dt_ref[t] + dtb_ref[...])
        dtv.append(d)
        run = d * a_neg if run is None else run + d * a_neg
        acs.append(run)
    a_tot = acs[L - 1]
    dtot_ref[...] = jnp.exp(a_tot)

    y_heads = [[None] * HEADS for _ in range(L)]
    xd_stack = []
    for h in range(HEADS):
        g = h // HEADS_PER_GROUP
        sl = slice(h * HEAD_DIM, (h + 1) * HEAD_DIM)
        xh = [xs[t][:, sl] * dtv[t][:, h:h + 1] for t in range(L)]
        for t in range(L):
            acc = None
            for s in range(t + 1):
                cb_dot = jnp.sum(cm[t][:, g * STATE:(g + 1) * STATE] * bm[s][:, g * STATE:(g + 1) * STATE],
                                 axis=-1, keepdims=True)
                w = cb_dot * jnp.exp(acs[t][:, h:h + 1] - acs[s][:, h:h + 1])
                acc = w * xh[s] if acc is None else acc + w * xh[s]
            y_heads[t][h] = acc
        xd_stack.append(jnp.concatenate(
            [xh[t] * jnp.exp(a_tot[:, h:h + 1] - acs[t][:, h:h + 1]) for t in range(L)], axis=0))
    c_stack = [jnp.concatenate([cm[t][:, g * STATE:(g + 1) * STATE] for t in range(L)], axis=0).astype(bf16)
               for g in range(GROUPS)]
    b_stack = [jnp.concatenate([bm[t][:, g * STATE:(g + 1) * STATE] for t in range(L)], axis=0).astype(bf16)
               for g in range(GROUPS)]
    seq_of_row = lax.broadcasted_iota(jnp.int32, (L * nb, HEAD_DIM), 0) & (nb - 1)

    def per_seq(b, yoff):
        mine = seq_of_row == b
        drow = dtot_ref[pl.ds(b, 1), :]
        out = []
        for h in range(HEADS):
            g = h // HEADS_PER_GROUP
            h0 = st_ref[b, h]
            r = lax.dot_general(c_stack[g], h0.astype(bf16), NT_DIMS, preferred_element_type=f32)
            out.append(yoff[h] + jnp.where(mine, r, 0.0))
            xm = jnp.where(mine, xd_stack[h], 0.0).astype(bf16)
            upd = lax.dot_general(xm, b_stack[g], TN_DIMS, preferred_element_type=f32)
            stn_ref[b, h] = drow[:, h:h + 1] * h0 + upd
        return tuple(out)

    yoff = lax.fori_loop(0, nb, per_seq,
                         tuple(jnp.zeros((L * nb, HEAD_DIM), f32) for _ in range(HEADS)))

    for t in range(L):
        parts = [y_heads[t][h] + yoff[h][t * nb:(t + 1) * nb] * jnp.exp(acs[t][:, h:h + 1])
                 for h in range(HEADS)]
        y = jnp.concatenate(parts, axis=-1) + xs[t] * dexp_ref[...]
        y_ref[t] = _gated_group_norm(y, z_ref[t], ng_ref[...])


def _ssd_sample(xbc, z, dt, cbuf, st, cw, cb, dtb, alog, dexp, ng, *, nb=8):
    L, B, _ = xbc.shape
    assert nb & (nb - 1) == 0 and B % nb == 0
    tblk = lambda n, wd: pl.BlockSpec((n, nb, wd), lambda i: (0, i, 0))
    st_spec = pl.BlockSpec((nb, HEADS, HEAD_DIM, STATE), lambda i: (i, 0, 0, 0))
    return pl.pallas_call(
        functools.partial(_ssd_sample_kernel, L=L, nb=nb),
        grid=(B // nb,),
        in_specs=[tblk(L, CONV_DIM), tblk(L, D_SSD), tblk(L, DT_PAD), tblk(CONV_W - 1, CONV_DIM), st_spec,
                  _const_spec((CONV_W, CONV_DIM)), _const_spec((1, CONV_DIM)),
                  _const_spec((1, DT_PAD)), _const_spec((1, DT_PAD)),
                  _const_spec((1, D_SSD)), _const_spec((1, D_SSD))],
        out_specs=[tblk(L, D_SSD), st_spec, tblk(CONV_W - 1, CONV_DIM)],
        out_shape=[jax.ShapeDtypeStruct((L, B, D_SSD), f32),
                   jax.ShapeDtypeStruct((B, HEADS, HEAD_DIM, STATE), f32),
                   jax.ShapeDtypeStruct((CONV_W - 1, B, CONV_DIM), f32)],
        scratch_shapes=[pltpu.VMEM((nb, DT_PAD), f32)],
        compiler_params=_params("parallel"),
        name="ssd_sample",
    )(xbc, z, dt, cbuf, st, cw, cb, dtb, alog, dexp, ng)


def _s5_param_kernel(lr_ref, li_ref, ls_ref, br_ref, bi_ref, abr_ref, abi_ref, bbr_ref, bbi_ref):
    lr, li = lr_ref[...], li_ref[...]
    step = jnp.exp(ls_ref[...])
    mag = jnp.exp(lr * step)
    abr = mag * jnp.cos(li * step)
    abi = mag * jnp.sin(li * step)
    nr, ni = abr - 1.0, abi
    den = lr * lr + li * li
    fr = (nr * lr + ni * li) / den
    fi = (ni * lr - nr * li) / den
    abr_ref[...] = abr
    abi_ref[...] = abi
    br, bi = br_ref[...], bi_ref[...]
    bbr_ref[...] = fr * br - fi * bi
    bbi_ref[...] = fr * bi + fi * br


def _s5_params(a_re, a_im, log_step, b_re_t, b_im_t):
    g, p = a_re.shape
    vec = jax.ShapeDtypeStruct((g, 1, p), f32)
    mat = jax.ShapeDtypeStruct(b_re_t.shape, f32)
    return pl.pallas_call(_s5_param_kernel, out_shape=[vec, vec, mat, mat], name="s5_params")(
        a_re.reshape(g, 1, p), a_im.reshape(g, 1, p), log_step.reshape(g, 1, 1), b_re_t, b_im_t)


def _gelu(x):
    return 0.5 * x * (1.0 + lax.erf(x * (2.0 ** -0.5)))


def _s5_scan_registers(hbuf_ref, hst_ref, ar_ref, ai_ref, s0, gs, tl):
    ns = S5_SLAB_STATE

    def body(t, carry):
        r = pl.multiple_of(SUBLANES + t * SUBLANES, SUBLANES)
        out = []
        for k in range(gs):
            pr, pi = carry[2 * k], carry[2 * k + 1]
            ar, ai = ar_ref[s0 + k], ai_ref[s0 + k]
            cur = hbuf_ref[k, pl.ds(r, SUBLANES), :]
            nr = ar * pr - ai * pi + cur[:, :ns]
            ni = ar * pi + ai * pr + cur[:, ns:]
            hbuf_ref[k, pl.ds(r, SUBLANES), :ns] = nr
            hbuf_ref[k, pl.ds(r, SUBLANES), ns:] = ni
            out += [nr, ni]
        return tuple(out)

    init = []
    for k in range(gs):
        init += [hst_ref[s0 + k, :, :ns], hst_ref[s0 + k, :, ns:]]
    fin = lax.fori_loop(0, tl, body, tuple(init), unroll=2)
    for k in range(gs):
        hst_ref[s0 + k, :, :ns] = fin[2 * k]
        hst_ref[s0 + k, :, ns:] = fin[2 * k + 1]


def _s5_scan_vmem(hbuf_ref, hst_ref, ar_ref, ai_ref, s, nb, rows):
    ns = S5_SLAB_STATE
    hbuf_ref[0, 0:nb, :] = hst_ref[s]
    ar, ai = ar_ref[s], ai_ref[s]

    def body(i, carry):
        j = pl.multiple_of(nb + i * SUBLANES, SUBLANES)
        prev = hbuf_ref[0, pl.ds(j - nb, SUBLANES), :]
        cur = hbuf_ref[0, pl.ds(j, SUBLANES), :]
        pr, pi = prev[:, :ns], prev[:, ns:]
        hbuf_ref[0, pl.ds(j, SUBLANES), :ns] = ar * pr - ai * pi + cur[:, :ns]
        hbuf_ref[0, pl.ds(j, SUBLANES), ns:] = ar * pi + ai * pr + cur[:, ns:]
        return carry

    lax.fori_loop(0, rows // SUBLANES, body, 0)
    hst_ref[s] = hbuf_ref[0, rows:rows + nb, :]


def _s5_kernel(u_ref, re0_ref, im0_ref, ar_ref, ai_ref, bblk_ref, cblk_ref, dsk_ref, wglu_ref, bglu_ref,
               y_ref, ren_ref, imn_ref, ut_ref, hbuf_ref, hst_ref, g_ref, *, nb, tl, gs, batch_major):
    step = pl.program_id(0)
    rows = nb * tl
    ns = S5_SLAB_STATE

    @pl.when(step == 0)
    def _():
        for s in range(S5_SLABS):
            hst_ref[s, :, :ns] = re0_ref[:, s * ns:(s + 1) * ns]
            hst_ref[s, :, ns:] = im0_ref[:, s * ns:(s + 1) * ns]

    if batch_major:
        ut_ref[...] = jnp.swapaxes(u_ref[...], 0, 1).reshape(rows, D_S5)
    else:
        ut_ref[...] = u_ref[...].reshape(rows, D_S5)

    for s0 in range(0, S5_SLABS, gs):
        for k in range(gs):
            us = ut_ref[:, (s0 + k) * LANES:(s0 + k + 1) * LANES]
            hbuf_ref[k, nb:nb + rows, :] = jnp.dot(us.astype(bf16), bblk_ref[s0 + k],
                                                   preferred_element_type=f32)
        if nb == SUBLANES:
            _s5_scan_registers(hbuf_ref, hst_ref, ar_ref, ai_ref, s0, gs, tl)
        else:
            _s5_scan_vmem(hbuf_ref, hst_ref, ar_ref, ai_ref, s0, nb, rows)
        for k in range(gs):
            sl = slice((s0 + k) * LANES, (s0 + k + 1) * LANES)
            ys = jnp.dot(hbuf_ref[k, nb:nb + rows, :].astype(bf16), cblk_ref[s0 + k],
                         preferred_element_type=f32)
            ys = ys + dsk_ref[:, sl] * ut_ref[:, sl]
            g_ref[:, sl] = _gelu(ys)

    g = g_ref[...]
    gate = jnp.dot(g.astype(bf16), wglu_ref[...], preferred_element_type=f32) + bglu_ref[...]
    out = g * jax.nn.sigmoid(gate)
    if batch_major:
        y_ref[...] = jnp.swapaxes(out.reshape(tl, nb, D_S5), 0, 1)
    else:
        y_ref[...] = out.reshape(tl, nb, D_S5)

    @pl.when(step == pl.num_programs(0) - 1)
    def _():
        for s in range(S5_SLABS):
            ren_ref[:, s * ns:(s + 1) * ns] = hst_ref[s, :, :ns]
            imn_ref[:, s * ns:(s + 1) * ns] = hst_ref[s, :, ns:]


def _s5(u, re0, im0, ar, ai, bblk, cblk, dsk, wglu, bglu, *, tl, batch_major):
    if batch_major:
        nb, L, _ = u.shape
        u_spec = pl.BlockSpec((nb, tl, D_S5), lambda i: (0, i, 0))
    else:
        L, nb, _ = u.shape
        u_spec = pl.BlockSpec((tl, nb, D_S5), lambda i: (i, 0, 0))
    rows = nb * tl
    nstate = S5_GROUPS * S5_STATE
    st_spec = _const_spec((nb, nstate))
    gs = 2 if nb == SUBLANES else 1
    return pl.pallas_call(
        functools.partial(_s5_kernel, nb=nb, tl=tl, gs=gs, batch_major=batch_major),
        grid=(L // tl,),
        in_specs=[u_spec, st_spec, st_spec,
                  _const_spec((S5_SLABS, SUBLANES, S5_SLAB_STATE)),
                  _const_spec((S5_SLABS, SUBLANES, S5_SLAB_STATE)),
                  _const_spec((S5_SLABS, LANES, 2 * S5_SLAB_STATE)),
                  _const_spec((S5_SLABS, 2 * S5_SLAB_STATE, LANES)),
                  _const_spec((1, D_S5)), _const_spec((D_S5, D_S5)), _const_spec((1, D_S5))],
        out_specs=[u_spec, st_spec, st_spec],
        out_shape=[jax.ShapeDtypeStruct(u.shape, f32),
                   jax.ShapeDtypeStruct((nb, nstate), f32),
                   jax.ShapeDtypeStruct((nb, nstate), f32)],
        scratch_shapes=[pltpu.VMEM((rows, D_S5), f32),
                        pltpu.VMEM((gs, nb + rows, 2 * S5_SLAB_STATE), f32),
                        pltpu.VMEM((S5_SLABS, nb, 2 * S5_SLAB_STATE), f32),
                        pltpu.VMEM((rows, D_S5), f32)],
        compiler_params=_params("arbitrary"),
        name="s5_mixer",
    )(u, re0, im0, ar, ai, bblk, cblk, dsk, wglu, bglu)


def _ffn_kernel(x_ref, ys_ref, y5_ref, mod_ref, modf_ref, n2g_ref, nfg_ref, wo_ref, wg_ref, wu_ref, wd_ref,
                o_ref, *, per_row, ff_chunk):
    g1 = _mod_row(mod_ref, 2, per_row)
    sh2 = _mod_row(mod_ref, 3, per_row)
    sc2 = _mod_row(mod_ref, 4, per_row)
    g2 = _mod_row(mod_ref, 5, per_row)
    shf = _mod_row(modf_ref, 0, per_row)
    scf = _mod_row(modf_ref, 1, per_row)
    att = jnp.dot(ys_ref[0].astype(bf16), wo_ref[:D_SSD, :], preferred_element_type=f32)
    att = att + jnp.dot(y5_ref[0].astype(bf16), wo_ref[D_SSD:, :], preferred_element_type=f32)
    x1 = x_ref[0] + _by_row(lambda v, s: v * s, att, g1)
    v = _rms_mod(x1, n2g_ref[...], sc2, sh2).astype(bf16)
    ff = None
    for o in range(0, D_FF, ff_chunk):
        gate = jnp.dot(v, wg_ref[:, o:o + ff_chunk], preferred_element_type=f32)
        up = jnp.dot(v, wu_ref[:, o:o + ff_chunk], preferred_element_type=f32)
        hid = (_silu(gate) * up).astype(bf16)
        part = jnp.dot(hid, wd_ref[o:o + ff_chunk, :], preferred_element_type=f32)
        ff = part if ff is None else ff + part
    x2 = x1 + _by_row(lambda v, s: v * s, ff, g2)
    o_ref[0] = _rms_mod(x2, nfg_ref[...], scf, shf)


def _ffn(x, ys, y5, mod, modf, n2g, nfg, wo, wg, wu, wd, *, tm, per_row, ff_chunk=256):
    nb, rows, d = x.shape
    nt = rows // tm
    if per_row:
        mod_spec = _const_spec(mod.shape)
        modf_spec = _const_spec(modf.shape)
    else:
        mod_spec = pl.BlockSpec((1, N_ADA, d), lambda i, j: (i, 0, 0))
        modf_spec = pl.BlockSpec((1, 2, d), lambda i, j: (i, 0, 0))
    blk = lambda wd_: pl.BlockSpec((1, tm, wd_), lambda i, j: (i, j, 0))
    single = dict(pipeline_mode=pl.Buffered(1))
    wspec = lambda shape: pl.BlockSpec(shape, lambda i, j: (0, 0), **single)
    return pl.pallas_call(
        functools.partial(_ffn_kernel, per_row=per_row, ff_chunk=ff_chunk),
        grid=(nb, nt),
        in_specs=[blk(d), blk(D_SSD), blk(D_S5), mod_spec, modf_spec,
                  _const_spec((1, d)), _const_spec((1, d)),
                  wspec((d, d)), wspec((d, D_FF)), wspec((d, D_FF)), wspec((D_FF, d))],
        out_specs=blk(d),
        out_shape=jax.ShapeDtypeStruct((nb, rows, d), f32),
        compiler_params=_params("parallel", "parallel"),
        name="out_ffn",
    )(x, ys, y5, mod, modf, n2g, nfg, wo, wg, wu, wd)


def _block_diag(m):
    s, k, a, b = m.shape
    eye = jnp.eye(k, dtype=m.dtype)
    return (m[:, :, :, None, :] * eye[None, :, None, :, None]).reshape(s, k * a, k * b)


def kernel(x_prompt, x_sample, c_prompt, c_sample, state_ssd, state_conv, state_s5_re, state_s5_im, w_ada, b_ada, norm1_g, w_in, conv_w, conv_b, ssd_dt_bias, ssd_A_log, ssd_D, ssd_norm_g, s5_A_re, s5_A_im, s5_log_step, s5_B_re, s5_B_im, s5_C_re, s5_C_im, s5_D, w_glu, b_glu, w_out, norm2_g, w_ffn_gate, w_ffn_up, w_ffn_down, w_ada_f, b_ada_f, normf_g):
    assert w_ada.shape[0] == 1, "single-layer stack"
    bp, seq, d = x_prompt.shape
    bs, dseq, _ = x_sample.shape

    c_all = jnp.concatenate([c_prompt, c_sample], axis=0)
    mod = _ada_mod(c_all, w_ada[0], b_ada[0])
    modf = _ada_mod(c_all, w_ada_f, b_ada_f)
    mod_p = mod[:bp].reshape(bp, N_ADA, d)
    modf_p = modf[:bp].reshape(bp, 2, d)
    mod_s = mod[bp:].reshape(bs, N_ADA, d).transpose(1, 0, 2)
    modf_s = modf[bp:].reshape(bs, 2, d).transpose(1, 0, 2)

    o1, o2, o3 = D_SSD, D_SSD + CONV_DIM, D_SSD + CONV_DIM + HEADS
    wi = w_in[0]
    w_in_p = jnp.concatenate(
        [wi[:, :o2], wi[:, o3:], wi[:, o2:o3], jnp.zeros((d, DT_PAD - HEADS), f32)], axis=1).astype(bf16)
    pad_h = lambda v: jnp.concatenate([v, jnp.zeros((DT_PAD - HEADS,), f32)]).reshape(1, DT_PAD)
    dtb = pad_h(ssd_dt_bias[0])
    alog = pad_h(ssd_A_log[0])
    dexp = jnp.repeat(ssd_D[0], HEAD_DIM).reshape(1, D_SSD)
    ng = ssd_norm_g[0].reshape(1, D_SSD)
    cw, cb = conv_w[0], conv_b[0].reshape(1, CONV_DIM)

    abr, abi, bbr, bbi = _s5_params(s5_A_re[0], s5_A_im[0], s5_log_step[0],
                                    s5_B_re[0].transpose(0, 2, 1), s5_B_im[0].transpose(0, 2, 1))
    gps = S5_GROUPS // S5_SLABS
    tile_rows = lambda v: jnp.broadcast_to(v.reshape(S5_SLABS, 1, S5_SLAB_STATE),
                                           (S5_SLABS, SUBLANES, S5_SLAB_STATE))
    ar, ai = tile_rows(abr), tile_rows(abi)
    slab = lambda m: m.reshape((S5_SLABS, gps) + m.shape[1:])
    bblk = jnp.concatenate([_block_diag(slab(bbr)), _block_diag(slab(bbi))], axis=2).astype(bf16)
    cblk = jnp.concatenate([_block_diag(slab(s5_C_re[0].transpose(0, 2, 1))),
                            _block_diag(slab(-s5_C_im[0].transpose(0, 2, 1)))], axis=1).astype(bf16)
    dsk = s5_D[0].reshape(1, D_S5)
    wglu = w_glu[0].astype(bf16)
    bglu = b_glu[0].reshape(1, D_S5)
    wo = w_out[0].astype(bf16)
    wg, wu, wd = w_ffn_gate[0].astype(bf16), w_ffn_up[0].astype(bf16), w_ffn_down[0].astype(bf16)
    n1g, n2g, nfg = norm1_g[0].reshape(1, d), norm2_g[0].reshape(1, d), normf_g.reshape(1, d)
    nstate = S5_GROUPS * S5_STATE

    z, xbc, u5, dt = _inproj(x_prompt, mod_p, n1g, w_in_p, tm=512, per_row=False)
    y_ssd, ssd_p, conv_p = _ssd_prompt(xbc, z, dt, cw, cb, dtb, alog, dexp, ng)
    zeros_st = jnp.zeros((bp, nstate), f32)
    y_s5, re_p, im_p = _s5(u5, zeros_st, zeros_st, ar, ai, bblk, cblk, dsk, wglu, bglu,
                           tl=128, batch_major=True)
    y_prompt = _ffn(x_prompt, y_ssd, y_s5, mod_p, modf_p, n2g, nfg, wo, wg, wu, wd, tm=512, per_row=False)

    rows_s = dseq * bs
    xs_t = x_sample.transpose(1, 0, 2).reshape(1, rows_s, d)
    steps = lambda a: a.reshape(dseq, bs, a.shape[-1])
    flat = lambda a: a.reshape(1, rows_s, a.shape[-1])
    z, xbc, u5, dt = _inproj(xs_t, mod_s, n1g, w_in_p, tm=rows_s, per_row=True)
    y_ssd, ssd_s, conv_s = _ssd_sample(steps(xbc), steps(z), steps(dt), state_conv[0].transpose(1, 0, 2),
                                       state_ssd[0], cw, cb, dtb, alog, dexp, ng)
    y_s5, re_s, im_s = _s5(steps(u5), state_s5_re[0].reshape(bs, nstate), state_s5_im[0].reshape(bs, nstate),
                           ar, ai, bblk, cblk, dsk, wglu, bglu, tl=dseq, batch_major=False)
    y_sample = _ffn(xs_t, flat(y_ssd), flat(y_s5), mod_s, modf_s, n2g, nfg, wo, wg, wu, wd,
                    tm=rows_s, per_row=True)
    y_sample = steps(y_sample).transpose(1, 0, 2)

    g5 = (S5_GROUPS, S5_STATE)
    return (y_prompt, y_sample,
            ssd_p[None], ssd_s[None],
            conv_p[None], conv_s.transpose(1, 0, 2)[None],
            re_p.reshape((1, bp) + g5), re_s.reshape((1, bs) + g5),
            im_p.reshape((1, bp) + g5), im_s.reshape((1, bs) + g5))
```

```python
import functools

import jax
import jax.numpy as jnp
from jax import lax
from jax.experimental import pallas as pl
from jax.experimental.pallas import tpu as pltpu

f32 = jnp.float32
bf16 = jnp.bfloat16

D_MODEL = 1024
D_SSD = 512
HEAD_DIM = 64
HEADS = 8
GROUPS = 2
HEADS_PER_GROUP = HEADS // GROUPS
STATE = 128
CONV_W = 4
CONV_DIM = D_SSD + 2 * GROUPS * STATE
D_S5 = 512
S5_CH = 16
S5_GROUPS = 32
S5_STATE = 64
D_FF = 2816
N_ADA = 6
EPS = 1e-6

LANES = 128
SUBLANES = 8
SSD_CHUNK = 128
S5_SLABS = D_S5 // LANES
S5_SLAB_STATE = (S5_GROUPS // S5_SLABS) * S5_STATE
DT_PAD = LANES
IN_PROJ_PACKED = D_SSD + CONV_DIM + D_S5 + DT_PAD
VMEM_LIMIT = 56 * 1024 * 1024

NT_DIMS = (((1,), (1,)), ((), ()))
TN_DIMS = (((0,), (0,)), ((), ()))


def _silu(x):
    return x * jax.nn.sigmoid(x)


def _by_row(fn, v, *ms):
    r, d = v.shape
    m_rows = ms[0].shape[0]
    if m_rows in (1, r):
        return fn(v, *ms)
    out = fn(v.reshape(r // m_rows, m_rows, d), *[m[None] for m in ms])
    return out.reshape(r, d)


def _rms_mod(x, g, sc, sh):
    y = x * lax.rsqrt(jnp.mean(x * x, axis=-1, keepdims=True) + EPS)
    return _by_row(lambda v, s, t: v * (1.0 + s) + t, y * g, sc, sh)


def _mod_row(mod_ref, i, per_row):
    return mod_ref[i] if per_row else mod_ref[0, i:i + 1, :]


def _const_spec(shape):
    nd = len(shape)
    return pl.BlockSpec(shape, lambda *_: (0,) * nd)


def _params(*sem):
    return pltpu.CompilerParams(dimension_semantics=sem, vmem_limit_bytes=VMEM_LIMIT)


def _ada_kernel(c_ref, w_ref, b_ref, o_ref):
    s = _silu(c_ref[...]).astype(bf16)
    o_ref[...] = jnp.dot(s, w_ref[...].astype(bf16), preferred_element_type=f32) + b_ref[...]


def _ada_mod(c, w, b, tn=512):
    m, k = c.shape
    n = w.shape[1]
    return pl.pallas_call(
        _ada_kernel,
        grid=(n // tn,),
        in_specs=[pl.BlockSpec((m, k), lambda j: (0, 0)),
                  pl.BlockSpec((k, tn), lambda j: (0, j)),
                  pl.BlockSpec((1, tn), lambda j: (0, j))],
        out_specs=pl.BlockSpec((m, tn), lambda j: (0, j)),
        out_shape=jax.ShapeDtypeStruct((m, n), f32),
        compiler_params=_params("parallel"),
        name="ada_mod",
    )(c, w, b.reshape(1, n))


def _inproj_kernel(x_ref, mod_ref, g_ref, w_ref, z_ref, xbc_ref, u5_ref, dt_ref, *, per_row):
    sh = _mod_row(mod_ref, 0, per_row)
    sc = _mod_row(mod_ref, 1, per_row)
    u = _rms_mod(x_ref[0], g_ref[...], sc, sh).astype(bf16)
    o = 0
    for ref, width in ((z_ref, D_SSD), (xbc_ref, CONV_DIM), (u5_ref, D_S5), (dt_ref, DT_PAD)):
        ref[0] = jnp.dot(u, w_ref[:, o:o + width], preferred_element_type=f32)
        o += width


def _inproj(x, mod, g, w, *, tm, per_row):
    nb, rows, d = x.shape
    nt = rows // tm
    if per_row:
        mod_spec = _const_spec(mod.shape)
    else:
        mod_spec = pl.BlockSpec((1, N_ADA, d), lambda i, j: (i, 0, 0))
    widths = (D_SSD, CONV_DIM, D_S5, DT_PAD)
    return pl.pallas_call(
        functools.partial(_inproj_kernel, per_row=per_row),
        grid=(nb, nt),
        in_specs=[pl.BlockSpec((1, tm, d), lambda i, j: (i, j, 0)),
                  mod_spec,
                  _const_spec((1, d)),
                  _const_spec((d, IN_PROJ_PACKED))],
        out_specs=[pl.BlockSpec((1, tm, wd), lambda i, j: (i, j, 0)) for wd in widths],
        out_shape=[jax.ShapeDtypeStruct((nb, rows, wd), f32) for wd in widths],
        compiler_params=_params("parallel", "parallel"),
        name="in_proj",
    )(x, mod, g, w)


def _split3(x):
    hi = x.astype(bf16)
    r1 = x - hi.astype(f32)
    mid = r1.astype(bf16)
    lo = (r1 - mid.astype(f32)).astype(bf16)
    return hi, mid, lo


def _dot_sel_lhs(sel16, x):
    return sum(jnp.dot(sel16, p, preferred_element_type=f32) for p in _split3(x))


def _dot_sel_rhs(x, sel16):
    return sum(jnp.dot(p, sel16, preferred_element_type=f32) for p in _split3(x))


def _gated_group_norm(y, z, ng):
    y = y * _silu(z)
    gw = D_SSD // GROUPS
    parts = []
    for g in range(GROUPS):
        yg = y[:, g * gw:(g + 1) * gw]
        parts.append(yg * lax.rsqrt(jnp.mean(yg * yg, axis=-1, keepdims=True) + EPS))
    return jnp.concatenate(parts, axis=-1) * ng


def _ssd_prompt_kernel(xbc_ref, z_ref, dt_ref, cw_ref, cb_ref, dtb_ref, alog_ref, dexp_ref, ng_ref,
                       y_ref, st_ref, cn_ref, ext_ref, act_ref, h_ref, *, cps):
    T = SSD_CHUNK
    gw = HEADS_PER_GROUP * HEAD_DIM

    @pl.when(pl.program_id(1) == 0)
    def _():
        ext_ref[0:SUBLANES, :] = jnp.zeros((SUBLANES, CONV_DIM), f32)
        h_ref[...] = jnp.zeros_like(h_ref)

    row = lax.broadcasted_iota(jnp.int32, (T, T), 0)
    col = lax.broadcasted_iota(jnp.int32, (T, T), 1)
    tri = row >= col
    tri16 = jnp.where(tri, 1.0, 0.0).astype(bf16)
    low_half = lax.broadcasted_iota(jnp.int32, (T, LANES), 1) < HEAD_DIM
    a_neg = -jnp.exp(alog_ref[...])

    R = cps * T
    ext_ref[SUBLANES:SUBLANES + R, :] = xbc_ref[0]
    for ci in range(cps):
        ext = ext_ref[ci * T:ci * T + SUBLANES + T, :]
        conv = cb_ref[...] + ext[SUBLANES:, :] * cw_ref[CONV_W - 1:CONV_W, :]
        for m in range(1, CONV_W):
            k = CONV_W - 1 - m
            conv = conv + pltpu.roll(ext, m, 0)[SUBLANES:, :] * cw_ref[k:k + 1, :]
        act_ref[ci * T:(ci + 1) * T, :] = _silu(conv)
    cn_ref[0] = ext_ref[SUBLANES + R - (CONV_W - 1):SUBLANES + R, :]
    ext_ref[0:SUBLANES, :] = ext_ref[R:R + SUBLANES, :]

    lane_head = lax.broadcasted_iota(jnp.int32, (DT_PAD, D_SSD), 1) // HEAD_DIM
    sel16 = jnp.where(lax.broadcasted_iota(jnp.int32, (DT_PAD, D_SSD), 0) == lane_head, 1.0, 0.0).astype(bf16)

    def spread(q):
        return _dot_sel_rhs(q, sel16)

    def state_free(ci):
        r0 = ci * T
        act = act_ref[r0:r0 + T, :]
        xs = act[:, :D_SSD]
        bm = act[:, D_SSD:D_SSD + GROUPS * STATE]
        cm = act[:, D_SSD + GROUPS * STATE:]
        dtv = jax.nn.softplus(dt_ref[0, r0:r0 + T, :] + dtb_ref[...])
        a_cs = _dot_sel_lhs(tri16, dtv * a_neg)
        bg16s = [bm[:, g * STATE:(g + 1) * STATE].astype(bf16) for g in range(GROUPS)]
        cg16s = [cm[:, g * STATE:(g + 1) * STATE].astype(bf16) for g in range(GROUPS)]
        gmats = [lax.dot_general(cg16s[g], bg16s[g], NT_DIMS, preferred_element_type=f32)
                 for g in range(GROUPS)]
        yield
        a_last = a_cs[T - 1:T, :]
        a_cs_t = a_cs.T
        rep = spread(jnp.concatenate([dtv, jnp.exp(a_last - a_cs), jnp.exp(a_cs)], axis=0))
        yield
        x = xs * rep[:T]
        x16 = x.astype(bf16)
        xd16 = (x * rep[T:2 * T]).astype(bf16)
        e_cs = rep[2 * T:]
        yield
        y_part = []
        for g in range(GROUPS):
            for pr in range(HEADS_PER_GROUP // 2):
                s_pair = []
                for q in range(2):
                    h = g * HEADS_PER_GROUP + 2 * pr + q
                    seg = a_cs[:, h:h + 1] - a_cs_t[h:h + 1, :]
                    lmat = jnp.exp(jnp.where(tri, seg, -jnp.inf))
                    s_pair.append((gmats[g] * lmat).astype(bf16))
                lanes = slice(g * gw + pr * LANES, g * gw + (pr + 1) * LANES)
                both = jnp.dot(jnp.concatenate(s_pair, axis=0), x16[:, lanes], preferred_element_type=f32)
                yield
                y_part.append(jnp.where(low_half, both[:T], both[T:]))
        y_free = jnp.concatenate(y_part, axis=-1) + xs * dexp_ref[...]
        return y_free, e_cs, jnp.exp(a_last), xd16, bg16s, cg16s

    def state_step(ci, y_free, e_cs, e_last, xd16, bg16s, cg16s):
        r0 = ci * T
        y_offs = []
        for g in range(GROUPS):
            hp = h_ref[g]
            y_offs.append(lax.dot_general(cg16s[g], hp.astype(bf16), NT_DIMS, preferred_element_type=f32))
            upd = lax.dot_general(xd16[:, g * gw:(g + 1) * gw], bg16s[g], TN_DIMS, preferred_element_type=f32)
            for hh in range(HEADS_PER_GROUP):
                h = g * HEADS_PER_GROUP + hh
                rows = slice(hh * HEAD_DIM, (hh + 1) * HEAD_DIM)
                h_ref[g, rows, :] = e_last[:, h:h + 1] * hp[rows] + upd[rows]
        y = y_free + jnp.concatenate(y_offs, axis=-1) * e_cs
        y_ref[0, r0:r0 + T, :] = _gated_group_norm(y, z_ref[0, r0:r0 + T, :], ng_ref[...])

    gens = [state_free(ci) for ci in range(cps)]
    free = [None] * cps
    while any(f is None for f in free):
        for ci in range(cps):
            if free[ci] is None:
                try:
                    next(gens[ci])
                except StopIteration as done:
                    free[ci] = done.value
    for ci in range(cps):
        state_step(ci, *free[ci])

    @pl.when(pl.program_id(1) == pl.num_programs(1) - 1)
    def _():
        st_ref[0] = h_ref[...].reshape(HEADS, HEAD_DIM, STATE)


def _ssd_prompt(xbc, z, dt, cw, cb, dtb, alog, dexp, ng, *, cps=8):
    nb, L, _ = xbc.shape
    T = SSD_CHUNK
    rows = cps * T
    blk = lambda wd: pl.BlockSpec((1, rows, wd), lambda b, c: (b, c, 0))
    return pl.pallas_call(
        functools.partial(_ssd_prompt_kernel, cps=cps),
        grid=(nb, L // rows),
        in_specs=[blk(CONV_DIM), blk(D_SSD), blk(DT_PAD),
                  _const_spec((CONV_W, CONV_DIM)), _const_spec((1, CONV_DIM)),
                  _const_spec((1, DT_PAD)), _const_spec((1, DT_PAD)),
                  _const_spec((1, D_SSD)), _const_spec((1, D_SSD))],
        out_specs=[blk(D_SSD),
                   pl.BlockSpec((1, HEADS, HEAD_DIM, STATE), lambda b, c: (b, 0, 0, 0)),
                   pl.BlockSpec((1, CONV_W - 1, CONV_DIM), lambda b, c: (b, 0, 0))],
        out_shape=[jax.ShapeDtypeStruct((nb, L, D_SSD), f32),
                   jax.ShapeDtypeStruct((nb, HEADS, HEAD_DIM, STATE), f32),
                   jax.ShapeDtypeStruct((nb, CONV_W - 1, CONV_DIM), f32)],
        scratch_shapes=[pltpu.VMEM((SUBLANES + rows, CONV_DIM), f32),
                        pltpu.VMEM((rows, CONV_DIM), f32),
                        pltpu.VMEM((GROUPS, HEADS_PER_GROUP * HEAD_DIM, STATE), f32)],
        compiler_params=_params("parallel", "arbitrary"),
        name="ssd_prompt",
    )(xbc, z, dt, cw, cb, dtb, alog, dexp, ng)


def _ssd_sample_kernel(xbc_ref, z_ref, dt_ref, cbuf_ref, st_ref, cw_ref, cb_ref, dtb_ref, alog_ref,
                       dexp_ref, ng_ref, y_ref, stn_ref, cn_ref, dtot_ref, *, L, nb):
    gw = HEADS_PER_GROUP * HEAD_DIM
    full = [cbuf_ref[j] for j in range(CONV_W - 1)] + [xbc_ref[t] for t in range(L)]
    for j in range(CONV_W - 1):
        cn_ref[j] = full[L + j]
    a_neg = -jnp.exp(alog_ref[...])
    xs, bm, cm, dtv, acs = [], [], [], [], []
    run = None
    for t in range(L):
        conv = cb_ref[...]
        for k in range(CONV_W):
            conv = conv + full[t + k] * cw_ref[k:k + 1, :]
        act = _silu(conv)
        xs.append(act[:, :D_SSD])
        bm.append(act[:, D_SSD:D_SSD + GROUPS * STATE])
        cm.append(act[:, D_SSD + GROUPS * STATE:])
        d = jax.nn.softplus(dt_ref[t] + dtb_ref[...])
        dtv.append(d)
        run = d * a_neg if run is None else run + d * a_neg
        acs.append(run)
    a_tot = acs[L - 1]
    dtot_ref[...] = jnp.exp(a_tot)

    lane_head = lax.broadcasted_iota(jnp.int32, (DT_PAD, D_SSD), 1) // HEAD_DIM
    sel16 = jnp.where(lax.broadcasted_iota(jnp.int32, (DT_PAD, D_SSD), 0) == lane_head, 1.0, 0.0).astype(bf16)
    pairs = [(t, s) for t in range(L) for s in range(t)]
    factors = (dtv + [jnp.exp(a_tot - acs[t]) for t in range(L)] + [jnp.exp(acs[t]) for t in range(L)]
               + [jnp.exp(acs[t] - acs[s]) for t, s in pairs])
    rep = _dot_sel_rhs(jnp.concatenate(factors, axis=0), sel16)
    piece = lambda i: rep[i * nb:(i + 1) * nb]
    x = [xs[t] * piece(t) for t in range(L)]
    xd_stack = jnp.concatenate([x[t] * piece(L + t) for t in range(L)], axis=0)
    e_cs = [piece(2 * L + t) for t in range(L)]
    decay = {ts: piece(3 * L + i) for i, ts in enumerate(pairs)}

    in_group0 = lax.broadcasted_iota(jnp.int32, (nb, D_SSD), 1) < gw
    y_intra = []
    for t in range(L):
        acc = None
        for s in range(t + 1):
            cb_dot = [jnp.sum(cm[t][:, g * STATE:(g + 1) * STATE] * bm[s][:, g * STATE:(g + 1) * STATE],
                              axis=-1, keepdims=True) for g in range(GROUPS)]
            w = jnp.where(in_group0, cb_dot[0], cb_dot[1])
            term = w * x[s] if s == t else w * decay[(t, s)] * x[s]
            acc = term if acc is None else acc + term
        y_intra.append(acc)

    c_stack = [jnp.concatenate([cm[t][:, g * STATE:(g + 1) * STATE] for t in range(L)], axis=0).astype(bf16)
               for g in range(GROUPS)]
    b_stack = [jnp.concatenate([bm[t][:, g * STATE:(g + 1) * STATE] for t in range(L)], axis=0).astype(bf16)
               for g in range(GROUPS)]
    seq_of_row = lax.broadcasted_iota(jnp.int32, (L * nb, gw), 0) & (nb - 1)

    def per_seq(b, yoff):
        mine = seq_of_row == b
        drow = dtot_ref[pl.ds(b, 1), :]
        out = []
        for g in range(GROUPS):
            h0 = st_ref[b, g]
            r = lax.dot_general(c_stack[g], h0.astype(bf16), NT_DIMS, preferred_element_type=f32)
            out.append(yoff[g] + jnp.where(mine, r, 0.0))
            xm = jnp.where(mine, xd_stack[:, g * gw:(g + 1) * gw], 0.0).astype(bf16)
            upd = lax.dot_general(xm, b_stack[g], TN_DIMS, preferred_element_type=f32)
            for hh in range(HEADS_PER_GROUP):
                h = g * HEADS_PER_GROUP + hh
                rows = slice(hh * HEAD_DIM, (hh + 1) * HEAD_DIM)
                stn_ref[b, g, rows, :] = drow[:, h:h + 1] * h0[rows] + upd[rows]
        return tuple(out)

    yoff = lax.fori_loop(0, nb, per_seq, tuple(jnp.zeros((L * nb, gw), f32) for _ in range(GROUPS)),
                         unroll=True)

    for t in range(L):
        y_off = jnp.concatenate([yoff[g][t * nb:(t + 1) * nb] for g in range(GROUPS)], axis=-1)
        y = y_intra[t] + y_off * e_cs[t] + xs[t] * dexp_ref[...]
        y_ref[t] = _gated_group_norm(y, z_ref[t], ng_ref[...])


def _ssd_sample(xbc, z, dt, cbuf, st, cw, cb, dtb, alog, dexp, ng, *, nb=8):
    L, B, _ = xbc.shape
    assert nb & (nb - 1) == 0 and B % nb == 0
    tblk = lambda n, wd: pl.BlockSpec((n, nb, wd), lambda i: (0, i, 0))
    gw = HEADS_PER_GROUP * HEAD_DIM
    st = st.reshape(B, GROUPS, gw, STATE)
    st_spec = pl.BlockSpec((nb, GROUPS, gw, STATE), lambda i: (i, 0, 0, 0))
    return pl.pallas_call(
        functools.partial(_ssd_sample_kernel, L=L, nb=nb),
        grid=(B // nb,),
        in_specs=[tblk(L, CONV_DIM), tblk(L, D_SSD), tblk(L, DT_PAD), tblk(CONV_W - 1, CONV_DIM), st_spec,
                  _const_spec((CONV_W, CONV_DIM)), _const_spec((1, CONV_DIM)),
                  _const_spec((1, DT_PAD)), _const_spec((1, DT_PAD)),
                  _const_spec((1, D_SSD)), _const_spec((1, D_SSD))],
        out_specs=[tblk(L, D_SSD), st_spec, tblk(CONV_W - 1, CONV_DIM)],
        out_shape=[jax.ShapeDtypeStruct((L, B, D_SSD), f32),
                   jax.ShapeDtypeStruct((B, GROUPS, gw, STATE), f32),
                   jax.ShapeDtypeStruct((CONV_W - 1, B, CONV_DIM), f32)],
        scratch_shapes=[pltpu.VMEM((nb, DT_PAD), f32)],
        compiler_params=_params("parallel"),
        name="ssd_sample",
    )(xbc, z, dt, cbuf, st, cw, cb, dtb, alog, dexp, ng)


def _s5_param_kernel(lr_ref, li_ref, ls_ref, br_ref, bi_ref, abr_ref, abi_ref, bbr_ref, bbi_ref):
    lr, li = lr_ref[...], li_ref[...]
    step = jnp.exp(ls_ref[...])
    mag = jnp.exp(lr * step)
    abr = mag * jnp.cos(li * step)
    abi = mag * jnp.sin(li * step)
    nr, ni = abr - 1.0, abi
    den = lr * lr + li * li
    fr = (nr * lr + ni * li) / den
    fi = (ni * lr - nr * li) / den
    abr_ref[...] = abr
    abi_ref[...] = abi
    br, bi = br_ref[...], bi_ref[...]
    bbr_ref[...] = fr * br - fi * bi
    bbi_ref[...] = fr * bi + fi * br


def _s5_params(a_re, a_im, log_step, b_re_t, b_im_t):
    g, p = a_re.shape
    vec = jax.ShapeDtypeStruct((g, 1, p), f32)
    mat = jax.ShapeDtypeStruct(b_re_t.shape, f32)
    return pl.pallas_call(_s5_param_kernel, out_shape=[vec, vec, mat, mat], name="s5_params")(
        a_re.reshape(g, 1, p), a_im.reshape(g, 1, p), log_step.reshape(g, 1, 1), b_re_t, b_im_t)


def _gelu(x):
    return 0.5 * x * (1.0 + lax.erf(x * (2.0 ** -0.5)))


def _interleave(*gens):
    live = list(gens)
    while live:
        for g in list(live):
            try:
                next(g)
            except StopIteration:
                live.remove(g)


def _s5_slabs_interleaved(ut_ref, hbuf_ref, hst_ref, g_ref, ar_ref, ai_ref, bblk_ref, cblk_ref, dsk_ref, tl,
                          row_block=256, n_stages=8):
    ns = S5_SLAB_STATE
    nb = SUBLANES
    rows = nb * tl

    def lanes(s):
        return slice(s * LANES, (s + 1) * LANES)

    def bu_stage(slabs):
        for s in slabs:
            for r in range(0, rows, row_block):
                us = ut_ref[r:r + row_block, lanes(s)]
                hbuf_ref[s, nb + r:nb + r + row_block, :] = jnp.dot(us.astype(bf16), bblk_ref[s],
                                                                    preferred_element_type=f32)
                yield

    def scan_stage(slabs):
        carry = [(hst_ref[s, :, :ns], hst_ref[s, :, ns:]) for s in slabs]
        coef = [(ar_ref[s], ai_ref[s]) for s in slabs]
        for t in range(tl):
            r = nb + nb * t
            for i, s in enumerate(slabs):
                (pr, pi), (ar, ai) = carry[i], coef[i]
                nr = ar * pr - ai * pi + hbuf_ref[s, r:r + nb, :ns]
                ni = ar * pi + ai * pr + hbuf_ref[s, r:r + nb, ns:]
                hbuf_ref[s, r:r + nb, :ns] = nr
                hbuf_ref[s, r:r + nb, ns:] = ni
                carry[i] = (nr, ni)
            if (t + 1) % (tl // n_stages) == 0:
                yield
        for i, s in enumerate(slabs):
            hst_ref[s, :, :ns] = carry[i][0]
            hst_ref[s, :, ns:] = carry[i][1]

    def y_stage(slabs):
        for s in slabs:
            for r in range(0, rows, row_block):
                h16 = hbuf_ref[s, nb + r:nb + r + row_block, :].astype(bf16)
                ys = jnp.dot(h16, cblk_ref[s], preferred_element_type=f32)
                ys = ys + dsk_ref[:, lanes(s)] * ut_ref[r:r + row_block, lanes(s)]
                g_ref[r:r + row_block, lanes(s)] = _gelu(ys)
                yield

    half = S5_SLABS // 2
    first, second = tuple(range(half)), tuple(range(half, S5_SLABS))
    _interleave(bu_stage(first))
    _interleave(scan_stage(first), bu_stage(second))
    _interleave(scan_stage(second), y_stage(first))
    _interleave(y_stage(second))


def _s5_scan_vmem(hbuf_ref, hst_ref, ar_ref, ai_ref, s, nb, rows):
    ns = S5_SLAB_STATE
    hbuf_ref[s, 0:nb, :] = hst_ref[s]
    ar, ai = ar_ref[s], ai_ref[s]

    def body(i, carry):
        j = pl.multiple_of(nb + i * SUBLANES, SUBLANES)
        prev = hbuf_ref[s, pl.ds(j - nb, SUBLANES), :]
        cur = hbuf_ref[s, pl.ds(j, SUBLANES), :]
        pr, pi = prev[:, :ns], prev[:, ns:]
        hbuf_ref[s, pl.ds(j, SUBLANES), :ns] = ar * pr - ai * pi + cur[:, :ns]
        hbuf_ref[s, pl.ds(j, SUBLANES), ns:] = ar * pi + ai * pr + cur[:, ns:]
        return carry

    lax.fori_loop(0, rows // SUBLANES, body, 0)
    hst_ref[s] = hbuf_ref[s, rows:rows + nb, :]


def _s5_kernel(u_ref, re0_ref, im0_ref, ar_ref, ai_ref, bblk_ref, cblk_ref, dsk_ref, wglu_ref, bglu_ref,
               y_ref, ren_ref, imn_ref, ut_ref, hbuf_ref, hst_ref, g_ref, *, nb, tl, batch_major):
    step = pl.program_id(0)
    rows = nb * tl
    ns = S5_SLAB_STATE

    @pl.when(step == 0)
    def _():
        for s in range(S5_SLABS):
            hst_ref[s, :, :ns] = re0_ref[:, s * ns:(s + 1) * ns]
            hst_ref[s, :, ns:] = im0_ref[:, s * ns:(s + 1) * ns]

    if batch_major:
        ut_ref[...] = jnp.swapaxes(u_ref[...], 0, 1).reshape(rows, D_S5)
    else:
        ut_ref[...] = u_ref[...].reshape(rows, D_S5)

    if nb == SUBLANES:
        _s5_slabs_interleaved(ut_ref, hbuf_ref, hst_ref, g_ref, ar_ref, ai_ref, bblk_ref, cblk_ref, dsk_ref, tl)
    else:
        for s in range(S5_SLABS):
            sl = slice(s * LANES, (s + 1) * LANES)
            hbuf_ref[s, nb:nb + rows, :] = jnp.dot(ut_ref[:, sl].astype(bf16), bblk_ref[s],
                                                   preferred_element_type=f32)
            _s5_scan_vmem(hbuf_ref, hst_ref, ar_ref, ai_ref, s, nb, rows)
            ys = jnp.dot(hbuf_ref[s, nb:nb + rows, :].astype(bf16), cblk_ref[s], preferred_element_type=f32)
            g_ref[:, sl] = _gelu(ys + dsk_ref[:, sl] * ut_ref[:, sl])

    g = g_ref[...]
    gate = jnp.dot(g.astype(bf16), wglu_ref[...], preferred_element_type=f32) + bglu_ref[...]
    out = g * jax.nn.sigmoid(gate)
    if batch_major:
        y_ref[...] = jnp.swapaxes(out.reshape(tl, nb, D_S5), 0, 1)
    else:
        y_ref[...] = out.reshape(tl, nb, D_S5)

    @pl.when(step == pl.num_programs(0) - 1)
    def _():
        for s in range(S5_SLABS):
            ren_ref[:, s * ns:(s + 1) * ns] = hst_ref[s, :, :ns]
            imn_ref[:, s * ns:(s + 1) * ns] = hst_ref[s, :, ns:]


def _s5(u, re0, im0, ar, ai, bblk, cblk, dsk, wglu, bglu, *, tl, batch_major):
    if batch_major:
        nb, L, _ = u.shape
        u_spec = pl.BlockSpec((nb, tl, D_S5), lambda i: (0, i, 0))
    else:
        L, nb, _ = u.shape
        u_spec = pl.BlockSpec((tl, nb, D_S5), lambda i: (i, 0, 0))
    rows = nb * tl
    nstate = S5_GROUPS * S5_STATE
    st_spec = _const_spec((nb, nstate))
    return pl.pallas_call(
        functools.partial(_s5_kernel, nb=nb, tl=tl, batch_major=batch_major),
        grid=(L // tl,),
        in_specs=[u_spec, st_spec, st_spec,
                  _const_spec((S5_SLABS, SUBLANES, S5_SLAB_STATE)),
                  _const_spec((S5_SLABS, SUBLANES, S5_SLAB_STATE)),
                  _const_spec((S5_SLABS, LANES, 2 * S5_SLAB_STATE)),
                  _const_spec((S5_SLABS, 2 * S5_SLAB_STATE, LANES)),
                  _const_spec((1, D_S5)), _const_spec((D_S5, D_S5)), _const_spec((1, D_S5))],
        out_specs=[u_spec, st_spec, st_spec],
        out_shape=[jax.ShapeDtypeStruct(u.shape, f32),
                   jax.ShapeDtypeStruct((nb, nstate), f32),
                   jax.ShapeDtypeStruct((nb, nstate), f32)],
        scratch_shapes=[pltpu.VMEM((rows, D_S5), f32),
                        pltpu.VMEM((S5_SLABS, nb + rows, 2 * S5_SLAB_STATE), f32),
                        pltpu.VMEM((S5_SLABS, nb, 2 * S5_SLAB_STATE), f32),
                        pltpu.VMEM((rows, D_S5), f32)],
        compiler_params=_params("arbitrary"),
        name="s5_mixer",
    )(u, re0, im0, ar, ai, bblk, cblk, dsk, wglu, bglu)


def _ffn_kernel(x_ref, ys_ref, y5_ref, mod_ref, modf_ref, n2g_ref, nfg_ref, wo_ref, wg_ref, wu_ref, wd_ref,
                o_ref, *, per_row, ff_chunk):
    g1 = _mod_row(mod_ref, 2, per_row)
    sh2 = _mod_row(mod_ref, 3, per_row)
    sc2 = _mod_row(mod_ref, 4, per_row)
    g2 = _mod_row(mod_ref, 5, per_row)
    shf = _mod_row(modf_ref, 0, per_row)
    scf = _mod_row(modf_ref, 1, per_row)
    att = jnp.dot(ys_ref[0].astype(bf16), wo_ref[:D_SSD, :], preferred_element_type=f32)
    att = att + jnp.dot(y5_ref[0].astype(bf16), wo_ref[D_SSD:, :], preferred_element_type=f32)
    x1 = x_ref[0] + _by_row(lambda v, s: v * s, att, g1)
    v = _rms_mod(x1, n2g_ref[...], sc2, sh2).astype(bf16)
    ff = None
    for o in range(0, D_FF, ff_chunk):
        gate = jnp.dot(v, wg_ref[:, o:o + ff_chunk], preferred_element_type=f32)
        up = jnp.dot(v, wu_ref[:, o:o + ff_chunk], preferred_element_type=f32)
        hid = (_silu(gate) * up).astype(bf16)
        part = jnp.dot(hid, wd_ref[o:o + ff_chunk, :], preferred_element_type=f32)
        ff = part if ff is None else ff + part
    x2 = x1 + _by_row(lambda v, s: v * s, ff, g2)
    o_ref[0] = _rms_mod(x2, nfg_ref[...], scf, shf)


def _ffn(x, ys, y5, mod, modf, n2g, nfg, wo, wg, wu, wd, *, tm, per_row, ff_chunk=256):
    nb, rows, d = x.shape
    nt = rows // tm
    if per_row:
        mod_spec = _const_spec(mod.shape)
        modf_spec = _const_spec(modf.shape)
    else:
        mod_spec = pl.BlockSpec((1, N_ADA, d), lambda i, j: (i, 0, 0))
        modf_spec = pl.BlockSpec((1, 2, d), lambda i, j: (i, 0, 0))
    blk = lambda wd_: pl.BlockSpec((1, tm, wd_), lambda i, j: (i, j, 0))
    single = dict(pipeline_mode=pl.Buffered(1))
    wspec = lambda shape: pl.BlockSpec(shape, lambda i, j: (0, 0), **single)
    return pl.pallas_call(
        functools.partial(_ffn_kernel, per_row=per_row, ff_chunk=ff_chunk),
        grid=(nb, nt),
        in_specs=[blk(d), blk(D_SSD), blk(D_S5), mod_spec, modf_spec,
                  _const_spec((1, d)), _const_spec((1, d)),
                  wspec((d, d)), wspec((d, D_FF)), wspec((d, D_FF)), wspec((D_FF, d))],
        out_specs=blk(d),
        out_shape=jax.ShapeDtypeStruct((nb, rows, d), f32),
        compiler_params=_params("parallel", "parallel"),
        name="out_ffn",
    )(x, ys, y5, mod, modf, n2g, nfg, wo, wg, wu, wd)


def _block_diag(m):
    s, k, a, b = m.shape
    eye = jnp.eye(k, dtype=m.dtype)
    return (m[:, :, :, None, :] * eye[None, :, None, :, None]).reshape(s, k * a, k * b)


def kernel(x_prompt, x_sample, c_prompt, c_sample, state_ssd, state_conv, state_s5_re, state_s5_im, w_ada, b_ada, norm1_g, w_in, conv_w, conv_b, ssd_dt_bias, ssd_A_log, ssd_D, ssd_norm_g, s5_A_re, s5_A_im, s5_log_step, s5_B_re, s5_B_im, s5_C_re, s5_C_im, s5_D, w_glu, b_glu, w_out, norm2_g, w_ffn_gate, w_ffn_up, w_ffn_down, w_ada_f, b_ada_f, normf_g):
    assert w_ada.shape[0] == 1, "single-layer stack"
    bp, seq, d = x_prompt.shape
    bs, dseq, _ = x_sample.shape

    c_all = jnp.concatenate([c_prompt, c_sample], axis=0)
    mod = _ada_mod(c_all, w_ada[0], b_ada[0])
    modf = _ada_mod(c_all, w_ada_f, b_ada_f)
    mod_p = mod[:bp].reshape(bp, N_ADA, d)
    modf_p = modf[:bp].reshape(bp, 2, d)
    mod_s = mod[bp:].reshape(bs, N_ADA, d).transpose(1, 0, 2)
    modf_s = modf[bp:].reshape(bs, 2, d).transpose(1, 0, 2)

    o1, o2, o3 = D_SSD, D_SSD + CONV_DIM, D_SSD + CONV_DIM + HEADS
    wi = w_in[0]
    w_in_p = jnp.concatenate(
        [wi[:, :o2], wi[:, o3:], wi[:, o2:o3], jnp.zeros((d, DT_PAD - HEADS), f32)], axis=1).astype(bf16)
    pad_h = lambda v: jnp.concatenate([v, jnp.zeros((DT_PAD - HEADS,), f32)]).reshape(1, DT_PAD)
    dtb = pad_h(ssd_dt_bias[0])
    alog = pad_h(ssd_A_log[0])
    dexp = jnp.repeat(ssd_D[0], HEAD_DIM).reshape(1, D_SSD)
    ng = ssd_norm_g[0].reshape(1, D_SSD)
    cw, cb = conv_w[0], conv_b[0].reshape(1, CONV_DIM)

    abr, abi, bbr, bbi = _s5_params(s5_A_re[0], s5_A_im[0], s5_log_step[0],
                                    s5_B_re[0].transpose(0, 2, 1), s5_B_im[0].transpose(0, 2, 1))
    gps = S5_GROUPS // S5_SLABS
    tile_rows = lambda v: jnp.broadcast_to(v.reshape(S5_SLABS, 1, S5_SLAB_STATE),
                                           (S5_SLABS, SUBLANES, S5_SLAB_STATE))
    ar, ai = tile_rows(abr), tile_rows(abi)
    slab = lambda m: m.reshape((S5_SLABS, gps) + m.shape[1:])
    bblk = jnp.concatenate([_block_diag(slab(bbr)), _block_diag(slab(bbi))], axis=2).astype(bf16)
    cblk = jnp.concatenate([_block_diag(slab(s5_C_re[0].transpose(0, 2, 1))),
                            _block_diag(slab(-s5_C_im[0].transpose(0, 2, 1)))], axis=1).astype(bf16)
    dsk = s5_D[0].reshape(1, D_S5)
    wglu = w_glu[0].astype(bf16)
    bglu = b_glu[0].reshape(1, D_S5)
    wo = w_out[0].astype(bf16)
    wg, wu, wd = w_ffn_gate[0].astype(bf16), w_ffn_up[0].astype(bf16), w_ffn_down[0].astype(bf16)
    n1g, n2g, nfg = norm1_g[0].reshape(1, d), norm2_g[0].reshape(1, d), normf_g.reshape(1, d)
    nstate = S5_GROUPS * S5_STATE

    z, xbc, u5, dt = _inproj(x_prompt, mod_p, n1g, w_in_p, tm=512, per_row=False)
    y_ssd, ssd_p, conv_p = _ssd_prompt(xbc, z, dt, cw, cb, dtb, alog, dexp, ng)
    zeros_st = jnp.zeros((bp, nstate), f32)
    y_s5, re_p, im_p = _s5(u5, zeros_st, zeros_st, ar, ai, bblk, cblk, dsk, wglu, bglu,
                           tl=128, batch_major=True)
    y_prompt = _ffn(x_prompt, y_ssd, y_s5, mod_p, modf_p, n2g, nfg, wo, wg, wu, wd, tm=512, per_row=False)

    rows_s = dseq * bs
    xs_t = x_sample.transpose(1, 0, 2).reshape(1, rows_s, d)
    steps = lambda a: a.reshape(dseq, bs, a.shape[-1])
    flat = lambda a: a.reshape(1, rows_s, a.shape[-1])
    z, xbc, u5, dt = _inproj(xs_t, mod_s, n1g, w_in_p, tm=rows_s, per_row=True)
    y_ssd, ssd_s, conv_s = _ssd_sample(steps(xbc), steps(z), steps(dt), state_conv[0].transpose(1, 0, 2),
                                       state_ssd[0], cw, cb, dtb, alog, dexp, ng)
    y_s5, re_s, im_s = _s5(steps(u5), state_s5_re[0].reshape(bs, nstate), state_s5_im[0].reshape(bs, nstate),
                           ar, ai, bblk, cblk, dsk, wglu, bglu, tl=dseq, batch_major=False)
    y_sample = _ffn(xs_t, flat(y_ssd), flat(y_s5), mod_s, modf_s, n2g, nfg, wo, wg, wu, wd,
                    tm=rows_s, per_row=True)
    y_sample = steps(y_sample).transpose(1, 0, 2)

    g5 = (S5_GROUPS, S5_STATE)
    return (y_prompt, y_sample,
            ssd_p[None], ssd_s.reshape((1,) + state_ssd.shape[1:]),
            conv_p[None], conv_s.transpose(1, 0, 2)[None],
            re_p.reshape((1, bp) + g5), re_s.reshape((1, bs) + g5),
            im_p.reshape((1, bp) + g5), im_s.reshape((1, bs) + g5))
```

```python
import functools

import jax
import jax.numpy as jnp
from jax import lax
from jax.experimental import pallas as pl
from jax.experimental.pallas import tpu as pltpu

f32 = jnp.float32
bf16 = jnp.bfloat16

D_MODEL = 1024
D_SSD = 512
HEAD_DIM = 64
HEADS = 8
GROUPS = 2
HEADS_PER_GROUP = HEADS // GROUPS
STATE = 128
CONV_W = 4
CONV_DIM = D_SSD + 2 * GROUPS * STATE
D_S5 = 512
S5_CH = 16
S5_GROUPS = 32
S5_STATE = 64
D_FF = 2816
N_ADA = 6
EPS = 1e-6

LANES = 128
SUBLANES = 8
SSD_CHUNK = 128
S5_SLABS = D_S5 // LANES
S5_SLAB_STATE = (S5_GROUPS // S5_SLABS) * S5_STATE
DT_PAD = LANES
VMEM_LIMIT = 56 * 1024 * 1024

NT_DIMS = (((1,), (1,)), ((), ()))
TN_DIMS = (((0,), (0,)), ((), ()))


def _silu(x):
    return x * jax.nn.sigmoid(x)


def _by_row(fn, v, *ms):
    r, d = v.shape
    m_rows = ms[0].shape[0]
    if m_rows in (1, r):
        return fn(v, *ms)
    out = fn(v.reshape(r // m_rows, m_rows, d), *[m[None] for m in ms])
    return out.reshape(r, d)


def _rms_mod(x, g, sc, sh):
    y = x * lax.rsqrt(jnp.mean(x * x, axis=-1, keepdims=True) + EPS)
    return _by_row(lambda v, s, t: v * (1.0 + s) + t, y * g, sc, sh)


def _mod_row(mod_ref, i, per_row):
    return mod_ref[i] if per_row else mod_ref[i, pl.ds(pl.program_id(0), 1), :]


def _mod_spec(mod, n_seq, first_seq):
    assert first_seq % n_seq == 0
    return pl.BlockSpec((mod.shape[0], n_seq, mod.shape[2]), lambda *_: (0, first_seq // n_seq, 0))


def _const_spec(shape):
    nd = len(shape)
    return pl.BlockSpec(shape, lambda *_: (0,) * nd)


def _params(*sem):
    return pltpu.CompilerParams(dimension_semantics=sem, vmem_limit_bytes=VMEM_LIMIT)


def _ada_kernel(c_ref, w_ref, b_ref, o_ref):
    s = _silu(c_ref[...]).astype(bf16)
    o_ref[0] = jnp.dot(s, w_ref[...].astype(bf16), preferred_element_type=f32) + b_ref[...]


def _ada_mod(c, w, b):
    m, k = c.shape
    n = w.shape[1] // k
    return pl.pallas_call(
        _ada_kernel,
        grid=(n,),
        in_specs=[pl.BlockSpec((m, k), lambda j: (0, 0)),
                  pl.BlockSpec((k, k), lambda j: (0, j)),
                  pl.BlockSpec((1, k), lambda j: (0, j))],
        out_specs=pl.BlockSpec((1, m, k), lambda j: (j, 0, 0)),
        out_shape=jax.ShapeDtypeStruct((n, m, k), f32),
        compiler_params=_params("parallel"),
        name="ada_mod",
    )(c, w, b.reshape(1, n * k))


def _inproj_kernel(x_ref, mod_ref, g_ref, wzx_ref, wu5_ref, wdt_ref, z_ref, xbc_ref, u5_ref, dt_ref, *, per_row):
    sh = _mod_row(mod_ref, 0, per_row)
    sc = _mod_row(mod_ref, 1, per_row)
    u = _rms_mod(x_ref[0], g_ref[...], sc, sh).astype(bf16)
    proj = lambda w: jnp.dot(u, w.astype(bf16), preferred_element_type=f32)
    z_ref[0] = proj(wzx_ref[:, :D_SSD])
    xbc_ref[0] = proj(wzx_ref[:, D_SSD:])
    u5_ref[0] = proj(wu5_ref[...])
    dt_ref[0] = proj(wdt_ref[...])


def _inproj(x, mod, g, w_in, w_u5, w_dt, *, tm, per_row, n_seq, first_seq):
    nb, rows, d = x.shape
    nt = rows // tm
    mod_spec = _mod_spec(mod, n_seq, first_seq)
    widths = (D_SSD, CONV_DIM, D_S5, DT_PAD)
    wspec = lambda cols: pl.BlockSpec((d, cols), lambda i, j: (0, 0), pipeline_mode=pl.Buffered(1))
    return pl.pallas_call(
        functools.partial(_inproj_kernel, per_row=per_row),
        grid=(nb, nt),
        in_specs=[pl.BlockSpec((1, tm, d), lambda i, j: (i, j, 0)),
                  mod_spec,
                  _const_spec((1, d)),
                  wspec(D_SSD + CONV_DIM), wspec(D_S5), wspec(DT_PAD)],
        out_specs=[pl.BlockSpec((1, tm, wd), lambda i, j: (i, j, 0)) for wd in widths],
        out_shape=[jax.ShapeDtypeStruct((nb, rows, wd), f32) for wd in widths],
        compiler_params=_params("parallel", "parallel"),
        name="in_proj",
    )(x, mod, g, w_in, w_u5, w_dt)


def _split3(x):
    hi = x.astype(bf16)
    r1 = x - hi.astype(f32)
    mid = r1.astype(bf16)
    lo = (r1 - mid.astype(f32)).astype(bf16)
    return hi, mid, lo


def _dot_sel_lhs(sel16, x):
    return sum(jnp.dot(sel16, p, preferred_element_type=f32) for p in _split3(x))


def _dot_sel_rhs(x, sel16):
    return sum(jnp.dot(p, sel16, preferred_element_type=f32) for p in _split3(x))


def _gated_group_norm(y, z, ng):
    y = y * _silu(z)
    gw = D_SSD // GROUPS
    parts = []
    for g in range(GROUPS):
        yg = y[:, g * gw:(g + 1) * gw]
        parts.append(yg * lax.rsqrt(jnp.mean(yg * yg, axis=-1, keepdims=True) + EPS))
    return jnp.concatenate(parts, axis=-1) * ng


def _ssd_prompt_kernel(xbc_ref, z_ref, dt_ref, cw_ref, cb_ref, dtb_ref, alog_ref, dexp_ref, ng_ref,
                       y_ref, st_ref, cn_ref, ext_ref, act_ref, h_ref, *, cps):
    T = SSD_CHUNK
    gw = HEADS_PER_GROUP * HEAD_DIM

    @pl.when(pl.program_id(1) == 0)
    def _():
        ext_ref[0:SUBLANES, :] = jnp.zeros((SUBLANES, CONV_DIM), f32)
        h_ref[...] = jnp.zeros_like(h_ref)

    row = lax.broadcasted_iota(jnp.int32, (T, T), 0)
    col = lax.broadcasted_iota(jnp.int32, (T, T), 1)
    tri = row >= col
    tri16 = jnp.where(tri, 1.0, 0.0).astype(bf16)
    low_half = lax.broadcasted_iota(jnp.int32, (T, LANES), 1) < HEAD_DIM
    a_neg = -jnp.exp(alog_ref[...])

    R = cps * T
    ext_ref[SUBLANES:SUBLANES + R, :] = xbc_ref[0]
    for ci in range(cps):
        ext = ext_ref[ci * T:ci * T + SUBLANES + T, :]
        conv = cb_ref[...] + ext[SUBLANES:, :] * cw_ref[CONV_W - 1:CONV_W, :]
        for m in range(1, CONV_W):
            k = CONV_W - 1 - m
            conv = conv + pltpu.roll(ext, m, 0)[SUBLANES:, :] * cw_ref[k:k + 1, :]
        act_ref[ci * T:(ci + 1) * T, :] = _silu(conv)
    cn_ref[0] = ext_ref[SUBLANES + R - (CONV_W - 1):SUBLANES + R, :]
    ext_ref[0:SUBLANES, :] = ext_ref[R:R + SUBLANES, :]

    lane_head = lax.broadcasted_iota(jnp.int32, (DT_PAD, D_SSD), 1) // HEAD_DIM
    sel16 = jnp.where(lax.broadcasted_iota(jnp.int32, (DT_PAD, D_SSD), 0) == lane_head, 1.0, 0.0).astype(bf16)

    def spread(q):
        return _dot_sel_rhs(q, sel16)

    def state_free(ci):
        r0 = ci * T
        act = act_ref[r0:r0 + T, :]
        xs = act[:, :D_SSD]
        bm = act[:, D_SSD:D_SSD + GROUPS * STATE]
        cm = act[:, D_SSD + GROUPS * STATE:]
        dtv = jax.nn.softplus(dt_ref[0, r0:r0 + T, :] + dtb_ref[...])
        a_cs = _dot_sel_lhs(tri16, dtv * a_neg)
        bg16s = [bm[:, g * STATE:(g + 1) * STATE].astype(bf16) for g in range(GROUPS)]
        cg16s = [cm[:, g * STATE:(g + 1) * STATE].astype(bf16) for g in range(GROUPS)]
        gmats = [lax.dot_general(cg16s[g], bg16s[g], NT_DIMS, preferred_element_type=f32)
                 for g in range(GROUPS)]
        yield
        a_last = a_cs[T - 1:T, :]
        a_cs_t = a_cs.T
        rep = spread(jnp.concatenate([dtv, jnp.exp(a_last - a_cs), jnp.exp(a_cs)], axis=0))
        yield
        x = xs * rep[:T]
        x16 = x.astype(bf16)
        xd16 = (x * rep[T:2 * T]).astype(bf16)
        e_cs = rep[2 * T:]
        yield
        y_part = []
        for g in range(GROUPS):
            for pr in range(HEADS_PER_GROUP // 2):
                s_pair = []
                for q in range(2):
                    h = g * HEADS_PER_GROUP + 2 * pr + q
                    seg = a_cs[:, h:h + 1] - a_cs_t[h:h + 1, :]
                    lmat = jnp.exp(jnp.where(tri, seg, -jnp.inf))
                    s_pair.append((gmats[g] * lmat).astype(bf16))
                lanes = slice(g * gw + pr * LANES, g * gw + (pr + 1) * LANES)
                both = jnp.dot(jnp.concatenate(s_pair, axis=0), x16[:, lanes], preferred_element_type=f32)
                yield
                y_part.append(jnp.where(low_half, both[:T], both[T:]))
        y_free = jnp.concatenate(y_part, axis=-1) + xs * dexp_ref[...]
        return y_free, e_cs, jnp.exp(a_last), xd16, bg16s, cg16s

    def state_step(ci, y_free, e_cs, e_last, xd16, bg16s, cg16s):
        r0 = ci * T
        y_offs = []
        for g in range(GROUPS):
            hp = h_ref[g]
            y_offs.append(lax.dot_general(cg16s[g], hp.astype(bf16), NT_DIMS, preferred_element_type=f32))
            upd = lax.dot_general(xd16[:, g * gw:(g + 1) * gw], bg16s[g], TN_DIMS, preferred_element_type=f32)
            for hh in range(HEADS_PER_GROUP):
                h = g * HEADS_PER_GROUP + hh
                rows = slice(hh * HEAD_DIM, (hh + 1) * HEAD_DIM)
                h_ref[g, rows, :] = e_last[:, h:h + 1] * hp[rows] + upd[rows]
        y = y_free + jnp.concatenate(y_offs, axis=-1) * e_cs
        y_ref[0, r0:r0 + T, :] = _gated_group_norm(y, z_ref[0, r0:r0 + T, :], ng_ref[...])

    gens = [state_free(ci) for ci in range(cps)]
    free = [None] * cps
    while any(f is None for f in free):
        for ci in range(cps):
            if free[ci] is None:
                try:
                    next(gens[ci])
                except StopIteration as done:
                    free[ci] = done.value
    for ci in range(cps):
        state_step(ci, *free[ci])

    @pl.when(pl.program_id(1) == pl.num_programs(1) - 1)
    def _():
        st_ref[0] = h_ref[...].reshape(HEADS, HEAD_DIM, STATE)


def _ssd_prompt(xbc, z, dt, cw, cb, dtb, alog, dexp, ng, *, cps=8):
    nb, L, _ = xbc.shape
    T = SSD_CHUNK
    rows = cps * T
    blk = lambda wd: pl.BlockSpec((1, rows, wd), lambda b, c: (b, c, 0))
    return pl.pallas_call(
        functools.partial(_ssd_prompt_kernel, cps=cps),
        grid=(nb, L // rows),
        in_specs=[blk(CONV_DIM), blk(D_SSD), blk(DT_PAD),
                  _const_spec((CONV_W, CONV_DIM)), _const_spec((1, CONV_DIM)),
                  _const_spec((1, DT_PAD)), _const_spec((1, DT_PAD)),
                  _const_spec((1, D_SSD)), _const_spec((1, D_SSD))],
        out_specs=[blk(D_SSD),
                   pl.BlockSpec((1, HEADS, HEAD_DIM, STATE), lambda b, c: (b, 0, 0, 0)),
                   pl.BlockSpec((1, CONV_W - 1, CONV_DIM), lambda b, c: (b, 0, 0))],
        out_shape=[jax.ShapeDtypeStruct((nb, L, D_SSD), f32),
                   jax.ShapeDtypeStruct((nb, HEADS, HEAD_DIM, STATE), f32),
                   jax.ShapeDtypeStruct((nb, CONV_W - 1, CONV_DIM), f32)],
        scratch_shapes=[pltpu.VMEM((SUBLANES + rows, CONV_DIM), f32),
                        pltpu.VMEM((rows, CONV_DIM), f32),
                        pltpu.VMEM((GROUPS, HEADS_PER_GROUP * HEAD_DIM, STATE), f32)],
        compiler_params=_params("parallel", "arbitrary"),
        name="ssd_prompt",
    )(xbc, z, dt, cw, cb, dtb, alog, dexp, ng)


def _ssd_sample_kernel(xbc_ref, z_ref, dt_ref, cbuf_ref, st_ref, cw_ref, cb_ref, dtb_ref, alog_ref,
                       dexp_ref, ng_ref, y_ref, stn_ref, cn_ref, dtot_ref, *, L, nb):
    gw = HEADS_PER_GROUP * HEAD_DIM
    full = [cbuf_ref[j] for j in range(CONV_W - 1)] + [xbc_ref[t] for t in range(L)]
    for j in range(CONV_W - 1):
        cn_ref[j] = full[L + j]
    a_neg = -jnp.exp(alog_ref[...])
    xs, bm, cm, dtv, acs = [], [], [], [], []
    run = None
    for t in range(L):
        conv = cb_ref[...]
        for k in range(CONV_W):
            conv = conv + full[t + k] * cw_ref[k:k + 1, :]
        act = _silu(conv)
        xs.append(act[:, :D_SSD])
        bm.append(act[:, D_SSD:D_SSD + GROUPS * STATE])
        cm.append(act[:, D_SSD + GROUPS * STATE:])
        d = jax.nn.softplus(dt_ref[t] + dtb_ref[...])
        dtv.append(d)
        run = d * a_neg if run is None else run + d * a_neg
        acs.append(run)
    a_tot = acs[L - 1]
    dtot_ref[...] = jnp.exp(a_tot)

    lane_head = lax.broadcasted_iota(jnp.int32, (DT_PAD, D_SSD), 1) // HEAD_DIM
    sel16 = jnp.where(lax.broadcasted_iota(jnp.int32, (DT_PAD, D_SSD), 0) == lane_head, 1.0, 0.0).astype(bf16)
    pairs = [(t, s) for t in range(L) for s in range(t)]
    factors = (dtv + [jnp.exp(a_tot - acs[t]) for t in range(L)] + [jnp.exp(acs[t]) for t in range(L)]
               + [jnp.exp(acs[t] - acs[s]) for t, s in pairs])
    rep = _dot_sel_rhs(jnp.concatenate(factors, axis=0), sel16)
    piece = lambda i: rep[i * nb:(i + 1) * nb]
    x = [xs[t] * piece(t) for t in range(L)]
    xd_stack = jnp.concatenate([x[t] * piece(L + t) for t in range(L)], axis=0)
    e_cs = [piece(2 * L + t) for t in range(L)]
    decay = {ts: piece(3 * L + i) for i, ts in enumerate(pairs)}

    in_group0 = lax.broadcasted_iota(jnp.int32, (nb, D_SSD), 1) < gw
    y_intra = []
    for t in range(L):
        acc = None
        for s in range(t + 1):
            cb_dot = [jnp.sum(cm[t][:, g * STATE:(g + 1) * STATE] * bm[s][:, g * STATE:(g + 1) * STATE],
                              axis=-1, keepdims=True) for g in range(GROUPS)]
            w = jnp.where(in_group0, cb_dot[0], cb_dot[1])
            term = w * x[s] if s == t else w * decay[(t, s)] * x[s]
            acc = term if acc is None else acc + term
        y_intra.append(acc)

    c_stack = [jnp.concatenate([cm[t][:, g * STATE:(g + 1) * STATE] for t in range(L)], axis=0).astype(bf16)
               for g in range(GROUPS)]
    b_stack = [jnp.concatenate([bm[t][:, g * STATE:(g + 1) * STATE] for t in range(L)], axis=0).astype(bf16)
               for g in range(GROUPS)]
    seq_of_row = lax.broadcasted_iota(jnp.int32, (L * nb, gw), 0) & (nb - 1)

    def per_seq(b, yoff):
        mine = seq_of_row == b
        drow = dtot_ref[pl.ds(b, 1), :]
        out = []
        for g in range(GROUPS):
            h0 = st_ref[b, g]
            r = lax.dot_general(c_stack[g], h0.astype(bf16), NT_DIMS, preferred_element_type=f32)
            out.append(yoff[g] + jnp.where(mine, r, 0.0))
            xm = jnp.where(mine, xd_stack[:, g * gw:(g + 1) * gw], 0.0).astype(bf16)
            upd = lax.dot_general(xm, b_stack[g], TN_DIMS, preferred_element_type=f32)
            for hh in range(HEADS_PER_GROUP):
                h = g * HEADS_PER_GROUP + hh
                rows = slice(hh * HEAD_DIM, (hh + 1) * HEAD_DIM)
                stn_ref[b, g, rows, :] = drow[:, h:h + 1] * h0[rows] + upd[rows]
        return tuple(out)

    yoff = lax.fori_loop(0, nb, per_seq, tuple(jnp.zeros((L * nb, gw), f32) for _ in range(GROUPS)),
                         unroll=True)

    for t in range(L):
        y_off = jnp.concatenate([yoff[g][t * nb:(t + 1) * nb] for g in range(GROUPS)], axis=-1)
        y = y_intra[t] + y_off * e_cs[t] + xs[t] * dexp_ref[...]
        y_ref[t] = _gated_group_norm(y, z_ref[t], ng_ref[...])


def _ssd_sample(xbc, z, dt, cbuf, st, cw, cb, dtb, alog, dexp, ng, *, nb=8):
    L, B, _ = xbc.shape
    assert nb & (nb - 1) == 0 and B % nb == 0
    tblk = lambda n, wd: pl.BlockSpec((n, nb, wd), lambda i: (0, i, 0))
    gw = HEADS_PER_GROUP * HEAD_DIM
    st = st.reshape(B, GROUPS, gw, STATE)
    st_spec = pl.BlockSpec((nb, GROUPS, gw, STATE), lambda i: (i, 0, 0, 0))
    return pl.pallas_call(
        functools.partial(_ssd_sample_kernel, L=L, nb=nb),
        grid=(B // nb,),
        in_specs=[tblk(L, CONV_DIM), tblk(L, D_SSD), tblk(L, DT_PAD), tblk(CONV_W - 1, CONV_DIM), st_spec,
                  _const_spec((CONV_W, CONV_DIM)), _const_spec((1, CONV_DIM)),
                  _const_spec((1, DT_PAD)), _const_spec((1, DT_PAD)),
                  _const_spec((1, D_SSD)), _const_spec((1, D_SSD))],
        out_specs=[tblk(L, D_SSD), st_spec, tblk(CONV_W - 1, CONV_DIM)],
        out_shape=[jax.ShapeDtypeStruct((L, B, D_SSD), f32),
                   jax.ShapeDtypeStruct((B, GROUPS, gw, STATE), f32),
                   jax.ShapeDtypeStruct((CONV_W - 1, B, CONV_DIM), f32)],
        scratch_shapes=[pltpu.VMEM((nb, DT_PAD), f32)],
        compiler_params=_params("parallel"),
        name="ssd_sample",
    )(xbc, z, dt, cbuf, st, cw, cb, dtb, alog, dexp, ng)


def _s5_param_kernel(lr_ref, li_ref, ls_ref, br_ref, bi_ref, abr_ref, abi_ref, bbr_ref, bbi_ref):
    lr, li = lr_ref[...], li_ref[...]
    step = jnp.exp(ls_ref[...])
    mag = jnp.exp(lr * step)
    abr = mag * jnp.cos(li * step)
    abi = mag * jnp.sin(li * step)
    nr, ni = abr - 1.0, abi
    den = lr * lr + li * li
    fr = (nr * lr + ni * li) / den
    fi = (ni * lr - nr * li) / den
    abr_ref[...] = abr
    abi_ref[...] = abi
    br, bi = br_ref[...], bi_ref[...]
    bbr_ref[...] = fr * br - fi * bi
    bbi_ref[...] = fr * bi + fi * br


def _s5_params(a_re, a_im, log_step, b_re_t, b_im_t):
    g, p = a_re.shape
    vec = jax.ShapeDtypeStruct((g, 1, p), f32)
    mat = jax.ShapeDtypeStruct(b_re_t.shape, f32)
    return pl.pallas_call(_s5_param_kernel, out_shape=[vec, vec, mat, mat], name="s5_params")(
        a_re.reshape(g, 1, p), a_im.reshape(g, 1, p), log_step.reshape(g, 1, 1), b_re_t, b_im_t)


def _gelu(x):
    return 0.5 * x * (1.0 + lax.erf(x * (2.0 ** -0.5)))


def _interleave(*gens):
    live = list(gens)
    while live:
        for g in list(live):
            try:
                next(g)
            except StopIteration:
                live.remove(g)


def _s5_slabs_interleaved(ut_ref, hbuf_ref, hst_ref, g_ref, ar_ref, ai_ref, bblk_ref, cblk_ref, dsk_ref, tl,
                          row_block=256, n_stages=8):
    ns = S5_SLAB_STATE
    nb = SUBLANES
    rows = nb * tl

    def lanes(s):
        return slice(s * LANES, (s + 1) * LANES)

    def bu_stage(slabs):
        for s in slabs:
            for r in range(0, rows, row_block):
                us = ut_ref[r:r + row_block, lanes(s)]
                hbuf_ref[s, nb + r:nb + r + row_block, :] = jnp.dot(us.astype(bf16), bblk_ref[s],
                                                                    preferred_element_type=f32)
                yield

    def scan_stage(slabs):
        carry = [(hst_ref[s, :, :ns], hst_ref[s, :, ns:]) for s in slabs]
        coef = [(ar_ref[s], ai_ref[s]) for s in slabs]
        for t in range(tl):
            r = nb + nb * t
            for i, s in enumerate(slabs):
                (pr, pi), (ar, ai) = carry[i], coef[i]
                nr = ar * pr - ai * pi + hbuf_ref[s, r:r + nb, :ns]
                ni = ar * pi + ai * pr + hbuf_ref[s, r:r + nb, ns:]
                hbuf_ref[s, r:r + nb, :ns] = nr
                hbuf_ref[s, r:r + nb, ns:] = ni
                carry[i] = (nr, ni)
            if (t + 1) % (tl // n_stages) == 0:
                yield
        for i, s in enumerate(slabs):
            hst_ref[s, :, :ns] = carry[i][0]
            hst_ref[s, :, ns:] = carry[i][1]

    def y_stage(slabs):
        for s in slabs:
            for r in range(0, rows, row_block):
                h16 = hbuf_ref[s, nb + r:nb + r + row_block, :].astype(bf16)
                ys = jnp.dot(h16, cblk_ref[s], preferred_element_type=f32)
                ys = ys + dsk_ref[:, lanes(s)] * ut_ref[r:r + row_block, lanes(s)]
                g_ref[r:r + row_block, lanes(s)] = _gelu(ys)
                yield

    half = S5_SLABS // 2
    first, second = tuple(range(half)), tuple(range(half, S5_SLABS))
    _interleave(bu_stage(first))
    _interleave(scan_stage(first), bu_stage(second))
    _interleave(scan_stage(second), y_stage(first))
    _interleave(y_stage(second))


def _s5_scan_vmem(hbuf_ref, hst_ref, ar_ref, ai_ref, s, nb, rows):
    ns = S5_SLAB_STATE
    hbuf_ref[s, 0:nb, :] = hst_ref[s]
    ar, ai = ar_ref[s], ai_ref[s]

    def body(i, carry):
        j = pl.multiple_of(nb + i * SUBLANES, SUBLANES)
        prev = hbuf_ref[s, pl.ds(j - nb, SUBLANES), :]
        cur = hbuf_ref[s, pl.ds(j, SUBLANES), :]
        pr, pi = prev[:, :ns], prev[:, ns:]
        hbuf_ref[s, pl.ds(j, SUBLANES), :ns] = ar * pr - ai * pi + cur[:, :ns]
        hbuf_ref[s, pl.ds(j, SUBLANES), ns:] = ar * pi + ai * pr + cur[:, ns:]
        return carry

    lax.fori_loop(0, rows // SUBLANES, body, 0)
    hst_ref[s] = hbuf_ref[s, rows:rows + nb, :]


S5_N_IN = 10


def _s5_kernel(*refs, nb, tl, batch_major, n_cast):
    (u_ref, re0_ref, im0_ref, ar_ref, ai_ref, bblk_ref, cblk_ref, dsk_ref, wglu_ref,
     bglu_ref) = refs[:S5_N_IN]
    cast_in = refs[S5_N_IN:S5_N_IN + n_cast]
    y_ref, ren_ref, imn_ref = refs[S5_N_IN + n_cast:S5_N_IN + n_cast + 3]
    cast_out = refs[S5_N_IN + n_cast + 3:S5_N_IN + 2 * n_cast + 3]
    ut_ref, hbuf_ref, hst_ref, g_ref = refs[S5_N_IN + 2 * n_cast + 3:]
    for w_ref, w16_ref in zip(cast_in, cast_out):
        w16_ref[...] = w_ref[...].astype(bf16)
    step = pl.program_id(0)
    rows = nb * tl
    ns = S5_SLAB_STATE

    @pl.when(step == 0)
    def _():
        for s in range(S5_SLABS):
            hst_ref[s, :, :ns] = re0_ref[:, s * ns:(s + 1) * ns]
            hst_ref[s, :, ns:] = im0_ref[:, s * ns:(s + 1) * ns]

    if batch_major:
        ut_ref[...] = jnp.swapaxes(u_ref[...], 0, 1).reshape(rows, D_S5)
    else:
        ut_ref[...] = u_ref[...].reshape(rows, D_S5)

    if nb == SUBLANES:
        _s5_slabs_interleaved(ut_ref, hbuf_ref, hst_ref, g_ref, ar_ref, ai_ref, bblk_ref, cblk_ref, dsk_ref, tl)
    else:
        for s in range(S5_SLABS):
            sl = slice(s * LANES, (s + 1) * LANES)
            hbuf_ref[s, nb:nb + rows, :] = jnp.dot(ut_ref[:, sl].astype(bf16), bblk_ref[s],
                                                   preferred_element_type=f32)
            _s5_scan_vmem(hbuf_ref, hst_ref, ar_ref, ai_ref, s, nb, rows)
            ys = jnp.dot(hbuf_ref[s, nb:nb + rows, :].astype(bf16), cblk_ref[s], preferred_element_type=f32)
            g_ref[:, sl] = _gelu(ys + dsk_ref[:, sl] * ut_ref[:, sl])

    g = g_ref[...]
    gate = jnp.dot(g.astype(bf16), wglu_ref[...], preferred_element_type=f32) + bglu_ref[...]
    out = g * jax.nn.sigmoid(gate)
    if batch_major:
        y_ref[...] = jnp.swapaxes(out.reshape(tl, nb, D_S5), 0, 1)
    else:
        y_ref[...] = out.reshape(tl, nb, D_S5)

    @pl.when(step == pl.num_programs(0) - 1)
    def _():
        for s in range(S5_SLABS):
            ren_ref[:, s * ns:(s + 1) * ns] = hst_ref[s, :, :ns]
            imn_ref[:, s * ns:(s + 1) * ns] = hst_ref[s, :, ns:]


def _s5(u, re0, im0, ar, ai, bblk, cblk, dsk, wglu, bglu, *, tl, batch_major, cast=()):
    if batch_major:
        nb, L, _ = u.shape
        u_spec = pl.BlockSpec((nb, tl, D_S5), lambda i: (0, i, 0))
    else:
        L, nb, _ = u.shape
        u_spec = pl.BlockSpec((tl, nb, D_S5), lambda i: (i, 0, 0))
    rows = nb * tl
    steps = L // tl
    nstate = S5_GROUPS * S5_STATE
    st_spec = _const_spec((nb, nstate))
    cast_specs = [pl.BlockSpec((w.shape[0] // steps, w.shape[1]), lambda i: (i, 0)) for w in cast]
    assert all(w.shape[0] % (steps * 2 * SUBLANES) == 0 for w in cast)
    return pl.pallas_call(
        functools.partial(_s5_kernel, nb=nb, tl=tl, batch_major=batch_major, n_cast=len(cast)),
        grid=(steps,),
        in_specs=[u_spec, st_spec, st_spec,
                  _const_spec((S5_SLABS, SUBLANES, S5_SLAB_STATE)),
                  _const_spec((S5_SLABS, SUBLANES, S5_SLAB_STATE)),
                  _const_spec((S5_SLABS, LANES, 2 * S5_SLAB_STATE)),
                  _const_spec((S5_SLABS, 2 * S5_SLAB_STATE, LANES)),
                  _const_spec((1, D_S5)), _const_spec((D_S5, D_S5)), _const_spec((1, D_S5))] + cast_specs,
        out_specs=[u_spec, st_spec, st_spec] + cast_specs,
        out_shape=[jax.ShapeDtypeStruct(u.shape, f32),
                   jax.ShapeDtypeStruct((nb, nstate), f32),
                   jax.ShapeDtypeStruct((nb, nstate), f32)]
                  + [jax.ShapeDtypeStruct(w.shape, bf16) for w in cast],
        scratch_shapes=[pltpu.VMEM((rows, D_S5), f32),
                        pltpu.VMEM((S5_SLABS, nb + rows, 2 * S5_SLAB_STATE), f32),
                        pltpu.VMEM((S5_SLABS, nb, 2 * S5_SLAB_STATE), f32),
                        pltpu.VMEM((rows, D_S5), f32)],
        compiler_params=_params("arbitrary"),
        name="s5_mixer",
    )(u, re0, im0, ar, ai, bblk, cblk, dsk, wglu, bglu, *cast)


def _ffn_kernel(x_ref, ys_ref, y5_ref, mod_ref, modf_ref, n2g_ref, nfg_ref, wo_ref, wg_ref, wu_ref, wd_ref,
                o_ref, *, per_row, ff_chunk):
    g1 = _mod_row(mod_ref, 2, per_row)
    sh2 = _mod_row(mod_ref, 3, per_row)
    sc2 = _mod_row(mod_ref, 4, per_row)
    g2 = _mod_row(mod_ref, 5, per_row)
    shf = _mod_row(modf_ref, 0, per_row)
    scf = _mod_row(modf_ref, 1, per_row)
    att = jnp.dot(ys_ref[0].astype(bf16), wo_ref[:D_SSD, :], preferred_element_type=f32)
    att = att + jnp.dot(y5_ref[0].astype(bf16), wo_ref[D_SSD:, :], preferred_element_type=f32)
    x1 = x_ref[0] + _by_row(lambda v, s: v * s, att, g1)
    v = _rms_mod(x1, n2g_ref[...], sc2, sh2).astype(bf16)
    ff = None
    for o in range(0, D_FF, ff_chunk):
        gate = jnp.dot(v, wg_ref[:, o:o + ff_chunk], preferred_element_type=f32)
        up = jnp.dot(v, wu_ref[:, o:o + ff_chunk], preferred_element_type=f32)
        hid = (_silu(gate) * up).astype(bf16)
        part = jnp.dot(hid, wd_ref[o:o + ff_chunk, :], preferred_element_type=f32)
        ff = part if ff is None else ff + part
    x2 = x1 + _by_row(lambda v, s: v * s, ff, g2)
    o_ref[0] = _rms_mod(x2, nfg_ref[...], scf, shf)


def _ffn(x, ys, y5, mod, modf, n2g, nfg, wo, wg, wu, wd, *, tm, per_row, n_seq, first_seq, ff_chunk=256):
    nb, rows, d = x.shape
    nt = rows // tm
    mod_spec = _mod_spec(mod, n_seq, first_seq)
    modf_spec = _mod_spec(modf, n_seq, first_seq)
    blk = lambda wd_: pl.BlockSpec((1, tm, wd_), lambda i, j: (i, j, 0))
    single = dict(pipeline_mode=pl.Buffered(1))
    wspec = lambda shape: pl.BlockSpec(shape, lambda i, j: (0, 0), **single)
    return pl.pallas_call(
        functools.partial(_ffn_kernel, per_row=per_row, ff_chunk=ff_chunk),
        grid=(nb, nt),
        in_specs=[blk(d), blk(D_SSD), blk(D_S5), mod_spec, modf_spec,
                  _const_spec((1, d)), _const_spec((1, d)),
                  wspec((d, d)), wspec((d, D_FF)), wspec((d, D_FF)), wspec((D_FF, d))],
        out_specs=blk(d),
        out_shape=jax.ShapeDtypeStruct((nb, rows, d), f32),
        compiler_params=_params("parallel", "parallel"),
        name="out_ffn",
    )(x, ys, y5, mod, modf, n2g, nfg, wo, wg, wu, wd)


def _block_diag(m):
    s, k, a, b = m.shape
    eye = jnp.eye(k, dtype=m.dtype)
    return (m[:, :, :, None, :] * eye[None, :, None, :, None]).reshape(s, k * a, k * b)


def kernel(x_prompt, x_sample, c_prompt, c_sample, state_ssd, state_conv, state_s5_re, state_s5_im, w_ada, b_ada, norm1_g, w_in, conv_w, conv_b, ssd_dt_bias, ssd_A_log, ssd_D, ssd_norm_g, s5_A_re, s5_A_im, s5_log_step, s5_B_re, s5_B_im, s5_C_re, s5_C_im, s5_D, w_glu, b_glu, w_out, norm2_g, w_ffn_gate, w_ffn_up, w_ffn_down, w_ada_f, b_ada_f, normf_g):
    assert w_ada.shape[0] == 1, "single-layer stack"
    bp, seq, d = x_prompt.shape
    bs, dseq, _ = x_sample.shape

    c_all = jnp.concatenate([c_sample, c_prompt], axis=0)
    mod = _ada_mod(c_all, w_ada[0], b_ada[0])
    modf = _ada_mod(c_all, w_ada_f, b_ada_f)
    seqs_s = dict(n_seq=bs, first_seq=0)
    seqs_p = dict(n_seq=bp, first_seq=bs)

    o2, o3 = D_SSD + CONV_DIM, D_SSD + CONV_DIM + HEADS
    wi = w_in[0]
    w_u5 = wi[:, o3:]
    w_dt = jnp.concatenate([wi[:, o2:o3], jnp.zeros((d, DT_PAD - HEADS), f32)], axis=1)
    pad_h = lambda v: jnp.concatenate([v, jnp.zeros((DT_PAD - HEADS,), f32)]).reshape(1, DT_PAD)
    dtb = pad_h(ssd_dt_bias[0])
    alog = pad_h(ssd_A_log[0])
    dexp = jnp.repeat(ssd_D[0], HEAD_DIM).reshape(1, D_SSD)
    ng = ssd_norm_g[0].reshape(1, D_SSD)
    cw, cb = conv_w[0], conv_b[0].reshape(1, CONV_DIM)

    abr, abi, bbr, bbi = _s5_params(s5_A_re[0], s5_A_im[0], s5_log_step[0],
                                    s5_B_re[0].transpose(0, 2, 1), s5_B_im[0].transpose(0, 2, 1))
    gps = S5_GROUPS // S5_SLABS
    tile_rows = lambda v: jnp.broadcast_to(v.reshape(S5_SLABS, 1, S5_SLAB_STATE),
                                           (S5_SLABS, SUBLANES, S5_SLAB_STATE))
    ar, ai = tile_rows(abr), tile_rows(abi)
    slab = lambda m: m.reshape((S5_SLABS, gps) + m.shape[1:])
    bblk = jnp.concatenate([_block_diag(slab(bbr)), _block_diag(slab(bbi))], axis=2).astype(bf16)
    cblk = jnp.concatenate([_block_diag(slab(s5_C_re[0].transpose(0, 2, 1))),
                            _block_diag(slab(-s5_C_im[0].transpose(0, 2, 1)))], axis=1).astype(bf16)
    dsk = s5_D[0].reshape(1, D_S5)
    wglu = w_glu[0].astype(bf16)
    bglu = b_glu[0].reshape(1, D_S5)
    n1g, n2g, nfg = norm1_g[0].reshape(1, d), norm2_g[0].reshape(1, d), normf_g.reshape(1, d)
    nstate = S5_GROUPS * S5_STATE

    z, xbc, u5, dt = _inproj(x_prompt, mod, n1g, wi, w_u5, w_dt, tm=512, per_row=False, **seqs_p)
    y_ssd, ssd_p, conv_p = _ssd_prompt(xbc, z, dt, cw, cb, dtb, alog, dexp, ng)
    zeros_st = jnp.zeros((bp, nstate), f32)
    y_s5, re_p, im_p, wo, wg, wu, wd = _s5(
        u5, zeros_st, zeros_st, ar, ai, bblk, cblk, dsk, wglu, bglu, tl=128, batch_major=True,
        cast=(w_out[0], w_ffn_gate[0], w_ffn_up[0], w_ffn_down[0]))
    y_prompt = _ffn(x_prompt, y_ssd, y_s5, mod, modf, n2g, nfg, wo, wg, wu, wd, tm=512, per_row=False, **seqs_p)

    rows_s = dseq * bs
    xs_t = x_sample.transpose(1, 0, 2).reshape(1, rows_s, d)
    steps = lambda a: a.reshape(dseq, bs, a.shape[-1])
    flat = lambda a: a.reshape(1, rows_s, a.shape[-1])
    z, xbc, u5, dt = _inproj(xs_t, mod, n1g, wi, w_u5, w_dt, tm=rows_s, per_row=True, **seqs_s)
    y_ssd, ssd_s, conv_s = _ssd_sample(steps(xbc), steps(z), steps(dt), state_conv[0].transpose(1, 0, 2),
                                       state_ssd[0], cw, cb, dtb, alog, dexp, ng)
    y_s5, re_s, im_s = _s5(steps(u5), state_s5_re[0].reshape(bs, nstate), state_s5_im[0].reshape(bs, nstate),
                           ar, ai, bblk, cblk, dsk, wglu, bglu, tl=dseq, batch_major=False)
    y_sample = _ffn(xs_t, flat(y_ssd), flat(y_s5), mod, modf, n2g, nfg, wo, wg, wu, wd,
                    tm=rows_s, per_row=True, **seqs_s)
    y_sample = steps(y_sample).transpose(1, 0, 2)

    g5 = (S5_GROUPS, S5_STATE)
    return (y_prompt, y_sample,
            ssd_p[None], ssd_s.reshape((1,) + state_ssd.shape[1:]),
            conv_p[None], conv_s.transpose(1, 0, 2)[None],
            re_p.reshape((1, bp) + g5), re_s.reshape((1, bs) + g5),
            im_p.reshape((1, bp) + g5), im_s.reshape((1, bs) + g5))
```

```python
import functools

import jax
import jax.numpy as jnp
from jax import lax
from jax.experimental import pallas as pl
from jax.experimental.pallas import tpu as pltpu

f32 = jnp.float32
bf16 = jnp.bfloat16

D_MODEL = 1024
D_SSD = 512
HEAD_DIM = 64
HEADS = 8
GROUPS = 2
HEADS_PER_GROUP = HEADS // GROUPS
STATE = 128
CONV_W = 4
CONV_DIM = D_SSD + 2 * GROUPS * STATE
D_S5 = 512
S5_CH = 16
S5_GROUPS = 32
S5_STATE = 64
D_FF = 2816
N_ADA = 6
EPS = 1e-6

LANES = 128
SUBLANES = 8
SSD_CHUNK = 128
S5_SLABS = D_S5 // LANES
S5_SLAB_STATE = (S5_GROUPS // S5_SLABS) * S5_STATE
DT_PAD = LANES
VMEM_LIMIT = 56 * 1024 * 1024

NT_DIMS = (((1,), (1,)), ((), ()))
TN_DIMS = (((0,), (0,)), ((), ()))


def _silu(x):
    return x * jax.nn.sigmoid(x)


def _by_row(fn, v, *ms):
    r, d = v.shape
    m_rows = ms[0].shape[0]
    if m_rows in (1, r):
        return fn(v, *ms)
    out = fn(v.reshape(r // m_rows, m_rows, d), *[m[None] for m in ms])
    return out.reshape(r, d)


def _rms_mod(x, g, sc, sh):
    y = x * lax.rsqrt(jnp.mean(x * x, axis=-1, keepdims=True) + EPS)
    return _by_row(lambda v, s, t: v * (1.0 + s) + t, y * g, sc, sh)


def _mod_row(mod_ref, i, per_row):
    return mod_ref[i] if per_row else mod_ref[i, pl.ds(pl.program_id(0), 1), :]


def _mod_spec(mod, n_seq, first_seq):
    assert first_seq % n_seq == 0
    return pl.BlockSpec((mod.shape[0], n_seq, mod.shape[2]), lambda *_: (0, first_seq // n_seq, 0))


def _const_spec(shape):
    nd = len(shape)
    return pl.BlockSpec(shape, lambda *_: (0,) * nd)


def _params(*sem):
    return pltpu.CompilerParams(dimension_semantics=sem, vmem_limit_bytes=VMEM_LIMIT)


def _ada_kernel(c_ref, w_ref, b_ref, o_ref):
    s = _silu(c_ref[...]).astype(bf16)
    o_ref[0] = jnp.dot(s, w_ref[...].astype(bf16), preferred_element_type=f32) + b_ref[...]


def _ada_mod(c, w, b):
    m, k = c.shape
    n = w.shape[1] // k
    return pl.pallas_call(
        _ada_kernel,
        grid=(n,),
        in_specs=[pl.BlockSpec((m, k), lambda j: (0, 0)),
                  pl.BlockSpec((k, k), lambda j: (0, j)),
                  pl.BlockSpec((1, k), lambda j: (0, j))],
        out_specs=pl.BlockSpec((1, m, k), lambda j: (j, 0, 0)),
        out_shape=jax.ShapeDtypeStruct((n, m, k), f32),
        compiler_params=_params("parallel"),
        name="ada_mod",
    )(c, w, b.reshape(1, n * k))


def _rows_in(x_ref, seq_major):
    if not seq_major:
        return x_ref[0]
    s, t, d = x_ref.shape
    return jnp.swapaxes(x_ref[...], 0, 1).reshape(s * t, d)


def _inproj_kernel(x_ref, mod_ref, g_ref, w_ref, z_ref, xbc_ref, u5_ref, dt_ref, *, seq_major):
    sh = _mod_row(mod_ref, 0, seq_major)
    sc = _mod_row(mod_ref, 1, seq_major)
    u = _rms_mod(_rows_in(x_ref, seq_major), g_ref[...], sc, sh).astype(bf16)
    o = 0
    for ref, width in ((z_ref, D_SSD), (xbc_ref, CONV_DIM), (u5_ref, D_S5), (dt_ref, DT_PAD)):
        ref[0] = jnp.dot(u, w_ref[:, o:o + width].astype(bf16), preferred_element_type=f32)
        o += width


def _inproj(x, mod, g, w, *, tm, seq_major, n_seq, first_seq):
    if seq_major:
        x_spec = _const_spec(x.shape)
        nb, rows, d = 1, x.shape[0] * x.shape[1], x.shape[2]
        assert tm == rows
    else:
        nb, rows, d = x.shape
        x_spec = pl.BlockSpec((1, tm, d), lambda i, j: (i, j, 0))
    nt = rows // tm
    widths = (D_SSD, CONV_DIM, D_S5, DT_PAD)
    return pl.pallas_call(
        functools.partial(_inproj_kernel, seq_major=seq_major),
        grid=(nb, nt),
        in_specs=[x_spec,
                  _mod_spec(mod, n_seq, first_seq),
                  _const_spec((1, d)),
                  pl.BlockSpec((d, sum(widths)), lambda i, j: (0, 0), pipeline_mode=pl.Buffered(1))],
        out_specs=[pl.BlockSpec((1, tm, wd), lambda i, j: (i, j, 0)) for wd in widths],
        out_shape=[jax.ShapeDtypeStruct((nb, rows, wd), f32) for wd in widths],
        compiler_params=_params("parallel", "parallel"),
        name="in_proj",
    )(x, mod, g, w)


def _split3(x):
    hi = x.astype(bf16)
    r1 = x - hi.astype(f32)
    mid = r1.astype(bf16)
    lo = (r1 - mid.astype(f32)).astype(bf16)
    return hi, mid, lo


def _dot_sel_lhs(sel16, x):
    return sum(jnp.dot(sel16, p, preferred_element_type=f32) for p in _split3(x))


def _dot_sel_rhs(x, sel16):
    return sum(jnp.dot(p, sel16, preferred_element_type=f32) for p in _split3(x))


def _gated_group_norm(y, z, ng):
    y = y * _silu(z)
    gw = D_SSD // GROUPS
    parts = []
    for g in range(GROUPS):
        yg = y[:, g * gw:(g + 1) * gw]
        parts.append(yg * lax.rsqrt(jnp.mean(yg * yg, axis=-1, keepdims=True) + EPS))
    return jnp.concatenate(parts, axis=-1) * ng


def _ssd_prompt_kernel(xbc_ref, z_ref, dt_ref, cw_ref, cb_ref, dtb_ref, alog_ref, dexp_ref, ng_ref,
                       y_ref, st_ref, cn_ref, ext_ref, act_ref, h_ref, *, cps):
    T = SSD_CHUNK
    gw = HEADS_PER_GROUP * HEAD_DIM

    @pl.when(pl.program_id(1) == 0)
    def _():
        ext_ref[0:SUBLANES, :] = jnp.zeros((SUBLANES, CONV_DIM), f32)
        h_ref[...] = jnp.zeros_like(h_ref)

    row = lax.broadcasted_iota(jnp.int32, (T, T), 0)
    col = lax.broadcasted_iota(jnp.int32, (T, T), 1)
    tri = row >= col
    tri16 = jnp.where(tri, 1.0, 0.0).astype(bf16)
    low_half = lax.broadcasted_iota(jnp.int32, (T, LANES), 1) < HEAD_DIM
    a_neg = -jnp.exp(alog_ref[...])

    R = cps * T
    ext_ref[SUBLANES:SUBLANES + R, :] = xbc_ref[0]
    for ci in range(cps):
        ext = ext_ref[ci * T:ci * T + SUBLANES + T, :]
        conv = cb_ref[...] + ext[SUBLANES:, :] * cw_ref[CONV_W - 1:CONV_W, :]
        for m in range(1, CONV_W):
            k = CONV_W - 1 - m
            conv = conv + pltpu.roll(ext, m, 0)[SUBLANES:, :] * cw_ref[k:k + 1, :]
        act_ref[ci * T:(ci + 1) * T, :] = _silu(conv)
    cn_ref[0] = ext_ref[SUBLANES + R - (CONV_W - 1):SUBLANES + R, :]
    ext_ref[0:SUBLANES, :] = ext_ref[R:R + SUBLANES, :]

    lane_head = lax.broadcasted_iota(jnp.int32, (DT_PAD, D_SSD), 1) // HEAD_DIM
    sel16 = jnp.where(lax.broadcasted_iota(jnp.int32, (DT_PAD, D_SSD), 0) == lane_head, 1.0, 0.0).astype(bf16)

    def spread(q):
        return _dot_sel_rhs(q, sel16)

    def state_free(ci):
        r0 = ci * T
        act = act_ref[r0:r0 + T, :]
        xs = act[:, :D_SSD]
        bm = act[:, D_SSD:D_SSD + GROUPS * STATE]
        cm = act[:, D_SSD + GROUPS * STATE:]
        dtv = jax.nn.softplus(dt_ref[0, r0:r0 + T, :] + dtb_ref[...])
        a_cs = _dot_sel_lhs(tri16, dtv * a_neg)
        bg16s = [bm[:, g * STATE:(g + 1) * STATE].astype(bf16) for g in range(GROUPS)]
        cg16s = [cm[:, g * STATE:(g + 1) * STATE].astype(bf16) for g in range(GROUPS)]
        gmats = [lax.dot_general(cg16s[g], bg16s[g], NT_DIMS, preferred_element_type=f32)
                 for g in range(GROUPS)]
        yield
        a_last = a_cs[T - 1:T, :]
        a_cs_t = a_cs.T
        rep = spread(jnp.concatenate([dtv, jnp.exp(a_last - a_cs), jnp.exp(a_cs)], axis=0))
        yield
        x = xs * rep[:T]
        x16 = x.astype(bf16)
        xd16 = (x * rep[T:2 * T]).astype(bf16)
        e_cs = rep[2 * T:]
        yield
        y_part = []
        for g in range(GROUPS):
            for pr in range(HEADS_PER_GROUP // 2):
                s_pair = []
                for q in range(2):
                    h = g * HEADS_PER_GROUP + 2 * pr + q
                    seg = a_cs[:, h:h + 1] - a_cs_t[h:h + 1, :]
                    lmat = jnp.exp(jnp.where(tri, seg, -jnp.inf))
                    s_pair.append((gmats[g] * lmat).astype(bf16))
                lanes = slice(g * gw + pr * LANES, g * gw + (pr + 1) * LANES)
                both = jnp.dot(jnp.concatenate(s_pair, axis=0), x16[:, lanes], preferred_element_type=f32)
                yield
                y_part.append(jnp.where(low_half, both[:T], both[T:]))
        y_free = jnp.concatenate(y_part, axis=-1) + xs * dexp_ref[...]
        return y_free, e_cs, jnp.exp(a_last), xd16, bg16s, cg16s

    def state_step(ci, y_free, e_cs, e_last, xd16, bg16s, cg16s):
        r0 = ci * T
        y_offs = []
        for g in range(GROUPS):
            hp = h_ref[g]
            y_offs.append(lax.dot_general(cg16s[g], hp.astype(bf16), NT_DIMS, preferred_element_type=f32))
            upd = lax.dot_general(xd16[:, g * gw:(g + 1) * gw], bg16s[g], TN_DIMS, preferred_element_type=f32)
            for hh in range(HEADS_PER_GROUP):
                h = g * HEADS_PER_GROUP + hh
                rows = slice(hh * HEAD_DIM, (hh + 1) * HEAD_DIM)
                h_ref[g, rows, :] = e_last[:, h:h + 1] * hp[rows] + upd[rows]
        y = y_free + jnp.concatenate(y_offs, axis=-1) * e_cs
        y_ref[0, r0:r0 + T, :] = _gated_group_norm(y, z_ref[0, r0:r0 + T, :], ng_ref[...])

    gens = [state_free(ci) for ci in range(cps)]
    free = [None] * cps
    while any(f is None for f in free):
        for ci in range(cps):
            if free[ci] is None:
                try:
                    next(gens[ci])
                except StopIteration as done:
                    free[ci] = done.value
    for ci in range(cps):
        state_step(ci, *free[ci])

    @pl.when(pl.program_id(1) == pl.num_programs(1) - 1)
    def _():
        st_ref[0] = h_ref[...].reshape(HEADS, HEAD_DIM, STATE)


def _ssd_prompt(xbc, z, dt, cw, cb, dtb, alog, dexp, ng, *, cps=8):
    nb, L, _ = xbc.shape
    T = SSD_CHUNK
    rows = cps * T
    blk = lambda wd: pl.BlockSpec((1, rows, wd), lambda b, c: (b, c, 0))
    return pl.pallas_call(
        functools.partial(_ssd_prompt_kernel, cps=cps),
        grid=(nb, L // rows),
        in_specs=[blk(CONV_DIM), blk(D_SSD), blk(DT_PAD),
                  _const_spec((CONV_W, CONV_DIM)), _const_spec((1, CONV_DIM)),
                  _const_spec((1, DT_PAD)), _const_spec((1, DT_PAD)),
                  _const_spec((1, D_SSD)), _const_spec((1, D_SSD))],
        out_specs=[blk(D_SSD),
                   pl.BlockSpec((1, HEADS, HEAD_DIM, STATE), lambda b, c: (b, 0, 0, 0)),
                   pl.BlockSpec((1, CONV_W - 1, CONV_DIM), lambda b, c: (b, 0, 0))],
        out_shape=[jax.ShapeDtypeStruct((nb, L, D_SSD), f32),
                   jax.ShapeDtypeStruct((nb, HEADS, HEAD_DIM, STATE), f32),
                   jax.ShapeDtypeStruct((nb, CONV_W - 1, CONV_DIM), f32)],
        scratch_shapes=[pltpu.VMEM((SUBLANES + rows, CONV_DIM), f32),
                        pltpu.VMEM((rows, CONV_DIM), f32),
                        pltpu.VMEM((GROUPS, HEADS_PER_GROUP * HEAD_DIM, STATE), f32)],
        compiler_params=_params("parallel", "arbitrary"),
        name="ssd_prompt",
    )(xbc, z, dt, cw, cb, dtb, alog, dexp, ng)


def _ssd_sample_kernel(xbc_ref, z_ref, dt_ref, cbuf_ref, st_ref, cw_ref, cb_ref, dtb_ref, alog_ref,
                       dexp_ref, ng_ref, y_ref, stn_ref, cn_ref, dtot_ref, *, L, nb):
    gw = HEADS_PER_GROUP * HEAD_DIM
    full = [cbuf_ref[j] for j in range(CONV_W - 1)] + [xbc_ref[t] for t in range(L)]
    for j in range(CONV_W - 1):
        cn_ref[j] = full[L + j]
    a_neg = -jnp.exp(alog_ref[...])
    xs, bm, cm, dtv, acs = [], [], [], [], []
    run = None
    for t in range(L):
        conv = cb_ref[...]
        for k in range(CONV_W):
            conv = conv + full[t + k] * cw_ref[k:k + 1, :]
        act = _silu(conv)
        xs.append(act[:, :D_SSD])
        bm.append(act[:, D_SSD:D_SSD + GROUPS * STATE])
        cm.append(act[:, D_SSD + GROUPS * STATE:])
        d = jax.nn.softplus(dt_ref[t] + dtb_ref[...])
        dtv.append(d)
        run = d * a_neg if run is None else run + d * a_neg
        acs.append(run)
    a_tot = acs[L - 1]
    dtot_ref[...] = jnp.exp(a_tot)

    lane_head = lax.broadcasted_iota(jnp.int32, (DT_PAD, D_SSD), 1) // HEAD_DIM
    sel16 = jnp.where(lax.broadcasted_iota(jnp.int32, (DT_PAD, D_SSD), 0) == lane_head, 1.0, 0.0).astype(bf16)
    pairs = [(t, s) for t in range(L) for s in range(t)]
    factors = (dtv + [jnp.exp(a_tot - acs[t]) for t in range(L)] + [jnp.exp(acs[t]) for t in range(L)]
               + [jnp.exp(acs[t] - acs[s]) for t, s in pairs])
    rep = _dot_sel_rhs(jnp.concatenate(factors, axis=0), sel16)
    piece = lambda i: rep[i * nb:(i + 1) * nb]
    x = [xs[t] * piece(t) for t in range(L)]
    xd_stack = jnp.concatenate([x[t] * piece(L + t) for t in range(L)], axis=0)
    e_cs = [piece(2 * L + t) for t in range(L)]
    decay = {ts: piece(3 * L + i) for i, ts in enumerate(pairs)}

    in_group0 = lax.broadcasted_iota(jnp.int32, (nb, D_SSD), 1) < gw
    y_intra = []
    for t in range(L):
        acc = None
        for s in range(t + 1):
            cb_dot = [jnp.sum(cm[t][:, g * STATE:(g + 1) * STATE] * bm[s][:, g * STATE:(g + 1) * STATE],
                              axis=-1, keepdims=True) for g in range(GROUPS)]
            w = jnp.where(in_group0, cb_dot[0], cb_dot[1])
            term = w * x[s] if s == t else w * decay[(t, s)] * x[s]
            acc = term if acc is None else acc + term
        y_intra.append(acc)

    c_stack = [jnp.concatenate([cm[t][:, g * STATE:(g + 1) * STATE] for t in range(L)], axis=0).astype(bf16)
               for g in range(GROUPS)]
    b_stack = [jnp.concatenate([bm[t][:, g * STATE:(g + 1) * STATE] for t in range(L)], axis=0).astype(bf16)
               for g in range(GROUPS)]
    seq_of_row = lax.broadcasted_iota(jnp.int32, (L * nb, gw), 0) & (nb - 1)

    def per_seq(b, yoff):
        mine = seq_of_row == b
        drow = dtot_ref[pl.ds(b, 1), :]
        out = []
        for g in range(GROUPS):
            h0 = st_ref[b, g]
            r = lax.dot_general(c_stack[g], h0.astype(bf16), NT_DIMS, preferred_element_type=f32)
            out.append(yoff[g] + jnp.where(mine, r, 0.0))
            xm = jnp.where(mine, xd_stack[:, g * gw:(g + 1) * gw], 0.0).astype(bf16)
            upd = lax.dot_general(xm, b_stack[g], TN_DIMS, preferred_element_type=f32)
            for hh in range(HEADS_PER_GROUP):
                h = g * HEADS_PER_GROUP + hh
                rows = slice(hh * HEAD_DIM, (hh + 1) * HEAD_DIM)
                stn_ref[b, g, rows, :] = drow[:, h:h + 1] * h0[rows] + upd[rows]
        return tuple(out)

    yoff = lax.fori_loop(0, nb, per_seq, tuple(jnp.zeros((L * nb, gw), f32) for _ in range(GROUPS)),
                         unroll=True)

    for t in range(L):
        y_off = jnp.concatenate([yoff[g][t * nb:(t + 1) * nb] for g in range(GROUPS)], axis=-1)
        y = y_intra[t] + y_off * e_cs[t] + xs[t] * dexp_ref[...]
        y_ref[t] = _gated_group_norm(y, z_ref[t], ng_ref[...])


def _ssd_sample(xbc, z, dt, cbuf, st, cw, cb, dtb, alog, dexp, ng, *, nb=8):
    L, B, _ = xbc.shape
    assert nb & (nb - 1) == 0 and B % nb == 0
    tblk = lambda n, wd: pl.BlockSpec((n, nb, wd), lambda i: (0, i, 0))
    gw = HEADS_PER_GROUP * HEAD_DIM
    st = st.reshape(B, GROUPS, gw, STATE)
    st_spec = pl.BlockSpec((nb, GROUPS, gw, STATE), lambda i: (i, 0, 0, 0))
    return pl.pallas_call(
        functools.partial(_ssd_sample_kernel, L=L, nb=nb),
        grid=(B // nb,),
        in_specs=[tblk(L, CONV_DIM), tblk(L, D_SSD), tblk(L, DT_PAD), tblk(CONV_W - 1, CONV_DIM), st_spec,
                  _const_spec((CONV_W, CONV_DIM)), _const_spec((1, CONV_DIM)),
                  _const_spec((1, DT_PAD)), _const_spec((1, DT_PAD)),
                  _const_spec((1, D_SSD)), _const_spec((1, D_SSD))],
        out_specs=[tblk(L, D_SSD), st_spec, tblk(CONV_W - 1, CONV_DIM)],
        out_shape=[jax.ShapeDtypeStruct((L, B, D_SSD), f32),
                   jax.ShapeDtypeStruct((B, GROUPS, gw, STATE), f32),
                   jax.ShapeDtypeStruct((CONV_W - 1, B, CONV_DIM), f32)],
        scratch_shapes=[pltpu.VMEM((nb, DT_PAD), f32)],
        compiler_params=_params("parallel"),
        name="ssd_sample",
    )(xbc, z, dt, cbuf, st, cw, cb, dtb, alog, dexp, ng)


def _s5_param_kernel(lr_ref, li_ref, ls_ref, br_ref, bi_ref, cr_ref, ci_ref,
                     ar_ref, ai_ref, bblk_ref, cblk_ref, b_scr, c_scr):
    lr, li = lr_ref[...], li_ref[...]
    step = jnp.exp(ls_ref[...])
    mag = jnp.exp(lr * step)
    abr = mag * jnp.cos(li * step)
    abi = mag * jnp.sin(li * step)
    nr, ni = abr - 1.0, abi
    den = lr * lr + li * li
    fr = (nr * lr + ni * li) / den
    fi = (ni * lr - nr * li) / den
    br, bi = br_ref[...], bi_ref[...]
    bbr = fr * br - fi * bi
    bbi = fr * bi + fi * br
    b_scr[...] = jnp.zeros_like(b_scr)
    c_scr[...] = jnp.zeros_like(c_scr)
    gps = S5_GROUPS // S5_SLABS
    ns = S5_SLAB_STATE
    for g in range(S5_GROUPS):
        s, gl = divmod(g, gps)
        ch = slice(gl * S5_CH, (gl + 1) * S5_CH)
        st = slice(gl * S5_STATE, (gl + 1) * S5_STATE)
        st_im = slice(ns + gl * S5_STATE, ns + (gl + 1) * S5_STATE)
        ar_ref[s, :, st] = jnp.broadcast_to(abr[g], (SUBLANES, S5_STATE))
        ai_ref[s, :, st] = jnp.broadcast_to(abi[g], (SUBLANES, S5_STATE))
        b_scr[s, ch, st] = bbr[g]
        b_scr[s, ch, st_im] = bbi[g]
        c_scr[s, st, ch] = cr_ref[g].T
        c_scr[s, st_im, ch] = -ci_ref[g].T
    bblk_ref[...] = b_scr[...].astype(bf16)
    cblk_ref[...] = c_scr[...].astype(bf16)


def _s5_params(a_re, a_im, log_step, b_re_t, b_im_t, c_re, c_im):
    g, p = a_re.shape
    ns = S5_SLAB_STATE
    return pl.pallas_call(
        _s5_param_kernel,
        out_shape=[jax.ShapeDtypeStruct((S5_SLABS, SUBLANES, ns), f32),
                   jax.ShapeDtypeStruct((S5_SLABS, SUBLANES, ns), f32),
                   jax.ShapeDtypeStruct((S5_SLABS, LANES, 2 * ns), bf16),
                   jax.ShapeDtypeStruct((S5_SLABS, 2 * ns, LANES), bf16)],
        scratch_shapes=[pltpu.VMEM((S5_SLABS, LANES, 2 * ns), f32),
                        pltpu.VMEM((S5_SLABS, 2 * ns, LANES), f32)],
        name="s5_params",
    )(a_re.reshape(g, 1, p), a_im.reshape(g, 1, p), log_step.reshape(g, 1, 1), b_re_t, b_im_t, c_re, c_im)


def _gelu(x):
    return 0.5 * x * (1.0 + lax.erf(x * (2.0 ** -0.5)))


def _interleave(*gens):
    live = list(gens)
    while live:
        for g in list(live):
            try:
                next(g)
            except StopIteration:
                live.remove(g)


def _s5_slabs_interleaved(ut_ref, hbuf_ref, hst_ref, g_ref, ar_ref, ai_ref, bblk_ref, cblk_ref, dsk_ref, tl,
                          row_block=256, n_stages=8):
    ns = S5_SLAB_STATE
    nb = SUBLANES
    rows = nb * tl

    def lanes(s):
        return slice(s * LANES, (s + 1) * LANES)

    def bu_stage(slabs):
        for s in slabs:
            for r in range(0, rows, row_block):
                us = ut_ref[r:r + row_block, lanes(s)]
                hbuf_ref[s, nb + r:nb + r + row_block, :] = jnp.dot(us.astype(bf16), bblk_ref[s],
                                                                    preferred_element_type=f32)
                yield

    def scan_stage(slabs):
        carry = [(hst_ref[s, :, :ns], hst_ref[s, :, ns:]) for s in slabs]
        coef = [(ar_ref[s], ai_ref[s]) for s in slabs]
        for t in range(tl):
            r = nb + nb * t
            for i, s in enumerate(slabs):
                (pr, pi), (ar, ai) = carry[i], coef[i]
                nr = ar * pr - ai * pi + hbuf_ref[s, r:r + nb, :ns]
                ni = ar * pi + ai * pr + hbuf_ref[s, r:r + nb, ns:]
                hbuf_ref[s, r:r + nb, :ns] = nr
                hbuf_ref[s, r:r + nb, ns:] = ni
                carry[i] = (nr, ni)
            if (t + 1) % (tl // n_stages) == 0:
                yield
        for i, s in enumerate(slabs):
            hst_ref[s, :, :ns] = carry[i][0]
            hst_ref[s, :, ns:] = carry[i][1]

    def y_stage(slabs):
        for s in slabs:
            for r in range(0, rows, row_block):
                h16 = hbuf_ref[s, nb + r:nb + r + row_block, :].astype(bf16)
                ys = jnp.dot(h16, cblk_ref[s], preferred_element_type=f32)
                ys = ys + dsk_ref[:, lanes(s)] * ut_ref[r:r + row_block, lanes(s)]
                g_ref[r:r + row_block, lanes(s)] = _gelu(ys)
                yield

    half = S5_SLABS // 2
    first, second = tuple(range(half)), tuple(range(half, S5_SLABS))
    _interleave(bu_stage(first))
    _interleave(scan_stage(first), bu_stage(second))
    _interleave(scan_stage(second), y_stage(first))
    _interleave(y_stage(second))


def _s5_scan_vmem(hbuf_ref, hst_ref, ar_ref, ai_ref, s, nb, rows):
    ns = S5_SLAB_STATE
    hbuf_ref[s, 0:nb, :] = hst_ref[s]
    ar, ai = ar_ref[s], ai_ref[s]

    def body(i, carry):
        j = pl.multiple_of(nb + i * SUBLANES, SUBLANES)
        prev = hbuf_ref[s, pl.ds(j - nb, SUBLANES), :]
        cur = hbuf_ref[s, pl.ds(j, SUBLANES), :]
        pr, pi = prev[:, :ns], prev[:, ns:]
        hbuf_ref[s, pl.ds(j, SUBLANES), :ns] = ar * pr - ai * pi + cur[:, :ns]
        hbuf_ref[s, pl.ds(j, SUBLANES), ns:] = ar * pi + ai * pr + cur[:, ns:]
        return carry

    lax.fori_loop(0, rows // SUBLANES, body, 0)
    hst_ref[s] = hbuf_ref[s, rows:rows + nb, :]


S5_N_IN = 10


def _s5_kernel(*refs, nb, tl, batch_major, n_cast):
    (u_ref, re0_ref, im0_ref, ar_ref, ai_ref, bblk_ref, cblk_ref, dsk_ref, wglu_ref,
     bglu_ref) = refs[:S5_N_IN]
    cast_in = refs[S5_N_IN:S5_N_IN + n_cast]
    y_ref, ren_ref, imn_ref = refs[S5_N_IN + n_cast:S5_N_IN + n_cast + 3]
    cast_out = refs[S5_N_IN + n_cast + 3:S5_N_IN + 2 * n_cast + 3]
    ut_ref, hbuf_ref, hst_ref, g_ref = refs[S5_N_IN + 2 * n_cast + 3:]
    for w_ref, w16_ref in zip(cast_in, cast_out):
        w16_ref[...] = w_ref[...].astype(bf16)
    step = pl.program_id(0)
    rows = nb * tl
    ns = S5_SLAB_STATE

    @pl.when(step == 0)
    def _():
        for s in range(S5_SLABS):
            hst_ref[s, :, :ns] = re0_ref[:, s * ns:(s + 1) * ns]
            hst_ref[s, :, ns:] = im0_ref[:, s * ns:(s + 1) * ns]

    if batch_major:
        ut_ref[...] = jnp.swapaxes(u_ref[...], 0, 1).reshape(rows, D_S5)
    else:
        ut_ref[...] = u_ref[...].reshape(rows, D_S5)

    if nb == SUBLANES:
        _s5_slabs_interleaved(ut_ref, hbuf_ref, hst_ref, g_ref, ar_ref, ai_ref, bblk_ref, cblk_ref, dsk_ref, tl)
    else:
        for s in range(S5_SLABS):
            sl = slice(s * LANES, (s + 1) * LANES)
            hbuf_ref[s, nb:nb + rows, :] = jnp.dot(ut_ref[:, sl].astype(bf16), bblk_ref[s],
                                                   preferred_element_type=f32)
            _s5_scan_vmem(hbuf_ref, hst_ref, ar_ref, ai_ref, s, nb, rows)
            ys = jnp.dot(hbuf_ref[s, nb:nb + rows, :].astype(bf16), cblk_ref[s], preferred_element_type=f32)
            g_ref[:, sl] = _gelu(ys + dsk_ref[:, sl] * ut_ref[:, sl])

    g = g_ref[...]
    gate = jnp.dot(g.astype(bf16), wglu_ref[...], preferred_element_type=f32) + bglu_ref[...]
    out = g * jax.nn.sigmoid(gate)
    if batch_major:
        y_ref[...] = jnp.swapaxes(out.reshape(tl, nb, D_S5), 0, 1)
    else:
        y_ref[...] = out.reshape(tl, nb, D_S5)

    @pl.when(step == pl.num_programs(0) - 1)
    def _():
        for s in range(S5_SLABS):
            ren_ref[:, s * ns:(s + 1) * ns] = hst_ref[s, :, :ns]
            imn_ref[:, s * ns:(s + 1) * ns] = hst_ref[s, :, ns:]


def _s5(u, re0, im0, ar, ai, bblk, cblk, dsk, wglu, bglu, *, tl, batch_major, cast=()):
    if batch_major:
        nb, L, _ = u.shape
        u_spec = pl.BlockSpec((nb, tl, D_S5), lambda i: (0, i, 0))
    else:
        L, nb, _ = u.shape
        u_spec = pl.BlockSpec((tl, nb, D_S5), lambda i: (i, 0, 0))
    rows = nb * tl
    steps = L // tl
    nstate = S5_GROUPS * S5_STATE
    st_spec = _const_spec((nb, nstate))
    cast_specs = [pl.BlockSpec((w.shape[0] // steps, w.shape[1]), lambda i: (i, 0)) for w in cast]
    assert all(w.shape[0] % (steps * 2 * SUBLANES) == 0 for w in cast)
    return pl.pallas_call(
        functools.partial(_s5_kernel, nb=nb, tl=tl, batch_major=batch_major, n_cast=len(cast)),
        grid=(steps,),
        in_specs=[u_spec, st_spec, st_spec,
                  _const_spec((S5_SLABS, SUBLANES, S5_SLAB_STATE)),
                  _const_spec((S5_SLABS, SUBLANES, S5_SLAB_STATE)),
                  _const_spec((S5_SLABS, LANES, 2 * S5_SLAB_STATE)),
                  _const_spec((S5_SLABS, 2 * S5_SLAB_STATE, LANES)),
                  _const_spec((1, D_S5)), _const_spec((D_S5, D_S5)), _const_spec((1, D_S5))] + cast_specs,
        out_specs=[u_spec, st_spec, st_spec] + cast_specs,
        out_shape=[jax.ShapeDtypeStruct(u.shape, f32),
                   jax.ShapeDtypeStruct((nb, nstate), f32),
                   jax.ShapeDtypeStruct((nb, nstate), f32)]
                  + [jax.ShapeDtypeStruct(w.shape, bf16) for w in cast],
        scratch_shapes=[pltpu.VMEM((rows, D_S5), f32),
                        pltpu.VMEM((S5_SLABS, nb + rows, 2 * S5_SLAB_STATE), f32),
                        pltpu.VMEM((S5_SLABS, nb, 2 * S5_SLAB_STATE), f32),
                        pltpu.VMEM((rows, D_S5), f32)],
        compiler_params=_params("arbitrary"),
        name="s5_mixer",
    )(u, re0, im0, ar, ai, bblk, cblk, dsk, wglu, bglu, *cast)


def _ffn_kernel(x_ref, ys_ref, y5_ref, mod_ref, modf_ref, n2g_ref, nfg_ref, wo_ref, wg_ref, wu_ref, wd_ref,
                o_ref, *, seq_major, ff_chunk):
    g1 = _mod_row(mod_ref, 2, seq_major)
    sh2 = _mod_row(mod_ref, 3, seq_major)
    sc2 = _mod_row(mod_ref, 4, seq_major)
    g2 = _mod_row(mod_ref, 5, seq_major)
    shf = _mod_row(modf_ref, 0, seq_major)
    scf = _mod_row(modf_ref, 1, seq_major)
    att = jnp.dot(ys_ref[0].astype(bf16), wo_ref[:D_SSD, :], preferred_element_type=f32)
    att = att + jnp.dot(y5_ref[0].astype(bf16), wo_ref[D_SSD:, :], preferred_element_type=f32)
    x1 = _rows_in(x_ref, seq_major) + _by_row(lambda v, s: v * s, att, g1)
    v = _rms_mod(x1, n2g_ref[...], sc2, sh2).astype(bf16)
    ff = None
    for o in range(0, D_FF, ff_chunk):
        gate = jnp.dot(v, wg_ref[:, o:o + ff_chunk], preferred_element_type=f32)
        up = jnp.dot(v, wu_ref[:, o:o + ff_chunk], preferred_element_type=f32)
        hid = (_silu(gate) * up).astype(bf16)
        part = jnp.dot(hid, wd_ref[o:o + ff_chunk, :], preferred_element_type=f32)
        ff = part if ff is None else ff + part
    x2 = x1 + _by_row(lambda v, s: v * s, ff, g2)
    y = _rms_mod(x2, nfg_ref[...], scf, shf)
    if seq_major:
        s, t, d = o_ref.shape
        o_ref[...] = jnp.swapaxes(y.reshape(t, s, d), 0, 1)
    else:
        o_ref[0] = y


def _ffn(x, ys, y5, mod, modf, n2g, nfg, wo, wg, wu, wd, *, tm, seq_major, n_seq, first_seq, ff_chunk=256):
    if seq_major:
        x_spec = _const_spec(x.shape)
        nb, rows, d = 1, x.shape[0] * x.shape[1], x.shape[2]
        assert tm == rows
    else:
        nb, rows, d = x.shape
        x_spec = pl.BlockSpec((1, tm, d), lambda i, j: (i, j, 0))
    nt = rows // tm
    mod_spec = _mod_spec(mod, n_seq, first_seq)
    modf_spec = _mod_spec(modf, n_seq, first_seq)
    blk = lambda wd_: pl.BlockSpec((1, tm, wd_), lambda i, j: (i, j, 0))
    single = dict(pipeline_mode=pl.Buffered(1))
    wspec = lambda shape: pl.BlockSpec(shape, lambda i, j: (0, 0), **single)
    return pl.pallas_call(
        functools.partial(_ffn_kernel, seq_major=seq_major, ff_chunk=ff_chunk),
        grid=(nb, nt),
        in_specs=[x_spec, blk(D_SSD), blk(D_S5), mod_spec, modf_spec,
                  _const_spec((1, d)), _const_spec((1, d)),
                  wspec((d, d)), wspec((d, D_FF)), wspec((d, D_FF)), wspec((D_FF, d))],
        out_specs=x_spec,
        out_shape=jax.ShapeDtypeStruct(x.shape, f32),
        compiler_params=_params("parallel", "parallel"),
        name="out_ffn",
    )(x, ys, y5, mod, modf, n2g, nfg, wo, wg, wu, wd)


def kernel(x_prompt, x_sample, c_prompt, c_sample, state_ssd, state_conv, state_s5_re, state_s5_im, w_ada, b_ada, norm1_g, w_in, conv_w, conv_b, ssd_dt_bias, ssd_A_log, ssd_D, ssd_norm_g, s5_A_re, s5_A_im, s5_log_step, s5_B_re, s5_B_im, s5_C_re, s5_C_im, s5_D, w_glu, b_glu, w_out, norm2_g, w_ffn_gate, w_ffn_up, w_ffn_down, w_ada_f, b_ada_f, normf_g):
    assert w_ada.shape[0] == 1, "single-layer stack"
    bp, seq, d = x_prompt.shape
    bs, dseq, _ = x_sample.shape

    c_all = jnp.concatenate([c_sample, c_prompt], axis=0)
    mod = _ada_mod(c_all, w_ada[0], b_ada[0])
    modf = _ada_mod(c_all, w_ada_f, b_ada_f)
    seqs_s = dict(n_seq=bs, first_seq=0)
    seqs_p = dict(n_seq=bp, first_seq=bs)

    o2, o3 = D_SSD + CONV_DIM, D_SSD + CONV_DIM + HEADS
    wi = w_in[0]
    w_in_p = jnp.concatenate([wi[:, :o2], wi[:, o3:], wi[:, o2:o3], jnp.zeros((d, DT_PAD - HEADS), f32)], axis=1)
    pad_h = lambda v: jnp.concatenate([v, jnp.zeros((DT_PAD - HEADS,), f32)]).reshape(1, DT_PAD)
    dtb = pad_h(ssd_dt_bias[0])
    alog = pad_h(ssd_A_log[0])
    dexp = jnp.repeat(ssd_D[0], HEAD_DIM).reshape(1, D_SSD)
    ng = ssd_norm_g[0].reshape(1, D_SSD)
    cw, cb = conv_w[0], conv_b[0].reshape(1, CONV_DIM)

    ar, ai, bblk, cblk = _s5_params(s5_A_re[0], s5_A_im[0], s5_log_step[0],
                                    s5_B_re[0].transpose(0, 2, 1), s5_B_im[0].transpose(0, 2, 1),
                                    s5_C_re[0], s5_C_im[0])
    dsk = s5_D[0].reshape(1, D_S5)
    wglu = w_glu[0].astype(bf16)
    bglu = b_glu[0].reshape(1, D_S5)
    n1g, n2g, nfg = norm1_g[0].reshape(1, d), norm2_g[0].reshape(1, d), normf_g.reshape(1, d)
    nstate = S5_GROUPS * S5_STATE

    z, xbc, u5, dt = _inproj(x_prompt, mod, n1g, w_in_p, tm=512, seq_major=False, **seqs_p)
    y_ssd, ssd_p, conv_p = _ssd_prompt(xbc, z, dt, cw, cb, dtb, alog, dexp, ng)
    zeros_st = jnp.zeros((bp, nstate), f32)
    y_s5, re_p, im_p, wo, wg, wu, wd = _s5(
        u5, zeros_st, zeros_st, ar, ai, bblk, cblk, dsk, wglu, bglu, tl=128, batch_major=True,
        cast=(w_out[0], w_ffn_gate[0], w_ffn_up[0], w_ffn_down[0]))
    y_prompt = _ffn(x_prompt, y_ssd, y_s5, mod, modf, n2g, nfg, wo, wg, wu, wd, tm=512, seq_major=False, **seqs_p)

    rows_s = dseq * bs
    steps = lambda a: a.reshape(dseq, bs, a.shape[-1])
    flat = lambda a: a.reshape(1, rows_s, a.shape[-1])
    z, xbc, u5, dt = _inproj(x_sample, mod, n1g, w_in_p, tm=rows_s, seq_major=True, **seqs_s)
    y_ssd, ssd_s, conv_s = _ssd_sample(steps(xbc), steps(z), steps(dt), state_conv[0].transpose(1, 0, 2),
                                       state_ssd[0], cw, cb, dtb, alog, dexp, ng)
    y_s5, re_s, im_s = _s5(steps(u5), state_s5_re[0].reshape(bs, nstate), state_s5_im[0].reshape(bs, nstate),
                           ar, ai, bblk, cblk, dsk, wglu, bglu, tl=dseq, batch_major=False)
    y_sample = _ffn(x_sample, flat(y_ssd), flat(y_s5), mod, modf, n2g, nfg, wo, wg, wu, wd,
                    tm=rows_s, seq_major=True, **seqs_s)

    g5 = (S5_GROUPS, S5_STATE)
    return (y_prompt, y_sample,
            ssd_p[None], ssd_s.reshape((1,) + state_ssd.shape[1:]),
            conv_p[None], conv_s.transpose(1, 0, 2)[None],
            re_p.reshape((1, bp) + g5), re_s.reshape((1, bs) + g5),
            im_p.reshape((1, bp) + g5), im_s.reshape((1, bs) + g5))
```

```python
import functools

import jax
import jax.numpy as jnp
from jax import lax
from jax.experimental import pallas as pl
from jax.experimental.pallas import tpu as pltpu

f32 = jnp.float32
bf16 = jnp.bfloat16

D_MODEL = 1024
D_SSD = 512
HEAD_DIM = 64
HEADS = 8
GROUPS = 2
HEADS_PER_GROUP = HEADS // GROUPS
STATE = 128
CONV_W = 4
CONV_DIM = D_SSD + 2 * GROUPS * STATE
D_S5 = 512
S5_CH = 16
S5_GROUPS = 32
S5_STATE = 64
D_FF = 2816
N_ADA = 6
EPS = 1e-6

LANES = 128
SUBLANES = 8
SSD_CHUNK = 128
S5_SLABS = D_S5 // LANES
S5_SLAB_STATE = (S5_GROUPS // S5_SLABS) * S5_STATE
DT_PAD = LANES
VMEM_LIMIT = 56 * 1024 * 1024

NT_DIMS = (((1,), (1,)), ((), ()))
TN_DIMS = (((0,), (0,)), ((), ()))


def _silu(x):
    return x * jax.nn.sigmoid(x)


def _by_row(fn, v, *ms):
    r, d = v.shape
    m_rows = ms[0].shape[0]
    if m_rows in (1, r):
        return fn(v, *ms)
    out = fn(v.reshape(r // m_rows, m_rows, d), *[m[None] for m in ms])
    return out.reshape(r, d)


def _rms_mod(x, g, sc, sh):
    y = x * lax.rsqrt(jnp.mean(x * x, axis=-1, keepdims=True) + EPS)
    return _by_row(lambda v, s, t: v * (1.0 + s) + t, y * g, sc, sh)


def _mod_row(mod_ref, i, per_row):
    return mod_ref[i] if per_row else mod_ref[i, pl.ds(pl.program_id(0), 1), :]


def _mod_spec(mod, n_seq, first_seq):
    assert first_seq % n_seq == 0
    return pl.BlockSpec((mod.shape[0], n_seq, mod.shape[2]), lambda *_: (0, first_seq // n_seq, 0))


def _const_spec(shape):
    nd = len(shape)
    return pl.BlockSpec(shape, lambda *_: (0,) * nd)


def _params(*sem):
    return pltpu.CompilerParams(dimension_semantics=sem, vmem_limit_bytes=VMEM_LIMIT)


def _ada_kernel(c_ref, w_ref, b_ref, o_ref):
    s = _silu(c_ref[...]).astype(bf16)
    o_ref[0] = jnp.dot(s, w_ref[...].astype(bf16), preferred_element_type=f32) + b_ref[...]


def _ada_mod(c, w, b):
    m, k = c.shape
    n = w.shape[1] // k
    return pl.pallas_call(
        _ada_kernel,
        grid=(n,),
        in_specs=[pl.BlockSpec((m, k), lambda j: (0, 0)),
                  pl.BlockSpec((k, k), lambda j: (0, j)),
                  pl.BlockSpec((1, k), lambda j: (0, j))],
        out_specs=pl.BlockSpec((1, m, k), lambda j: (j, 0, 0)),
        out_shape=jax.ShapeDtypeStruct((n, m, k), f32),
        compiler_params=_params("parallel"),
        name="ada_mod",
    )(c, w, b.reshape(1, n * k))


def _rows_in(x_ref, seq_major):
    if not seq_major:
        return x_ref[0]
    s, t, d = x_ref.shape
    return jnp.swapaxes(x_ref[...], 0, 1).reshape(s * t, d)


def _inproj_kernel(x_ref, mod_ref, g_ref, w_ref, z_ref, xbc_ref, u5_ref, dt_ref, *, seq_major):
    sh = _mod_row(mod_ref, 0, seq_major)
    sc = _mod_row(mod_ref, 1, seq_major)
    u = _rms_mod(_rows_in(x_ref, seq_major), g_ref[...], sc, sh).astype(bf16)
    o_dt = D_SSD + CONV_DIM
    for ref, o, width in ((z_ref, 0, D_SSD), (xbc_ref, D_SSD, CONV_DIM), (u5_ref, o_dt + HEADS, D_S5),
                          (dt_ref, o_dt, DT_PAD)):
        ref[0] = lax.dot_general(u, w_ref[o:o + width, :].astype(bf16), NT_DIMS, preferred_element_type=f32)


def _inproj(x, mod, g, w, *, tm, seq_major, n_seq, first_seq):
    if seq_major:
        x_spec = _const_spec(x.shape)
        nb, rows, d = 1, x.shape[0] * x.shape[1], x.shape[2]
        assert tm == rows
    else:
        nb, rows, d = x.shape
        x_spec = pl.BlockSpec((1, tm, d), lambda i, j: (i, j, 0))
    nt = rows // tm
    widths = (D_SSD, CONV_DIM, D_S5, DT_PAD)
    return pl.pallas_call(
        functools.partial(_inproj_kernel, seq_major=seq_major),
        grid=(nb, nt),
        in_specs=[x_spec,
                  _mod_spec(mod, n_seq, first_seq),
                  _const_spec((1, d)),
                  pl.BlockSpec(w.shape, lambda i, j: (0, 0), pipeline_mode=pl.Buffered(1))],
        out_specs=[pl.BlockSpec((1, tm, wd), lambda i, j: (i, j, 0)) for wd in widths],
        out_shape=[jax.ShapeDtypeStruct((nb, rows, wd), f32) for wd in widths],
        compiler_params=_params("parallel", "parallel"),
        name="in_proj",
    )(x, mod, g, w)


def _split3(x):
    hi = x.astype(bf16)
    r1 = x - hi.astype(f32)
    mid = r1.astype(bf16)
    lo = (r1 - mid.astype(f32)).astype(bf16)
    return hi, mid, lo


def _dot_sel_lhs(sel16, x):
    return sum(jnp.dot(sel16, p, preferred_element_type=f32) for p in _split3(x))


def _dot_sel_rhs(x, sel16):
    return sum(jnp.dot(p, sel16, preferred_element_type=f32) for p in _split3(x))


def _gated_group_norm(y, z, ng):
    y = y * _silu(z)
    gw = D_SSD // GROUPS
    parts = []
    for g in range(GROUPS):
        yg = y[:, g * gw:(g + 1) * gw]
        parts.append(yg * lax.rsqrt(jnp.mean(yg * yg, axis=-1, keepdims=True) + EPS))
    return jnp.concatenate(parts, axis=-1) * ng


def _ssd_prompt_kernel(xbc_ref, z_ref, dt_ref, cw_ref, cb_ref, dtb_ref, alog_ref, dexp_ref, ng_ref,
                       y_ref, st_ref, cn_ref, ext_ref, act_ref, h_ref, *, cps):
    T = SSD_CHUNK
    gw = HEADS_PER_GROUP * HEAD_DIM

    @pl.when(pl.program_id(1) == 0)
    def _():
        ext_ref[0:SUBLANES, :] = jnp.zeros((SUBLANES, CONV_DIM), f32)
        h_ref[...] = jnp.zeros_like(h_ref)

    row = lax.broadcasted_iota(jnp.int32, (T, T), 0)
    col = lax.broadcasted_iota(jnp.int32, (T, T), 1)
    tri = row >= col
    tri16 = jnp.where(tri, 1.0, 0.0).astype(bf16)
    low_half = lax.broadcasted_iota(jnp.int32, (T, LANES), 1) < HEAD_DIM
    a_neg = -jnp.exp(alog_ref[...])

    R = cps * T
    ext_ref[SUBLANES:SUBLANES + R, :] = xbc_ref[0]
    for ci in range(cps):
        ext = ext_ref[ci * T:ci * T + SUBLANES + T, :]
        conv = cb_ref[...] + ext[SUBLANES:, :] * cw_ref[CONV_W - 1:CONV_W, :]
        for m in range(1, CONV_W):
            k = CONV_W - 1 - m
            conv = conv + pltpu.roll(ext, m, 0)[SUBLANES:, :] * cw_ref[k:k + 1, :]
        act_ref[ci * T:(ci + 1) * T, :] = _silu(conv)
    cn_ref[0] = ext_ref[SUBLANES + R - (CONV_W - 1):SUBLANES + R, :]
    ext_ref[0:SUBLANES, :] = ext_ref[R:R + SUBLANES, :]

    lane_head = lax.broadcasted_iota(jnp.int32, (DT_PAD, D_SSD), 1) // HEAD_DIM
    sel16 = jnp.where(lax.broadcasted_iota(jnp.int32, (DT_PAD, D_SSD), 0) == lane_head, 1.0, 0.0).astype(bf16)

    def spread(q):
        return _dot_sel_rhs(q, sel16)

    def state_free(ci):
        r0 = ci * T
        act = act_ref[r0:r0 + T, :]
        xs = act[:, :D_SSD]
        bm = act[:, D_SSD:D_SSD + GROUPS * STATE]
        cm = act[:, D_SSD + GROUPS * STATE:]
        dtv = jax.nn.softplus(dt_ref[0, r0:r0 + T, :] + dtb_ref[...])
        a_cs = _dot_sel_lhs(tri16, dtv * a_neg)
        bg16s = [bm[:, g * STATE:(g + 1) * STATE].astype(bf16) for g in range(GROUPS)]
        cg16s = [cm[:, g * STATE:(g + 1) * STATE].astype(bf16) for g in range(GROUPS)]
        gmats = [lax.dot_general(cg16s[g], bg16s[g], NT_DIMS, preferred_element_type=f32)
                 for g in range(GROUPS)]
        yield
        a_last = a_cs[T - 1:T, :]
        a_cs_t = a_cs.T
        rep = spread(jnp.concatenate([dtv, jnp.exp(a_last - a_cs), jnp.exp(a_cs)], axis=0))
        yield
        x = xs * rep[:T]
        x16 = x.astype(bf16)
        xd16 = (x * rep[T:2 * T]).astype(bf16)
        e_cs = rep[2 * T:]
        yield
        y_part = []
        for g in range(GROUPS):
            for pr in range(HEADS_PER_GROUP // 2):
                s_pair = []
                for q in range(2):
                    h = g * HEADS_PER_GROUP + 2 * pr + q
                    seg = a_cs[:, h:h + 1] - a_cs_t[h:h + 1, :]
                    lmat = jnp.exp(jnp.where(tri, seg, -jnp.inf))
                    s_pair.append((gmats[g] * lmat).astype(bf16))
                lanes = slice(g * gw + pr * LANES, g * gw + (pr + 1) * LANES)
                both = jnp.dot(jnp.concatenate(s_pair, axis=0), x16[:, lanes], preferred_element_type=f32)
                yield
                y_part.append(jnp.where(low_half, both[:T], both[T:]))
        y_free = jnp.concatenate(y_part, axis=-1) + xs * dexp_ref[...]
        return y_free, e_cs, jnp.exp(a_last), xd16, bg16s, cg16s

    def state_step(ci, y_free, e_cs, e_last, xd16, bg16s, cg16s):
        r0 = ci * T
        y_offs = []
        for g in range(GROUPS):
            hp = h_ref[g]
            y_offs.append(lax.dot_general(cg16s[g], hp.astype(bf16), NT_DIMS, preferred_element_type=f32))
            upd = lax.dot_general(xd16[:, g * gw:(g + 1) * gw], bg16s[g], TN_DIMS, preferred_element_type=f32)
            for hh in range(HEADS_PER_GROUP):
                h = g * HEADS_PER_GROUP + hh
                rows = slice(hh * HEAD_DIM, (hh + 1) * HEAD_DIM)
                h_ref[g, rows, :] = e_last[:, h:h + 1] * hp[rows] + upd[rows]
        y = y_free + jnp.concatenate(y_offs, axis=-1) * e_cs
        y_ref[0, r0:r0 + T, :] = _gated_group_norm(y, z_ref[0, r0:r0 + T, :], ng_ref[...])

    gens = [state_free(ci) for ci in range(cps)]
    free = [None] * cps
    while any(f is None for f in free):
        for ci in range(cps):
            if free[ci] is None:
                try:
                    next(gens[ci])
                except StopIteration as done:
                    free[ci] = done.value
    for ci in range(cps):
        state_step(ci, *free[ci])

    @pl.when(pl.program_id(1) == pl.num_programs(1) - 1)
    def _():
        st_ref[0] = h_ref[...].reshape(HEADS, HEAD_DIM, STATE)


def _ssd_prompt(xbc, z, dt, cw, cb, dtb, alog, dexp, ng, *, cps=8):
    nb, L, _ = xbc.shape
    T = SSD_CHUNK
    rows = cps * T
    blk = lambda wd: pl.BlockSpec((1, rows, wd), lambda b, c: (b, c, 0))
    return pl.pallas_call(
        functools.partial(_ssd_prompt_kernel, cps=cps),
        grid=(nb, L // rows),
        in_specs=[blk(CONV_DIM), blk(D_SSD), blk(DT_PAD),
                  _const_spec((CONV_W, CONV_DIM)), _const_spec((1, CONV_DIM)),
                  _const_spec((1, DT_PAD)), _const_spec((1, DT_PAD)),
                  _const_spec((1, D_SSD)), _const_spec((1, D_SSD))],
        out_specs=[blk(D_SSD),
                   pl.BlockSpec((1, HEADS, HEAD_DIM, STATE), lambda b, c: (b, 0, 0, 0)),
                   pl.BlockSpec((1, CONV_W - 1, CONV_DIM), lambda b, c: (b, 0, 0))],
        out_shape=[jax.ShapeDtypeStruct((nb, L, D_SSD), f32),
                   jax.ShapeDtypeStruct((nb, HEADS, HEAD_DIM, STATE), f32),
                   jax.ShapeDtypeStruct((nb, CONV_W - 1, CONV_DIM), f32)],
        scratch_shapes=[pltpu.VMEM((SUBLANES + rows, CONV_DIM), f32),
                        pltpu.VMEM((rows, CONV_DIM), f32),
                        pltpu.VMEM((GROUPS, HEADS_PER_GROUP * HEAD_DIM, STATE), f32)],
        compiler_params=_params("parallel", "arbitrary"),
        name="ssd_prompt",
    )(xbc, z, dt, cw, cb, dtb, alog, dexp, ng)


def _ssd_sample_kernel(xbc_ref, z_ref, dt_ref, cbuf_ref, st_ref, cw_ref, cb_ref, dtb_ref, alog_ref,
                       dexp_ref, ng_ref, y_ref, stn_ref, cn_ref, dtot_ref, *, L, nb):
    gw = HEADS_PER_GROUP * HEAD_DIM
    full = [cbuf_ref[j] for j in range(CONV_W - 1)] + [xbc_ref[t] for t in range(L)]
    for j in range(CONV_W - 1):
        cn_ref[j] = full[L + j]
    a_neg = -jnp.exp(alog_ref[...])
    xs, bm, cm, dtv, acs = [], [], [], [], []
    run = None
    for t in range(L):
        conv = cb_ref[...]
        for k in range(CONV_W):
            conv = conv + full[t + k] * cw_ref[k:k + 1, :]
        act = _silu(conv)
        xs.append(act[:, :D_SSD])
        bm.append(act[:, D_SSD:D_SSD + GROUPS * STATE])
        cm.append(act[:, D_SSD + GROUPS * STATE:])
        d = jax.nn.softplus(dt_ref[t] + dtb_ref[...])
        dtv.append(d)
        run = d * a_neg if run is None else run + d * a_neg
        acs.append(run)
    a_tot = acs[L - 1]
    dtot_ref[...] = jnp.exp(a_tot)

    lane_head = lax.broadcasted_iota(jnp.int32, (DT_PAD, D_SSD), 1) // HEAD_DIM
    sel16 = jnp.where(lax.broadcasted_iota(jnp.int32, (DT_PAD, D_SSD), 0) == lane_head, 1.0, 0.0).astype(bf16)
    pairs = [(t, s) for t in range(L) for s in range(t)]
    factors = (dtv + [jnp.exp(a_tot - acs[t]) for t in range(L)] + [jnp.exp(acs[t]) for t in range(L)]
               + [jnp.exp(acs[t] - acs[s]) for t, s in pairs])
    rep = _dot_sel_rhs(jnp.concatenate(factors, axis=0), sel16)
    piece = lambda i: rep[i * nb:(i + 1) * nb]
    x = [xs[t] * piece(t) for t in range(L)]
    xd_stack = jnp.concatenate([x[t] * piece(L + t) for t in range(L)], axis=0)
    e_cs = [piece(2 * L + t) for t in range(L)]
    decay = {ts: piece(3 * L + i) for i, ts in enumerate(pairs)}

    in_group0 = lax.broadcasted_iota(jnp.int32, (nb, D_SSD), 1) < gw
    y_intra = []
    for t in range(L):
        acc = None
        for s in range(t + 1):
            cb_dot = [jnp.sum(cm[t][:, g * STATE:(g + 1) * STATE] * bm[s][:, g * STATE:(g + 1) * STATE],
                              axis=-1, keepdims=True) for g in range(GROUPS)]
            w = jnp.where(in_group0, cb_dot[0], cb_dot[1])
            term = w * x[s] if s == t else w * decay[(t, s)] * x[s]
            acc = term if acc is None else acc + term
        y_intra.append(acc)

    c_stack = [jnp.concatenate([cm[t][:, g * STATE:(g + 1) * STATE] for t in range(L)], axis=0).astype(bf16)
               for g in range(GROUPS)]
    b_stack = [jnp.concatenate([bm[t][:, g * STATE:(g + 1) * STATE] for t in range(L)], axis=0).astype(bf16)
               for g in range(GROUPS)]
    seq_of_row = lax.broadcasted_iota(jnp.int32, (L * nb, gw), 0) & (nb - 1)

    def per_seq(b, yoff):
        mine = seq_of_row == b
        drow = dtot_ref[pl.ds(b, 1), :]
        out = []
        for g in range(GROUPS):
            h0 = st_ref[b, g]
            r = lax.dot_general(c_stack[g], h0.astype(bf16), NT_DIMS, preferred_element_type=f32)
            out.append(yoff[g] + jnp.where(mine, r, 0.0))
            xm = jnp.where(mine, xd_stack[:, g * gw:(g + 1) * gw], 0.0).astype(bf16)
            upd = lax.dot_general(xm, b_stack[g], TN_DIMS, preferred_element_type=f32)
            for hh in range(HEADS_PER_GROUP):
                h = g * HEADS_PER_GROUP + hh
                rows = slice(hh * HEAD_DIM, (hh + 1) * HEAD_DIM)
                stn_ref[b, g, rows, :] = drow[:, h:h + 1] * h0[rows] + upd[rows]
        return tuple(out)

    yoff = lax.fori_loop(0, nb, per_seq, tuple(jnp.zeros((L * nb, gw), f32) for _ in range(GROUPS)),
                         unroll=True)

    for t in range(L):
        y_off = jnp.concatenate([yoff[g][t * nb:(t + 1) * nb] for g in range(GROUPS)], axis=-1)
        y = y_intra[t] + y_off * e_cs[t] + xs[t] * dexp_ref[...]
        y_ref[t] = _gated_group_norm(y, z_ref[t], ng_ref[...])


def _ssd_sample(xbc, z, dt, cbuf, st, cw, cb, dtb, alog, dexp, ng, *, nb=8):
    L, B, _ = xbc.shape
    assert nb & (nb - 1) == 0 and B % nb == 0
    tblk = lambda n, wd: pl.BlockSpec((n, nb, wd), lambda i: (0, i, 0))
    gw = HEADS_PER_GROUP * HEAD_DIM
    st = st.reshape(B, GROUPS, gw, STATE)
    st_spec = pl.BlockSpec((nb, GROUPS, gw, STATE), lambda i: (i, 0, 0, 0))
    return pl.pallas_call(
        functools.partial(_ssd_sample_kernel, L=L, nb=nb),
        grid=(B // nb,),
        in_specs=[tblk(L, CONV_DIM), tblk(L, D_SSD), tblk(L, DT_PAD), tblk(CONV_W - 1, CONV_DIM), st_spec,
                  _const_spec((CONV_W, CONV_DIM)), _const_spec((1, CONV_DIM)),
                  _const_spec((1, DT_PAD)), _const_spec((1, DT_PAD)),
                  _const_spec((1, D_SSD)), _const_spec((1, D_SSD))],
        out_specs=[tblk(L, D_SSD), st_spec, tblk(CONV_W - 1, CONV_DIM)],
        out_shape=[jax.ShapeDtypeStruct((L, B, D_SSD), f32),
                   jax.ShapeDtypeStruct((B, GROUPS, gw, STATE), f32),
                   jax.ShapeDtypeStruct((CONV_W - 1, B, CONV_DIM), f32)],
        scratch_shapes=[pltpu.VMEM((nb, DT_PAD), f32)],
        compiler_params=_params("parallel"),
        name="ssd_sample",
    )(xbc, z, dt, cbuf, st, cw, cb, dtb, alog, dexp, ng)


def _s5_param_kernel(lr_ref, li_ref, ls_ref, br_ref, bi_ref, cr_ref, ci_ref,
                     ar_ref, ai_ref, bblk_ref, cblk_ref, b_scr, c_scr):
    lr, li = lr_ref[...], li_ref[...]
    step = jnp.exp(ls_ref[...])
    mag = jnp.exp(lr * step)
    abr = mag * jnp.cos(li * step)
    abi = mag * jnp.sin(li * step)
    nr, ni = abr - 1.0, abi
    den = lr * lr + li * li
    fr = (nr * lr + ni * li) / den
    fi = (ni * lr - nr * li) / den
    br, bi = br_ref[...], bi_ref[...]
    bbr = fr * br - fi * bi
    bbi = fr * bi + fi * br
    b_scr[...] = jnp.zeros_like(b_scr)
    c_scr[...] = jnp.zeros_like(c_scr)
    gps = S5_GROUPS // S5_SLABS
    ns = S5_SLAB_STATE
    for g in range(S5_GROUPS):
        s, gl = divmod(g, gps)
        ch = slice(gl * S5_CH, (gl + 1) * S5_CH)
        st = slice(gl * S5_STATE, (gl + 1) * S5_STATE)
        st_im = slice(ns + gl * S5_STATE, ns + (gl + 1) * S5_STATE)
        ar_ref[s, :, st] = jnp.broadcast_to(abr[g], (SUBLANES, S5_STATE))
        ai_ref[s, :, st] = jnp.broadcast_to(abi[g], (SUBLANES, S5_STATE))
        b_scr[s, ch, st] = bbr[g]
        b_scr[s, ch, st_im] = bbi[g]
        c_scr[s, st, ch] = cr_ref[g].T
        c_scr[s, st_im, ch] = -ci_ref[g].T
    bblk_ref[...] = b_scr[...].astype(bf16)
    cblk_ref[...] = c_scr[...].astype(bf16)


def _s5_params(a_re, a_im, log_step, b_re_t, b_im_t, c_re, c_im):
    g, p = a_re.shape
    ns = S5_SLAB_STATE
    return pl.pallas_call(
        _s5_param_kernel,
        out_shape=[jax.ShapeDtypeStruct((S5_SLABS, SUBLANES, ns), f32),
                   jax.ShapeDtypeStruct((S5_SLABS, SUBLANES, ns), f32),
                   jax.ShapeDtypeStruct((S5_SLABS, LANES, 2 * ns), bf16),
                   jax.ShapeDtypeStruct((S5_SLABS, 2 * ns, LANES), bf16)],
        scratch_shapes=[pltpu.VMEM((S5_SLABS, LANES, 2 * ns), f32),
                        pltpu.VMEM((S5_SLABS, 2 * ns, LANES), f32)],
        name="s5_params",
    )(a_re.reshape(g, 1, p), a_im.reshape(g, 1, p), log_step.reshape(g, 1, 1), b_re_t, b_im_t, c_re, c_im)


def _gelu(x):
    return 0.5 * x * (1.0 + lax.erf(x * (2.0 ** -0.5)))


def _interleave(*gens):
    live = list(gens)
    while live:
        for g in list(live):
            try:
                next(g)
            except StopIteration:
                live.remove(g)


def _s5_slabs_interleaved(ut_ref, hbuf_ref, hst_ref, g_ref, ar_ref, ai_ref, bblk_ref, cblk_ref, dsk_ref, tl,
                          row_block=256, n_stages=8):
    ns = S5_SLAB_STATE
    nb = SUBLANES
    rows = nb * tl

    def lanes(s):
        return slice(s * LANES, (s + 1) * LANES)

    def bu_stage(slabs):
        for s in slabs:
            for r in range(0, rows, row_block):
                us = ut_ref[r:r + row_block, lanes(s)]
                hbuf_ref[s, nb + r:nb + r + row_block, :] = jnp.dot(us.astype(bf16), bblk_ref[s],
                                                                    preferred_element_type=f32)
                yield

    def scan_stage(slabs):
        carry = [(hst_ref[s, :, :ns], hst_ref[s, :, ns:]) for s in slabs]
        coef = [(ar_ref[s], ai_ref[s]) for s in slabs]
        for t in range(tl):
            r = nb + nb * t
            for i, s in enumerate(slabs):
                (pr, pi), (ar, ai) = carry[i], coef[i]
                nr = ar * pr - ai * pi + hbuf_ref[s, r:r + nb, :ns]
                ni = ar * pi + ai * pr + hbuf_ref[s, r:r + nb, ns:]
                hbuf_ref[s, r:r + nb, :ns] = nr
                hbuf_ref[s, r:r + nb, ns:] = ni
                carry[i] = (nr, ni)
            if (t + 1) % (tl // n_stages) == 0:
                yield
        for i, s in enumerate(slabs):
            hst_ref[s, :, :ns] = carry[i][0]
            hst_ref[s, :, ns:] = carry[i][1]

    def y_stage(slabs):
        for s in slabs:
            for r in range(0, rows, row_block):
                h16 = hbuf_ref[s, nb + r:nb + r + row_block, :].astype(bf16)
                ys = jnp.dot(h16, cblk_ref[s], preferred_element_type=f32)
                ys = ys + dsk_ref[:, lanes(s)] * ut_ref[r:r + row_block, lanes(s)]
                g_ref[r:r + row_block, lanes(s)] = _gelu(ys)
                yield

    half = S5_SLABS // 2
    first, second = tuple(range(half)), tuple(range(half, S5_SLABS))
    _interleave(bu_stage(first))
    _interleave(scan_stage(first), bu_stage(second))
    _interleave(scan_stage(second), y_stage(first))
    _interleave(y_stage(second))


def _s5_scan_vmem(hbuf_ref, hst_ref, ar_ref, ai_ref, s, nb, rows):
    ns = S5_SLAB_STATE
    hbuf_ref[s, 0:nb, :] = hst_ref[s]
    ar, ai = ar_ref[s], ai_ref[s]

    def body(i, carry):
        j = pl.multiple_of(nb + i * SUBLANES, SUBLANES)
        prev = hbuf_ref[s, pl.ds(j - nb, SUBLANES), :]
        cur = hbuf_ref[s, pl.ds(j, SUBLANES), :]
        pr, pi = prev[:, :ns], prev[:, ns:]
        hbuf_ref[s, pl.ds(j, SUBLANES), :ns] = ar * pr - ai * pi + cur[:, :ns]
        hbuf_ref[s, pl.ds(j, SUBLANES), ns:] = ar * pi + ai * pr + cur[:, ns:]
        return carry

    lax.fori_loop(0, rows // SUBLANES, body, 0)
    hst_ref[s] = hbuf_ref[s, rows:rows + nb, :]


S5_N_IN = 10


def _s5_kernel(*refs, nb, tl, batch_major, n_cast):
    (u_ref, re0_ref, im0_ref, ar_ref, ai_ref, bblk_ref, cblk_ref, dsk_ref, wglu_ref,
     bglu_ref) = refs[:S5_N_IN]
    cast_in = refs[S5_N_IN:S5_N_IN + n_cast]
    y_ref, ren_ref, imn_ref = refs[S5_N_IN + n_cast:S5_N_IN + n_cast + 3]
    cast_out = refs[S5_N_IN + n_cast + 3:S5_N_IN + 2 * n_cast + 3]
    ut_ref, hbuf_ref, hst_ref, g_ref = refs[S5_N_IN + 2 * n_cast + 3:]
    for w_ref, w16_ref in zip(cast_in, cast_out):
        w16_ref[...] = w_ref[...].astype(bf16)
    step = pl.program_id(0)
    rows = nb * tl
    ns = S5_SLAB_STATE

    @pl.when(step == 0)
    def _():
        for s in range(S5_SLABS):
            hst_ref[s, :, :ns] = re0_ref[:, s * ns:(s + 1) * ns]
            hst_ref[s, :, ns:] = im0_ref[:, s * ns:(s + 1) * ns]

    if batch_major:
        ut_ref[...] = jnp.swapaxes(u_ref[...], 0, 1).reshape(rows, D_S5)
    else:
        ut_ref[...] = u_ref[...].reshape(rows, D_S5)

    if nb == SUBLANES:
        _s5_slabs_interleaved(ut_ref, hbuf_ref, hst_ref, g_ref, ar_ref, ai_ref, bblk_ref, cblk_ref, dsk_ref, tl)
    else:
        for s in range(S5_SLABS):
            sl = slice(s * LANES, (s + 1) * LANES)
            hbuf_ref[s, nb:nb + rows, :] = jnp.dot(ut_ref[:, sl].astype(bf16), bblk_ref[s],
                                                   preferred_element_type=f32)
            _s5_scan_vmem(hbuf_ref, hst_ref, ar_ref, ai_ref, s, nb, rows)
            ys = jnp.dot(hbuf_ref[s, nb:nb + rows, :].astype(bf16), cblk_ref[s], preferred_element_type=f32)
            g_ref[:, sl] = _gelu(ys + dsk_ref[:, sl] * ut_ref[:, sl])

    g = g_ref[...]
    gate = jnp.dot(g.astype(bf16), wglu_ref[...], preferred_element_type=f32) + bglu_ref[...]
    out = g * jax.nn.sigmoid(gate)
    if batch_major:
        y_ref[...] = jnp.swapaxes(out.reshape(tl, nb, D_S5), 0, 1)
    else:
        y_ref[...] = out.reshape(tl, nb, D_S5)

    @pl.when(step == pl.num_programs(0) - 1)
    def _():
        for s in range(S5_SLABS):
            ren_ref[:, s * ns:(s + 1) * ns] = hst_ref[s, :, :ns]
            imn_ref[:, s * ns:(s + 1) * ns] = hst_ref[s, :, ns:]


def _s5(u, re0, im0, ar, ai, bblk, cblk, dsk, wglu, bglu, *, tl, batch_major, cast=()):
    if batch_major:
        nb, L, _ = u.shape
        u_spec = pl.BlockSpec((nb, tl, D_S5), lambda i: (0, i, 0))
    else:
        L, nb, _ = u.shape
        u_spec = pl.BlockSpec((tl, nb, D_S5), lambda i: (i, 0, 0))
    rows = nb * tl
    steps = L // tl
    nstate = S5_GROUPS * S5_STATE
    st_spec = _const_spec((nb, nstate))
    cast_specs = [pl.BlockSpec((w.shape[0] // steps, w.shape[1]), lambda i: (i, 0)) for w in cast]
    assert all(w.shape[0] % (steps * 2 * SUBLANES) == 0 for w in cast)
    return pl.pallas_call(
        functools.partial(_s5_kernel, nb=nb, tl=tl, batch_major=batch_major, n_cast=len(cast)),
        grid=(steps,),
        in_specs=[u_spec, st_spec, st_spec,
                  _const_spec((S5_SLABS, SUBLANES, S5_SLAB_STATE)),
                  _const_spec((S5_SLABS, SUBLANES, S5_SLAB_STATE)),
                  _const_spec((S5_SLABS, LANES, 2 * S5_SLAB_STATE)),
                  _const_spec((S5_SLABS, 2 * S5_SLAB_STATE, LANES)),
                  _const_spec((1, D_S5)), _const_spec((D_S5, D_S5)), _const_spec((1, D_S5))] + cast_specs,
        out_specs=[u_spec, st_spec, st_spec] + cast_specs,
        out_shape=[jax.ShapeDtypeStruct(u.shape, f32),
                   jax.ShapeDtypeStruct((nb, nstate), f32),
                   jax.ShapeDtypeStruct((nb, nstate), f32)]
                  + [jax.ShapeDtypeStruct(w.shape, bf16) for w in cast],
        scratch_shapes=[pltpu.VMEM((rows, D_S5), f32),
                        pltpu.VMEM((S5_SLABS, nb + rows, 2 * S5_SLAB_STATE), f32),
                        pltpu.VMEM((S5_SLABS, nb, 2 * S5_SLAB_STATE), f32),
                        pltpu.VMEM((rows, D_S5), f32)],
        compiler_params=_params("arbitrary"),
        name="s5_mixer",
    )(u, re0, im0, ar, ai, bblk, cblk, dsk, wglu, bglu, *cast)


def _ffn_kernel(x_ref, ys_ref, y5_ref, mod_ref, modf_ref, n2g_ref, nfg_ref, wo_ref, wg_ref, wu_ref, wd_ref,
                o_ref, *, seq_major, ff_chunk):
    g1 = _mod_row(mod_ref, 2, seq_major)
    sh2 = _mod_row(mod_ref, 3, seq_major)
    sc2 = _mod_row(mod_ref, 4, seq_major)
    g2 = _mod_row(mod_ref, 5, seq_major)
    shf = _mod_row(modf_ref, 0, seq_major)
    scf = _mod_row(modf_ref, 1, seq_major)
    att = jnp.dot(ys_ref[0].astype(bf16), wo_ref[:D_SSD, :], preferred_element_type=f32)
    att = att + jnp.dot(y5_ref[0].astype(bf16), wo_ref[D_SSD:, :], preferred_element_type=f32)
    x1 = _rows_in(x_ref, seq_major) + _by_row(lambda v, s: v * s, att, g1)
    v = _rms_mod(x1, n2g_ref[...], sc2, sh2).astype(bf16)
    ff = None
    for o in range(0, D_FF, ff_chunk):
        gate = jnp.dot(v, wg_ref[:, o:o + ff_chunk], preferred_element_type=f32)
        up = jnp.dot(v, wu_ref[:, o:o + ff_chunk], preferred_element_type=f32)
        hid = (_silu(gate) * up).astype(bf16)
        part = jnp.dot(hid, wd_ref[o:o + ff_chunk, :], preferred_element_type=f32)
        ff = part if ff is None else ff + part
    x2 = x1 + _by_row(lambda v, s: v * s, ff, g2)
    y = _rms_mod(x2, nfg_ref[...], scf, shf)
    if seq_major:
        s, t, d = o_ref.shape
        o_ref[...] = jnp.swapaxes(y.reshape(t, s, d), 0, 1)
    else:
        o_ref[0] = y


def _ffn(x, ys, y5, mod, modf, n2g, nfg, wo, wg, wu, wd, *, tm, seq_major, n_seq, first_seq, ff_chunk=256):
    if seq_major:
        x_spec = _const_spec(x.shape)
        nb, rows, d = 1, x.shape[0] * x.shape[1], x.shape[2]
        assert tm == rows
    else:
        nb, rows, d = x.shape
        x_spec = pl.BlockSpec((1, tm, d), lambda i, j: (i, j, 0))
    nt = rows // tm
    mod_spec = _mod_spec(mod, n_seq, first_seq)
    modf_spec = _mod_spec(modf, n_seq, first_seq)
    blk = lambda wd_: pl.BlockSpec((1, tm, wd_), lambda i, j: (i, j, 0))
    single = dict(pipeline_mode=pl.Buffered(1))
    wspec = lambda shape: pl.BlockSpec(shape, lambda i, j: (0, 0), **single)
    return pl.pallas_call(
        functools.partial(_ffn_kernel, seq_major=seq_major, ff_chunk=ff_chunk),
        grid=(nb, nt),
        in_specs=[x_spec, blk(D_SSD), blk(D_S5), mod_spec, modf_spec,
                  _const_spec((1, d)), _const_spec((1, d)),
                  wspec((d, d)), wspec((d, D_FF)), wspec((d, D_FF)), wspec((D_FF, d))],
        out_specs=x_spec,
        out_shape=jax.ShapeDtypeStruct(x.shape, f32),
        compiler_params=_params("parallel", "parallel"),
        name="out_ffn",
    )(x, ys, y5, mod, modf, n2g, nfg, wo, wg, wu, wd)


def kernel(x_prompt, x_sample, c_prompt, c_sample, state_ssd, state_conv, state_s5_re, state_s5_im, w_ada, b_ada, norm1_g, w_in, conv_w, conv_b, ssd_dt_bias, ssd_A_log, ssd_D, ssd_norm_g, s5_A_re, s5_A_im, s5_log_step, s5_B_re, s5_B_im, s5_C_re, s5_C_im, s5_D, w_glu, b_glu, w_out, norm2_g, w_ffn_gate, w_ffn_up, w_ffn_down, w_ada_f, b_ada_f, normf_g):
    assert w_ada.shape[0] == 1, "single-layer stack"
    bp, seq, d = x_prompt.shape
    bs, dseq, _ = x_sample.shape

    c_all = jnp.concatenate([c_sample, c_prompt], axis=0)
    mod = _ada_mod(c_all, w_ada[0], b_ada[0])
    modf = _ada_mod(c_all, w_ada_f, b_ada_f)
    seqs_s = dict(n_seq=bs, first_seq=0)
    seqs_p = dict(n_seq=bp, first_seq=bs)

    w_in_p = w_in[0].T
    pad_h = lambda v: jnp.concatenate([v, jnp.zeros((DT_PAD - HEADS,), f32)]).reshape(1, DT_PAD)
    dtb = pad_h(ssd_dt_bias[0])
    alog = pad_h(ssd_A_log[0])
    dexp = jnp.repeat(ssd_D[0], HEAD_DIM).reshape(1, D_SSD)
    ng = ssd_norm_g[0].reshape(1, D_SSD)
    cw, cb = conv_w[0], conv_b[0].reshape(1, CONV_DIM)

    ar, ai, bblk, cblk = _s5_params(s5_A_re[0], s5_A_im[0], s5_log_step[0],
                                    s5_B_re[0].transpose(0, 2, 1), s5_B_im[0].transpose(0, 2, 1),
                                    s5_C_re[0], s5_C_im[0])
    dsk = s5_D[0].reshape(1, D_S5)
    wglu = w_glu[0].astype(bf16)
    bglu = b_glu[0].reshape(1, D_S5)
    n1g, n2g, nfg = norm1_g[0].reshape(1, d), norm2_g[0].reshape(1, d), normf_g.reshape(1, d)
    nstate = S5_GROUPS * S5_STATE

    z, xbc, u5, dt = _inproj(x_prompt, mod, n1g, w_in_p, tm=512, seq_major=False, **seqs_p)
    y_ssd, ssd_p, conv_p = _ssd_prompt(xbc, z, dt, cw, cb, dtb, alog, dexp, ng)
    zeros_st = jnp.zeros((bp, nstate), f32)
    y_s5, re_p, im_p, wo, wg, wu, wd = _s5(
        u5, zeros_st, zeros_st, ar, ai, bblk, cblk, dsk, wglu, bglu, tl=128, batch_major=True,
        cast=(w_out[0], w_ffn_gate[0], w_ffn_up[0], w_ffn_down[0]))
    y_prompt = _ffn(x_prompt, y_ssd, y_s5, mod, modf, n2g, nfg, wo, wg, wu, wd, tm=512, seq_major=False, **seqs_p)

    rows_s = dseq * bs
    steps = lambda a: a.reshape(dseq, bs, a.shape[-1])
    flat = lambda a: a.reshape(1, rows_s, a.shape[-1])
    z, xbc, u5, dt = _inproj(x_sample, mod, n1g, w_in_p, tm=rows_s, seq_major=True, **seqs_s)
    y_ssd, ssd_s, conv_s = _ssd_sample(steps(xbc), steps(z), steps(dt), state_conv[0].transpose(1, 0, 2),
                                       state_ssd[0], cw, cb, dtb, alog, dexp, ng)
    y_s5, re_s, im_s = _s5(steps(u5), state_s5_re[0].reshape(bs, nstate), state_s5_im[0].reshape(bs, nstate),
                           ar, ai, bblk, cblk, dsk, wglu, bglu, tl=dseq, batch_major=False)
    y_sample = _ffn(x_sample, flat(y_ssd), flat(y_s5), mod, modf, n2g, nfg, wo, wg, wu, wd,
                    tm=rows_s, seq_major=True, **seqs_s)

    g5 = (S5_GROUPS, S5_STATE)
    return (y_prompt, y_sample,
            ssd_p[None], ssd_s.reshape((1,) + state_ssd.shape[1:]),
            conv_p[None], conv_s.transpose(1, 0, 2)[None],
            re_p.reshape((1, bp) + g5), re_s.reshape((1, bs) + g5),
            im_p.reshape((1, bp) + g5), im_s.reshape((1, bs) + g5))
```

```python
import functools

import jax
import jax.numpy as jnp
from jax import lax
from jax.experimental import pallas as pl
from jax.experimental.pallas import tpu as pltpu

f32 = jnp.float32
bf16 = jnp.bfloat16

D_MODEL = 1024
D_SSD = 512
HEAD_DIM = 64
HEADS = 8
GROUPS = 2
HEADS_PER_GROUP = HEADS // GROUPS
STATE = 128
CONV_W = 4
CONV_DIM = D_SSD + 2 * GROUPS * STATE
D_S5 = 512
S5_CH = 16
S5_GROUPS = 32
S5_STATE = 64
D_FF = 2816
N_ADA = 6
EPS = 1e-6

LANES = 128
SUBLANES = 8
SSD_CHUNK = 128
S5_SLABS = D_S5 // LANES
S5_SLAB_STATE = (S5_GROUPS // S5_SLABS) * S5_STATE
DT_PAD = LANES
VMEM_LIMIT = 56 * 1024 * 1024

NT_DIMS = (((1,), (1,)), ((), ()))
TN_DIMS = (((0,), (0,)), ((), ()))


def _silu(x):
    return x * jax.nn.sigmoid(x)


def _interleave(*gens):
    live = list(gens)
    while live:
        for g in list(live):
            try:
                next(g)
            except StopIteration:
                live.remove(g)


def _by_row(fn, v, *ms):
    r, d = v.shape
    m_rows = ms[0].shape[0]
    if m_rows in (1, r):
        return fn(v, *ms)
    out = fn(v.reshape(r // m_rows, m_rows, d), *[m[None] for m in ms])
    return out.reshape(r, d)


def _rms_mod(x, g, sc, sh):
    y = x * lax.rsqrt(jnp.mean(x * x, axis=-1, keepdims=True) + EPS)
    return _by_row(lambda v, s, t: v * (1.0 + s) + t, y * g, sc, sh)


def _mod_row(mod_ref, i, per_row):
    return mod_ref[i] if per_row else mod_ref[i, pl.ds(pl.program_id(0), 1), :]


def _mod_spec(mod, n_seq, first_seq):
    assert first_seq % n_seq == 0
    return pl.BlockSpec((mod.shape[0], n_seq, mod.shape[2]), lambda *_: (0, first_seq // n_seq, 0))


def _const_spec(shape):
    nd = len(shape)
    return pl.BlockSpec(shape, lambda *_: (0,) * nd)


def _params(*sem):
    return pltpu.CompilerParams(dimension_semantics=sem, vmem_limit_bytes=VMEM_LIMIT)


def _ada_kernel(c_ref, w_ref, b_ref, o_ref):
    s = _silu(c_ref[...]).astype(bf16)
    o_ref[0] = jnp.dot(s, w_ref[...].astype(bf16), preferred_element_type=f32) + b_ref[...]


def _ada_mod(c, w, b):
    m, k = c.shape
    n = w.shape[1] // k
    return pl.pallas_call(
        _ada_kernel,
        grid=(n,),
        in_specs=[pl.BlockSpec((m, k), lambda j: (0, 0)),
                  pl.BlockSpec((k, k), lambda j: (0, j)),
                  pl.BlockSpec((1, k), lambda j: (0, j))],
        out_specs=pl.BlockSpec((1, m, k), lambda j: (j, 0, 0)),
        out_shape=jax.ShapeDtypeStruct((n, m, k), f32),
        compiler_params=_params("parallel"),
        name="ada_mod",
    )(c, w, b.reshape(1, n * k))


def _rows_in(x_ref, seq_major):
    if not seq_major:
        return x_ref[0]
    s, t, d = x_ref.shape
    return jnp.swapaxes(x_ref[...], 0, 1).reshape(s * t, d)


def _inproj_kernel(x_ref, mod_ref, g_ref, w_ref, z_ref, xbc_ref, u5_ref, dt_ref, *, seq_major):
    sh = _mod_row(mod_ref, 0, seq_major)
    sc = _mod_row(mod_ref, 1, seq_major)
    u = _rms_mod(_rows_in(x_ref, seq_major), g_ref[...], sc, sh).astype(bf16)
    o_dt = D_SSD + CONV_DIM
    for ref, o, width in ((z_ref, 0, D_SSD), (xbc_ref, D_SSD, CONV_DIM), (u5_ref, o_dt + HEADS, D_S5),
                          (dt_ref, o_dt, DT_PAD)):
        ref[0] = lax.dot_general(u, w_ref[o:o + width, :].astype(bf16), NT_DIMS, preferred_element_type=f32)


def _inproj(x, mod, g, w, *, tm, seq_major, n_seq, first_seq):
    if seq_major:
        x_spec = _const_spec(x.shape)
        nb, rows, d = 1, x.shape[0] * x.shape[1], x.shape[2]
        assert tm == rows
    else:
        nb, rows, d = x.shape
        x_spec = pl.BlockSpec((1, tm, d), lambda i, j: (i, j, 0))
    nt = rows // tm
    widths = (D_SSD, CONV_DIM, D_S5, DT_PAD)
    return pl.pallas_call(
        functools.partial(_inproj_kernel, seq_major=seq_major),
        grid=(nb, nt),
        in_specs=[x_spec,
                  _mod_spec(mod, n_seq, first_seq),
                  _const_spec((1, d)),
                  pl.BlockSpec(w.shape, lambda i, j: (0, 0), pipeline_mode=pl.Buffered(1))],
        out_specs=[pl.BlockSpec((1, tm, wd), lambda i, j: (i, j, 0)) for wd in widths],
        out_shape=[jax.ShapeDtypeStruct((nb, rows, wd), f32) for wd in widths],
        compiler_params=_params("parallel", "parallel"),
        name="in_proj",
    )(x, mod, g, w)


def _split3(x):
    hi = x.astype(bf16)
    r1 = x - hi.astype(f32)
    mid = r1.astype(bf16)
    lo = (r1 - mid.astype(f32)).astype(bf16)
    return hi, mid, lo


def _dot_sel_lhs(sel16, x):
    return sum(jnp.dot(sel16, p, preferred_element_type=f32) for p in _split3(x))


def _dot_sel_rhs(x, sel16):
    return sum(jnp.dot(p, sel16, preferred_element_type=f32) for p in _split3(x))


def _gated_group_norm(y, z, ng):
    y = y * _silu(z)
    gw = D_SSD // GROUPS
    parts = []
    for g in range(GROUPS):
        yg = y[:, g * gw:(g + 1) * gw]
        parts.append(yg * lax.rsqrt(jnp.mean(yg * yg, axis=-1, keepdims=True) + EPS))
    return jnp.concatenate(parts, axis=-1) * ng


def _ssd_prompt_kernel(xbc_ref, z_ref, dt_ref, cw_ref, cb_ref, dtb_ref, alog_ref, dexp_ref, ng_ref,
                       y_ref, st_ref, cn_ref, ext_ref, act_ref, h_ref, *, cps):
    T = SSD_CHUNK
    gw = HEADS_PER_GROUP * HEAD_DIM

    @pl.when(pl.program_id(1) == 0)
    def _():
        ext_ref[0:SUBLANES, :] = jnp.zeros((SUBLANES, CONV_DIM), f32)
        h_ref[...] = jnp.zeros_like(h_ref)

    row = lax.broadcasted_iota(jnp.int32, (T, T), 0)
    col = lax.broadcasted_iota(jnp.int32, (T, T), 1)
    tri = row >= col
    tri16 = jnp.where(tri, 1.0, 0.0).astype(bf16)
    low_half = lax.broadcasted_iota(jnp.int32, (T, LANES), 1) < HEAD_DIM
    a_neg = -jnp.exp(alog_ref[...])

    R = cps * T
    ext_ref[SUBLANES:SUBLANES + R, :] = xbc_ref[0]
    for ci in range(cps):
        ext = ext_ref[ci * T:ci * T + SUBLANES + T, :]
        conv = cb_ref[...] + ext[SUBLANES:, :] * cw_ref[CONV_W - 1:CONV_W, :]
        for m in range(1, CONV_W):
            k = CONV_W - 1 - m
            conv = conv + pltpu.roll(ext, m, 0)[SUBLANES:, :] * cw_ref[k:k + 1, :]
        act_ref[ci * T:(ci + 1) * T, :] = _silu(conv)
    cn_ref[0] = ext_ref[SUBLANES + R - (CONV_W - 1):SUBLANES + R, :]
    ext_ref[0:SUBLANES, :] = ext_ref[R:R + SUBLANES, :]

    lane_head = lax.broadcasted_iota(jnp.int32, (DT_PAD, D_SSD), 1) // HEAD_DIM
    sel16 = jnp.where(lax.broadcasted_iota(jnp.int32, (DT_PAD, D_SSD), 0) == lane_head, 1.0, 0.0).astype(bf16)

    def spread(q):
        return _dot_sel_rhs(q, sel16)

    def state_free(ci):
        r0 = ci * T
        act = act_ref[r0:r0 + T, :]
        xs = act[:, :D_SSD]
        bm = act[:, D_SSD:D_SSD + GROUPS * STATE]
        cm = act[:, D_SSD + GROUPS * STATE:]
        dtv = jax.nn.softplus(dt_ref[0, r0:r0 + T, :] + dtb_ref[...])
        a_cs = _dot_sel_lhs(tri16, dtv * a_neg)
        bg16s = [bm[:, g * STATE:(g + 1) * STATE].astype(bf16) for g in range(GROUPS)]
        cg16s = [cm[:, g * STATE:(g + 1) * STATE].astype(bf16) for g in range(GROUPS)]
        gmats = [lax.dot_general(cg16s[g], bg16s[g], NT_DIMS, preferred_element_type=f32)
                 for g in range(GROUPS)]
        yield
        a_last = a_cs[T - 1:T, :]
        a_cs_t = a_cs.T
        rep = spread(jnp.concatenate([dtv, jnp.exp(a_last - a_cs), jnp.exp(a_cs)], axis=0))
        yield
        x = xs * rep[:T]
        x16 = x.astype(bf16)
        xd16 = (x * rep[T:2 * T]).astype(bf16)
        e_cs = rep[2 * T:]
        yield
        y_part = []
        for g in range(GROUPS):
            for pr in range(HEADS_PER_GROUP // 2):
                s_pair = []
                for q in range(2):
                    h = g * HEADS_PER_GROUP + 2 * pr + q
                    seg = a_cs[:, h:h + 1] - a_cs_t[h:h + 1, :]
                    lmat = jnp.exp(jnp.where(tri, seg, -jnp.inf))
                    s_pair.append((gmats[g] * lmat).astype(bf16))
                lanes = slice(g * gw + pr * LANES, g * gw + (pr + 1) * LANES)
                both = jnp.dot(jnp.concatenate(s_pair, axis=0), x16[:, lanes], preferred_element_type=f32)
                yield
                y_part.append(jnp.where(low_half, both[:T], both[T:]))
        y_free = jnp.concatenate(y_part, axis=-1) + xs * dexp_ref[...]
        return y_free, e_cs, jnp.exp(a_last), xd16, bg16s, cg16s

    def state_step(ci, y_free, e_cs, e_last, xd16, bg16s, cg16s):
        r0 = ci * T
        y_offs = []
        for g in range(GROUPS):
            hp = h_ref[g]
            y_offs.append(lax.dot_general(cg16s[g], hp.astype(bf16), NT_DIMS, preferred_element_type=f32))
            upd = lax.dot_general(xd16[:, g * gw:(g + 1) * gw], bg16s[g], TN_DIMS, preferred_element_type=f32)
            for hh in range(HEADS_PER_GROUP):
                h = g * HEADS_PER_GROUP + hh
                rows = slice(hh * HEAD_DIM, (hh + 1) * HEAD_DIM)
                h_ref[g, rows, :] = e_last[:, h:h + 1] * hp[rows] + upd[rows]
        y = y_free + jnp.concatenate(y_offs, axis=-1) * e_cs
        y_ref[0, r0:r0 + T, :] = _gated_group_norm(y, z_ref[0, r0:r0 + T, :], ng_ref[...])

    gens = [state_free(ci) for ci in range(cps)]
    free = [None] * cps
    while any(f is None for f in free):
        for ci in range(cps):
            if free[ci] is None:
                try:
                    next(gens[ci])
                except StopIteration as done:
                    free[ci] = done.value
    for ci in range(cps):
        state_step(ci, *free[ci])

    @pl.when(pl.program_id(1) == pl.num_programs(1) - 1)
    def _():
        st_ref[0] = h_ref[...].reshape(HEADS, HEAD_DIM, STATE)


def _ssd_prompt(xbc, z, dt, cw, cb, dtb, alog, dexp, ng, *, cps=8):
    nb, L, _ = xbc.shape
    T = SSD_CHUNK
    rows = cps * T
    blk = lambda wd: pl.BlockSpec((1, rows, wd), lambda b, c: (b, c, 0))
    return pl.pallas_call(
        functools.partial(_ssd_prompt_kernel, cps=cps),
        grid=(nb, L // rows),
        in_specs=[blk(CONV_DIM), blk(D_SSD), blk(DT_PAD),
                  _const_spec((CONV_W, CONV_DIM)), _const_spec((1, CONV_DIM)),
                  _const_spec((1, DT_PAD)), _const_spec((1, DT_PAD)),
                  _const_spec((1, D_SSD)), _const_spec((1, D_SSD))],
        out_specs=[blk(D_SSD),
                   pl.BlockSpec((1, HEADS, HEAD_DIM, STATE), lambda b, c: (b, 0, 0, 0)),
                   pl.BlockSpec((1, CONV_W - 1, CONV_DIM), lambda b, c: (b, 0, 0))],
        out_shape=[jax.ShapeDtypeStruct((nb, L, D_SSD), f32),
                   jax.ShapeDtypeStruct((nb, HEADS, HEAD_DIM, STATE), f32),
                   jax.ShapeDtypeStruct((nb, CONV_W - 1, CONV_DIM), f32)],
        scratch_shapes=[pltpu.VMEM((SUBLANES + rows, CONV_DIM), f32),
                        pltpu.VMEM((rows, CONV_DIM), f32),
                        pltpu.VMEM((GROUPS, HEADS_PER_GROUP * HEAD_DIM, STATE), f32)],
        compiler_params=_params("parallel", "arbitrary"),
        name="ssd_prompt",
    )(xbc, z, dt, cw, cb, dtb, alog, dexp, ng)


def _ssd_sample_kernel(xbc_ref, z_ref, dt_ref, cbuf_ref, st_ref, cw_ref, cb_ref, dtb_ref, alog_ref,
                       dexp_ref, ng_ref, y_ref, stn_ref, cn_ref, dtot_ref, *, L, nb):
    gw = HEADS_PER_GROUP * HEAD_DIM
    full = [cbuf_ref[j] for j in range(CONV_W - 1)] + [xbc_ref[t] for t in range(L)]
    for j in range(CONV_W - 1):
        cn_ref[j] = full[L + j]
    a_neg = -jnp.exp(alog_ref[...])
    xs, bm, cm, dtv, acs = [], [], [], [], []
    run = None
    for t in range(L):
        conv = cb_ref[...]
        for k in range(CONV_W):
            conv = conv + full[t + k] * cw_ref[k:k + 1, :]
        act = _silu(conv)
        xs.append(act[:, :D_SSD])
        bm.append(act[:, D_SSD:D_SSD + GROUPS * STATE])
        cm.append(act[:, D_SSD + GROUPS * STATE:])
        d = jax.nn.softplus(dt_ref[t] + dtb_ref[...])
        dtv.append(d)
        run = d * a_neg if run is None else run + d * a_neg
        acs.append(run)
    a_tot = acs[L - 1]
    dtot_ref[...] = jnp.exp(a_tot)

    lane_head = lax.broadcasted_iota(jnp.int32, (DT_PAD, D_SSD), 1) // HEAD_DIM
    sel16 = jnp.where(lax.broadcasted_iota(jnp.int32, (DT_PAD, D_SSD), 0) == lane_head, 1.0, 0.0).astype(bf16)
    pairs = [(t, s) for t in range(L) for s in range(t)]
    factors = (dtv + [jnp.exp(a_tot - acs[t]) for t in range(L)] + [jnp.exp(acs[t]) for t in range(L)]
               + [jnp.exp(acs[t] - acs[s]) for t, s in pairs])
    rep = _dot_sel_rhs(jnp.concatenate(factors, axis=0), sel16)
    piece = lambda i: rep[i * nb:(i + 1) * nb]
    x = [xs[t] * piece(t) for t in range(L)]
    xd_stack = jnp.concatenate([x[t] * piece(L + t) for t in range(L)], axis=0)
    e_cs = [piece(2 * L + t) for t in range(L)]
    decay = {ts: piece(3 * L + i) for i, ts in enumerate(pairs)}

    in_group0 = lax.broadcasted_iota(jnp.int32, (nb, D_SSD), 1) < gw
    y_intra = []
    for t in range(L):
        acc = None
        for s in range(t + 1):
            cb_dot = [jnp.sum(cm[t][:, g * STATE:(g + 1) * STATE] * bm[s][:, g * STATE:(g + 1) * STATE],
                              axis=-1, keepdims=True) for g in range(GROUPS)]
            w = jnp.where(in_group0, cb_dot[0], cb_dot[1])
            term = w * x[s] if s == t else w * decay[(t, s)] * x[s]
            acc = term if acc is None else acc + term
        y_intra.append(acc)

    c_stack = [jnp.concatenate([cm[t][:, g * STATE:(g + 1) * STATE] for t in range(L)], axis=0).astype(bf16)
               for g in range(GROUPS)]
    b_stack = [jnp.concatenate([bm[t][:, g * STATE:(g + 1) * STATE] for t in range(L)], axis=0).astype(bf16)
               for g in range(GROUPS)]
    seq_of_row = lax.broadcasted_iota(jnp.int32, (L * nb, gw), 0) & (nb - 1)

    def per_seq(b, yoff):
        mine = seq_of_row == b
        drow = dtot_ref[pl.ds(b, 1), :]
        out = []
        for g in range(GROUPS):
            h0 = st_ref[b, g]
            r = lax.dot_general(c_stack[g], h0.astype(bf16), NT_DIMS, preferred_element_type=f32)
            out.append(yoff[g] + jnp.where(mine, r, 0.0))
            xm = jnp.where(mine, xd_stack[:, g * gw:(g + 1) * gw], 0.0).astype(bf16)
            upd = lax.dot_general(xm, b_stack[g], TN_DIMS, preferred_element_type=f32)
            for hh in range(HEADS_PER_GROUP):
                h = g * HEADS_PER_GROUP + hh
                rows = slice(hh * HEAD_DIM, (hh + 1) * HEAD_DIM)
                stn_ref[b, g, rows, :] = drow[:, h:h + 1] * h0[rows] + upd[rows]
        return tuple(out)

    yoff = lax.fori_loop(0, nb, per_seq, tuple(jnp.zeros((L * nb, gw), f32) for _ in range(GROUPS)),
                         unroll=True)

    for t in range(L):
        y_off = jnp.concatenate([yoff[g][t * nb:(t + 1) * nb] for g in range(GROUPS)], axis=-1)
        y = y_intra[t] + y_off * e_cs[t] + xs[t] * dexp_ref[...]
        y_ref[t] = _gated_group_norm(y, z_ref[t], ng_ref[...])


def _ssd_sample(xbc, z, dt, cbuf, st, cw, cb, dtb, alog, dexp, ng, *, nb=8):
    L, B, _ = xbc.shape
    assert nb & (nb - 1) == 0 and B % nb == 0
    tblk = lambda n, wd: pl.BlockSpec((n, nb, wd), lambda i: (0, i, 0))
    gw = HEADS_PER_GROUP * HEAD_DIM
    st = st.reshape(B, GROUPS, gw, STATE)
    st_spec = pl.BlockSpec((nb, GROUPS, gw, STATE), lambda i: (i, 0, 0, 0))
    return pl.pallas_call(
        functools.partial(_ssd_sample_kernel, L=L, nb=nb),
        grid=(B // nb,),
        in_specs=[tblk(L, CONV_DIM), tblk(L, D_SSD), tblk(L, DT_PAD), tblk(CONV_W - 1, CONV_DIM), st_spec,
                  _const_spec((CONV_W, CONV_DIM)), _const_spec((1, CONV_DIM)),
                  _const_spec((1, DT_PAD)), _const_spec((1, DT_PAD)),
                  _const_spec((1, D_SSD)), _const_spec((1, D_SSD))],
        out_specs=[tblk(L, D_SSD), st_spec, tblk(CONV_W - 1, CONV_DIM)],
        out_shape=[jax.ShapeDtypeStruct((L, B, D_SSD), f32),
                   jax.ShapeDtypeStruct((B, GROUPS, gw, STATE), f32),
                   jax.ShapeDtypeStruct((CONV_W - 1, B, CONV_DIM), f32)],
        scratch_shapes=[pltpu.VMEM((nb, DT_PAD), f32)],
        compiler_params=_params("parallel"),
        name="ssd_sample",
    )(xbc, z, dt, cbuf, st, cw, cb, dtb, alog, dexp, ng)


def _s5_param_kernel(lr_ref, li_ref, ls_ref, br_ref, bi_ref, cr_ref, ci_ref,
                     ar_ref, ai_ref, bblk_ref, cblk_ref, b_scr, c_scr):
    lr, li = lr_ref[...], li_ref[...]
    step = jnp.exp(ls_ref[...])
    mag = jnp.exp(lr * step)
    abr = mag * jnp.cos(li * step)
    abi = mag * jnp.sin(li * step)
    nr, ni = abr - 1.0, abi
    den = lr * lr + li * li
    fr = (nr * lr + ni * li) / den
    fi = (ni * lr - nr * li) / den
    br, bi = br_ref[...], bi_ref[...]
    bbr = fr * br - fi * bi
    bbi = fr * bi + fi * br
    b_scr[...] = jnp.zeros_like(b_scr)
    c_scr[...] = jnp.zeros_like(c_scr)
    gps = S5_GROUPS // S5_SLABS
    ns = S5_SLAB_STATE
    for g in range(S5_GROUPS):
        s, gl = divmod(g, gps)
        ch = slice(gl * S5_CH, (gl + 1) * S5_CH)
        st = slice(gl * S5_STATE, (gl + 1) * S5_STATE)
        st_im = slice(ns + gl * S5_STATE, ns + (gl + 1) * S5_STATE)
        ar_ref[s, :, st] = jnp.broadcast_to(abr[g], (SUBLANES, S5_STATE))
        ai_ref[s, :, st] = jnp.broadcast_to(abi[g], (SUBLANES, S5_STATE))
        b_scr[s, ch, st] = bbr[g]
        b_scr[s, ch, st_im] = bbi[g]
        c_scr[s, st, ch] = cr_ref[g].T
        c_scr[s, st_im, ch] = -ci_ref[g].T
    bblk_ref[...] = b_scr[...].astype(bf16)
    cblk_ref[...] = c_scr[...].astype(bf16)


def _s5_params(a_re, a_im, log_step, b_re_t, b_im_t, c_re, c_im):
    g, p = a_re.shape
    ns = S5_SLAB_STATE
    return pl.pallas_call(
        _s5_param_kernel,
        out_shape=[jax.ShapeDtypeStruct((S5_SLABS, SUBLANES, ns), f32),
                   jax.ShapeDtypeStruct((S5_SLABS, SUBLANES, ns), f32),
                   jax.ShapeDtypeStruct((S5_SLABS, LANES, 2 * ns), bf16),
                   jax.ShapeDtypeStruct((S5_SLABS, 2 * ns, LANES), bf16)],
        scratch_shapes=[pltpu.VMEM((S5_SLABS, LANES, 2 * ns), f32),
                        pltpu.VMEM((S5_SLABS, 2 * ns, LANES), f32)],
        name="s5_params",
    )(a_re.reshape(g, 1, p), a_im.reshape(g, 1, p), log_step.reshape(g, 1, 1), b_re_t, b_im_t, c_re, c_im)


def _gelu(x):
    return 0.5 * x * (1.0 + lax.erf(x * (2.0 ** -0.5)))


def _s5_slabs_interleaved(ut_ref, hbuf_ref, hst_ref, g_ref, ar_ref, ai_ref, bblk_ref, cblk_ref, dsk_ref, tl,
                          row_block=256, n_stages=8):
    ns = S5_SLAB_STATE
    nb = SUBLANES
    rows = nb * tl

    def lanes(s):
        return slice(s * LANES, (s + 1) * LANES)

    def bu_stage(slabs):
        for s in slabs:
            for r in range(0, rows, row_block):
                us = ut_ref[r:r + row_block, lanes(s)]
                hbuf_ref[s, nb + r:nb + r + row_block, :] = jnp.dot(us.astype(bf16), bblk_ref[s],
                                                                    preferred_element_type=f32)
                yield

    def scan_stage(slabs):
        carry = [(hst_ref[s, :, :ns], hst_ref[s, :, ns:]) for s in slabs]
        coef = [(ar_ref[s], ai_ref[s]) for s in slabs]
        for t in range(tl):
            r = nb + nb * t
            for i, s in enumerate(slabs):
                (pr, pi), (ar, ai) = carry[i], coef[i]
                nr = ar * pr - ai * pi + hbuf_ref[s, r:r + nb, :ns]
                ni = ar * pi + ai * pr + hbuf_ref[s, r:r + nb, ns:]
                hbuf_ref[s, r:r + nb, :ns] = nr
                hbuf_ref[s, r:r + nb, ns:] = ni
                carry[i] = (nr, ni)
            if (t + 1) % (tl // n_stages) == 0:
                yield
        for i, s in enumerate(slabs):
            hst_ref[s, :, :ns] = carry[i][0]
            hst_ref[s, :, ns:] = carry[i][1]

    def y_stage(slabs):
        for s in slabs:
            for r in range(0, rows, row_block):
                h16 = hbuf_ref[s, nb + r:nb + r + row_block, :].astype(bf16)
                ys = jnp.dot(h16, cblk_ref[s], preferred_element_type=f32)
                ys = ys + dsk_ref[:, lanes(s)] * ut_ref[r:r + row_block, lanes(s)]
                g_ref[r:r + row_block, lanes(s)] = _gelu(ys)
                yield

    half = S5_SLABS // 2
    first, second = tuple(range(half)), tuple(range(half, S5_SLABS))
    _interleave(bu_stage(first))
    _interleave(scan_stage(first), bu_stage(second))
    _interleave(scan_stage(second), y_stage(first))
    _interleave(y_stage(second))


def _s5_scan_vmem(hbuf_ref, hst_ref, ar_ref, ai_ref, s, nb, rows):
    ns = S5_SLAB_STATE
    hbuf_ref[s, 0:nb, :] = hst_ref[s]
    ar, ai = ar_ref[s], ai_ref[s]

    def body(i, carry):
        j = pl.multiple_of(nb + i * SUBLANES, SUBLANES)
        prev = hbuf_ref[s, pl.ds(j - nb, SUBLANES), :]
        cur = hbuf_ref[s, pl.ds(j, SUBLANES), :]
        pr, pi = prev[:, :ns], prev[:, ns:]
        hbuf_ref[s, pl.ds(j, SUBLANES), :ns] = ar * pr - ai * pi + cur[:, :ns]
        hbuf_ref[s, pl.ds(j, SUBLANES), ns:] = ar * pi + ai * pr + cur[:, ns:]
        return carry

    lax.fori_loop(0, rows // SUBLANES, body, 0)
    hst_ref[s] = hbuf_ref[s, rows:rows + nb, :]


S5_N_IN = 10


def _s5_kernel(*refs, nb, tl, batch_major, n_cast):
    (u_ref, re0_ref, im0_ref, ar_ref, ai_ref, bblk_ref, cblk_ref, dsk_ref, wglu_ref,
     bglu_ref) = refs[:S5_N_IN]
    cast_in = refs[S5_N_IN:S5_N_IN + n_cast]
    y_ref, ren_ref, imn_ref = refs[S5_N_IN + n_cast:S5_N_IN + n_cast + 3]
    cast_out = refs[S5_N_IN + n_cast + 3:S5_N_IN + 2 * n_cast + 3]
    ut_ref, hbuf_ref, hst_ref, g_ref = refs[S5_N_IN + 2 * n_cast + 3:]
    for w_ref, w16_ref in zip(cast_in, cast_out):
        w16_ref[...] = w_ref[...].astype(bf16)
    step = pl.program_id(0)
    rows = nb * tl
    ns = S5_SLAB_STATE

    @pl.when(step == 0)
    def _():
        for s in range(S5_SLABS):
            hst_ref[s, :, :ns] = re0_ref[:, s * ns:(s + 1) * ns]
            hst_ref[s, :, ns:] = im0_ref[:, s * ns:(s + 1) * ns]

    if batch_major:
        ut_ref[...] = jnp.swapaxes(u_ref[...], 0, 1).reshape(rows, D_S5)
    else:
        ut_ref[...] = u_ref[...].reshape(rows, D_S5)

    if nb == SUBLANES:
        _s5_slabs_interleaved(ut_ref, hbuf_ref, hst_ref, g_ref, ar_ref, ai_ref, bblk_ref, cblk_ref, dsk_ref, tl)
    else:
        for s in range(S5_SLABS):
            sl = slice(s * LANES, (s + 1) * LANES)
            hbuf_ref[s, nb:nb + rows, :] = jnp.dot(ut_ref[:, sl].astype(bf16), bblk_ref[s],
                                                   preferred_element_type=f32)
            _s5_scan_vmem(hbuf_ref, hst_ref, ar_ref, ai_ref, s, nb, rows)
            ys = jnp.dot(hbuf_ref[s, nb:nb + rows, :].astype(bf16), cblk_ref[s], preferred_element_type=f32)
            g_ref[:, sl] = _gelu(ys + dsk_ref[:, sl] * ut_ref[:, sl])

    g = g_ref[...]
    gate = jnp.dot(g.astype(bf16), wglu_ref[...], preferred_element_type=f32) + bglu_ref[...]
    out = g * jax.nn.sigmoid(gate)
    if batch_major:
        y_ref[...] = jnp.swapaxes(out.reshape(tl, nb, D_S5), 0, 1)
    else:
        y_ref[...] = out.reshape(tl, nb, D_S5)

    @pl.when(step == pl.num_programs(0) - 1)
    def _():
        for s in range(S5_SLABS):
            ren_ref[:, s * ns:(s + 1) * ns] = hst_ref[s, :, :ns]
            imn_ref[:, s * ns:(s + 1) * ns] = hst_ref[s, :, ns:]


def _s5(u, re0, im0, ar, ai, bblk, cblk, dsk, wglu, bglu, *, tl, batch_major, cast=()):
    if batch_major:
        nb, L, _ = u.shape
        u_spec = pl.BlockSpec((nb, tl, D_S5), lambda i: (0, i, 0))
    else:
        L, nb, _ = u.shape
        u_spec = pl.BlockSpec((tl, nb, D_S5), lambda i: (i, 0, 0))
    rows = nb * tl
    steps = L // tl
    nstate = S5_GROUPS * S5_STATE
    st_spec = _const_spec((nb, nstate))
    cast_specs = [pl.BlockSpec((w.shape[0] // steps, w.shape[1]), lambda i: (i, 0)) for w in cast]
    assert all(w.shape[0] % (steps * 2 * SUBLANES) == 0 for w in cast)
    return pl.pallas_call(
        functools.partial(_s5_kernel, nb=nb, tl=tl, batch_major=batch_major, n_cast=len(cast)),
        grid=(steps,),
        in_specs=[u_spec, st_spec, st_spec,
                  _const_spec((S5_SLABS, SUBLANES, S5_SLAB_STATE)),
                  _const_spec((S5_SLABS, SUBLANES, S5_SLAB_STATE)),
                  _const_spec((S5_SLABS, LANES, 2 * S5_SLAB_STATE)),
                  _const_spec((S5_SLABS, 2 * S5_SLAB_STATE, LANES)),
                  _const_spec((1, D_S5)), _const_spec((D_S5, D_S5)), _const_spec((1, D_S5))] + cast_specs,
        out_specs=[u_spec, st_spec, st_spec] + cast_specs,
        out_shape=[jax.ShapeDtypeStruct(u.shape, f32),
                   jax.ShapeDtypeStruct((nb, nstate), f32),
                   jax.ShapeDtypeStruct((nb, nstate), f32)]
                  + [jax.ShapeDtypeStruct(w.shape, bf16) for w in cast],
        scratch_shapes=[pltpu.VMEM((rows, D_S5), f32),
                        pltpu.VMEM((S5_SLABS, nb + rows, 2 * S5_SLAB_STATE), f32),
                        pltpu.VMEM((S5_SLABS, nb, 2 * S5_SLAB_STATE), f32),
                        pltpu.VMEM((rows, D_S5), f32)],
        compiler_params=_params("arbitrary"),
        name="s5_mixer",
    )(u, re0, im0, ar, ai, bblk, cblk, dsk, wglu, bglu, *cast)


def _ffn_stages(x, ys, y5, mods, n2g, nfg, wo_ref, wg_ref, wu_ref, wd_ref, write_out, ff_chunk):
    g1, sh2, sc2, g2, shf, scf = mods
    att = jnp.dot(ys.astype(bf16), wo_ref[:D_SSD, :], preferred_element_type=f32)
    att = att + jnp.dot(y5.astype(bf16), wo_ref[D_SSD:, :], preferred_element_type=f32)
    yield
    x1 = x + _by_row(lambda v, s: v * s, att, g1)
    v = _rms_mod(x1, n2g, sc2, sh2).astype(bf16)
    ff = None
    for o in range(0, D_FF, ff_chunk):
        gate = jnp.dot(v, wg_ref[:, o:o + ff_chunk], preferred_element_type=f32)
        up = jnp.dot(v, wu_ref[:, o:o + ff_chunk], preferred_element_type=f32)
        hid = (_silu(gate) * up).astype(bf16)
        part = jnp.dot(hid, wd_ref[o:o + ff_chunk, :], preferred_element_type=f32)
        ff = part if ff is None else ff + part
        yield
    x2 = x1 + _by_row(lambda v, s: v * s, ff, g2)
    write_out(_rms_mod(x2, nfg, scf, shf))


def _ffn_kernel(x_ref, ys_ref, y5_ref, mod_ref, modf_ref, n2g_ref, nfg_ref, wo_ref, wg_ref, wu_ref, wd_ref,
                o_ref, *, seq_major, ff_chunk, sub):
    mods = (_mod_row(mod_ref, 2, seq_major), _mod_row(mod_ref, 3, seq_major), _mod_row(mod_ref, 4, seq_major),
            _mod_row(mod_ref, 5, seq_major), _mod_row(modf_ref, 0, seq_major), _mod_row(modf_ref, 1, seq_major))
    x = _rows_in(x_ref, seq_major)
    rows = x.shape[0]
    tiles = {}

    def stages(r0):
        def write_out(y):
            if seq_major:
                tiles[r0] = y
            else:
                o_ref[0, r0:r0 + sub, :] = y
        return _ffn_stages(x[r0:r0 + sub], ys_ref[0, r0:r0 + sub, :], y5_ref[0, r0:r0 + sub, :], mods,
                           n2g_ref[...], nfg_ref[...], wo_ref, wg_ref, wu_ref, wd_ref, write_out, ff_chunk)

    _interleave(*[stages(r0) for r0 in range(0, rows, sub)])
    if seq_major:
        s, t, d = o_ref.shape
        y = jnp.concatenate([tiles[r0] for r0 in range(0, rows, sub)], axis=0)
        o_ref[...] = jnp.swapaxes(y.reshape(t, s, d), 0, 1)


def _ffn(x, ys, y5, mod, modf, n2g, nfg, wo, wg, wu, wd, *, tm, sub, seq_major, n_seq, first_seq, ff_chunk=256):
    if seq_major:
        x_spec = _const_spec(x.shape)
        nb, rows, d = 1, x.shape[0] * x.shape[1], x.shape[2]
        assert tm == rows
    else:
        nb, rows, d = x.shape
        x_spec = pl.BlockSpec((1, tm, d), lambda i, j: (i, j, 0))
    nt = rows // tm
    mod_spec = _mod_spec(mod, n_seq, first_seq)
    modf_spec = _mod_spec(modf, n_seq, first_seq)
    blk = lambda wd_: pl.BlockSpec((1, tm, wd_), lambda i, j: (i, j, 0))
    single = dict(pipeline_mode=pl.Buffered(1))
    wspec = lambda shape: pl.BlockSpec(shape, lambda i, j: (0, 0), **single)
    return pl.pallas_call(
        functools.partial(_ffn_kernel, seq_major=seq_major, ff_chunk=ff_chunk, sub=sub),
        grid=(nb, nt),
        in_specs=[x_spec, blk(D_SSD), blk(D_S5), mod_spec, modf_spec,
                  _const_spec((1, d)), _const_spec((1, d)),
                  wspec((d, d)), wspec((d, D_FF)), wspec((d, D_FF)), wspec((D_FF, d))],
        out_specs=x_spec,
        out_shape=jax.ShapeDtypeStruct(x.shape, f32),
        compiler_params=_params("parallel", "parallel"),
        name="out_ffn",
    )(x, ys, y5, mod, modf, n2g, nfg, wo, wg, wu, wd)


def kernel(x_prompt, x_sample, c_prompt, c_sample, state_ssd, state_conv, state_s5_re, state_s5_im, w_ada, b_ada, norm1_g, w_in, conv_w, conv_b, ssd_dt_bias, ssd_A_log, ssd_D, ssd_norm_g, s5_A_re, s5_A_im, s5_log_step, s5_B_re, s5_B_im, s5_C_re, s5_C_im, s5_D, w_glu, b_glu, w_out, norm2_g, w_ffn_gate, w_ffn_up, w_ffn_down, w_ada_f, b_ada_f, normf_g):
    assert w_ada.shape[0] == 1, "single-layer stack"
    bp, seq, d = x_prompt.shape
    bs, dseq, _ = x_sample.shape

    c_all = jnp.concatenate([c_sample, c_prompt], axis=0)
    mod = _ada_mod(c_all, w_ada[0], b_ada[0])
    modf = _ada_mod(c_all, w_ada_f, b_ada_f)
    seqs_s = dict(n_seq=bs, first_seq=0)
    seqs_p = dict(n_seq=bp, first_seq=bs)

    w_in_p = w_in[0].T
    pad_h = lambda v: jnp.concatenate([v, jnp.zeros((DT_PAD - HEADS,), f32)]).reshape(1, DT_PAD)
    dtb = pad_h(ssd_dt_bias[0])
    alog = pad_h(ssd_A_log[0])
    dexp = jnp.repeat(ssd_D[0], HEAD_DIM).reshape(1, D_SSD)
    ng = ssd_norm_g[0].reshape(1, D_SSD)
    cw, cb = conv_w[0], conv_b[0].reshape(1, CONV_DIM)

    ar, ai, bblk, cblk = _s5_params(s5_A_re[0], s5_A_im[0], s5_log_step[0],
                                    s5_B_re[0].transpose(0, 2, 1), s5_B_im[0].transpose(0, 2, 1),
                                    s5_C_re[0], s5_C_im[0])
    dsk = s5_D[0].reshape(1, D_S5)
    wglu = w_glu[0].astype(bf16)
    bglu = b_glu[0].reshape(1, D_S5)
    n1g, n2g, nfg = norm1_g[0].reshape(1, d), norm2_g[0].reshape(1, d), normf_g.reshape(1, d)
    nstate = S5_GROUPS * S5_STATE

    z, xbc, u5, dt = _inproj(x_prompt, mod, n1g, w_in_p, tm=512, seq_major=False, **seqs_p)
    y_ssd, ssd_p, conv_p = _ssd_prompt(xbc, z, dt, cw, cb, dtb, alog, dexp, ng)
    zeros_st = jnp.zeros((bp, nstate), f32)
    y_s5, re_p, im_p, wo, wg, wu, wd = _s5(
        u5, zeros_st, zeros_st, ar, ai, bblk, cblk, dsk, wglu, bglu, tl=128, batch_major=True,
        cast=(w_out[0], w_ffn_gate[0], w_ffn_up[0], w_ffn_down[0]))
    y_prompt = _ffn(x_prompt, y_ssd, y_s5, mod, modf, n2g, nfg, wo, wg, wu, wd, tm=1024, sub=512,
                    seq_major=False, **seqs_p)

    rows_s = dseq * bs
    steps = lambda a: a.reshape(dseq, bs, a.shape[-1])
    flat = lambda a: a.reshape(1, rows_s, a.shape[-1])
    z, xbc, u5, dt = _inproj(x_sample, mod, n1g, w_in_p, tm=rows_s, seq_major=True, **seqs_s)
    y_ssd, ssd_s, conv_s = _ssd_sample(steps(xbc), steps(z), steps(dt), state_conv[0].transpose(1, 0, 2),
                                       state_ssd[0], cw, cb, dtb, alog, dexp, ng)
    y_s5, re_s, im_s = _s5(steps(u5), state_s5_re[0].reshape(bs, nstate), state_s5_im[0].reshape(bs, nstate),
                           ar, ai, bblk, cblk, dsk, wglu, bglu, tl=dseq, batch_major=False)
    y_sample = _ffn(x_sample, flat(y_ssd), flat(y_s5), mod, modf, n2g, nfg, wo, wg, wu, wd,
                    tm=rows_s, sub=rows_s, seq_major=True, **seqs_s)

    g5 = (S5_GROUPS, S5_STATE)
    return (y_prompt, y_sample,
            ssd_p[None], ssd_s.reshape((1,) + state_ssd.shape[1:]),
            conv_p[None], conv_s.transpose(1, 0, 2)[None],
            re_p.reshape((1, bp) + g5), re_s.reshape((1, bs) + g5),
            im_p.reshape((1, bp) + g5), im_s.reshape((1, bs) + g5))
```

```python
import functools

import jax
import jax.numpy as jnp
from jax import lax
from jax.experimental import pallas as pl
from jax.experimental.pallas import tpu as pltpu

f32 = jnp.float32
bf16 = jnp.bfloat16

D_MODEL = 1024
D_SSD = 512
HEAD_DIM = 64
HEADS = 8
GROUPS = 2
HEADS_PER_GROUP = HEADS // GROUPS
STATE = 128
CONV_W = 4
CONV_DIM = D_SSD + 2 * GROUPS * STATE
D_S5 = 512
S5_CH = 16
S5_GROUPS = 32
S5_STATE = 64
D_FF = 2816
N_ADA = 6
EPS = 1e-6

LANES = 128
SUBLANES = 8
SSD_CHUNK = 128
S5_SLABS = D_S5 // LANES
S5_SLAB_STATE = (S5_GROUPS // S5_SLABS) * S5_STATE
DT_PAD = LANES
VMEM_LIMIT = 56 * 1024 * 1024

NT_DIMS = (((1,), (1,)), ((), ()))
TN_DIMS = (((0,), (0,)), ((), ()))


def _silu(x):
    return x * jax.nn.sigmoid(x)


def _interleave(*gens):
    live = list(gens)
    while live:
        for g in list(live):
            try:
                next(g)
            except StopIteration:
                live.remove(g)


def _by_row(fn, v, *ms):
    r, d = v.shape
    m_rows = ms[0].shape[0]
    if m_rows in (1, r):
        return fn(v, *ms)
    out = fn(v.reshape(r // m_rows, m_rows, d), *[m[None] for m in ms])
    return out.reshape(r, d)


def _rms_mod(x, g, sc, sh):
    y = x * lax.rsqrt(jnp.mean(x * x, axis=-1, keepdims=True) + EPS)
    return _by_row(lambda v, s, t: v * (1.0 + s) + t, y * g, sc, sh)


def _mod_row(mod_ref, i, per_row):
    return mod_ref[i] if per_row else mod_ref[i, pl.ds(pl.program_id(0), 1), :]


def _mod_spec(mod, n_seq, first_seq):
    assert first_seq % n_seq == 0
    return pl.BlockSpec((mod.shape[0], n_seq, mod.shape[2]), lambda *_: (0, first_seq // n_seq, 0))


def _const_spec(shape):
    nd = len(shape)
    return pl.BlockSpec(shape, lambda *_: (0,) * nd)


def _params(*sem):
    return pltpu.CompilerParams(dimension_semantics=sem, vmem_limit_bytes=VMEM_LIMIT)


def _ada_kernel(c_ref, w_ref, b_ref, o_ref):
    s = _silu(c_ref[...]).astype(bf16)
    o_ref[0] = jnp.dot(s, w_ref[...].astype(bf16), preferred_element_type=f32) + b_ref[...]


def _ada_mod(c, w, b):
    m, k = c.shape
    n = w.shape[1] // k
    return pl.pallas_call(
        _ada_kernel,
        grid=(n,),
        in_specs=[pl.BlockSpec((m, k), lambda j: (0, 0)),
                  pl.BlockSpec((k, k), lambda j: (0, j)),
                  pl.BlockSpec((1, k), lambda j: (0, j))],
        out_specs=pl.BlockSpec((1, m, k), lambda j: (j, 0, 0)),
        out_shape=jax.ShapeDtypeStruct((n, m, k), f32),
        compiler_params=_params("parallel"),
        name="ada_mod",
    )(c, w, b.reshape(1, n * k))


def _rows_in(x_ref, seq_major):
    if not seq_major:
        return x_ref[0]
    s, t, d = x_ref.shape
    return jnp.swapaxes(x_ref[...], 0, 1).reshape(s * t, d)


def _inproj_stages(x, sh, sc, g, outs):
    u = _rms_mod(x, g, sc, sh).astype(bf16)
    yield
    for write, w16 in outs:
        write(lax.dot_general(u, w16, NT_DIMS, preferred_element_type=f32))
        yield


def _inproj_kernel(x_ref, mod_ref, g_ref, w_ref, z_ref, xbc_ref, u5_ref, dt_ref, *, seq_major, sub):
    sh = _mod_row(mod_ref, 0, seq_major)
    sc = _mod_row(mod_ref, 1, seq_major)
    x = _rows_in(x_ref, seq_major)
    o_dt = D_SSD + CONV_DIM
    slabs = [(ref, w_ref[o:o + width, :].astype(bf16))
             for ref, o, width in ((z_ref, 0, D_SSD), (xbc_ref, D_SSD, CONV_DIM), (u5_ref, o_dt + HEADS, D_S5),
                                   (dt_ref, o_dt, DT_PAD))]

    def stages(r0):
        def to(ref):
            def write(v):
                ref[0, r0:r0 + sub, :] = v
            return write
        return _inproj_stages(x[r0:r0 + sub], sh, sc, g_ref[...], [(to(ref), w16) for ref, w16 in slabs])

    _interleave(*[stages(r0) for r0 in range(0, x.shape[0], sub)])


def _inproj(x, mod, g, w, *, tm, sub, seq_major, n_seq, first_seq):
    if seq_major:
        x_spec = _const_spec(x.shape)
        nb, rows, d = 1, x.shape[0] * x.shape[1], x.shape[2]
        assert tm == rows
    else:
        nb, rows, d = x.shape
        x_spec = pl.BlockSpec((1, tm, d), lambda i, j: (i, j, 0))
    nt = rows // tm
    widths = (D_SSD, CONV_DIM, D_S5, DT_PAD)
    return pl.pallas_call(
        functools.partial(_inproj_kernel, seq_major=seq_major, sub=sub),
        grid=(nb, nt),
        in_specs=[x_spec,
                  _mod_spec(mod, n_seq, first_seq),
                  _const_spec((1, d)),
                  pl.BlockSpec(w.shape, lambda i, j: (0, 0), pipeline_mode=pl.Buffered(1))],
        out_specs=[pl.BlockSpec((1, tm, wd), lambda i, j: (i, j, 0)) for wd in widths],
        out_shape=[jax.ShapeDtypeStruct((nb, rows, wd), f32) for wd in widths],
        compiler_params=_params("parallel", "parallel"),
        name="in_proj",
    )(x, mod, g, w)


def _split3(x):
    hi = x.astype(bf16)
    r1 = x - hi.astype(f32)
    mid = r1.astype(bf16)
    lo = (r1 - mid.astype(f32)).astype(bf16)
    return hi, mid, lo


def _dot_sel_lhs(sel16, x):
    return sum(jnp.dot(sel16, p, preferred_element_type=f32) for p in _split3(x))


def _dot_sel_rhs(x, sel16):
    return sum(jnp.dot(p, sel16, preferred_element_type=f32) for p in _split3(x))


def _gated_group_norm(y, z, ng):
    y = y * _silu(z)
    gw = D_SSD // GROUPS
    parts = []
    for g in range(GROUPS):
        yg = y[:, g * gw:(g + 1) * gw]
        parts.append(yg * lax.rsqrt(jnp.mean(yg * yg, axis=-1, keepdims=True) + EPS))
    return jnp.concatenate(parts, axis=-1) * ng


def _ssd_prompt_kernel(xbc_ref, z_ref, dt_ref, cw_ref, cb_ref, dtb_ref, alog_ref, dexp_ref, ng_ref,
                       y_ref, st_ref, cn_ref, ext_ref, act_ref, h_ref, *, cps):
    T = SSD_CHUNK
    gw = HEADS_PER_GROUP * HEAD_DIM

    @pl.when(pl.program_id(1) == 0)
    def _():
        ext_ref[0:SUBLANES, :] = jnp.zeros((SUBLANES, CONV_DIM), f32)
        h_ref[...] = jnp.zeros_like(h_ref)

    row = lax.broadcasted_iota(jnp.int32, (T, T), 0)
    col = lax.broadcasted_iota(jnp.int32, (T, T), 1)
    tri = row >= col
    tri16 = jnp.where(tri, 1.0, 0.0).astype(bf16)
    low_half = lax.broadcasted_iota(jnp.int32, (T, LANES), 1) < HEAD_DIM
    a_neg = -jnp.exp(alog_ref[...])

    R = cps * T
    ext_ref[SUBLANES:SUBLANES + R, :] = xbc_ref[0]
    for ci in range(cps):
        ext = ext_ref[ci * T:ci * T + SUBLANES + T, :]
        conv = cb_ref[...] + ext[SUBLANES:, :] * cw_ref[CONV_W - 1:CONV_W, :]
        for m in range(1, CONV_W):
            k = CONV_W - 1 - m
            conv = conv + pltpu.roll(ext, m, 0)[SUBLANES:, :] * cw_ref[k:k + 1, :]
        act_ref[ci * T:(ci + 1) * T, :] = _silu(conv)
    cn_ref[0] = ext_ref[SUBLANES + R - (CONV_W - 1):SUBLANES + R, :]
    ext_ref[0:SUBLANES, :] = ext_ref[R:R + SUBLANES, :]

    lane_head = lax.broadcasted_iota(jnp.int32, (DT_PAD, D_SSD), 1) // HEAD_DIM
    sel16 = jnp.where(lax.broadcasted_iota(jnp.int32, (DT_PAD, D_SSD), 0) == lane_head, 1.0, 0.0).astype(bf16)

    def spread(q):
        return _dot_sel_rhs(q, sel16)

    def state_free(ci):
        r0 = ci * T
        act = act_ref[r0:r0 + T, :]
        xs = act[:, :D_SSD]
        bm = act[:, D_SSD:D_SSD + GROUPS * STATE]
        cm = act[:, D_SSD + GROUPS * STATE:]
        dtv = jax.nn.softplus(dt_ref[0, r0:r0 + T, :] + dtb_ref[...])
        a_cs = _dot_sel_lhs(tri16, dtv * a_neg)
        bg16s = [bm[:, g * STATE:(g + 1) * STATE].astype(bf16) for g in range(GROUPS)]
        cg16s = [cm[:, g * STATE:(g + 1) * STATE].astype(bf16) for g in range(GROUPS)]
        gmats = [lax.dot_general(cg16s[g], bg16s[g], NT_DIMS, preferred_element_type=f32)
                 for g in range(GROUPS)]
        yield
        a_last = a_cs[T - 1:T, :]
        a_cs_t = a_cs.T
        rep = spread(jnp.concatenate([dtv, jnp.exp(a_last - a_cs), jnp.exp(a_cs)], axis=0))
        yield
        x = xs * rep[:T]
        x16 = x.astype(bf16)
        xd16 = (x * rep[T:2 * T]).astype(bf16)
        e_cs = rep[2 * T:]
        yield
        y_part = []
        for g in range(GROUPS):
            for pr in range(HEADS_PER_GROUP // 2):
                s_pair = []
                for q in range(2):
                    h = g * HEADS_PER_GROUP + 2 * pr + q
                    seg = a_cs[:, h:h + 1] - a_cs_t[h:h + 1, :]
                    lmat = jnp.exp(jnp.where(tri, seg, -jnp.inf))
                    s_pair.append((gmats[g] * lmat).astype(bf16))
                lanes = slice(g * gw + pr * LANES, g * gw + (pr + 1) * LANES)
                both = jnp.dot(jnp.concatenate(s_pair, axis=0), x16[:, lanes], preferred_element_type=f32)
                yield
                y_part.append(jnp.where(low_half, both[:T], both[T:]))
        y_free = jnp.concatenate(y_part, axis=-1) + xs * dexp_ref[...]
        return y_free, e_cs, jnp.exp(a_last), xd16, bg16s, cg16s

    def state_step(ci, y_free, e_cs, e_last, xd16, bg16s, cg16s):
        r0 = ci * T
        y_offs = []
        for g in range(GROUPS):
            hp = h_ref[g]
            y_offs.append(lax.dot_general(cg16s[g], hp.astype(bf16), NT_DIMS, preferred_element_type=f32))
            upd = lax.dot_general(xd16[:, g * gw:(g + 1) * gw], bg16s[g], TN_DIMS, preferred_element_type=f32)
            for hh in range(HEADS_PER_GROUP):
                h = g * HEADS_PER_GROUP + hh
                rows = slice(hh * HEAD_DIM, (hh + 1) * HEAD_DIM)
                h_ref[g, rows, :] = e_last[:, h:h + 1] * hp[rows] + upd[rows]
        y = y_free + jnp.concatenate(y_offs, axis=-1) * e_cs
        y_ref[0, r0:r0 + T, :] = _gated_group_norm(y, z_ref[0, r0:r0 + T, :], ng_ref[...])

    gens = [state_free(ci) for ci in range(cps)]
    free = [None] * cps
    while any(f is None for f in free):
        for ci in range(cps):
            if free[ci] is None:
                try:
                    next(gens[ci])
                except StopIteration as done:
                    free[ci] = done.value
    for ci in range(cps):
        state_step(ci, *free[ci])

    @pl.when(pl.program_id(1) == pl.num_programs(1) - 1)
    def _():
        st_ref[0] = h_ref[...].reshape(HEADS, HEAD_DIM, STATE)


def _ssd_prompt(xbc, z, dt, cw, cb, dtb, alog, dexp, ng, *, cps=8):
    nb, L, _ = xbc.shape
    T = SSD_CHUNK
    rows = cps * T
    blk = lambda wd: pl.BlockSpec((1, rows, wd), lambda b, c: (b, c, 0))
    return pl.pallas_call(
        functools.partial(_ssd_prompt_kernel, cps=cps),
        grid=(nb, L // rows),
        in_specs=[blk(CONV_DIM), blk(D_SSD), blk(DT_PAD),
                  _const_spec((CONV_W, CONV_DIM)), _const_spec((1, CONV_DIM)),
                  _const_spec((1, DT_PAD)), _const_spec((1, DT_PAD)),
                  _const_spec((1, D_SSD)), _const_spec((1, D_SSD))],
        out_specs=[blk(D_SSD),
                   pl.BlockSpec((1, HEADS, HEAD_DIM, STATE), lambda b, c: (b, 0, 0, 0)),
                   pl.BlockSpec((1, CONV_W - 1, CONV_DIM), lambda b, c: (b, 0, 0))],
        out_shape=[jax.ShapeDtypeStruct((nb, L, D_SSD), f32),
                   jax.ShapeDtypeStruct((nb, HEADS, HEAD_DIM, STATE), f32),
                   jax.ShapeDtypeStruct((nb, CONV_W - 1, CONV_DIM), f32)],
        scratch_shapes=[pltpu.VMEM((SUBLANES + rows, CONV_DIM), f32),
                        pltpu.VMEM((rows, CONV_DIM), f32),
                        pltpu.VMEM((GROUPS, HEADS_PER_GROUP * HEAD_DIM, STATE), f32)],
        compiler_params=_params("parallel", "arbitrary"),
        name="ssd_prompt",
    )(xbc, z, dt, cw, cb, dtb, alog, dexp, ng)


def _ssd_sample_kernel(xbc_ref, z_ref, dt_ref, cbuf_ref, st_ref, cw_ref, cb_ref, dtb_ref, alog_ref,
                       dexp_ref, ng_ref, y_ref, stn_ref, cn_ref, dtot_ref, *, L, nb):
    gw = HEADS_PER_GROUP * HEAD_DIM
    full = [cbuf_ref[j] for j in range(CONV_W - 1)] + [xbc_ref[t] for t in range(L)]
    for j in range(CONV_W - 1):
        cn_ref[j] = full[L + j]
    a_neg = -jnp.exp(alog_ref[...])
    xs, bm, cm, dtv, acs = [], [], [], [], []
    run = None
    for t in range(L):
        conv = cb_ref[...]
        for k in range(CONV_W):
            conv = conv + full[t + k] * cw_ref[k:k + 1, :]
        act = _silu(conv)
        xs.append(act[:, :D_SSD])
        bm.append(act[:, D_SSD:D_SSD + GROUPS * STATE])
        cm.append(act[:, D_SSD + GROUPS * STATE:])
        d = jax.nn.softplus(dt_ref[t] + dtb_ref[...])
        dtv.append(d)
        run = d * a_neg if run is None else run + d * a_neg
        acs.append(run)
    a_tot = acs[L - 1]
    dtot_ref[...] = jnp.exp(a_tot)

    lane_head = lax.broadcasted_iota(jnp.int32, (DT_PAD, D_SSD), 1) // HEAD_DIM
    sel16 = jnp.where(lax.broadcasted_iota(jnp.int32, (DT_PAD, D_SSD), 0) == lane_head, 1.0, 0.0).astype(bf16)
    pairs = [(t, s) for t in range(L) for s in range(t)]
    factors = (dtv + [jnp.exp(a_tot - acs[t]) for t in range(L)] + [jnp.exp(acs[t]) for t in range(L)]
               + [jnp.exp(acs[t] - acs[s]) for t, s in pairs])
    rep = _dot_sel_rhs(jnp.concatenate(factors, axis=0), sel16)
    piece = lambda i: rep[i * nb:(i + 1) * nb]
    x = [xs[t] * piece(t) for t in range(L)]
    xd_stack = jnp.concatenate([x[t] * piece(L + t) for t in range(L)], axis=0)
    e_cs = [piece(2 * L + t) for t in range(L)]
    decay = {ts: piece(3 * L + i) for i, ts in enumerate(pairs)}

    in_group0 = lax.broadcasted_iota(jnp.int32, (nb, D_SSD), 1) < gw
    y_intra = []
    for t in range(L):
        acc = None
        for s in range(t + 1):
            cb_dot = [jnp.sum(cm[t][:, g * STATE:(g + 1) * STATE] * bm[s][:, g * STATE:(g + 1) * STATE],
                              axis=-1, keepdims=True) for g in range(GROUPS)]
            w = jnp.where(in_group0, cb_dot[0], cb_dot[1])
            term = w * x[s] if s == t else w * decay[(t, s)] * x[s]
            acc = term if acc is None else acc + term
        y_intra.append(acc)

    c_stack = [jnp.concatenate([cm[t][:, g * STATE:(g + 1) * STATE] for t in range(L)], axis=0).astype(bf16)
               for g in range(GROUPS)]
    b_stack = [jnp.concatenate([bm[t][:, g * STATE:(g + 1) * STATE] for t in range(L)], axis=0).astype(bf16)
               for g in range(GROUPS)]
    seq_of_row = lax.broadcasted_iota(jnp.int32, (L * nb, gw), 0) & (nb - 1)

    def per_seq(b, yoff):
        mine = seq_of_row == b
        drow = dtot_ref[pl.ds(b, 1), :]
        out = []
        for g in range(GROUPS):
            h0 = st_ref[b, g]
            r = lax.dot_general(c_stack[g], h0.astype(bf16), NT_DIMS, preferred_element_type=f32)
            out.append(yoff[g] + jnp.where(mine, r, 0.0))
            xm = jnp.where(mine, xd_stack[:, g * gw:(g + 1) * gw], 0.0).astype(bf16)
            upd = lax.dot_general(xm, b_stack[g], TN_DIMS, preferred_element_type=f32)
            for hh in range(HEADS_PER_GROUP):
                h = g * HEADS_PER_GROUP + hh
                rows = slice(hh * HEAD_DIM, (hh + 1) * HEAD_DIM)
                stn_ref[b, g, rows, :] = drow[:, h:h + 1] * h0[rows] + upd[rows]
        return tuple(out)

    yoff = lax.fori_loop(0, nb, per_seq, tuple(jnp.zeros((L * nb, gw), f32) for _ in range(GROUPS)),
                         unroll=8)

    for t in range(L):
        y_off = jnp.concatenate([yoff[g][t * nb:(t + 1) * nb] for g in range(GROUPS)], axis=-1)
        y = y_intra[t] + y_off * e_cs[t] + xs[t] * dexp_ref[...]
        y_ref[t] = _gated_group_norm(y, z_ref[t], ng_ref[...])


def _ssd_sample(xbc, z, dt, cbuf, st, cw, cb, dtb, alog, dexp, ng, *, nb=16):
    L, B, _ = xbc.shape
    assert nb & (nb - 1) == 0 and B % nb == 0
    tblk = lambda n, wd: pl.BlockSpec((n, nb, wd), lambda i: (0, i, 0))
    gw = HEADS_PER_GROUP * HEAD_DIM
    st = st.reshape(B, GROUPS, gw, STATE)
    st_spec = pl.BlockSpec((nb, GROUPS, gw, STATE), lambda i: (i, 0, 0, 0))
    return pl.pallas_call(
        functools.partial(_ssd_sample_kernel, L=L, nb=nb),
        grid=(B // nb,),
        in_specs=[tblk(L, CONV_DIM), tblk(L, D_SSD), tblk(L, DT_PAD), tblk(CONV_W - 1, CONV_DIM), st_spec,
                  _const_spec((CONV_W, CONV_DIM)), _const_spec((1, CONV_DIM)),
                  _const_spec((1, DT_PAD)), _const_spec((1, DT_PAD)),
                  _const_spec((1, D_SSD)), _const_spec((1, D_SSD))],
        out_specs=[tblk(L, D_SSD), st_spec, tblk(CONV_W - 1, CONV_DIM)],
        out_shape=[jax.ShapeDtypeStruct((L, B, D_SSD), f32),
                   jax.ShapeDtypeStruct((B, GROUPS, gw, STATE), f32),
                   jax.ShapeDtypeStruct((CONV_W - 1, B, CONV_DIM), f32)],
        scratch_shapes=[pltpu.VMEM((nb, DT_PAD), f32)],
        compiler_params=_params("parallel"),
        name="ssd_sample",
    )(xbc, z, dt, cbuf, st, cw, cb, dtb, alog, dexp, ng)


def _s5_param_kernel(lr_ref, li_ref, ls_ref, br_ref, bi_ref, cr_ref, ci_ref,
                     ar_ref, ai_ref, bblk_ref, cblk_ref, b_scr, c_scr):
    lr, li = lr_ref[...], li_ref[...]
    step = jnp.exp(ls_ref[...])
    mag = jnp.exp(lr * step)
    abr = mag * jnp.cos(li * step)
    abi = mag * jnp.sin(li * step)
    nr, ni = abr - 1.0, abi
    den = lr * lr + li * li
    fr = (nr * lr + ni * li) / den
    fi = (ni * lr - nr * li) / den
    br, bi = br_ref[...], bi_ref[...]
    bbr = fr * br - fi * bi
    bbi = fr * bi + fi * br
    b_scr[...] = jnp.zeros_like(b_scr)
    c_scr[...] = jnp.zeros_like(c_scr)
    gps = S5_GROUPS // S5_SLABS
    ns = S5_SLAB_STATE
    for g in range(S5_GROUPS):
        s, gl = divmod(g, gps)
        ch = slice(gl * S5_CH, (gl + 1) * S5_CH)
        st = slice(gl * S5_STATE, (gl + 1) * S5_STATE)
        st_im = slice(ns + gl * S5_STATE, ns + (gl + 1) * S5_STATE)
        ar_ref[s, :, st] = jnp.broadcast_to(abr[g], (SUBLANES, S5_STATE))
        ai_ref[s, :, st] = jnp.broadcast_to(abi[g], (SUBLANES, S5_STATE))
        b_scr[s, ch, st] = bbr[g]
        b_scr[s, ch, st_im] = bbi[g]
        c_scr[s, st, ch] = cr_ref[g].T
        c_scr[s, st_im, ch] = -ci_ref[g].T
    bblk_ref[...] = b_scr[...].astype(bf16)
    cblk_ref[...] = c_scr[...].astype(bf16)


def _s5_params(a_re, a_im, log_step, b_re_t, b_im_t, c_re, c_im):
    g, p = a_re.shape
    ns = S5_SLAB_STATE
    return pl.pallas_call(
        _s5_param_kernel,
        out_shape=[jax.ShapeDtypeStruct((S5_SLABS, SUBLANES, ns), f32),
                   jax.ShapeDtypeStruct((S5_SLABS, SUBLANES, ns), f32),
                   jax.ShapeDtypeStruct((S5_SLABS, LANES, 2 * ns), bf16),
                   jax.ShapeDtypeStruct((S5_SLABS, 2 * ns, LANES), bf16)],
        scratch_shapes=[pltpu.VMEM((S5_SLABS, LANES, 2 * ns), f32),
                        pltpu.VMEM((S5_SLABS, 2 * ns, LANES), f32)],
        name="s5_params",
    )(a_re.reshape(g, 1, p), a_im.reshape(g, 1, p), log_step.reshape(g, 1, 1), b_re_t, b_im_t, c_re, c_im)


def _gelu(x):
    return 0.5 * x * (1.0 + lax.erf(x * (2.0 ** -0.5)))


def _s5_slabs_interleaved(ut_ref, hbuf_ref, hst_ref, g_ref, ar_ref, ai_ref, bblk_ref, cblk_ref, dsk_ref, tl,
                          row_block=256, n_stages=8):
    ns = S5_SLAB_STATE
    nb = SUBLANES
    rows = nb * tl

    def lanes(s):
        return slice(s * LANES, (s + 1) * LANES)

    def bu_stage(slabs):
        for s in slabs:
            for r in range(0, rows, row_block):
                us = ut_ref[r:r + row_block, lanes(s)]
                hbuf_ref[s, nb + r:nb + r + row_block, :] = jnp.dot(us.astype(bf16), bblk_ref[s],
                                                                    preferred_element_type=f32)
                yield

    def scan_stage(slabs):
        carry = [(hst_ref[s, :, :ns], hst_ref[s, :, ns:]) for s in slabs]
        coef = [(ar_ref[s], ai_ref[s]) for s in slabs]
        for t in range(tl):
            r = nb + nb * t
            for i, s in enumerate(slabs):
                (pr, pi), (ar, ai) = carry[i], coef[i]
                nr = ar * pr - ai * pi + hbuf_ref[s, r:r + nb, :ns]
                ni = ar * pi + ai * pr + hbuf_ref[s, r:r + nb, ns:]
                hbuf_ref[s, r:r + nb, :ns] = nr
                hbuf_ref[s, r:r + nb, ns:] = ni
                carry[i] = (nr, ni)
            if (t + 1) % (tl // n_stages) == 0:
                yield
        for i, s in enumerate(slabs):
            hst_ref[s, :, :ns] = carry[i][0]
            hst_ref[s, :, ns:] = carry[i][1]

    def y_stage(slabs):
        for s in slabs:
            for r in range(0, rows, row_block):
                h16 = hbuf_ref[s, nb + r:nb + r + row_block, :].astype(bf16)
                ys = jnp.dot(h16, cblk_ref[s], preferred_element_type=f32)
                ys = ys + dsk_ref[:, lanes(s)] * ut_ref[r:r + row_block, lanes(s)]
                g_ref[r:r + row_block, lanes(s)] = _gelu(ys)
                yield

    half = S5_SLABS // 2
    first, second = tuple(range(half)), tuple(range(half, S5_SLABS))
    _interleave(bu_stage(first))
    _interleave(scan_stage(first), bu_stage(second))
    _interleave(scan_stage(second), y_stage(first))
    _interleave(y_stage(second))


def _s5_scan_vmem(hbuf_ref, hst_ref, ar_ref, ai_ref, s, nb, rows):
    ns = S5_SLAB_STATE
    hbuf_ref[s, 0:nb, :] = hst_ref[s]
    ar, ai = ar_ref[s], ai_ref[s]

    def body(i, carry):
        j = pl.multiple_of(nb + i * SUBLANES, SUBLANES)
        prev = hbuf_ref[s, pl.ds(j - nb, SUBLANES), :]
        cur = hbuf_ref[s, pl.ds(j, SUBLANES), :]
        pr, pi = prev[:, :ns], prev[:, ns:]
        hbuf_ref[s, pl.ds(j, SUBLANES), :ns] = ar * pr - ai * pi + cur[:, :ns]
        hbuf_ref[s, pl.ds(j, SUBLANES), ns:] = ar * pi + ai * pr + cur[:, ns:]
        return carry

    lax.fori_loop(0, rows // SUBLANES, body, 0, unroll=min(8, nb // SUBLANES))
    hst_ref[s] = hbuf_ref[s, rows:rows + nb, :]


S5_N_IN = 10


def _s5_kernel(*refs, nb, tl, batch_major, n_cast):
    (u_ref, re0_ref, im0_ref, ar_ref, ai_ref, bblk_ref, cblk_ref, dsk_ref, wglu_ref,
     bglu_ref) = refs[:S5_N_IN]
    cast_in = refs[S5_N_IN:S5_N_IN + n_cast]
    y_ref, ren_ref, imn_ref = refs[S5_N_IN + n_cast:S5_N_IN + n_cast + 3]
    cast_out = refs[S5_N_IN + n_cast + 3:S5_N_IN + 2 * n_cast + 3]
    ut_ref, hbuf_ref, hst_ref, g_ref = refs[S5_N_IN + 2 * n_cast + 3:]
    for w_ref, w16_ref in zip(cast_in, cast_out):
        w16_ref[...] = w_ref[...].astype(bf16)
    step = pl.program_id(0)
    rows = nb * tl
    ns = S5_SLAB_STATE

    @pl.when(step == 0)
    def _():
        for s in range(S5_SLABS):
            hst_ref[s, :, :ns] = re0_ref[:, s * ns:(s + 1) * ns]
            hst_ref[s, :, ns:] = im0_ref[:, s * ns:(s + 1) * ns]

    if batch_major:
        ut_ref[...] = jnp.swapaxes(u_ref[...], 0, 1).reshape(rows, D_S5)
    else:
        ut_ref[...] = u_ref[...].reshape(rows, D_S5)

    if nb == SUBLANES:
        _s5_slabs_interleaved(ut_ref, hbuf_ref, hst_ref, g_ref, ar_ref, ai_ref, bblk_ref, cblk_ref, dsk_ref, tl)
    else:
        for s in range(S5_SLABS):
            sl = slice(s * LANES, (s + 1) * LANES)
            hbuf_ref[s, nb:nb + rows, :] = jnp.dot(ut_ref[:, sl].astype(bf16), bblk_ref[s],
                                                   preferred_element_type=f32)
            _s5_scan_vmem(hbuf_ref, hst_ref, ar_ref, ai_ref, s, nb, rows)
            ys = jnp.dot(hbuf_ref[s, nb:nb + rows, :].astype(bf16), cblk_ref[s], preferred_element_type=f32)
            g_ref[:, sl] = _gelu(ys + dsk_ref[:, sl] * ut_ref[:, sl])

    g = g_ref[...]
    gate = jnp.dot(g.astype(bf16), wglu_ref[...], preferred_element_type=f32) + bglu_ref[...]
    out = g * jax.nn.sigmoid(gate)
    if batch_major:
        y_ref[...] = jnp.swapaxes(out.reshape(tl, nb, D_S5), 0, 1)
    else:
        y_ref[...] = out.reshape(tl, nb, D_S5)

    @pl.when(step == pl.num_programs(0) - 1)
    def _():
        for s in range(S5_SLABS):
            ren_ref[:, s * ns:(s + 1) * ns] = hst_ref[s, :, :ns]
            imn_ref[:, s * ns:(s + 1) * ns] = hst_ref[s, :, ns:]


def _s5(u, re0, im0, ar, ai, bblk, cblk, dsk, wglu, bglu, *, tl, batch_major, cast=()):
    if batch_major:
        nb, L, _ = u.shape
        u_spec = pl.BlockSpec((nb, tl, D_S5), lambda i: (0, i, 0))
    else:
        L, nb, _ = u.shape
        u_spec = pl.BlockSpec((tl, nb, D_S5), lambda i: (i, 0, 0))
    rows = nb * tl
    steps = L // tl
    nstate = S5_GROUPS * S5_STATE
    st_spec = _const_spec((nb, nstate))
    cast_specs = [pl.BlockSpec((w.shape[0] // steps, w.shape[1]), lambda i: (i, 0)) for w in cast]
    assert all(w.shape[0] % (steps * 2 * SUBLANES) == 0 for w in cast)
    return pl.pallas_call(
        functools.partial(_s5_kernel, nb=nb, tl=tl, batch_major=batch_major, n_cast=len(cast)),
        grid=(steps,),
        in_specs=[u_spec, st_spec, st_spec,
                  _const_spec((S5_SLABS, SUBLANES, S5_SLAB_STATE)),
                  _const_spec((S5_SLABS, SUBLANES, S5_SLAB_STATE)),
                  _const_spec((S5_SLABS, LANES, 2 * S5_SLAB_STATE)),
                  _const_spec((S5_SLABS, 2 * S5_SLAB_STATE, LANES)),
                  _const_spec((1, D_S5)), _const_spec((D_S5, D_S5)), _const_spec((1, D_S5))] + cast_specs,
        out_specs=[u_spec, st_spec, st_spec] + cast_specs,
        out_shape=[jax.ShapeDtypeStruct(u.shape, f32),
                   jax.ShapeDtypeStruct((nb, nstate), f32),
                   jax.ShapeDtypeStruct((nb, nstate), f32)]
                  + [jax.ShapeDtypeStruct(w.shape, bf16) for w in cast],
        scratch_shapes=[pltpu.VMEM((rows, D_S5), f32),
                        pltpu.VMEM((S5_SLABS, nb + rows, 2 * S5_SLAB_STATE), f32),
                        pltpu.VMEM((S5_SLABS, nb, 2 * S5_SLAB_STATE), f32),
                        pltpu.VMEM((rows, D_S5), f32)],
        compiler_params=_params("arbitrary"),
        name="s5_mixer",
    )(u, re0, im0, ar, ai, bblk, cblk, dsk, wglu, bglu, *cast)


def _ffn_stages(x, ys, y5, mods, n2g, nfg, wo_ref, wg_ref, wu_ref, wd_ref, write_out, ff_chunk):
    g1, sh2, sc2, g2, shf, scf = mods
    att = jnp.dot(ys.astype(bf16), wo_ref[:D_SSD, :], preferred_element_type=f32)
    att = att + jnp.dot(y5.astype(bf16), wo_ref[D_SSD:, :], preferred_element_type=f32)
    yield
    x1 = x + _by_row(lambda v, s: v * s, att, g1)
    v = _rms_mod(x1, n2g, sc2, sh2).astype(bf16)
    ff = None
    for o in range(0, D_FF, ff_chunk):
        gate = jnp.dot(v, wg_ref[:, o:o + ff_chunk], preferred_element_type=f32)
        up = jnp.dot(v, wu_ref[:, o:o + ff_chunk], preferred_element_type=f32)
        hid = (_silu(gate) * up).astype(bf16)
        part = jnp.dot(hid, wd_ref[o:o + ff_chunk, :], preferred_element_type=f32)
        ff = part if ff is None else ff + part
        yield
    x2 = x1 + _by_row(lambda v, s: v * s, ff, g2)
    write_out(_rms_mod(x2, nfg, scf, shf))


def _ffn_kernel(x_ref, ys_ref, y5_ref, mod_ref, modf_ref, n2g_ref, nfg_ref, wo_ref, wg_ref, wu_ref, wd_ref,
                o_ref, *, seq_major, ff_chunk, sub):
    mods = (_mod_row(mod_ref, 2, seq_major), _mod_row(mod_ref, 3, seq_major), _mod_row(mod_ref, 4, seq_major),
            _mod_row(mod_ref, 5, seq_major), _mod_row(modf_ref, 0, seq_major), _mod_row(modf_ref, 1, seq_major))
    x = _rows_in(x_ref, seq_major)
    rows = x.shape[0]
    tiles = {}

    def stages(r0):
        def write_out(y):
            if seq_major:
                tiles[r0] = y
            else:
                o_ref[0, r0:r0 + sub, :] = y
        return _ffn_stages(x[r0:r0 + sub], ys_ref[0, r0:r0 + sub, :], y5_ref[0, r0:r0 + sub, :], mods,
                           n2g_ref[...], nfg_ref[...], wo_ref, wg_ref, wu_ref, wd_ref, write_out, ff_chunk)

    _interleave(*[stages(r0) for r0 in range(0, rows, sub)])
    if seq_major:
        s, t, d = o_ref.shape
        y = jnp.concatenate([tiles[r0] for r0 in range(0, rows, sub)], axis=0)
        o_ref[...] = jnp.swapaxes(y.reshape(t, s, d), 0, 1)


def _ffn(x, ys, y5, mod, modf, n2g, nfg, wo, wg, wu, wd, *, tm, sub, seq_major, n_seq, first_seq, ff_chunk=256):
    if seq_major:
        x_spec = _const_spec(x.shape)
        nb, rows, d = 1, x.shape[0] * x.shape[1], x.shape[2]
        assert tm == rows
    else:
        nb, rows, d = x.shape
        x_spec = pl.BlockSpec((1, tm, d), lambda i, j: (i, j, 0))
    nt = rows // tm
    mod_spec = _mod_spec(mod, n_seq, first_seq)
    modf_spec = _mod_spec(modf, n_seq, first_seq)
    blk = lambda wd_: pl.BlockSpec((1, tm, wd_), lambda i, j: (i, j, 0))
    single = dict(pipeline_mode=pl.Buffered(1))
    wspec = lambda shape: pl.BlockSpec(shape, lambda i, j: (0, 0), **single)
    return pl.pallas_call(
        functools.partial(_ffn_kernel, seq_major=seq_major, ff_chunk=ff_chunk, sub=sub),
        grid=(nb, nt),
        in_specs=[x_spec, blk(D_SSD), blk(D_S5), mod_spec, modf_spec,
                  _const_spec((1, d)), _const_spec((1, d)),
                  wspec((d, d)), wspec((d, D_FF)), wspec((d, D_FF)), wspec((D_FF, d))],
        out_specs=x_spec,
        out_shape=jax.ShapeDtypeStruct(x.shape, f32),
        compiler_params=_params("parallel", "parallel"),
        name="out_ffn",
    )(x, ys, y5, mod, modf, n2g, nfg, wo, wg, wu, wd)


def kernel(x_prompt, x_sample, c_prompt, c_sample, state_ssd, state_conv, state_s5_re, state_s5_im, w_ada, b_ada, norm1_g, w_in, conv_w, conv_b, ssd_dt_bias, ssd_A_log, ssd_D, ssd_norm_g, s5_A_re, s5_A_im, s5_log_step, s5_B_re, s5_B_im, s5_C_re, s5_C_im, s5_D, w_glu, b_glu, w_out, norm2_g, w_ffn_gate, w_ffn_up, w_ffn_down, w_ada_f, b_ada_f, normf_g):
    assert w_ada.shape[0] == 1, "single-layer stack"
    bp, seq, d = x_prompt.shape
    bs, dseq, _ = x_sample.shape

    c_all = jnp.concatenate([c_sample, c_prompt], axis=0)
    mod = _ada_mod(c_all, w_ada[0], b_ada[0])
    modf = _ada_mod(c_all, w_ada_f, b_ada_f)
    seqs_s = dict(n_seq=bs, first_seq=0)
    seqs_p = dict(n_seq=bp, first_seq=bs)

    w_in_p = w_in[0].T
    pad_h = lambda v: jnp.concatenate([v, jnp.zeros((DT_PAD - HEADS,), f32)]).reshape(1, DT_PAD)
    dtb = pad_h(ssd_dt_bias[0])
    alog = pad_h(ssd_A_log[0])
    dexp = jnp.repeat(ssd_D[0], HEAD_DIM).reshape(1, D_SSD)
    ng = ssd_norm_g[0].reshape(1, D_SSD)
    cw, cb = conv_w[0], conv_b[0].reshape(1, CONV_DIM)

    ar, ai, bblk, cblk = _s5_params(s5_A_re[0], s5_A_im[0], s5_log_step[0],
                                    s5_B_re[0].transpose(0, 2, 1), s5_B_im[0].transpose(0, 2, 1),
                                    s5_C_re[0], s5_C_im[0])
    dsk = s5_D[0].reshape(1, D_S5)
    wglu = w_glu[0].astype(bf16)
    bglu = b_glu[0].reshape(1, D_S5)
    n1g, n2g, nfg = norm1_g[0].reshape(1, d), norm2_g[0].reshape(1, d), normf_g.reshape(1, d)
    nstate = S5_GROUPS * S5_STATE

    z, xbc, u5, dt = _inproj(x_prompt, mod, n1g, w_in_p, tm=1024, sub=512, seq_major=False, **seqs_p)
    y_ssd, ssd_p, conv_p = _ssd_prompt(xbc, z, dt, cw, cb, dtb, alog, dexp, ng)
    zeros_st = jnp.zeros((bp, nstate), f32)
    y_s5, re_p, im_p, wo, wg, wu, wd = _s5(
        u5, zeros_st, zeros_st, ar, ai, bblk, cblk, dsk, wglu, bglu, tl=128, batch_major=True,
        cast=(w_out[0], w_ffn_gate[0], w_ffn_up[0], w_ffn_down[0]))
    y_prompt = _ffn(x_prompt, y_ssd, y_s5, mod, modf, n2g, nfg, wo, wg, wu, wd, tm=1024, sub=512,
                    seq_major=False, **seqs_p)

    rows_s = dseq * bs
    steps = lambda a: a.reshape(dseq, bs, a.shape[-1])
    flat = lambda a: a.reshape(1, rows_s, a.shape[-1])
    z, xbc, u5, dt = _inproj(x_sample, mod, n1g, w_in_p, tm=rows_s, sub=rows_s // 2, seq_major=True, **seqs_s)
    y_ssd, ssd_s, conv_s = _ssd_sample(steps(xbc), steps(z), steps(dt), state_conv[0].transpose(1, 0, 2),
                                       state_ssd[0], cw, cb, dtb, alog, dexp, ng)
    y_s5, re_s, im_s = _s5(steps(u5), state_s5_re[0].reshape(bs, nstate), state_s5_im[0].reshape(bs, nstate),
                           ar, ai, bblk, cblk, dsk, wglu, bglu, tl=dseq, batch_major=False)
    y_sample = _ffn(x_sample, flat(y_ssd), flat(y_s5), mod, modf, n2g, nfg, wo, wg, wu, wd,
                    tm=rows_s, sub=rows_s, seq_major=True, **seqs_s)

    g5 = (S5_GROUPS, S5_STATE)
    return (y_prompt, y_sample,
            ssd_p[None], ssd_s.reshape((1,) + state_ssd.shape[1:]),
            conv_p[None], conv_s.transpose(1, 0, 2)[None],
            re_p.reshape((1, bp) + g5), re_s.reshape((1, bs) + g5),
            im_p.reshape((1, bp) + g5), im_s.reshape((1, bs) + g5))
```

```python
import functools

import jax
import jax.numpy as jnp
from jax import lax
from jax.experimental import pallas as pl
from jax.experimental.pallas import tpu as pltpu

f32 = jnp.float32
bf16 = jnp.bfloat16

D_MODEL = 1024
D_SSD = 512
HEAD_DIM = 64
HEADS = 8
GROUPS = 2
HEADS_PER_GROUP = HEADS // GROUPS
STATE = 128
CONV_W = 4
CONV_DIM = D_SSD + 2 * GROUPS * STATE
D_S5 = 512
S5_CH = 16
S5_GROUPS = 32
S5_STATE = 64
D_FF = 2816
N_ADA = 6
EPS = 1e-6

LANES = 128
SUBLANES = 8
SSD_CHUNK = 128
S5_SLABS = D_S5 // LANES
S5_SLAB_STATE = (S5_GROUPS // S5_SLABS) * S5_STATE
DT_PAD = LANES
VMEM_LIMIT = 56 * 1024 * 1024

NT_DIMS = (((1,), (1,)), ((), ()))
TN_DIMS = (((0,), (0,)), ((), ()))


def _silu(x):
    return x * jax.nn.sigmoid(x)


def _interleave(*gens):
    live = list(gens)
    while live:
        for g in list(live):
            try:
                next(g)
            except StopIteration:
                live.remove(g)


def _by_row(fn, v, *ms):
    r, d = v.shape
    m_rows = ms[0].shape[0]
    if m_rows in (1, r):
        return fn(v, *ms)
    out = fn(v.reshape(r // m_rows, m_rows, d), *[m[None] for m in ms])
    return out.reshape(r, d)


def _rms_mod(x, g, sc, sh):
    y = x * lax.rsqrt(jnp.mean(x * x, axis=-1, keepdims=True) + EPS)
    return _by_row(lambda v, s, t: v * (1.0 + s) + t, y * g, sc, sh)


def _mod_row(mod_ref, i, per_row):
    return mod_ref[i] if per_row else mod_ref[i, pl.ds(pl.program_id(0), 1), :]


def _mod_spec(mod, n_seq, first_seq):
    assert first_seq % n_seq == 0
    return pl.BlockSpec((mod.shape[0], n_seq, mod.shape[2]), lambda *_: (0, first_seq // n_seq, 0))


def _const_spec(shape):
    nd = len(shape)
    return pl.BlockSpec(shape, lambda *_: (0,) * nd)


def _params(*sem):
    return pltpu.CompilerParams(dimension_semantics=sem, vmem_limit_bytes=VMEM_LIMIT)


def _ada_kernel(c_ref, w_ref, b_ref, o_ref):
    s = _silu(c_ref[...]).astype(bf16)
    o_ref[0] = jnp.dot(s, w_ref[...].astype(bf16), preferred_element_type=f32) + b_ref[...]


def _ada_mod(c, w, b):
    m, k = c.shape
    n = w.shape[1] // k
    return pl.pallas_call(
        _ada_kernel,
        grid=(n,),
        in_specs=[pl.BlockSpec((m, k), lambda j: (0, 0)),
                  pl.BlockSpec((k, k), lambda j: (0, j)),
                  pl.BlockSpec((1, k), lambda j: (0, j))],
        out_specs=pl.BlockSpec((1, m, k), lambda j: (j, 0, 0)),
        out_shape=jax.ShapeDtypeStruct((n, m, k), f32),
        compiler_params=_params("parallel"),
        name="ada_mod",
    )(c, w, b.reshape(1, n * k))


def _rows_in(x_ref, seq_major):
    if not seq_major:
        return x_ref[0]
    s, t, d = x_ref.shape
    return jnp.swapaxes(x_ref[...], 0, 1).reshape(s * t, d)


def _conv_silu(ext, cw_ref, cb_ref):
    conv = cb_ref[...] + ext[SUBLANES:, :] * cw_ref[CONV_W - 1:CONV_W, :]
    for m in range(1, CONV_W):
        k = CONV_W - 1 - m
        conv = conv + pltpu.roll(ext, m, 0)[SUBLANES:, :] * cw_ref[k:k + 1, :]
    return _silu(conv)


def _inproj_stages(x, sh, sc, g, outs, after=()):
    u = _rms_mod(x, g, sc, sh).astype(bf16)
    yield
    for write, w16 in outs:
        write(lax.dot_general(u, w16, NT_DIMS, preferred_element_type=f32))
        yield
    for stage in after:
        stage()
        yield


def _inproj_kernel(x_ref, mod_ref, g_ref, w_ref, *rest, seq_major, sub, conv):
    if conv:
        cw_ref, cb_ref, z_ref, act_ref, u5_ref, dt_ref, cn_ref, ext_ref = rest
        pl.when(pl.program_id(1) == 0)(lambda: ext_ref.__setitem__(
            (slice(0, SUBLANES), slice(None)), jnp.zeros((SUBLANES, CONV_DIM), f32)))
    else:
        z_ref, xbc_ref, u5_ref, dt_ref = rest
    sh = _mod_row(mod_ref, 0, seq_major)
    sc = _mod_row(mod_ref, 1, seq_major)
    x = _rows_in(x_ref, seq_major)
    rows = x.shape[0]
    T = SSD_CHUNK
    o_dt = D_SSD + CONV_DIM
    w16 = lambda o, width: w_ref[o:o + width, :].astype(bf16)
    w_z, w_xbc, w_u5, w_dt = w16(0, D_SSD), w16(D_SSD, CONV_DIM), w16(o_dt + HEADS, D_S5), w16(o_dt, DT_PAD)

    def stages(r0):
        def to(ref):
            def write(v):
                ref[0, r0:r0 + sub, :] = v
            return write

        def to_ext(v):
            ext_ref[SUBLANES + r0:SUBLANES + r0 + sub, :] = v

        def conv_chunk(c0):
            def stage():
                act_ref[0, c0:c0 + T, :] = _conv_silu(ext_ref[c0:c0 + SUBLANES + T, :], cw_ref, cb_ref)
            return stage

        outs = [(to(z_ref), w_z), (to_ext if conv else to(xbc_ref), w_xbc), (to(u5_ref), w_u5), (to(dt_ref), w_dt)]
        after = [conv_chunk(c0) for c0 in range(r0, r0 + sub, T)] if conv else ()
        return _inproj_stages(x[r0:r0 + sub], sh, sc, g_ref[...], outs, after)

    _interleave(*[stages(r0) for r0 in range(0, rows, sub)])
    if conv:
        cn_ref[0] = ext_ref[SUBLANES + rows - (CONV_W - 1):SUBLANES + rows, :]
        ext_ref[0:SUBLANES, :] = ext_ref[rows:rows + SUBLANES, :]


def _inproj(x, mod, g, w, conv_wb=None, *, tm, sub, seq_major, n_seq, first_seq):
    if seq_major:
        x_spec = _const_spec(x.shape)
        nb, rows, d = 1, x.shape[0] * x.shape[1], x.shape[2]
        assert tm == rows
    else:
        nb, rows, d = x.shape
        x_spec = pl.BlockSpec((1, tm, d), lambda i, j: (i, j, 0))
    nt = rows // tm
    widths = (D_SSD, CONV_DIM, D_S5, DT_PAD)
    conv = conv_wb is not None
    in_specs = [x_spec, _mod_spec(mod, n_seq, first_seq), _const_spec((1, d)),
                pl.BlockSpec(w.shape, lambda i, j: (0, 0), pipeline_mode=pl.Buffered(1))]
    out_specs = [pl.BlockSpec((1, tm, wd), lambda i, j: (i, j, 0)) for wd in widths]
    out_shape = [jax.ShapeDtypeStruct((nb, rows, wd), f32) for wd in widths]
    scratch = []
    if conv:
        in_specs += [_const_spec((CONV_W, CONV_DIM)), _const_spec((1, CONV_DIM))]
        out_specs.append(pl.BlockSpec((1, CONV_W - 1, CONV_DIM), lambda i, j: (i, 0, 0)))
        out_shape.append(jax.ShapeDtypeStruct((nb, CONV_W - 1, CONV_DIM), f32))
        scratch.append(pltpu.VMEM((SUBLANES + tm, CONV_DIM), f32))
    return pl.pallas_call(
        functools.partial(_inproj_kernel, seq_major=seq_major, sub=sub, conv=conv),
        grid=(nb, nt),
        in_specs=in_specs, out_specs=out_specs, out_shape=out_shape, scratch_shapes=scratch,
        compiler_params=_params("parallel", "arbitrary" if conv else "parallel"),
        name="in_proj",
    )(x, mod, g, w, *(conv_wb or ()))


def _split3(x):
    hi = x.astype(bf16)
    r1 = x - hi.astype(f32)
    mid = r1.astype(bf16)
    lo = (r1 - mid.astype(f32)).astype(bf16)
    return hi, mid, lo


def _dot_sel_lhs(sel16, x):
    return sum(jnp.dot(sel16, p, preferred_element_type=f32) for p in _split3(x))


def _dot_sel_rhs(x, sel16):
    hi, mid, _ = _split3(x)
    return jnp.dot(hi, sel16, preferred_element_type=f32) + jnp.dot(mid, sel16, preferred_element_type=f32)


def _gated_group_norm(y, z, ng):
    y = y * _silu(z)
    gw = D_SSD // GROUPS
    parts = []
    for g in range(GROUPS):
        yg = y[:, g * gw:(g + 1) * gw]
        parts.append(yg * lax.rsqrt(jnp.mean(yg * yg, axis=-1, keepdims=True) + EPS))
    return jnp.concatenate(parts, axis=-1) * ng


def _ssd_prompt_kernel(act_ref, z_ref, dt_ref, dtb_ref, alog_ref, dexp_ref, ng_ref, y_ref, st_ref, h_ref, *, cps):
    T = SSD_CHUNK
    gw = HEADS_PER_GROUP * HEAD_DIM

    @pl.when(pl.program_id(1) == 0)
    def _():
        h_ref[...] = jnp.zeros_like(h_ref)

    row = lax.broadcasted_iota(jnp.int32, (T, T), 0)
    col = lax.broadcasted_iota(jnp.int32, (T, T), 1)
    tri = row >= col
    tri16 = jnp.where(tri, 1.0, 0.0).astype(bf16)
    low_half = lax.broadcasted_iota(jnp.int32, (T, LANES), 1) < HEAD_DIM
    a_neg = -jnp.exp(alog_ref[...])

    lane_head = lax.broadcasted_iota(jnp.int32, (DT_PAD, D_SSD), 1) // HEAD_DIM
    sel16 = jnp.where(lax.broadcasted_iota(jnp.int32, (DT_PAD, D_SSD), 0) == lane_head, 1.0, 0.0).astype(bf16)

    def spread(q):
        return _dot_sel_rhs(q, sel16)

    def state_free(ci):
        r0 = ci * T
        act = act_ref[0, r0:r0 + T, :]
        xs = act[:, :D_SSD]
        bm = act[:, D_SSD:D_SSD + GROUPS * STATE]
        cm = act[:, D_SSD + GROUPS * STATE:]
        dtv = jax.nn.softplus(dt_ref[0, r0:r0 + T, :] + dtb_ref[...])
        a_cs = _dot_sel_lhs(tri16, dtv * a_neg)
        bg16s = [bm[:, g * STATE:(g + 1) * STATE].astype(bf16) for g in range(GROUPS)]
        cg16s = [cm[:, g * STATE:(g + 1) * STATE].astype(bf16) for g in range(GROUPS)]
        gmats = [lax.dot_general(cg16s[g], bg16s[g], NT_DIMS, preferred_element_type=f32)
                 for g in range(GROUPS)]
        yield
        a_last = a_cs[T - 1:T, :]
        a_cs_t = a_cs.T
        rep = spread(jnp.concatenate([dtv, jnp.exp(a_last - a_cs), jnp.exp(a_cs)], axis=0))
        yield
        x = xs * rep[:T]
        x16 = x.astype(bf16)
        xd16 = (x * rep[T:2 * T]).astype(bf16)
        e_cs = rep[2 * T:]
        yield
        y_part = []
        for g in range(GROUPS):
            for pr in range(HEADS_PER_GROUP // 2):
                s_pair = []
                for q in range(2):
                    h = g * HEADS_PER_GROUP + 2 * pr + q
                    seg = a_cs[:, h:h + 1] - a_cs_t[h:h + 1, :]
                    lmat = jnp.exp(jnp.where(tri, seg, -jnp.inf))
                    s_pair.append((gmats[g] * lmat).astype(bf16))
                lanes = slice(g * gw + pr * LANES, g * gw + (pr + 1) * LANES)
                both = jnp.dot(jnp.concatenate(s_pair, axis=0), x16[:, lanes], preferred_element_type=f32)
                yield
                y_part.append(jnp.where(low_half, both[:T], both[T:]))
        y_free = jnp.concatenate(y_part, axis=-1) + xs * dexp_ref[...]
        return y_free, e_cs, jnp.exp(a_last), xd16, bg16s, cg16s

    def state_step(ci, y_free, e_cs, e_last, xd16, bg16s, cg16s):
        r0 = ci * T
        y_offs = []
        for g in range(GROUPS):
            hp = h_ref[g]
            y_offs.append(lax.dot_general(cg16s[g], hp.astype(bf16), NT_DIMS, preferred_element_type=f32))
            upd = lax.dot_general(xd16[:, g * gw:(g + 1) * gw], bg16s[g], TN_DIMS, preferred_element_type=f32)
            for hh in range(HEADS_PER_GROUP):
                h = g * HEADS_PER_GROUP + hh
                rows = slice(hh * HEAD_DIM, (hh + 1) * HEAD_DIM)
                h_ref[g, rows, :] = e_last[:, h:h + 1] * hp[rows] + upd[rows]
        y = y_free + jnp.concatenate(y_offs, axis=-1) * e_cs
        y_ref[0, r0:r0 + T, :] = _gated_group_norm(y, z_ref[0, r0:r0 + T, :], ng_ref[...])

    gens = [state_free(ci) for ci in range(cps)]
    free = [None] * cps
    while any(f is None for f in free):
        for ci in range(cps):
            if free[ci] is None:
                try:
                    next(gens[ci])
                except StopIteration as done:
                    free[ci] = done.value
    for ci in range(cps):
        state_step(ci, *free[ci])

    @pl.when(pl.program_id(1) == pl.num_programs(1) - 1)
    def _():
        st_ref[0] = h_ref[...].reshape(HEADS, HEAD_DIM, STATE)


def _ssd_prompt(act, z, dt, dtb, alog, dexp, ng, *, cps=8):
    nb, L, _ = act.shape
    rows = cps * SSD_CHUNK
    blk = lambda wd: pl.BlockSpec((1, rows, wd), lambda b, c: (b, c, 0))
    return pl.pallas_call(
        functools.partial(_ssd_prompt_kernel, cps=cps),
        grid=(nb, L // rows),
        in_specs=[blk(CONV_DIM), blk(D_SSD), blk(DT_PAD), _const_spec((1, DT_PAD)), _const_spec((1, DT_PAD)),
                  _const_spec((1, D_SSD)), _const_spec((1, D_SSD))],
        out_specs=[blk(D_SSD), pl.BlockSpec((1, HEADS, HEAD_DIM, STATE), lambda b, c: (b, 0, 0, 0))],
        out_shape=[jax.ShapeDtypeStruct((nb, L, D_SSD), f32),
                   jax.ShapeDtypeStruct((nb, HEADS, HEAD_DIM, STATE), f32)],
        scratch_shapes=[pltpu.VMEM((GROUPS, HEADS_PER_GROUP * HEAD_DIM, STATE), f32)],
        compiler_params=_params("parallel", "arbitrary"),
        name="ssd_prompt",
    )(act, z, dt, dtb, alog, dexp, ng)


def _ssd_sample_kernel(xbc_ref, z_ref, dt_ref, cbuf_ref, st_ref, cw_ref, cb_ref, dtb_ref, alog_ref,
                       dexp_ref, ng_ref, y_ref, stn_ref, cn_ref, dtot_ref, *, L, nb):
    gw = HEADS_PER_GROUP * HEAD_DIM
    full = [cbuf_ref[j] for j in range(CONV_W - 1)] + [xbc_ref[t] for t in range(L)]
    for j in range(CONV_W - 1):
        cn_ref[j] = full[L + j]
    a_neg = -jnp.exp(alog_ref[...])
    xs, bm, cm, dtv, acs = [], [], [], [], []
    run = None
    for t in range(L):
        conv = cb_ref[...]
        for k in range(CONV_W):
            conv = conv + full[t + k] * cw_ref[k:k + 1, :]
        act = _silu(conv)
        xs.append(act[:, :D_SSD])
        bm.append(act[:, D_SSD:D_SSD + GROUPS * STATE])
        cm.append(act[:, D_SSD + GROUPS * STATE:])
        d = jax.nn.softplus(dt_ref[t] + dtb_ref[...])
        dtv.append(d)
        run = d * a_neg if run is None else run + d * a_neg
        acs.append(run)
    a_tot = acs[L - 1]
    dtot_ref[...] = jnp.exp(a_tot)

    lane_head = lax.broadcasted_iota(jnp.int32, (DT_PAD, D_SSD), 1) // HEAD_DIM
    sel16 = jnp.where(lax.broadcasted_iota(jnp.int32, (DT_PAD, D_SSD), 0) == lane_head, 1.0, 0.0).astype(bf16)
    pairs = [(t, s) for t in range(L) for s in range(t)]
    factors = (dtv + [jnp.exp(a_tot - acs[t]) for t in range(L)] + [jnp.exp(acs[t]) for t in range(L)]
               + [jnp.exp(acs[t] - acs[s]) for t, s in pairs])
    rep = _dot_sel_rhs(jnp.concatenate(factors, axis=0), sel16)
    piece = lambda i: rep[i * nb:(i + 1) * nb]
    x = [xs[t] * piece(t) for t in range(L)]
    xd_stack = jnp.concatenate([x[t] * piece(L + t) for t in range(L)], axis=0)
    e_cs = [piece(2 * L + t) for t in range(L)]
    decay = {ts: piece(3 * L + i) for i, ts in enumerate(pairs)}

    in_group0 = lax.broadcasted_iota(jnp.int32, (nb, D_SSD), 1) < gw
    y_intra = []
    for t in range(L):
        acc = None
        for s in range(t + 1):
            cb_dot = [jnp.sum(cm[t][:, g * STATE:(g + 1) * STATE] * bm[s][:, g * STATE:(g + 1) * STATE],
                              axis=-1, keepdims=True) for g in range(GROUPS)]
            w = jnp.where(in_group0, cb_dot[0], cb_dot[1])
            term = w * x[s] if s == t else w * decay[(t, s)] * x[s]
            acc = term if acc is None else acc + term
        y_intra.append(acc)

    c_stack = [jnp.concatenate([cm[t][:, g * STATE:(g + 1) * STATE] for t in range(L)], axis=0).astype(bf16)
               for g in range(GROUPS)]
    b_stack = [jnp.concatenate([bm[t][:, g * STATE:(g + 1) * STATE] for t in range(L)], axis=0).astype(bf16)
               for g in range(GROUPS)]
    seq_of_row = lax.broadcasted_iota(jnp.int32, (L * nb, gw), 0) & (nb - 1)

    def per_seq(b, yoff):
        mine = seq_of_row == b
        drow = dtot_ref[pl.ds(b, 1), :]
        out = []
        for g in range(GROUPS):
            h0 = st_ref[b, g]
            r = lax.dot_general(c_stack[g], h0.astype(bf16), NT_DIMS, preferred_element_type=f32)
            out.append(yoff[g] + jnp.where(mine, r, 0.0))
            xm = jnp.where(mine, xd_stack[:, g * gw:(g + 1) * gw], 0.0).astype(bf16)
            upd = lax.dot_general(xm, b_stack[g], TN_DIMS, preferred_element_type=f32)
            for hh in range(HEADS_PER_GROUP):
                h = g * HEADS_PER_GROUP + hh
                rows = slice(hh * HEAD_DIM, (hh + 1) * HEAD_DIM)
                stn_ref[b, g, rows, :] = drow[:, h:h + 1] * h0[rows] + upd[rows]
        return tuple(out)

    yoff = lax.fori_loop(0, nb, per_seq, tuple(jnp.zeros((L * nb, gw), f32) for _ in range(GROUPS)),
                         unroll=8)

    for t in range(L):
        y_off = jnp.concatenate([yoff[g][t * nb:(t + 1) * nb] for g in range(GROUPS)], axis=-1)
        y = y_intra[t] + y_off * e_cs[t] + xs[t] * dexp_ref[...]
        y_ref[t] = _gated_group_norm(y, z_ref[t], ng_ref[...])


def _ssd_sample(xbc, z, dt, cbuf, st, cw, cb, dtb, alog, dexp, ng, *, nb=16):
    L, B, _ = xbc.shape
    assert nb & (nb - 1) == 0 and B % nb == 0
    tblk = lambda n, wd: pl.BlockSpec((n, nb, wd), lambda i: (0, i, 0))
    gw = HEADS_PER_GROUP * HEAD_DIM
    st = st.reshape(B, GROUPS, gw, STATE)
    st_spec = pl.BlockSpec((nb, GROUPS, gw, STATE), lambda i: (i, 0, 0, 0))
    return pl.pallas_call(
        functools.partial(_ssd_sample_kernel, L=L, nb=nb),
        grid=(B // nb,),
        in_specs=[tblk(L, CONV_DIM), tblk(L, D_SSD), tblk(L, DT_PAD), tblk(CONV_W - 1, CONV_DIM), st_spec,
                  _const_spec((CONV_W, CONV_DIM)), _const_spec((1, CONV_DIM)),
                  _const_spec((1, DT_PAD)), _const_spec((1, DT_PAD)),
                  _const_spec((1, D_SSD)), _const_spec((1, D_SSD))],
        out_specs=[tblk(L, D_SSD), st_spec, tblk(CONV_W - 1, CONV_DIM)],
        out_shape=[jax.ShapeDtypeStruct((L, B, D_SSD), f32),
                   jax.ShapeDtypeStruct((B, GROUPS, gw, STATE), f32),
                   jax.ShapeDtypeStruct((CONV_W - 1, B, CONV_DIM), f32)],
        scratch_shapes=[pltpu.VMEM((nb, DT_PAD), f32)],
        compiler_params=_params("parallel"),
        name="ssd_sample",
    )(xbc, z, dt, cbuf, st, cw, cb, dtb, alog, dexp, ng)


def _s5_param_kernel(lr_ref, li_ref, ls_ref, br_ref, bi_ref, cr_ref, ci_ref,
                     ar_ref, ai_ref, bblk_ref, cblk_ref, b_scr, c_scr):
    lr, li = lr_ref[...], li_ref[...]
    step = jnp.exp(ls_ref[...])
    mag = jnp.exp(lr * step)
    abr = mag * jnp.cos(li * step)
    abi = mag * jnp.sin(li * step)
    nr, ni = abr - 1.0, abi
    den = lr * lr + li * li
    fr = (nr * lr + ni * li) / den
    fi = (ni * lr - nr * li) / den
    br, bi = br_ref[...], bi_ref[...]
    bbr = fr * br - fi * bi
    bbi = fr * bi + fi * br
    b_scr[...] = jnp.zeros_like(b_scr)
    c_scr[...] = jnp.zeros_like(c_scr)
    gps = S5_GROUPS // S5_SLABS
    ns = S5_SLAB_STATE
    for g in range(S5_GROUPS):
        s, gl = divmod(g, gps)
        ch = slice(gl * S5_CH, (gl + 1) * S5_CH)
        st = slice(gl * S5_STATE, (gl + 1) * S5_STATE)
        st_im = slice(ns + gl * S5_STATE, ns + (gl + 1) * S5_STATE)
        ar_ref[s, :, st] = jnp.broadcast_to(abr[g], (SUBLANES, S5_STATE))
        ai_ref[s, :, st] = jnp.broadcast_to(abi[g], (SUBLANES, S5_STATE))
        b_scr[s, ch, st] = bbr[g]
        b_scr[s, ch, st_im] = bbi[g]
        c_scr[s, st, ch] = cr_ref[g].T
        c_scr[s, st_im, ch] = -ci_ref[g].T
    bblk_ref[...] = b_scr[...].astype(bf16)
    cblk_ref[...] = c_scr[...].astype(bf16)


def _s5_params(a_re, a_im, log_step, b_re_t, b_im_t, c_re, c_im):
    g, p = a_re.shape
    ns = S5_SLAB_STATE
    return pl.pallas_call(
        _s5_param_kernel,
        out_shape=[jax.ShapeDtypeStruct((S5_SLABS, SUBLANES, ns), f32),
                   jax.ShapeDtypeStruct((S5_SLABS, SUBLANES, ns), f32),
                   jax.ShapeDtypeStruct((S5_SLABS, LANES, 2 * ns), bf16),
                   jax.ShapeDtypeStruct((S5_SLABS, 2 * ns, LANES), bf16)],
        scratch_shapes=[pltpu.VMEM((S5_SLABS, LANES, 2 * ns), f32),
                        pltpu.VMEM((S5_SLABS, 2 * ns, LANES), f32)],
        name="s5_params",
    )(a_re.reshape(g, 1, p), a_im.reshape(g, 1, p), log_step.reshape(g, 1, 1), b_re_t, b_im_t, c_re, c_im)


def _gelu(x):
    return 0.5 * x * (1.0 + lax.erf(x * (2.0 ** -0.5)))


def _s5_slabs_interleaved(ut_ref, hbuf_ref, hst_ref, g_ref, ar_ref, ai_ref, bblk_ref, cblk_ref, dsk_ref, tl,
                          row_block=256, n_stages=8):
    ns = S5_SLAB_STATE
    nb = SUBLANES
    rows = nb * tl

    def lanes(s):
        return slice(s * LANES, (s + 1) * LANES)

    def bu_stage(slabs):
        for s in slabs:
            for r in range(0, rows, row_block):
                us = ut_ref[r:r + row_block, lanes(s)]
                hbuf_ref[s, nb + r:nb + r + row_block, :] = jnp.dot(us.astype(bf16), bblk_ref[s],
                                                                    preferred_element_type=f32)
                yield

    def scan_stage(slabs):
        carry = [(hst_ref[s, :, :ns], hst_ref[s, :, ns:]) for s in slabs]
        coef = [(ar_ref[s], ai_ref[s]) for s in slabs]
        for t in range(tl):
            r = nb + nb * t
            for i, s in enumerate(slabs):
                (pr, pi), (ar, ai) = carry[i], coef[i]
                nr = ar * pr - ai * pi + hbuf_ref[s, r:r + nb, :ns]
                ni = ar * pi + ai * pr + hbuf_ref[s, r:r + nb, ns:]
                hbuf_ref[s, r:r + nb, :ns] = nr
                hbuf_ref[s, r:r + nb, ns:] = ni
                carry[i] = (nr, ni)
            if (t + 1) % (tl // n_stages) == 0:
                yield
        for i, s in enumerate(slabs):
            hst_ref[s, :, :ns] = carry[i][0]
            hst_ref[s, :, ns:] = carry[i][1]

    def y_stage(slabs):
        for s in slabs:
            for r in range(0, rows, row_block):
                h16 = hbuf_ref[s, nb + r:nb + r + row_block, :].astype(bf16)
                ys = jnp.dot(h16, cblk_ref[s], preferred_element_type=f32)
                ys = ys + dsk_ref[:, lanes(s)] * ut_ref[r:r + row_block, lanes(s)]
                g_ref[r:r + row_block, lanes(s)] = _gelu(ys)
                yield

    half = S5_SLABS // 2
    first, second = tuple(range(half)), tuple(range(half, S5_SLABS))
    _interleave(bu_stage(first))
    _interleave(scan_stage(first), bu_stage(second))
    _interleave(scan_stage(second), y_stage(first))
    _interleave(y_stage(second))


def _s5_scan_vmem(hbuf_ref, hst_ref, ar_ref, ai_ref, s, nb, rows):
    ns = S5_SLAB_STATE
    hbuf_ref[s, 0:nb, :] = hst_ref[s]
    ar, ai = ar_ref[s], ai_ref[s]

    def body(i, carry):
        j = pl.multiple_of(nb + i * SUBLANES, SUBLANES)
        prev = hbuf_ref[s, pl.ds(j - nb, SUBLANES), :]
        cur = hbuf_ref[s, pl.ds(j, SUBLANES), :]
        pr, pi = prev[:, :ns], prev[:, ns:]
        hbuf_ref[s, pl.ds(j, SUBLANES), :ns] = ar * pr - ai * pi + cur[:, :ns]
        hbuf_ref[s, pl.ds(j, SUBLANES), ns:] = ar * pi + ai * pr + cur[:, ns:]
        return carry

    lax.fori_loop(0, rows // SUBLANES, body, 0, unroll=min(8, nb // SUBLANES))
    hst_ref[s] = hbuf_ref[s, rows:rows + nb, :]


S5_N_IN = 10


def _s5_kernel(*refs, nb, tl, batch_major, n_cast):
    (u_ref, re0_ref, im0_ref, ar_ref, ai_ref, bblk_ref, cblk_ref, dsk_ref, wglu_ref,
     bglu_ref) = refs[:S5_N_IN]
    cast_in = refs[S5_N_IN:S5_N_IN + n_cast]
    y_ref, ren_ref, imn_ref = refs[S5_N_IN + n_cast:S5_N_IN + n_cast + 3]
    cast_out = refs[S5_N_IN + n_cast + 3:S5_N_IN + 2 * n_cast + 3]
    ut_ref, hbuf_ref, hst_ref, g_ref = refs[S5_N_IN + 2 * n_cast + 3:]
    for w_ref, w16_ref in zip(cast_in, cast_out):
        w16_ref[...] = w_ref[...].astype(bf16)
    step = pl.program_id(0)
    rows = nb * tl
    ns = S5_SLAB_STATE

    @pl.when(step == 0)
    def _():
        for s in range(S5_SLABS):
            hst_ref[s, :, :ns] = re0_ref[:, s * ns:(s + 1) * ns]
            hst_ref[s, :, ns:] = im0_ref[:, s * ns:(s + 1) * ns]

    if batch_major:
        ut_ref[...] = jnp.swapaxes(u_ref[...], 0, 1).reshape(rows, D_S5)
    else:
        ut_ref[...] = u_ref[...].reshape(rows, D_S5)

    if nb == SUBLANES:
        _s5_slabs_interleaved(ut_ref, hbuf_ref, hst_ref, g_ref, ar_ref, ai_ref, bblk_ref, cblk_ref, dsk_ref, tl)
    else:
        for s in range(S5_SLABS):
            sl = slice(s * LANES, (s + 1) * LANES)
            hbuf_ref[s, nb:nb + rows, :] = jnp.dot(ut_ref[:, sl].astype(bf16), bblk_ref[s],
                                                   preferred_element_type=f32)
            _s5_scan_vmem(hbuf_ref, hst_ref, ar_ref, ai_ref, s, nb, rows)
            ys = jnp.dot(hbuf_ref[s, nb:nb + rows, :].astype(bf16), cblk_ref[s], preferred_element_type=f32)
            g_ref[:, sl] = _gelu(ys + dsk_ref[:, sl] * ut_ref[:, sl])

    g = g_ref[...]
    gate = jnp.dot(g.astype(bf16), wglu_ref[...], preferred_element_type=f32) + bglu_ref[...]
    out = g * jax.nn.sigmoid(gate)
    if batch_major:
        y_ref[...] = jnp.swapaxes(out.reshape(tl, nb, D_S5), 0, 1)
    else:
        y_ref[...] = out.reshape(tl, nb, D_S5)

    @pl.when(step == pl.num_programs(0) - 1)
    def _():
        for s in range(S5_SLABS):
            ren_ref[:, s * ns:(s + 1) * ns] = hst_ref[s, :, :ns]
            imn_ref[:, s * ns:(s + 1) * ns] = hst_ref[s, :, ns:]


def _s5(u, re0, im0, ar, ai, bblk, cblk, dsk, wglu, bglu, *, tl, batch_major, cast=()):
    if batch_major:
        nb, L, _ = u.shape
        u_spec = pl.BlockSpec((nb, tl, D_S5), lambda i: (0, i, 0))
    else:
        L, nb, _ = u.shape
        u_spec = pl.BlockSpec((tl, nb, D_S5), lambda i: (i, 0, 0))
    rows = nb * tl
    steps = L // tl
    nstate = S5_GROUPS * S5_STATE
    st_spec = _const_spec((nb, nstate))
    cast_specs = [pl.BlockSpec((w.shape[0] // steps, w.shape[1]), lambda i: (i, 0)) for w in cast]
    assert all(w.shape[0] % (steps * 2 * SUBLANES) == 0 for w in cast)
    return pl.pallas_call(
        functools.partial(_s5_kernel, nb=nb, tl=tl, batch_major=batch_major, n_cast=len(cast)),
        grid=(steps,),
        in_specs=[u_spec, st_spec, st_spec,
                  _const_spec((S5_SLABS, SUBLANES, S5_SLAB_STATE)),
                  _const_spec((S5_SLABS, SUBLANES, S5_SLAB_STATE)),
                  _const_spec((S5_SLABS, LANES, 2 * S5_SLAB_STATE)),
                  _const_spec((S5_SLABS, 2 * S5_SLAB_STATE, LANES)),
                  _const_spec((1, D_S5)), _const_spec((D_S5, D_S5)), _const_spec((1, D_S5))] + cast_specs,
        out_specs=[u_spec, st_spec, st_spec] + cast_specs,
        out_shape=[jax.ShapeDtypeStruct(u.shape, f32),
                   jax.ShapeDtypeStruct((nb, nstate), f32),
                   jax.ShapeDtypeStruct((nb, nstate), f32)]
                  + [jax.ShapeDtypeStruct(w.shape, bf16) for w in cast],
        scratch_shapes=[pltpu.VMEM((rows, D_S5), f32),
                        pltpu.VMEM((S5_SLABS, nb + rows, 2 * S5_SLAB_STATE), f32),
                        pltpu.VMEM((S5_SLABS, nb, 2 * S5_SLAB_STATE), f32),
                        pltpu.VMEM((rows, D_S5), f32)],
        compiler_params=_params("arbitrary"),
        name="s5_mixer",
    )(u, re0, im0, ar, ai, bblk, cblk, dsk, wglu, bglu, *cast)


def _ffn_stages(x, ys, y5, mods, n2g, nfg, wo_ref, wg_ref, wu_ref, wd_ref, write_out, ff_chunk):
    g1, sh2, sc2, g2, shf, scf = mods
    att = jnp.dot(ys.astype(bf16), wo_ref[:D_SSD, :], preferred_element_type=f32)
    att = att + jnp.dot(y5.astype(bf16), wo_ref[D_SSD:, :], preferred_element_type=f32)
    yield
    x1 = x + _by_row(lambda v, s: v * s, att, g1)
    v = _rms_mod(x1, n2g, sc2, sh2).astype(bf16)
    ff = None
    for o in range(0, D_FF, ff_chunk):
        gate = jnp.dot(v, wg_ref[:, o:o + ff_chunk], preferred_element_type=f32)
        up = jnp.dot(v, wu_ref[:, o:o + ff_chunk], preferred_element_type=f32)
        hid = (_silu(gate) * up).astype(bf16)
        part = jnp.dot(hid, wd_ref[o:o + ff_chunk, :], preferred_element_type=f32)
        ff = part if ff is None else ff + part
        yield
    x2 = x1 + _by_row(lambda v, s: v * s, ff, g2)
    write_out(_rms_mod(x2, nfg, scf, shf))


def _ffn_kernel(x_ref, ys_ref, y5_ref, mod_ref, modf_ref, n2g_ref, nfg_ref, wo_ref, wg_ref, wu_ref, wd_ref,
                o_ref, *, seq_major, ff_chunk, sub):
    mods = (_mod_row(mod_ref, 2, seq_major), _mod_row(mod_ref, 3, seq_major), _mod_row(mod_ref, 4, seq_major),
            _mod_row(mod_ref, 5, seq_major), _mod_row(modf_ref, 0, seq_major), _mod_row(modf_ref, 1, seq_major))
    x = _rows_in(x_ref, seq_major)
    rows = x.shape[0]
    tiles = {}

    def stages(r0):
        def write_out(y):
            if seq_major:
                tiles[r0] = y
            else:
                o_ref[0, r0:r0 + sub, :] = y
        return _ffn_stages(x[r0:r0 + sub], ys_ref[0, r0:r0 + sub, :], y5_ref[0, r0:r0 + sub, :], mods,
                           n2g_ref[...], nfg_ref[...], wo_ref, wg_ref, wu_ref, wd_ref, write_out, ff_chunk)

    _interleave(*[stages(r0) for r0 in range(0, rows, sub)])
    if seq_major:
        s, t, d = o_ref.shape
        y = jnp.concatenate([tiles[r0] for r0 in range(0, rows, sub)], axis=0)
        o_ref[...] = jnp.swapaxes(y.reshape(t, s, d), 0, 1)


def _ffn(x, ys, y5, mod, modf, n2g, nfg, wo, wg, wu, wd, *, tm, sub, seq_major, n_seq, first_seq, ff_chunk=256):
    if seq_major:
        x_spec = _const_spec(x.shape)
        nb, rows, d = 1, x.shape[0] * x.shape[1], x.shape[2]
        assert tm == rows
    else:
        nb, rows, d = x.shape
        x_spec = pl.BlockSpec((1, tm, d), lambda i, j: (i, j, 0))
    nt = rows // tm
    mod_spec = _mod_spec(mod, n_seq, first_seq)
    modf_spec = _mod_spec(modf, n_seq, first_seq)
    blk = lambda wd_: pl.BlockSpec((1, tm, wd_), lambda i, j: (i, j, 0))
    single = dict(pipeline_mode=pl.Buffered(1))
    wspec = lambda shape: pl.BlockSpec(shape, lambda i, j: (0, 0), **single)
    return pl.pallas_call(
        functools.partial(_ffn_kernel, seq_major=seq_major, ff_chunk=ff_chunk, sub=sub),
        grid=(nb, nt),
        in_specs=[x_spec, blk(D_SSD), blk(D_S5), mod_spec, modf_spec,
                  _const_spec((1, d)), _const_spec((1, d)),
                  wspec((d, d)), wspec((d, D_FF)), wspec((d, D_FF)), wspec((D_FF, d))],
        out_specs=x_spec,
        out_shape=jax.ShapeDtypeStruct(x.shape, f32),
        compiler_params=_params("parallel", "parallel"),
        name="out_ffn",
    )(x, ys, y5, mod, modf, n2g, nfg, wo, wg, wu, wd)


def kernel(x_prompt, x_sample, c_prompt, c_sample, state_ssd, state_conv, state_s5_re, state_s5_im, w_ada, b_ada, norm1_g, w_in, conv_w, conv_b, ssd_dt_bias, ssd_A_log, ssd_D, ssd_norm_g, s5_A_re, s5_A_im, s5_log_step, s5_B_re, s5_B_im, s5_C_re, s5_C_im, s5_D, w_glu, b_glu, w_out, norm2_g, w_ffn_gate, w_ffn_up, w_ffn_down, w_ada_f, b_ada_f, normf_g):
    assert w_ada.shape[0] == 1, "single-layer stack"
    bp, seq, d = x_prompt.shape
    bs, dseq, _ = x_sample.shape

    c_all = jnp.concatenate([c_sample, c_prompt], axis=0)
    mod = _ada_mod(c_all, w_ada[0], b_ada[0])
    modf = _ada_mod(c_all, w_ada_f, b_ada_f)
    seqs_s = dict(n_seq=bs, first_seq=0)
    seqs_p = dict(n_seq=bp, first_seq=bs)

    w_in_p = w_in[0].T
    pad_h = lambda v: jnp.concatenate([v, jnp.zeros((DT_PAD - HEADS,), f32)]).reshape(1, DT_PAD)
    dtb = pad_h(ssd_dt_bias[0])
    alog = pad_h(ssd_A_log[0])
    dexp = jnp.repeat(ssd_D[0], HEAD_DIM).reshape(1, D_SSD)
    ng = ssd_norm_g[0].reshape(1, D_SSD)
    cw, cb = conv_w[0], conv_b[0].reshape(1, CONV_DIM)

    ar, ai, bblk, cblk = _s5_params(s5_A_re[0], s5_A_im[0], s5_log_step[0],
                                    s5_B_re[0].transpose(0, 2, 1), s5_B_im[0].transpose(0, 2, 1),
                                    s5_C_re[0], s5_C_im[0])
    dsk = s5_D[0].reshape(1, D_S5)
    wglu = w_glu[0].astype(bf16)
    bglu = b_glu[0].reshape(1, D_S5)
    n1g, n2g, nfg = norm1_g[0].reshape(1, d), norm2_g[0].reshape(1, d), normf_g.reshape(1, d)
    nstate = S5_GROUPS * S5_STATE

    z, act, u5, dt, conv_p = _inproj(x_prompt, mod, n1g, w_in_p, (cw, cb), tm=1024, sub=512, seq_major=False,
                                     **seqs_p)
    y_ssd, ssd_p = _ssd_prompt(act, z, dt, dtb, alog, dexp, ng)
    zeros_st = jnp.zeros((bp, nstate), f32)
    y_s5, re_p, im_p, wo, wg, wu, wd = _s5(
        u5, zeros_st, zeros_st, ar, ai, bblk, cblk, dsk, wglu, bglu, tl=128, batch_major=True,
        cast=(w_out[0], w_ffn_gate[0], w_ffn_up[0], w_ffn_down[0]))
    y_prompt = _ffn(x_prompt, y_ssd, y_s5, mod, modf, n2g, nfg, wo, wg, wu, wd, tm=1024, sub=512,
                    seq_major=False, **seqs_p)

    rows_s = dseq * bs
    steps = lambda a: a.reshape(dseq, bs, a.shape[-1])
    flat = lambda a: a.reshape(1, rows_s, a.shape[-1])
    z, xbc, u5, dt = _inproj(x_sample, mod, n1g, w_in_p, tm=rows_s, sub=rows_s // 2, seq_major=True, **seqs_s)
    y_ssd, ssd_s, conv_s = _ssd_sample(steps(xbc), steps(z), steps(dt), state_conv[0].transpose(1, 0, 2),
                                       state_ssd[0], cw, cb, dtb, alog, dexp, ng)
    y_s5, re_s, im_s = _s5(steps(u5), state_s5_re[0].reshape(bs, nstate), state_s5_im[0].reshape(bs, nstate),
                           ar, ai, bblk, cblk, dsk, wglu, bglu, tl=dseq, batch_major=False)
    y_sample = _ffn(x_sample, flat(y_ssd), flat(y_s5), mod, modf, n2g, nfg, wo, wg, wu, wd,
                    tm=rows_s, sub=rows_s, seq_major=True, **seqs_s)

    g5 = (S5_GROUPS, S5_STATE)
    return (y_prompt, y_sample,
            ssd_p[None], ssd_s.reshape((1,) + state_ssd.shape[1:]),
            conv_p[None], conv_s.transpose(1, 0, 2)[None],
            re_p.reshape((1, bp) + g5), re_s.reshape((1, bs) + g5),
            im_p.reshape((1, bp) + g5), im_s.reshape((1, bs) + g5))
```

```python
import functools

import jax
import jax.numpy as jnp
from jax import lax
from jax.experimental import pallas as pl
from jax.experimental.pallas import tpu as pltpu

f32 = jnp.float32
bf16 = jnp.bfloat16

D_MODEL = 1024
D_SSD = 512
HEAD_DIM = 64
HEADS = 8
GROUPS = 2
HEADS_PER_GROUP = HEADS // GROUPS
STATE = 128
CONV_W = 4
CONV_DIM = D_SSD + 2 * GROUPS * STATE
D_S5 = 512
S5_CH = 16
S5_GROUPS = 32
S5_STATE = 64
D_FF = 2816
N_ADA = 6
EPS = 1e-6

LANES = 128
SUBLANES = 8
SSD_CHUNK = 128
S5_SLABS = D_S5 // LANES
S5_SLAB_STATE = (S5_GROUPS // S5_SLABS) * S5_STATE
DT_PAD = LANES
VMEM_LIMIT = 56 * 1024 * 1024

NT_DIMS = (((1,), (1,)), ((), ()))
TN_DIMS = (((0,), (0,)), ((), ()))


def _silu(x):
    return x * jax.nn.sigmoid(x)


def _interleave(*gens):
    live = list(gens)
    while live:
        for g in list(live):
            try:
                next(g)
            except StopIteration:
                live.remove(g)


def _by_row(fn, v, *ms):
    r, d = v.shape
    m_rows = ms[0].shape[0]
    if m_rows in (1, r):
        return fn(v, *ms)
    out = fn(v.reshape(r // m_rows, m_rows, d), *[m[None] for m in ms])
    return out.reshape(r, d)


def _rms_mod(x, g, sc, sh):
    y = x * lax.rsqrt(jnp.mean(x * x, axis=-1, keepdims=True) + EPS)
    return _by_row(lambda v, s, t: v * (1.0 + s) + t, y * g, sc, sh)


def _mod_row(mod_ref, i, per_row):
    return mod_ref[i] if per_row else mod_ref[i, pl.ds(pl.program_id(0), 1), :]


def _mod_spec(mod, n_seq, first_seq):
    assert first_seq % n_seq == 0
    return pl.BlockSpec((mod.shape[0], n_seq, mod.shape[2]), lambda *_: (0, first_seq // n_seq, 0))


def _const_spec(shape):
    nd = len(shape)
    return pl.BlockSpec(shape, lambda *_: (0,) * nd)


def _params(*sem):
    return pltpu.CompilerParams(dimension_semantics=sem, vmem_limit_bytes=VMEM_LIMIT)


def _ada_kernel(c_ref, w_ref, b_ref, o_ref):
    s = _silu(c_ref[...]).astype(bf16)
    o_ref[0] = jnp.dot(s, w_ref[...].astype(bf16), preferred_element_type=f32) + b_ref[...]


def _ada_mod(c, w, b):
    m, k = c.shape
    n = w.shape[1] // k
    return pl.pallas_call(
        _ada_kernel,
        grid=(n,),
        in_specs=[pl.BlockSpec((m, k), lambda j: (0, 0)),
                  pl.BlockSpec((k, k), lambda j: (0, j)),
                  pl.BlockSpec((1, k), lambda j: (0, j))],
        out_specs=pl.BlockSpec((1, m, k), lambda j: (j, 0, 0)),
        out_shape=jax.ShapeDtypeStruct((n, m, k), f32),
        compiler_params=_params("parallel"),
        name="ada_mod",
    )(c, w, b.reshape(1, n * k))


def _rows_in(x_ref, seq_major):
    if not seq_major:
        return x_ref[0]
    s, t, d = x_ref.shape
    return jnp.swapaxes(x_ref[...], 0, 1).reshape(s * t, d)


def _conv_silu(ext, cw_ref, cb_ref):
    w = [cw_ref[k:k + 1, :] for k in range(CONV_W)]
    back1 = pltpu.roll(ext, 1, 0)
    older = pltpu.roll(ext * w[1] + back1 * w[0], 2, 0)
    conv = cb_ref[...] + ext[SUBLANES:, :] * w[3] + back1[SUBLANES:, :] * w[2] + older[SUBLANES:, :]
    return _silu(conv)


def _inproj_stages(x, sh, sc, g, outs, after=()):
    u = _rms_mod(x, g, sc, sh).astype(bf16)
    yield
    for write, w16 in outs:
        write(lax.dot_general(u, w16, NT_DIMS, preferred_element_type=f32))
        yield
    for stage in after:
        stage()
        yield


def _inproj_kernel(x_ref, mod_ref, g_ref, w_ref, *rest, seq_major, sub, conv):
    if conv:
        cw_ref, cb_ref, z_ref, act_ref, u5_ref, dt_ref, cn_ref, ext_ref = rest
        pl.when(pl.program_id(1) == 0)(lambda: ext_ref.__setitem__(
            (slice(0, SUBLANES), slice(None)), jnp.zeros((SUBLANES, CONV_DIM), f32)))
    else:
        z_ref, xbc_ref, u5_ref, dt_ref = rest
    sh = _mod_row(mod_ref, 0, seq_major)
    sc = _mod_row(mod_ref, 1, seq_major)
    x = _rows_in(x_ref, seq_major)
    rows = x.shape[0]
    T = SSD_CHUNK
    o_dt = D_SSD + CONV_DIM
    w16 = lambda o, width: w_ref[o:o + width, :].astype(bf16)
    w_z, w_xbc, w_u5, w_dt = w16(0, D_SSD), w16(D_SSD, CONV_DIM), w16(o_dt + HEADS, D_S5), w16(o_dt, DT_PAD)

    def stages(r0):
        def to(ref):
            def write(v):
                ref[0, r0:r0 + sub, :] = v
            return write

        def to_ext(v):
            ext_ref[SUBLANES + r0:SUBLANES + r0 + sub, :] = v

        def conv_chunk(c0):
            def stage():
                act_ref[0, c0:c0 + T, :] = _conv_silu(ext_ref[c0:c0 + SUBLANES + T, :], cw_ref, cb_ref)
            return stage

        outs = [(to(z_ref), w_z), (to_ext if conv else to(xbc_ref), w_xbc), (to(u5_ref), w_u5), (to(dt_ref), w_dt)]
        after = [conv_chunk(c0) for c0 in range(r0, r0 + sub, T)] if conv else ()
        return _inproj_stages(x[r0:r0 + sub], sh, sc, g_ref[...], outs, after)

    _interleave(*[stages(r0) for r0 in range(0, rows, sub)])
    if conv:
        cn_ref[0] = ext_ref[SUBLANES + rows - (CONV_W - 1):SUBLANES + rows, :]
        ext_ref[0:SUBLANES, :] = ext_ref[rows:rows + SUBLANES, :]


def _inproj(x, mod, g, w, conv_wb=None, *, tm, sub, seq_major, n_seq, first_seq):
    if seq_major:
        x_spec = _const_spec(x.shape)
        nb, rows, d = 1, x.shape[0] * x.shape[1], x.shape[2]
        assert tm == rows
    else:
        nb, rows, d = x.shape
        x_spec = pl.BlockSpec((1, tm, d), lambda i, j: (i, j, 0))
    nt = rows // tm
    widths = (D_SSD, CONV_DIM, D_S5, DT_PAD)
    conv = conv_wb is not None
    in_specs = [x_spec, _mod_spec(mod, n_seq, first_seq), _const_spec((1, d)),
                pl.BlockSpec(w.shape, lambda i, j: (0, 0), pipeline_mode=pl.Buffered(1))]
    out_specs = [pl.BlockSpec((1, tm, wd), lambda i, j: (i, j, 0)) for wd in widths]
    out_shape = [jax.ShapeDtypeStruct((nb, rows, wd), f32) for wd in widths]
    scratch = []
    if conv:
        in_specs += [_const_spec((CONV_W, CONV_DIM)), _const_spec((1, CONV_DIM))]
        out_specs.append(pl.BlockSpec((1, CONV_W - 1, CONV_DIM), lambda i, j: (i, 0, 0)))
        out_shape.append(jax.ShapeDtypeStruct((nb, CONV_W - 1, CONV_DIM), f32))
        scratch.append(pltpu.VMEM((SUBLANES + tm, CONV_DIM), f32))
    return pl.pallas_call(
        functools.partial(_inproj_kernel, seq_major=seq_major, sub=sub, conv=conv),
        grid=(nb, nt),
        in_specs=in_specs, out_specs=out_specs, out_shape=out_shape, scratch_shapes=scratch,
        compiler_params=_params("parallel", "arbitrary" if conv else "parallel"),
        name="in_proj",
    )(x, mod, g, w, *(conv_wb or ()))


def _split3(x):
    hi = x.astype(bf16)
    r1 = x - hi.astype(f32)
    mid = r1.astype(bf16)
    lo = (r1 - mid.astype(f32)).astype(bf16)
    return hi, mid, lo


def _dot_sel_lhs(sel16, x):
    return sum(jnp.dot(sel16, p, preferred_element_type=f32) for p in _split3(x))


def _dot_sel_rhs(x, sel16):
    hi, mid, _ = _split3(x)
    return jnp.dot(hi, sel16, preferred_element_type=f32) + jnp.dot(mid, sel16, preferred_element_type=f32)


def _gated_group_norm(y, z, ng):
    y = y * _silu(z)
    gw = D_SSD // GROUPS
    parts = []
    for g in range(GROUPS):
        yg = y[:, g * gw:(g + 1) * gw]
        parts.append(yg * lax.rsqrt(jnp.mean(yg * yg, axis=-1, keepdims=True) + EPS))
    return jnp.concatenate(parts, axis=-1) * ng


def _ssd_prompt_kernel(act_ref, z_ref, dt_ref, dtb_ref, alog_ref, dexp_ref, ng_ref, y_ref, st_ref, h_ref, *, cps):
    T = SSD_CHUNK
    gw = HEADS_PER_GROUP * HEAD_DIM

    @pl.when(pl.program_id(1) == 0)
    def _():
        h_ref[...] = jnp.zeros_like(h_ref)

    row = lax.broadcasted_iota(jnp.int32, (T, T), 0)
    col = lax.broadcasted_iota(jnp.int32, (T, T), 1)
    tri = row >= col
    tri16 = jnp.where(tri, 1.0, 0.0).astype(bf16)
    low_half = lax.broadcasted_iota(jnp.int32, (T, LANES), 1) < HEAD_DIM
    a_neg = -jnp.exp(alog_ref[...])

    lane_head = lax.broadcasted_iota(jnp.int32, (DT_PAD, D_SSD), 1) // HEAD_DIM
    sel16 = jnp.where(lax.broadcasted_iota(jnp.int32, (DT_PAD, D_SSD), 0) == lane_head, 1.0, 0.0).astype(bf16)

    def spread(q):
        return _dot_sel_rhs(q, sel16)

    def state_free(ci):
        r0 = ci * T
        act = act_ref[0, r0:r0 + T, :]
        xs = act[:, :D_SSD]
        bm = act[:, D_SSD:D_SSD + GROUPS * STATE]
        cm = act[:, D_SSD + GROUPS * STATE:]
        dtv = jax.nn.softplus(dt_ref[0, r0:r0 + T, :] + dtb_ref[...])
        a_cs = _dot_sel_lhs(tri16, dtv * a_neg)
        bg16s = [bm[:, g * STATE:(g + 1) * STATE].astype(bf16) for g in range(GROUPS)]
        cg16s = [cm[:, g * STATE:(g + 1) * STATE].astype(bf16) for g in range(GROUPS)]
        gmats = [lax.dot_general(cg16s[g], bg16s[g], NT_DIMS, preferred_element_type=f32)
                 for g in range(GROUPS)]
        yield
        a_last = a_cs[T - 1:T, :]
        a_cs_t = a_cs.T
        rep = spread(jnp.concatenate([dtv, jnp.exp(a_last - a_cs), jnp.exp(a_cs)], axis=0))
        yield
        x = xs * rep[:T]
        x16 = x.astype(bf16)
        xd16 = (x * rep[T:2 * T]).astype(bf16)
        e_cs = rep[2 * T:]
        yield
        y_part = []
        for g in range(GROUPS):
            for pr in range(HEADS_PER_GROUP // 2):
                s_pair = []
                for q in range(2):
                    h = g * HEADS_PER_GROUP + 2 * pr + q
                    seg = a_cs[:, h:h + 1] - a_cs_t[h:h + 1, :]
                    lmat = jnp.exp(jnp.where(tri, seg, -jnp.inf))
                    s_pair.append((gmats[g] * lmat).astype(bf16))
                lanes = slice(g * gw + pr * LANES, g * gw + (pr + 1) * LANES)
                both = jnp.dot(jnp.concatenate(s_pair, axis=0), x16[:, lanes], preferred_element_type=f32)
                yield
                y_part.append(jnp.where(low_half, both[:T], both[T:]))
        y_free = jnp.concatenate(y_part, axis=-1) + xs * dexp_ref[...]
        return y_free, e_cs, jnp.exp(a_last), xd16, bg16s, cg16s

    def state_step(ci, y_free, e_cs, e_last, xd16, bg16s, cg16s):
        r0 = ci * T
        y_offs = []
        for g in range(GROUPS):
            hp = h_ref[g]
            y_offs.append(lax.dot_general(cg16s[g], hp.astype(bf16), NT_DIMS, preferred_element_type=f32))
            upd = lax.dot_general(xd16[:, g * gw:(g + 1) * gw], bg16s[g], TN_DIMS, preferred_element_type=f32)
            for hh in range(HEADS_PER_GROUP):
                h = g * HEADS_PER_GROUP + hh
                rows = slice(hh * HEAD_DIM, (hh + 1) * HEAD_DIM)
                h_ref[g, rows, :] = e_last[:, h:h + 1] * hp[rows] + upd[rows]
        y = y_free + jnp.concatenate(y_offs, axis=-1) * e_cs
        y_ref[0, r0:r0 + T, :] = _gated_group_norm(y, z_ref[0, r0:r0 + T, :], ng_ref[...])

    gens = [state_free(ci) for ci in range(cps)]
    free = [None] * cps
    while any(f is None for f in free):
        for ci in range(cps):
            if free[ci] is None:
                try:
                    next(gens[ci])
                except StopIteration as done:
                    free[ci] = done.value
    for ci in range(cps):
        state_step(ci, *free[ci])

    @pl.when(pl.program_id(1) == pl.num_programs(1) - 1)
    def _():
        st_ref[0] = h_ref[...].reshape(HEADS, HEAD_DIM, STATE)


def _ssd_prompt(act, z, dt, dtb, alog, dexp, ng, *, cps=8):
    nb, L, _ = act.shape
    rows = cps * SSD_CHUNK
    blk = lambda wd: pl.BlockSpec((1, rows, wd), lambda b, c: (b, c, 0))
    return pl.pallas_call(
        functools.partial(_ssd_prompt_kernel, cps=cps),
        grid=(nb, L // rows),
        in_specs=[blk(CONV_DIM), blk(D_SSD), blk(DT_PAD), _const_spec((1, DT_PAD)), _const_spec((1, DT_PAD)),
                  _const_spec((1, D_SSD)), _const_spec((1, D_SSD))],
        out_specs=[blk(D_SSD), pl.BlockSpec((1, HEADS, HEAD_DIM, STATE), lambda b, c: (b, 0, 0, 0))],
        out_shape=[jax.ShapeDtypeStruct((nb, L, D_SSD), f32),
                   jax.ShapeDtypeStruct((nb, HEADS, HEAD_DIM, STATE), f32)],
        scratch_shapes=[pltpu.VMEM((GROUPS, HEADS_PER_GROUP * HEAD_DIM, STATE), f32)],
        compiler_params=_params("parallel", "arbitrary"),
        name="ssd_prompt",
    )(act, z, dt, dtb, alog, dexp, ng)


def _ssd_sample_kernel(xbc_ref, z_ref, dt_ref, cbuf_ref, st_ref, cw_ref, cb_ref, dtb_ref, alog_ref,
                       dexp_ref, ng_ref, y_ref, stn_ref, cn_ref, dtot_ref, *, L, nb):
    gw = HEADS_PER_GROUP * HEAD_DIM
    full = [cbuf_ref[j] for j in range(CONV_W - 1)] + [xbc_ref[t] for t in range(L)]
    for j in range(CONV_W - 1):
        cn_ref[j] = full[L + j]
    a_neg = -jnp.exp(alog_ref[...])
    xs, bm, cm, dtv, acs = [], [], [], [], []
    run = None
    for t in range(L):
        conv = cb_ref[...]
        for k in range(CONV_W):
            conv = conv + full[t + k] * cw_ref[k:k + 1, :]
        act = _silu(conv)
        xs.append(act[:, :D_SSD])
        bm.append(act[:, D_SSD:D_SSD + GROUPS * STATE])
        cm.append(act[:, D_SSD + GROUPS * STATE:])
        d = jax.nn.softplus(dt_ref[t] + dtb_ref[...])
        dtv.append(d)
        run = d * a_neg if run is None else run + d * a_neg
        acs.append(run)
    a_tot = acs[L - 1]
    dtot_ref[...] = jnp.exp(a_tot)

    lane_head = lax.broadcasted_iota(jnp.int32, (DT_PAD, D_SSD), 1) // HEAD_DIM
    sel16 = jnp.where(lax.broadcasted_iota(jnp.int32, (DT_PAD, D_SSD), 0) == lane_head, 1.0, 0.0).astype(bf16)
    pairs = [(t, s) for t in range(L) for s in range(t)]
    factors = (dtv + [jnp.exp(a_tot - acs[t]) for t in range(L)] + [jnp.exp(acs[t]) for t in range(L)]
               + [jnp.exp(acs[t] - acs[s]) for t, s in pairs])
    rep = _dot_sel_rhs(jnp.concatenate(factors, axis=0), sel16)
    piece = lambda i: rep[i * nb:(i + 1) * nb]
    x = [xs[t] * piece(t) for t in range(L)]
    xd_stack = jnp.concatenate([x[t] * piece(L + t) for t in range(L)], axis=0)
    e_cs = [piece(2 * L + t) for t in range(L)]
    decay = {ts: piece(3 * L + i) for i, ts in enumerate(pairs)}

    in_group0 = lax.broadcasted_iota(jnp.int32, (nb, D_SSD), 1) < gw
    y_intra = []
    for t in range(L):
        acc = None
        for s in range(t + 1):
            cb_dot = [jnp.sum(cm[t][:, g * STATE:(g + 1) * STATE] * bm[s][:, g * STATE:(g + 1) * STATE],
                              axis=-1, keepdims=True) for g in range(GROUPS)]
            w = jnp.where(in_group0, cb_dot[0], cb_dot[1])
            term = w * x[s] if s == t else w * decay[(t, s)] * x[s]
            acc = term if acc is None else acc + term
        y_intra.append(acc)

    c_stack = [jnp.concatenate([cm[t][:, g * STATE:(g + 1) * STATE] for t in range(L)], axis=0).astype(bf16)
               for g in range(GROUPS)]
    b_stack = [jnp.concatenate([bm[t][:, g * STATE:(g + 1) * STATE] for t in range(L)], axis=0).astype(bf16)
               for g in range(GROUPS)]
    seq_of_row = lax.broadcasted_iota(jnp.int32, (L * nb, gw), 0) & (nb - 1)

    def per_seq(b, yoff):
        mine = seq_of_row == b
        drow = dtot_ref[pl.ds(b, 1), :]
        out = []
        for g in range(GROUPS):
            h0 = st_ref[b, g]
            r = lax.dot_general(c_stack[g], h0.astype(bf16), NT_DIMS, preferred_element_type=f32)
            out.append(yoff[g] + jnp.where(mine, r, 0.0))
            xm = jnp.where(mine, xd_stack[:, g * gw:(g + 1) * gw], 0.0).astype(bf16)
            upd = lax.dot_general(xm, b_stack[g], TN_DIMS, preferred_element_type=f32)
            for hh in range(HEADS_PER_GROUP):
                h = g * HEADS_PER_GROUP + hh
                rows = slice(hh * HEAD_DIM, (hh + 1) * HEAD_DIM)
                stn_ref[b, g, rows, :] = drow[:, h:h + 1] * h0[rows] + upd[rows]
        return tuple(out)

    yoff = lax.fori_loop(0, nb, per_seq, tuple(jnp.zeros((L * nb, gw), f32) for _ in range(GROUPS)),
                         unroll=8)

    for t in range(L):
        y_off = jnp.concatenate([yoff[g][t * nb:(t + 1) * nb] for g in range(GROUPS)], axis=-1)
        y = y_intra[t] + y_off * e_cs[t] + xs[t] * dexp_ref[...]
        y_ref[t] = _gated_group_norm(y, z_ref[t], ng_ref[...])


def _ssd_sample(xbc, z, dt, cbuf, st, cw, cb, dtb, alog, dexp, ng, *, nb=16):
    L, B, _ = xbc.shape
    assert nb & (nb - 1) == 0 and B % nb == 0
    tblk = lambda n, wd: pl.BlockSpec((n, nb, wd), lambda i: (0, i, 0))
    gw = HEADS_PER_GROUP * HEAD_DIM
    st = st.reshape(B, GROUPS, gw, STATE)
    st_spec = pl.BlockSpec((nb, GROUPS, gw, STATE), lambda i: (i, 0, 0, 0))
    return pl.pallas_call(
        functools.partial(_ssd_sample_kernel, L=L, nb=nb),
        grid=(B // nb,),
        in_specs=[tblk(L, CONV_DIM), tblk(L, D_SSD), tblk(L, DT_PAD), tblk(CONV_W - 1, CONV_DIM), st_spec,
                  _const_spec((CONV_W, CONV_DIM)), _const_spec((1, CONV_DIM)),
                  _const_spec((1, DT_PAD)), _const_spec((1, DT_PAD)),
                  _const_spec((1, D_SSD)), _const_spec((1, D_SSD))],
        out_specs=[tblk(L, D_SSD), st_spec, tblk(CONV_W - 1, CONV_DIM)],
        out_shape=[jax.ShapeDtypeStruct((L, B, D_SSD), f32),
                   jax.ShapeDtypeStruct((B, GROUPS, gw, STATE), f32),
                   jax.ShapeDtypeStruct((CONV_W - 1, B, CONV_DIM), f32)],
        scratch_shapes=[pltpu.VMEM((nb, DT_PAD), f32)],
        compiler_params=_params("parallel"),
        name="ssd_sample",
    )(xbc, z, dt, cbuf, st, cw, cb, dtb, alog, dexp, ng)


def _s5_param_kernel(lr_ref, li_ref, ls_ref, br_ref, bi_ref, cr_ref, ci_ref,
                     ar_ref, ai_ref, bblk_ref, cblk_ref, b_scr, c_scr):
    lr, li = lr_ref[...], li_ref[...]
    step = jnp.exp(ls_ref[...])
    mag = jnp.exp(lr * step)
    abr = mag * jnp.cos(li * step)
    abi = mag * jnp.sin(li * step)
    nr, ni = abr - 1.0, abi
    den = lr * lr + li * li
    fr = (nr * lr + ni * li) / den
    fi = (ni * lr - nr * li) / den
    br, bi = br_ref[...], bi_ref[...]
    bbr = fr * br - fi * bi
    bbi = fr * bi + fi * br
    b_scr[...] = jnp.zeros_like(b_scr)
    c_scr[...] = jnp.zeros_like(c_scr)
    gps = S5_GROUPS // S5_SLABS
    ns = S5_SLAB_STATE
    for g in range(S5_GROUPS):
        s, gl = divmod(g, gps)
        ch = slice(gl * S5_CH, (gl + 1) * S5_CH)
        st = slice(gl * S5_STATE, (gl + 1) * S5_STATE)
        st_im = slice(ns + gl * S5_STATE, ns + (gl + 1) * S5_STATE)
        ar_ref[s, :, st] = jnp.broadcast_to(abr[g], (SUBLANES, S5_STATE))
        ai_ref[s, :, st] = jnp.broadcast_to(abi[g], (SUBLANES, S5_STATE))
        b_scr[s, ch, st] = bbr[g]
        b_scr[s, ch, st_im] = bbi[g]
        c_scr[s, st, ch] = cr_ref[g].T
        c_scr[s, st_im, ch] = -ci_ref[g].T
    bblk_ref[...] = b_scr[...].astype(bf16)
    cblk_ref[...] = c_scr[...].astype(bf16)


def _s5_params(a_re, a_im, log_step, b_re_t, b_im_t, c_re, c_im):
    g, p = a_re.shape
    ns = S5_SLAB_STATE
    return pl.pallas_call(
        _s5_param_kernel,
        out_shape=[jax.ShapeDtypeStruct((S5_SLABS, SUBLANES, ns), f32),
                   jax.ShapeDtypeStruct((S5_SLABS, SUBLANES, ns), f32),
                   jax.ShapeDtypeStruct((S5_SLABS, LANES, 2 * ns), bf16),
                   jax.ShapeDtypeStruct((S5_SLABS, 2 * ns, LANES), bf16)],
        scratch_shapes=[pltpu.VMEM((S5_SLABS, LANES, 2 * ns), f32),
                        pltpu.VMEM((S5_SLABS, 2 * ns, LANES), f32)],
        name="s5_params",
    )(a_re.reshape(g, 1, p), a_im.reshape(g, 1, p), log_step.reshape(g, 1, 1), b_re_t, b_im_t, c_re, c_im)


def _gelu(x):
    return 0.5 * x * (1.0 + lax.erf(x * (2.0 ** -0.5)))


def _s5_slabs_interleaved(ut_ref, hbuf_ref, hst_ref, g_ref, ar_ref, ai_ref, bblk_ref, cblk_ref, dsk_ref, tl,
                          row_block=256, n_stages=8):
    ns = S5_SLAB_STATE
    nb = SUBLANES
    rows = nb * tl

    def lanes(s):
        return slice(s * LANES, (s + 1) * LANES)

    def bu_stage(slabs):
        for s in slabs:
            for r in range(0, rows, row_block):
                us = ut_ref[s, r:r + row_block, :]
                hbuf_ref[s, nb + r:nb + r + row_block, :] = jnp.dot(us.astype(bf16), bblk_ref[s],
                                                                    preferred_element_type=f32)
                yield

    def scan_stage(slabs):
        carry = [(hst_ref[s, :, :ns], hst_ref[s, :, ns:]) for s in slabs]
        coef = [(ar_ref[s], ai_ref[s]) for s in slabs]
        for t in range(tl):
            r = nb + nb * t
            for i, s in enumerate(slabs):
                (pr, pi), (ar, ai) = carry[i], coef[i]
                nr = ar * pr - ai * pi + hbuf_ref[s, r:r + nb, :ns]
                ni = ar * pi + ai * pr + hbuf_ref[s, r:r + nb, ns:]
                hbuf_ref[s, r:r + nb, :ns] = nr
                hbuf_ref[s, r:r + nb, ns:] = ni
                carry[i] = (nr, ni)
            if (t + 1) % (tl // n_stages) == 0:
                yield
        for i, s in enumerate(slabs):
            hst_ref[s, :, :ns] = carry[i][0]
            hst_ref[s, :, ns:] = carry[i][1]

    def y_stage(slabs):
        for s in slabs:
            for r in range(0, rows, row_block):
                h16 = hbuf_ref[s, nb + r:nb + r + row_block, :].astype(bf16)
                ys = jnp.dot(h16, cblk_ref[s], preferred_element_type=f32)
                ys = ys + dsk_ref[:, lanes(s)] * ut_ref[s, r:r + row_block, :]
                g_ref[s, r:r + row_block, :] = _gelu(ys)
                yield

    half = S5_SLABS // 2
    first, second = tuple(range(half)), tuple(range(half, S5_SLABS))
    _interleave(bu_stage(first))
    _interleave(scan_stage(first), bu_stage(second))
    _interleave(scan_stage(second), y_stage(first))
    _interleave(y_stage(second))


def _s5_scan_vmem(hbuf_ref, hst_ref, ar_ref, ai_ref, s, nb, rows):
    ns = S5_SLAB_STATE
    hbuf_ref[s, 0:nb, :] = hst_ref[s]
    ar, ai = ar_ref[s], ai_ref[s]

    def body(i, carry):
        j = pl.multiple_of(nb + i * SUBLANES, SUBLANES)
        prev = hbuf_ref[s, pl.ds(j - nb, SUBLANES), :]
        cur = hbuf_ref[s, pl.ds(j, SUBLANES), :]
        pr, pi = prev[:, :ns], prev[:, ns:]
        hbuf_ref[s, pl.ds(j, SUBLANES), :ns] = ar * pr - ai * pi + cur[:, :ns]
        hbuf_ref[s, pl.ds(j, SUBLANES), ns:] = ar * pi + ai * pr + cur[:, ns:]
        return carry

    lax.fori_loop(0, rows // SUBLANES, body, 0, unroll=min(8, nb // SUBLANES))
    hst_ref[s] = hbuf_ref[s, rows:rows + nb, :]


S5_N_IN = 10


def _s5_kernel(*refs, nb, tl, batch_major, n_cast):
    (u_ref, re0_ref, im0_ref, ar_ref, ai_ref, bblk_ref, cblk_ref, dsk_ref, wglu_ref,
     bglu_ref) = refs[:S5_N_IN]
    cast_in = refs[S5_N_IN:S5_N_IN + n_cast]
    y_ref, ren_ref, imn_ref = refs[S5_N_IN + n_cast:S5_N_IN + n_cast + 3]
    cast_out = refs[S5_N_IN + n_cast + 3:S5_N_IN + 2 * n_cast + 3]
    ut_ref, hbuf_ref, hst_ref, g_ref = refs[S5_N_IN + 2 * n_cast + 3:]
    for w_ref, w16_ref in zip(cast_in, cast_out):
        w16_ref[...] = w_ref[...].astype(bf16)
    step = pl.program_id(0)
    rows = nb * tl
    ns = S5_SLAB_STATE

    @pl.when(step == 0)
    def _():
        for s in range(S5_SLABS):
            hst_ref[s, :, :ns] = re0_ref[:, s * ns:(s + 1) * ns]
            hst_ref[s, :, ns:] = im0_ref[:, s * ns:(s + 1) * ns]

    for s in range(S5_SLABS):
        sl = slice(s * LANES, (s + 1) * LANES)
        if batch_major:
            for b in range(nb):
                ut_ref[s, pl.ds(b, tl, stride=nb), :] = u_ref[b, :, sl]
        else:
            ut_ref[s] = u_ref[:, :, sl].reshape(rows, LANES)

    if nb == SUBLANES:
        _s5_slabs_interleaved(ut_ref, hbuf_ref, hst_ref, g_ref, ar_ref, ai_ref, bblk_ref, cblk_ref, dsk_ref, tl)
    else:
        for s in range(S5_SLABS):
            sl = slice(s * LANES, (s + 1) * LANES)
            hbuf_ref[s, nb:nb + rows, :] = jnp.dot(ut_ref[s].astype(bf16), bblk_ref[s],
                                                   preferred_element_type=f32)
            _s5_scan_vmem(hbuf_ref, hst_ref, ar_ref, ai_ref, s, nb, rows)
            ys = jnp.dot(hbuf_ref[s, nb:nb + rows, :].astype(bf16), cblk_ref[s], preferred_element_type=f32)
            g_ref[s] = _gelu(ys + dsk_ref[:, sl] * ut_ref[s])

    g = jnp.concatenate([g_ref[s] for s in range(S5_SLABS)], axis=-1)
    gate = jnp.dot(g.astype(bf16), wglu_ref[...], preferred_element_type=f32) + bglu_ref[...]
    out = g * jax.nn.sigmoid(gate)
    if batch_major:
        for s in range(S5_SLABS):
            sl = slice(s * LANES, (s + 1) * LANES)
            ut_ref[s] = out[:, sl]
            for b in range(nb):
                y_ref[b, :, sl] = ut_ref[s, pl.ds(b, tl, stride=nb), :]
    else:
        y_ref[...] = out.reshape(tl, nb, D_S5)

    @pl.when(step == pl.num_programs(0) - 1)
    def _():
        for s in range(S5_SLABS):
            ren_ref[:, s * ns:(s + 1) * ns] = hst_ref[s, :, :ns]
            imn_ref[:, s * ns:(s + 1) * ns] = hst_ref[s, :, ns:]


def _s5(u, re0, im0, ar, ai, bblk, cblk, dsk, wglu, bglu, *, tl, batch_major, cast=()):
    if batch_major:
        nb, L, _ = u.shape
        u_spec = pl.BlockSpec((nb, tl, D_S5), lambda i: (0, i, 0))
    else:
        L, nb, _ = u.shape
        u_spec = pl.BlockSpec((tl, nb, D_S5), lambda i: (i, 0, 0))
    rows = nb * tl
    steps = L // tl
    nstate = S5_GROUPS * S5_STATE
    st_spec = _const_spec((nb, nstate))
    cast_specs = [pl.BlockSpec((w.shape[0] // steps, w.shape[1]), lambda i: (i, 0)) for w in cast]
    assert all(w.shape[0] % (steps * 2 * SUBLANES) == 0 for w in cast)
    return pl.pallas_call(
        functools.partial(_s5_kernel, nb=nb, tl=tl, batch_major=batch_major, n_cast=len(cast)),
        grid=(steps,),
        in_specs=[u_spec, st_spec, st_spec,
                  _const_spec((S5_SLABS, SUBLANES, S5_SLAB_STATE)),
                  _const_spec((S5_SLABS, SUBLANES, S5_SLAB_STATE)),
                  _const_spec((S5_SLABS, LANES, 2 * S5_SLAB_STATE)),
                  _const_spec((S5_SLABS, 2 * S5_SLAB_STATE, LANES)),
                  _const_spec((1, D_S5)), _const_spec((D_S5, D_S5)), _const_spec((1, D_S5))] + cast_specs,
        out_specs=[u_spec, st_spec, st_spec] + cast_specs,
        out_shape=[jax.ShapeDtypeStruct(u.shape, f32),
                   jax.ShapeDtypeStruct((nb, nstate), f32),
                   jax.ShapeDtypeStruct((nb, nstate), f32)]
                  + [jax.ShapeDtypeStruct(w.shape, bf16) for w in cast],
        scratch_shapes=[pltpu.VMEM((S5_SLABS, rows, LANES), f32),
                        pltpu.VMEM((S5_SLABS, nb + rows, 2 * S5_SLAB_STATE), f32),
                        pltpu.VMEM((S5_SLABS, nb, 2 * S5_SLAB_STATE), f32),
                        pltpu.VMEM((S5_SLABS, rows, LANES), f32)],
        compiler_params=_params("arbitrary"),
        name="s5_mixer",
    )(u, re0, im0, ar, ai, bblk, cblk, dsk, wglu, bglu, *cast)


def _ffn_stages(x, ys, y5, mods, n2g, nfg, wo_ref, wg_ref, wu_ref, wd_ref, write_out, ff_chunk):
    g1, sh2, sc2, g2, shf, scf = mods
    att = jnp.dot(ys.astype(bf16), wo_ref[:D_SSD, :], preferred_element_type=f32)
    att = att + jnp.dot(y5.astype(bf16), wo_ref[D_SSD:, :], preferred_element_type=f32)
    yield
    x1 = x + _by_row(lambda v, s: v * s, att, g1)
    v = _rms_mod(x1, n2g, sc2, sh2).astype(bf16)
    ff = None
    for o in range(0, D_FF, ff_chunk):
        gate = jnp.dot(v, wg_ref[:, o:o + ff_chunk], preferred_element_type=f32)
        up = jnp.dot(v, wu_ref[:, o:o + ff_chunk], preferred_element_type=f32)
        hid = (_silu(gate) * up).astype(bf16)
        part = jnp.dot(hid, wd_ref[o:o + ff_chunk, :], preferred_element_type=f32)
        ff = part if ff is None else ff + part
        yield
    x2 = x1 + _by_row(lambda v, s: v * s, ff, g2)
    write_out(_rms_mod(x2, nfg, scf, shf))


def _ffn_kernel(x_ref, ys_ref, y5_ref, mod_ref, modf_ref, n2g_ref, nfg_ref, wo_ref, wg_ref, wu_ref, wd_ref,
                o_ref, *, seq_major, ff_chunk, sub):
    mods = (_mod_row(mod_ref, 2, seq_major), _mod_row(mod_ref, 3, seq_major), _mod_row(mod_ref, 4, seq_major),
            _mod_row(mod_ref, 5, seq_major), _mod_row(modf_ref, 0, seq_major), _mod_row(modf_ref, 1, seq_major))
    x = _rows_in(x_ref, seq_major)
    rows = x.shape[0]
    tiles = {}

    def stages(r0):
        def write_out(y):
            if seq_major:
                tiles[r0] = y
            else:
                o_ref[0, r0:r0 + sub, :] = y
        return _ffn_stages(x[r0:r0 + sub], ys_ref[0, r0:r0 + sub, :], y5_ref[0, r0:r0 + sub, :], mods,
                           n2g_ref[...], nfg_ref[...], wo_ref, wg_ref, wu_ref, wd_ref, write_out, ff_chunk)

    _interleave(*[stages(r0) for r0 in range(0, rows, sub)])
    if seq_major:
        s, t, d = o_ref.shape
        y = jnp.concatenate([tiles[r0] for r0 in range(0, rows, sub)], axis=0)
        o_ref[...] = jnp.swapaxes(y.reshape(t, s, d), 0, 1)


def _ffn(x, ys, y5, mod, modf, n2g, nfg, wo, wg, wu, wd, *, tm, sub, seq_major, n_seq, first_seq, ff_chunk=256):
    if seq_major:
        x_spec = _const_spec(x.shape)
        nb, rows, d = 1, x.shape[0] * x.shape[1], x.shape[2]
        assert tm == rows
    else:
        nb, rows, d = x.shape
        x_spec = pl.BlockSpec((1, tm, d), lambda i, j: (i, j, 0))
    nt = rows // tm
    mod_spec = _mod_spec(mod, n_seq, first_seq)
    modf_spec = _mod_spec(modf, n_seq, first_seq)
    blk = lambda wd_: pl.BlockSpec((1, tm, wd_), lambda i, j: (i, j, 0))
    single = dict(pipeline_mode=pl.Buffered(1))
    wspec = lambda shape: pl.BlockSpec(shape, lambda i, j: (0, 0), **single)
    return pl.pallas_call(
        functools.partial(_ffn_kernel, seq_major=seq_major, ff_chunk=ff_chunk, sub=sub),
        grid=(nb, nt),
        in_specs=[x_spec, blk(D_SSD), blk(D_S5), mod_spec, modf_spec,
                  _const_spec((1, d)), _const_spec((1, d)),
                  wspec((d, d)), wspec((d, D_FF)), wspec((d, D_FF)), wspec((D_FF, d))],
        out_specs=x_spec,
        out_shape=jax.ShapeDtypeStruct(x.shape, f32),
        compiler_params=_params("parallel", "parallel"),
        name="out_ffn",
    )(x, ys, y5, mod, modf, n2g, nfg, wo, wg, wu, wd)


def kernel(x_prompt, x_sample, c_prompt, c_sample, state_ssd, state_conv, state_s5_re, state_s5_im, w_ada, b_ada, norm1_g, w_in, conv_w, conv_b, ssd_dt_bias, ssd_A_log, ssd_D, ssd_norm_g, s5_A_re, s5_A_im, s5_log_step, s5_B_re, s5_B_im, s5_C_re, s5_C_im, s5_D, w_glu, b_glu, w_out, norm2_g, w_ffn_gate, w_ffn_up, w_ffn_down, w_ada_f, b_ada_f, normf_g):
    assert w_ada.shape[0] == 1, "single-layer stack"
    bp, seq, d = x_prompt.shape
    bs, dseq, _ = x_sample.shape

    c_all = jnp.concatenate([c_sample, c_prompt], axis=0)
    mod = _ada_mod(c_all, w_ada[0], b_ada[0])
    modf = _ada_mod(c_all, w_ada_f, b_ada_f)
    seqs_s = dict(n_seq=bs, first_seq=0)
    seqs_p = dict(n_seq=bp, first_seq=bs)

    w_in_p = w_in[0].T
    pad_h = lambda v: jnp.concatenate([v, jnp.zeros((DT_PAD - HEADS,), f32)]).reshape(1, DT_PAD)
    dtb = pad_h(ssd_dt_bias[0])
    alog = pad_h(ssd_A_log[0])
    dexp = jnp.repeat(ssd_D[0], HEAD_DIM).reshape(1, D_SSD)
    ng = ssd_norm_g[0].reshape(1, D_SSD)
    cw, cb = conv_w[0], conv_b[0].reshape(1, CONV_DIM)

    ar, ai, bblk, cblk = _s5_params(s5_A_re[0], s5_A_im[0], s5_log_step[0],
                                    s5_B_re[0].transpose(0, 2, 1), s5_B_im[0].transpose(0, 2, 1),
                                    s5_C_re[0], s5_C_im[0])
    dsk = s5_D[0].reshape(1, D_S5)
    wglu = w_glu[0].astype(bf16)
    bglu = b_glu[0].reshape(1, D_S5)
    n1g, n2g, nfg = norm1_g[0].reshape(1, d), norm2_g[0].reshape(1, d), normf_g.reshape(1, d)
    nstate = S5_GROUPS * S5_STATE

    z, act, u5, dt, conv_p = _inproj(x_prompt, mod, n1g, w_in_p, (cw, cb), tm=1024, sub=512, seq_major=False,
                                     **seqs_p)
    y_ssd, ssd_p = _ssd_prompt(act, z, dt, dtb, alog, dexp, ng)
    zeros_st = jnp.zeros((bp, nstate), f32)
    y_s5, re_p, im_p, wo, wg, wu, wd = _s5(
        u5, zeros_st, zeros_st, ar, ai, bblk, cblk, dsk, wglu, bglu, tl=128, batch_major=True,
        cast=(w_out[0], w_ffn_gate[0], w_ffn_up[0], w_ffn_down[0]))
    y_prompt = _ffn(x_prompt, y_ssd, y_s5, mod, modf, n2g, nfg, wo, wg, wu, wd, tm=1024, sub=512,
                    seq_major=False, **seqs_p)

    rows_s = dseq * bs
    steps = lambda a: a.reshape(dseq, bs, a.shape[-1])
    flat = lambda a: a.reshape(1, rows_s, a.shape[-1])
    z, xbc, u5, dt = _inproj(x_sample, mod, n1g, w_in_p, tm=rows_s, sub=rows_s // 2, seq_major=True, **seqs_s)
    y_ssd, ssd_s, conv_s = _ssd_sample(steps(xbc), steps(z), steps(dt), state_conv[0].transpose(1, 0, 2),
                                       state_ssd[0], cw, cb, dtb, alog, dexp, ng)
    y_s5, re_s, im_s = _s5(steps(u5), state_s5_re[0].reshape(bs, nstate), state_s5_im[0].reshape(bs, nstate),
                           ar, ai, bblk, cblk, dsk, wglu, bglu, tl=dseq, batch_major=False)
    y_sample = _ffn(x_sample, flat(y_ssd), flat(y_s5), mod, modf, n2g, nfg, wo, wg, wu, wd,
                    tm=rows_s, sub=rows_s, seq_major=True, **seqs_s)

    g5 = (S5_GROUPS, S5_STATE)
    return (y_prompt, y_sample,
            ssd_p[None], ssd_s.reshape((1,) + state_ssd.shape[1:]),
            conv_p[None], conv_s.transpose(1, 0, 2)[None],
            re_p.reshape((1, bp) + g5), re_s.reshape((1, bs) + g5),
            im_p.reshape((1, bp) + g5), im_s.reshape((1, bs) + g5))
```

```python
import functools

import jax
import jax.numpy as jnp
from jax import lax
from jax.experimental import pallas as pl
from jax.experimental.pallas import tpu as pltpu

f32 = jnp.float32
bf16 = jnp.bfloat16

D_MODEL = 1024
D_SSD = 512
HEAD_DIM = 64
HEADS = 8
GROUPS = 2
HEADS_PER_GROUP = HEADS // GROUPS
STATE = 128
CONV_W = 4
CONV_DIM = D_SSD + 2 * GROUPS * STATE
D_S5 = 512
S5_CH = 16
S5_GROUPS = 32
S5_STATE = 64
D_FF = 2816
N_ADA = 6
EPS = 1e-6

LANES = 128
SUBLANES = 8
SSD_CHUNK = 128
S5_SLABS = D_S5 // LANES
S5_SLAB_STATE = (S5_GROUPS // S5_SLABS) * S5_STATE
DT_PAD = LANES
VMEM_LIMIT = 56 * 1024 * 1024
FF_CHUNKS = 11

NT_DIMS = (((1,), (1,)), ((), ()))
TN_DIMS = (((0,), (0,)), ((), ()))


def _silu(x):
    return x * jax.nn.sigmoid(x)


def _interleave(*gens):
    live = list(gens)
    while live:
        for g in list(live):
            try:
                next(g)
            except StopIteration:
                live.remove(g)


def _by_row(fn, v, *ms):
    r, d = v.shape
    m_rows = ms[0].shape[0]
    if m_rows in (1, r):
        return fn(v, *ms)
    out = fn(v.reshape(r // m_rows, m_rows, d), *[m[None] for m in ms])
    return out.reshape(r, d)


def _rms_mod(x, g, sc, sh):
    y = x * lax.rsqrt(jnp.mean(x * x, axis=-1, keepdims=True) + EPS)
    return _by_row(lambda v, s, t: v * (1.0 + s) + t, y * g, sc, sh)


def _mod_row(mod_ref, i, per_row):
    return mod_ref[i] if per_row else mod_ref[i, pl.ds(pl.program_id(0), 1), :]


def _mod_spec(mod, n_seq, first_seq):
    assert first_seq % n_seq == 0
    return pl.BlockSpec((mod.shape[0], n_seq, mod.shape[2]), lambda *_: (0, first_seq // n_seq, 0))


def _const_spec(shape):
    nd = len(shape)
    return pl.BlockSpec(shape, lambda *_: (0,) * nd)


def _params(*sem):
    return pltpu.CompilerParams(dimension_semantics=sem, vmem_limit_bytes=VMEM_LIMIT)


def _ada_kernel(c_ref, w_ref, b_ref, o_ref):
    s = _silu(c_ref[...]).astype(bf16)
    o_ref[0] = jnp.dot(s, w_ref[...].astype(bf16), preferred_element_type=f32) + b_ref[...]


def _ada_mod(c, w, b):
    m, k = c.shape
    n = w.shape[1] // k
    return pl.pallas_call(
        _ada_kernel,
        grid=(n,),
        in_specs=[pl.BlockSpec((m, k), lambda j: (0, 0)),
                  pl.BlockSpec((k, k), lambda j: (0, j)),
                  pl.BlockSpec((1, k), lambda j: (0, j))],
        out_specs=pl.BlockSpec((1, m, k), lambda j: (j, 0, 0)),
        out_shape=jax.ShapeDtypeStruct((n, m, k), f32),
        compiler_params=_params("parallel"),
        name="ada_mod",
    )(c, w, b.reshape(1, n * k))


def _rows_in(x_ref, seq_major):
    if not seq_major:
        return x_ref[0]
    s, t, d = x_ref.shape
    return jnp.swapaxes(x_ref[...], 0, 1).reshape(s * t, d)


def _conv_silu(ext, cw_ref, cb_ref):
    w = [cw_ref[k:k + 1, :] for k in range(CONV_W)]
    back1 = pltpu.roll(ext, 1, 0)
    older = pltpu.roll(ext * w[1] + back1 * w[0], 2, 0)
    conv = cb_ref[...] + ext[SUBLANES:, :] * w[3] + back1[SUBLANES:, :] * w[2] + older[SUBLANES:, :]
    return _silu(conv)


def _inproj_stages(x, sh, sc, g, outs, after=()):
    u = _rms_mod(x, g, sc, sh).astype(bf16)
    yield
    for write, w16 in outs:
        write(lax.dot_general(u, w16, NT_DIMS, preferred_element_type=f32))
        yield
    for stage in after:
        stage()
        yield


def _inproj_kernel(x_ref, mod_ref, g_ref, w_ref, *rest, seq_major, sub, conv):
    if conv:
        cw_ref, cb_ref, z_ref, act_ref, u5_ref, dt_ref, cn_ref, ext_ref = rest
        pl.when(pl.program_id(1) == 0)(lambda: ext_ref.__setitem__(
            (slice(0, SUBLANES), slice(None)), jnp.zeros((SUBLANES, CONV_DIM), f32)))
    else:
        z_ref, xbc_ref, u5_ref, dt_ref = rest
    sh = _mod_row(mod_ref, 0, seq_major)
    sc = _mod_row(mod_ref, 1, seq_major)
    x = _rows_in(x_ref, seq_major)
    rows = x.shape[0]
    T = SSD_CHUNK
    o_dt = D_SSD + CONV_DIM
    w16 = lambda o, width: w_ref[o:o + width, :].astype(bf16)
    w_z, w_xbc, w_u5, w_dt = w16(0, D_SSD), w16(D_SSD, CONV_DIM), w16(o_dt + HEADS, D_S5), w16(o_dt, DT_PAD)

    def stages(r0):
        def to(ref):
            def write(v):
                ref[0, r0:r0 + sub, :] = v
            return write

        def to_ext(v):
            ext_ref[SUBLANES + r0:SUBLANES + r0 + sub, :] = v

        def conv_chunk(c0):
            def stage():
                act_ref[0, c0:c0 + T, :] = _conv_silu(ext_ref[c0:c0 + SUBLANES + T, :], cw_ref, cb_ref)
            return stage

        outs = [(to(z_ref), w_z), (to_ext if conv else to(xbc_ref), w_xbc), (to(u5_ref), w_u5), (to(dt_ref), w_dt)]
        after = [conv_chunk(c0) for c0 in range(r0, r0 + sub, T)] if conv else ()
        return _inproj_stages(x[r0:r0 + sub], sh, sc, g_ref[...], outs, after)

    _interleave(*[stages(r0) for r0 in range(0, rows, sub)])
    if conv:
        cn_ref[0] = ext_ref[SUBLANES + rows - (CONV_W - 1):SUBLANES + rows, :]
        ext_ref[0:SUBLANES, :] = ext_ref[rows:rows + SUBLANES, :]


def _inproj(x, mod, g, w, conv_wb=None, *, tm, sub, seq_major, n_seq, first_seq):
    if seq_major:
        x_spec = _const_spec(x.shape)
        nb, rows, d = 1, x.shape[0] * x.shape[1], x.shape[2]
        assert tm == rows
    else:
        nb, rows, d = x.shape
        x_spec = pl.BlockSpec((1, tm, d), lambda i, j: (i, j, 0))
    nt = rows // tm
    widths = (D_SSD, CONV_DIM, D_S5, DT_PAD)
    conv = conv_wb is not None
    in_specs = [x_spec, _mod_spec(mod, n_seq, first_seq), _const_spec((1, d)),
                pl.BlockSpec(w.shape, lambda i, j: (0, 0), pipeline_mode=pl.Buffered(1))]
    out_specs = [pl.BlockSpec((1, tm, wd), lambda i, j: (i, j, 0)) for wd in widths]
    out_shape = [jax.ShapeDtypeStruct((nb, rows, wd), f32) for wd in widths]
    scratch = []
    if conv:
        in_specs += [_const_spec((CONV_W, CONV_DIM)), _const_spec((1, CONV_DIM))]
        out_specs.append(pl.BlockSpec((1, CONV_W - 1, CONV_DIM), lambda i, j: (i, 0, 0)))
        out_shape.append(jax.ShapeDtypeStruct((nb, CONV_W - 1, CONV_DIM), f32))
        scratch.append(pltpu.VMEM((SUBLANES + tm, CONV_DIM), f32))
    return pl.pallas_call(
        functools.partial(_inproj_kernel, seq_major=seq_major, sub=sub, conv=conv),
        grid=(nb, nt),
        in_specs=in_specs, out_specs=out_specs, out_shape=out_shape, scratch_shapes=scratch,
        compiler_params=_params("parallel", "arbitrary" if conv else "parallel"),
        name="in_proj",
    )(x, mod, g, w, *(conv_wb or ()))


def _split3(x):
    hi = x.astype(bf16)
    r1 = x - hi.astype(f32)
    mid = r1.astype(bf16)
    lo = (r1 - mid.astype(f32)).astype(bf16)
    return hi, mid, lo


def _dot_sel_lhs(sel16, x):
    return sum(jnp.dot(sel16, p, preferred_element_type=f32) for p in _split3(x))


def _dot_sel_rhs(x, sel16):
    hi, mid, _ = _split3(x)
    return jnp.dot(hi, sel16, preferred_element_type=f32) + jnp.dot(mid, sel16, preferred_element_type=f32)


def _gated_group_norm(y, z, ng):
    y = y * _silu(z)
    gw = D_SSD // GROUPS
    parts = []
    for g in range(GROUPS):
        yg = y[:, g * gw:(g + 1) * gw]
        parts.append(yg * lax.rsqrt(jnp.mean(yg * yg, axis=-1, keepdims=True) + EPS))
    return jnp.concatenate(parts, axis=-1) * ng


def _ssd_prompt_kernel(act_ref, z_ref, dt_ref, dtb_ref, alog_ref, dexp_ref, ng_ref, y_ref, st_ref, h_ref, *, cps):
    T = SSD_CHUNK
    gw = HEADS_PER_GROUP * HEAD_DIM

    @pl.when(pl.program_id(1) == 0)
    def _():
        h_ref[...] = jnp.zeros_like(h_ref)

    row = lax.broadcasted_iota(jnp.int32, (T, T), 0)
    col = lax.broadcasted_iota(jnp.int32, (T, T), 1)
    tri = row >= col
    tri16 = jnp.where(tri, 1.0, 0.0).astype(bf16)
    low_half = lax.broadcasted_iota(jnp.int32, (T, LANES), 1) < HEAD_DIM
    a_neg = -jnp.exp(alog_ref[...])

    lane_head = lax.broadcasted_iota(jnp.int32, (DT_PAD, D_SSD), 1) // HEAD_DIM
    sel16 = jnp.where(lax.broadcasted_iota(jnp.int32, (DT_PAD, D_SSD), 0) == lane_head, 1.0, 0.0).astype(bf16)

    def spread(q):
        return _dot_sel_rhs(q, sel16)

    def state_free(ci):
        r0 = ci * T
        act = act_ref[0, r0:r0 + T, :]
        xs = act[:, :D_SSD]
        bm = act[:, D_SSD:D_SSD + GROUPS * STATE]
        cm = act[:, D_SSD + GROUPS * STATE:]
        dtv = jax.nn.softplus(dt_ref[0, r0:r0 + T, :] + dtb_ref[...])
        a_cs = _dot_sel_lhs(tri16, dtv * a_neg)
        bg16s = [bm[:, g * STATE:(g + 1) * STATE].astype(bf16) for g in range(GROUPS)]
        cg16s = [cm[:, g * STATE:(g + 1) * STATE].astype(bf16) for g in range(GROUPS)]
        gmats = [lax.dot_general(cg16s[g], bg16s[g], NT_DIMS, preferred_element_type=f32)
                 for g in range(GROUPS)]
        yield
        a_last = a_cs[T - 1:T, :]
        a_cs_t = a_cs.T
        rep = spread(jnp.concatenate([dtv, jnp.exp(a_last - a_cs), jnp.exp(a_cs)], axis=0))
        yield
        x = xs * rep[:T]
        x16 = x.astype(bf16)
        xd16 = (x * rep[T:2 * T]).astype(bf16)
        e_cs = rep[2 * T:]
        yield
        y_part = []
        for g in range(GROUPS):
            for pr in range(HEADS_PER_GROUP // 2):
                s_pair = []
                for q in range(2):
                    h = g * HEADS_PER_GROUP + 2 * pr + q
                    seg = a_cs[:, h:h + 1] - a_cs_t[h:h + 1, :]
                    lmat = jnp.exp(jnp.where(tri, seg, -jnp.inf))
                    s_pair.append((gmats[g] * lmat).astype(bf16))
                lanes = slice(g * gw + pr * LANES, g * gw + (pr + 1) * LANES)
                both = jnp.dot(jnp.concatenate(s_pair, axis=0), x16[:, lanes], preferred_element_type=f32)
                yield
                y_part.append(jnp.where(low_half, both[:T], both[T:]))
        y_free = jnp.concatenate(y_part, axis=-1) + xs * dexp_ref[...]
        return y_free, e_cs, jnp.exp(a_last), xd16, bg16s, cg16s

    def state_step(ci, y_free, e_cs, e_last, xd16, bg16s, cg16s):
        r0 = ci * T
        y_offs = []
        for g in range(GROUPS):
            hp = h_ref[g]
            y_offs.append(lax.dot_general(cg16s[g], hp.astype(bf16), NT_DIMS, preferred_element_type=f32))
            upd = lax.dot_general(xd16[:, g * gw:(g + 1) * gw], bg16s[g], TN_DIMS, preferred_element_type=f32)
            for hh in range(HEADS_PER_GROUP):
                h = g * HEADS_PER_GROUP + hh
                rows = slice(hh * HEAD_DIM, (hh + 1) * HEAD_DIM)
                h_ref[g, rows, :] = e_last[:, h:h + 1] * hp[rows] + upd[rows]
        y = y_free + jnp.concatenate(y_offs, axis=-1) * e_cs
        y_ref[0, r0:r0 + T, :] = _gated_group_norm(y, z_ref[0, r0:r0 + T, :], ng_ref[...])

    gens = [state_free(ci) for ci in range(cps)]
    free = [None] * cps
    while any(f is None for f in free):
        for ci in range(cps):
            if free[ci] is None:
                try:
                    next(gens[ci])
                except StopIteration as done:
                    free[ci] = done.value
    for ci in range(cps):
        state_step(ci, *free[ci])

    @pl.when(pl.program_id(1) == pl.num_programs(1) - 1)
    def _():
        st_ref[0] = h_ref[...].reshape(HEADS, HEAD_DIM, STATE)


def _ssd_prompt(act, z, dt, dtb, alog, dexp, ng, *, cps=8):
    nb, L, _ = act.shape
    rows = cps * SSD_CHUNK
    blk = lambda wd: pl.BlockSpec((1, rows, wd), lambda b, c: (b, c, 0))
    return pl.pallas_call(
        functools.partial(_ssd_prompt_kernel, cps=cps),
        grid=(nb, L // rows),
        in_specs=[blk(CONV_DIM), blk(D_SSD), blk(DT_PAD), _const_spec((1, DT_PAD)), _const_spec((1, DT_PAD)),
                  _const_spec((1, D_SSD)), _const_spec((1, D_SSD))],
        out_specs=[blk(D_SSD), pl.BlockSpec((1, HEADS, HEAD_DIM, STATE), lambda b, c: (b, 0, 0, 0))],
        out_shape=[jax.ShapeDtypeStruct((nb, L, D_SSD), f32),
                   jax.ShapeDtypeStruct((nb, HEADS, HEAD_DIM, STATE), f32)],
        scratch_shapes=[pltpu.VMEM((GROUPS, HEADS_PER_GROUP * HEAD_DIM, STATE), f32)],
        compiler_params=_params("parallel", "arbitrary"),
        name="ssd_prompt",
    )(act, z, dt, dtb, alog, dexp, ng)


def _ssd_sample_kernel(xbc_ref, z_ref, dt_ref, cbuf_ref, st_ref, cw_ref, cb_ref, dtb_ref, alog_ref,
                       dexp_ref, ng_ref, y_ref, stn_ref, cn_ref, dtot_ref, *, L, nb):
    gw = HEADS_PER_GROUP * HEAD_DIM
    full = [cbuf_ref[j] for j in range(CONV_W - 1)] + [xbc_ref[t] for t in range(L)]
    for j in range(CONV_W - 1):
        cn_ref[j] = full[L + j]
    a_neg = -jnp.exp(alog_ref[...])
    xs, bm, cm, dtv, acs = [], [], [], [], []
    run = None
    for t in range(L):
        conv = cb_ref[...]
        for k in range(CONV_W):
            conv = conv + full[t + k] * cw_ref[k:k + 1, :]
        act = _silu(conv)
        xs.append(act[:, :D_SSD])
        bm.append(act[:, D_SSD:D_SSD + GROUPS * STATE])
        cm.append(act[:, D_SSD + GROUPS * STATE:])
        d = jax.nn.softplus(dt_ref[t] + dtb_ref[...])
        dtv.append(d)
        run = d * a_neg if run is None else run + d * a_neg
        acs.append(run)
    a_tot = acs[L - 1]
    dtot_ref[...] = jnp.exp(a_tot)

    lane_head = lax.broadcasted_iota(jnp.int32, (DT_PAD, D_SSD), 1) // HEAD_DIM
    sel16 = jnp.where(lax.broadcasted_iota(jnp.int32, (DT_PAD, D_SSD), 0) == lane_head, 1.0, 0.0).astype(bf16)
    pairs = [(t, s) for t in range(L) for s in range(t)]
    factors = (dtv + [jnp.exp(a_tot - acs[t]) for t in range(L)] + [jnp.exp(acs[t]) for t in range(L)]
               + [jnp.exp(acs[t] - acs[s]) for t, s in pairs])
    rep = _dot_sel_rhs(jnp.concatenate(factors, axis=0), sel16)
    piece = lambda i: rep[i * nb:(i + 1) * nb]
    x = [xs[t] * piece(t) for t in range(L)]
    xd_stack = jnp.concatenate([x[t] * piece(L + t) for t in range(L)], axis=0)
    e_cs = [piece(2 * L + t) for t in range(L)]
    decay = {ts: piece(3 * L + i) for i, ts in enumerate(pairs)}

    in_group0 = lax.broadcasted_iota(jnp.int32, (nb, D_SSD), 1) < gw
    y_intra = []
    for t in range(L):
        acc = None
        for s in range(t + 1):
            cb_dot = [jnp.sum(cm[t][:, g * STATE:(g + 1) * STATE] * bm[s][:, g * STATE:(g + 1) * STATE],
                              axis=-1, keepdims=True) for g in range(GROUPS)]
            w = jnp.where(in_group0, cb_dot[0], cb_dot[1])
            term = w * x[s] if s == t else w * decay[(t, s)] * x[s]
            acc = term if acc is None else acc + term
        y_intra.append(acc)

    c_stack = [jnp.concatenate([cm[t][:, g * STATE:(g + 1) * STATE] for t in range(L)], axis=0).astype(bf16)
               for g in range(GROUPS)]
    b_stack = [jnp.concatenate([bm[t][:, g * STATE:(g + 1) * STATE] for t in range(L)], axis=0).astype(bf16)
               for g in range(GROUPS)]
    seq_of_row = lax.broadcasted_iota(jnp.int32, (L * nb, gw), 0) & (nb - 1)

    def per_seq(b, yoff):
        mine = seq_of_row == b
        drow = dtot_ref[pl.ds(b, 1), :]
        out = []
        for g in range(GROUPS):
            h0 = st_ref[b, g]
            r = lax.dot_general(c_stack[g], h0.astype(bf16), NT_DIMS, preferred_element_type=f32)
            out.append(yoff[g] + jnp.where(mine, r, 0.0))
            xm = jnp.where(mine, xd_stack[:, g * gw:(g + 1) * gw], 0.0).astype(bf16)
            upd = lax.dot_general(xm, b_stack[g], TN_DIMS, preferred_element_type=f32)
            for hh in range(HEADS_PER_GROUP):
                h = g * HEADS_PER_GROUP + hh
                rows = slice(hh * HEAD_DIM, (hh + 1) * HEAD_DIM)
                stn_ref[b, g, rows, :] = drow[:, h:h + 1] * h0[rows] + upd[rows]
        return tuple(out)

    yoff = lax.fori_loop(0, nb, per_seq, tuple(jnp.zeros((L * nb, gw), f32) for _ in range(GROUPS)),
                         unroll=8)

    for t in range(L):
        y_off = jnp.concatenate([yoff[g][t * nb:(t + 1) * nb] for g in range(GROUPS)], axis=-1)
        y = y_intra[t] + y_off * e_cs[t] + xs[t] * dexp_ref[...]
        y_ref[t] = _gated_group_norm(y, z_ref[t], ng_ref[...])


def _ssd_sample(xbc, z, dt, cbuf, st, cw, cb, dtb, alog, dexp, ng, *, nb=16):
    L, B, _ = xbc.shape
    assert nb & (nb - 1) == 0 and B % nb == 0
    tblk = lambda n, wd: pl.BlockSpec((n, nb, wd), lambda i: (0, i, 0))
    gw = HEADS_PER_GROUP * HEAD_DIM
    st = st.reshape(B, GROUPS, gw, STATE)
    st_spec = pl.BlockSpec((nb, GROUPS, gw, STATE), lambda i: (i, 0, 0, 0))
    return pl.pallas_call(
        functools.partial(_ssd_sample_kernel, L=L, nb=nb),
        grid=(B // nb,),
        in_specs=[tblk(L, CONV_DIM), tblk(L, D_SSD), tblk(L, DT_PAD), tblk(CONV_W - 1, CONV_DIM), st_spec,
                  _const_spec((CONV_W, CONV_DIM)), _const_spec((1, CONV_DIM)),
                  _const_spec((1, DT_PAD)), _const_spec((1, DT_PAD)),
                  _const_spec((1, D_SSD)), _const_spec((1, D_SSD))],
        out_specs=[tblk(L, D_SSD), st_spec, tblk(CONV_W - 1, CONV_DIM)],
        out_shape=[jax.ShapeDtypeStruct((L, B, D_SSD), f32),
                   jax.ShapeDtypeStruct((B, GROUPS, gw, STATE), f32),
                   jax.ShapeDtypeStruct((CONV_W - 1, B, CONV_DIM), f32)],
        scratch_shapes=[pltpu.VMEM((nb, DT_PAD), f32)],
        compiler_params=_params("parallel"),
        name="ssd_sample",
    )(xbc, z, dt, cbuf, st, cw, cb, dtb, alog, dexp, ng)


def _s5_param_kernel(lr_ref, li_ref, ls_ref, br_ref, bi_ref, cr_ref, ci_ref,
                     ar_ref, ai_ref, bblk_ref, cblk_ref, b_scr, c_scr):
    lr, li = lr_ref[...], li_ref[...]
    step = jnp.exp(ls_ref[...])
    mag = jnp.exp(lr * step)
    abr = mag * jnp.cos(li * step)
    abi = mag * jnp.sin(li * step)
    nr, ni = abr - 1.0, abi
    den = lr * lr + li * li
    fr = (nr * lr + ni * li) / den
    fi = (ni * lr - nr * li) / den
    br, bi = br_ref[...], bi_ref[...]
    bbr = fr * br - fi * bi
    bbi = fr * bi + fi * br
    b_scr[...] = jnp.zeros_like(b_scr)
    c_scr[...] = jnp.zeros_like(c_scr)
    gps = S5_GROUPS // S5_SLABS
    ns = S5_SLAB_STATE
    for g in range(S5_GROUPS):
        s, gl = divmod(g, gps)
        ch = slice(gl * S5_CH, (gl + 1) * S5_CH)
        st = slice(gl * S5_STATE, (gl + 1) * S5_STATE)
        st_im = slice(ns + gl * S5_STATE, ns + (gl + 1) * S5_STATE)
        ar_ref[s, :, st] = jnp.broadcast_to(abr[g], (SUBLANES, S5_STATE))
        ai_ref[s, :, st] = jnp.broadcast_to(abi[g], (SUBLANES, S5_STATE))
        b_scr[s, ch, st] = bbr[g]
        b_scr[s, ch, st_im] = bbi[g]
        c_scr[s, st, ch] = cr_ref[g].T
        c_scr[s, st_im, ch] = -ci_ref[g].T
    bblk_ref[...] = b_scr[...].astype(bf16)
    cblk_ref[...] = c_scr[...].astype(bf16)


def _s5_params(a_re, a_im, log_step, b_re_t, b_im_t, c_re, c_im):
    g, p = a_re.shape
    ns = S5_SLAB_STATE
    return pl.pallas_call(
        _s5_param_kernel,
        out_shape=[jax.ShapeDtypeStruct((S5_SLABS, SUBLANES, ns), f32),
                   jax.ShapeDtypeStruct((S5_SLABS, SUBLANES, ns), f32),
                   jax.ShapeDtypeStruct((S5_SLABS, LANES, 2 * ns), bf16),
                   jax.ShapeDtypeStruct((S5_SLABS, 2 * ns, LANES), bf16)],
        scratch_shapes=[pltpu.VMEM((S5_SLABS, LANES, 2 * ns), f32),
                        pltpu.VMEM((S5_SLABS, 2 * ns, LANES), f32)],
        name="s5_params",
    )(a_re.reshape(g, 1, p), a_im.reshape(g, 1, p), log_step.reshape(g, 1, 1), b_re_t, b_im_t, c_re, c_im)


def _gelu(x):
    return 0.5 * x * (1.0 + lax.erf(x * (2.0 ** -0.5)))


def _s5_slabs_interleaved(ut_ref, hbuf_ref, hst_ref, g_ref, ar_ref, ai_ref, bblk_ref, cblk_ref, dsk_ref, tl,
                          row_block=256, n_stages=8):
    ns = S5_SLAB_STATE
    nb = SUBLANES
    rows = nb * tl

    def lanes(s):
        return slice(s * LANES, (s + 1) * LANES)

    def bu_stage(slabs):
        for s in slabs:
            for r in range(0, rows, row_block):
                us = ut_ref[s, r:r + row_block, :]
                hbuf_ref[s, nb + r:nb + r + row_block, :] = jnp.dot(us.astype(bf16), bblk_ref[s],
                                                                    preferred_element_type=f32)
                yield

    def scan_stage(slabs):
        carry = [(hst_ref[s, :, :ns], hst_ref[s, :, ns:]) for s in slabs]
        coef = [(ar_ref[s], ai_ref[s]) for s in slabs]
        for t in range(tl):
            r = nb + nb * t
            for i, s in enumerate(slabs):
                (pr, pi), (ar, ai) = carry[i], coef[i]
                nr = ar * pr - ai * pi + hbuf_ref[s, r:r + nb, :ns]
                ni = ar * pi + ai * pr + hbuf_ref[s, r:r + nb, ns:]
                hbuf_ref[s, r:r + nb, :ns] = nr
                hbuf_ref[s, r:r + nb, ns:] = ni
                carry[i] = (nr, ni)
            if (t + 1) % (tl // n_stages) == 0:
                yield
        for i, s in enumerate(slabs):
            hst_ref[s, :, :ns] = carry[i][0]
            hst_ref[s, :, ns:] = carry[i][1]

    def y_stage(slabs):
        for s in slabs:
            for r in range(0, rows, row_block):
                h16 = hbuf_ref[s, nb + r:nb + r + row_block, :].astype(bf16)
                ys = jnp.dot(h16, cblk_ref[s], preferred_element_type=f32)
                ys = ys + dsk_ref[:, lanes(s)] * ut_ref[s, r:r + row_block, :]
                g_ref[s, r:r + row_block, :] = _gelu(ys)
                yield

    half = S5_SLABS // 2
    first, second = tuple(range(half)), tuple(range(half, S5_SLABS))
    _interleave(bu_stage(first))
    _interleave(scan_stage(first), bu_stage(second))
    _interleave(scan_stage(second), y_stage(first))
    _interleave(y_stage(second))


def _s5_scan_vmem(hbuf_ref, hst_ref, ar_ref, ai_ref, s, nb, rows):
    ns = S5_SLAB_STATE
    hbuf_ref[s, 0:nb, :] = hst_ref[s]
    ar, ai = ar_ref[s], ai_ref[s]

    def body(i, carry):
        j = pl.multiple_of(nb + i * SUBLANES, SUBLANES)
        prev = hbuf_ref[s, pl.ds(j - nb, SUBLANES), :]
        cur = hbuf_ref[s, pl.ds(j, SUBLANES), :]
        pr, pi = prev[:, :ns], prev[:, ns:]
        hbuf_ref[s, pl.ds(j, SUBLANES), :ns] = ar * pr - ai * pi + cur[:, :ns]
        hbuf_ref[s, pl.ds(j, SUBLANES), ns:] = ar * pi + ai * pr + cur[:, ns:]
        return carry

    lax.fori_loop(0, rows // SUBLANES, body, 0, unroll=min(8, nb // SUBLANES))
    hst_ref[s] = hbuf_ref[s, rows:rows + nb, :]


S5_N_IN = 10


def _s5_kernel(*refs, nb, tl, batch_major, n_cast):
    (u_ref, re0_ref, im0_ref, ar_ref, ai_ref, bblk_ref, cblk_ref, dsk_ref, wglu_ref,
     bglu_ref) = refs[:S5_N_IN]
    cast_in = refs[S5_N_IN:S5_N_IN + n_cast]
    y_ref, ren_ref, imn_ref = refs[S5_N_IN + n_cast:S5_N_IN + n_cast + 3]
    cast_out = refs[S5_N_IN + n_cast + 3:S5_N_IN + 2 * n_cast + 3]
    ut_ref, hbuf_ref, hst_ref, g_ref = refs[S5_N_IN + 2 * n_cast + 3:]
    for w_ref, w16_ref in zip(cast_in, cast_out):
        w16_ref[...] = w_ref[...].astype(bf16)
    step = pl.program_id(0)
    rows = nb * tl
    ns = S5_SLAB_STATE

    @pl.when(step == 0)
    def _():
        for s in range(S5_SLABS):
            hst_ref[s, :, :ns] = re0_ref[:, s * ns:(s + 1) * ns]
            hst_ref[s, :, ns:] = im0_ref[:, s * ns:(s + 1) * ns]

    for s in range(S5_SLABS):
        sl = slice(s * LANES, (s + 1) * LANES)
        if batch_major:
            for b in range(nb):
                ut_ref[s, pl.ds(b, tl, stride=nb), :] = u_ref[b, :, sl]
        else:
            ut_ref[s] = u_ref[:, :, sl].reshape(rows, LANES)

    if nb == SUBLANES:
        _s5_slabs_interleaved(ut_ref, hbuf_ref, hst_ref, g_ref, ar_ref, ai_ref, bblk_ref, cblk_ref, dsk_ref, tl)
    else:
        for s in range(S5_SLABS):
            sl = slice(s * LANES, (s + 1) * LANES)
            hbuf_ref[s, nb:nb + rows, :] = jnp.dot(ut_ref[s].astype(bf16), bblk_ref[s],
                                                   preferred_element_type=f32)
            _s5_scan_vmem(hbuf_ref, hst_ref, ar_ref, ai_ref, s, nb, rows)
            ys = jnp.dot(hbuf_ref[s, nb:nb + rows, :].astype(bf16), cblk_ref[s], preferred_element_type=f32)
            g_ref[s] = _gelu(ys + dsk_ref[:, sl] * ut_ref[s])

    g = jnp.concatenate([g_ref[s] for s in range(S5_SLABS)], axis=-1)
    gate = jnp.dot(g.astype(bf16), wglu_ref[...], preferred_element_type=f32) + bglu_ref[...]
    out = g * jax.nn.sigmoid(gate)
    if batch_major:
        for s in range(S5_SLABS):
            sl = slice(s * LANES, (s + 1) * LANES)
            ut_ref[s] = out[:, sl]
            for b in range(nb):
                y_ref[b, :, sl] = ut_ref[s, pl.ds(b, tl, stride=nb), :]
    else:
        y_ref[...] = out.reshape(tl, nb, D_S5)

    @pl.when(step == pl.num_programs(0) - 1)
    def _():
        for s in range(S5_SLABS):
            ren_ref[:, s * ns:(s + 1) * ns] = hst_ref[s, :, :ns]
            imn_ref[:, s * ns:(s + 1) * ns] = hst_ref[s, :, ns:]


def _s5(u, re0, im0, ar, ai, bblk, cblk, dsk, wglu, bglu, *, tl, batch_major, cast=()):
    if batch_major:
        nb, L, _ = u.shape
        u_spec = pl.BlockSpec((nb, tl, D_S5), lambda i: (0, i, 0))
    else:
        L, nb, _ = u.shape
        u_spec = pl.BlockSpec((tl, nb, D_S5), lambda i: (i, 0, 0))
    rows = nb * tl
    steps = L // tl
    nstate = S5_GROUPS * S5_STATE
    st_spec = _const_spec((nb, nstate))
    cast_specs = [pl.BlockSpec((w.shape[0] // steps, w.shape[1]), lambda i: (i, 0)) for w in cast]
    assert all(w.shape[0] % (steps * 2 * SUBLANES) == 0 for w in cast)
    return pl.pallas_call(
        functools.partial(_s5_kernel, nb=nb, tl=tl, batch_major=batch_major, n_cast=len(cast)),
        grid=(steps,),
        in_specs=[u_spec, st_spec, st_spec,
                  _const_spec((S5_SLABS, SUBLANES, S5_SLAB_STATE)),
                  _const_spec((S5_SLABS, SUBLANES, S5_SLAB_STATE)),
                  _const_spec((S5_SLABS, LANES, 2 * S5_SLAB_STATE)),
                  _const_spec((S5_SLABS, 2 * S5_SLAB_STATE, LANES)),
                  _const_spec((1, D_S5)), _const_spec((D_S5, D_S5)), _const_spec((1, D_S5))] + cast_specs,
        out_specs=[u_spec, st_spec, st_spec] + cast_specs,
        out_shape=[jax.ShapeDtypeStruct(u.shape, f32),
                   jax.ShapeDtypeStruct((nb, nstate), f32),
                   jax.ShapeDtypeStruct((nb, nstate), f32)]
                  + [jax.ShapeDtypeStruct(w.shape, bf16) for w in cast],
        scratch_shapes=[pltpu.VMEM((S5_SLABS, rows, LANES), f32),
                        pltpu.VMEM((S5_SLABS, nb + rows, 2 * S5_SLAB_STATE), f32),
                        pltpu.VMEM((S5_SLABS, nb, 2 * S5_SLAB_STATE), f32),
                        pltpu.VMEM((S5_SLABS, rows, LANES), f32)],
        compiler_params=_params("arbitrary"),
        name="s5_mixer",
    )(u, re0, im0, ar, ai, bblk, cblk, dsk, wglu, bglu, *cast)


def _ffn_mods(mod_ref, modf_ref, per_row):
    return (_mod_row(mod_ref, 2, per_row), _mod_row(mod_ref, 3, per_row), _mod_row(mod_ref, 4, per_row),
            _mod_row(mod_ref, 5, per_row), _mod_row(modf_ref, 0, per_row), _mod_row(modf_ref, 1, per_row))


def _ffn_head(x, ys, y5, mods, n2g, wo_ref):
    g1, sh2, sc2 = mods[:3]
    att = jnp.dot(ys.astype(bf16), wo_ref[:D_SSD, :], preferred_element_type=f32)
    att = att + jnp.dot(y5.astype(bf16), wo_ref[D_SSD:, :], preferred_element_type=f32)
    x1 = x + _by_row(lambda v, s: v * s, att, g1)
    return x1, _rms_mod(x1, n2g, sc2, sh2).astype(bf16)


def _ffn_chunk(v, wg, wu, wd):
    gate = jnp.dot(v, wg, preferred_element_type=f32)
    up = jnp.dot(v, wu, preferred_element_type=f32)
    return jnp.dot((_silu(gate) * up).astype(bf16), wd, preferred_element_type=f32)


def _ffn_tail(x1, ff, mods, nfg):
    g2, shf, scf = mods[3:]
    return _rms_mod(x1 + _by_row(lambda v, s: v * s, ff, g2), nfg, scf, shf)


def _ffn_stages(x, ys, y5, mods, n2g, nfg, wo_ref, wg_ref, wu_ref, wd_ref, write_out, ff_chunk):
    x1, v = _ffn_head(x, ys, y5, mods, n2g, wo_ref)
    yield
    ff = None
    for o in range(0, D_FF, ff_chunk):
        part = _ffn_chunk(v, wg_ref[:, o:o + ff_chunk], wu_ref[:, o:o + ff_chunk], wd_ref[o:o + ff_chunk, :])
        ff = part if ff is None else ff + part
        yield
    write_out(_ffn_tail(x1, ff, mods, nfg))


def _ffn_kernel(x_ref, ys_ref, y5_ref, mod_ref, modf_ref, n2g_ref, nfg_ref, wo_ref, wg_ref, wu_ref, wd_ref,
                o_ref, *, sub):
    mods = _ffn_mods(mod_ref, modf_ref, False)

    def stages(r0):
        def write_out(y):
            o_ref[0, r0:r0 + sub, :] = y
        tile = lambda ref: ref[0, r0:r0 + sub, :]
        return _ffn_stages(tile(x_ref), tile(ys_ref), tile(y5_ref), mods, n2g_ref[...], nfg_ref[...],
                           wo_ref, wg_ref, wu_ref, wd_ref, write_out, wg_ref.shape[1] // FF_CHUNKS)

    _interleave(*[stages(r0) for r0 in range(0, x_ref.shape[1], sub)])


def _ffn_streamed_kernel(x_ref, ys_ref, y5_ref, mod_ref, modf_ref, n2g_ref, nfg_ref, wo_ref, wg_ref, wu_ref,
                         wd_ref, o_ref, x1_ref, v_ref, ff_ref):
    k = pl.program_id(0)
    mods = _ffn_mods(mod_ref, modf_ref, True)

    @pl.when(k == 0)
    def _():
        x1, v = _ffn_head(_rows_in(x_ref, True), ys_ref[0], y5_ref[0], mods, n2g_ref[...], wo_ref)
        x1_ref[...] = x1
        v_ref[...] = v
        ff_ref[...] = jnp.zeros_like(ff_ref)

    ff_ref[...] += _ffn_chunk(v_ref[...], wg_ref[...], wu_ref[...], wd_ref[...])

    @pl.when(k == pl.num_programs(0) - 1)
    def _():
        s, t, d = o_ref.shape
        y = _ffn_tail(x1_ref[...], ff_ref[...], mods, nfg_ref[...])
        o_ref[...] = jnp.swapaxes(y.reshape(t, s, d), 0, 1)


def _ffn(x, ys, y5, mod, modf, n2g, nfg, wo, wg, wu, wd, *, seq_major, n_seq, first_seq, tm=1024, sub=512):
    d = x.shape[-1]
    mod_spec = _mod_spec(mod, n_seq, first_seq)
    modf_spec = _mod_spec(modf, n_seq, first_seq)
    vec = _const_spec((1, d))
    if seq_major:
        rows = x.shape[0] * x.shape[1]
        ck = D_FF // FF_CHUNKS
        return pl.pallas_call(
            _ffn_streamed_kernel,
            grid=(FF_CHUNKS,),
            in_specs=[_const_spec(x.shape), _const_spec(ys.shape), _const_spec(y5.shape), mod_spec, modf_spec,
                      vec, vec, _const_spec((d, d)),
                      pl.BlockSpec((d, ck), lambda k: (0, k)), pl.BlockSpec((d, ck), lambda k: (0, k)),
                      pl.BlockSpec((ck, d), lambda k: (k, 0))],
            out_specs=_const_spec(x.shape),
            out_shape=jax.ShapeDtypeStruct(x.shape, f32),
            scratch_shapes=[pltpu.VMEM((rows, d), f32), pltpu.VMEM((rows, d), bf16), pltpu.VMEM((rows, d), f32)],
            compiler_params=_params("arbitrary"),
            name="out_ffn_streamed",
        )(x, ys, y5, mod, modf, n2g, nfg, wo, wg, wu, wd)
    nb, rows, _ = x.shape
    blk = lambda wd_: pl.BlockSpec((1, tm, wd_), lambda i, j: (i, j, 0))
    wspec = lambda shape: pl.BlockSpec(shape, lambda i, j: (0, 0), pipeline_mode=pl.Buffered(1))
    return pl.pallas_call(
        functools.partial(_ffn_kernel, sub=sub),
        grid=(nb, rows // tm),
        in_specs=[blk(d), blk(D_SSD), blk(D_S5), mod_spec, modf_spec, vec, vec,
                  wspec((d, d)), wspec((d, D_FF)), wspec((d, D_FF)), wspec((D_FF, d))],
        out_specs=blk(d),
        out_shape=jax.ShapeDtypeStruct(x.shape, f32),
        compiler_params=_params("parallel", "parallel"),
        name="out_ffn",
    )(x, ys, y5, mod, modf, n2g, nfg, wo, wg, wu, wd)


def kernel(x_prompt, x_sample, c_prompt, c_sample, state_ssd, state_conv, state_s5_re, state_s5_im, w_ada, b_ada, norm1_g, w_in, conv_w, conv_b, ssd_dt_bias, ssd_A_log, ssd_D, ssd_norm_g, s5_A_re, s5_A_im, s5_log_step, s5_B_re, s5_B_im, s5_C_re, s5_C_im, s5_D, w_glu, b_glu, w_out, norm2_g, w_ffn_gate, w_ffn_up, w_ffn_down, w_ada_f, b_ada_f, normf_g):
    assert w_ada.shape[0] == 1, "single-layer stack"
    bp, seq, d = x_prompt.shape
    bs, dseq, _ = x_sample.shape

    c_all = jnp.concatenate([c_sample, c_prompt], axis=0)
    mod = _ada_mod(c_all, w_ada[0], b_ada[0])
    modf = _ada_mod(c_all, w_ada_f, b_ada_f)
    seqs_s = dict(n_seq=bs, first_seq=0)
    seqs_p = dict(n_seq=bp, first_seq=bs)

    w_in_p = w_in[0].T
    pad_h = lambda v: jnp.concatenate([v, jnp.zeros((DT_PAD - HEADS,), f32)]).reshape(1, DT_PAD)
    dtb = pad_h(ssd_dt_bias[0])
    alog = pad_h(ssd_A_log[0])
    dexp = jnp.repeat(ssd_D[0], HEAD_DIM).reshape(1, D_SSD)
    ng = ssd_norm_g[0].reshape(1, D_SSD)
    cw, cb = conv_w[0], conv_b[0].reshape(1, CONV_DIM)

    ar, ai, bblk, cblk = _s5_params(s5_A_re[0], s5_A_im[0], s5_log_step[0],
                                    s5_B_re[0].transpose(0, 2, 1), s5_B_im[0].transpose(0, 2, 1),
                                    s5_C_re[0], s5_C_im[0])
    dsk = s5_D[0].reshape(1, D_S5)
    wglu = w_glu[0].astype(bf16)
    bglu = b_glu[0].reshape(1, D_S5)
    n1g, n2g, nfg = norm1_g[0].reshape(1, d), norm2_g[0].reshape(1, d), normf_g.reshape(1, d)
    nstate = S5_GROUPS * S5_STATE

    z, act, u5, dt, conv_p = _inproj(x_prompt, mod, n1g, w_in_p, (cw, cb), tm=1024, sub=512, seq_major=False,
                                     **seqs_p)
    y_ssd, ssd_p = _ssd_prompt(act, z, dt, dtb, alog, dexp, ng)
    zeros_st = jnp.zeros((bp, nstate), f32)
    y_s5, re_p, im_p, wo, wg, wu, wd = _s5(
        u5, zeros_st, zeros_st, ar, ai, bblk, cblk, dsk, wglu, bglu, tl=128, batch_major=True,
        cast=(w_out[0], w_ffn_gate[0], w_ffn_up[0], w_ffn_down[0]))
    y_prompt = _ffn(x_prompt, y_ssd, y_s5, mod, modf, n2g, nfg, wo, wg, wu, wd, seq_major=False, **seqs_p)

    rows_s = dseq * bs
    steps = lambda a: a.reshape(dseq, bs, a.shape[-1])
    flat = lambda a: a.reshape(1, rows_s, a.shape[-1])
    z, xbc, u5, dt = _inproj(x_sample, mod, n1g, w_in_p, tm=rows_s, sub=rows_s // 2, seq_major=True, **seqs_s)
    y_ssd, ssd_s, conv_s = _ssd_sample(steps(xbc), steps(z), steps(dt), state_conv[0].transpose(1, 0, 2),
                                       state_ssd[0], cw, cb, dtb, alog, dexp, ng)
    y_s5, re_s, im_s = _s5(steps(u5), state_s5_re[0].reshape(bs, nstate), state_s5_im[0].reshape(bs, nstate),
                           ar, ai, bblk, cblk, dsk, wglu, bglu, tl=dseq, batch_major=False)
    y_sample = _ffn(x_sample, flat(y_ssd), flat(y_s5), mod, modf, n2g, nfg, wo, wg, wu, wd, seq_major=True,
                    **seqs_s)

    g5 = (S5_GROUPS, S5_STATE)
    return (y_prompt, y_sample,
            ssd_p[None], ssd_s.reshape((1,) + state_ssd.shape[1:]),
            conv_p[None], conv_s.transpose(1, 0, 2)[None],
            re_p.reshape((1, bp) + g5), re_s.reshape((1, bs) + g5),
            im_p.reshape((1, bp) + g5), im_s.reshape((1, bs) + g5))
```

```python
import functools

import jax
import jax.numpy as jnp
from jax import lax
from jax.experimental import pallas as pl
from jax.experimental.pallas import tpu as pltpu

f32 = jnp.float32
bf16 = jnp.bfloat16

D_MODEL = 1024
D_SSD = 512
HEAD_DIM = 64
HEADS = 8
GROUPS = 2
HEADS_PER_GROUP = HEADS // GROUPS
STATE = 128
CONV_W = 4
CONV_DIM = D_SSD + 2 * GROUPS * STATE
D_S5 = 512
S5_CH = 16
S5_GROUPS = 32
S5_STATE = 64
D_FF = 2816
N_ADA = 6
EPS = 1e-6

LANES = 128
SUBLANES = 8
SSD_CHUNK = 128
S5_SLABS = D_S5 // LANES
S5_SLAB_STATE = (S5_GROUPS // S5_SLABS) * S5_STATE
DT_PAD = LANES
VMEM_LIMIT = 56 * 1024 * 1024

NT_DIMS = (((1,), (1,)), ((), ()))
TN_DIMS = (((0,), (0,)), ((), ()))


def _silu(x):
    return x * jax.nn.sigmoid(x)


def _interleave(*gens):
    live = list(gens)
    while live:
        for g in list(live):
            try:
                next(g)
            except StopIteration:
                live.remove(g)


def _by_row(fn, v, *ms):
    r, d = v.shape
    m_rows = ms[0].shape[0]
    if m_rows in (1, r):
        return fn(v, *ms)
    out = fn(v.reshape(r // m_rows, m_rows, d), *[m[None] for m in ms])
    return out.reshape(r, d)


def _rms_mod(x, g, sc, sh):
    y = x * lax.rsqrt(jnp.mean(x * x, axis=-1, keepdims=True) + EPS)
    return _by_row(lambda v, s, t: v * (1.0 + s) + t, y * g, sc, sh)


def _mod_row(mod_ref, i, per_row):
    return mod_ref[i] if per_row else mod_ref[i, pl.ds(pl.program_id(0), 1), :]


def _mod_spec(mod, n_seq, first_seq):
    assert first_seq % n_seq == 0
    return pl.BlockSpec((mod.shape[0], n_seq, mod.shape[2]), lambda *_: (0, first_seq // n_seq, 0))


def _const_spec(shape):
    nd = len(shape)
    return pl.BlockSpec(shape, lambda *_: (0,) * nd)


def _params(*sem):
    return pltpu.CompilerParams(dimension_semantics=sem, vmem_limit_bytes=VMEM_LIMIT)


ADA_K_BLOCK = 256


def _ada_kernel(c_ref, w_ref, b_ref, o_ref):
    n, _, d = o_ref.shape
    s = _silu(c_ref[...]).astype(bf16)

    @pl.when(pl.program_id(0) == 0)
    def _():
        for j in range(n):
            o_ref[j] = jnp.broadcast_to(b_ref[:, j * d:(j + 1) * d], o_ref.shape[1:])

    for j in range(n):
        o_ref[j] += jnp.dot(s, w_ref[:, j * d:(j + 1) * d].astype(bf16), preferred_element_type=f32)


def _ada_mod(c, w, b):
    m, k = c.shape
    n = w.shape[1] // k
    return pl.pallas_call(
        _ada_kernel,
        grid=(k // ADA_K_BLOCK,),
        in_specs=[pl.BlockSpec((m, ADA_K_BLOCK), lambda i: (0, i)),
                  pl.BlockSpec((ADA_K_BLOCK, n * k), lambda i: (i, 0)),
                  _const_spec((1, n * k))],
        out_specs=_const_spec((n, m, k)),
        out_shape=jax.ShapeDtypeStruct((n, m, k), f32),
        compiler_params=_params("arbitrary"),
        name="ada_mod",
    )(c, w, b.reshape(1, n * k))


def _rows_in(x_ref, seq_major):
    if not seq_major:
        return x_ref[0]
    s, t, d = x_ref.shape
    return jnp.swapaxes(x_ref[...], 0, 1).reshape(s * t, d)


def _conv_silu(ext, cw_ref, cb_ref):
    w = [cw_ref[k:k + 1, :] for k in range(CONV_W)]
    back1 = pltpu.roll(ext, 1, 0)
    older = pltpu.roll(ext * w[1] + back1 * w[0], 2, 0)
    conv = cb_ref[...] + ext[SUBLANES:, :] * w[3] + back1[SUBLANES:, :] * w[2] + older[SUBLANES:, :]
    return _silu(conv)


def _inproj_stages(x, sh, sc, g, outs, after=()):
    u = _rms_mod(x, g, sc, sh).astype(bf16)
    yield
    for write, w16 in outs:
        write(lax.dot_general(u, w16, NT_DIMS, preferred_element_type=f32))
        yield
    for stage in after:
        stage()
        yield


def _inproj_kernel(x_ref, mod_ref, g_ref, w_ref, *rest, seq_major, sub, conv):
    if conv:
        cw_ref, cb_ref, z_ref, act_ref, u5_ref, dt_ref, cn_ref, ext_ref = rest
        pl.when(pl.program_id(1) == 0)(lambda: ext_ref.__setitem__(
            (slice(0, SUBLANES), slice(None)), jnp.zeros((SUBLANES, CONV_DIM), f32)))
    else:
        z_ref, xbc_ref, u5_ref, dt_ref = rest
    sh = _mod_row(mod_ref, 0, seq_major)
    sc = _mod_row(mod_ref, 1, seq_major)
    x = _rows_in(x_ref, seq_major)
    rows = x.shape[0]
    T = SSD_CHUNK
    o_dt = D_SSD + CONV_DIM
    w16 = lambda o, width: w_ref[o:o + width, :].astype(bf16)
    w_z, w_xbc, w_u5, w_dt = w16(0, D_SSD), w16(D_SSD, CONV_DIM), w16(o_dt + HEADS, D_S5), w16(o_dt, DT_PAD)

    def stages(r0):
        def to(ref):
            def write(v):
                ref[0, r0:r0 + sub, :] = v
            return write

        def to_ext(v):
            ext_ref[SUBLANES + r0:SUBLANES + r0 + sub, :] = v

        def conv_chunk(c0):
            def stage():
                act_ref[0, c0:c0 + T, :] = _conv_silu(ext_ref[c0:c0 + SUBLANES + T, :], cw_ref, cb_ref)
            return stage

        outs = [(to(z_ref), w_z), (to_ext if conv else to(xbc_ref), w_xbc), (to(u5_ref), w_u5), (to(dt_ref), w_dt)]
        after = [conv_chunk(c0) for c0 in range(r0, r0 + sub, T)] if conv else ()
        return _inproj_stages(x[r0:r0 + sub], sh, sc, g_ref[...], outs, after)

    _interleave(*[stages(r0) for r0 in range(0, rows, sub)])
    if conv:
        cn_ref[0] = ext_ref[SUBLANES + rows - (CONV_W - 1):SUBLANES + rows, :]
        ext_ref[0:SUBLANES, :] = ext_ref[rows:rows + SUBLANES, :]


def _inproj(x, mod, g, w, conv_wb=None, *, tm, sub, seq_major, n_seq, first_seq):
    if seq_major:
        x_spec = _const_spec(x.shape)
        nb, rows, d = 1, x.shape[0] * x.shape[1], x.shape[2]
        assert tm == rows
    else:
        nb, rows, d = x.shape
        x_spec = pl.BlockSpec((1, tm, d), lambda i, j: (i, j, 0))
    nt = rows // tm
    widths = (D_SSD, CONV_DIM, D_S5, DT_PAD)
    conv = conv_wb is not None
    in_specs = [x_spec, _mod_spec(mod, n_seq, first_seq), _const_spec((1, d)),
                pl.BlockSpec(w.shape, lambda i, j: (0, 0), pipeline_mode=pl.Buffered(1))]
    out_specs = [pl.BlockSpec((1, tm, wd), lambda i, j: (i, j, 0)) for wd in widths]
    out_shape = [jax.ShapeDtypeStruct((nb, rows, wd), f32) for wd in widths]
    scratch = []
    if conv:
        in_specs += [_const_spec((CONV_W, CONV_DIM)), _const_spec((1, CONV_DIM))]
        out_specs.append(pl.BlockSpec((1, CONV_W - 1, CONV_DIM), lambda i, j: (i, 0, 0)))
        out_shape.append(jax.ShapeDtypeStruct((nb, CONV_W - 1, CONV_DIM), f32))
        scratch.append(pltpu.VMEM((SUBLANES + tm, CONV_DIM), f32))
    return pl.pallas_call(
        functools.partial(_inproj_kernel, seq_major=seq_major, sub=sub, conv=conv),
        grid=(nb, nt),
        in_specs=in_specs, out_specs=out_specs, out_shape=out_shape, scratch_shapes=scratch,
        compiler_params=_params("parallel", "arbitrary" if conv else "parallel"),
        name="in_proj",
    )(x, mod, g, w, *(conv_wb or ()))


def _split3(x):
    hi = x.astype(bf16)
    r1 = x - hi.astype(f32)
    mid = r1.astype(bf16)
    lo = (r1 - mid.astype(f32)).astype(bf16)
    return hi, mid, lo


def _dot_sel_lhs(sel16, x):
    return sum(jnp.dot(sel16, p, preferred_element_type=f32) for p in _split3(x))


def _dot_sel_rhs(x, sel16):
    hi, mid, _ = _split3(x)
    return jnp.dot(hi, sel16, preferred_element_type=f32) + jnp.dot(mid, sel16, preferred_element_type=f32)


def _gated_group_norm(y, z, ng):
    y = y * _silu(z)
    gw = D_SSD // GROUPS
    parts = []
    for g in range(GROUPS):
        yg = y[:, g * gw:(g + 1) * gw]
        parts.append(yg * lax.rsqrt(jnp.mean(yg * yg, axis=-1, keepdims=True) + EPS))
    return jnp.concatenate(parts, axis=-1) * ng


def _ssd_prompt_kernel(act_ref, z_ref, dt_ref, dtb_ref, alog_ref, dexp_ref, ng_ref, y_ref, st_ref, h_ref, *, cps):
    T = SSD_CHUNK
    gw = HEADS_PER_GROUP * HEAD_DIM

    @pl.when(pl.program_id(1) == 0)
    def _():
        h_ref[...] = jnp.zeros_like(h_ref)

    row = lax.broadcasted_iota(jnp.int32, (T, T), 0)
    col = lax.broadcasted_iota(jnp.int32, (T, T), 1)
    tri = row >= col
    tri16 = jnp.where(tri, 1.0, 0.0).astype(bf16)
    low_half = lax.broadcasted_iota(jnp.int32, (T, LANES), 1) < HEAD_DIM
    a_neg = -jnp.exp(alog_ref[...])

    lane_head = lax.broadcasted_iota(jnp.int32, (DT_PAD, D_SSD), 1) // HEAD_DIM
    sel16 = jnp.where(lax.broadcasted_iota(jnp.int32, (DT_PAD, D_SSD), 0) == lane_head, 1.0, 0.0).astype(bf16)

    def spread(q):
        return _dot_sel_rhs(q, sel16)

    def state_free(ci):
        r0 = ci * T
        act = act_ref[0, r0:r0 + T, :]
        xs = act[:, :D_SSD]
        bm = act[:, D_SSD:D_SSD + GROUPS * STATE]
        cm = act[:, D_SSD + GROUPS * STATE:]
        dtv = jax.nn.softplus(dt_ref[0, r0:r0 + T, :] + dtb_ref[...])
        a_cs = _dot_sel_lhs(tri16, dtv * a_neg)
        bg16s = [bm[:, g * STATE:(g + 1) * STATE].astype(bf16) for g in range(GROUPS)]
        cg16s = [cm[:, g * STATE:(g + 1) * STATE].astype(bf16) for g in range(GROUPS)]
        gmats = [lax.dot_general(cg16s[g], bg16s[g], NT_DIMS, preferred_element_type=f32)
                 for g in range(GROUPS)]
        yield
        a_last = a_cs[T - 1:T, :]
        a_cs_t = a_cs.T
        rep = spread(jnp.concatenate([dtv, jnp.exp(a_last - a_cs), jnp.exp(a_cs)], axis=0))
        yield
        x = xs * rep[:T]
        x16 = x.astype(bf16)
        xd16 = (x * rep[T:2 * T]).astype(bf16)
        e_cs = rep[2 * T:]
        yield
        y_part = []
        for g in range(GROUPS):
            for pr in range(HEADS_PER_GROUP // 2):
                s_pair = []
                for q in range(2):
                    h = g * HEADS_PER_GROUP + 2 * pr + q
                    seg = a_cs[:, h:h + 1] - a_cs_t[h:h + 1, :]
                    lmat = jnp.exp(jnp.where(tri, seg, -jnp.inf))
                    s_pair.append((gmats[g] * lmat).astype(bf16))
                lanes = slice(g * gw + pr * LANES, g * gw + (pr + 1) * LANES)
                both = jnp.dot(jnp.concatenate(s_pair, axis=0), x16[:, lanes], preferred_element_type=f32)
                yield
                y_part.append(jnp.where(low_half, both[:T], both[T:]))
        y_free = jnp.concatenate(y_part, axis=-1) + xs * dexp_ref[...]
        return y_free, e_cs, jnp.exp(a_last), xd16, bg16s, cg16s

    def state_step(ci, y_free, e_cs, e_last, xd16, bg16s, cg16s):
        r0 = ci * T
        y_offs = []
        for g in range(GROUPS):
            hp = h_ref[g]
            y_offs.append(lax.dot_general(cg16s[g], hp.astype(bf16), NT_DIMS, preferred_element_type=f32))
            upd = lax.dot_general(xd16[:, g * gw:(g + 1) * gw], bg16s[g], TN_DIMS, preferred_element_type=f32)
            for hh in range(HEADS_PER_GROUP):
                h = g * HEADS_PER_GROUP + hh
                rows = slice(hh * HEAD_DIM, (hh + 1) * HEAD_DIM)
                h_ref[g, rows, :] = e_last[:, h:h + 1] * hp[rows] + upd[rows]
        y = y_free + jnp.concatenate(y_offs, axis=-1) * e_cs
        y_ref[0, r0:r0 + T, :] = _gated_group_norm(y, z_ref[0, r0:r0 + T, :], ng_ref[...])

    gens = [state_free(ci) for ci in range(cps)]
    free = [None] * cps
    while any(f is None for f in free):
        for ci in range(cps):
            if free[ci] is None:
                try:
                    next(gens[ci])
                except StopIteration as done:
                    free[ci] = done.value
    for ci in range(cps):
        state_step(ci, *free[ci])

    @pl.when(pl.program_id(1) == pl.num_programs(1) - 1)
    def _():
        st_ref[0] = h_ref[...].reshape(HEADS, HEAD_DIM, STATE)


def _ssd_prompt(act, z, dt, dtb, alog, dexp, ng, *, cps=8):
    nb, L, _ = act.shape
    rows = cps * SSD_CHUNK
    blk = lambda wd: pl.BlockSpec((1, rows, wd), lambda b, c: (b, c, 0))
    return pl.pallas_call(
        functools.partial(_ssd_prompt_kernel, cps=cps),
        grid=(nb, L // rows),
        in_specs=[blk(CONV_DIM), blk(D_SSD), blk(DT_PAD), _const_spec((1, DT_PAD)), _const_spec((1, DT_PAD)),
                  _const_spec((1, D_SSD)), _const_spec((1, D_SSD))],
        out_specs=[blk(D_SSD), pl.BlockSpec((1, HEADS, HEAD_DIM, STATE), lambda b, c: (b, 0, 0, 0))],
        out_shape=[jax.ShapeDtypeStruct((nb, L, D_SSD), f32),
                   jax.ShapeDtypeStruct((nb, HEADS, HEAD_DIM, STATE), f32)],
        scratch_shapes=[pltpu.VMEM((GROUPS, HEADS_PER_GROUP * HEAD_DIM, STATE), f32)],
        compiler_params=_params("parallel", "arbitrary"),
        name="ssd_prompt",
    )(act, z, dt, dtb, alog, dexp, ng)


def _ssd_sample_kernel(xbc_ref, z_ref, dt_ref, cbuf_ref, st_ref, cw_ref, cb_ref, dtb_ref, alog_ref,
                       dexp_ref, ng_ref, y_ref, stn_ref, cn_ref, dtot_ref, *, L, nb):
    gw = HEADS_PER_GROUP * HEAD_DIM
    full = [cbuf_ref[j] for j in range(CONV_W - 1)] + [xbc_ref[t] for t in range(L)]
    for j in range(CONV_W - 1):
        cn_ref[j] = full[L + j]
    a_neg = -jnp.exp(alog_ref[...])
    xs, bm, cm, dtv, acs = [], [], [], [], []
    run = None
    for t in range(L):
        conv = cb_ref[...]
        for k in range(CONV_W):
            conv = conv + full[t + k] * cw_ref[k:k + 1, :]
        act = _silu(conv)
        xs.append(act[:, :D_SSD])
        bm.append(act[:, D_SSD:D_SSD + GROUPS * STATE])
        cm.append(act[:, D_SSD + GROUPS * STATE:])
        d = jax.nn.softplus(dt_ref[t] + dtb_ref[...])
        dtv.append(d)
        run = d * a_neg if run is None else run + d * a_neg
        acs.append(run)
    a_tot = acs[L - 1]
    dtot_ref[...] = jnp.exp(a_tot)

    lane_head = lax.broadcasted_iota(jnp.int32, (DT_PAD, D_SSD), 1) // HEAD_DIM
    sel16 = jnp.where(lax.broadcasted_iota(jnp.int32, (DT_PAD, D_SSD), 0) == lane_head, 1.0, 0.0).astype(bf16)
    pairs = [(t, s) for t in range(L) for s in range(t)]
    factors = (dtv + [jnp.exp(a_tot - acs[t]) for t in range(L)] + [jnp.exp(acs[t]) for t in range(L)]
               + [jnp.exp(acs[t] - acs[s]) for t, s in pairs])
    rep = _dot_sel_rhs(jnp.concatenate(factors, axis=0), sel16)
    piece = lambda i: rep[i * nb:(i + 1) * nb]
    x = [xs[t] * piece(t) for t in range(L)]
    xd_stack = jnp.concatenate([x[t] * piece(L + t) for t in range(L)], axis=0)
    e_cs = [piece(2 * L + t) for t in range(L)]
    decay = {ts: piece(3 * L + i) for i, ts in enumerate(pairs)}

    in_group0 = lax.broadcasted_iota(jnp.int32, (nb, D_SSD), 1) < gw
    y_intra = []
    for t in range(L):
        acc = None
        for s in range(t + 1):
            cb_dot = [jnp.sum(cm[t][:, g * STATE:(g + 1) * STATE] * bm[s][:, g * STATE:(g + 1) * STATE],
                              axis=-1, keepdims=True) for g in range(GROUPS)]
            w = jnp.where(in_group0, cb_dot[0], cb_dot[1])
            term = w * x[s] if s == t else w * decay[(t, s)] * x[s]
            acc = term if acc is None else acc + term
        y_intra.append(acc)

    c_stack = [jnp.concatenate([cm[t][:, g * STATE:(g + 1) * STATE] for t in range(L)], axis=0).astype(bf16)
               for g in range(GROUPS)]
    b_stack = [jnp.concatenate([bm[t][:, g * STATE:(g + 1) * STATE] for t in range(L)], axis=0).astype(bf16)
               for g in range(GROUPS)]
    seq_of_row = lax.broadcasted_iota(jnp.int32, (L * nb, gw), 0) & (nb - 1)

    def per_seq(b, yoff):
        mine = seq_of_row == b
        drow = dtot_ref[pl.ds(b, 1), :]
        out = []
        for g in range(GROUPS):
            h0 = st_ref[b, g]
            r = lax.dot_general(c_stack[g], h0.astype(bf16), NT_DIMS, preferred_element_type=f32)
            out.append(yoff[g] + jnp.where(mine, r, 0.0))
            xm = jnp.where(mine, xd_stack[:, g * gw:(g + 1) * gw], 0.0).astype(bf16)
            upd = lax.dot_general(xm, b_stack[g], TN_DIMS, preferred_element_type=f32)
            for hh in range(HEADS_PER_GROUP):
                h = g * HEADS_PER_GROUP + hh
                rows = slice(hh * HEAD_DIM, (hh + 1) * HEAD_DIM)
                stn_ref[b, g, rows, :] = drow[:, h:h + 1] * h0[rows] + upd[rows]
        return tuple(out)

    yoff = lax.fori_loop(0, nb, per_seq, tuple(jnp.zeros((L * nb, gw), f32) for _ in range(GROUPS)),
                         unroll=8)

    for t in range(L):
        y_off = jnp.concatenate([yoff[g][t * nb:(t + 1) * nb] for g in range(GROUPS)], axis=-1)
        y = y_intra[t] + y_off * e_cs[t] + xs[t] * dexp_ref[...]
        y_ref[t] = _gated_group_norm(y, z_ref[t], ng_ref[...])


def _ssd_sample(xbc, z, dt, cbuf, st, cw, cb, dtb, alog, dexp, ng, *, nb=16):
    L, B, _ = xbc.shape
    assert nb & (nb - 1) == 0 and B % nb == 0
    tblk = lambda n, wd: pl.BlockSpec((n, nb, wd), lambda i: (0, i, 0))
    gw = HEADS_PER_GROUP * HEAD_DIM
    st = st.reshape(B, GROUPS, gw, STATE)
    st_spec = pl.BlockSpec((nb, GROUPS, gw, STATE), lambda i: (i, 0, 0, 0))
    return pl.pallas_call(
        functools.partial(_ssd_sample_kernel, L=L, nb=nb),
        grid=(B // nb,),
        in_specs=[tblk(L, CONV_DIM), tblk(L, D_SSD), tblk(L, DT_PAD), tblk(CONV_W - 1, CONV_DIM), st_spec,
                  _const_spec((CONV_W, CONV_DIM)), _const_spec((1, CONV_DIM)),
                  _const_spec((1, DT_PAD)), _const_spec((1, DT_PAD)),
                  _const_spec((1, D_SSD)), _const_spec((1, D_SSD))],
        out_specs=[tblk(L, D_SSD), st_spec, tblk(CONV_W - 1, CONV_DIM)],
        out_shape=[jax.ShapeDtypeStruct((L, B, D_SSD), f32),
                   jax.ShapeDtypeStruct((B, GROUPS, gw, STATE), f32),
                   jax.ShapeDtypeStruct((CONV_W - 1, B, CONV_DIM), f32)],
        scratch_shapes=[pltpu.VMEM((nb, DT_PAD), f32)],
        compiler_params=_params("parallel"),
        name="ssd_sample",
    )(xbc, z, dt, cbuf, st, cw, cb, dtb, alog, dexp, ng)


def _s5_param_kernel(lr_ref, li_ref, ls_ref, br_ref, bi_ref, cr_ref, ci_ref,
                     ar_ref, ai_ref, bblk_ref, cblk_ref, b_scr, c_scr):
    lr, li = lr_ref[...], li_ref[...]
    step = jnp.exp(ls_ref[...])
    mag = jnp.exp(lr * step)
    abr = mag * jnp.cos(li * step)
    abi = mag * jnp.sin(li * step)
    nr, ni = abr - 1.0, abi
    den = lr * lr + li * li
    fr = (nr * lr + ni * li) / den
    fi = (ni * lr - nr * li) / den
    br, bi = br_ref[...], bi_ref[...]
    bbr = fr * br - fi * bi
    bbi = fr * bi + fi * br
    b_scr[...] = jnp.zeros_like(b_scr)
    c_scr[...] = jnp.zeros_like(c_scr)
    gps = S5_GROUPS // S5_SLABS
    ns = S5_SLAB_STATE
    for g in range(S5_GROUPS):
        s, gl = divmod(g, gps)
        ch = slice(gl * S5_CH, (gl + 1) * S5_CH)
        st = slice(gl * S5_STATE, (gl + 1) * S5_STATE)
        st_im = slice(ns + gl * S5_STATE, ns + (gl + 1) * S5_STATE)
        ar_ref[s, :, st] = jnp.broadcast_to(abr[g], (SUBLANES, S5_STATE))
        ai_ref[s, :, st] = jnp.broadcast_to(abi[g], (SUBLANES, S5_STATE))
        b_scr[s, ch, st] = bbr[g]
        b_scr[s, ch, st_im] = bbi[g]
        c_scr[s, st, ch] = cr_ref[g].T
        c_scr[s, st_im, ch] = -ci_ref[g].T
    bblk_ref[...] = b_scr[...].astype(bf16)
    cblk_ref[...] = c_scr[...].astype(bf16)


def _s5_params(a_re, a_im, log_step, b_re_t, b_im_t, c_re, c_im):
    g, p = a_re.shape
    ns = S5_SLAB_STATE
    return pl.pallas_call(
        _s5_param_kernel,
        out_shape=[jax.ShapeDtypeStruct((S5_SLABS, SUBLANES, ns), f32),
                   jax.ShapeDtypeStruct((S5_SLABS, SUBLANES, ns), f32),
                   jax.ShapeDtypeStruct((S5_SLABS, LANES, 2 * ns), bf16),
                   jax.ShapeDtypeStruct((S5_SLABS, 2 * ns, LANES), bf16)],
        scratch_shapes=[pltpu.VMEM((S5_SLABS, LANES, 2 * ns), f32),
                        pltpu.VMEM((S5_SLABS, 2 * ns, LANES), f32)],
        name="s5_params",
    )(a_re.reshape(g, 1, p), a_im.reshape(g, 1, p), log_step.reshape(g, 1, 1), b_re_t, b_im_t, c_re, c_im)


def _gelu(x):
    return 0.5 * x * (1.0 + lax.erf(x * (2.0 ** -0.5)))


def _s5_slabs_interleaved(ut_ref, hbuf_ref, hst_ref, g_ref, ar_ref, ai_ref, bblk_ref, cblk_ref, dsk_ref, tl,
                          row_block=256, n_stages=8):
    ns = S5_SLAB_STATE
    nb = SUBLANES
    rows = nb * tl

    def lanes(s):
        return slice(s * LANES, (s + 1) * LANES)

    def bu_stage(slabs):
        for s in slabs:
            for r in range(0, rows, row_block):
                us = ut_ref[s, r:r + row_block, :]
                hbuf_ref[s, nb + r:nb + r + row_block, :] = jnp.dot(us.astype(bf16), bblk_ref[s],
                                                                    preferred_element_type=f32)
                yield

    def scan_stage(slabs):
        carry = [(hst_ref[s, :, :ns], hst_ref[s, :, ns:]) for s in slabs]
        coef = [(ar_ref[s], ai_ref[s]) for s in slabs]
        for t in range(tl):
            r = nb + nb * t
            for i, s in enumerate(slabs):
                (pr, pi), (ar, ai) = carry[i], coef[i]
                nr = ar * pr - ai * pi + hbuf_ref[s, r:r + nb, :ns]
                ni = ar * pi + ai * pr + hbuf_ref[s, r:r + nb, ns:]
                hbuf_ref[s, r:r + nb, :ns] = nr
                hbuf_ref[s, r:r + nb, ns:] = ni
                carry[i] = (nr, ni)
            if (t + 1) % (tl // n_stages) == 0:
                yield
        for i, s in enumerate(slabs):
            hst_ref[s, :, :ns] = carry[i][0]
            hst_ref[s, :, ns:] = carry[i][1]

    def y_stage(slabs):
        for s in slabs:
            for r in range(0, rows, row_block):
                h16 = hbuf_ref[s, nb + r:nb + r + row_block, :].astype(bf16)
                ys = jnp.dot(h16, cblk_ref[s], preferred_element_type=f32)
                ys = ys + dsk_ref[:, lanes(s)] * ut_ref[s, r:r + row_block, :]
                g_ref[s, r:r + row_block, :] = _gelu(ys)
                yield

    half = S5_SLABS // 2
    first, second = tuple(range(half)), tuple(range(half, S5_SLABS))
    _interleave(bu_stage(first))
    _interleave(scan_stage(first), bu_stage(second))
    _interleave(scan_stage(second), y_stage(first))
    _interleave(y_stage(second))


def _s5_scan_vmem(hbuf_ref, hst_ref, ar_ref, ai_ref, s, nb, rows):
    ns = S5_SLAB_STATE
    hbuf_ref[s, 0:nb, :] = hst_ref[s]
    ar, ai = ar_ref[s], ai_ref[s]

    def body(i, carry):
        j = pl.multiple_of(nb + i * SUBLANES, SUBLANES)
        prev = hbuf_ref[s, pl.ds(j - nb, SUBLANES), :]
        cur = hbuf_ref[s, pl.ds(j, SUBLANES), :]
        pr, pi = prev[:, :ns], prev[:, ns:]
        hbuf_ref[s, pl.ds(j, SUBLANES), :ns] = ar * pr - ai * pi + cur[:, :ns]
        hbuf_ref[s, pl.ds(j, SUBLANES), ns:] = ar * pi + ai * pr + cur[:, ns:]
        return carry

    lax.fori_loop(0, rows // SUBLANES, body, 0, unroll=min(8, nb // SUBLANES))
    hst_ref[s] = hbuf_ref[s, rows:rows + nb, :]


S5_N_IN = 10


def _s5_kernel(*refs, nb, tl, batch_major, n_cast):
    (u_ref, re0_ref, im0_ref, ar_ref, ai_ref, bblk_ref, cblk_ref, dsk_ref, wglu_ref,
     bglu_ref) = refs[:S5_N_IN]
    cast_in = refs[S5_N_IN:S5_N_IN + n_cast]
    y_ref, ren_ref, imn_ref = refs[S5_N_IN + n_cast:S5_N_IN + n_cast + 3]
    cast_out = refs[S5_N_IN + n_cast + 3:S5_N_IN + 2 * n_cast + 3]
    ut_ref, hbuf_ref, hst_ref, g_ref = refs[S5_N_IN + 2 * n_cast + 3:]
    for w_ref, w16_ref in zip(cast_in, cast_out):
        w16_ref[...] = w_ref[...].astype(bf16)
    step = pl.program_id(0)
    rows = nb * tl
    ns = S5_SLAB_STATE

    @pl.when(step == 0)
    def _():
        for s in range(S5_SLABS):
            hst_ref[s, :, :ns] = re0_ref[:, s * ns:(s + 1) * ns]
            hst_ref[s, :, ns:] = im0_ref[:, s * ns:(s + 1) * ns]

    for s in range(S5_SLABS):
        sl = slice(s * LANES, (s + 1) * LANES)
        if batch_major:
            for b in range(nb):
                ut_ref[s, pl.ds(b, tl, stride=nb), :] = u_ref[b, :, sl]
        else:
            ut_ref[s] = u_ref[:, :, sl].reshape(rows, LANES)

    if nb == SUBLANES:
        _s5_slabs_interleaved(ut_ref, hbuf_ref, hst_ref, g_ref, ar_ref, ai_ref, bblk_ref, cblk_ref, dsk_ref, tl)
    else:
        for s in range(S5_SLABS):
            sl = slice(s * LANES, (s + 1) * LANES)
            hbuf_ref[s, nb:nb + rows, :] = jnp.dot(ut_ref[s].astype(bf16), bblk_ref[s],
                                                   preferred_element_type=f32)
            _s5_scan_vmem(hbuf_ref, hst_ref, ar_ref, ai_ref, s, nb, rows)
            ys = jnp.dot(hbuf_ref[s, nb:nb + rows, :].astype(bf16), cblk_ref[s], preferred_element_type=f32)
            g_ref[s] = _gelu(ys + dsk_ref[:, sl] * ut_ref[s])

    g = jnp.concatenate([g_ref[s] for s in range(S5_SLABS)], axis=-1)
    gate = jnp.dot(g.astype(bf16), wglu_ref[...], preferred_element_type=f32) + bglu_ref[...]
    out = g * jax.nn.sigmoid(gate)
    if batch_major:
        for s in range(S5_SLABS):
            sl = slice(s * LANES, (s + 1) * LANES)
            ut_ref[s] = out[:, sl]
            for b in range(nb):
                y_ref[b, :, sl] = ut_ref[s, pl.ds(b, tl, stride=nb), :]
    else:
        y_ref[...] = out.reshape(tl, nb, D_S5)

    @pl.when(step == pl.num_programs(0) - 1)
    def _():
        for s in range(S5_SLABS):
            ren_ref[:, s * ns:(s + 1) * ns] = hst_ref[s, :, :ns]
            imn_ref[:, s * ns:(s + 1) * ns] = hst_ref[s, :, ns:]


def _s5(u, re0, im0, ar, ai, bblk, cblk, dsk, wglu, bglu, *, tl, batch_major, cast=()):
    if batch_major:
        nb, L, _ = u.shape
        u_spec = pl.BlockSpec((nb, tl, D_S5), lambda i: (0, i, 0))
    else:
        L, nb, _ = u.shape
        u_spec = pl.BlockSpec((tl, nb, D_S5), lambda i: (i, 0, 0))
    rows = nb * tl
    steps = L // tl
    nstate = S5_GROUPS * S5_STATE
    st_spec = _const_spec((nb, nstate))
    cast_specs = [pl.BlockSpec((w.shape[0] // steps, w.shape[1]), lambda i: (i, 0)) for w in cast]
    assert all(w.shape[0] % (steps * 2 * SUBLANES) == 0 for w in cast)
    return pl.pallas_call(
        functools.partial(_s5_kernel, nb=nb, tl=tl, batch_major=batch_major, n_cast=len(cast)),
        grid=(steps,),
        in_specs=[u_spec, st_spec, st_spec,
                  _const_spec((S5_SLABS, SUBLANES, S5_SLAB_STATE)),
                  _const_spec((S5_SLABS, SUBLANES, S5_SLAB_STATE)),
                  _const_spec((S5_SLABS, LANES, 2 * S5_SLAB_STATE)),
                  _const_spec((S5_SLABS, 2 * S5_SLAB_STATE, LANES)),
                  _const_spec((1, D_S5)), _const_spec((D_S5, D_S5)), _const_spec((1, D_S5))] + cast_specs,
        out_specs=[u_spec, st_spec, st_spec] + cast_specs,
        out_shape=[jax.ShapeDtypeStruct(u.shape, f32),
                   jax.ShapeDtypeStruct((nb, nstate), f32),
                   jax.ShapeDtypeStruct((nb, nstate), f32)]
                  + [jax.ShapeDtypeStruct(w.shape, bf16) for w in cast],
        scratch_shapes=[pltpu.VMEM((S5_SLABS, rows, LANES), f32),
                        pltpu.VMEM((S5_SLABS, nb + rows, 2 * S5_SLAB_STATE), f32),
                        pltpu.VMEM((S5_SLABS, nb, 2 * S5_SLAB_STATE), f32),
                        pltpu.VMEM((S5_SLABS, rows, LANES), f32)],
        compiler_params=_params("arbitrary"),
        name="s5_mixer",
    )(u, re0, im0, ar, ai, bblk, cblk, dsk, wglu, bglu, *cast)


def _ffn_stages(x, ys, y5, mods, n2g, nfg, wo_ref, wg_ref, wu_ref, wd_ref, write_out, ff_chunk):
    g1, sh2, sc2, g2, shf, scf = mods
    att = jnp.dot(ys.astype(bf16), wo_ref[:D_SSD, :], preferred_element_type=f32)
    att = att + jnp.dot(y5.astype(bf16), wo_ref[D_SSD:, :], preferred_element_type=f32)
    yield
    x1 = x + _by_row(lambda v, s: v * s, att, g1)
    v = _rms_mod(x1, n2g, sc2, sh2).astype(bf16)
    ff = None
    for o in range(0, D_FF, ff_chunk):
        gate = jnp.dot(v, wg_ref[:, o:o + ff_chunk], preferred_element_type=f32)
        up = jnp.dot(v, wu_ref[:, o:o + ff_chunk], preferred_element_type=f32)
        hid = (_silu(gate) * up).astype(bf16)
        part = jnp.dot(hid, wd_ref[o:o + ff_chunk, :], preferred_element_type=f32)
        ff = part if ff is None else ff + part
        yield
    x2 = x1 + _by_row(lambda v, s: v * s, ff, g2)
    write_out(_rms_mod(x2, nfg, scf, shf))


def _ffn_kernel(x_ref, ys_ref, y5_ref, mod_ref, modf_ref, n2g_ref, nfg_ref, wo_ref, wg_ref, wu_ref, wd_ref,
                o_ref, *, seq_major, ff_chunk, sub):
    mods = (_mod_row(mod_ref, 2, seq_major), _mod_row(mod_ref, 3, seq_major), _mod_row(mod_ref, 4, seq_major),
            _mod_row(mod_ref, 5, seq_major), _mod_row(modf_ref, 0, seq_major), _mod_row(modf_ref, 1, seq_major))
    x = _rows_in(x_ref, seq_major)
    rows = x.shape[0]
    tiles = {}

    def stages(r0):
        def write_out(y):
            if seq_major:
                tiles[r0] = y
            else:
                o_ref[0, r0:r0 + sub, :] = y
        return _ffn_stages(x[r0:r0 + sub], ys_ref[0, r0:r0 + sub, :], y5_ref[0, r0:r0 + sub, :], mods,
                           n2g_ref[...], nfg_ref[...], wo_ref, wg_ref, wu_ref, wd_ref, write_out, ff_chunk)

    _interleave(*[stages(r0) for r0 in range(0, rows, sub)])
    if seq_major:
        s, t, d = o_ref.shape
        y = jnp.concatenate([tiles[r0] for r0 in range(0, rows, sub)], axis=0)
        o_ref[...] = jnp.swapaxes(y.reshape(t, s, d), 0, 1)


def _ffn(x, ys, y5, mod, modf, n2g, nfg, wo, wg, wu, wd, *, tm, sub, seq_major, n_seq, first_seq, ff_chunk=256):
    if seq_major:
        x_spec = _const_spec(x.shape)
        nb, rows, d = 1, x.shape[0] * x.shape[1], x.shape[2]
        assert tm == rows
    else:
        nb, rows, d = x.shape
        x_spec = pl.BlockSpec((1, tm, d), lambda i, j: (i, j, 0))
    nt = rows // tm
    mod_spec = _mod_spec(mod, n_seq, first_seq)
    modf_spec = _mod_spec(modf, n_seq, first_seq)
    blk = lambda wd_: pl.BlockSpec((1, tm, wd_), lambda i, j: (i, j, 0))
    single = dict(pipeline_mode=pl.Buffered(1))
    wspec = lambda shape: pl.BlockSpec(shape, lambda i, j: (0, 0), **single)
    return pl.pallas_call(
        functools.partial(_ffn_kernel, seq_major=seq_major, ff_chunk=ff_chunk, sub=sub),
        grid=(nb, nt),
        in_specs=[x_spec, blk(D_SSD), blk(D_S5), mod_spec, modf_spec,
                  _const_spec((1, d)), _const_spec((1, d)),
                  wspec((d, d)), wspec((d, D_FF)), wspec((d, D_FF)), wspec((D_FF, d))],
        out_specs=x_spec,
        out_shape=jax.ShapeDtypeStruct(x.shape, f32),
        compiler_params=_params("parallel", "parallel"),
        name="out_ffn",
    )(x, ys, y5, mod, modf, n2g, nfg, wo, wg, wu, wd)


def kernel(x_prompt, x_sample, c_prompt, c_sample, state_ssd, state_conv, state_s5_re, state_s5_im, w_ada, b_ada, norm1_g, w_in, conv_w, conv_b, ssd_dt_bias, ssd_A_log, ssd_D, ssd_norm_g, s5_A_re, s5_A_im, s5_log_step, s5_B_re, s5_B_im, s5_C_re, s5_C_im, s5_D, w_glu, b_glu, w_out, norm2_g, w_ffn_gate, w_ffn_up, w_ffn_down, w_ada_f, b_ada_f, normf_g):
    assert w_ada.shape[0] == 1, "single-layer stack"
    bp, seq, d = x_prompt.shape
    bs, dseq, _ = x_sample.shape

    c_all = jnp.concatenate([c_sample, c_prompt], axis=0)
    mod = _ada_mod(c_all, w_ada[0], b_ada[0])
    modf = _ada_mod(c_all, w_ada_f, b_ada_f)
    seqs_s = dict(n_seq=bs, first_seq=0)
    seqs_p = dict(n_seq=bp, first_seq=bs)

    w_in_p = w_in[0].T
    pad_h = lambda v: jnp.concatenate([v, jnp.zeros((DT_PAD - HEADS,), f32)]).reshape(1, DT_PAD)
    dtb = pad_h(ssd_dt_bias[0])
    alog = pad_h(ssd_A_log[0])
    dexp = jnp.repeat(ssd_D[0], HEAD_DIM).reshape(1, D_SSD)
    ng = ssd_norm_g[0].reshape(1, D_SSD)
    cw, cb = conv_w[0], conv_b[0].reshape(1, CONV_DIM)

    ar, ai, bblk, cblk = _s5_params(s5_A_re[0], s5_A_im[0], s5_log_step[0],
                                    s5_B_re[0].transpose(0, 2, 1), s5_B_im[0].transpose(0, 2, 1),
                                    s5_C_re[0], s5_C_im[0])
    dsk = s5_D[0].reshape(1, D_S5)
    wglu = w_glu[0].astype(bf16)
    bglu = b_glu[0].reshape(1, D_S5)
    n1g, n2g, nfg = norm1_g[0].reshape(1, d), norm2_g[0].reshape(1, d), normf_g.reshape(1, d)
    nstate = S5_GROUPS * S5_STATE

    z, act, u5, dt, conv_p = _inproj(x_prompt, mod, n1g, w_in_p, (cw, cb), tm=1024, sub=512, seq_major=False,
                                     **seqs_p)
    y_ssd, ssd_p = _ssd_prompt(act, z, dt, dtb, alog, dexp, ng)
    zeros_st = jnp.zeros((bp, nstate), f32)
    y_s5, re_p, im_p, wo, wg, wu, wd = _s5(
        u5, zeros_st, zeros_st, ar, ai, bblk, cblk, dsk, wglu, bglu, tl=128, batch_major=True,
        cast=(w_out[0], w_ffn_gate[0], w_ffn_up[0], w_ffn_down[0]))
    y_prompt = _ffn(x_prompt, y_ssd, y_s5, mod, modf, n2g, nfg, wo, wg, wu, wd, tm=1024, sub=512,
                    seq_major=False, **seqs_p)

    rows_s = dseq * bs
    steps = lambda a: a.reshape(dseq, bs, a.shape[-1])
    flat = lambda a: a.reshape(1, rows_s, a.shape[-1])
    z, xbc, u5, dt = _inproj(x_sample, mod, n1g, w_in_p, tm=rows_s, sub=rows_s // 2, seq_major=True, **seqs_s)
    y_ssd, ssd_s, conv_s = _ssd_sample(steps(xbc), steps(z), steps(dt), state_conv[0].transpose(1, 0, 2),
                                       state_ssd[0], cw, cb, dtb, alog, dexp, ng)
    y_s5, re_s, im_s = _s5(steps(u5), state_s5_re[0].reshape(bs, nstate), state_s5_im[0].reshape(bs, nstate),
                           ar, ai, bblk, cblk, dsk, wglu, bglu, tl=dseq, batch_major=False)
    y_sample = _ffn(x_sample, flat(y_ssd), flat(y_s5), mod, modf, n2g, nfg, wo, wg, wu, wd,
                    tm=rows_s, sub=rows_s, seq_major=True, **seqs_s)

    g5 = (S5_GROUPS, S5_STATE)
    return (y_prompt, y_sample,
            ssd_p[None], ssd_s.reshape((1,) + state_ssd.shape[1:]),
            conv_p[None], conv_s.transpose(1, 0, 2)[None],
            re_p.reshape((1, bp) + g5), re_s.reshape((1, bs) + g5),
            im_p.reshape((1, bp) + g5), im_s.reshape((1, bs) + g5))
```

```python
import functools

import jax
import jax.numpy as jnp
from jax import lax
from jax.experimental import pallas as pl
from jax.experimental.pallas import tpu as pltpu

f32 = jnp.float32
bf16 = jnp.bfloat16

D_MODEL = 1024
D_SSD = 512
HEAD_DIM = 64
HEADS = 8
GROUPS = 2
HEADS_PER_GROUP = HEADS // GROUPS
STATE = 128
CONV_W = 4
CONV_DIM = D_SSD + 2 * GROUPS * STATE
D_S5 = 512
S5_CH = 16
S5_GROUPS = 32
S5_STATE = 64
D_FF = 2816
N_ADA = 6
EPS = 1e-6

LANES = 128
SUBLANES = 8
SSD_CHUNK = 128
S5_SLABS = D_S5 // LANES
S5_SLAB_STATE = (S5_GROUPS // S5_SLABS) * S5_STATE
DT_PAD = LANES
VMEM_LIMIT = 56 * 1024 * 1024

NT_DIMS = (((1,), (1,)), ((), ()))
TN_DIMS = (((0,), (0,)), ((), ()))


def _silu(x):
    return x * jax.nn.sigmoid(x)


def _interleave(*gens):
    live = list(gens)
    while live:
        for g in list(live):
            try:
                next(g)
            except StopIteration:
                live.remove(g)


def _by_row(fn, v, *ms):
    r, d = v.shape
    m_rows = ms[0].shape[0]
    if m_rows in (1, r):
        return fn(v, *ms)
    out = fn(v.reshape(r // m_rows, m_rows, d), *[m[None] for m in ms])
    return out.reshape(r, d)


def _rms_mod(x, g, sc, sh):
    y = x * lax.rsqrt(jnp.mean(x * x, axis=-1, keepdims=True) + EPS)
    return _by_row(lambda v, s, t: v * (1.0 + s) + t, y * g, sc, sh)


def _mod_row(mod_ref, i, per_row):
    return mod_ref[i] if per_row else mod_ref[i, pl.ds(pl.program_id(0), 1), :]


def _mod_spec(mod, n_seq, first_seq):
    assert first_seq % n_seq == 0
    return pl.BlockSpec((mod.shape[0], n_seq, mod.shape[2]), lambda *_: (0, first_seq // n_seq, 0))


def _const_spec(shape):
    nd = len(shape)
    return pl.BlockSpec(shape, lambda *_: (0,) * nd)


def _params(*sem):
    return pltpu.CompilerParams(dimension_semantics=sem, vmem_limit_bytes=VMEM_LIMIT)


ADA_K_BLOCK = 256


def _ada_kernel(c_ref, w_ref, b_ref, o_ref):
    n, _, d = o_ref.shape
    s = _silu(c_ref[...]).astype(bf16)

    @pl.when(pl.program_id(0) == 0)
    def _():
        for j in range(n):
            o_ref[j] = jnp.broadcast_to(b_ref[:, j * d:(j + 1) * d], o_ref.shape[1:])

    for j in range(n):
        o_ref[j] += jnp.dot(s, w_ref[:, j * d:(j + 1) * d].astype(bf16), preferred_element_type=f32)


def _ada_mod(c, w, b):
    m, k = c.shape
    n = w.shape[1] // k
    return pl.pallas_call(
        _ada_kernel,
        grid=(k // ADA_K_BLOCK,),
        in_specs=[pl.BlockSpec((m, ADA_K_BLOCK), lambda i: (0, i)),
                  pl.BlockSpec((ADA_K_BLOCK, n * k), lambda i: (i, 0)),
                  _const_spec((1, n * k))],
        out_specs=_const_spec((n, m, k)),
        out_shape=jax.ShapeDtypeStruct((n, m, k), f32),
        compiler_params=_params("arbitrary"),
        name="ada_mod",
    )(c, w, b.reshape(1, n * k))


def _rows_in(x_ref, seq_major):
    if not seq_major:
        return x_ref[0]
    s, t, d = x_ref.shape
    return jnp.swapaxes(x_ref[...], 0, 1).reshape(s * t, d)


def _conv_silu(ext, cw_ref, cb_ref):
    w = [cw_ref[k:k + 1, :] for k in range(CONV_W)]
    back1 = pltpu.roll(ext, 1, 0)
    older = pltpu.roll(ext * w[1] + back1 * w[0], 2, 0)
    conv = cb_ref[...] + ext[SUBLANES:, :] * w[3] + back1[SUBLANES:, :] * w[2] + older[SUBLANES:, :]
    return _silu(conv)


def _inproj_stages(x, sh, sc, g, outs, after=()):
    u = _rms_mod(x, g, sc, sh).astype(bf16)
    yield
    for write, w16 in outs:
        write(lax.dot_general(u, w16, NT_DIMS, preferred_element_type=f32))
        yield
    for stage in after:
        stage()
        yield


def _inproj_kernel(x_ref, mod_ref, g_ref, w_ref, *rest, seq_major, sub, conv):
    if conv:
        cw_ref, cb_ref, z_ref, act_ref, u5_ref, dt_ref, cn_ref, ext_ref = rest
        pl.when(pl.program_id(1) == 0)(lambda: ext_ref.__setitem__(
            (slice(0, SUBLANES), slice(None)), jnp.zeros((SUBLANES, CONV_DIM), f32)))
    else:
        z_ref, xbc_ref, u5_ref, dt_ref = rest
    sh = _mod_row(mod_ref, 0, seq_major)
    sc = _mod_row(mod_ref, 1, seq_major)
    x = _rows_in(x_ref, seq_major)
    rows = x.shape[0]
    T = SSD_CHUNK
    o_dt = D_SSD + CONV_DIM
    w16 = lambda o, width: w_ref[o:o + width, :].astype(bf16)
    w_z, w_xbc, w_u5, w_dt = w16(0, D_SSD), w16(D_SSD, CONV_DIM), w16(o_dt + HEADS, D_S5), w16(o_dt, DT_PAD)

    def stages(r0):
        def to(ref):
            def write(v):
                ref[0, r0:r0 + sub, :] = v
            return write

        def to_ext(v):
            ext_ref[SUBLANES + r0:SUBLANES + r0 + sub, :] = v

        def conv_chunk(c0):
            def stage():
                act_ref[0, c0:c0 + T, :] = _conv_silu(ext_ref[c0:c0 + SUBLANES + T, :], cw_ref, cb_ref)
            return stage

        outs = [(to(z_ref), w_z), (to_ext if conv else to(xbc_ref), w_xbc), (to(u5_ref), w_u5), (to(dt_ref), w_dt)]
        after = [conv_chunk(c0) for c0 in range(r0, r0 + sub, T)] if conv else ()
        return _inproj_stages(x[r0:r0 + sub], sh, sc, g_ref[...], outs, after)

    _interleave(*[stages(r0) for r0 in range(0, rows, sub)])
    if conv:
        cn_ref[0] = ext_ref[SUBLANES + rows - (CONV_W - 1):SUBLANES + rows, :]
        ext_ref[0:SUBLANES, :] = ext_ref[rows:rows + SUBLANES, :]


def _inproj(x, mod, g, w, conv_wb=None, *, tm, sub, seq_major, n_seq, first_seq):
    if seq_major:
        x_spec = _const_spec(x.shape)
        nb, rows, d = 1, x.shape[0] * x.shape[1], x.shape[2]
        assert tm == rows
    else:
        nb, rows, d = x.shape
        x_spec = pl.BlockSpec((1, tm, d), lambda i, j: (i, j, 0))
    nt = rows // tm
    widths = (D_SSD, CONV_DIM, D_S5, DT_PAD)
    conv = conv_wb is not None
    in_specs = [x_spec, _mod_spec(mod, n_seq, first_seq), _const_spec((1, d)),
                pl.BlockSpec(w.shape, lambda i, j: (0, 0), pipeline_mode=pl.Buffered(1))]
    out_specs = [pl.BlockSpec((1, tm, wd), lambda i, j: (i, j, 0)) for wd in widths]
    out_shape = [jax.ShapeDtypeStruct((nb, rows, wd), f32) for wd in widths]
    scratch = []
    if conv:
        in_specs += [_const_spec((CONV_W, CONV_DIM)), _const_spec((1, CONV_DIM))]
        out_specs.append(pl.BlockSpec((1, CONV_W - 1, CONV_DIM), lambda i, j: (i, 0, 0)))
        out_shape.append(jax.ShapeDtypeStruct((nb, CONV_W - 1, CONV_DIM), f32))
        scratch.append(pltpu.VMEM((SUBLANES + tm, CONV_DIM), f32))
    return pl.pallas_call(
        functools.partial(_inproj_kernel, seq_major=seq_major, sub=sub, conv=conv),
        grid=(nb, nt),
        in_specs=in_specs, out_specs=out_specs, out_shape=out_shape, scratch_shapes=scratch,
        compiler_params=_params("parallel", "arbitrary" if conv else "parallel"),
        name="in_proj",
    )(x, mod, g, w, *(conv_wb or ()))


def _split3(x):
    hi = x.astype(bf16)
    r1 = x - hi.astype(f32)
    mid = r1.astype(bf16)
    lo = (r1 - mid.astype(f32)).astype(bf16)
    return hi, mid, lo


def _dot_sel_lhs(sel16, x):
    return sum(jnp.dot(sel16, p, preferred_element_type=f32) for p in _split3(x))


def _dot_sel_rhs(x, sel16):
    hi, mid, _ = _split3(x)
    return jnp.dot(hi, sel16, preferred_element_type=f32) + jnp.dot(mid, sel16, preferred_element_type=f32)


def _gated_group_norm(y, z, ng):
    y = y * _silu(z)
    gw = D_SSD // GROUPS
    parts = []
    for g in range(GROUPS):
        yg = y[:, g * gw:(g + 1) * gw]
        parts.append(yg * lax.rsqrt(jnp.mean(yg * yg, axis=-1, keepdims=True) + EPS))
    return jnp.concatenate(parts, axis=-1) * ng


def _ssd_prompt_kernel(act_ref, z_ref, dt_ref, dtb_ref, alog_ref, dexp_ref, ng_ref, y_ref, st_ref, h_ref, *, cps):
    T = SSD_CHUNK
    gw = HEADS_PER_GROUP * HEAD_DIM

    @pl.when(pl.program_id(1) == 0)
    def _():
        h_ref[...] = jnp.zeros_like(h_ref)

    row = lax.broadcasted_iota(jnp.int32, (T, T), 0)
    col = lax.broadcasted_iota(jnp.int32, (T, T), 1)
    tri = row >= col
    tri16 = jnp.where(tri, 1.0, 0.0).astype(bf16)
    low_half = lax.broadcasted_iota(jnp.int32, (T, LANES), 1) < HEAD_DIM
    a_neg = -jnp.exp(alog_ref[...])

    lane_head = lax.broadcasted_iota(jnp.int32, (DT_PAD, D_SSD), 1) // HEAD_DIM
    sel16 = jnp.where(lax.broadcasted_iota(jnp.int32, (DT_PAD, D_SSD), 0) == lane_head, 1.0, 0.0).astype(bf16)

    def spread(q):
        return _dot_sel_rhs(q, sel16)

    def state_free(ci):
        r0 = ci * T
        act = act_ref[0, r0:r0 + T, :]
        xs = act[:, :D_SSD]
        bm = act[:, D_SSD:D_SSD + GROUPS * STATE]
        cm = act[:, D_SSD + GROUPS * STATE:]
        dtv = jax.nn.softplus(dt_ref[0, r0:r0 + T, :] + dtb_ref[...])
        a_cs = _dot_sel_lhs(tri16, dtv * a_neg)
        bg16s = [bm[:, g * STATE:(g + 1) * STATE].astype(bf16) for g in range(GROUPS)]
        cg16s = [cm[:, g * STATE:(g + 1) * STATE].astype(bf16) for g in range(GROUPS)]
        gmats = [lax.dot_general(cg16s[g], bg16s[g], NT_DIMS, preferred_element_type=f32)
                 for g in range(GROUPS)]
        yield
        a_last = a_cs[T - 1:T, :]
        a_cs_t = a_cs.T
        rep = spread(jnp.concatenate([dtv, jnp.exp(a_last - a_cs), jnp.exp(a_cs)], axis=0))
        yield
        x = xs * rep[:T]
        x16 = x.astype(bf16)
        xd16 = (x * rep[T:2 * T]).astype(bf16)
        e_cs = rep[2 * T:]
        yield
        y_part = []
        for g in range(GROUPS):
            for pr in range(HEADS_PER_GROUP // 2):
                s_pair = []
                for q in range(2):
                    h = g * HEADS_PER_GROUP + 2 * pr + q
                    seg = a_cs[:, h:h + 1] - a_cs_t[h:h + 1, :]
                    lmat = jnp.exp(jnp.where(tri, seg, -jnp.inf))
                    s_pair.append((gmats[g] * lmat).astype(bf16))
                lanes = slice(g * gw + pr * LANES, g * gw + (pr + 1) * LANES)
                both = jnp.dot(jnp.concatenate(s_pair, axis=0), x16[:, lanes], preferred_element_type=f32)
                yield
                y_part.append(jnp.where(low_half, both[:T], both[T:]))
        y_free = jnp.concatenate(y_part, axis=-1) + xs * dexp_ref[...]
        return y_free, e_cs, jnp.exp(a_last), xd16, bg16s, cg16s

    def state_step(ci, y_free, e_cs, e_last, xd16, bg16s, cg16s):
        r0 = ci * T
        y_offs = []
        for g in range(GROUPS):
            hp = h_ref[g]
            y_offs.append(lax.dot_general(cg16s[g], hp.astype(bf16), NT_DIMS, preferred_element_type=f32))
            upd = lax.dot_general(xd16[:, g * gw:(g + 1) * gw], bg16s[g], TN_DIMS, preferred_element_type=f32)
            for hh in range(HEADS_PER_GROUP):
                h = g * HEADS_PER_GROUP + hh
                rows = slice(hh * HEAD_DIM, (hh + 1) * HEAD_DIM)
                h_ref[g, rows, :] = e_last[:, h:h + 1] * hp[rows] + upd[rows]
        y = y_free + jnp.concatenate(y_offs, axis=-1) * e_cs
        y_ref[0, r0:r0 + T, :] = _gated_group_norm(y, z_ref[0, r0:r0 + T, :], ng_ref[...])

    gens = [state_free(ci) for ci in range(cps)]
    free = [None] * cps
    while any(f is None for f in free):
        for ci in range(cps):
            if free[ci] is None:
                try:
                    next(gens[ci])
                except StopIteration as done:
                    free[ci] = done.value
    for ci in range(cps):
        state_step(ci, *free[ci])

    @pl.when(pl.program_id(1) == pl.num_programs(1) - 1)
    def _():
        st_ref[0] = h_ref[...].reshape(HEADS, HEAD_DIM, STATE)


def _ssd_prompt(act, z, dt, dtb, alog, dexp, ng, *, cps=16):
    nb, L, _ = act.shape
    rows = cps * SSD_CHUNK
    blk = lambda wd: pl.BlockSpec((1, rows, wd), lambda b, c: (b, c, 0))
    return pl.pallas_call(
        functools.partial(_ssd_prompt_kernel, cps=cps),
        grid=(nb, L // rows),
        in_specs=[blk(CONV_DIM), blk(D_SSD), blk(DT_PAD), _const_spec((1, DT_PAD)), _const_spec((1, DT_PAD)),
                  _const_spec((1, D_SSD)), _const_spec((1, D_SSD))],
        out_specs=[blk(D_SSD), pl.BlockSpec((1, HEADS, HEAD_DIM, STATE), lambda b, c: (b, 0, 0, 0))],
        out_shape=[jax.ShapeDtypeStruct((nb, L, D_SSD), f32),
                   jax.ShapeDtypeStruct((nb, HEADS, HEAD_DIM, STATE), f32)],
        scratch_shapes=[pltpu.VMEM((GROUPS, HEADS_PER_GROUP * HEAD_DIM, STATE), f32)],
        compiler_params=_params("parallel", "arbitrary"),
        name="ssd_prompt",
    )(act, z, dt, dtb, alog, dexp, ng)


def _ssd_sample_kernel(xbc_ref, z_ref, dt_ref, cbuf_ref, st_ref, cw_ref, cb_ref, dtb_ref, alog_ref,
                       dexp_ref, ng_ref, y_ref, stn_ref, cn_ref, dtot_ref, *, L, nb):
    gw = HEADS_PER_GROUP * HEAD_DIM
    full = [cbuf_ref[j] for j in range(CONV_W - 1)] + [xbc_ref[t] for t in range(L)]
    for j in range(CONV_W - 1):
        cn_ref[j] = full[L + j]
    a_neg = -jnp.exp(alog_ref[...])
    xs, bm, cm, dtv, acs = [], [], [], [], []
    run = None
    for t in range(L):
        conv = cb_ref[...]
        for k in range(CONV_W):
            conv = conv + full[t + k] * cw_ref[k:k + 1, :]
        act = _silu(conv)
        xs.append(act[:, :D_SSD])
        bm.append(act[:, D_SSD:D_SSD + GROUPS * STATE])
        cm.append(act[:, D_SSD + GROUPS * STATE:])
        d = jax.nn.softplus(dt_ref[t] + dtb_ref[...])
        dtv.append(d)
        run = d * a_neg if run is None else run + d * a_neg
        acs.append(run)
    a_tot = acs[L - 1]
    dtot_ref[...] = jnp.exp(a_tot)

    lane_head = lax.broadcasted_iota(jnp.int32, (DT_PAD, D_SSD), 1) // HEAD_DIM
    sel16 = jnp.where(lax.broadcasted_iota(jnp.int32, (DT_PAD, D_SSD), 0) == lane_head, 1.0, 0.0).astype(bf16)
    pairs = [(t, s) for t in range(L) for s in range(t)]
    factors = (dtv + [jnp.exp(a_tot - acs[t]) for t in range(L)] + [jnp.exp(acs[t]) for t in range(L)]
               + [jnp.exp(acs[t] - acs[s]) for t, s in pairs])
    rep = _dot_sel_rhs(jnp.concatenate(factors, axis=0), sel16)
    piece = lambda i: rep[i * nb:(i + 1) * nb]
    x = [xs[t] * piece(t) for t in range(L)]
    xd_stack = jnp.concatenate([x[t] * piece(L + t) for t in range(L)], axis=0)
    e_cs = [piece(2 * L + t) for t in range(L)]
    decay = {ts: piece(3 * L + i) for i, ts in enumerate(pairs)}

    in_group0 = lax.broadcasted_iota(jnp.int32, (nb, D_SSD), 1) < gw
    y_intra = []
    for t in range(L):
        acc = None
        for s in range(t + 1):
            cb_dot = [jnp.sum(cm[t][:, g * STATE:(g + 1) * STATE] * bm[s][:, g * STATE:(g + 1) * STATE],
                              axis=-1, keepdims=True) for g in range(GROUPS)]
            w = jnp.where(in_group0, cb_dot[0], cb_dot[1])
            term = w * x[s] if s == t else w * decay[(t, s)] * x[s]
            acc = term if acc is None else acc + term
        y_intra.append(acc)

    c_stack = [jnp.concatenate([cm[t][:, g * STATE:(g + 1) * STATE] for t in range(L)], axis=0).astype(bf16)
               for g in range(GROUPS)]
    b_stack = [jnp.concatenate([bm[t][:, g * STATE:(g + 1) * STATE] for t in range(L)], axis=0).astype(bf16)
               for g in range(GROUPS)]
    seq_of_row = lax.broadcasted_iota(jnp.int32, (L * nb, gw), 0) & (nb - 1)

    def per_seq(b, yoff):
        mine = seq_of_row == b
        drow = dtot_ref[pl.ds(b, 1), :]
        out = []
        for g in range(GROUPS):
            h0 = st_ref[b, g]
            r = lax.dot_general(c_stack[g], h0.astype(bf16), NT_DIMS, preferred_element_type=f32)
            out.append(yoff[g] + jnp.where(mine, r, 0.0))
            xm = jnp.where(mine, xd_stack[:, g * gw:(g + 1) * gw], 0.0).astype(bf16)
            upd = lax.dot_general(xm, b_stack[g], TN_DIMS, preferred_element_type=f32)
            for hh in range(HEADS_PER_GROUP):
                h = g * HEADS_PER_GROUP + hh
                rows = slice(hh * HEAD_DIM, (hh + 1) * HEAD_DIM)
                stn_ref[b, g, rows, :] = drow[:, h:h + 1] * h0[rows] + upd[rows]
        return tuple(out)

    yoff = lax.fori_loop(0, nb, per_seq, tuple(jnp.zeros((L * nb, gw), f32) for _ in range(GROUPS)),
                         unroll=8)

    for t in range(L):
        y_off = jnp.concatenate([yoff[g][t * nb:(t + 1) * nb] for g in range(GROUPS)], axis=-1)
        y = y_intra[t] + y_off * e_cs[t] + xs[t] * dexp_ref[...]
        y_ref[t] = _gated_group_norm(y, z_ref[t], ng_ref[...])


def _ssd_sample(xbc, z, dt, cbuf, st, cw, cb, dtb, alog, dexp, ng, *, nb=32):
    L, B, _ = xbc.shape
    assert nb & (nb - 1) == 0 and B % nb == 0
    tblk = lambda n, wd: pl.BlockSpec((n, nb, wd), lambda i: (0, i, 0))
    gw = HEADS_PER_GROUP * HEAD_DIM
    st = st.reshape(B, GROUPS, gw, STATE)
    st_spec = pl.BlockSpec((nb, GROUPS, gw, STATE), lambda i: (i, 0, 0, 0))
    return pl.pallas_call(
        functools.partial(_ssd_sample_kernel, L=L, nb=nb),
        grid=(B // nb,),
        in_specs=[tblk(L, CONV_DIM), tblk(L, D_SSD), tblk(L, DT_PAD), tblk(CONV_W - 1, CONV_DIM), st_spec,
                  _const_spec((CONV_W, CONV_DIM)), _const_spec((1, CONV_DIM)),
                  _const_spec((1, DT_PAD)), _const_spec((1, DT_PAD)),
                  _const_spec((1, D_SSD)), _const_spec((1, D_SSD))],
        out_specs=[tblk(L, D_SSD), st_spec, tblk(CONV_W - 1, CONV_DIM)],
        out_shape=[jax.ShapeDtypeStruct((L, B, D_SSD), f32),
                   jax.ShapeDtypeStruct((B, GROUPS, gw, STATE), f32),
                   jax.ShapeDtypeStruct((CONV_W - 1, B, CONV_DIM), f32)],
        scratch_shapes=[pltpu.VMEM((nb, DT_PAD), f32)],
        compiler_params=_params("parallel"),
        name="ssd_sample",
    )(xbc, z, dt, cbuf, st, cw, cb, dtb, alog, dexp, ng)


def _s5_param_kernel(lr_ref, li_ref, ls_ref, br_ref, bi_ref, cr_ref, ci_ref,
                     ar_ref, ai_ref, bblk_ref, cblk_ref, b_scr, c_scr):
    lr, li = lr_ref[...], li_ref[...]
    step = jnp.exp(ls_ref[...])
    mag = jnp.exp(lr * step)
    abr = mag * jnp.cos(li * step)
    abi = mag * jnp.sin(li * step)
    nr, ni = abr - 1.0, abi
    den = lr * lr + li * li
    fr = (nr * lr + ni * li) / den
    fi = (ni * lr - nr * li) / den
    br, bi = br_ref[...], bi_ref[...]
    bbr = fr * br - fi * bi
    bbi = fr * bi + fi * br
    b_scr[...] = jnp.zeros_like(b_scr)
    c_scr[...] = jnp.zeros_like(c_scr)
    gps = S5_GROUPS // S5_SLABS
    ns = S5_SLAB_STATE
    for g in range(S5_GROUPS):
        s, gl = divmod(g, gps)
        ch = slice(gl * S5_CH, (gl + 1) * S5_CH)
        st = slice(gl * S5_STATE, (gl + 1) * S5_STATE)
        st_im = slice(ns + gl * S5_STATE, ns + (gl + 1) * S5_STATE)
        ar_ref[s, :, st] = jnp.broadcast_to(abr[g], (SUBLANES, S5_STATE))
        ai_ref[s, :, st] = jnp.broadcast_to(abi[g], (SUBLANES, S5_STATE))
        b_scr[s, ch, st] = bbr[g]
        b_scr[s, ch, st_im] = bbi[g]
        c_scr[s, st, ch] = cr_ref[g].T
        c_scr[s, st_im, ch] = -ci_ref[g].T
    bblk_ref[...] = b_scr[...].astype(bf16)
    cblk_ref[...] = c_scr[...].astype(bf16)


def _s5_params(a_re, a_im, log_step, b_re_t, b_im_t, c_re, c_im):
    g, p = a_re.shape
    ns = S5_SLAB_STATE
    return pl.pallas_call(
        _s5_param_kernel,
        out_shape=[jax.ShapeDtypeStruct((S5_SLABS, SUBLANES, ns), f32),
                   jax.ShapeDtypeStruct((S5_SLABS, SUBLANES, ns), f32),
                   jax.ShapeDtypeStruct((S5_SLABS, LANES, 2 * ns), bf16),
                   jax.ShapeDtypeStruct((S5_SLABS, 2 * ns, LANES), bf16)],
        scratch_shapes=[pltpu.VMEM((S5_SLABS, LANES, 2 * ns), f32),
                        pltpu.VMEM((S5_SLABS, 2 * ns, LANES), f32)],
        name="s5_params",
    )(a_re.reshape(g, 1, p), a_im.reshape(g, 1, p), log_step.reshape(g, 1, 1), b_re_t, b_im_t, c_re, c_im)


def _gelu(x):
    return 0.5 * x * (1.0 + lax.erf(x * (2.0 ** -0.5)))


def _s5_slabs_interleaved(ut_ref, hbuf_ref, hst_ref, g_ref, ar_ref, ai_ref, bblk_ref, cblk_ref, dsk_ref, tl,
                          row_block=256, n_stages=8):
    ns = S5_SLAB_STATE
    nb = SUBLANES
    rows = nb * tl

    def lanes(s):
        return slice(s * LANES, (s + 1) * LANES)

    def bu_stage(slabs):
        for s in slabs:
            for r in range(0, rows, row_block):
                us = ut_ref[s, r:r + row_block, :]
                hbuf_ref[s, nb + r:nb + r + row_block, :] = jnp.dot(us.astype(bf16), bblk_ref[s],
                                                                    preferred_element_type=f32)
                yield

    def scan_stage(slabs):
        carry = [(hst_ref[s, :, :ns], hst_ref[s, :, ns:]) for s in slabs]
        coef = [(ar_ref[s], ai_ref[s]) for s in slabs]
        for t in range(tl):
            r = nb + nb * t
            for i, s in enumerate(slabs):
                (pr, pi), (ar, ai) = carry[i], coef[i]
                nr = ar * pr - ai * pi + hbuf_ref[s, r:r + nb, :ns]
                ni = ar * pi + ai * pr + hbuf_ref[s, r:r + nb, ns:]
                hbuf_ref[s, r:r + nb, :ns] = nr
                hbuf_ref[s, r:r + nb, ns:] = ni
                carry[i] = (nr, ni)
            if (t + 1) % (tl // n_stages) == 0:
                yield
        for i, s in enumerate(slabs):
            hst_ref[s, :, :ns] = carry[i][0]
            hst_ref[s, :, ns:] = carry[i][1]

    def y_stage(slabs):
        for s in slabs:
            for r in range(0, rows, row_block):
                h16 = hbuf_ref[s, nb + r:nb + r + row_block, :].astype(bf16)
                ys = jnp.dot(h16, cblk_ref[s], preferred_element_type=f32)
                ys = ys + dsk_ref[:, lanes(s)] * ut_ref[s, r:r + row_block, :]
                g_ref[s, r:r + row_block, :] = _gelu(ys)
                yield

    half = S5_SLABS // 2
    first, second = tuple(range(half)), tuple(range(half, S5_SLABS))
    _interleave(bu_stage(first))
    _interleave(scan_stage(first), bu_stage(second))
    _interleave(scan_stage(second), y_stage(first))
    _interleave(y_stage(second))


def _s5_scan_vmem(hbuf_ref, hst_ref, ar_ref, ai_ref, s, nb, rows):
    ns = S5_SLAB_STATE
    hbuf_ref[s, 0:nb, :] = hst_ref[s]
    ar, ai = ar_ref[s], ai_ref[s]

    def body(i, carry):
        j = pl.multiple_of(nb + i * SUBLANES, SUBLANES)
        prev = hbuf_ref[s, pl.ds(j - nb, SUBLANES), :]
        cur = hbuf_ref[s, pl.ds(j, SUBLANES), :]
        pr, pi = prev[:, :ns], prev[:, ns:]
        hbuf_ref[s, pl.ds(j, SUBLANES), :ns] = ar * pr - ai * pi + cur[:, :ns]
        hbuf_ref[s, pl.ds(j, SUBLANES), ns:] = ar * pi + ai * pr + cur[:, ns:]
        return carry

    lax.fori_loop(0, rows // SUBLANES, body, 0, unroll=min(8, nb // SUBLANES))
    hst_ref[s] = hbuf_ref[s, rows:rows + nb, :]


S5_N_IN = 10


def _s5_kernel(*refs, nb, tl, batch_major, n_cast):
    (u_ref, re0_ref, im0_ref, ar_ref, ai_ref, bblk_ref, cblk_ref, dsk_ref, wglu_ref,
     bglu_ref) = refs[:S5_N_IN]
    cast_in = refs[S5_N_IN:S5_N_IN + n_cast]
    y_ref, ren_ref, imn_ref = refs[S5_N_IN + n_cast:S5_N_IN + n_cast + 3]
    cast_out = refs[S5_N_IN + n_cast + 3:S5_N_IN + 2 * n_cast + 3]
    ut_ref, hbuf_ref, hst_ref, g_ref = refs[S5_N_IN + 2 * n_cast + 3:]
    for w_ref, w16_ref in zip(cast_in, cast_out):
        w16_ref[...] = w_ref[...].astype(bf16)
    step = pl.program_id(0)
    rows = nb * tl
    ns = S5_SLAB_STATE

    @pl.when(step == 0)
    def _():
        for s in range(S5_SLABS):
            hst_ref[s, :, :ns] = re0_ref[:, s * ns:(s + 1) * ns]
            hst_ref[s, :, ns:] = im0_ref[:, s * ns:(s + 1) * ns]

    for s in range(S5_SLABS):
        sl = slice(s * LANES, (s + 1) * LANES)
        if batch_major:
            for b in range(nb):
                ut_ref[s, pl.ds(b, tl, stride=nb), :] = u_ref[b, :, sl]
        else:
            ut_ref[s] = u_ref[:, :, sl].reshape(rows, LANES)

    if nb == SUBLANES:
        _s5_slabs_interleaved(ut_ref, hbuf_ref, hst_ref, g_ref, ar_ref, ai_ref, bblk_ref, cblk_ref, dsk_ref, tl)
    else:
        for s in range(S5_SLABS):
            sl = slice(s * LANES, (s + 1) * LANES)
            hbuf_ref[s, nb:nb + rows, :] = jnp.dot(ut_ref[s].astype(bf16), bblk_ref[s],
                                                   preferred_element_type=f32)
            _s5_scan_vmem(hbuf_ref, hst_ref, ar_ref, ai_ref, s, nb, rows)
            ys = jnp.dot(hbuf_ref[s, nb:nb + rows, :].astype(bf16), cblk_ref[s], preferred_element_type=f32)
            g_ref[s] = _gelu(ys + dsk_ref[:, sl] * ut_ref[s])

    g = jnp.concatenate([g_ref[s] for s in range(S5_SLABS)], axis=-1)
    gate = jnp.dot(g.astype(bf16), wglu_ref[...], preferred_element_type=f32) + bglu_ref[...]
    out = g * jax.nn.sigmoid(gate)
    if batch_major:
        for s in range(S5_SLABS):
            sl = slice(s * LANES, (s + 1) * LANES)
            ut_ref[s] = out[:, sl]
            for b in range(nb):
                y_ref[b, :, sl] = ut_ref[s, pl.ds(b, tl, stride=nb), :]
    else:
        y_ref[...] = out.reshape(tl, nb, D_S5)

    @pl.when(step == pl.num_programs(0) - 1)
    def _():
        for s in range(S5_SLABS):
            ren_ref[:, s * ns:(s + 1) * ns] = hst_ref[s, :, :ns]
            imn_ref[:, s * ns:(s + 1) * ns] = hst_ref[s, :, ns:]


def _s5(u, re0, im0, ar, ai, bblk, cblk, dsk, wglu, bglu, *, tl, batch_major, cast=()):
    if batch_major:
        nb, L, _ = u.shape
        u_spec = pl.BlockSpec((nb, tl, D_S5), lambda i: (0, i, 0))
    else:
        L, nb, _ = u.shape
        u_spec = pl.BlockSpec((tl, nb, D_S5), lambda i: (i, 0, 0))
    rows = nb * tl
    steps = L // tl
    nstate = S5_GROUPS * S5_STATE
    st_spec = _const_spec((nb, nstate))
    cast_specs = [pl.BlockSpec((w.shape[0] // steps, w.shape[1]), lambda i: (i, 0)) for w in cast]
    assert all(w.shape[0] % (steps * 2 * SUBLANES) == 0 for w in cast)
    return pl.pallas_call(
        functools.partial(_s5_kernel, nb=nb, tl=tl, batch_major=batch_major, n_cast=len(cast)),
        grid=(steps,),
        in_specs=[u_spec, st_spec, st_spec,
                  _const_spec((S5_SLABS, SUBLANES, S5_SLAB_STATE)),
                  _const_spec((S5_SLABS, SUBLANES, S5_SLAB_STATE)),
                  _const_spec((S5_SLABS, LANES, 2 * S5_SLAB_STATE)),
                  _const_spec((S5_SLABS, 2 * S5_SLAB_STATE, LANES)),
                  _const_spec((1, D_S5)), _const_spec((D_S5, D_S5)), _const_spec((1, D_S5))] + cast_specs,
        out_specs=[u_spec, st_spec, st_spec] + cast_specs,
        out_shape=[jax.ShapeDtypeStruct(u.shape, f32),
                   jax.ShapeDtypeStruct((nb, nstate), f32),
                   jax.ShapeDtypeStruct((nb, nstate), f32)]
                  + [jax.ShapeDtypeStruct(w.shape, bf16) for w in cast],
        scratch_shapes=[pltpu.VMEM((S5_SLABS, rows, LANES), f32),
                        pltpu.VMEM((S5_SLABS, nb + rows, 2 * S5_SLAB_STATE), f32),
                        pltpu.VMEM((S5_SLABS, nb, 2 * S5_SLAB_STATE), f32),
                        pltpu.VMEM((S5_SLABS, rows, LANES), f32)],
        compiler_params=_params("arbitrary"),
        name="s5_mixer",
    )(u, re0, im0, ar, ai, bblk, cblk, dsk, wglu, bglu, *cast)


def _ffn_stages(x, ys, y5, mods, n2g, nfg, wo_ref, wg_ref, wu_ref, wd_ref, write_out, ff_chunk):
    g1, sh2, sc2, g2, shf, scf = mods
    att = jnp.dot(ys.astype(bf16), wo_ref[:D_SSD, :], preferred_element_type=f32)
    att = att + jnp.dot(y5.astype(bf16), wo_ref[D_SSD:, :], preferred_element_type=f32)
    yield
    x1 = x + _by_row(lambda v, s: v * s, att, g1)
    v = _rms_mod(x1, n2g, sc2, sh2).astype(bf16)
    ff = None
    for o in range(0, D_FF, ff_chunk):
        gate = jnp.dot(v, wg_ref[:, o:o + ff_chunk], preferred_element_type=f32)
        up = jnp.dot(v, wu_ref[:, o:o + ff_chunk], preferred_element_type=f32)
        hid = (_silu(gate) * up).astype(bf16)
        part = jnp.dot(hid, wd_ref[o:o + ff_chunk, :], preferred_element_type=f32)
        ff = part if ff is None else ff + part
        yield
    x2 = x1 + _by_row(lambda v, s: v * s, ff, g2)
    write_out(_rms_mod(x2, nfg, scf, shf))


def _ffn_kernel(x_ref, ys_ref, y5_ref, mod_ref, modf_ref, n2g_ref, nfg_ref, wo_ref, wg_ref, wu_ref, wd_ref,
                o_ref, *, seq_major, ff_chunk, sub):
    mods = (_mod_row(mod_ref, 2, seq_major), _mod_row(mod_ref, 3, seq_major), _mod_row(mod_ref, 4, seq_major),
            _mod_row(mod_ref, 5, seq_major), _mod_row(modf_ref, 0, seq_major), _mod_row(modf_ref, 1, seq_major))
    x = _rows_in(x_ref, seq_major)
    rows = x.shape[0]
    tiles = {}

    def stages(r0):
        def write_out(y):
            if seq_major:
                tiles[r0] = y
            else:
                o_ref[0, r0:r0 + sub, :] = y
        return _ffn_stages(x[r0:r0 + sub], ys_ref[0, r0:r0 + sub, :], y5_ref[0, r0:r0 + sub, :], mods,
                           n2g_ref[...], nfg_ref[...], wo_ref, wg_ref, wu_ref, wd_ref, write_out, ff_chunk)

    _interleave(*[stages(r0) for r0 in range(0, rows, sub)])
    if seq_major:
        s, t, d = o_ref.shape
        y = jnp.concatenate([tiles[r0] for r0 in range(0, rows, sub)], axis=0)
        o_ref[...] = jnp.swapaxes(y.reshape(t, s, d), 0, 1)


def _ffn(x, ys, y5, mod, modf, n2g, nfg, wo, wg, wu, wd, *, tm, sub, seq_major, n_seq, first_seq, ff_chunk=256):
    if seq_major:
        x_spec = _const_spec(x.shape)
        nb, rows, d = 1, x.shape[0] * x.shape[1], x.shape[2]
        assert tm == rows
    else:
        nb, rows, d = x.shape
        x_spec = pl.BlockSpec((1, tm, d), lambda i, j: (i, j, 0))
    nt = rows // tm
    mod_spec = _mod_spec(mod, n_seq, first_seq)
    modf_spec = _mod_spec(modf, n_seq, first_seq)
    blk = lambda wd_: pl.BlockSpec((1, tm, wd_), lambda i, j: (i, j, 0))
    single = dict(pipeline_mode=pl.Buffered(1))
    wspec = lambda shape: pl.BlockSpec(shape, lambda i, j: (0, 0), **single)
    return pl.pallas_call(
        functools.partial(_ffn_kernel, seq_major=seq_major, ff_chunk=ff_chunk, sub=sub),
        grid=(nb, nt),
        in_specs=[x_spec, blk(D_SSD), blk(D_S5), mod_spec, modf_spec,
                  _const_spec((1, d)), _const_spec((1, d)),
                  wspec((d, d)), wspec((d, D_FF)), wspec((d, D_FF)), wspec((D_FF, d))],
        out_specs=x_spec,
        out_shape=jax.ShapeDtypeStruct(x.shape, f32),
        compiler_params=_params("parallel", "parallel"),
        name="out_ffn",
    )(x, ys, y5, mod, modf, n2g, nfg, wo, wg, wu, wd)


def kernel(x_prompt, x_sample, c_prompt, c_sample, state_ssd, state_conv, state_s5_re, state_s5_im, w_ada, b_ada, norm1_g, w_in, conv_w, conv_b, ssd_dt_bias, ssd_A_log, ssd_D, ssd_norm_g, s5_A_re, s5_A_im, s5_log_step, s5_B_re, s5_B_im, s5_C_re, s5_C_im, s5_D, w_glu, b_glu, w_out, norm2_g, w_ffn_gate, w_ffn_up, w_ffn_down, w_ada_f, b_ada_f, normf_g):
    assert w_ada.shape[0] == 1, "single-layer stack"
    bp, seq, d = x_prompt.shape
    bs, dseq, _ = x_sample.shape

    c_all = jnp.concatenate([c_sample, c_prompt], axis=0)
    mod = _ada_mod(c_all, w_ada[0], b_ada[0])
    modf = _ada_mod(c_all, w_ada_f, b_ada_f)
    seqs_s = dict(n_seq=bs, first_seq=0)
    seqs_p = dict(n_seq=bp, first_seq=bs)

    w_in_p = w_in[0].T
    pad_h = lambda v: jnp.concatenate([v, jnp.zeros((DT_PAD - HEADS,), f32)]).reshape(1, DT_PAD)
    dtb = pad_h(ssd_dt_bias[0])
    alog = pad_h(ssd_A_log[0])
    dexp = jnp.repeat(ssd_D[0], HEAD_DIM).reshape(1, D_SSD)
    ng = ssd_norm_g[0].reshape(1, D_SSD)
    cw, cb = conv_w[0], conv_b[0].reshape(1, CONV_DIM)

    ar, ai, bblk, cblk = _s5_params(s5_A_re[0], s5_A_im[0], s5_log_step[0],
                                    s5_B_re[0].transpose(0, 2, 1), s5_B_im[0].transpose(0, 2, 1),
                                    s5_C_re[0], s5_C_im[0])
    dsk = s5_D[0].reshape(1, D_S5)
    wglu = w_glu[0].astype(bf16)
    bglu = b_glu[0].reshape(1, D_S5)
    n1g, n2g, nfg = norm1_g[0].reshape(1, d), norm2_g[0].reshape(1, d), normf_g.reshape(1, d)
    nstate = S5_GROUPS * S5_STATE

    z, act, u5, dt, conv_p = _inproj(x_prompt, mod, n1g, w_in_p, (cw, cb), tm=1024, sub=512, seq_major=False,
                                     **seqs_p)
    y_ssd, ssd_p = _ssd_prompt(act, z, dt, dtb, alog, dexp, ng)
    zeros_st = jnp.zeros((bp, nstate), f32)
    y_s5, re_p, im_p, wo, wg, wu, wd = _s5(
        u5, zeros_st, zeros_st, ar, ai, bblk, cblk, dsk, wglu, bglu, tl=128, batch_major=True,
        cast=(w_out[0], w_ffn_gate[0], w_ffn_up[0], w_ffn_down[0]))
    y_prompt = _ffn(x_prompt, y_ssd, y_s5, mod, modf, n2g, nfg, wo, wg, wu, wd, tm=1024, sub=512,
                    seq_major=False, **seqs_p)

    rows_s = dseq * bs
    steps = lambda a: a.reshape(dseq, bs, a.shape[-1])
    flat = lambda a: a.reshape(1, rows_s, a.shape[-1])
    z, xbc, u5, dt = _inproj(x_sample, mod, n1g, w_in_p, tm=rows_s, sub=rows_s // 2, seq_major=True, **seqs_s)
    y_ssd, ssd_s, conv_s = _ssd_sample(steps(xbc), steps(z), steps(dt), state_conv[0].transpose(1, 0, 2),
                                       state_ssd[0], cw, cb, dtb, alog, dexp, ng)
    y_s5, re_s, im_s = _s5(steps(u5), state_s5_re[0].reshape(bs, nstate), state_s5_im[0].reshape(bs, nstate),
                           ar, ai, bblk, cblk, dsk, wglu, bglu, tl=dseq, batch_major=False)
    y_sample = _ffn(x_sample, flat(y_ssd), flat(y_s5), mod, modf, n2g, nfg, wo, wg, wu, wd,
                    tm=rows_s, sub=rows_s, seq_major=True, **seqs_s)

    g5 = (S5_GROUPS, S5_STATE)
    return (y_prompt, y_sample,
            ssd_p[None], ssd_s.reshape((1,) + state_ssd.shape[1:]),
            conv_p[None], conv_s.transpose(1, 0, 2)[None],
            re_p.reshape((1, bp) + g5), re_s.reshape((1, bs) + g5),
            im_p.reshape((1, bp) + g5), im_s.reshape((1, bs) + g5))
```

```python
import functools

import jax
import jax.numpy as jnp
from jax import lax
from jax.experimental import pallas as pl
from jax.experimental.pallas import tpu as pltpu

f32 = jnp.float32
bf16 = jnp.bfloat16

D_MODEL = 1024
D_SSD = 512
HEAD_DIM = 64
HEADS = 8
GROUPS = 2
HEADS_PER_GROUP = HEADS // GROUPS
STATE = 128
CONV_W = 4
CONV_DIM = D_SSD + 2 * GROUPS * STATE
D_S5 = 512
S5_CH = 16
S5_GROUPS = 32
S5_STATE = 64
D_FF = 2816
N_ADA = 6
EPS = 1e-6

LANES = 128
SUBLANES = 8
SSD_CHUNK = 128
S5_SLABS = D_S5 // LANES
S5_SLAB_STATE = (S5_GROUPS // S5_SLABS) * S5_STATE
DT_PAD = LANES
VMEM_LIMIT = 56 * 1024 * 1024

NT_DIMS = (((1,), (1,)), ((), ()))
TN_DIMS = (((0,), (0,)), ((), ()))


def _silu(x):
    return x * jax.nn.sigmoid(x)


def _interleave(*gens):
    live = list(gens)
    while live:
        for g in list(live):
            try:
                next(g)
            except StopIteration:
                live.remove(g)


def _by_row(fn, v, *ms):
    r, d = v.shape
    m_rows = ms[0].shape[0]
    if m_rows in (1, r):
        return fn(v, *ms)
    out = fn(v.reshape(r // m_rows, m_rows, d), *[m[None] for m in ms])
    return out.reshape(r, d)


def _rms_mod(x, g, sc, sh):
    y = x * lax.rsqrt(jnp.mean(x * x, axis=-1, keepdims=True) + EPS)
    return _by_row(lambda v, s, t: v * (1.0 + s) + t, y * g, sc, sh)


def _mod_row(mod_ref, i, per_row):
    return mod_ref[i] if per_row else mod_ref[i, pl.ds(pl.program_id(0), 1), :]


def _mod_spec(mod, n_seq, first_seq):
    assert first_seq % n_seq == 0
    return pl.BlockSpec((mod.shape[0], n_seq, mod.shape[2]), lambda *_: (0, first_seq // n_seq, 0))


def _const_spec(shape):
    nd = len(shape)
    return pl.BlockSpec(shape, lambda *_: (0,) * nd)


def _params(*sem):
    return pltpu.CompilerParams(dimension_semantics=sem, vmem_limit_bytes=VMEM_LIMIT)


ADA_K_BLOCK = 256


def _ada_kernel(c_ref, w_ref, b_ref, o_ref):
    n, _, d = o_ref.shape
    s = _silu(c_ref[...]).astype(bf16)

    @pl.when(pl.program_id(0) == 0)
    def _():
        for j in range(n):
            o_ref[j] = jnp.broadcast_to(b_ref[:, j * d:(j + 1) * d], o_ref.shape[1:])

    for j in range(n):
        o_ref[j] += jnp.dot(s, w_ref[:, j * d:(j + 1) * d].astype(bf16), preferred_element_type=f32)


def _ada_mod(c, w, b):
    m, k = c.shape
    n = w.shape[1] // k
    return pl.pallas_call(
        _ada_kernel,
        grid=(k // ADA_K_BLOCK,),
        in_specs=[pl.BlockSpec((m, ADA_K_BLOCK), lambda i: (0, i)),
                  pl.BlockSpec((ADA_K_BLOCK, n * k), lambda i: (i, 0)),
                  _const_spec((1, n * k))],
        out_specs=_const_spec((n, m, k)),
        out_shape=jax.ShapeDtypeStruct((n, m, k), f32),
        compiler_params=_params("arbitrary"),
        name="ada_mod",
    )(c, w, b.reshape(1, n * k))


def _rows_in(x_ref, seq_major):
    if not seq_major:
        return x_ref[0]
    s, t, d = x_ref.shape
    return jnp.swapaxes(x_ref[...], 0, 1).reshape(s * t, d)


def _conv_silu(ext, cw_ref, cb_ref):
    w = [cw_ref[k:k + 1, :] for k in range(CONV_W)]
    back1 = pltpu.roll(ext, 1, 0)
    older = pltpu.roll(ext * w[1] + back1 * w[0], 2, 0)
    conv = cb_ref[...] + ext[SUBLANES:, :] * w[3] + back1[SUBLANES:, :] * w[2] + older[SUBLANES:, :]
    return _silu(conv)


def _inproj_stages(x, sh, sc, g, outs, after=()):
    u = _rms_mod(x, g, sc, sh).astype(bf16)
    yield
    for write, w16 in outs:
        write(lax.dot_general(u, w16, NT_DIMS, preferred_element_type=f32))
        yield
    for stage in after:
        stage()
        yield


def _inproj_kernel(x_ref, mod_ref, g_ref, w_ref, *rest, seq_major, sub, conv):
    if conv:
        cw_ref, cb_ref, z_ref, act_ref, u5_ref, dt_ref, cn_ref, ext_ref = rest
        pl.when(pl.program_id(1) == 0)(lambda: ext_ref.__setitem__(
            (slice(0, SUBLANES), slice(None)), jnp.zeros((SUBLANES, CONV_DIM), f32)))
    else:
        z_ref, xbc_ref, u5_ref, dt_ref = rest
    sh = _mod_row(mod_ref, 0, seq_major)
    sc = _mod_row(mod_ref, 1, seq_major)
    x = _rows_in(x_ref, seq_major)
    rows = x.shape[0]
    T = SSD_CHUNK
    o_dt = D_SSD + CONV_DIM
    w16 = lambda o, width: w_ref[o:o + width, :].astype(bf16)
    w_z, w_xbc, w_u5, w_dt = w16(0, D_SSD), w16(D_SSD, CONV_DIM), w16(o_dt + HEADS, D_S5), w16(o_dt, DT_PAD)

    def stages(r0):
        def to(ref):
            def write(v):
                ref[0, r0:r0 + sub, :] = v
            return write

        def to_ext(v):
            ext_ref[SUBLANES + r0:SUBLANES + r0 + sub, :] = v

        def conv_chunk(c0):
            def stage():
                act_ref[0, c0:c0 + T, :] = _conv_silu(ext_ref[c0:c0 + SUBLANES + T, :], cw_ref, cb_ref)
            return stage

        outs = [(to(z_ref), w_z), (to_ext if conv else to(xbc_ref), w_xbc), (to(u5_ref), w_u5), (to(dt_ref), w_dt)]
        after = [conv_chunk(c0) for c0 in range(r0, r0 + sub, T)] if conv else ()
        return _inproj_stages(x[r0:r0 + sub], sh, sc, g_ref[...], outs, after)

    gens = [stages(r0) for r0 in range(0, rows, sub)]
    if conv and len(gens) > 1:
        for _ in range(5):
            next(gens[0])
    _interleave(*gens)
    if conv:
        cn_ref[0] = ext_ref[SUBLANES + rows - (CONV_W - 1):SUBLANES + rows, :]
        ext_ref[0:SUBLANES, :] = ext_ref[rows:rows + SUBLANES, :]


def _inproj(x, mod, g, w, conv_wb=None, *, tm, sub, seq_major, n_seq, first_seq):
    if seq_major:
        x_spec = _const_spec(x.shape)
        nb, rows, d = 1, x.shape[0] * x.shape[1], x.shape[2]
        assert tm == rows
    else:
        nb, rows, d = x.shape
        x_spec = pl.BlockSpec((1, tm, d), lambda i, j: (i, j, 0))
    nt = rows // tm
    widths = (D_SSD, CONV_DIM, D_S5, DT_PAD)
    conv = conv_wb is not None
    in_specs = [x_spec, _mod_spec(mod, n_seq, first_seq), _const_spec((1, d)),
                pl.BlockSpec(w.shape, lambda i, j: (0, 0), pipeline_mode=pl.Buffered(1))]
    out_specs = [pl.BlockSpec((1, tm, wd), lambda i, j: (i, j, 0)) for wd in widths]
    out_shape = [jax.ShapeDtypeStruct((nb, rows, wd), f32) for wd in widths]
    scratch = []
    if conv:
        in_specs += [_const_spec((CONV_W, CONV_DIM)), _const_spec((1, CONV_DIM))]
        out_specs.append(pl.BlockSpec((1, CONV_W - 1, CONV_DIM), lambda i, j: (i, 0, 0)))
        out_shape.append(jax.ShapeDtypeStruct((nb, CONV_W - 1, CONV_DIM), f32))
        scratch.append(pltpu.VMEM((SUBLANES + tm, CONV_DIM), f32))
    return pl.pallas_call(
        functools.partial(_inproj_kernel, seq_major=seq_major, sub=sub, conv=conv),
        grid=(nb, nt),
        in_specs=in_specs, out_specs=out_specs, out_shape=out_shape, scratch_shapes=scratch,
        compiler_params=_params("parallel", "arbitrary" if conv else "parallel"),
        name="in_proj",
    )(x, mod, g, w, *(conv_wb or ()))


def _split3(x):
    hi = x.astype(bf16)
    r1 = x - hi.astype(f32)
    mid = r1.astype(bf16)
    lo = (r1 - mid.astype(f32)).astype(bf16)
    return hi, mid, lo


def _dot_sel_lhs(sel16, x):
    return sum(jnp.dot(sel16, p, preferred_element_type=f32) for p in _split3(x))


def _dot_sel_rhs(x, sel16):
    hi, mid, _ = _split3(x)
    return jnp.dot(hi, sel16, preferred_element_type=f32) + jnp.dot(mid, sel16, preferred_element_type=f32)


def _gated_group_norm(y, z, ng):
    y = y * _silu(z)
    gw = D_SSD // GROUPS
    parts = []
    for g in range(GROUPS):
        yg = y[:, g * gw:(g + 1) * gw]
        parts.append(yg * lax.rsqrt(jnp.mean(yg * yg, axis=-1, keepdims=True) + EPS))
    return jnp.concatenate(parts, axis=-1) * ng


def _ssd_prompt_kernel(act_ref, z_ref, dt_ref, dtb_ref, alog_ref, dexp_ref, ng_ref, y_ref, st_ref, h_ref, *, cps):
    T = SSD_CHUNK
    gw = HEADS_PER_GROUP * HEAD_DIM

    @pl.when(pl.program_id(1) == 0)
    def _():
        h_ref[...] = jnp.zeros_like(h_ref)

    row = lax.broadcasted_iota(jnp.int32, (T, T), 0)
    col = lax.broadcasted_iota(jnp.int32, (T, T), 1)
    tri = row >= col
    tri16 = jnp.where(tri, 1.0, 0.0).astype(bf16)
    low_half = lax.broadcasted_iota(jnp.int32, (T, LANES), 1) < HEAD_DIM
    a_neg = -jnp.exp(alog_ref[...])

    lane_head = lax.broadcasted_iota(jnp.int32, (DT_PAD, D_SSD), 1) // HEAD_DIM
    sel16 = jnp.where(lax.broadcasted_iota(jnp.int32, (DT_PAD, D_SSD), 0) == lane_head, 1.0, 0.0).astype(bf16)

    def spread(q):
        return _dot_sel_rhs(q, sel16)

    def state_free(ci):
        r0 = ci * T
        act = act_ref[0, r0:r0 + T, :]
        xs = act[:, :D_SSD]
        bm = act[:, D_SSD:D_SSD + GROUPS * STATE]
        cm = act[:, D_SSD + GROUPS * STATE:]
        dtv = jax.nn.softplus(dt_ref[0, r0:r0 + T, :] + dtb_ref[...])
        a_cs = _dot_sel_lhs(tri16, dtv * a_neg)
        bg16s = [bm[:, g * STATE:(g + 1) * STATE].astype(bf16) for g in range(GROUPS)]
        cg16s = [cm[:, g * STATE:(g + 1) * STATE].astype(bf16) for g in range(GROUPS)]
        gmats = [lax.dot_general(cg16s[g], bg16s[g], NT_DIMS, preferred_element_type=f32)
                 for g in range(GROUPS)]
        yield
        a_last = a_cs[T - 1:T, :]
        a_cs_t = a_cs.T
        rep = spread(jnp.concatenate([dtv, jnp.exp(a_last - a_cs), jnp.exp(a_cs)], axis=0))
        yield
        x = xs * rep[:T]
        x16 = x.astype(bf16)
        xd16 = (x * rep[T:2 * T]).astype(bf16)
        e_cs = rep[2 * T:]
        yield
        y_part = []
        for g in range(GROUPS):
            for pr in range(HEADS_PER_GROUP // 2):
                s_pair = []
                for q in range(2):
                    h = g * HEADS_PER_GROUP + 2 * pr + q
                    seg = a_cs[:, h:h + 1] - a_cs_t[h:h + 1, :]
                    lmat = jnp.exp(jnp.where(tri, seg, -jnp.inf))
                    s_pair.append((gmats[g] * lmat).astype(bf16))
                lanes = slice(g * gw + pr * LANES, g * gw + (pr + 1) * LANES)
                both = jnp.dot(jnp.concatenate(s_pair, axis=0), x16[:, lanes], preferred_element_type=f32)
                yield
                y_part.append(jnp.where(low_half, both[:T], both[T:]))
        y_free = jnp.concatenate(y_part, axis=-1) + xs * dexp_ref[...]
        return y_free, e_cs, jnp.exp(a_last), xd16, bg16s, cg16s

    def state_step(ci, y_free, e_cs, e_last, xd16, bg16s, cg16s):
        r0 = ci * T
        y_offs = []
        for g in range(GROUPS):
            hp = h_ref[g]
            y_offs.append(lax.dot_general(cg16s[g], hp.astype(bf16), NT_DIMS, preferred_element_type=f32))
            upd = lax.dot_general(xd16[:, g * gw:(g + 1) * gw], bg16s[g], TN_DIMS, preferred_element_type=f32)
            for hh in range(HEADS_PER_GROUP):
                h = g * HEADS_PER_GROUP + hh
                rows = slice(hh * HEAD_DIM, (hh + 1) * HEAD_DIM)
                h_ref[g, rows, :] = e_last[:, h:h + 1] * hp[rows] + upd[rows]
        y = y_free + jnp.concatenate(y_offs, axis=-1) * e_cs
        y_ref[0, r0:r0 + T, :] = _gated_group_norm(y, z_ref[0, r0:r0 + T, :], ng_ref[...])

    gens = [state_free(ci) for ci in range(cps)]
    free = [None] * cps
    while any(f is None for f in free):
        for ci in range(cps):
            if free[ci] is None:
                try:
                    next(gens[ci])
                except StopIteration as done:
                    free[ci] = done.value
    for ci in range(cps):
        state_step(ci, *free[ci])

    @pl.when(pl.program_id(1) == pl.num_programs(1) - 1)
    def _():
        st_ref[0] = h_ref[...].reshape(HEADS, HEAD_DIM, STATE)


def _ssd_prompt(act, z, dt, dtb, alog, dexp, ng, *, cps=8):
    nb, L, _ = act.shape
    rows = cps * SSD_CHUNK
    blk = lambda wd: pl.BlockSpec((1, rows, wd), lambda b, c: (b, c, 0))
    return pl.pallas_call(
        functools.partial(_ssd_prompt_kernel, cps=cps),
        grid=(nb, L // rows),
        in_specs=[blk(CONV_DIM), blk(D_SSD), blk(DT_PAD), _const_spec((1, DT_PAD)), _const_spec((1, DT_PAD)),
                  _const_spec((1, D_SSD)), _const_spec((1, D_SSD))],
        out_specs=[blk(D_SSD), pl.BlockSpec((1, HEADS, HEAD_DIM, STATE), lambda b, c: (b, 0, 0, 0))],
        out_shape=[jax.ShapeDtypeStruct((nb, L, D_SSD), f32),
                   jax.ShapeDtypeStruct((nb, HEADS, HEAD_DIM, STATE), f32)],
        scratch_shapes=[pltpu.VMEM((GROUPS, HEADS_PER_GROUP * HEAD_DIM, STATE), f32)],
        compiler_params=_params("parallel", "arbitrary"),
        name="ssd_prompt",
    )(act, z, dt, dtb, alog, dexp, ng)


def _ssd_sample_kernel(xbc_ref, z_ref, dt_ref, cbuf_ref, st_ref, cw_ref, cb_ref, dtb_ref, alog_ref,
                       dexp_ref, ng_ref, y_ref, stn_ref, cn_ref, dtot_ref, *, L, nb):
    gw = HEADS_PER_GROUP * HEAD_DIM
    full = [cbuf_ref[j] for j in range(CONV_W - 1)] + [xbc_ref[t] for t in range(L)]
    for j in range(CONV_W - 1):
        cn_ref[j] = full[L + j]
    a_neg = -jnp.exp(alog_ref[...])
    xs, bm, cm, dtv, acs = [], [], [], [], []
    run = None
    for t in range(L):
        conv = cb_ref[...]
        for k in range(CONV_W):
            conv = conv + full[t + k] * cw_ref[k:k + 1, :]
        act = _silu(conv)
        xs.append(act[:, :D_SSD])
        bm.append(act[:, D_SSD:D_SSD + GROUPS * STATE])
        cm.append(act[:, D_SSD + GROUPS * STATE:])
        d = jax.nn.softplus(dt_ref[t] + dtb_ref[...])
        dtv.append(d)
        run = d * a_neg if run is None else run + d * a_neg
        acs.append(run)
    a_tot = acs[L - 1]
    dtot_ref[...] = jnp.exp(a_tot)

    lane_head = lax.broadcasted_iota(jnp.int32, (DT_PAD, D_SSD), 1) // HEAD_DIM
    sel16 = jnp.where(lax.broadcasted_iota(jnp.int32, (DT_PAD, D_SSD), 0) == lane_head, 1.0, 0.0).astype(bf16)
    pairs = [(t, s) for t in range(L) for s in range(t)]
    factors = (dtv + [jnp.exp(a_tot - acs[t]) for t in range(L)] + [jnp.exp(acs[t]) for t in range(L)]
               + [jnp.exp(acs[t] - acs[s]) for t, s in pairs])
    rep = _dot_sel_rhs(jnp.concatenate(factors, axis=0), sel16)
    piece = lambda i: rep[i * nb:(i + 1) * nb]
    x = [xs[t] * piece(t) for t in range(L)]
    xd_stack = jnp.concatenate([x[t] * piece(L + t) for t in range(L)], axis=0)
    e_cs = [piece(2 * L + t) for t in range(L)]
    decay = {ts: piece(3 * L + i) for i, ts in enumerate(pairs)}

    in_group0 = lax.broadcasted_iota(jnp.int32, (nb, D_SSD), 1) < gw
    y_intra = []
    for t in range(L):
        acc = None
        for s in range(t + 1):
            cb_dot = [jnp.sum(cm[t][:, g * STATE:(g + 1) * STATE] * bm[s][:, g * STATE:(g + 1) * STATE],
                              axis=-1, keepdims=True) for g in range(GROUPS)]
            w = jnp.where(in_group0, cb_dot[0], cb_dot[1])
            term = w * x[s] if s == t else w * decay[(t, s)] * x[s]
            acc = term if acc is None else acc + term
        y_intra.append(acc)

    c_stack = [jnp.concatenate([cm[t][:, g * STATE:(g + 1) * STATE] for t in range(L)], axis=0).astype(bf16)
               for g in range(GROUPS)]
    b_stack = [jnp.concatenate([bm[t][:, g * STATE:(g + 1) * STATE] for t in range(L)], axis=0).astype(bf16)
               for g in range(GROUPS)]
    seq_of_row = lax.broadcasted_iota(jnp.int32, (L * nb, gw), 0) & (nb - 1)

    def per_seq(b, yoff):
        mine = seq_of_row == b
        drow = dtot_ref[pl.ds(b, 1), :]
        out = []
        for g in range(GROUPS):
            h0 = st_ref[b, g]
            r = lax.dot_general(c_stack[g], h0.astype(bf16), NT_DIMS, preferred_element_type=f32)
            out.append(yoff[g] + jnp.where(mine, r, 0.0))
            xm = jnp.where(mine, xd_stack[:, g * gw:(g + 1) * gw], 0.0).astype(bf16)
            upd = lax.dot_general(xm, b_stack[g], TN_DIMS, preferred_element_type=f32)
            for hh in range(HEADS_PER_GROUP):
                h = g * HEADS_PER_GROUP + hh
                rows = slice(hh * HEAD_DIM, (hh + 1) * HEAD_DIM)
                stn_ref[b, g, rows, :] = drow[:, h:h + 1] * h0[rows] + upd[rows]
        return tuple(out)

    yoff = lax.fori_loop(0, nb, per_seq, tuple(jnp.zeros((L * nb, gw), f32) for _ in range(GROUPS)),
                         unroll=8)

    for t in range(L):
        y_off = jnp.concatenate([yoff[g][t * nb:(t + 1) * nb] for g in range(GROUPS)], axis=-1)
        y = y_intra[t] + y_off * e_cs[t] + xs[t] * dexp_ref[...]
        y_ref[t] = _gated_group_norm(y, z_ref[t], ng_ref[...])


def _ssd_sample(xbc, z, dt, cbuf, st, cw, cb, dtb, alog, dexp, ng, *, nb=16):
    L, B, _ = xbc.shape
    assert nb & (nb - 1) == 0 and B % nb == 0
    tblk = lambda n, wd: pl.BlockSpec((n, nb, wd), lambda i: (0, i, 0))
    gw = HEADS_PER_GROUP * HEAD_DIM
    st = st.reshape(B, GROUPS, gw, STATE)
    st_spec = pl.BlockSpec((nb, GROUPS, gw, STATE), lambda i: (i, 0, 0, 0))
    return pl.pallas_call(
        functools.partial(_ssd_sample_kernel, L=L, nb=nb),
        grid=(B // nb,),
        in_specs=[tblk(L, CONV_DIM), tblk(L, D_SSD), tblk(L, DT_PAD), tblk(CONV_W - 1, CONV_DIM), st_spec,
                  _const_spec((CONV_W, CONV_DIM)), _const_spec((1, CONV_DIM)),
                  _const_spec((1, DT_PAD)), _const_spec((1, DT_PAD)),
                  _const_spec((1, D_SSD)), _const_spec((1, D_SSD))],
        out_specs=[tblk(L, D_SSD), st_spec, tblk(CONV_W - 1, CONV_DIM)],
        out_shape=[jax.ShapeDtypeStruct((L, B, D_SSD), f32),
                   jax.ShapeDtypeStruct((B, GROUPS, gw, STATE), f32),
                   jax.ShapeDtypeStruct((CONV_W - 1, B, CONV_DIM), f32)],
        scratch_shapes=[pltpu.VMEM((nb, DT_PAD), f32)],
        compiler_params=_params("parallel"),
        name="ssd_sample",
    )(xbc, z, dt, cbuf, st, cw, cb, dtb, alog, dexp, ng)


def _s5_param_kernel(lr_ref, li_ref, ls_ref, br_ref, bi_ref, cr_ref, ci_ref,
                     ar_ref, ai_ref, bblk_ref, cblk_ref, b_scr, c_scr):
    lr, li = lr_ref[...], li_ref[...]
    step = jnp.exp(ls_ref[...])
    mag = jnp.exp(lr * step)
    abr = mag * jnp.cos(li * step)
    abi = mag * jnp.sin(li * step)
    nr, ni = abr - 1.0, abi
    den = lr * lr + li * li
    fr = (nr * lr + ni * li) / den
    fi = (ni * lr - nr * li) / den
    br, bi = br_ref[...], bi_ref[...]
    bbr = fr * br - fi * bi
    bbi = fr * bi + fi * br
    b_scr[...] = jnp.zeros_like(b_scr)
    c_scr[...] = jnp.zeros_like(c_scr)
    gps = S5_GROUPS // S5_SLABS
    ns = S5_SLAB_STATE
    for g in range(S5_GROUPS):
        s, gl = divmod(g, gps)
        ch = slice(gl * S5_CH, (gl + 1) * S5_CH)
        st = slice(gl * S5_STATE, (gl + 1) * S5_STATE)
        st_im = slice(ns + gl * S5_STATE, ns + (gl + 1) * S5_STATE)
        ar_ref[s, :, st] = jnp.broadcast_to(abr[g], (SUBLANES, S5_STATE))
        ai_ref[s, :, st] = jnp.broadcast_to(abi[g], (SUBLANES, S5_STATE))
        b_scr[s, ch, st] = bbr[g]
        b_scr[s, ch, st_im] = bbi[g]
        c_scr[s, st, ch] = cr_ref[g].T
        c_scr[s, st_im, ch] = -ci_ref[g].T
    bblk_ref[...] = b_scr[...].astype(bf16)
    cblk_ref[...] = c_scr[...].astype(bf16)


def _s5_params(a_re, a_im, log_step, b_re_t, b_im_t, c_re, c_im):
    g, p = a_re.shape
    ns = S5_SLAB_STATE
    return pl.pallas_call(
        _s5_param_kernel,
        out_shape=[jax.ShapeDtypeStruct((S5_SLABS, SUBLANES, ns), f32),
                   jax.ShapeDtypeStruct((S5_SLABS, SUBLANES, ns), f32),
                   jax.ShapeDtypeStruct((S5_SLABS, LANES, 2 * ns), bf16),
                   jax.ShapeDtypeStruct((S5_SLABS, 2 * ns, LANES), bf16)],
        scratch_shapes=[pltpu.VMEM((S5_SLABS, LANES, 2 * ns), f32),
                        pltpu.VMEM((S5_SLABS, 2 * ns, LANES), f32)],
        name="s5_params",
    )(a_re.reshape(g, 1, p), a_im.reshape(g, 1, p), log_step.reshape(g, 1, 1), b_re_t, b_im_t, c_re, c_im)


def _gelu(x):
    return 0.5 * x * (1.0 + lax.erf(x * (2.0 ** -0.5)))


def _s5_slabs_interleaved(ut_ref, hbuf_ref, hst_ref, g_ref, ar_ref, ai_ref, bblk_ref, cblk_ref, dsk_ref, tl,
                          row_block=256, n_stages=8):
    ns = S5_SLAB_STATE
    nb = SUBLANES
    rows = nb * tl

    def lanes(s):
        return slice(s * LANES, (s + 1) * LANES)

    def bu_stage(slabs):
        for s in slabs:
            for r in range(0, rows, row_block):
                us = ut_ref[s, r:r + row_block, :]
                hbuf_ref[s, nb + r:nb + r + row_block, :] = jnp.dot(us.astype(bf16), bblk_ref[s],
                                                                    preferred_element_type=f32)
                yield

    def scan_stage(slabs):
        carry = [(hst_ref[s, :, :ns], hst_ref[s, :, ns:]) for s in slabs]
        coef = [(ar_ref[s], ai_ref[s]) for s in slabs]
        for t in range(tl):
            r = nb + nb * t
            for i, s in enumerate(slabs):
                (pr, pi), (ar, ai) = carry[i], coef[i]
                nr = ar * pr - ai * pi + hbuf_ref[s, r:r + nb, :ns]
                ni = ar * pi + ai * pr + hbuf_ref[s, r:r + nb, ns:]
                hbuf_ref[s, r:r + nb, :ns] = nr
                hbuf_ref[s, r:r + nb, ns:] = ni
                carry[i] = (nr, ni)
            if (t + 1) % (tl // n_stages) == 0:
                yield
        for i, s in enumerate(slabs):
            hst_ref[s, :, :ns] = carry[i][0]
            hst_ref[s, :, ns:] = carry[i][1]

    def y_stage(slabs):
        for s in slabs:
            for r in range(0, rows, row_block):
                h16 = hbuf_ref[s, nb + r:nb + r + row_block, :].astype(bf16)
                ys = jnp.dot(h16, cblk_ref[s], preferred_element_type=f32)
                ys = ys + dsk_ref[:, lanes(s)] * ut_ref[s, r:r + row_block, :]
                g_ref[s, r:r + row_block, :] = _gelu(ys)
                yield

    half = S5_SLABS // 2
    first, second = tuple(range(half)), tuple(range(half, S5_SLABS))
    _interleave(bu_stage(first))
    _interleave(scan_stage(first), bu_stage(second))
    _interleave(scan_stage(second), y_stage(first))
    _interleave(y_stage(second))


def _s5_scan_vmem(hbuf_ref, hst_ref, ar_ref, ai_ref, s, nb, rows):
    ns = S5_SLAB_STATE
    hbuf_ref[s, 0:nb, :] = hst_ref[s]
    ar, ai = ar_ref[s], ai_ref[s]

    def body(i, carry):
        j = pl.multiple_of(nb + i * SUBLANES, SUBLANES)
        prev = hbuf_ref[s, pl.ds(j - nb, SUBLANES), :]
        cur = hbuf_ref[s, pl.ds(j, SUBLANES), :]
        pr, pi = prev[:, :ns], prev[:, ns:]
        hbuf_ref[s, pl.ds(j, SUBLANES), :ns] = ar * pr - ai * pi + cur[:, :ns]
        hbuf_ref[s, pl.ds(j, SUBLANES), ns:] = ar * pi + ai * pr + cur[:, ns:]
        return carry

    lax.fori_loop(0, rows // SUBLANES, body, 0, unroll=min(8, nb // SUBLANES))
    hst_ref[s] = hbuf_ref[s, rows:rows + nb, :]


S5_N_IN = 10


def _s5_kernel(*refs, nb, tl, batch_major, n_cast):
    (u_ref, re0_ref, im0_ref, ar_ref, ai_ref, bblk_ref, cblk_ref, dsk_ref, wglu_ref,
     bglu_ref) = refs[:S5_N_IN]
    cast_in = refs[S5_N_IN:S5_N_IN + n_cast]
    y_ref, ren_ref, imn_ref = refs[S5_N_IN + n_cast:S5_N_IN + n_cast + 3]
    cast_out = refs[S5_N_IN + n_cast + 3:S5_N_IN + 2 * n_cast + 3]
    ut_ref, hbuf_ref, hst_ref, g_ref = refs[S5_N_IN + 2 * n_cast + 3:]
    for w_ref, w16_ref in zip(cast_in, cast_out):
        w16_ref[...] = w_ref[...].astype(bf16)
    step = pl.program_id(0)
    rows = nb * tl
    ns = S5_SLAB_STATE

    @pl.when(step == 0)
    def _():
        for s in range(S5_SLABS):
            hst_ref[s, :, :ns] = re0_ref[:, s * ns:(s + 1) * ns]
            hst_ref[s, :, ns:] = im0_ref[:, s * ns:(s + 1) * ns]

    for s in range(S5_SLABS):
        sl = slice(s * LANES, (s + 1) * LANES)
        if batch_major:
            for b in range(nb):
                ut_ref[s, pl.ds(b, tl, stride=nb), :] = u_ref[b, :, sl]
        else:
            ut_ref[s] = u_ref[:, :, sl].reshape(rows, LANES)

    if nb == SUBLANES:
        _s5_slabs_interleaved(ut_ref, hbuf_ref, hst_ref, g_ref, ar_ref, ai_ref, bblk_ref, cblk_ref, dsk_ref, tl)
    else:
        for s in range(S5_SLABS):
            sl = slice(s * LANES, (s + 1) * LANES)
            hbuf_ref[s, nb:nb + rows, :] = jnp.dot(ut_ref[s].astype(bf16), bblk_ref[s],
                                                   preferred_element_type=f32)
            _s5_scan_vmem(hbuf_ref, hst_ref, ar_ref, ai_ref, s, nb, rows)
            ys = jnp.dot(hbuf_ref[s, nb:nb + rows, :].astype(bf16), cblk_ref[s], preferred_element_type=f32)
            g_ref[s] = _gelu(ys + dsk_ref[:, sl] * ut_ref[s])

    g = jnp.concatenate([g_ref[s] for s in range(S5_SLABS)], axis=-1)
    gate = jnp.dot(g.astype(bf16), wglu_ref[...], preferred_element_type=f32) + bglu_ref[...]
    out = g * jax.nn.sigmoid(gate)
    if batch_major:
        for s in range(S5_SLABS):
            sl = slice(s * LANES, (s + 1) * LANES)
            ut_ref[s] = out[:, sl]
            for b in range(nb):
                y_ref[b, :, sl] = ut_ref[s, pl.ds(b, tl, stride=nb), :]
    else:
        y_ref[...] = out.reshape(tl, nb, D_S5)

    @pl.when(step == pl.num_programs(0) - 1)
    def _():
        for s in range(S5_SLABS):
            ren_ref[:, s * ns:(s + 1) * ns] = hst_ref[s, :, :ns]
            imn_ref[:, s * ns:(s + 1) * ns] = hst_ref[s, :, ns:]


def _s5(u, re0, im0, ar, ai, bblk, cblk, dsk, wglu, bglu, *, tl, batch_major, cast=()):
    if batch_major:
        nb, L, _ = u.shape
        u_spec = pl.BlockSpec((nb, tl, D_S5), lambda i: (0, i, 0))
    else:
        L, nb, _ = u.shape
        u_spec = pl.BlockSpec((tl, nb, D_S5), lambda i: (i, 0, 0))
    rows = nb * tl
    steps = L // tl
    nstate = S5_GROUPS * S5_STATE
    st_spec = _const_spec((nb, nstate))
    cast_specs = [pl.BlockSpec((w.shape[0] // steps, w.shape[1]), lambda i: (i, 0)) for w in cast]
    assert all(w.shape[0] % (steps * 2 * SUBLANES) == 0 for w in cast)
    return pl.pallas_call(
        functools.partial(_s5_kernel, nb=nb, tl=tl, batch_major=batch_major, n_cast=len(cast)),
        grid=(steps,),
        in_specs=[u_spec, st_spec, st_spec,
                  _const_spec((S5_SLABS, SUBLANES, S5_SLAB_STATE)),
                  _const_spec((S5_SLABS, SUBLANES, S5_SLAB_STATE)),
                  _const_spec((S5_SLABS, LANES, 2 * S5_SLAB_STATE)),
                  _const_spec((S5_SLABS, 2 * S5_SLAB_STATE, LANES)),
                  _const_spec((1, D_S5)), _const_spec((D_S5, D_S5)), _const_spec((1, D_S5))] + cast_specs,
        out_specs=[u_spec, st_spec, st_spec] + cast_specs,
        out_shape=[jax.ShapeDtypeStruct(u.shape, f32),
                   jax.ShapeDtypeStruct((nb, nstate), f32),
                   jax.ShapeDtypeStruct((nb, nstate), f32)]
                  + [jax.ShapeDtypeStruct(w.shape, bf16) for w in cast],
        scratch_shapes=[pltpu.VMEM((S5_SLABS, rows, LANES), f32),
                        pltpu.VMEM((S5_SLABS, nb + rows, 2 * S5_SLAB_STATE), f32),
                        pltpu.VMEM((S5_SLABS, nb, 2 * S5_SLAB_STATE), f32),
                        pltpu.VMEM((S5_SLABS, rows, LANES), f32)],
        compiler_params=_params("arbitrary"),
        name="s5_mixer",
    )(u, re0, im0, ar, ai, bblk, cblk, dsk, wglu, bglu, *cast)


def _ffn_stages(x, ys, y5, mods, n2g, nfg, wo_ref, wg_ref, wu_ref, wd_ref, write_out, ff_chunk):
    g1, sh2, sc2, g2, shf, scf = mods
    att = jnp.dot(ys.astype(bf16), wo_ref[:D_SSD, :], preferred_element_type=f32)
    att = att + jnp.dot(y5.astype(bf16), wo_ref[D_SSD:, :], preferred_element_type=f32)
    yield
    x1 = x + _by_row(lambda v, s: v * s, att, g1)
    v = _rms_mod(x1, n2g, sc2, sh2).astype(bf16)
    ff = None
    for o in range(0, D_FF, ff_chunk):
        gate = jnp.dot(v, wg_ref[:, o:o + ff_chunk], preferred_element_type=f32)
        up = jnp.dot(v, wu_ref[:, o:o + ff_chunk], preferred_element_type=f32)
        hid = (_silu(gate) * up).astype(bf16)
        part = jnp.dot(hid, wd_ref[o:o + ff_chunk, :], preferred_element_type=f32)
        ff = part if ff is None else ff + part
        yield
    x2 = x1 + _by_row(lambda v, s: v * s, ff, g2)
    write_out(_rms_mod(x2, nfg, scf, shf))


def _ffn_kernel(x_ref, ys_ref, y5_ref, mod_ref, modf_ref, n2g_ref, nfg_ref, wo_ref, wg_ref, wu_ref, wd_ref,
                o_ref, *, seq_major, ff_chunk, sub):
    mods = (_mod_row(mod_ref, 2, seq_major), _mod_row(mod_ref, 3, seq_major), _mod_row(mod_ref, 4, seq_major),
            _mod_row(mod_ref, 5, seq_major), _mod_row(modf_ref, 0, seq_major), _mod_row(modf_ref, 1, seq_major))
    x = _rows_in(x_ref, seq_major)
    rows = x.shape[0]
    tiles = {}

    def stages(r0):
        def write_out(y):
            if seq_major:
                tiles[r0] = y
            else:
                o_ref[0, r0:r0 + sub, :] = y
        return _ffn_stages(x[r0:r0 + sub], ys_ref[0, r0:r0 + sub, :], y5_ref[0, r0:r0 + sub, :], mods,
                           n2g_ref[...], nfg_ref[...], wo_ref, wg_ref, wu_ref, wd_ref, write_out, ff_chunk)

    _interleave(*[stages(r0) for r0 in range(0, rows, sub)])
    if seq_major:
        s, t, d = o_ref.shape
        y = jnp.concatenate([tiles[r0] for r0 in range(0, rows, sub)], axis=0)
        o_ref[...] = jnp.swapaxes(y.reshape(t, s, d), 0, 1)


def _ffn(x, ys, y5, mod, modf, n2g, nfg, wo, wg, wu, wd, *, tm, sub, seq_major, n_seq, first_seq, ff_chunk=256):
    if seq_major:
        x_spec = _const_spec(x.shape)
        nb, rows, d = 1, x.shape[0] * x.shape[1], x.shape[2]
        assert tm == rows
    else:
        nb, rows, d = x.shape
        x_spec = pl.BlockSpec((1, tm, d), lambda i, j: (i, j, 0))
    nt = rows // tm
    mod_spec = _mod_spec(mod, n_seq, first_seq)
    modf_spec = _mod_spec(modf, n_seq, first_seq)
    blk = lambda wd_: pl.BlockSpec((1, tm, wd_), lambda i, j: (i, j, 0))
    single = dict(pipeline_mode=pl.Buffered(1))
    wspec = lambda shape: pl.BlockSpec(shape, lambda i, j: (0, 0), **single)
    return pl.pallas_call(
        functools.partial(_ffn_kernel, seq_major=seq_major, ff_chunk=ff_chunk, sub=sub),
        grid=(nb, nt),
        in_specs=[x_spec, blk(D_SSD), blk(D_S5), mod_spec, modf_spec,
                  _const_spec((1, d)), _const_spec((1, d)),
                  wspec((d, d)), wspec((d, D_FF)), wspec((d, D_FF)), wspec((D_FF, d))],
        out_specs=x_spec,
        out_shape=jax.ShapeDtypeStruct(x.shape, f32),
        compiler_params=_params("parallel", "parallel"),
        name="out_ffn",
    )(x, ys, y5, mod, modf, n2g, nfg, wo, wg, wu, wd)


def kernel(x_prompt, x_sample, c_prompt, c_sample, state_ssd, state_conv, state_s5_re, state_s5_im, w_ada, b_ada, norm1_g, w_in, conv_w, conv_b, ssd_dt_bias, ssd_A_log, ssd_D, ssd_norm_g, s5_A_re, s5_A_im, s5_log_step, s5_B_re, s5_B_im, s5_C_re, s5_C_im, s5_D, w_glu, b_glu, w_out, norm2_g, w_ffn_gate, w_ffn_up, w_ffn_down, w_ada_f, b_ada_f, normf_g):
    assert w_ada.shape[0] == 1, "single-layer stack"
    bp, seq, d = x_prompt.shape
    bs, dseq, _ = x_sample.shape

    c_all = jnp.concatenate([c_sample, c_prompt], axis=0)
    mod = _ada_mod(c_all, w_ada[0], b_ada[0])
    modf = _ada_mod(c_all, w_ada_f, b_ada_f)
    seqs_s = dict(n_seq=bs, first_seq=0)
    seqs_p = dict(n_seq=bp, first_seq=bs)

    w_in_p = w_in[0].T
    pad_h = lambda v: jnp.concatenate([v, jnp.zeros((DT_PAD - HEADS,), f32)]).reshape(1, DT_PAD)
    dtb = pad_h(ssd_dt_bias[0])
    alog = pad_h(ssd_A_log[0])
    dexp = jnp.repeat(ssd_D[0], HEAD_DIM).reshape(1, D_SSD)
    ng = ssd_norm_g[0].reshape(1, D_SSD)
    cw, cb = conv_w[0], conv_b[0].reshape(1, CONV_DIM)

    ar, ai, bblk, cblk = _s5_params(s5_A_re[0], s5_A_im[0], s5_log_step[0],
                                    s5_B_re[0].transpose(0, 2, 1), s5_B_im[0].transpose(0, 2, 1),
                                    s5_C_re[0], s5_C_im[0])
    dsk = s5_D[0].reshape(1, D_S5)
    wglu = w_glu[0].astype(bf16)
    bglu = b_glu[0].reshape(1, D_S5)
    n1g, n2g, nfg = norm1_g[0].reshape(1, d), norm2_g[0].reshape(1, d), normf_g.reshape(1, d)
    nstate = S5_GROUPS * S5_STATE

    z, act, u5, dt, conv_p = _inproj(x_prompt, mod, n1g, w_in_p, (cw, cb), tm=1024, sub=512, seq_major=False,
                                     **seqs_p)
    y_ssd, ssd_p = _ssd_prompt(act, z, dt, dtb, alog, dexp, ng)
    zeros_st = jnp.zeros((bp, nstate), f32)
    y_s5, re_p, im_p, wo, wg, wu, wd = _s5(
        u5, zeros_st, zeros_st, ar, ai, bblk, cblk, dsk, wglu, bglu, tl=128, batch_major=True,
        cast=(w_out[0], w_ffn_gate[0], w_ffn_up[0], w_ffn_down[0]))
    y_prompt = _ffn(x_prompt, y_ssd, y_s5, mod, modf, n2g, nfg, wo, wg, wu, wd, tm=1024, sub=512,
                    seq_major=False, **seqs_p)

    rows_s = dseq * bs
    steps = lambda a: a.reshape(dseq, bs, a.shape[-1])
    flat = lambda a: a.reshape(1, rows_s, a.shape[-1])
    z, xbc, u5, dt = _inproj(x_sample, mod, n1g, w_in_p, tm=rows_s, sub=rows_s // 2, seq_major=True, **seqs_s)
    y_ssd, ssd_s, conv_s = _ssd_sample(steps(xbc), steps(z), steps(dt), state_conv[0].transpose(1, 0, 2),
                                       state_ssd[0], cw, cb, dtb, alog, dexp, ng)
    y_s5, re_s, im_s = _s5(steps(u5), state_s5_re[0].reshape(bs, nstate), state_s5_im[0].reshape(bs, nstate),
                           ar, ai, bblk, cblk, dsk, wglu, bglu, tl=dseq, batch_major=False)
    y_sample = _ffn(x_sample, flat(y_ssd), flat(y_s5), mod, modf, n2g, nfg, wo, wg, wu, wd,
                    tm=rows_s, sub=rows_s, seq_major=True, **seqs_s)

    g5 = (S5_GROUPS, S5_STATE)
    return (y_prompt, y_sample,
            ssd_p[None], ssd_s.reshape((1,) + state_ssd.shape[1:]),
            conv_p[None], conv_s.transpose(1, 0, 2)[None],
            re_p.reshape((1, bp) + g5), re_s.reshape((1, bs) + g5),
            im_p.reshape((1, bp) + g5), im_s.reshape((1, bs) + g5))
```

```python
import functools

import jax
import jax.numpy as jnp
from jax import lax
from jax.experimental import pallas as pl
from jax.experimental.pallas import tpu as pltpu

f32 = jnp.float32
bf16 = jnp.bfloat16

D_MODEL = 1024
D_SSD = 512
HEAD_DIM = 64
HEADS = 8
GROUPS = 2
HEADS_PER_GROUP = HEADS // GROUPS
STATE = 128
CONV_W = 4
CONV_DIM = D_SSD + 2 * GROUPS * STATE
D_S5 = 512
S5_CH = 16
S5_GROUPS = 32
S5_STATE = 64
D_FF = 2816
N_ADA = 6
EPS = 1e-6

LANES = 128
SUBLANES = 8
SSD_CHUNK = 128
S5_SLABS = D_S5 // LANES
S5_SLAB_STATE = (S5_GROUPS // S5_SLABS) * S5_STATE
DT_PAD = LANES
VMEM_LIMIT = 56 * 1024 * 1024

NT_DIMS = (((1,), (1,)), ((), ()))
TN_DIMS = (((0,), (0,)), ((), ()))


def _silu(x):
    return x * jax.nn.sigmoid(x)


def _interleave(*gens, lead=0):
    pending, live, rnd = list(gens), [], 0
    while pending or live:
        while pending and rnd >= lead * (len(gens) - len(pending)):
            live.append(pending.pop(0))
        for g in list(live):
            try:
                next(g)
            except StopIteration:
                live.remove(g)
        rnd += 1


def _by_row(fn, v, *ms):
    r, d = v.shape
    m_rows = ms[0].shape[0]
    if m_rows in (1, r):
        return fn(v, *ms)
    out = fn(v.reshape(r // m_rows, m_rows, d), *[m[None] for m in ms])
    return out.reshape(r, d)


def _rms_mod(x, g, sc, sh):
    y = x * lax.rsqrt(jnp.mean(x * x, axis=-1, keepdims=True) + EPS)
    return _by_row(lambda v, s, t: v * (1.0 + s) + t, y * g, sc, sh)


def _mod_row(mod_ref, i, per_row):
    return mod_ref[i] if per_row else mod_ref[i, pl.ds(pl.program_id(0), 1), :]


def _mod_spec(mod, n_seq, first_seq):
    assert first_seq % n_seq == 0
    return pl.BlockSpec((mod.shape[0], n_seq, mod.shape[2]), lambda *_: (0, first_seq // n_seq, 0))


def _const_spec(shape):
    nd = len(shape)
    return pl.BlockSpec(shape, lambda *_: (0,) * nd)


def _params(*sem):
    return pltpu.CompilerParams(dimension_semantics=sem, vmem_limit_bytes=VMEM_LIMIT)


ADA_K_BLOCK = 256


def _ada_kernel(c_ref, w_ref, b_ref, o_ref):
    n, _, d = o_ref.shape
    s = _silu(c_ref[...]).astype(bf16)

    @pl.when(pl.program_id(0) == 0)
    def _():
        for j in range(n):
            o_ref[j] = jnp.broadcast_to(b_ref[:, j * d:(j + 1) * d], o_ref.shape[1:])

    for j in range(n):
        o_ref[j] += jnp.dot(s, w_ref[:, j * d:(j + 1) * d].astype(bf16), preferred_element_type=f32)


def _ada_mod(c, w, b):
    m, k = c.shape
    n = w.shape[1] // k
    return pl.pallas_call(
        _ada_kernel,
        grid=(k // ADA_K_BLOCK,),
        in_specs=[pl.BlockSpec((m, ADA_K_BLOCK), lambda i: (0, i)),
                  pl.BlockSpec((ADA_K_BLOCK, n * k), lambda i: (i, 0)),
                  _const_spec((1, n * k))],
        out_specs=_const_spec((n, m, k)),
        out_shape=jax.ShapeDtypeStruct((n, m, k), f32),
        compiler_params=_params("arbitrary"),
        name="ada_mod",
    )(c, w, b.reshape(1, n * k))


def _rows_in(x_ref, seq_major):
    if not seq_major:
        return x_ref[0]
    s, t, d = x_ref.shape
    return jnp.swapaxes(x_ref[...], 0, 1).reshape(s * t, d)


def _conv_silu(ext, cw_ref, cb_ref):
    w = [cw_ref[k:k + 1, :] for k in range(CONV_W)]
    back1 = pltpu.roll(ext, 1, 0)
    older = pltpu.roll(ext * w[1] + back1 * w[0], 2, 0)
    conv = cb_ref[...] + ext[SUBLANES:, :] * w[3] + back1[SUBLANES:, :] * w[2] + older[SUBLANES:, :]
    return _silu(conv)


def _inproj_stages(x, sh, sc, g, outs, after=()):
    u = _rms_mod(x, g, sc, sh).astype(bf16)
    yield
    for write, w16 in outs:
        write(lax.dot_general(u, w16, NT_DIMS, preferred_element_type=f32))
        yield
    for stage in after:
        stage()
        yield


def _inproj_kernel(x_ref, mod_ref, g_ref, w_ref, *rest, seq_major, sub, conv):
    if conv:
        cw_ref, cb_ref, z_ref, act_ref, u5_ref, dt_ref, cn_ref, ext_ref = rest

        @pl.when(pl.program_id(1) == 0)
        def _():
            ext_ref[0:SUBLANES, :] = jnp.zeros((SUBLANES, CONV_DIM), f32)
    else:
        z_ref, xbc_ref, u5_ref, dt_ref = rest
    sh = _mod_row(mod_ref, 0, seq_major)
    sc = _mod_row(mod_ref, 1, seq_major)
    x = _rows_in(x_ref, seq_major)
    rows = x.shape[0]
    T = SSD_CHUNK
    o_dt = D_SSD + CONV_DIM
    w16 = lambda o, width: w_ref[o:o + width, :].astype(bf16)
    w_z, w_xbc, w_u5, w_dt = w16(0, D_SSD), w16(D_SSD, CONV_DIM), w16(o_dt + HEADS, D_S5), w16(o_dt, DT_PAD)

    def stages(r0):
        def to(ref):
            def write(v):
                ref[0, r0:r0 + sub, :] = v
            return write

        def to_ext(v):
            ext_ref[SUBLANES + r0:SUBLANES + r0 + sub, :] = v

        def conv_chunk(c0):
            def stage():
                act_ref[0, c0:c0 + T, :] = _conv_silu(ext_ref[c0:c0 + SUBLANES + T, :], cw_ref, cb_ref)
            return stage

        outs = [(to(z_ref), w_z), (to_ext if conv else to(xbc_ref), w_xbc), (to(u5_ref), w_u5), (to(dt_ref), w_dt)]
        after = [conv_chunk(c0) for c0 in range(r0, r0 + sub, T)] if conv else ()
        return _inproj_stages(x[r0:r0 + sub], sh, sc, g_ref[...], outs, after)

    n_matmul_stages = 5
    _interleave(*[stages(r0) for r0 in range(0, rows, sub)], lead=n_matmul_stages if conv else 0)
    if conv:
        cn_ref[0] = ext_ref[SUBLANES + rows - (CONV_W - 1):SUBLANES + rows, :]
        ext_ref[0:SUBLANES, :] = ext_ref[rows:rows + SUBLANES, :]


def _inproj(x, mod, g, w, conv_wb=None, *, tm, sub, seq_major, n_seq, first_seq):
    if seq_major:
        x_spec = _const_spec(x.shape)
        nb, rows, d = 1, x.shape[0] * x.shape[1], x.shape[2]
        assert tm == rows
    else:
        nb, rows, d = x.shape
        x_spec = pl.BlockSpec((1, tm, d), lambda i, j: (i, j, 0))
    nt = rows // tm
    widths = (D_SSD, CONV_DIM, D_S5, DT_PAD)
    conv = conv_wb is not None
    in_specs = [x_spec, _mod_spec(mod, n_seq, first_seq), _const_spec((1, d)),
                pl.BlockSpec(w.shape, lambda i, j: (0, 0), pipeline_mode=pl.Buffered(1))]
    out_specs = [pl.BlockSpec((1, tm, wd), lambda i, j: (i, j, 0)) for wd in widths]
    out_shape = [jax.ShapeDtypeStruct((nb, rows, wd), f32) for wd in widths]
    scratch = []
    if conv:
        in_specs += [_const_spec((CONV_W, CONV_DIM)), _const_spec((1, CONV_DIM))]
        out_specs.append(pl.BlockSpec((1, CONV_W - 1, CONV_DIM), lambda i, j: (i, 0, 0)))
        out_shape.append(jax.ShapeDtypeStruct((nb, CONV_W - 1, CONV_DIM), f32))
        scratch.append(pltpu.VMEM((SUBLANES + tm, CONV_DIM), f32))
    return pl.pallas_call(
        functools.partial(_inproj_kernel, seq_major=seq_major, sub=sub, conv=conv),
        grid=(nb, nt),
        in_specs=in_specs, out_specs=out_specs, out_shape=out_shape, scratch_shapes=scratch,
        compiler_params=_params("parallel", "arbitrary" if conv else "parallel"),
        name="in_proj",
    )(x, mod, g, w, *(conv_wb or ()))


def _split3(x):
    hi = x.astype(bf16)
    r1 = x - hi.astype(f32)
    mid = r1.astype(bf16)
    lo = (r1 - mid.astype(f32)).astype(bf16)
    return hi, mid, lo


def _dot_sel_lhs(sel16, x):
    return sum(jnp.dot(sel16, p, preferred_element_type=f32) for p in _split3(x))


def _dot_sel_rhs(x, sel16):
    hi, mid, _ = _split3(x)
    return jnp.dot(hi, sel16, preferred_element_type=f32) + jnp.dot(mid, sel16, preferred_element_type=f32)


def _gated_group_norm(y, z, ng):
    y = y * _silu(z)
    gw = D_SSD // GROUPS
    parts = []
    for g in range(GROUPS):
        yg = y[:, g * gw:(g + 1) * gw]
        parts.append(yg * lax.rsqrt(jnp.mean(yg * yg, axis=-1, keepdims=True) + EPS))
    return jnp.concatenate(parts, axis=-1) * ng


def _ssd_prompt_kernel(act_ref, z_ref, dt_ref, dtb_ref, alog_ref, dexp_ref, ng_ref, y_ref, st_ref, h_ref, *, cps):
    T = SSD_CHUNK
    gw = HEADS_PER_GROUP * HEAD_DIM

    @pl.when(pl.program_id(1) == 0)
    def _():
        h_ref[...] = jnp.zeros_like(h_ref)

    row = lax.broadcasted_iota(jnp.int32, (T, T), 0)
    col = lax.broadcasted_iota(jnp.int32, (T, T), 1)
    tri = row >= col
    tri16 = jnp.where(tri, 1.0, 0.0).astype(bf16)
    low_half = lax.broadcasted_iota(jnp.int32, (T, LANES), 1) < HEAD_DIM
    a_neg = -jnp.exp(alog_ref[...])

    lane_head = lax.broadcasted_iota(jnp.int32, (DT_PAD, D_SSD), 1) // HEAD_DIM
    sel16 = jnp.where(lax.broadcasted_iota(jnp.int32, (DT_PAD, D_SSD), 0) == lane_head, 1.0, 0.0).astype(bf16)

    def spread(q):
        return _dot_sel_rhs(q, sel16)

    def state_free(ci):
        r0 = ci * T
        act = act_ref[0, r0:r0 + T, :]
        xs = act[:, :D_SSD]
        bm = act[:, D_SSD:D_SSD + GROUPS * STATE]
        cm = act[:, D_SSD + GROUPS * STATE:]
        dtv = jax.nn.softplus(dt_ref[0, r0:r0 + T, :] + dtb_ref[...])
        a_cs = _dot_sel_lhs(tri16, dtv * a_neg)
        bg16s = [bm[:, g * STATE:(g + 1) * STATE].astype(bf16) for g in range(GROUPS)]
        cg16s = [cm[:, g * STATE:(g + 1) * STATE].astype(bf16) for g in range(GROUPS)]
        gmats = [lax.dot_general(cg16s[g], bg16s[g], NT_DIMS, preferred_element_type=f32)
                 for g in range(GROUPS)]
        yield
        a_last = a_cs[T - 1:T, :]
        a_cs_t = a_cs.T
        rep = spread(jnp.concatenate([dtv, jnp.exp(a_last - a_cs), jnp.exp(a_cs)], axis=0))
        yield
        x = xs * rep[:T]
        x16 = x.astype(bf16)
        xd16 = (x * rep[T:2 * T]).astype(bf16)
        e_cs = rep[2 * T:]
        yield
        y_part = []
        for g in range(GROUPS):
            for pr in range(HEADS_PER_GROUP // 2):
                s_pair = []
                for q in range(2):
                    h = g * HEADS_PER_GROUP + 2 * pr + q
                    seg = a_cs[:, h:h + 1] - a_cs_t[h:h + 1, :]
                    lmat = jnp.exp(jnp.where(tri, seg, -jnp.inf))
                    s_pair.append((gmats[g] * lmat).astype(bf16))
                lanes = slice(g * gw + pr * LANES, g * gw + (pr + 1) * LANES)
                both = jnp.dot(jnp.concatenate(s_pair, axis=0), x16[:, lanes], preferred_element_type=f32)
                yield
                y_part.append(jnp.where(low_half, both[:T], both[T:]))
        y_free = jnp.concatenate(y_part, axis=-1) + xs * dexp_ref[...]
        return y_free, e_cs, jnp.exp(a_last), xd16, bg16s, cg16s

    def state_step(ci, y_free, e_cs, e_last, xd16, bg16s, cg16s):
        r0 = ci * T
        y_offs = []
        for g in range(GROUPS):
            hp = h_ref[g]
            y_offs.append(lax.dot_general(cg16s[g], hp.astype(bf16), NT_DIMS, preferred_element_type=f32))
            upd = lax.dot_general(xd16[:, g * gw:(g + 1) * gw], bg16s[g], TN_DIMS, preferred_element_type=f32)
            for hh in range(HEADS_PER_GROUP):
                h = g * HEADS_PER_GROUP + hh
                rows = slice(hh * HEAD_DIM, (hh + 1) * HEAD_DIM)
                h_ref[g, rows, :] = e_last[:, h:h + 1] * hp[rows] + upd[rows]
        y = y_free + jnp.concatenate(y_offs, axis=-1) * e_cs
        y_ref[0, r0:r0 + T, :] = _gated_group_norm(y, z_ref[0, r0:r0 + T, :], ng_ref[...])

    gens = [state_free(ci) for ci in range(cps)]
    free = [None] * cps
    while any(f is None for f in free):
        for ci in range(cps):
            if free[ci] is None:
                try:
                    next(gens[ci])
                except StopIteration as done:
                    free[ci] = done.value
    for ci in range(cps):
        state_step(ci, *free[ci])

    @pl.when(pl.program_id(1) == pl.num_programs(1) - 1)
    def _():
        st_ref[0] = h_ref[...].reshape(HEADS, HEAD_DIM, STATE)


def _ssd_prompt(act, z, dt, dtb, alog, dexp, ng, *, cps=8):
    nb, L, _ = act.shape
    rows = cps * SSD_CHUNK
    blk = lambda wd: pl.BlockSpec((1, rows, wd), lambda b, c: (b, c, 0))
    return pl.pallas_call(
        functools.partial(_ssd_prompt_kernel, cps=cps),
        grid=(nb, L // rows),
        in_specs=[blk(CONV_DIM), blk(D_SSD), blk(DT_PAD), _const_spec((1, DT_PAD)), _const_spec((1, DT_PAD)),
                  _const_spec((1, D_SSD)), _const_spec((1, D_SSD))],
        out_specs=[blk(D_SSD), pl.BlockSpec((1, HEADS, HEAD_DIM, STATE), lambda b, c: (b, 0, 0, 0))],
        out_shape=[jax.ShapeDtypeStruct((nb, L, D_SSD), f32),
                   jax.ShapeDtypeStruct((nb, HEADS, HEAD_DIM, STATE), f32)],
        scratch_shapes=[pltpu.VMEM((GROUPS, HEADS_PER_GROUP * HEAD_DIM, STATE), f32)],
        compiler_params=_params("parallel", "arbitrary"),
        name="ssd_prompt",
    )(act, z, dt, dtb, alog, dexp, ng)


def _ssd_sample_kernel(xbc_ref, z_ref, dt_ref, cbuf_ref, st_ref, cw_ref, cb_ref, dtb_ref, alog_ref,
                       dexp_ref, ng_ref, y_ref, stn_ref, cn_ref, dtot_ref, *, L, nb):
    gw = HEADS_PER_GROUP * HEAD_DIM
    full = [cbuf_ref[j] for j in range(CONV_W - 1)] + [xbc_ref[t] for t in range(L)]
    for j in range(CONV_W - 1):
        cn_ref[j] = full[L + j]
    a_neg = -jnp.exp(alog_ref[...])
    xs, bm, cm, dtv, acs = [], [], [], [], []
    run = None
    for t in range(L):
        conv = cb_ref[...]
        for k in range(CONV_W):
            conv = conv + full[t + k] * cw_ref[k:k + 1, :]
        act = _silu(conv)
        xs.append(act[:, :D_SSD])
        bm.append(act[:, D_SSD:D_SSD + GROUPS * STATE])
        cm.append(act[:, D_SSD + GROUPS * STATE:])
        d = jax.nn.softplus(dt_ref[t] + dtb_ref[...])
        dtv.append(d)
        run = d * a_neg if run is None else run + d * a_neg
        acs.append(run)
    a_tot = acs[L - 1]
    dtot_ref[...] = jnp.exp(a_tot)

    lane_head = lax.broadcasted_iota(jnp.int32, (DT_PAD, D_SSD), 1) // HEAD_DIM
    sel16 = jnp.where(lax.broadcasted_iota(jnp.int32, (DT_PAD, D_SSD), 0) == lane_head, 1.0, 0.0).astype(bf16)
    pairs = [(t, s) for t in range(L) for s in range(t)]
    factors = (dtv + [jnp.exp(a_tot - acs[t]) for t in range(L)] + [jnp.exp(acs[t]) for t in range(L)]
               + [jnp.exp(acs[t] - acs[s]) for t, s in pairs])
    rep = _dot_sel_rhs(jnp.concatenate(factors, axis=0), sel16)
    piece = lambda i: rep[i * nb:(i + 1) * nb]
    x = [xs[t] * piece(t) for t in range(L)]
    xd_stack = jnp.concatenate([x[t] * piece(L + t) for t in range(L)], axis=0)
    e_cs = [piece(2 * L + t) for t in range(L)]
    decay = {ts: piece(3 * L + i) for i, ts in enumerate(pairs)}

    in_group0 = lax.broadcasted_iota(jnp.int32, (nb, D_SSD), 1) < gw
    y_intra = []
    for t in range(L):
        acc = None
        for s in range(t + 1):
            cb_dot = [jnp.sum(cm[t][:, g * STATE:(g + 1) * STATE] * bm[s][:, g * STATE:(g + 1) * STATE],
                              axis=-1, keepdims=True) for g in range(GROUPS)]
            w = jnp.where(in_group0, cb_dot[0], cb_dot[1])
            term = w * x[s] if s == t else w * decay[(t, s)] * x[s]
            acc = term if acc is None else acc + term
        y_intra.append(acc)

    c_stack = [jnp.concatenate([cm[t][:, g * STATE:(g + 1) * STATE] for t in range(L)], axis=0).astype(bf16)
               for g in range(GROUPS)]
    b_stack = [jnp.concatenate([bm[t][:, g * STATE:(g + 1) * STATE] for t in range(L)], axis=0).astype(bf16)
               for g in range(GROUPS)]
    seq_of_row = lax.broadcasted_iota(jnp.int32, (L * nb, gw), 0) & (nb - 1)

    def per_seq(b, yoff):
        mine = seq_of_row == b
        drow = dtot_ref[pl.ds(b, 1), :]
        out = []
        for g in range(GROUPS):
            h0 = st_ref[b, g]
            r = lax.dot_general(c_stack[g], h0.astype(bf16), NT_DIMS, preferred_element_type=f32)
            out.append(yoff[g] + jnp.where(mine, r, 0.0))
            xm = jnp.where(mine, xd_stack[:, g * gw:(g + 1) * gw], 0.0).astype(bf16)
            upd = lax.dot_general(xm, b_stack[g], TN_DIMS, preferred_element_type=f32)
            for hh in range(HEADS_PER_GROUP):
                h = g * HEADS_PER_GROUP + hh
                rows = slice(hh * HEAD_DIM, (hh + 1) * HEAD_DIM)
                stn_ref[b, g, rows, :] = drow[:, h:h + 1] * h0[rows] + upd[rows]
        return tuple(out)

    yoff = lax.fori_loop(0, nb, per_seq, tuple(jnp.zeros((L * nb, gw), f32) for _ in range(GROUPS)),
                         unroll=8)

    for t in range(L):
        y_off = jnp.concatenate([yoff[g][t * nb:(t + 1) * nb] for g in range(GROUPS)], axis=-1)
        y = y_intra[t] + y_off * e_cs[t] + xs[t] * dexp_ref[...]
        y_ref[t] = _gated_group_norm(y, z_ref[t], ng_ref[...])


def _ssd_sample(xbc, z, dt, cbuf, st, cw, cb, dtb, alog, dexp, ng, *, nb=16):
    L, B, _ = xbc.shape
    assert nb & (nb - 1) == 0 and B % nb == 0
    tblk = lambda n, wd: pl.BlockSpec((n, nb, wd), lambda i: (0, i, 0))
    gw = HEADS_PER_GROUP * HEAD_DIM
    st = st.reshape(B, GROUPS, gw, STATE)
    st_spec = pl.BlockSpec((nb, GROUPS, gw, STATE), lambda i: (i, 0, 0, 0))
    return pl.pallas_call(
        functools.partial(_ssd_sample_kernel, L=L, nb=nb),
        grid=(B // nb,),
        in_specs=[tblk(L, CONV_DIM), tblk(L, D_SSD), tblk(L, DT_PAD), tblk(CONV_W - 1, CONV_DIM), st_spec,
                  _const_spec((CONV_W, CONV_DIM)), _const_spec((1, CONV_DIM)),
                  _const_spec((1, DT_PAD)), _const_spec((1, DT_PAD)),
                  _const_spec((1, D_SSD)), _const_spec((1, D_SSD))],
        out_specs=[tblk(L, D_SSD), st_spec, tblk(CONV_W - 1, CONV_DIM)],
        out_shape=[jax.ShapeDtypeStruct((L, B, D_SSD), f32),
                   jax.ShapeDtypeStruct((B, GROUPS, gw, STATE), f32),
                   jax.ShapeDtypeStruct((CONV_W - 1, B, CONV_DIM), f32)],
        scratch_shapes=[pltpu.VMEM((nb, DT_PAD), f32)],
        compiler_params=_params("parallel"),
        name="ssd_sample",
    )(xbc, z, dt, cbuf, st, cw, cb, dtb, alog, dexp, ng)


def _s5_param_kernel(lr_ref, li_ref, ls_ref, br_ref, bi_ref, cr_ref, ci_ref,
                     ar_ref, ai_ref, bblk_ref, cblk_ref, b_scr, c_scr):
    lr, li = lr_ref[...], li_ref[...]
    step = jnp.exp(ls_ref[...])
    mag = jnp.exp(lr * step)
    abr = mag * jnp.cos(li * step)
    abi = mag * jnp.sin(li * step)
    nr, ni = abr - 1.0, abi
    den = lr * lr + li * li
    fr = (nr * lr + ni * li) / den
    fi = (ni * lr - nr * li) / den
    br, bi = br_ref[...], bi_ref[...]
    bbr = fr * br - fi * bi
    bbi = fr * bi + fi * br
    b_scr[...] = jnp.zeros_like(b_scr)
    c_scr[...] = jnp.zeros_like(c_scr)
    gps = S5_GROUPS // S5_SLABS
    ns = S5_SLAB_STATE
    for g in range(S5_GROUPS):
        s, gl = divmod(g, gps)
        ch = slice(gl * S5_CH, (gl + 1) * S5_CH)
        st = slice(gl * S5_STATE, (gl + 1) * S5_STATE)
        st_im = slice(ns + gl * S5_STATE, ns + (gl + 1) * S5_STATE)
        ar_ref[s, :, st] = jnp.broadcast_to(abr[g], (SUBLANES, S5_STATE))
        ai_ref[s, :, st] = jnp.broadcast_to(abi[g], (SUBLANES, S5_STATE))
        b_scr[s, ch, st] = bbr[g]
        b_scr[s, ch, st_im] = bbi[g]
        c_scr[s, st, ch] = cr_ref[g].T
        c_scr[s, st_im, ch] = -ci_ref[g].T
    bblk_ref[...] = b_scr[...].astype(bf16)
    cblk_ref[...] = c_scr[...].astype(bf16)


def _s5_params(a_re, a_im, log_step, b_re_t, b_im_t, c_re, c_im):
    g, p = a_re.shape
    ns = S5_SLAB_STATE
    return pl.pallas_call(
        _s5_param_kernel,
        out_shape=[jax.ShapeDtypeStruct((S5_SLABS, SUBLANES, ns), f32),
                   jax.ShapeDtypeStruct((S5_SLABS, SUBLANES, ns), f32),
                   jax.ShapeDtypeStruct((S5_SLABS, LANES, 2 * ns), bf16),
                   jax.ShapeDtypeStruct((S5_SLABS, 2 * ns, LANES), bf16)],
        scratch_shapes=[pltpu.VMEM((S5_SLABS, LANES, 2 * ns), f32),
                        pltpu.VMEM((S5_SLABS, 2 * ns, LANES), f32)],
        name="s5_params",
    )(a_re.reshape(g, 1, p), a_im.reshape(g, 1, p), log_step.reshape(g, 1, 1), b_re_t, b_im_t, c_re, c_im)


def _gelu(x):
    return 0.5 * x * (1.0 + lax.erf(x * (2.0 ** -0.5)))


def _s5_slabs_interleaved(ut_ref, hbuf_ref, hst_ref, g_ref, ar_ref, ai_ref, bblk_ref, cblk_ref, dsk_ref, tl,
                          row_block=256, n_stages=8):
    ns = S5_SLAB_STATE
    nb = SUBLANES
    rows = nb * tl

    def lanes(s):
        return slice(s * LANES, (s + 1) * LANES)

    def bu_stage(slabs):
        for s in slabs:
            for r in range(0, rows, row_block):
                us = ut_ref[s, r:r + row_block, :]
                hbuf_ref[s, nb + r:nb + r + row_block, :] = jnp.dot(us.astype(bf16), bblk_ref[s],
                                                                    preferred_element_type=f32)
                yield

    def scan_stage(slabs):
        carry = [(hst_ref[s, :, :ns], hst_ref[s, :, ns:]) for s in slabs]
        coef = [(ar_ref[s], ai_ref[s]) for s in slabs]
        for t in range(tl):
            r = nb + nb * t
            for i, s in enumerate(slabs):
                (pr, pi), (ar, ai) = carry[i], coef[i]
                nr = ar * pr - ai * pi + hbuf_ref[s, r:r + nb, :ns]
                ni = ar * pi + ai * pr + hbuf_ref[s, r:r + nb, ns:]
                hbuf_ref[s, r:r + nb, :ns] = nr
                hbuf_ref[s, r:r + nb, ns:] = ni
                carry[i] = (nr, ni)
            if (t + 1) % (tl // n_stages) == 0:
                yield
        for i, s in enumerate(slabs):
            hst_ref[s, :, :ns] = carry[i][0]
            hst_ref[s, :, ns:] = carry[i][1]

    def y_stage(slabs):
        for s in slabs:
            for r in range(0, rows, row_block):
                h16 = hbuf_ref[s, nb + r:nb + r + row_block, :].astype(bf16)
                ys = jnp.dot(h16, cblk_ref[s], preferred_element_type=f32)
                ys = ys + dsk_ref[:, lanes(s)] * ut_ref[s, r:r + row_block, :]
                g_ref[s, r:r + row_block, :] = _gelu(ys)
                yield

    half = S5_SLABS // 2
    first, second = tuple(range(half)), tuple(range(half, S5_SLABS))
    _interleave(bu_stage(first))
    _interleave(scan_stage(first), bu_stage(second))
    _interleave(scan_stage(second), y_stage(first))
    _interleave(y_stage(second))


def _s5_scan_vmem(hbuf_ref, hst_ref, ar_ref, ai_ref, s, nb, rows):
    ns = S5_SLAB_STATE
    hbuf_ref[s, 0:nb, :] = hst_ref[s]
    ar, ai = ar_ref[s], ai_ref[s]

    def body(i, carry):
        j = pl.multiple_of(nb + i * SUBLANES, SUBLANES)
        prev = hbuf_ref[s, pl.ds(j - nb, SUBLANES), :]
        cur = hbuf_ref[s, pl.ds(j, SUBLANES), :]
        pr, pi = prev[:, :ns], prev[:, ns:]
        hbuf_ref[s, pl.ds(j, SUBLANES), :ns] = ar * pr - ai * pi + cur[:, :ns]
        hbuf_ref[s, pl.ds(j, SUBLANES), ns:] = ar * pi + ai * pr + cur[:, ns:]
        return carry

    lax.fori_loop(0, rows // SUBLANES, body, 0, unroll=min(8, nb // SUBLANES))
    hst_ref[s] = hbuf_ref[s, rows:rows + nb, :]


S5_N_IN = 10


def _s5_kernel(*refs, nb, tl, batch_major, n_cast):
    (u_ref, re0_ref, im0_ref, ar_ref, ai_ref, bblk_ref, cblk_ref, dsk_ref, wglu_ref,
     bglu_ref) = refs[:S5_N_IN]
    cast_in = refs[S5_N_IN:S5_N_IN + n_cast]
    y_ref, ren_ref, imn_ref = refs[S5_N_IN + n_cast:S5_N_IN + n_cast + 3]
    cast_out = refs[S5_N_IN + n_cast + 3:S5_N_IN + 2 * n_cast + 3]
    ut_ref, hbuf_ref, hst_ref, g_ref = refs[S5_N_IN + 2 * n_cast + 3:]
    for w_ref, w16_ref in zip(cast_in, cast_out):
        w16_ref[...] = w_ref[...].astype(bf16)
    step = pl.program_id(0)
    rows = nb * tl
    ns = S5_SLAB_STATE

    @pl.when(step == 0)
    def _():
        for s in range(S5_SLABS):
            hst_ref[s, :, :ns] = re0_ref[:, s * ns:(s + 1) * ns]
            hst_ref[s, :, ns:] = im0_ref[:, s * ns:(s + 1) * ns]

    for s in range(S5_SLABS):
        sl = slice(s * LANES, (s + 1) * LANES)
        if batch_major:
            for b in range(nb):
                ut_ref[s, pl.ds(b, tl, stride=nb), :] = u_ref[b, :, sl]
        else:
            ut_ref[s] = u_ref[:, :, sl].reshape(rows, LANES)

    if nb == SUBLANES:
        _s5_slabs_interleaved(ut_ref, hbuf_ref, hst_ref, g_ref, ar_ref, ai_ref, bblk_ref, cblk_ref, dsk_ref, tl)
    else:
        for s in range(S5_SLABS):
            sl = slice(s * LANES, (s + 1) * LANES)
            hbuf_ref[s, nb:nb + rows, :] = jnp.dot(ut_ref[s].astype(bf16), bblk_ref[s],
                                                   preferred_element_type=f32)
            _s5_scan_vmem(hbuf_ref, hst_ref, ar_ref, ai_ref, s, nb, rows)
            ys = jnp.dot(hbuf_ref[s, nb:nb + rows, :].astype(bf16), cblk_ref[s], preferred_element_type=f32)
            g_ref[s] = _gelu(ys + dsk_ref[:, sl] * ut_ref[s])

    g = jnp.concatenate([g_ref[s] for s in range(S5_SLABS)], axis=-1)
    gate = jnp.dot(g.astype(bf16), wglu_ref[...], preferred_element_type=f32) + bglu_ref[...]
    out = g * jax.nn.sigmoid(gate)
    if batch_major:
        for s in range(S5_SLABS):
            sl = slice(s * LANES, (s + 1) * LANES)
            ut_ref[s] = out[:, sl]
            for b in range(nb):
                y_ref[b, :, sl] = ut_ref[s, pl.ds(b, tl, stride=nb), :]
    else:
        y_ref[...] = out.reshape(tl, nb, D_S5)

    @pl.when(step == pl.num_programs(0) - 1)
    def _():
        for s in range(S5_SLABS):
            ren_ref[:, s * ns:(s + 1) * ns] = hst_ref[s, :, :ns]
            imn_ref[:, s * ns:(s + 1) * ns] = hst_ref[s, :, ns:]


def _s5(u, re0, im0, ar, ai, bblk, cblk, dsk, wglu, bglu, *, tl, batch_major, cast=()):
    if batch_major:
        nb, L, _ = u.shape
        u_spec = pl.BlockSpec((nb, tl, D_S5), lambda i: (0, i, 0))
    else:
        L, nb, _ = u.shape
        u_spec = pl.BlockSpec((tl, nb, D_S5), lambda i: (i, 0, 0))
    rows = nb * tl
    steps = L // tl
    nstate = S5_GROUPS * S5_STATE
    st_spec = _const_spec((nb, nstate))
    cast_specs = [pl.BlockSpec((w.shape[0] // steps, w.shape[1]), lambda i: (i, 0)) for w in cast]
    assert all(w.shape[0] % (steps * 2 * SUBLANES) == 0 for w in cast)
    return pl.pallas_call(
        functools.partial(_s5_kernel, nb=nb, tl=tl, batch_major=batch_major, n_cast=len(cast)),
        grid=(steps,),
        in_specs=[u_spec, st_spec, st_spec,
                  _const_spec((S5_SLABS, SUBLANES, S5_SLAB_STATE)),
                  _const_spec((S5_SLABS, SUBLANES, S5_SLAB_STATE)),
                  _const_spec((S5_SLABS, LANES, 2 * S5_SLAB_STATE)),
                  _const_spec((S5_SLABS, 2 * S5_SLAB_STATE, LANES)),
                  _const_spec((1, D_S5)), _const_spec((D_S5, D_S5)), _const_spec((1, D_S5))] + cast_specs,
        out_specs=[u_spec, st_spec, st_spec] + cast_specs,
        out_shape=[jax.ShapeDtypeStruct(u.shape, f32),
                   jax.ShapeDtypeStruct((nb, nstate), f32),
                   jax.ShapeDtypeStruct((nb, nstate), f32)]
                  + [jax.ShapeDtypeStruct(w.shape, bf16) for w in cast],
        scratch_shapes=[pltpu.VMEM((S5_SLABS, rows, LANES), f32),
                        pltpu.VMEM((S5_SLABS, nb + rows, 2 * S5_SLAB_STATE), f32),
                        pltpu.VMEM((S5_SLABS, nb, 2 * S5_SLAB_STATE), f32),
                        pltpu.VMEM((S5_SLABS, rows, LANES), f32)],
        compiler_params=_params("arbitrary"),
        name="s5_mixer",
    )(u, re0, im0, ar, ai, bblk, cblk, dsk, wglu, bglu, *cast)


def _ffn_stages(x, ys, y5, mods, n2g, nfg, wo_ref, wg_ref, wu_ref, wd_ref, write_out, ff_chunk):
    g1, sh2, sc2, g2, shf, scf = mods
    att = jnp.dot(ys.astype(bf16), wo_ref[:D_SSD, :], preferred_element_type=f32)
    att = att + jnp.dot(y5.astype(bf16), wo_ref[D_SSD:, :], preferred_element_type=f32)
    yield
    x1 = x + _by_row(lambda v, s: v * s, att, g1)
    v = _rms_mod(x1, n2g, sc2, sh2).astype(bf16)
    ff = None
    for o in range(0, D_FF, ff_chunk):
        gate = jnp.dot(v, wg_ref[:, o:o + ff_chunk], preferred_element_type=f32)
        up = jnp.dot(v, wu_ref[:, o:o + ff_chunk], preferred_element_type=f32)
        hid = (_silu(gate) * up).astype(bf16)
        part = jnp.dot(hid, wd_ref[o:o + ff_chunk, :], preferred_element_type=f32)
        ff = part if ff is None else ff + part
        yield
    x2 = x1 + _by_row(lambda v, s: v * s, ff, g2)
    write_out(_rms_mod(x2, nfg, scf, shf))


def _ffn_kernel(x_ref, ys_ref, y5_ref, mod_ref, modf_ref, n2g_ref, nfg_ref, wo_ref, wg_ref, wu_ref, wd_ref,
                o_ref, *, seq_major, ff_chunk, sub):
    mods = (_mod_row(mod_ref, 2, seq_major), _mod_row(mod_ref, 3, seq_major), _mod_row(mod_ref, 4, seq_major),
            _mod_row(mod_ref, 5, seq_major), _mod_row(modf_ref, 0, seq_major), _mod_row(modf_ref, 1, seq_major))
    x = _rows_in(x_ref, seq_major)
    rows = x.shape[0]
    tiles = {}

    def stages(r0):
        def write_out(y):
            if seq_major:
                tiles[r0] = y
            else:
                o_ref[0, r0:r0 + sub, :] = y
        return _ffn_stages(x[r0:r0 + sub], ys_ref[0, r0:r0 + sub, :], y5_ref[0, r0:r0 + sub, :], mods,
                           n2g_ref[...], nfg_ref[...], wo_ref, wg_ref, wu_ref, wd_ref, write_out, ff_chunk)

    _interleave(*[stages(r0) for r0 in range(0, rows, sub)])
    if seq_major:
        s, t, d = o_ref.shape
        y = jnp.concatenate([tiles[r0] for r0 in range(0, rows, sub)], axis=0)
        o_ref[...] = jnp.swapaxes(y.reshape(t, s, d), 0, 1)


def _ffn(x, ys, y5, mod, modf, n2g, nfg, wo, wg, wu, wd, *, tm, sub, seq_major, n_seq, first_seq, ff_chunk=256):
    if seq_major:
        x_spec = _const_spec(x.shape)
        nb, rows, d = 1, x.shape[0] * x.shape[1], x.shape[2]
        assert tm == rows
    else:
        nb, rows, d = x.shape
        x_spec = pl.BlockSpec((1, tm, d), lambda i, j: (i, j, 0))
    nt = rows // tm
    mod_spec = _mod_spec(mod, n_seq, first_seq)
    modf_spec = _mod_spec(modf, n_seq, first_seq)
    blk = lambda wd_: pl.BlockSpec((1, tm, wd_), lambda i, j: (i, j, 0))
    single = dict(pipeline_mode=pl.Buffered(1))
    wspec = lambda shape: pl.BlockSpec(shape, lambda i, j: (0, 0), **single)
    return pl.pallas_call(
        functools.partial(_ffn_kernel, seq_major=seq_major, ff_chunk=ff_chunk, sub=sub),
        grid=(nb, nt),
        in_specs=[x_spec, blk(D_SSD), blk(D_S5), mod_spec, modf_spec,
                  _const_spec((1, d)), _const_spec((1, d)),
                  wspec((d, d)), wspec((d, D_FF)), wspec((d, D_FF)), wspec((D_FF, d))],
        out_specs=x_spec,
        out_shape=jax.ShapeDtypeStruct(x.shape, f32),
        compiler_params=_params("parallel", "parallel"),
        name="out_ffn",
    )(x, ys, y5, mod, modf, n2g, nfg, wo, wg, wu, wd)


def kernel(x_prompt, x_sample, c_prompt, c_sample, state_ssd, state_conv, state_s5_re, state_s5_im, w_ada, b_ada, norm1_g, w_in, conv_w, conv_b, ssd_dt_bias, ssd_A_log, ssd_D, ssd_norm_g, s5_A_re, s5_A_im, s5_log_step, s5_B_re, s5_B_im, s5_C_re, s5_C_im, s5_D, w_glu, b_glu, w_out, norm2_g, w_ffn_gate, w_ffn_up, w_ffn_down, w_ada_f, b_ada_f, normf_g):
    assert w_ada.shape[0] == 1, "single-layer stack"
    bp, seq, d = x_prompt.shape
    bs, dseq, _ = x_sample.shape

    c_all = jnp.concatenate([c_sample, c_prompt], axis=0)
    mod = _ada_mod(c_all, w_ada[0], b_ada[0])
    modf = _ada_mod(c_all, w_ada_f, b_ada_f)
    seqs_s = dict(n_seq=bs, first_seq=0)
    seqs_p = dict(n_seq=bp, first_seq=bs)

    w_in_p = w_in[0].T
    pad_h = lambda v: jnp.concatenate([v, jnp.zeros((DT_PAD - HEADS,), f32)]).reshape(1, DT_PAD)
    dtb = pad_h(ssd_dt_bias[0])
    alog = pad_h(ssd_A_log[0])
    dexp = jnp.repeat(ssd_D[0], HEAD_DIM).reshape(1, D_SSD)
    ng = ssd_norm_g[0].reshape(1, D_SSD)
    cw, cb = conv_w[0], conv_b[0].reshape(1, CONV_DIM)

    ar, ai, bblk, cblk = _s5_params(s5_A_re[0], s5_A_im[0], s5_log_step[0],
                                    s5_B_re[0].transpose(0, 2, 1), s5_B_im[0].transpose(0, 2, 1),
                                    s5_C_re[0], s5_C_im[0])
    dsk = s5_D[0].reshape(1, D_S5)
    wglu = w_glu[0].astype(bf16)
    bglu = b_glu[0].reshape(1, D_S5)
    n1g, n2g, nfg = norm1_g[0].reshape(1, d), norm2_g[0].reshape(1, d), normf_g.reshape(1, d)
    nstate = S5_GROUPS * S5_STATE

    z, act, u5, dt, conv_p = _inproj(x_prompt, mod, n1g, w_in_p, (cw, cb), tm=1024, sub=256, seq_major=False,
                                     **seqs_p)
    y_ssd, ssd_p = _ssd_prompt(act, z, dt, dtb, alog, dexp, ng)
    zeros_st = jnp.zeros((bp, nstate), f32)
    y_s5, re_p, im_p, wo, wg, wu, wd = _s5(
        u5, zeros_st, zeros_st, ar, ai, bblk, cblk, dsk, wglu, bglu, tl=128, batch_major=True,
        cast=(w_out[0], w_ffn_gate[0], w_ffn_up[0], w_ffn_down[0]))
    y_prompt = _ffn(x_prompt, y_ssd, y_s5, mod, modf, n2g, nfg, wo, wg, wu, wd, tm=1024, sub=512,
                    seq_major=False, **seqs_p)

    rows_s = dseq * bs
    steps = lambda a: a.reshape(dseq, bs, a.shape[-1])
    flat = lambda a: a.reshape(1, rows_s, a.shape[-1])
    z, xbc, u5, dt = _inproj(x_sample, mod, n1g, w_in_p, tm=rows_s, sub=rows_s // 2, seq_major=True, **seqs_s)
    y_ssd, ssd_s, conv_s = _ssd_sample(steps(xbc), steps(z), steps(dt), state_conv[0].transpose(1, 0, 2),
                                       state_ssd[0], cw, cb, dtb, alog, dexp, ng)
    y_s5, re_s, im_s = _s5(steps(u5), state_s5_re[0].reshape(bs, nstate), state_s5_im[0].reshape(bs, nstate),
                           ar, ai, bblk, cblk, dsk, wglu, bglu, tl=dseq, batch_major=False)
    y_sample = _ffn(x_sample, flat(y_ssd), flat(y_s5), mod, modf, n2g, nfg, wo, wg, wu, wd,
                    tm=rows_s, sub=rows_s, seq_major=True, **seqs_s)

    g5 = (S5_GROUPS, S5_STATE)
    return (y_prompt, y_sample,
            ssd_p[None], ssd_s.reshape((1,) + state_ssd.shape[1:]),
            conv_p[None], conv_s.transpose(1, 0, 2)[None],
            re_p.reshape((1, bp) + g5), re_s.reshape((1, bs) + g5),
            im_p.reshape((1, bp) + g5), im_s.reshape((1, bs) + g5))
```

```python
import functools

import jax
import jax.numpy as jnp
from jax import lax
from jax.experimental import pallas as pl
from jax.experimental.pallas import tpu as pltpu

f32 = jnp.float32
bf16 = jnp.bfloat16

D_MODEL = 1024
D_SSD = 512
HEAD_DIM = 64
HEADS = 8
GROUPS = 2
HEADS_PER_GROUP = HEADS // GROUPS
STATE = 128
CONV_W = 4
CONV_DIM = D_SSD + 2 * GROUPS * STATE
D_S5 = 512
S5_CH = 16
S5_GROUPS = 32
S5_STATE = 64
D_FF = 2816
N_ADA = 6
EPS = 1e-6

LANES = 128
SUBLANES = 8
SSD_CHUNK = 128
S5_SLABS = D_S5 // LANES
S5_SLAB_STATE = (S5_GROUPS // S5_SLABS) * S5_STATE
DT_PAD = LANES
VMEM_LIMIT = 56 * 1024 * 1024

NT_DIMS = (((1,), (1,)), ((), ()))
TN_DIMS = (((0,), (0,)), ((), ()))


def _silu(x):
    return x * jax.nn.sigmoid(x)


def _interleave(*gens, lead=0):
    pending, live, rnd = list(gens), [], 0
    while pending or live:
        while pending and rnd >= lead * (len(gens) - len(pending)):
            live.append(pending.pop(0))
        for g in list(live):
            try:
                next(g)
            except StopIteration:
                live.remove(g)
        rnd += 1


def _by_row(fn, v, *ms):
    r, d = v.shape
    m_rows = ms[0].shape[0]
    if m_rows in (1, r):
        return fn(v, *ms)
    out = fn(v.reshape(r // m_rows, m_rows, d), *[m[None] for m in ms])
    return out.reshape(r, d)


def _rms_mod(x, g, sc, sh):
    y = x * lax.rsqrt(jnp.mean(x * x, axis=-1, keepdims=True) + EPS)
    return _by_row(lambda v, s, t: v * (1.0 + s) + t, y * g, sc, sh)


def _mod_row(mod_ref, i, per_row):
    return mod_ref[i] if per_row else mod_ref[i, pl.ds(pl.program_id(0), 1), :]


def _mod_spec(mod, n_seq, first_seq):
    assert first_seq % n_seq == 0
    return pl.BlockSpec((mod.shape[0], n_seq, mod.shape[2]), lambda *_: (0, first_seq // n_seq, 0))


def _const_spec(shape):
    nd = len(shape)
    return pl.BlockSpec(shape, lambda *_: (0,) * nd)


def _params(*sem):
    return pltpu.CompilerParams(dimension_semantics=sem, vmem_limit_bytes=VMEM_LIMIT)


ADA_K_BLOCK = 256


def _ada_kernel(c_ref, w_ref, b_ref, o_ref):
    n, _, d = o_ref.shape
    s = _silu(c_ref[...]).astype(bf16)

    @pl.when(pl.program_id(0) == 0)
    def _():
        for j in range(n):
            o_ref[j] = jnp.broadcast_to(b_ref[:, j * d:(j + 1) * d], o_ref.shape[1:])

    for j in range(n):
        o_ref[j] += jnp.dot(s, w_ref[:, j * d:(j + 1) * d].astype(bf16), preferred_element_type=f32)


def _ada_mod(c, w, b):
    m, k = c.shape
    n = w.shape[1] // k
    return pl.pallas_call(
        _ada_kernel,
        grid=(k // ADA_K_BLOCK,),
        in_specs=[pl.BlockSpec((m, ADA_K_BLOCK), lambda i: (0, i)),
                  pl.BlockSpec((ADA_K_BLOCK, n * k), lambda i: (i, 0)),
                  _const_spec((1, n * k))],
        out_specs=_const_spec((n, m, k)),
        out_shape=jax.ShapeDtypeStruct((n, m, k), f32),
        compiler_params=_params("arbitrary"),
        name="ada_mod",
    )(c, w, b.reshape(1, n * k))


def _rows_in(x_ref, seq_major):
    if not seq_major:
        return x_ref[0]
    s, t, d = x_ref.shape
    return jnp.swapaxes(x_ref[...], 0, 1).reshape(s * t, d)


def _conv_silu(ext, cw_ref, cb_ref):
    w = [cw_ref[k:k + 1, :] for k in range(CONV_W)]
    back1 = pltpu.roll(ext, 1, 0)
    older = pltpu.roll(ext * w[1] + back1 * w[0], 2, 0)
    conv = cb_ref[...] + ext[SUBLANES:, :] * w[3] + back1[SUBLANES:, :] * w[2] + older[SUBLANES:, :]
    return _silu(conv)


def _inproj_stages(x, sh, sc, g, outs, after=()):
    u = _rms_mod(x, g, sc, sh).astype(bf16)
    yield
    for write, w16 in outs:
        write(lax.dot_general(u, w16, NT_DIMS, preferred_element_type=f32))
        yield
    for stage in after:
        stage()
        yield


def _inproj_kernel(x_ref, mod_ref, g_ref, w_ref, *rest, seq_major, sub, conv):
    if conv:
        cw_ref, cb_ref, z_ref, act_ref, u5_ref, dt_ref, cn_ref, ext_ref = rest

        @pl.when(pl.program_id(1) == 0)
        def _():
            ext_ref[0:SUBLANES, :] = jnp.zeros((SUBLANES, CONV_DIM), f32)
    else:
        z_ref, xbc_ref, u5_ref, dt_ref = rest
    sh = _mod_row(mod_ref, 0, seq_major)
    sc = _mod_row(mod_ref, 1, seq_major)
    x = _rows_in(x_ref, seq_major)
    rows = x.shape[0]
    T = SSD_CHUNK
    o_dt = D_SSD + CONV_DIM
    w16 = lambda o, width: w_ref[o:o + width, :].astype(bf16)
    w_z, w_xbc, w_u5, w_dt = w16(0, D_SSD), w16(D_SSD, CONV_DIM), w16(o_dt + HEADS, D_S5), w16(o_dt, DT_PAD)

    def stages(r0):
        def to(ref):
            def write(v):
                ref[0, r0:r0 + sub, :] = v
            return write

        def to_ext(v):
            ext_ref[SUBLANES + r0:SUBLANES + r0 + sub, :] = v

        def conv_chunk(c0):
            def stage():
                act_ref[0, c0:c0 + T, :] = _conv_silu(ext_ref[c0:c0 + SUBLANES + T, :], cw_ref, cb_ref)
            return stage

        outs = [(to(z_ref), w_z), (to_ext if conv else to(xbc_ref), w_xbc), (to(u5_ref), w_u5), (to(dt_ref), w_dt)]
        after = [conv_chunk(c0) for c0 in range(r0, r0 + sub, T)] if conv else ()
        return _inproj_stages(x[r0:r0 + sub], sh, sc, g_ref[...], outs, after)

    n_matmul_stages = 5
    _interleave(*[stages(r0) for r0 in range(0, rows, sub)], lead=n_matmul_stages if conv else 0)
    if conv:
        cn_ref[0] = ext_ref[SUBLANES + rows - (CONV_W - 1):SUBLANES + rows, :]
        ext_ref[0:SUBLANES, :] = ext_ref[rows:rows + SUBLANES, :]


def _inproj(x, mod, g, w, conv_wb=None, *, tm, sub, seq_major, n_seq, first_seq):
    if seq_major:
        x_spec = _const_spec(x.shape)
        nb, rows, d = 1, x.shape[0] * x.shape[1], x.shape[2]
        assert tm == rows
    else:
        nb, rows, d = x.shape
        x_spec = pl.BlockSpec((1, tm, d), lambda i, j: (i, j, 0))
    nt = rows // tm
    widths = (D_SSD, CONV_DIM, D_S5, DT_PAD)
    conv = conv_wb is not None
    in_specs = [x_spec, _mod_spec(mod, n_seq, first_seq), _const_spec((1, d)),
                pl.BlockSpec(w.shape, lambda i, j: (0, 0), pipeline_mode=pl.Buffered(1))]
    out_specs = [pl.BlockSpec((1, tm, wd), lambda i, j: (i, j, 0)) for wd in widths]
    out_shape = [jax.ShapeDtypeStruct((nb, rows, wd), f32) for wd in widths]
    scratch = []
    if conv:
        in_specs += [_const_spec((CONV_W, CONV_DIM)), _const_spec((1, CONV_DIM))]
        out_specs.append(pl.BlockSpec((1, CONV_W - 1, CONV_DIM), lambda i, j: (i, 0, 0)))
        out_shape.append(jax.ShapeDtypeStruct((nb, CONV_W - 1, CONV_DIM), f32))
        scratch.append(pltpu.VMEM((SUBLANES + tm, CONV_DIM), f32))
    return pl.pallas_call(
        functools.partial(_inproj_kernel, seq_major=seq_major, sub=sub, conv=conv),
        grid=(nb, nt),
        in_specs=in_specs, out_specs=out_specs, out_shape=out_shape, scratch_shapes=scratch,
        compiler_params=_params("parallel", "arbitrary" if conv else "parallel"),
        name="in_proj",
    )(x, mod, g, w, *(conv_wb or ()))


def _split3(x):
    hi = x.astype(bf16)
    r1 = x - hi.astype(f32)
    mid = r1.astype(bf16)
    lo = (r1 - mid.astype(f32)).astype(bf16)
    return hi, mid, lo


def _dot_sel_lhs(sel16, x):
    return sum(jnp.dot(sel16, p, preferred_element_type=f32) for p in _split3(x))


def _dot_sel_rhs(x, sel16):
    hi, mid, _ = _split3(x)
    return jnp.dot(hi, sel16, preferred_element_type=f32) + jnp.dot(mid, sel16, preferred_element_type=f32)


def _gated_group_norm(y, z, ng):
    y = y * _silu(z)
    gw = D_SSD // GROUPS
    parts = []
    for g in range(GROUPS):
        yg = y[:, g * gw:(g + 1) * gw]
        parts.append(yg * lax.rsqrt(jnp.mean(yg * yg, axis=-1, keepdims=True) + EPS))
    return jnp.concatenate(parts, axis=-1) * ng


def _ssd_prompt_kernel(act_ref, z_ref, dt_ref, dtb_ref, alog_ref, dexp_ref, ng_ref, y_ref, st_ref, h_ref, *, cps):
    T = SSD_CHUNK
    gw = HEADS_PER_GROUP * HEAD_DIM

    @pl.when(pl.program_id(1) == 0)
    def _():
        h_ref[...] = jnp.zeros_like(h_ref)

    row = lax.broadcasted_iota(jnp.int32, (T, T), 0)
    col = lax.broadcasted_iota(jnp.int32, (T, T), 1)
    tri = row >= col
    tri16 = jnp.where(tri, 1.0, 0.0).astype(bf16)
    low_half = lax.broadcasted_iota(jnp.int32, (T, LANES), 1) < HEAD_DIM
    a_neg = -jnp.exp(alog_ref[...])

    lane_head = lax.broadcasted_iota(jnp.int32, (DT_PAD, D_SSD), 1) // HEAD_DIM
    sel16 = jnp.where(lax.broadcasted_iota(jnp.int32, (DT_PAD, D_SSD), 0) == lane_head, 1.0, 0.0).astype(bf16)

    def spread(q):
        return _dot_sel_rhs(q, sel16)

    def state_free(ci):
        r0 = ci * T
        act = act_ref[0, r0:r0 + T, :]
        xs = act[:, :D_SSD]
        bm = act[:, D_SSD:D_SSD + GROUPS * STATE]
        cm = act[:, D_SSD + GROUPS * STATE:]
        dtv = jax.nn.softplus(dt_ref[0, r0:r0 + T, :] + dtb_ref[...])
        a_cs = _dot_sel_lhs(tri16, dtv * a_neg)
        bg16s = [bm[:, g * STATE:(g + 1) * STATE].astype(bf16) for g in range(GROUPS)]
        cg16s = [cm[:, g * STATE:(g + 1) * STATE].astype(bf16) for g in range(GROUPS)]
        gmats = [lax.dot_general(cg16s[g], bg16s[g], NT_DIMS, preferred_element_type=f32)
                 for g in range(GROUPS)]
        yield
        a_last = a_cs[T - 1:T, :]
        a_cs_t = a_cs.T
        rep = spread(jnp.concatenate([dtv, jnp.exp(a_last - a_cs), jnp.exp(a_cs)], axis=0))
        yield
        x = xs * rep[:T]
        x16 = x.astype(bf16)
        xd16 = (x * rep[T:2 * T]).astype(bf16)
        e_cs = rep[2 * T:]
        yield
        y_part = []
        for g in range(GROUPS):
            for pr in range(HEADS_PER_GROUP // 2):
                s_pair = []
                for q in range(2):
                    h = g * HEADS_PER_GROUP + 2 * pr + q
                    seg = a_cs[:, h:h + 1] - a_cs_t[h:h + 1, :]
                    lmat = jnp.exp(jnp.where(tri, seg, -jnp.inf))
                    s_pair.append((gmats[g] * lmat).astype(bf16))
                lanes = slice(g * gw + pr * LANES, g * gw + (pr + 1) * LANES)
                both = jnp.dot(jnp.concatenate(s_pair, axis=0), x16[:, lanes], preferred_element_type=f32)
                yield
                y_part.append(jnp.where(low_half, both[:T], both[T:]))
        y_free = jnp.concatenate(y_part, axis=-1) + xs * dexp_ref[...]
        return y_free, e_cs, jnp.exp(a_last), xd16, bg16s, cg16s

    def state_step(ci, y_free, e_cs, e_last, xd16, bg16s, cg16s):
        r0 = ci * T
        y_offs = []
        for g in range(GROUPS):
            hp = h_ref[g]
            y_offs.append(lax.dot_general(cg16s[g], hp.astype(bf16), NT_DIMS, preferred_element_type=f32))
            upd = lax.dot_general(xd16[:, g * gw:(g + 1) * gw], bg16s[g], TN_DIMS, preferred_element_type=f32)
            for hh in range(HEADS_PER_GROUP):
                h = g * HEADS_PER_GROUP + hh
                rows = slice(hh * HEAD_DIM, (hh + 1) * HEAD_DIM)
                h_ref[g, rows, :] = e_last[:, h:h + 1] * hp[rows] + upd[rows]
        y = y_free + jnp.concatenate(y_offs, axis=-1) * e_cs
        y_ref[0, r0:r0 + T, :] = _gated_group_norm(y, z_ref[0, r0:r0 + T, :], ng_ref[...])

    gens = [state_free(ci) for ci in range(cps)]
    free = [None] * cps
    while any(f is None for f in free):
        for ci in range(cps):
            if free[ci] is None:
                try:
                    next(gens[ci])
                except StopIteration as done:
                    free[ci] = done.value
    for ci in range(cps):
        state_step(ci, *free[ci])

    @pl.when(pl.program_id(1) == pl.num_programs(1) - 1)
    def _():
        st_ref[0] = h_ref[...].reshape(HEADS, HEAD_DIM, STATE)


def _ssd_prompt(act, z, dt, dtb, alog, dexp, ng, *, cps=8):
    nb, L, _ = act.shape
    rows = cps * SSD_CHUNK
    blk = lambda wd: pl.BlockSpec((1, rows, wd), lambda b, c: (b, c, 0))
    return pl.pallas_call(
        functools.partial(_ssd_prompt_kernel, cps=cps),
        grid=(nb, L // rows),
        in_specs=[blk(CONV_DIM), blk(D_SSD), blk(DT_PAD), _const_spec((1, DT_PAD)), _const_spec((1, DT_PAD)),
                  _const_spec((1, D_SSD)), _const_spec((1, D_SSD))],
        out_specs=[blk(D_SSD), pl.BlockSpec((1, HEADS, HEAD_DIM, STATE), lambda b, c: (b, 0, 0, 0))],
        out_shape=[jax.ShapeDtypeStruct((nb, L, D_SSD), f32),
                   jax.ShapeDtypeStruct((nb, HEADS, HEAD_DIM, STATE), f32)],
        scratch_shapes=[pltpu.VMEM((GROUPS, HEADS_PER_GROUP * HEAD_DIM, STATE), f32)],
        compiler_params=_params("parallel", "arbitrary"),
        name="ssd_prompt",
    )(act, z, dt, dtb, alog, dexp, ng)


def _ssd_sample_kernel(xbc_ref, z_ref, dt_ref, cbuf_ref, st_ref, cw_ref, cb_ref, dtb_ref, alog_ref,
                       dexp_ref, ng_ref, y_ref, stn_ref, cn_ref, dtot_ref, *, L, nb):
    gw = HEADS_PER_GROUP * HEAD_DIM
    full = [cbuf_ref[j] for j in range(CONV_W - 1)] + [xbc_ref[t] for t in range(L)]
    for j in range(CONV_W - 1):
        cn_ref[j] = full[L + j]
    a_neg = -jnp.exp(alog_ref[...])
    xs, bm, cm, dtv, acs = [], [], [], [], []
    run = None
    for t in range(L):
        conv = cb_ref[...]
        for k in range(CONV_W):
            conv = conv + full[t + k] * cw_ref[k:k + 1, :]
        act = _silu(conv)
        xs.append(act[:, :D_SSD])
        bm.append(act[:, D_SSD:D_SSD + GROUPS * STATE])
        cm.append(act[:, D_SSD + GROUPS * STATE:])
        d = jax.nn.softplus(dt_ref[t] + dtb_ref[...])
        dtv.append(d)
        run = d * a_neg if run is None else run + d * a_neg
        acs.append(run)
    a_tot = acs[L - 1]
    dtot_ref[...] = jnp.exp(a_tot)

    lane_head = lax.broadcasted_iota(jnp.int32, (DT_PAD, D_SSD), 1) // HEAD_DIM
    sel16 = jnp.where(lax.broadcasted_iota(jnp.int32, (DT_PAD, D_SSD), 0) == lane_head, 1.0, 0.0).astype(bf16)
    pairs = [(t, s) for t in range(L) for s in range(t)]
    factors = (dtv + [jnp.exp(a_tot - acs[t]) for t in range(L)] + [jnp.exp(acs[t]) for t in range(L)]
               + [jnp.exp(acs[t] - acs[s]) for t, s in pairs])
    rep = _dot_sel_rhs(jnp.concatenate(factors, axis=0), sel16)
    piece = lambda i: rep[i * nb:(i + 1) * nb]
    x = [xs[t] * piece(t) for t in range(L)]
    xd_stack = jnp.concatenate([x[t] * piece(L + t) for t in range(L)], axis=0)
    e_cs = [piece(2 * L + t) for t in range(L)]
    decay = {ts: piece(3 * L + i) for i, ts in enumerate(pairs)}

    in_group0 = lax.broadcasted_iota(jnp.int32, (nb, D_SSD), 1) < gw
    y_intra = []
    for t in range(L):
        acc = None
        for s in range(t + 1):
            cb_dot = [jnp.sum(cm[t][:, g * STATE:(g + 1) * STATE] * bm[s][:, g * STATE:(g + 1) * STATE],
                              axis=-1, keepdims=True) for g in range(GROUPS)]
            w = jnp.where(in_group0, cb_dot[0], cb_dot[1])
            term = w * x[s] if s == t else w * decay[(t, s)] * x[s]
            acc = term if acc is None else acc + term
        y_intra.append(acc)

    c_stack = [jnp.concatenate([cm[t][:, g * STATE:(g + 1) * STATE] for t in range(L)], axis=0).astype(bf16)
               for g in range(GROUPS)]
    b_stack = [jnp.concatenate([bm[t][:, g * STATE:(g + 1) * STATE] for t in range(L)], axis=0).astype(bf16)
               for g in range(GROUPS)]
    seq_of_row = lax.broadcasted_iota(jnp.int32, (L * nb, gw), 0) & (nb - 1)

    def per_seq(b, yoff):
        mine = seq_of_row == b
        drow = dtot_ref[pl.ds(b, 1), :]
        out = []
        for g in range(GROUPS):
            h0 = st_ref[b, g]
            r = lax.dot_general(c_stack[g], h0.astype(bf16), NT_DIMS, preferred_element_type=f32)
            out.append(yoff[g] + jnp.where(mine, r, 0.0))
            xm = jnp.where(mine, xd_stack[:, g * gw:(g + 1) * gw], 0.0).astype(bf16)
            upd = lax.dot_general(xm, b_stack[g], TN_DIMS, preferred_element_type=f32)
            for hh in range(HEADS_PER_GROUP):
                h = g * HEADS_PER_GROUP + hh
                rows = slice(hh * HEAD_DIM, (hh + 1) * HEAD_DIM)
                stn_ref[b, g, rows, :] = drow[:, h:h + 1] * h0[rows] + upd[rows]
        return tuple(out)

    yoff = lax.fori_loop(0, nb, per_seq, tuple(jnp.zeros((L * nb, gw), f32) for _ in range(GROUPS)),
                         unroll=8)

    for t in range(L):
        y_off = jnp.concatenate([yoff[g][t * nb:(t + 1) * nb] for g in range(GROUPS)], axis=-1)
        y = y_intra[t] + y_off * e_cs[t] + xs[t] * dexp_ref[...]
        y_ref[t] = _gated_group_norm(y, z_ref[t], ng_ref[...])


def _ssd_sample(xbc, z, dt, cbuf, st, cw, cb, dtb, alog, dexp, ng, *, nb=16):
    L, B, _ = xbc.shape
    assert nb & (nb - 1) == 0 and B % nb == 0
    tblk = lambda n, wd: pl.BlockSpec((n, nb, wd), lambda i: (0, i, 0))
    gw = HEADS_PER_GROUP * HEAD_DIM
    st = st.reshape(B, GROUPS, gw, STATE)
    st_spec = pl.BlockSpec((nb, GROUPS, gw, STATE), lambda i: (i, 0, 0, 0))
    return pl.pallas_call(
        functools.partial(_ssd_sample_kernel, L=L, nb=nb),
        grid=(B // nb,),
        in_specs=[tblk(L, CONV_DIM), tblk(L, D_SSD), tblk(L, DT_PAD), tblk(CONV_W - 1, CONV_DIM), st_spec,
                  _const_spec((CONV_W, CONV_DIM)), _const_spec((1, CONV_DIM)),
                  _const_spec((1, DT_PAD)), _const_spec((1, DT_PAD)),
                  _const_spec((1, D_SSD)), _const_spec((1, D_SSD))],
        out_specs=[tblk(L, D_SSD), st_spec, tblk(CONV_W - 1, CONV_DIM)],
        out_shape=[jax.ShapeDtypeStruct((L, B, D_SSD), f32),
                   jax.ShapeDtypeStruct((B, GROUPS, gw, STATE), f32),
                   jax.ShapeDtypeStruct((CONV_W - 1, B, CONV_DIM), f32)],
        scratch_shapes=[pltpu.VMEM((nb, DT_PAD), f32)],
        compiler_params=_params("parallel"),
        name="ssd_sample",
    )(xbc, z, dt, cbuf, st, cw, cb, dtb, alog, dexp, ng)


def _s5_param_kernel(lr_ref, li_ref, ls_ref, br_ref, bi_ref, cr_ref, ci_ref,
                     ar_ref, ai_ref, bblk_ref, cblk_ref, b_scr, c_scr):
    lr, li = lr_ref[...], li_ref[...]
    step = jnp.exp(ls_ref[...])
    mag = jnp.exp(lr * step)
    abr = mag * jnp.cos(li * step)
    abi = mag * jnp.sin(li * step)
    nr, ni = abr - 1.0, abi
    den = lr * lr + li * li
    fr = (nr * lr + ni * li) / den
    fi = (ni * lr - nr * li) / den
    br, bi = br_ref[...], bi_ref[...]
    bbr = fr * br - fi * bi
    bbi = fr * bi + fi * br
    b_scr[...] = jnp.zeros_like(b_scr)
    c_scr[...] = jnp.zeros_like(c_scr)
    gps = S5_GROUPS // S5_SLABS
    ns = S5_SLAB_STATE
    for g in range(S5_GROUPS):
        s, gl = divmod(g, gps)
        ch = slice(gl * S5_CH, (gl + 1) * S5_CH)
        st = slice(gl * S5_STATE, (gl + 1) * S5_STATE)
        st_im = slice(ns + gl * S5_STATE, ns + (gl + 1) * S5_STATE)
        ar_ref[s, :, st] = jnp.broadcast_to(abr[g], (SUBLANES, S5_STATE))
        ai_ref[s, :, st] = jnp.broadcast_to(abi[g], (SUBLANES, S5_STATE))
        b_scr[s, ch, st] = bbr[g]
        b_scr[s, ch, st_im] = bbi[g]
        c_scr[s, st, ch] = cr_ref[g].T
        c_scr[s, st_im, ch] = -ci_ref[g].T
    bblk_ref[...] = b_scr[...].astype(bf16)
    cblk_ref[...] = c_scr[...].astype(bf16)


def _s5_params(a_re, a_im, log_step, b_re_t, b_im_t, c_re, c_im):
    g, p = a_re.shape
    ns = S5_SLAB_STATE
    return pl.pallas_call(
        _s5_param_kernel,
        out_shape=[jax.ShapeDtypeStruct((S5_SLABS, SUBLANES, ns), f32),
                   jax.ShapeDtypeStruct((S5_SLABS, SUBLANES, ns), f32),
                   jax.ShapeDtypeStruct((S5_SLABS, LANES, 2 * ns), bf16),
                   jax.ShapeDtypeStruct((S5_SLABS, 2 * ns, LANES), bf16)],
        scratch_shapes=[pltpu.VMEM((S5_SLABS, LANES, 2 * ns), f32),
                        pltpu.VMEM((S5_SLABS, 2 * ns, LANES), f32)],
        name="s5_params",
    )(a_re.reshape(g, 1, p), a_im.reshape(g, 1, p), log_step.reshape(g, 1, 1), b_re_t, b_im_t, c_re, c_im)


def _gelu(x):
    return 0.5 * x * (1.0 + lax.erf(x * (2.0 ** -0.5)))


def _s5_slabs_interleaved(ut_ref, hbuf_ref, hst_ref, g_ref, ar_ref, ai_ref, bblk_ref, cblk_ref, dsk_ref, tl,
                          row_block=256, n_stages=8):
    ns = S5_SLAB_STATE
    nb = SUBLANES
    rows = nb * tl

    def lanes(s):
        return slice(s * LANES, (s + 1) * LANES)

    def bu_stage(slabs):
        for s in slabs:
            for r in range(0, rows, row_block):
                us = ut_ref[s, r:r + row_block, :]
                hbuf_ref[s, nb + r:nb + r + row_block, :] = jnp.dot(us.astype(bf16), bblk_ref[s],
                                                                    preferred_element_type=f32)
                yield

    def scan_stage(slabs):
        carry = [(hst_ref[s, :, :ns], hst_ref[s, :, ns:]) for s in slabs]
        coef = [(ar_ref[s], ai_ref[s]) for s in slabs]
        for t in range(tl):
            r = nb + nb * t
            for i, s in enumerate(slabs):
                (pr, pi), (ar, ai) = carry[i], coef[i]
                nr = ar * pr - ai * pi + hbuf_ref[s, r:r + nb, :ns]
                ni = ar * pi + ai * pr + hbuf_ref[s, r:r + nb, ns:]
                hbuf_ref[s, r:r + nb, :ns] = nr
                hbuf_ref[s, r:r + nb, ns:] = ni
                carry[i] = (nr, ni)
            if (t + 1) % (tl // n_stages) == 0:
                yield
        for i, s in enumerate(slabs):
            hst_ref[s, :, :ns] = carry[i][0]
            hst_ref[s, :, ns:] = carry[i][1]

    def y_stage(slabs):
        for s in slabs:
            for r in range(0, rows, row_block):
                h16 = hbuf_ref[s, nb + r:nb + r + row_block, :].astype(bf16)
                ys = jnp.dot(h16, cblk_ref[s], preferred_element_type=f32)
                ys = ys + dsk_ref[:, lanes(s)] * ut_ref[s, r:r + row_block, :]
                g_ref[s, r:r + row_block, :] = _gelu(ys)
                yield

    half = S5_SLABS // 2
    first, second = tuple(range(half)), tuple(range(half, S5_SLABS))
    _interleave(bu_stage(first))
    _interleave(scan_stage(first), bu_stage(second))
    _interleave(scan_stage(second), y_stage(first))
    _interleave(y_stage(second))


def _s5_scan_vmem(hbuf_ref, hst_ref, ar_ref, ai_ref, s, nb, rows):
    ns = S5_SLAB_STATE
    hbuf_ref[s, 0:nb, :] = hst_ref[s]
    ar, ai = ar_ref[s], ai_ref[s]

    def body(i, carry):
        j = pl.multiple_of(nb + i * SUBLANES, SUBLANES)
        prev = hbuf_ref[s, pl.ds(j - nb, SUBLANES), :]
        cur = hbuf_ref[s, pl.ds(j, SUBLANES), :]
        pr, pi = prev[:, :ns], prev[:, ns:]
        hbuf_ref[s, pl.ds(j, SUBLANES), :ns] = ar * pr - ai * pi + cur[:, :ns]
        hbuf_ref[s, pl.ds(j, SUBLANES), ns:] = ar * pi + ai * pr + cur[:, ns:]
        return carry

    lax.fori_loop(0, rows // SUBLANES, body, 0, unroll=min(8, nb // SUBLANES))
    hst_ref[s] = hbuf_ref[s, rows:rows + nb, :]


S5_N_IN = 10


def _s5_kernel(*refs, nb, tl, batch_major, n_cast):
    (u_ref, re0_ref, im0_ref, ar_ref, ai_ref, bblk_ref, cblk_ref, dsk_ref, wglu_ref,
     bglu_ref) = refs[:S5_N_IN]
    cast_in = refs[S5_N_IN:S5_N_IN + n_cast]
    y_ref, ren_ref, imn_ref = refs[S5_N_IN + n_cast:S5_N_IN + n_cast + 3]
    cast_out = refs[S5_N_IN + n_cast + 3:S5_N_IN + 2 * n_cast + 3]
    ut_ref, hbuf_ref, hst_ref, g_ref = refs[S5_N_IN + 2 * n_cast + 3:]
    for w_ref, w16_ref in zip(cast_in, cast_out):
        w16_ref[...] = w_ref[...].astype(bf16)
    step = pl.program_id(0)
    rows = nb * tl
    ns = S5_SLAB_STATE

    @pl.when(step == 0)
    def _():
        for s in range(S5_SLABS):
            hst_ref[s, :, :ns] = re0_ref[:, s * ns:(s + 1) * ns]
            hst_ref[s, :, ns:] = im0_ref[:, s * ns:(s + 1) * ns]

    for s in range(S5_SLABS):
        sl = slice(s * LANES, (s + 1) * LANES)
        if batch_major:
            for b in range(nb):
                ut_ref[s, pl.ds(b, tl, stride=nb), :] = u_ref[b, :, sl]
        else:
            ut_ref[s] = u_ref[:, :, sl].reshape(rows, LANES)

    if nb == SUBLANES:
        _s5_slabs_interleaved(ut_ref, hbuf_ref, hst_ref, g_ref, ar_ref, ai_ref, bblk_ref, cblk_ref, dsk_ref, tl)
    else:
        for s in range(S5_SLABS):
            sl = slice(s * LANES, (s + 1) * LANES)
            hbuf_ref[s, nb:nb + rows, :] = jnp.dot(ut_ref[s].astype(bf16), bblk_ref[s],
                                                   preferred_element_type=f32)
            _s5_scan_vmem(hbuf_ref, hst_ref, ar_ref, ai_ref, s, nb, rows)
            ys = jnp.dot(hbuf_ref[s, nb:nb + rows, :].astype(bf16), cblk_ref[s], preferred_element_type=f32)
            g_ref[s] = _gelu(ys + dsk_ref[:, sl] * ut_ref[s])

    g = jnp.concatenate([g_ref[s] for s in range(S5_SLABS)], axis=-1)
    gate = jnp.dot(g.astype(bf16), wglu_ref[...], preferred_element_type=f32) + bglu_ref[...]
    out = g * jax.nn.sigmoid(gate)
    if batch_major:
        for s in range(S5_SLABS):
            sl = slice(s * LANES, (s + 1) * LANES)
            ut_ref[s] = out[:, sl]
            for b in range(nb):
                y_ref[b, :, sl] = ut_ref[s, pl.ds(b, tl, stride=nb), :]
    else:
        y_ref[...] = out.reshape(tl, nb, D_S5)

    @pl.when(step == pl.num_programs(0) - 1)
    def _():
        for s in range(S5_SLABS):
            ren_ref[:, s * ns:(s + 1) * ns] = hst_ref[s, :, :ns]
            imn_ref[:, s * ns:(s + 1) * ns] = hst_ref[s, :, ns:]


def _s5(u, re0, im0, ar, ai, bblk, cblk, dsk, wglu, bglu, *, tl, batch_major, cast=()):
    if batch_major:
        nb, L, _ = u.shape
        u_spec = pl.BlockSpec((nb, tl, D_S5), lambda i: (0, i, 0))
    else:
        L, nb, _ = u.shape
        u_spec = pl.BlockSpec((tl, nb, D_S5), lambda i: (i, 0, 0))
    rows = nb * tl
    steps = L // tl
    nstate = S5_GROUPS * S5_STATE
    st_spec = _const_spec((nb, nstate))
    cast_specs = [pl.BlockSpec((w.shape[0] // steps, w.shape[1]), lambda i: (i, 0)) for w in cast]
    assert all(w.shape[0] % (steps * 2 * SUBLANES) == 0 for w in cast)
    return pl.pallas_call(
        functools.partial(_s5_kernel, nb=nb, tl=tl, batch_major=batch_major, n_cast=len(cast)),
        grid=(steps,),
        in_specs=[u_spec, st_spec, st_spec,
                  _const_spec((S5_SLABS, SUBLANES, S5_SLAB_STATE)),
                  _const_spec((S5_SLABS, SUBLANES, S5_SLAB_STATE)),
                  _const_spec((S5_SLABS, LANES, 2 * S5_SLAB_STATE)),
                  _const_spec((S5_SLABS, 2 * S5_SLAB_STATE, LANES)),
                  _const_spec((1, D_S5)), _const_spec((D_S5, D_S5)), _const_spec((1, D_S5))] + cast_specs,
        out_specs=[u_spec, st_spec, st_spec] + cast_specs,
        out_shape=[jax.ShapeDtypeStruct(u.shape, f32),
                   jax.ShapeDtypeStruct((nb, nstate), f32),
                   jax.ShapeDtypeStruct((nb, nstate), f32)]
                  + [jax.ShapeDtypeStruct(w.shape, bf16) for w in cast],
        scratch_shapes=[pltpu.VMEM((S5_SLABS, rows, LANES), f32),
                        pltpu.VMEM((S5_SLABS, nb + rows, 2 * S5_SLAB_STATE), f32),
                        pltpu.VMEM((S5_SLABS, nb, 2 * S5_SLAB_STATE), f32),
                        pltpu.VMEM((S5_SLABS, rows, LANES), f32)],
        compiler_params=_params("arbitrary"),
        name="s5_mixer",
    )(u, re0, im0, ar, ai, bblk, cblk, dsk, wglu, bglu, *cast)


def _ffn_stages(x, ys, y5, mods, n2g, nfg, wo_ref, wg_ref, wu_ref, wd_ref, write_out, ff_chunk):
    g1, sh2, sc2, g2, shf, scf = mods
    att = jnp.dot(ys.astype(bf16), wo_ref[:D_SSD, :], preferred_element_type=f32)
    att = att + jnp.dot(y5.astype(bf16), wo_ref[D_SSD:, :], preferred_element_type=f32)
    yield
    x1 = x + _by_row(lambda v, s: v * s, att, g1)
    v = _rms_mod(x1, n2g, sc2, sh2).astype(bf16)
    ff = None
    for o in range(0, D_FF, ff_chunk):
        gate = jnp.dot(v, wg_ref[:, o:o + ff_chunk], preferred_element_type=f32)
        up = jnp.dot(v, wu_ref[:, o:o + ff_chunk], preferred_element_type=f32)
        hid = (_silu(gate) * up).astype(bf16)
        part = jnp.dot(hid, wd_ref[o:o + ff_chunk, :], preferred_element_type=f32)
        ff = part if ff is None else ff + part
        yield
    x2 = x1 + _by_row(lambda v, s: v * s, ff, g2)
    write_out(_rms_mod(x2, nfg, scf, shf))


def _ffn_kernel(x_ref, ys_ref, y5_ref, mod_ref, modf_ref, n2g_ref, nfg_ref, wo_ref, wg_ref, wu_ref, wd_ref,
                o_ref, *, seq_major, ff_chunk, sub):
    mods = (_mod_row(mod_ref, 2, seq_major), _mod_row(mod_ref, 3, seq_major), _mod_row(mod_ref, 4, seq_major),
            _mod_row(mod_ref, 5, seq_major), _mod_row(modf_ref, 0, seq_major), _mod_row(modf_ref, 1, seq_major))
    x = _rows_in(x_ref, seq_major)
    rows = x.shape[0]
    tiles = {}

    def stages(r0):
        def write_out(y):
            if seq_major:
                tiles[r0] = y
            else:
                o_ref[0, r0:r0 + sub, :] = y
        return _ffn_stages(x[r0:r0 + sub], ys_ref[0, r0:r0 + sub, :], y5_ref[0, r0:r0 + sub, :], mods,
                           n2g_ref[...], nfg_ref[...], wo_ref, wg_ref, wu_ref, wd_ref, write_out, ff_chunk)

    _interleave(*[stages(r0) for r0 in range(0, rows, sub)])
    if seq_major:
        s, t, d = o_ref.shape
        y = jnp.concatenate([tiles[r0] for r0 in range(0, rows, sub)], axis=0)
        o_ref[...] = jnp.swapaxes(y.reshape(t, s, d), 0, 1)


def _ffn(x, ys, y5, mod, modf, n2g, nfg, wo, wg, wu, wd, *, tm, sub, seq_major, n_seq, first_seq, ff_chunk=256):
    if seq_major:
        x_spec = _const_spec(x.shape)
        nb, rows, d = 1, x.shape[0] * x.shape[1], x.shape[2]
        assert tm == rows
    else:
        nb, rows, d = x.shape
        x_spec = pl.BlockSpec((1, tm, d), lambda i, j: (i, j, 0))
    nt = rows // tm
    mod_spec = _mod_spec(mod, n_seq, first_seq)
    modf_spec = _mod_spec(modf, n_seq, first_seq)
    blk = lambda wd_: pl.BlockSpec((1, tm, wd_), lambda i, j: (i, j, 0))
    single = dict(pipeline_mode=pl.Buffered(1))
    wspec = lambda shape: pl.BlockSpec(shape, lambda i, j: (0, 0), **single)
    return pl.pallas_call(
        functools.partial(_ffn_kernel, seq_major=seq_major, ff_chunk=ff_chunk, sub=sub),
        grid=(nb, nt),
        in_specs=[x_spec, blk(D_SSD), blk(D_S5), mod_spec, modf_spec,
                  _const_spec((1, d)), _const_spec((1, d)),
                  wspec((d, d)), wspec((d, D_FF)), wspec((d, D_FF)), wspec((D_FF, d))],
        out_specs=x_spec,
        out_shape=jax.ShapeDtypeStruct(x.shape, f32),
        compiler_params=_params("parallel", "parallel"),
        name="out_ffn",
    )(x, ys, y5, mod, modf, n2g, nfg, wo, wg, wu, wd)


def kernel(x_prompt, x_sample, c_prompt, c_sample, state_ssd, state_conv, state_s5_re, state_s5_im, w_ada, b_ada, norm1_g, w_in, conv_w, conv_b, ssd_dt_bias, ssd_A_log, ssd_D, ssd_norm_g, s5_A_re, s5_A_im, s5_log_step, s5_B_re, s5_B_im, s5_C_re, s5_C_im, s5_D, w_glu, b_glu, w_out, norm2_g, w_ffn_gate, w_ffn_up, w_ffn_down, w_ada_f, b_ada_f, normf_g):
    assert w_ada.shape[0] == 1, "single-layer stack"
    bp, seq, d = x_prompt.shape
    bs, dseq, _ = x_sample.shape

    c_all = jnp.concatenate([c_sample, c_prompt], axis=0)
    mod = _ada_mod(c_all, w_ada[0], b_ada[0])
    modf = _ada_mod(c_all, w_ada_f, b_ada_f)
    seqs_s = dict(n_seq=bs, first_seq=0)
    seqs_p = dict(n_seq=bp, first_seq=bs)

    w_in_p = w_in[0].T
    pad_h = lambda v: jnp.concatenate([v, jnp.zeros((DT_PAD - HEADS,), f32)]).reshape(1, DT_PAD)
    dtb = pad_h(ssd_dt_bias[0])
    alog = pad_h(ssd_A_log[0])
    dexp = jnp.repeat(ssd_D[0], HEAD_DIM).reshape(1, D_SSD)
    ng = ssd_norm_g[0].reshape(1, D_SSD)
    cw, cb = conv_w[0], conv_b[0].reshape(1, CONV_DIM)

    ar, ai, bblk, cblk = _s5_params(s5_A_re[0], s5_A_im[0], s5_log_step[0],
                                    s5_B_re[0].transpose(0, 2, 1), s5_B_im[0].transpose(0, 2, 1),
                                    s5_C_re[0], s5_C_im[0])
    dsk = s5_D[0].reshape(1, D_S5)
    wglu = w_glu[0].astype(bf16)
    bglu = b_glu[0].reshape(1, D_S5)
    n1g, n2g, nfg = norm1_g[0].reshape(1, d), norm2_g[0].reshape(1, d), normf_g.reshape(1, d)
    nstate = S5_GROUPS * S5_STATE

    z, act, u5, dt, conv_p = _inproj(x_prompt, mod, n1g, w_in_p, (cw, cb), tm=1024, sub=512, seq_major=False,
                                     **seqs_p)
    y_ssd, ssd_p = _ssd_prompt(act, z, dt, dtb, alog, dexp, ng)
    zeros_st = jnp.zeros((bp, nstate), f32)
    y_s5, re_p, im_p, wo, wg, wu, wd = _s5(
        u5, zeros_st, zeros_st, ar, ai, bblk, cblk, dsk, wglu, bglu, tl=128, batch_major=True,
        cast=(w_out[0], w_ffn_gate[0], w_ffn_up[0], w_ffn_down[0]))
    y_prompt = _ffn(x_prompt, y_ssd, y_s5, mod, modf, n2g, nfg, wo, wg, wu, wd, tm=1024, sub=512,
                    seq_major=False, **seqs_p)

    rows_s = dseq * bs
    steps = lambda a: a.reshape(dseq, bs, a.shape[-1])
    flat = lambda a: a.reshape(1, rows_s, a.shape[-1])
    z, xbc, u5, dt = _inproj(x_sample, mod, n1g, w_in_p, tm=rows_s, sub=rows_s // 2, seq_major=True, **seqs_s)
    y_ssd, ssd_s, conv_s = _ssd_sample(steps(xbc), steps(z), steps(dt), state_conv[0].transpose(1, 0, 2),
                                       state_ssd[0], cw, cb, dtb, alog, dexp, ng)
    y_s5, re_s, im_s = _s5(steps(u5), state_s5_re[0].reshape(bs, nstate), state_s5_im[0].reshape(bs, nstate),
                           ar, ai, bblk, cblk, dsk, wglu, bglu, tl=dseq, batch_major=False)
    y_sample = _ffn(x_sample, flat(y_ssd), flat(y_s5), mod, modf, n2g, nfg, wo, wg, wu, wd,
                    tm=rows_s, sub=rows_s, seq_major=True, **seqs_s)

    g5 = (S5_GROUPS, S5_STATE)
    return (y_prompt, y_sample,
            ssd_p[None], ssd_s.reshape((1,) + state_ssd.shape[1:]),
            conv_p[None], conv_s.transpose(1, 0, 2)[None],
            re_p.reshape((1, bp) + g5), re_s.reshape((1, bs) + g5),
            im_p.reshape((1, bp) + g5), im_s.reshape((1, bs) + g5))
```

```python
import functools

import jax
import jax.numpy as jnp
from jax import lax
from jax.experimental import pallas as pl
from jax.experimental.pallas import tpu as pltpu

f32 = jnp.float32
bf16 = jnp.bfloat16

D_MODEL = 1024
D_SSD = 512
HEAD_DIM = 64
HEADS = 8
GROUPS = 2
HEADS_PER_GROUP = HEADS // GROUPS
STATE = 128
CONV_W = 4
CONV_DIM = D_SSD + 2 * GROUPS * STATE
D_S5 = 512
S5_CH = 16
S5_GROUPS = 32
S5_STATE = 64
D_FF = 2816
N_ADA = 6
EPS = 1e-6

LANES = 128
SUBLANES = 8
SSD_CHUNK = 128
S5_SLABS = D_S5 // LANES
S5_SLAB_STATE = (S5_GROUPS // S5_SLABS) * S5_STATE
DT_PAD = LANES
VMEM_LIMIT = 56 * 1024 * 1024

NT_DIMS = (((1,), (1,)), ((), ()))
TN_DIMS = (((0,), (0,)), ((), ()))


def _silu(x):
    return x * jax.nn.sigmoid(x)


def _interleave(*gens, lead=0):
    pending, live, rnd = list(gens), [], 0
    while pending or live:
        while pending and rnd >= lead * (len(gens) - len(pending)):
            live.append(pending.pop(0))
        for g in list(live):
            try:
                next(g)
            except StopIteration:
                live.remove(g)
        rnd += 1


def _by_row(fn, v, *ms):
    r, d = v.shape
    m_rows = ms[0].shape[0]
    if m_rows in (1, r):
        return fn(v, *ms)
    out = fn(v.reshape(r // m_rows, m_rows, d), *[m[None] for m in ms])
    return out.reshape(r, d)


def _rms_mod(x, g, sc, sh):
    y = x * lax.rsqrt(jnp.mean(x * x, axis=-1, keepdims=True) + EPS)
    return _by_row(lambda v, s, t: v * (1.0 + s) + t, y * g, sc, sh)


def _mod_row(mod_ref, i, per_row):
    return mod_ref[i] if per_row else mod_ref[i, pl.ds(pl.program_id(0), 1), :]


def _mod_spec(mod, n_seq, first_seq):
    assert first_seq % n_seq == 0
    return pl.BlockSpec((mod.shape[0], n_seq, mod.shape[2]), lambda *_: (0, first_seq // n_seq, 0))


def _const_spec(shape):
    nd = len(shape)
    return pl.BlockSpec(shape, lambda *_: (0,) * nd)


def _params(*sem):
    return pltpu.CompilerParams(dimension_semantics=sem, vmem_limit_bytes=VMEM_LIMIT)


ADA_K_BLOCK = 256


def _ada_kernel(c_ref, *refs):
    nw = len(refs) // 3
    s = _silu(c_ref[...]).astype(bf16)
    for w_ref, b_ref, o_ref in zip(refs[:nw], refs[nw:2 * nw], refs[2 * nw:]):
        n, _, d = o_ref.shape

        @pl.when(pl.program_id(0) == 0)
        def _():
            for j in range(n):
                o_ref[j] = jnp.broadcast_to(b_ref[:, j * d:(j + 1) * d], o_ref.shape[1:])

        for j in range(n):
            o_ref[j] += jnp.dot(s, w_ref[:, j * d:(j + 1) * d].astype(bf16), preferred_element_type=f32)


def _ada_mod(c, ws, bs):
    m, k = c.shape
    ns = [w.shape[1] // k for w in ws]
    return pl.pallas_call(
        _ada_kernel,
        grid=(k // ADA_K_BLOCK,),
        in_specs=[pl.BlockSpec((m, ADA_K_BLOCK), lambda i: (0, i))]
                 + [pl.BlockSpec((ADA_K_BLOCK, n * k), lambda i: (i, 0)) for n in ns]
                 + [_const_spec((1, n * k)) for n in ns],
        out_specs=[_const_spec((n, m, k)) for n in ns],
        out_shape=[jax.ShapeDtypeStruct((n, m, k), f32) for n in ns],
        compiler_params=_params("arbitrary"),
        name="ada_mod",
    )(c, *ws, *[b.reshape(1, n * k) for b, n in zip(bs, ns)])


def _rows_in(x_ref, seq_major):
    if not seq_major:
        return x_ref[0]
    s, t, d = x_ref.shape
    return jnp.swapaxes(x_ref[...], 0, 1).reshape(s * t, d)


def _conv_silu(ext, cw_ref, cb_ref):
    w = [cw_ref[k:k + 1, :] for k in range(CONV_W)]
    back1 = pltpu.roll(ext, 1, 0)
    older = pltpu.roll(ext * w[1] + back1 * w[0], 2, 0)
    conv = cb_ref[...] + ext[SUBLANES:, :] * w[3] + back1[SUBLANES:, :] * w[2] + older[SUBLANES:, :]
    return _silu(conv)


def _inproj_stages(x, sh, sc, g, outs, after=()):
    u = _rms_mod(x, g, sc, sh).astype(bf16)
    yield
    for write, w16 in outs:
        write(lax.dot_general(u, w16, NT_DIMS, preferred_element_type=f32))
        yield
    for stage in after:
        stage()
        yield


def _inproj_kernel(x_ref, mod_ref, g_ref, w_ref, *rest, seq_major, sub, conv):
    if conv:
        cw_ref, cb_ref, z_ref, act_ref, u5_ref, dt_ref, cn_ref, ext_ref = rest

        @pl.when(pl.program_id(1) == 0)
        def _():
            ext_ref[0:SUBLANES, :] = jnp.zeros((SUBLANES, CONV_DIM), f32)
    else:
        z_ref, xbc_ref, u5_ref, dt_ref = rest
    sh = _mod_row(mod_ref, 0, seq_major)
    sc = _mod_row(mod_ref, 1, seq_major)
    x = _rows_in(x_ref, seq_major)
    rows = x.shape[0]
    T = SSD_CHUNK
    o_dt = D_SSD + CONV_DIM
    w16 = lambda o, width: w_ref[o:o + width, :].astype(bf16)
    w_z, w_xbc, w_u5, w_dt = w16(0, D_SSD), w16(D_SSD, CONV_DIM), w16(o_dt + HEADS, D_S5), w16(o_dt, DT_PAD)

    def stages(r0):
        def to(ref):
            def write(v):
                ref[0, r0:r0 + sub, :] = v
            return write

        def to_ext(v):
            ext_ref[SUBLANES + r0:SUBLANES + r0 + sub, :] = v

        def conv_chunk(c0):
            def stage():
                act_ref[0, c0:c0 + T, :] = _conv_silu(ext_ref[c0:c0 + SUBLANES + T, :], cw_ref, cb_ref)
            return stage

        outs = [(to(z_ref), w_z), (to_ext if conv else to(xbc_ref), w_xbc), (to(u5_ref), w_u5), (to(dt_ref), w_dt)]
        after = [conv_chunk(c0) for c0 in range(r0, r0 + sub, T)] if conv else ()
        return _inproj_stages(x[r0:r0 + sub], sh, sc, g_ref[...], outs, after)

    n_matmul_stages = 5
    _interleave(*[stages(r0) for r0 in range(0, rows, sub)], lead=n_matmul_stages if conv else 0)
    if conv:
        cn_ref[0] = ext_ref[SUBLANES + rows - (CONV_W - 1):SUBLANES + rows, :]
        ext_ref[0:SUBLANES, :] = ext_ref[rows:rows + SUBLANES, :]


def _inproj(x, mod, g, w, conv_wb=None, *, tm, sub, seq_major, n_seq, first_seq):
    if seq_major:
        x_spec = _const_spec(x.shape)
        nb, rows, d = 1, x.shape[0] * x.shape[1], x.shape[2]
        assert tm == rows
    else:
        nb, rows, d = x.shape
        x_spec = pl.BlockSpec((1, tm, d), lambda i, j: (i, j, 0))
    nt = rows // tm
    widths = (D_SSD, CONV_DIM, D_S5, DT_PAD)
    conv = conv_wb is not None
    in_specs = [x_spec, _mod_spec(mod, n_seq, first_seq), _const_spec((1, d)),
                pl.BlockSpec(w.shape, lambda i, j: (0, 0), pipeline_mode=pl.Buffered(1))]
    out_specs = [pl.BlockSpec((1, tm, wd), lambda i, j: (i, j, 0)) for wd in widths]
    out_shape = [jax.ShapeDtypeStruct((nb, rows, wd), f32) for wd in widths]
    scratch = []
    if conv:
        in_specs += [_const_spec((CONV_W, CONV_DIM)), _const_spec((1, CONV_DIM))]
        out_specs.append(pl.BlockSpec((1, CONV_W - 1, CONV_DIM), lambda i, j: (i, 0, 0)))
        out_shape.append(jax.ShapeDtypeStruct((nb, CONV_W - 1, CONV_DIM), f32))
        scratch.append(pltpu.VMEM((SUBLANES + tm, CONV_DIM), f32))
    return pl.pallas_call(
        functools.partial(_inproj_kernel, seq_major=seq_major, sub=sub, conv=conv),
        grid=(nb, nt),
        in_specs=in_specs, out_specs=out_specs, out_shape=out_shape, scratch_shapes=scratch,
        compiler_params=_params("parallel", "arbitrary" if conv else "parallel"),
        name="in_proj",
    )(x, mod, g, w, *(conv_wb or ()))


def _split3(x):
    hi = x.astype(bf16)
    r1 = x - hi.astype(f32)
    mid = r1.astype(bf16)
    lo = (r1 - mid.astype(f32)).astype(bf16)
    return hi, mid, lo


def _dot_sel_lhs(sel16, x):
    return sum(jnp.dot(sel16, p, preferred_element_type=f32) for p in _split3(x))


def _dot_sel_rhs(x, sel16):
    hi, mid, _ = _split3(x)
    return jnp.dot(hi, sel16, preferred_element_type=f32) + jnp.dot(mid, sel16, preferred_element_type=f32)


def _gated_group_norm(y, z, ng):
    y = y * _silu(z)
    gw = D_SSD // GROUPS
    parts = []
    for g in range(GROUPS):
        yg = y[:, g * gw:(g + 1) * gw]
        parts.append(yg * lax.rsqrt(jnp.mean(yg * yg, axis=-1, keepdims=True) + EPS))
    return jnp.concatenate(parts, axis=-1) * ng


def _ssd_prompt_kernel(act_ref, z_ref, dt_ref, dtb_ref, alog_ref, dexp_ref, ng_ref, y_ref, st_ref, h_ref, *, cps):
    T = SSD_CHUNK
    gw = HEADS_PER_GROUP * HEAD_DIM

    @pl.when(pl.program_id(1) == 0)
    def _():
        h_ref[...] = jnp.zeros_like(h_ref)

    row = lax.broadcasted_iota(jnp.int32, (T, T), 0)
    col = lax.broadcasted_iota(jnp.int32, (T, T), 1)
    tri = row >= col
    tri16 = jnp.where(tri, 1.0, 0.0).astype(bf16)
    low_half = lax.broadcasted_iota(jnp.int32, (T, LANES), 1) < HEAD_DIM
    a_neg = -jnp.exp(alog_ref[...])

    lane_head = lax.broadcasted_iota(jnp.int32, (DT_PAD, D_SSD), 1) // HEAD_DIM
    sel16 = jnp.where(lax.broadcasted_iota(jnp.int32, (DT_PAD, D_SSD), 0) == lane_head, 1.0, 0.0).astype(bf16)

    def spread(q):
        return _dot_sel_rhs(q, sel16)

    def state_free(ci):
        r0 = ci * T
        act = act_ref[0, r0:r0 + T, :]
        xs = act[:, :D_SSD]
        bm = act[:, D_SSD:D_SSD + GROUPS * STATE]
        cm = act[:, D_SSD + GROUPS * STATE:]
        dtv = jax.nn.softplus(dt_ref[0, r0:r0 + T, :] + dtb_ref[...])
        a_cs = _dot_sel_lhs(tri16, dtv * a_neg)
        bg16s = [bm[:, g * STATE:(g + 1) * STATE].astype(bf16) for g in range(GROUPS)]
        cg16s = [cm[:, g * STATE:(g + 1) * STATE].astype(bf16) for g in range(GROUPS)]
        gmats = [lax.dot_general(cg16s[g], bg16s[g], NT_DIMS, preferred_element_type=f32)
                 for g in range(GROUPS)]
        yield
        a_last = a_cs[T - 1:T, :]
        a_cs_t = a_cs.T
        rep = spread(jnp.concatenate([dtv, jnp.exp(a_last - a_cs), jnp.exp(a_cs)], axis=0))
        yield
        x = xs * rep[:T]
        x16 = x.astype(bf16)
        xd16 = (x * rep[T:2 * T]).astype(bf16)
        e_cs = rep[2 * T:]
        yield
        y_part = []
        for g in range(GROUPS):
            for pr in range(HEADS_PER_GROUP // 2):
                s_pair = []
                for q in range(2):
                    h = g * HEADS_PER_GROUP + 2 * pr + q
                    seg = a_cs[:, h:h + 1] - a_cs_t[h:h + 1, :]
                    lmat = jnp.exp(jnp.where(tri, seg, -jnp.inf))
                    s_pair.append((gmats[g] * lmat).astype(bf16))
                lanes = slice(g * gw + pr * LANES, g * gw + (pr + 1) * LANES)
                both = jnp.dot(jnp.concatenate(s_pair, axis=0), x16[:, lanes], preferred_element_type=f32)
                yield
                y_part.append(jnp.where(low_half, both[:T], both[T:]))
        y_free = jnp.concatenate(y_part, axis=-1) + xs * dexp_ref[...]
        return y_free, e_cs, jnp.exp(a_last), xd16, bg16s, cg16s

    def state_step(ci, y_free, e_cs, e_last, xd16, bg16s, cg16s):
        r0 = ci * T
        y_offs = []
        for g in range(GROUPS):
            hp = h_ref[g]
            y_offs.append(lax.dot_general(cg16s[g], hp.astype(bf16), NT_DIMS, preferred_element_type=f32))
            upd = lax.dot_general(xd16[:, g * gw:(g + 1) * gw], bg16s[g], TN_DIMS, preferred_element_type=f32)
            for hh in range(HEADS_PER_GROUP):
                h = g * HEADS_PER_GROUP + hh
                rows = slice(hh * HEAD_DIM, (hh + 1) * HEAD_DIM)
                h_ref[g, rows, :] = e_last[:, h:h + 1] * hp[rows] + upd[rows]
        y = y_free + jnp.concatenate(y_offs, axis=-1) * e_cs
        y_ref[0, r0:r0 + T, :] = _gated_group_norm(y, z_ref[0, r0:r0 + T, :], ng_ref[...])

    gens = [state_free(ci) for ci in range(cps)]
    free = [None] * cps
    while any(f is None for f in free):
        for ci in range(cps):
            if free[ci] is None:
                try:
                    next(gens[ci])
                except StopIteration as done:
                    free[ci] = done.value
    for ci in range(cps):
        state_step(ci, *free[ci])

    @pl.when(pl.program_id(1) == pl.num_programs(1) - 1)
    def _():
        st_ref[0] = h_ref[...].reshape(HEADS, HEAD_DIM, STATE)


def _ssd_prompt(act, z, dt, dtb, alog, dexp, ng, *, cps=8):
    nb, L, _ = act.shape
    rows = cps * SSD_CHUNK
    blk = lambda wd: pl.BlockSpec((1, rows, wd), lambda b, c: (b, c, 0))
    return pl.pallas_call(
        functools.partial(_ssd_prompt_kernel, cps=cps),
        grid=(nb, L // rows),
        in_specs=[blk(CONV_DIM), blk(D_SSD), blk(DT_PAD), _const_spec((1, DT_PAD)), _const_spec((1, DT_PAD)),
                  _const_spec((1, D_SSD)), _const_spec((1, D_SSD))],
        out_specs=[blk(D_SSD), pl.BlockSpec((1, HEADS, HEAD_DIM, STATE), lambda b, c: (b, 0, 0, 0))],
        out_shape=[jax.ShapeDtypeStruct((nb, L, D_SSD), f32),
                   jax.ShapeDtypeStruct((nb, HEADS, HEAD_DIM, STATE), f32)],
        scratch_shapes=[pltpu.VMEM((GROUPS, HEADS_PER_GROUP * HEAD_DIM, STATE), f32)],
        compiler_params=_params("parallel", "arbitrary"),
        name="ssd_prompt",
    )(act, z, dt, dtb, alog, dexp, ng)


def _ssd_sample_kernel(xbc_ref, z_ref, dt_ref, cbuf_ref, st_ref, cw_ref, cb_ref, dtb_ref, alog_ref,
                       dexp_ref, ng_ref, y_ref, stn_ref, cn_ref, dtot_ref, *, L, nb):
    gw = HEADS_PER_GROUP * HEAD_DIM
    full = [cbuf_ref[j] for j in range(CONV_W - 1)] + [xbc_ref[t] for t in range(L)]
    for j in range(CONV_W - 1):
        cn_ref[j] = full[L + j]
    a_neg = -jnp.exp(alog_ref[...])
    xs, bm, cm, dtv, acs = [], [], [], [], []
    run = None
    for t in range(L):
        conv = cb_ref[...]
        for k in range(CONV_W):
            conv = conv + full[t + k] * cw_ref[k:k + 1, :]
        act = _silu(conv)
        xs.append(act[:, :D_SSD])
        bm.append(act[:, D_SSD:D_SSD + GROUPS * STATE])
        cm.append(act[:, D_SSD + GROUPS * STATE:])
        d = jax.nn.softplus(dt_ref[t] + dtb_ref[...])
        dtv.append(d)
        run = d * a_neg if run is None else run + d * a_neg
        acs.append(run)
    a_tot = acs[L - 1]
    dtot_ref[...] = jnp.exp(a_tot)

    lane_head = lax.broadcasted_iota(jnp.int32, (DT_PAD, D_SSD), 1) // HEAD_DIM
    sel16 = jnp.where(lax.broadcasted_iota(jnp.int32, (DT_PAD, D_SSD), 0) == lane_head, 1.0, 0.0).astype(bf16)
    pairs = [(t, s) for t in range(L) for s in range(t)]
    factors = (dtv + [jnp.exp(a_tot - acs[t]) for t in range(L)] + [jnp.exp(acs[t]) for t in range(L)]
               + [jnp.exp(acs[t] - acs[s]) for t, s in pairs])
    rep = _dot_sel_rhs(jnp.concatenate(factors, axis=0), sel16)
    piece = lambda i: rep[i * nb:(i + 1) * nb]
    x = [xs[t] * piece(t) for t in range(L)]
    xd_stack = jnp.concatenate([x[t] * piece(L + t) for t in range(L)], axis=0)
    e_cs = [piece(2 * L + t) for t in range(L)]
    decay = {ts: piece(3 * L + i) for i, ts in enumerate(pairs)}

    in_group0 = lax.broadcasted_iota(jnp.int32, (nb, D_SSD), 1) < gw
    y_intra = []
    for t in range(L):
        acc = None
        for s in range(t + 1):
            cb_dot = [jnp.sum(cm[t][:, g * STATE:(g + 1) * STATE] * bm[s][:, g * STATE:(g + 1) * STATE],
                              axis=-1, keepdims=True) for g in range(GROUPS)]
            w = jnp.where(in_group0, cb_dot[0], cb_dot[1])
            term = w * x[s] if s == t else w * decay[(t, s)] * x[s]
            acc = term if acc is None else acc + term
        y_intra.append(acc)

    c_stack = [jnp.concatenate([cm[t][:, g * STATE:(g + 1) * STATE] for t in range(L)], axis=0).astype(bf16)
               for g in range(GROUPS)]
    b_stack = [jnp.concatenate([bm[t][:, g * STATE:(g + 1) * STATE] for t in range(L)], axis=0).astype(bf16)
               for g in range(GROUPS)]
    seq_of_row = lax.broadcasted_iota(jnp.int32, (L * nb, gw), 0) & (nb - 1)

    def per_seq(b, yoff):
        mine = seq_of_row == b
        drow = dtot_ref[pl.ds(b, 1), :]
        out = []
        for g in range(GROUPS):
            h0 = st_ref[b, g]
            r = lax.dot_general(c_stack[g], h0.astype(bf16), NT_DIMS, preferred_element_type=f32)
            out.append(yoff[g] + jnp.where(mine, r, 0.0))
            xm = jnp.where(mine, xd_stack[:, g * gw:(g + 1) * gw], 0.0).astype(bf16)
            upd = lax.dot_general(xm, b_stack[g], TN_DIMS, preferred_element_type=f32)
            for hh in range(HEADS_PER_GROUP):
                h = g * HEADS_PER_GROUP + hh
                rows = slice(hh * HEAD_DIM, (hh + 1) * HEAD_DIM)
                stn_ref[b, g, rows, :] = drow[:, h:h + 1] * h0[rows] + upd[rows]
        return tuple(out)

    yoff = lax.fori_loop(0, nb, per_seq, tuple(jnp.zeros((L * nb, gw), f32) for _ in range(GROUPS)),
                         unroll=8)

    for t in range(L):
        y_off = jnp.concatenate([yoff[g][t * nb:(t + 1) * nb] for g in range(GROUPS)], axis=-1)
        y = y_intra[t] + y_off * e_cs[t] + xs[t] * dexp_ref[...]
        y_ref[t] = _gated_group_norm(y, z_ref[t], ng_ref[...])


def _ssd_sample(xbc, z, dt, cbuf, st, cw, cb, dtb, alog, dexp, ng, *, nb=16):
    L, B, _ = xbc.shape
    assert nb & (nb - 1) == 0 and B % nb == 0
    tblk = lambda n, wd: pl.BlockSpec((n, nb, wd), lambda i: (0, i, 0))
    gw = HEADS_PER_GROUP * HEAD_DIM
    st = st.reshape(B, GROUPS, gw, STATE)
    st_spec = pl.BlockSpec((nb, GROUPS, gw, STATE), lambda i: (i, 0, 0, 0))
    return pl.pallas_call(
        functools.partial(_ssd_sample_kernel, L=L, nb=nb),
        grid=(B // nb,),
        in_specs=[tblk(L, CONV_DIM), tblk(L, D_SSD), tblk(L, DT_PAD), tblk(CONV_W - 1, CONV_DIM), st_spec,
                  _const_spec((CONV_W, CONV_DIM)), _const_spec((1, CONV_DIM)),
                  _const_spec((1, DT_PAD)), _const_spec((1, DT_PAD)),
                  _const_spec((1, D_SSD)), _const_spec((1, D_SSD))],
        out_specs=[tblk(L, D_SSD), st_spec, tblk(CONV_W - 1, CONV_DIM)],
        out_shape=[jax.ShapeDtypeStruct((L, B, D_SSD), f32),
                   jax.ShapeDtypeStruct((B, GROUPS, gw, STATE), f32),
                   jax.ShapeDtypeStruct((CONV_W - 1, B, CONV_DIM), f32)],
        scratch_shapes=[pltpu.VMEM((nb, DT_PAD), f32)],
        compiler_params=_params("parallel"),
        name="ssd_sample",
    )(xbc, z, dt, cbuf, st, cw, cb, dtb, alog, dexp, ng)


def _s5_param_kernel(lr_ref, li_ref, ls_ref, br_ref, bi_ref, cr_ref, ci_ref,
                     ar_ref, ai_ref, bblk_ref, cblk_ref, b_scr, c_scr):
    lr, li = lr_ref[...], li_ref[...]
    step = jnp.exp(ls_ref[...])
    mag = jnp.exp(lr * step)
    abr = mag * jnp.cos(li * step)
    abi = mag * jnp.sin(li * step)
    nr, ni = abr - 1.0, abi
    den = lr * lr + li * li
    fr = (nr * lr + ni * li) / den
    fi = (ni * lr - nr * li) / den
    br, bi = br_ref[...], bi_ref[...]
    bbr = fr * br - fi * bi
    bbi = fr * bi + fi * br
    b_scr[...] = jnp.zeros_like(b_scr)
    c_scr[...] = jnp.zeros_like(c_scr)
    gps = S5_GROUPS // S5_SLABS
    ns = S5_SLAB_STATE
    for g in range(S5_GROUPS):
        s, gl = divmod(g, gps)
        ch = slice(gl * S5_CH, (gl + 1) * S5_CH)
        st = slice(gl * S5_STATE, (gl + 1) * S5_STATE)
        st_im = slice(ns + gl * S5_STATE, ns + (gl + 1) * S5_STATE)
        ar_ref[s, :, st] = jnp.broadcast_to(abr[g], (SUBLANES, S5_STATE))
        ai_ref[s, :, st] = jnp.broadcast_to(abi[g], (SUBLANES, S5_STATE))
        b_scr[s, ch, st] = bbr[g]
        b_scr[s, ch, st_im] = bbi[g]
        c_scr[s, st, ch] = cr_ref[g].T
        c_scr[s, st_im, ch] = -ci_ref[g].T
    bblk_ref[...] = b_scr[...].astype(bf16)
    cblk_ref[...] = c_scr[...].astype(bf16)


def _s5_params(a_re, a_im, log_step, b_re_t, b_im_t, c_re, c_im):
    g, p = a_re.shape
    ns = S5_SLAB_STATE
    return pl.pallas_call(
        _s5_param_kernel,
        out_shape=[jax.ShapeDtypeStruct((S5_SLABS, SUBLANES, ns), f32),
                   jax.ShapeDtypeStruct((S5_SLABS, SUBLANES, ns), f32),
                   jax.ShapeDtypeStruct((S5_SLABS, LANES, 2 * ns), bf16),
                   jax.ShapeDtypeStruct((S5_SLABS, 2 * ns, LANES), bf16)],
        scratch_shapes=[pltpu.VMEM((S5_SLABS, LANES, 2 * ns), f32),
                        pltpu.VMEM((S5_SLABS, 2 * ns, LANES), f32)],
        name="s5_params",
    )(a_re.reshape(g, 1, p), a_im.reshape(g, 1, p), log_step.reshape(g, 1, 1), b_re_t, b_im_t, c_re, c_im)


def _gelu(x):
    return 0.5 * x * (1.0 + lax.erf(x * (2.0 ** -0.5)))


def _s5_slabs_interleaved(ut_ref, hbuf_ref, hst_ref, g_ref, ar_ref, ai_ref, bblk_ref, cblk_ref, dsk_ref, tl,
                          row_block=256, n_stages=8):
    ns = S5_SLAB_STATE
    nb = SUBLANES
    rows = nb * tl

    def lanes(s):
        return slice(s * LANES, (s + 1) * LANES)

    def bu_stage(slabs):
        for s in slabs:
            for r in range(0, rows, row_block):
                us = ut_ref[s, r:r + row_block, :]
                hbuf_ref[s, nb + r:nb + r + row_block, :] = jnp.dot(us.astype(bf16), bblk_ref[s],
                                                                    preferred_element_type=f32)
                yield

    def scan_stage(slabs):
        carry = [(hst_ref[s, :, :ns], hst_ref[s, :, ns:]) for s in slabs]
        coef = [(ar_ref[s], ai_ref[s]) for s in slabs]
        for t in range(tl):
            r = nb + nb * t
            for i, s in enumerate(slabs):
                (pr, pi), (ar, ai) = carry[i], coef[i]
                nr = ar * pr - ai * pi + hbuf_ref[s, r:r + nb, :ns]
                ni = ar * pi + ai * pr + hbuf_ref[s, r:r + nb, ns:]
                hbuf_ref[s, r:r + nb, :ns] = nr
                hbuf_ref[s, r:r + nb, ns:] = ni
                carry[i] = (nr, ni)
            if (t + 1) % (tl // n_stages) == 0:
                yield
        for i, s in enumerate(slabs):
            hst_ref[s, :, :ns] = carry[i][0]
            hst_ref[s, :, ns:] = carry[i][1]

    def y_stage(slabs):
        for s in slabs:
            for r in range(0, rows, row_block):
                h16 = hbuf_ref[s, nb + r:nb + r + row_block, :].astype(bf16)
                ys = jnp.dot(h16, cblk_ref[s], preferred_element_type=f32)
                ys = ys + dsk_ref[:, lanes(s)] * ut_ref[s, r:r + row_block, :]
                g_ref[s, r:r + row_block, :] = _gelu(ys)
                yield

    half = S5_SLABS // 2
    first, second = tuple(range(half)), tuple(range(half, S5_SLABS))
    _interleave(bu_stage(first))
    _interleave(scan_stage(first), bu_stage(second))
    _interleave(scan_stage(second), y_stage(first))
    _interleave(y_stage(second))


def _s5_scan_vmem(hbuf_ref, hst_ref, ar_ref, ai_ref, s, nb, rows):
    ns = S5_SLAB_STATE
    hbuf_ref[s, 0:nb, :] = hst_ref[s]
    ar, ai = ar_ref[s], ai_ref[s]

    def body(i, carry):
        j = pl.multiple_of(nb + i * SUBLANES, SUBLANES)
        prev = hbuf_ref[s, pl.ds(j - nb, SUBLANES), :]
        cur = hbuf_ref[s, pl.ds(j, SUBLANES), :]
        pr, pi = prev[:, :ns], prev[:, ns:]
        hbuf_ref[s, pl.ds(j, SUBLANES), :ns] = ar * pr - ai * pi + cur[:, :ns]
        hbuf_ref[s, pl.ds(j, SUBLANES), ns:] = ar * pi + ai * pr + cur[:, ns:]
        return carry

    lax.fori_loop(0, rows // SUBLANES, body, 0, unroll=min(8, nb // SUBLANES))
    hst_ref[s] = hbuf_ref[s, rows:rows + nb, :]


S5_N_IN = 10


def _s5_kernel(*refs, nb, tl, batch_major, n_cast):
    (u_ref, re0_ref, im0_ref, ar_ref, ai_ref, bblk_ref, cblk_ref, dsk_ref, wglu_ref,
     bglu_ref) = refs[:S5_N_IN]
    cast_in = refs[S5_N_IN:S5_N_IN + n_cast]
    y_ref, ren_ref, imn_ref = refs[S5_N_IN + n_cast:S5_N_IN + n_cast + 3]
    cast_out = refs[S5_N_IN + n_cast + 3:S5_N_IN + 2 * n_cast + 3]
    ut_ref, hbuf_ref, hst_ref, g_ref = refs[S5_N_IN + 2 * n_cast + 3:]
    for w_ref, w16_ref in zip(cast_in, cast_out):
        w16_ref[...] = w_ref[...].astype(bf16)
    step = pl.program_id(0)
    rows = nb * tl
    ns = S5_SLAB_STATE

    @pl.when(step == 0)
    def _():
        for s in range(S5_SLABS):
            hst_ref[s, :, :ns] = re0_ref[:, s * ns:(s + 1) * ns]
            hst_ref[s, :, ns:] = im0_ref[:, s * ns:(s + 1) * ns]

    for s in range(S5_SLABS):
        sl = slice(s * LANES, (s + 1) * LANES)
        if batch_major:
            for b in range(nb):
                ut_ref[s, pl.ds(b, tl, stride=nb), :] = u_ref[b, :, sl]
        else:
            ut_ref[s] = u_ref[:, :, sl].reshape(rows, LANES)

    if nb == SUBLANES:
        _s5_slabs_interleaved(ut_ref, hbuf_ref, hst_ref, g_ref, ar_ref, ai_ref, bblk_ref, cblk_ref, dsk_ref, tl)
    else:
        for s in range(S5_SLABS):
            sl = slice(s * LANES, (s + 1) * LANES)
            hbuf_ref[s, nb:nb + rows, :] = jnp.dot(ut_ref[s].astype(bf16), bblk_ref[s],
                                                   preferred_element_type=f32)
            _s5_scan_vmem(hbuf_ref, hst_ref, ar_ref, ai_ref, s, nb, rows)
            ys = jnp.dot(hbuf_ref[s, nb:nb + rows, :].astype(bf16), cblk_ref[s], preferred_element_type=f32)
            g_ref[s] = _gelu(ys + dsk_ref[:, sl] * ut_ref[s])

    g = jnp.concatenate([g_ref[s] for s in range(S5_SLABS)], axis=-1)
    gate = jnp.dot(g.astype(bf16), wglu_ref[...], preferred_element_type=f32) + bglu_ref[...]
    out = g * jax.nn.sigmoid(gate)
    if batch_major:
        for s in range(S5_SLABS):
            sl = slice(s * LANES, (s + 1) * LANES)
            ut_ref[s] = out[:, sl]
            for b in range(nb):
                y_ref[b, :, sl] = ut_ref[s, pl.ds(b, tl, stride=nb), :]
    else:
        y_ref[...] = out.reshape(tl, nb, D_S5)

    @pl.when(step == pl.num_programs(0) - 1)
    def _():
        for s in range(S5_SLABS):
            ren_ref[:, s * ns:(s + 1) * ns] = hst_ref[s, :, :ns]
            imn_ref[:, s * ns:(s + 1) * ns] = hst_ref[s, :, ns:]


def _s5(u, re0, im0, ar, ai, bblk, cblk, dsk, wglu, bglu, *, tl, batch_major, cast=()):
    if batch_major:
        nb, L, _ = u.shape
        u_spec = pl.BlockSpec((nb, tl, D_S5), lambda i: (0, i, 0))
    else:
        L, nb, _ = u.shape
        u_spec = pl.BlockSpec((tl, nb, D_S5), lambda i: (i, 0, 0))
    rows = nb * tl
    steps = L // tl
    nstate = S5_GROUPS * S5_STATE
    st_spec = _const_spec((nb, nstate))
    cast_specs = [pl.BlockSpec((w.shape[0] // steps, w.shape[1]), lambda i: (i, 0)) for w in cast]
    assert all(w.shape[0] % (steps * 2 * SUBLANES) == 0 for w in cast)
    return pl.pallas_call(
        functools.partial(_s5_kernel, nb=nb, tl=tl, batch_major=batch_major, n_cast=len(cast)),
        grid=(steps,),
        in_specs=[u_spec, st_spec, st_spec,
                  _const_spec((S5_SLABS, SUBLANES, S5_SLAB_STATE)),
                  _const_spec((S5_SLABS, SUBLANES, S5_SLAB_STATE)),
                  _const_spec((S5_SLABS, LANES, 2 * S5_SLAB_STATE)),
                  _const_spec((S5_SLABS, 2 * S5_SLAB_STATE, LANES)),
                  _const_spec((1, D_S5)), _const_spec((D_S5, D_S5)), _const_spec((1, D_S5))] + cast_specs,
        out_specs=[u_spec, st_spec, st_spec] + cast_specs,
        out_shape=[jax.ShapeDtypeStruct(u.shape, f32),
                   jax.ShapeDtypeStruct((nb, nstate), f32),
                   jax.ShapeDtypeStruct((nb, nstate), f32)]
                  + [jax.ShapeDtypeStruct(w.shape, bf16) for w in cast],
        scratch_shapes=[pltpu.VMEM((S5_SLABS, rows, LANES), f32),
                        pltpu.VMEM((S5_SLABS, nb + rows, 2 * S5_SLAB_STATE), f32),
                        pltpu.VMEM((S5_SLABS, nb, 2 * S5_SLAB_STATE), f32),
                        pltpu.VMEM((S5_SLABS, rows, LANES), f32)],
        compiler_params=_params("arbitrary"),
        name="s5_mixer",
    )(u, re0, im0, ar, ai, bblk, cblk, dsk, wglu, bglu, *cast)


def _ffn_stages(x, ys, y5, mods, n2g, nfg, wo_ref, wg_ref, wu_ref, wd_ref, write_out, ff_chunk):
    g1, sh2, sc2, g2, shf, scf = mods
    att = jnp.dot(ys.astype(bf16), wo_ref[:D_SSD, :], preferred_element_type=f32)
    att = att + jnp.dot(y5.astype(bf16), wo_ref[D_SSD:, :], preferred_element_type=f32)
    yield
    x1 = x + _by_row(lambda v, s: v * s, att, g1)
    v = _rms_mod(x1, n2g, sc2, sh2).astype(bf16)
    ff = None
    for o in range(0, D_FF, ff_chunk):
        gate = jnp.dot(v, wg_ref[:, o:o + ff_chunk], preferred_element_type=f32)
        up = jnp.dot(v, wu_ref[:, o:o + ff_chunk], preferred_element_type=f32)
        hid = (_silu(gate) * up).astype(bf16)
        part = jnp.dot(hid, wd_ref[o:o + ff_chunk, :], preferred_element_type=f32)
        ff = part if ff is None else ff + part
        yield
    x2 = x1 + _by_row(lambda v, s: v * s, ff, g2)
    write_out(_rms_mod(x2, nfg, scf, shf))


def _ffn_kernel(x_ref, ys_ref, y5_ref, mod_ref, modf_ref, n2g_ref, nfg_ref, wo_ref, wg_ref, wu_ref, wd_ref,
                o_ref, *, seq_major, ff_chunk, sub):
    mods = (_mod_row(mod_ref, 2, seq_major), _mod_row(mod_ref, 3, seq_major), _mod_row(mod_ref, 4, seq_major),
            _mod_row(mod_ref, 5, seq_major), _mod_row(modf_ref, 0, seq_major), _mod_row(modf_ref, 1, seq_major))
    x = _rows_in(x_ref, seq_major)
    rows = x.shape[0]
    tiles = {}

    def stages(r0):
        def write_out(y):
            if seq_major:
                tiles[r0] = y
            else:
                o_ref[0, r0:r0 + sub, :] = y
        return _ffn_stages(x[r0:r0 + sub], ys_ref[0, r0:r0 + sub, :], y5_ref[0, r0:r0 + sub, :], mods,
                           n2g_ref[...], nfg_ref[...], wo_ref, wg_ref, wu_ref, wd_ref, write_out, ff_chunk)

    _interleave(*[stages(r0) for r0 in range(0, rows, sub)])
    if seq_major:
        s, t, d = o_ref.shape
        y = jnp.concatenate([tiles[r0] for r0 in range(0, rows, sub)], axis=0)
        o_ref[...] = jnp.swapaxes(y.reshape(t, s, d), 0, 1)


def _ffn(x, ys, y5, mod, modf, n2g, nfg, wo, wg, wu, wd, *, tm, sub, seq_major, n_seq, first_seq, ff_chunk=256):
    if seq_major:
        x_spec = _const_spec(x.shape)
        nb, rows, d = 1, x.shape[0] * x.shape[1], x.shape[2]
        assert tm == rows
    else:
        nb, rows, d = x.shape
        x_spec = pl.BlockSpec((1, tm, d), lambda i, j: (i, j, 0))
    nt = rows // tm
    mod_spec = _mod_spec(mod, n_seq, first_seq)
    modf_spec = _mod_spec(modf, n_seq, first_seq)
    blk = lambda wd_: pl.BlockSpec((1, tm, wd_), lambda i, j: (i, j, 0))
    single = dict(pipeline_mode=pl.Buffered(1))
    wspec = lambda shape: pl.BlockSpec(shape, lambda i, j: (0, 0), **single)
    return pl.pallas_call(
        functools.partial(_ffn_kernel, seq_major=seq_major, ff_chunk=ff_chunk, sub=sub),
        grid=(nb, nt),
        in_specs=[x_spec, blk(D_SSD), blk(D_S5), mod_spec, modf_spec,
                  _const_spec((1, d)), _const_spec((1, d)),
                  wspec((d, d)), wspec((d, D_FF)), wspec((d, D_FF)), wspec((D_FF, d))],
        out_specs=x_spec,
        out_shape=jax.ShapeDtypeStruct(x.shape, f32),
        compiler_params=_params("parallel", "parallel"),
        name="out_ffn",
    )(x, ys, y5, mod, modf, n2g, nfg, wo, wg, wu, wd)


def kernel(x_prompt, x_sample, c_prompt, c_sample, state_ssd, state_conv, state_s5_re, state_s5_im, w_ada, b_ada, norm1_g, w_in, conv_w, conv_b, ssd_dt_bias, ssd_A_log, ssd_D, ssd_norm_g, s5_A_re, s5_A_im, s5_log_step, s5_B_re, s5_B_im, s5_C_re, s5_C_im, s5_D, w_glu, b_glu, w_out, norm2_g, w_ffn_gate, w_ffn_up, w_ffn_down, w_ada_f, b_ada_f, normf_g):
    assert w_ada.shape[0] == 1, "single-layer stack"
    bp, seq, d = x_prompt.shape
    bs, dseq, _ = x_sample.shape

    c_all = jnp.concatenate([c_sample, c_prompt], axis=0)
    mod, modf = _ada_mod(c_all, (w_ada[0], w_ada_f), (b_ada[0], b_ada_f))
    seqs_s = dict(n_seq=bs, first_seq=0)
    seqs_p = dict(n_seq=bp, first_seq=bs)

    w_in_p = w_in[0].T
    pad_h = lambda v: jnp.concatenate([v, jnp.zeros((DT_PAD - HEADS,), f32)]).reshape(1, DT_PAD)
    dtb = pad_h(ssd_dt_bias[0])
    alog = pad_h(ssd_A_log[0])
    dexp = jnp.repeat(ssd_D[0], HEAD_DIM).reshape(1, D_SSD)
    ng = ssd_norm_g[0].reshape(1, D_SSD)
    cw, cb = conv_w[0], conv_b[0].reshape(1, CONV_DIM)

    ar, ai, bblk, cblk = _s5_params(s5_A_re[0], s5_A_im[0], s5_log_step[0],
                                    s5_B_re[0].transpose(0, 2, 1), s5_B_im[0].transpose(0, 2, 1),
                                    s5_C_re[0], s5_C_im[0])
    dsk = s5_D[0].reshape(1, D_S5)
    wglu = w_glu[0].astype(bf16)
    bglu = b_glu[0].reshape(1, D_S5)
    n1g, n2g, nfg = norm1_g[0].reshape(1, d), norm2_g[0].reshape(1, d), normf_g.reshape(1, d)
    nstate = S5_GROUPS * S5_STATE

    z, act, u5, dt, conv_p = _inproj(x_prompt, mod, n1g, w_in_p, (cw, cb), tm=1024, sub=512, seq_major=False,
                                     **seqs_p)
    y_ssd, ssd_p = _ssd_prompt(act, z, dt, dtb, alog, dexp, ng)
    zeros_st = jnp.zeros((bp, nstate), f32)
    y_s5, re_p, im_p, wo, wg, wu, wd = _s5(
        u5, zeros_st, zeros_st, ar, ai, bblk, cblk, dsk, wglu, bglu, tl=128, batch_major=True,
        cast=(w_out[0], w_ffn_gate[0], w_ffn_up[0], w_ffn_down[0]))
    y_prompt = _ffn(x_prompt, y_ssd, y_s5, mod, modf, n2g, nfg, wo, wg, wu, wd, tm=1024, sub=512,
                    seq_major=False, **seqs_p)

    rows_s = dseq * bs
    steps = lambda a: a.reshape(dseq, bs, a.shape[-1])
    flat = lambda a: a.reshape(1, rows_s, a.shape[-1])
    z, xbc, u5, dt = _inproj(x_sample, mod, n1g, w_in_p, tm=rows_s, sub=rows_s // 2, seq_major=True, **seqs_s)
    y_ssd, ssd_s, conv_s = _ssd_sample(steps(xbc), steps(z), steps(dt), state_conv[0].transpose(1, 0, 2),
                                       state_ssd[0], cw, cb, dtb, alog, dexp, ng)
    y_s5, re_s, im_s = _s5(steps(u5), state_s5_re[0].reshape(bs, nstate), state_s5_im[0].reshape(bs, nstate),
                           ar, ai, bblk, cblk, dsk, wglu, bglu, tl=dseq, batch_major=False)
    y_sample = _ffn(x_sample, flat(y_ssd), flat(y_s5), mod, modf, n2g, nfg, wo, wg, wu, wd,
                    tm=rows_s, sub=rows_s, seq_major=True, **seqs_s)

    g5 = (S5_GROUPS, S5_STATE)
    return (y_prompt, y_sample,
            ssd_p[None], ssd_s.reshape((1,) + state_ssd.shape[1:]),
            conv_p[None], conv_s.transpose(1, 0, 2)[None],
            re_p.reshape((1, bp) + g5), re_s.reshape((1, bs) + g5),
            im_p.reshape((1, bp) + g5), im_s.reshape((1, bs) + g5))
```

```python
import functools

import jax
import jax.numpy as jnp
from jax import lax
from jax.experimental import pallas as pl
from jax.experimental.pallas import tpu as pltpu

f32 = jnp.float32
bf16 = jnp.bfloat16

D_MODEL = 1024
D_SSD = 512
HEAD_DIM = 64
HEADS = 8
GROUPS = 2
HEADS_PER_GROUP = HEADS // GROUPS
STATE = 128
CONV_W = 4
CONV_DIM = D_SSD + 2 * GROUPS * STATE
D_S5 = 512
S5_CH = 16
S5_GROUPS = 32
S5_STATE = 64
D_FF = 2816
N_ADA = 6
EPS = 1e-6

LANES = 128
SUBLANES = 8
SSD_CHUNK = 128
S5_SLABS = D_S5 // LANES
S5_SLAB_STATE = (S5_GROUPS // S5_SLABS) * S5_STATE
DT_PAD = LANES
VMEM_LIMIT = 56 * 1024 * 1024

NT_DIMS = (((1,), (1,)), ((), ()))
TN_DIMS = (((0,), (0,)), ((), ()))


def _silu(x):
    return x * jax.nn.sigmoid(x)


def _interleave(*gens, lead=0):
    pending, live, rnd = list(gens), [], 0
    while pending or live:
        while pending and rnd >= lead * (len(gens) - len(pending)):
            live.append(pending.pop(0))
        for g in list(live):
            try:
                next(g)
            except StopIteration:
                live.remove(g)
        rnd += 1


def _by_row(fn, v, *ms):
    r, d = v.shape
    m_rows = ms[0].shape[0]
    if m_rows in (1, r):
        return fn(v, *ms)
    out = fn(v.reshape(r // m_rows, m_rows, d), *[m[None] for m in ms])
    return out.reshape(r, d)


def _rms_mod(x, g, sc, sh):
    y = x * lax.rsqrt(jnp.mean(x * x, axis=-1, keepdims=True) + EPS)
    return _by_row(lambda v, s, t: v * (1.0 + s) + t, y * g, sc, sh)


def _mod_row(mod_ref, i, per_row):
    return mod_ref[i] if per_row else mod_ref[i, pl.ds(pl.program_id(0), 1), :]


def _mod_spec(mod, n_seq, first_seq):
    assert first_seq % n_seq == 0
    return pl.BlockSpec((mod.shape[0], n_seq, mod.shape[2]), lambda *_: (0, first_seq // n_seq, 0))


def _const_spec(shape):
    nd = len(shape)
    return pl.BlockSpec(shape, lambda *_: (0,) * nd)


def _params(*sem):
    return pltpu.CompilerParams(dimension_semantics=sem, vmem_limit_bytes=VMEM_LIMIT)


ADA_K_BLOCK = 128


def _ada_kernel(c_ref, *refs):
    nw = len(refs) // 3
    s = _silu(c_ref[...]).astype(bf16)
    for w_ref, b_ref, o_ref in zip(refs[:nw], refs[nw:2 * nw], refs[2 * nw:]):
        n, _, d = o_ref.shape

        @pl.when(pl.program_id(0) == 0)
        def _():
            for j in range(n):
                o_ref[j] = jnp.broadcast_to(b_ref[:, j * d:(j + 1) * d], o_ref.shape[1:])

        for j in range(n):
            o_ref[j] += jnp.dot(s, w_ref[:, j * d:(j + 1) * d].astype(bf16), preferred_element_type=f32)


def _ada_mod(c, ws, bs):
    m, k = c.shape
    ns = [w.shape[1] // k for w in ws]
    return pl.pallas_call(
        _ada_kernel,
        grid=(k // ADA_K_BLOCK,),
        in_specs=[pl.BlockSpec((m, ADA_K_BLOCK), lambda i: (0, i))]
                 + [pl.BlockSpec((ADA_K_BLOCK, n * k), lambda i: (i, 0)) for n in ns]
                 + [_const_spec((1, n * k)) for n in ns],
        out_specs=[_const_spec((n, m, k)) for n in ns],
        out_shape=[jax.ShapeDtypeStruct((n, m, k), f32) for n in ns],
        compiler_params=_params("arbitrary"),
        name="ada_mod",
    )(c, *ws, *[b.reshape(1, n * k) for b, n in zip(bs, ns)])


def _rows_in(x_ref, seq_major):
    if not seq_major:
        return x_ref[0]
    s, t, d = x_ref.shape
    return jnp.swapaxes(x_ref[...], 0, 1).reshape(s * t, d)


def _conv_silu(ext, cw_ref, cb_ref):
    w = [cw_ref[k:k + 1, :] for k in range(CONV_W)]
    back1 = pltpu.roll(ext, 1, 0)
    older = pltpu.roll(ext * w[1] + back1 * w[0], 2, 0)
    conv = cb_ref[...] + ext[SUBLANES:, :] * w[3] + back1[SUBLANES:, :] * w[2] + older[SUBLANES:, :]
    return _silu(conv)


def _inproj_stages(x, sh, sc, g, outs, after=()):
    u = _rms_mod(x, g, sc, sh).astype(bf16)
    yield
    for write, w16 in outs:
        write(lax.dot_general(u, w16, NT_DIMS, preferred_element_type=f32))
        yield
    for stage in after:
        stage()
        yield


def _inproj_kernel(x_ref, mod_ref, g_ref, w_ref, *rest, seq_major, sub, conv):
    if conv:
        cw_ref, cb_ref, z_ref, act_ref, u5_ref, dt_ref, cn_ref, ext_ref = rest

        @pl.when(pl.program_id(1) == 0)
        def _():
            ext_ref[0:SUBLANES, :] = jnp.zeros((SUBLANES, CONV_DIM), f32)
    else:
        z_ref, xbc_ref, u5_ref, dt_ref = rest
    sh = _mod_row(mod_ref, 0, seq_major)
    sc = _mod_row(mod_ref, 1, seq_major)
    x = _rows_in(x_ref, seq_major)
    rows = x.shape[0]
    T = SSD_CHUNK
    o_dt = D_SSD + CONV_DIM
    w16 = lambda o, width: w_ref[o:o + width, :].astype(bf16)
    w_z, w_xbc, w_u5, w_dt = w16(0, D_SSD), w16(D_SSD, CONV_DIM), w16(o_dt + HEADS, D_S5), w16(o_dt, DT_PAD)

    def stages(r0):
        def to(ref):
            def write(v):
                ref[0, r0:r0 + sub, :] = v
            return write

        def to_ext(v):
            ext_ref[SUBLANES + r0:SUBLANES + r0 + sub, :] = v

        def conv_chunk(c0):
            def stage():
                act_ref[0, c0:c0 + T, :] = _conv_silu(ext_ref[c0:c0 + SUBLANES + T, :], cw_ref, cb_ref)
            return stage

        outs = [(to(z_ref), w_z), (to_ext if conv else to(xbc_ref), w_xbc), (to(u5_ref), w_u5), (to(dt_ref), w_dt)]
        after = [conv_chunk(c0) for c0 in range(r0, r0 + sub, T)] if conv else ()
        return _inproj_stages(x[r0:r0 + sub], sh, sc, g_ref[...], outs, after)

    n_matmul_stages = 5
    _interleave(*[stages(r0) for r0 in range(0, rows, sub)], lead=n_matmul_stages if conv else 0)
    if conv:
        cn_ref[0] = ext_ref[SUBLANES + rows - (CONV_W - 1):SUBLANES + rows, :]
        ext_ref[0:SUBLANES, :] = ext_ref[rows:rows + SUBLANES, :]


def _inproj(x, mod, g, w, conv_wb=None, *, tm, sub, seq_major, n_seq, first_seq):
    if seq_major:
        x_spec = _const_spec(x.shape)
        nb, rows, d = 1, x.shape[0] * x.shape[1], x.shape[2]
        assert tm == rows
    else:
        nb, rows, d = x.shape
        x_spec = pl.BlockSpec((1, tm, d), lambda i, j: (i, j, 0))
    nt = rows // tm
    widths = (D_SSD, CONV_DIM, D_S5, DT_PAD)
    conv = conv_wb is not None
    in_specs = [x_spec, _mod_spec(mod, n_seq, first_seq), _const_spec((1, d)),
                pl.BlockSpec(w.shape, lambda i, j: (0, 0), pipeline_mode=pl.Buffered(1))]
    out_specs = [pl.BlockSpec((1, tm, wd), lambda i, j: (i, j, 0)) for wd in widths]
    out_shape = [jax.ShapeDtypeStruct((nb, rows, wd), f32) for wd in widths]
    scratch = []
    if conv:
        in_specs += [_const_spec((CONV_W, CONV_DIM)), _const_spec((1, CONV_DIM))]
        out_specs.append(pl.BlockSpec((1, CONV_W - 1, CONV_DIM), lambda i, j: (i, 0, 0)))
        out_shape.append(jax.ShapeDtypeStruct((nb, CONV_W - 1, CONV_DIM), f32))
        scratch.append(pltpu.VMEM((SUBLANES + tm, CONV_DIM), f32))
    return pl.pallas_call(
        functools.partial(_inproj_kernel, seq_major=seq_major, sub=sub, conv=conv),
        grid=(nb, nt),
        in_specs=in_specs, out_specs=out_specs, out_shape=out_shape, scratch_shapes=scratch,
        compiler_params=_params("parallel", "arbitrary" if conv else "parallel"),
        name="in_proj",
    )(x, mod, g, w, *(conv_wb or ()))


def _split3(x):
    hi = x.astype(bf16)
    r1 = x - hi.astype(f32)
    mid = r1.astype(bf16)
    lo = (r1 - mid.astype(f32)).astype(bf16)
    return hi, mid, lo


def _dot_sel_lhs(sel16, x):
    return sum(jnp.dot(sel16, p, preferred_element_type=f32) for p in _split3(x))


def _dot_sel_rhs(x, sel16):
    hi, mid, _ = _split3(x)
    return jnp.dot(hi, sel16, preferred_element_type=f32) + jnp.dot(mid, sel16, preferred_element_type=f32)


def _gated_group_norm(y, z, ng):
    y = y * _silu(z)
    gw = D_SSD // GROUPS
    parts = []
    for g in range(GROUPS):
        yg = y[:, g * gw:(g + 1) * gw]
        parts.append(yg * lax.rsqrt(jnp.mean(yg * yg, axis=-1, keepdims=True) + EPS))
    return jnp.concatenate(parts, axis=-1) * ng


def _ssd_prompt_kernel(act_ref, z_ref, dt_ref, dtb_ref, alog_ref, dexp_ref, ng_ref, y_ref, st_ref, h_ref, *, cps):
    T = SSD_CHUNK
    gw = HEADS_PER_GROUP * HEAD_DIM

    @pl.when(pl.program_id(1) == 0)
    def _():
        h_ref[...] = jnp.zeros_like(h_ref)

    row = lax.broadcasted_iota(jnp.int32, (T, T), 0)
    col = lax.broadcasted_iota(jnp.int32, (T, T), 1)
    tri = row >= col
    tri16 = jnp.where(tri, 1.0, 0.0).astype(bf16)
    low_half = lax.broadcasted_iota(jnp.int32, (T, LANES), 1) < HEAD_DIM
    a_neg = -jnp.exp(alog_ref[...])

    lane_head = lax.broadcasted_iota(jnp.int32, (DT_PAD, D_SSD), 1) // HEAD_DIM
    sel16 = jnp.where(lax.broadcasted_iota(jnp.int32, (DT_PAD, D_SSD), 0) == lane_head, 1.0, 0.0).astype(bf16)

    def spread(q):
        return _dot_sel_rhs(q, sel16)

    def state_free(ci):
        r0 = ci * T
        act = act_ref[0, r0:r0 + T, :]
        xs = act[:, :D_SSD]
        bm = act[:, D_SSD:D_SSD + GROUPS * STATE]
        cm = act[:, D_SSD + GROUPS * STATE:]
        dtv = jax.nn.softplus(dt_ref[0, r0:r0 + T, :] + dtb_ref[...])
        a_cs = _dot_sel_lhs(tri16, dtv * a_neg)
        bg16s = [bm[:, g * STATE:(g + 1) * STATE].astype(bf16) for g in range(GROUPS)]
        cg16s = [cm[:, g * STATE:(g + 1) * STATE].astype(bf16) for g in range(GROUPS)]
        gmats = [lax.dot_general(cg16s[g], bg16s[g], NT_DIMS, preferred_element_type=f32)
                 for g in range(GROUPS)]
        yield
        a_last = a_cs[T - 1:T, :]
        a_cs_t = a_cs.T
        rep = spread(jnp.concatenate([dtv, jnp.exp(a_last - a_cs), jnp.exp(a_cs)], axis=0))
        yield
        x = xs * rep[:T]
        x16 = x.astype(bf16)
        xd16 = (x * rep[T:2 * T]).astype(bf16)
        e_cs = rep[2 * T:]
        yield
        y_part = []
        for g in range(GROUPS):
            for pr in range(HEADS_PER_GROUP // 2):
                s_pair = []
                for q in range(2):
                    h = g * HEADS_PER_GROUP + 2 * pr + q
                    seg = a_cs[:, h:h + 1] - a_cs_t[h:h + 1, :]
                    lmat = jnp.exp(jnp.where(tri, seg, -jnp.inf))
                    s_pair.append((gmats[g] * lmat).astype(bf16))
                lanes = slice(g * gw + pr * LANES, g * gw + (pr + 1) * LANES)
                both = jnp.dot(jnp.concatenate(s_pair, axis=0), x16[:, lanes], preferred_element_type=f32)
                yield
                y_part.append(jnp.where(low_half, both[:T], both[T:]))
        y_free = jnp.concatenate(y_part, axis=-1) + xs * dexp_ref[...]
        return y_free, e_cs, jnp.exp(a_last), xd16, bg16s, cg16s

    def state_step(ci, y_free, e_cs, e_last, xd16, bg16s, cg16s):
        r0 = ci * T
        y_offs = []
        for g in range(GROUPS):
            hp = h_ref[g]
            y_offs.append(lax.dot_general(cg16s[g], hp.astype(bf16), NT_DIMS, preferred_element_type=f32))
            upd = lax.dot_general(xd16[:, g * gw:(g + 1) * gw], bg16s[g], TN_DIMS, preferred_element_type=f32)
            for hh in range(HEADS_PER_GROUP):
                h = g * HEADS_PER_GROUP + hh
                rows = slice(hh * HEAD_DIM, (hh + 1) * HEAD_DIM)
                h_ref[g, rows, :] = e_last[:, h:h + 1] * hp[rows] + upd[rows]
        y = y_free + jnp.concatenate(y_offs, axis=-1) * e_cs
        y_ref[0, r0:r0 + T, :] = _gated_group_norm(y, z_ref[0, r0:r0 + T, :], ng_ref[...])

    gens = [state_free(ci) for ci in range(cps)]
    free = [None] * cps
    while any(f is None for f in free):
        for ci in range(cps):
            if free[ci] is None:
                try:
                    next(gens[ci])
                except StopIteration as done:
                    free[ci] = done.value
    for ci in range(cps):
        state_step(ci, *free[ci])

    @pl.when(pl.program_id(1) == pl.num_programs(1) - 1)
    def _():
        st_ref[0] = h_ref[...].reshape(HEADS, HEAD_DIM, STATE)


def _ssd_prompt(act, z, dt, dtb, alog, dexp, ng, *, cps=8):
    nb, L, _ = act.shape
    rows = cps * SSD_CHUNK
    blk = lambda wd: pl.BlockSpec((1, rows, wd), lambda b, c: (b, c, 0))
    return pl.pallas_call(
        functools.partial(_ssd_prompt_kernel, cps=cps),
        grid=(nb, L // rows),
        in_specs=[blk(CONV_DIM), blk(D_SSD), blk(DT_PAD), _const_spec((1, DT_PAD)), _const_spec((1, DT_PAD)),
                  _const_spec((1, D_SSD)), _const_spec((1, D_SSD))],
        out_specs=[blk(D_SSD), pl.BlockSpec((1, HEADS, HEAD_DIM, STATE), lambda b, c: (b, 0, 0, 0))],
        out_shape=[jax.ShapeDtypeStruct((nb, L, D_SSD), f32),
                   jax.ShapeDtypeStruct((nb, HEADS, HEAD_DIM, STATE), f32)],
        scratch_shapes=[pltpu.VMEM((GROUPS, HEADS_PER_GROUP * HEAD_DIM, STATE), f32)],
        compiler_params=_params("parallel", "arbitrary"),
        name="ssd_prompt",
    )(act, z, dt, dtb, alog, dexp, ng)


def _ssd_sample_kernel(xbc_ref, z_ref, dt_ref, cbuf_ref, st_ref, cw_ref, cb_ref, dtb_ref, alog_ref,
                       dexp_ref, ng_ref, y_ref, stn_ref, cn_ref, dtot_ref, *, L, nb):
    gw = HEADS_PER_GROUP * HEAD_DIM
    full = [cbuf_ref[j] for j in range(CONV_W - 1)] + [xbc_ref[t] for t in range(L)]
    for j in range(CONV_W - 1):
        cn_ref[j] = full[L + j]
    a_neg = -jnp.exp(alog_ref[...])
    xs, bm, cm, dtv, acs = [], [], [], [], []
    run = None
    for t in range(L):
        conv = cb_ref[...]
        for k in range(CONV_W):
            conv = conv + full[t + k] * cw_ref[k:k + 1, :]
        act = _silu(conv)
        xs.append(act[:, :D_SSD])
        bm.append(act[:, D_SSD:D_SSD + GROUPS * STATE])
        cm.append(act[:, D_SSD + GROUPS * STATE:])
        d = jax.nn.softplus(dt_ref[t] + dtb_ref[...])
        dtv.append(d)
        run = d * a_neg if run is None else run + d * a_neg
        acs.append(run)
    a_tot = acs[L - 1]
    dtot_ref[...] = jnp.exp(a_tot)

    lane_head = lax.broadcasted_iota(jnp.int32, (DT_PAD, D_SSD), 1) // HEAD_DIM
    sel16 = jnp.where(lax.broadcasted_iota(jnp.int32, (DT_PAD, D_SSD), 0) == lane_head, 1.0, 0.0).astype(bf16)
    pairs = [(t, s) for t in range(L) for s in range(t)]
    factors = (dtv + [jnp.exp(a_tot - acs[t]) for t in range(L)] + [jnp.exp(acs[t]) for t in range(L)]
               + [jnp.exp(acs[t] - acs[s]) for t, s in pairs])
    rep = _dot_sel_rhs(jnp.concatenate(factors, axis=0), sel16)
    piece = lambda i: rep[i * nb:(i + 1) * nb]
    x = [xs[t] * piece(t) for t in range(L)]
    xd_stack = jnp.concatenate([x[t] * piece(L + t) for t in range(L)], axis=0)
    e_cs = [piece(2 * L + t) for t in range(L)]
    decay = {ts: piece(3 * L + i) for i, ts in enumerate(pairs)}

    in_group0 = lax.broadcasted_iota(jnp.int32, (nb, D_SSD), 1) < gw
    y_intra = []
    for t in range(L):
        acc = None
        for s in range(t + 1):
            cb_dot = [jnp.sum(cm[t][:, g * STATE:(g + 1) * STATE] * bm[s][:, g * STATE:(g + 1) * STATE],
                              axis=-1, keepdims=True) for g in range(GROUPS)]
            w = jnp.where(in_group0, cb_dot[0], cb_dot[1])
            term = w * x[s] if s == t else w * decay[(t, s)] * x[s]
            acc = term if acc is None else acc + term
        y_intra.append(acc)

    c_stack = [jnp.concatenate([cm[t][:, g * STATE:(g + 1) * STATE] for t in range(L)], axis=0).astype(bf16)
               for g in range(GROUPS)]
    b_stack = [jnp.concatenate([bm[t][:, g * STATE:(g + 1) * STATE] for t in range(L)], axis=0).astype(bf16)
               for g in range(GROUPS)]
    seq_of_row = lax.broadcasted_iota(jnp.int32, (L * nb, gw), 0) & (nb - 1)

    def per_seq(b, yoff):
        mine = seq_of_row == b
        drow = dtot_ref[pl.ds(b, 1), :]
        out = []
        for g in range(GROUPS):
            h0 = st_ref[b, g]
            r = lax.dot_general(c_stack[g], h0.astype(bf16), NT_DIMS, preferred_element_type=f32)
            out.append(yoff[g] + jnp.where(mine, r, 0.0))
            xm = jnp.where(mine, xd_stack[:, g * gw:(g + 1) * gw], 0.0).astype(bf16)
            upd = lax.dot_general(xm, b_stack[g], TN_DIMS, preferred_element_type=f32)
            for hh in range(HEADS_PER_GROUP):
                h = g * HEADS_PER_GROUP + hh
                rows = slice(hh * HEAD_DIM, (hh + 1) * HEAD_DIM)
                stn_ref[b, g, rows, :] = drow[:, h:h + 1] * h0[rows] + upd[rows]
        return tuple(out)

    yoff = lax.fori_loop(0, nb, per_seq, tuple(jnp.zeros((L * nb, gw), f32) for _ in range(GROUPS)),
                         unroll=8)

    for t in range(L):
        y_off = jnp.concatenate([yoff[g][t * nb:(t + 1) * nb] for g in range(GROUPS)], axis=-1)
        y = y_intra[t] + y_off * e_cs[t] + xs[t] * dexp_ref[...]
        y_ref[t] = _gated_group_norm(y, z_ref[t], ng_ref[...])


def _ssd_sample(xbc, z, dt, cbuf, st, cw, cb, dtb, alog, dexp, ng, *, nb=16):
    L, B, _ = xbc.shape
    assert nb & (nb - 1) == 0 and B % nb == 0
    tblk = lambda n, wd: pl.BlockSpec((n, nb, wd), lambda i: (0, i, 0))
    gw = HEADS_PER_GROUP * HEAD_DIM
    st = st.reshape(B, GROUPS, gw, STATE)
    st_spec = pl.BlockSpec((nb, GROUPS, gw, STATE), lambda i: (i, 0, 0, 0))
    return pl.pallas_call(
        functools.partial(_ssd_sample_kernel, L=L, nb=nb),
        grid=(B // nb,),
        in_specs=[tblk(L, CONV_DIM), tblk(L, D_SSD), tblk(L, DT_PAD), tblk(CONV_W - 1, CONV_DIM), st_spec,
                  _const_spec((CONV_W, CONV_DIM)), _const_spec((1, CONV_DIM)),
                  _const_spec((1, DT_PAD)), _const_spec((1, DT_PAD)),
                  _const_spec((1, D_SSD)), _const_spec((1, D_SSD))],
        out_specs=[tblk(L, D_SSD), st_spec, tblk(CONV_W - 1, CONV_DIM)],
        out_shape=[jax.ShapeDtypeStruct((L, B, D_SSD), f32),
                   jax.ShapeDtypeStruct((B, GROUPS, gw, STATE), f32),
                   jax.ShapeDtypeStruct((CONV_W - 1, B, CONV_DIM), f32)],
        scratch_shapes=[pltpu.VMEM((nb, DT_PAD), f32)],
        compiler_params=_params("parallel"),
        name="ssd_sample",
    )(xbc, z, dt, cbuf, st, cw, cb, dtb, alog, dexp, ng)


def _s5_param_kernel(lr_ref, li_ref, ls_ref, br_ref, bi_ref, cr_ref, ci_ref,
                     ar_ref, ai_ref, bblk_ref, cblk_ref, b_scr, c_scr):
    lr, li = lr_ref[...], li_ref[...]
    step = jnp.exp(ls_ref[...])
    mag = jnp.exp(lr * step)
    abr = mag * jnp.cos(li * step)
    abi = mag * jnp.sin(li * step)
    nr, ni = abr - 1.0, abi
    den = lr * lr + li * li
    fr = (nr * lr + ni * li) / den
    fi = (ni * lr - nr * li) / den
    br, bi = br_ref[...], bi_ref[...]
    bbr = fr * br - fi * bi
    bbi = fr * bi + fi * br
    b_scr[...] = jnp.zeros_like(b_scr)
    c_scr[...] = jnp.zeros_like(c_scr)
    gps = S5_GROUPS // S5_SLABS
    ns = S5_SLAB_STATE
    for g in range(S5_GROUPS):
        s, gl = divmod(g, gps)
        ch = slice(gl * S5_CH, (gl + 1) * S5_CH)
        st = slice(gl * S5_STATE, (gl + 1) * S5_STATE)
        st_im = slice(ns + gl * S5_STATE, ns + (gl + 1) * S5_STATE)
        ar_ref[s, :, st] = jnp.broadcast_to(abr[g], (SUBLANES, S5_STATE))
        ai_ref[s, :, st] = jnp.broadcast_to(abi[g], (SUBLANES, S5_STATE))
        b_scr[s, ch, st] = bbr[g]
        b_scr[s, ch, st_im] = bbi[g]
        c_scr[s, st, ch] = cr_ref[g].T
        c_scr[s, st_im, ch] = -ci_ref[g].T
    bblk_ref[...] = b_scr[...].astype(bf16)
    cblk_ref[...] = c_scr[...].astype(bf16)


def _s5_params(a_re, a_im, log_step, b_re_t, b_im_t, c_re, c_im):
    g, p = a_re.shape
    ns = S5_SLAB_STATE
    return pl.pallas_call(
        _s5_param_kernel,
        out_shape=[jax.ShapeDtypeStruct((S5_SLABS, SUBLANES, ns), f32),
                   jax.ShapeDtypeStruct((S5_SLABS, SUBLANES, ns), f32),
                   jax.ShapeDtypeStruct((S5_SLABS, LANES, 2 * ns), bf16),
                   jax.ShapeDtypeStruct((S5_SLABS, 2 * ns, LANES), bf16)],
        scratch_shapes=[pltpu.VMEM((S5_SLABS, LANES, 2 * ns), f32),
                        pltpu.VMEM((S5_SLABS, 2 * ns, LANES), f32)],
        name="s5_params",
    )(a_re.reshape(g, 1, p), a_im.reshape(g, 1, p), log_step.reshape(g, 1, 1), b_re_t, b_im_t, c_re, c_im)


def _gelu(x):
    return 0.5 * x * (1.0 + lax.erf(x * (2.0 ** -0.5)))


def _s5_slabs_interleaved(ut_ref, hbuf_ref, hst_ref, g_ref, ar_ref, ai_ref, bblk_ref, cblk_ref, dsk_ref, tl,
                          row_block=256, n_stages=8):
    ns = S5_SLAB_STATE
    nb = SUBLANES
    rows = nb * tl

    def lanes(s):
        return slice(s * LANES, (s + 1) * LANES)

    def bu_stage(slabs):
        for s in slabs:
            for r in range(0, rows, row_block):
                us = ut_ref[s, r:r + row_block, :]
                hbuf_ref[s, nb + r:nb + r + row_block, :] = jnp.dot(us.astype(bf16), bblk_ref[s],
                                                                    preferred_element_type=f32)
                yield

    def scan_stage(slabs):
        carry = [(hst_ref[s, :, :ns], hst_ref[s, :, ns:]) for s in slabs]
        coef = [(ar_ref[s], ai_ref[s]) for s in slabs]
        for t in range(tl):
            r = nb + nb * t
            for i, s in enumerate(slabs):
                (pr, pi), (ar, ai) = carry[i], coef[i]
                nr = ar * pr - ai * pi + hbuf_ref[s, r:r + nb, :ns]
                ni = ar * pi + ai * pr + hbuf_ref[s, r:r + nb, ns:]
                hbuf_ref[s, r:r + nb, :ns] = nr
                hbuf_ref[s, r:r + nb, ns:] = ni
                carry[i] = (nr, ni)
            if (t + 1) % (tl // n_stages) == 0:
                yield
        for i, s in enumerate(slabs):
            hst_ref[s, :, :ns] = carry[i][0]
            hst_ref[s, :, ns:] = carry[i][1]

    def y_stage(slabs):
        for s in slabs:
            for r in range(0, rows, row_block):
                h16 = hbuf_ref[s, nb + r:nb + r + row_block, :].astype(bf16)
                ys = jnp.dot(h16, cblk_ref[s], preferred_element_type=f32)
                ys = ys + dsk_ref[:, lanes(s)] * ut_ref[s, r:r + row_block, :]
                g_ref[s, r:r + row_block, :] = _gelu(ys)
                yield

    half = S5_SLABS // 2
    first, second = tuple(range(half)), tuple(range(half, S5_SLABS))
    _interleave(bu_stage(first))
    _interleave(scan_stage(first), bu_stage(second))
    _interleave(scan_stage(second), y_stage(first))
    _interleave(y_stage(second))


def _s5_scan_vmem(hbuf_ref, hst_ref, ar_ref, ai_ref, s, nb, rows):
    ns = S5_SLAB_STATE
    hbuf_ref[s, 0:nb, :] = hst_ref[s]
    ar, ai = ar_ref[s], ai_ref[s]

    def body(i, carry):
        j = pl.multiple_of(nb + i * SUBLANES, SUBLANES)
        prev = hbuf_ref[s, pl.ds(j - nb, SUBLANES), :]
        cur = hbuf_ref[s, pl.ds(j, SUBLANES), :]
        pr, pi = prev[:, :ns], prev[:, ns:]
        hbuf_ref[s, pl.ds(j, SUBLANES), :ns] = ar * pr - ai * pi + cur[:, :ns]
        hbuf_ref[s, pl.ds(j, SUBLANES), ns:] = ar * pi + ai * pr + cur[:, ns:]
        return carry

    lax.fori_loop(0, rows // SUBLANES, body, 0, unroll=min(8, nb // SUBLANES))
    hst_ref[s] = hbuf_ref[s, rows:rows + nb, :]


S5_N_IN = 10


def _s5_kernel(*refs, nb, tl, batch_major, n_cast):
    (u_ref, re0_ref, im0_ref, ar_ref, ai_ref, bblk_ref, cblk_ref, dsk_ref, wglu_ref,
     bglu_ref) = refs[:S5_N_IN]
    cast_in = refs[S5_N_IN:S5_N_IN + n_cast]
    y_ref, ren_ref, imn_ref = refs[S5_N_IN + n_cast:S5_N_IN + n_cast + 3]
    cast_out = refs[S5_N_IN + n_cast + 3:S5_N_IN + 2 * n_cast + 3]
    ut_ref, hbuf_ref, hst_ref, g_ref = refs[S5_N_IN + 2 * n_cast + 3:]
    for w_ref, w16_ref in zip(cast_in, cast_out):
        w16_ref[...] = w_ref[...].astype(bf16)
    step = pl.program_id(0)
    rows = nb * tl
    ns = S5_SLAB_STATE

    @pl.when(step == 0)
    def _():
        for s in range(S5_SLABS):
            hst_ref[s, :, :ns] = re0_ref[:, s * ns:(s + 1) * ns]
            hst_ref[s, :, ns:] = im0_ref[:, s * ns:(s + 1) * ns]

    for s in range(S5_SLABS):
        sl = slice(s * LANES, (s + 1) * LANES)
        if batch_major:
            for b in range(nb):
                ut_ref[s, pl.ds(b, tl, stride=nb), :] = u_ref[b, :, sl]
        else:
            ut_ref[s] = u_ref[:, :, sl].reshape(rows, LANES)

    if nb == SUBLANES:
        _s5_slabs_interleaved(ut_ref, hbuf_ref, hst_ref, g_ref, ar_ref, ai_ref, bblk_ref, cblk_ref, dsk_ref, tl)
    else:
        for s in range(S5_SLABS):
            sl = slice(s * LANES, (s + 1) * LANES)
            hbuf_ref[s, nb:nb + rows, :] = jnp.dot(ut_ref[s].astype(bf16), bblk_ref[s],
                                                   preferred_element_type=f32)
            _s5_scan_vmem(hbuf_ref, hst_ref, ar_ref, ai_ref, s, nb, rows)
            ys = jnp.dot(hbuf_ref[s, nb:nb + rows, :].astype(bf16), cblk_ref[s], preferred_element_type=f32)
            g_ref[s] = _gelu(ys + dsk_ref[:, sl] * ut_ref[s])

    g = jnp.concatenate([g_ref[s] for s in range(S5_SLABS)], axis=-1)
    gate = jnp.dot(g.astype(bf16), wglu_ref[...], preferred_element_type=f32) + bglu_ref[...]
    out = g * jax.nn.sigmoid(gate)
    if batch_major:
        for s in range(S5_SLABS):
            sl = slice(s * LANES, (s + 1) * LANES)
            ut_ref[s] = out[:, sl]
            for b in range(nb):
                y_ref[b, :, sl] = ut_ref[s, pl.ds(b, tl, stride=nb), :]
    else:
        y_ref[...] = out.reshape(tl, nb, D_S5)

    @pl.when(step == pl.num_programs(0) - 1)
    def _():
        for s in range(S5_SLABS):
            ren_ref[:, s * ns:(s + 1) * ns] = hst_ref[s, :, :ns]
            imn_ref[:, s * ns:(s + 1) * ns] = hst_ref[s, :, ns:]


def _s5(u, re0, im0, ar, ai, bblk, cblk, dsk, wglu, bglu, *, tl, batch_major, cast=()):
    if batch_major:
        nb, L, _ = u.shape
        u_spec = pl.BlockSpec((nb, tl, D_S5), lambda i: (0, i, 0))
    else:
        L, nb, _ = u.shape
        u_spec = pl.BlockSpec((tl, nb, D_S5), lambda i: (i, 0, 0))
    rows = nb * tl
    steps = L // tl
    nstate = S5_GROUPS * S5_STATE
    st_spec = _const_spec((nb, nstate))
    cast_specs = [pl.BlockSpec((w.shape[0] // steps, w.shape[1]), lambda i: (i, 0)) for w in cast]
    assert all(w.shape[0] % (steps * 2 * SUBLANES) == 0 for w in cast)
    return pl.pallas_call(
        functools.partial(_s5_kernel, nb=nb, tl=tl, batch_major=batch_major, n_cast=len(cast)),
        grid=(steps,),
        in_specs=[u_spec, st_spec, st_spec,
                  _const_spec((S5_SLABS, SUBLANES, S5_SLAB_STATE)),
                  _const_spec((S5_SLABS, SUBLANES, S5_SLAB_STATE)),
                  _const_spec((S5_SLABS, LANES, 2 * S5_SLAB_STATE)),
                  _const_spec((S5_SLABS, 2 * S5_SLAB_STATE, LANES)),
                  _const_spec((1, D_S5)), _const_spec((D_S5, D_S5)), _const_spec((1, D_S5))] + cast_specs,
        out_specs=[u_spec, st_spec, st_spec] + cast_specs,
        out_shape=[jax.ShapeDtypeStruct(u.shape, f32),
                   jax.ShapeDtypeStruct((nb, nstate), f32),
                   jax.ShapeDtypeStruct((nb, nstate), f32)]
                  + [jax.ShapeDtypeStruct(w.shape, bf16) for w in cast],
        scratch_shapes=[pltpu.VMEM((S5_SLABS, rows, LANES), f32),
                        pltpu.VMEM((S5_SLABS, nb + rows, 2 * S5_SLAB_STATE), f32),
                        pltpu.VMEM((S5_SLABS, nb, 2 * S5_SLAB_STATE), f32),
                        pltpu.VMEM((S5_SLABS, rows, LANES), f32)],
        compiler_params=_params("arbitrary"),
        name="s5_mixer",
    )(u, re0, im0, ar, ai, bblk, cblk, dsk, wglu, bglu, *cast)


def _ffn_stages(x, ys, y5, mods, n2g, nfg, wo_ref, wg_ref, wu_ref, wd_ref, write_out, ff_chunk):
    g1, sh2, sc2, g2, shf, scf = mods
    att = jnp.dot(ys.astype(bf16), wo_ref[:D_SSD, :], preferred_element_type=f32)
    att = att + jnp.dot(y5.astype(bf16), wo_ref[D_SSD:, :], preferred_element_type=f32)
    yield
    x1 = x + _by_row(lambda v, s: v * s, att, g1)
    v = _rms_mod(x1, n2g, sc2, sh2).astype(bf16)
    ff = None
    for o in range(0, D_FF, ff_chunk):
        gate = jnp.dot(v, wg_ref[:, o:o + ff_chunk], preferred_element_type=f32)
        up = jnp.dot(v, wu_ref[:, o:o + ff_chunk], preferred_element_type=f32)
        hid = (_silu(gate) * up).astype(bf16)
        part = jnp.dot(hid, wd_ref[o:o + ff_chunk, :], preferred_element_type=f32)
        ff = part if ff is None else ff + part
        yield
    x2 = x1 + _by_row(lambda v, s: v * s, ff, g2)
    write_out(_rms_mod(x2, nfg, scf, shf))


def _ffn_kernel(x_ref, ys_ref, y5_ref, mod_ref, modf_ref, n2g_ref, nfg_ref, wo_ref, wg_ref, wu_ref, wd_ref,
                o_ref, *, seq_major, ff_chunk, sub):
    mods = (_mod_row(mod_ref, 2, seq_major), _mod_row(mod_ref, 3, seq_major), _mod_row(mod_ref, 4, seq_major),
            _mod_row(mod_ref, 5, seq_major), _mod_row(modf_ref, 0, seq_major), _mod_row(modf_ref, 1, seq_major))
    x = _rows_in(x_ref, seq_major)
    rows = x.shape[0]
    tiles = {}

    def stages(r0):
        def write_out(y):
            if seq_major:
                tiles[r0] = y
            else:
                o_ref[0, r0:r0 + sub, :] = y
        return _ffn_stages(x[r0:r0 + sub], ys_ref[0, r0:r0 + sub, :], y5_ref[0, r0:r0 + sub, :], mods,
                           n2g_ref[...], nfg_ref[...], wo_ref, wg_ref, wu_ref, wd_ref, write_out, ff_chunk)

    _interleave(*[stages(r0) for r0 in range(0, rows, sub)])
    if seq_major:
        s, t, d = o_ref.shape
        y = jnp.concatenate([tiles[r0] for r0 in range(0, rows, sub)], axis=0)
        o_ref[...] = jnp.swapaxes(y.reshape(t, s, d), 0, 1)


def _ffn(x, ys, y5, mod, modf, n2g, nfg, wo, wg, wu, wd, *, tm, sub, seq_major, n_seq, first_seq, ff_chunk=256):
    if seq_major:
        x_spec = _const_spec(x.shape)
        nb, rows, d = 1, x.shape[0] * x.shape[1], x.shape[2]
        assert tm == rows
    else:
        nb, rows, d = x.shape
        x_spec = pl.BlockSpec((1, tm, d), lambda i, j: (i, j, 0))
    nt = rows // tm
    mod_spec = _mod_spec(mod, n_seq, first_seq)
    modf_spec = _mod_spec(modf, n_seq, first_seq)
    blk = lambda wd_: pl.BlockSpec((1, tm, wd_), lambda i, j: (i, j, 0))
    single = dict(pipeline_mode=pl.Buffered(1))
    wspec = lambda shape: pl.BlockSpec(shape, lambda i, j: (0, 0), **single)
    return pl.pallas_call(
        functools.partial(_ffn_kernel, seq_major=seq_major, ff_chunk=ff_chunk, sub=sub),
        grid=(nb, nt),
        in_specs=[x_spec, blk(D_SSD), blk(D_S5), mod_spec, modf_spec,
                  _const_spec((1, d)), _const_spec((1, d)),
                  wspec((d, d)), wspec((d, D_FF)), wspec((d, D_FF)), wspec((D_FF, d))],
        out_specs=x_spec,
        out_shape=jax.ShapeDtypeStruct(x.shape, f32),
        compiler_params=_params("parallel", "parallel"),
        name="out_ffn",
    )(x, ys, y5, mod, modf, n2g, nfg, wo, wg, wu, wd)


def kernel(x_prompt, x_sample, c_prompt, c_sample, state_ssd, state_conv, state_s5_re, state_s5_im, w_ada, b_ada, norm1_g, w_in, conv_w, conv_b, ssd_dt_bias, ssd_A_log, ssd_D, ssd_norm_g, s5_A_re, s5_A_im, s5_log_step, s5_B_re, s5_B_im, s5_C_re, s5_C_im, s5_D, w_glu, b_glu, w_out, norm2_g, w_ffn_gate, w_ffn_up, w_ffn_down, w_ada_f, b_ada_f, normf_g):
    assert w_ada.shape[0] == 1, "single-layer stack"
    bp, seq, d = x_prompt.shape
    bs, dseq, _ = x_sample.shape

    c_all = jnp.concatenate([c_sample, c_prompt], axis=0)
    mod, modf = _ada_mod(c_all, (w_ada[0], w_ada_f), (b_ada[0], b_ada_f))
    seqs_s = dict(n_seq=bs, first_seq=0)
    seqs_p = dict(n_seq=bp, first_seq=bs)

    w_in_p = w_in[0].T
    pad_h = lambda v: jnp.concatenate([v, jnp.zeros((DT_PAD - HEADS,), f32)]).reshape(1, DT_PAD)
    dtb = pad_h(ssd_dt_bias[0])
    alog = pad_h(ssd_A_log[0])
    dexp = jnp.repeat(ssd_D[0], HEAD_DIM).reshape(1, D_SSD)
    ng = ssd_norm_g[0].reshape(1, D_SSD)
    cw, cb = conv_w[0], conv_b[0].reshape(1, CONV_DIM)

    ar, ai, bblk, cblk = _s5_params(s5_A_re[0], s5_A_im[0], s5_log_step[0],
                                    s5_B_re[0].transpose(0, 2, 1), s5_B_im[0].transpose(0, 2, 1),
                                    s5_C_re[0], s5_C_im[0])
    dsk = s5_D[0].reshape(1, D_S5)
    wglu = w_glu[0].astype(bf16)
    bglu = b_glu[0].reshape(1, D_S5)
    n1g, n2g, nfg = norm1_g[0].reshape(1, d), norm2_g[0].reshape(1, d), normf_g.reshape(1, d)
    nstate = S5_GROUPS * S5_STATE

    z, act, u5, dt, conv_p = _inproj(x_prompt, mod, n1g, w_in_p, (cw, cb), tm=1024, sub=512, seq_major=False,
                                     **seqs_p)
    y_ssd, ssd_p = _ssd_prompt(act, z, dt, dtb, alog, dexp, ng)
    zeros_st = jnp.zeros((bp, nstate), f32)
    y_s5, re_p, im_p, wo, wg, wu, wd = _s5(
        u5, zeros_st, zeros_st, ar, ai, bblk, cblk, dsk, wglu, bglu, tl=128, batch_major=True,
        cast=(w_out[0], w_ffn_gate[0], w_ffn_up[0], w_ffn_down[0]))
    y_prompt = _ffn(x_prompt, y_ssd, y_s5, mod, modf, n2g, nfg, wo, wg, wu, wd, tm=1024, sub=512,
                    seq_major=False, **seqs_p)

    rows_s = dseq * bs
    steps = lambda a: a.reshape(dseq, bs, a.shape[-1])
    flat = lambda a: a.reshape(1, rows_s, a.shape[-1])
    z, xbc, u5, dt = _inproj(x_sample, mod, n1g, w_in_p, tm=rows_s, sub=rows_s // 2, seq_major=True, **seqs_s)
    y_ssd, ssd_s, conv_s = _ssd_sample(steps(xbc), steps(z), steps(dt), state_conv[0].transpose(1, 0, 2),
                                       state_ssd[0], cw, cb, dtb, alog, dexp, ng)
    y_s5, re_s, im_s = _s5(steps(u5), state_s5_re[0].reshape(bs, nstate), state_s5_im[0].reshape(bs, nstate),
                           ar, ai, bblk, cblk, dsk, wglu, bglu, tl=dseq, batch_major=False)
    y_sample = _ffn(x_sample, flat(y_ssd), flat(y_s5), mod, modf, n2g, nfg, wo, wg, wu, wd,
                    tm=rows_s, sub=rows_s, seq_major=True, **seqs_s)

    g5 = (S5_GROUPS, S5_STATE)
    return (y_prompt, y_sample,
            ssd_p[None], ssd_s.reshape((1,) + state_ssd.shape[1:]),
            conv_p[None], conv_s.transpose(1, 0, 2)[None],
            re_p.reshape((1, bp) + g5), re_s.reshape((1, bs) + g5),
            im_p.reshape((1, bp) + g5), im_s.reshape((1, bs) + g5))
```

```python
import functools

import jax
import jax.numpy as jnp
from jax import lax
from jax.experimental import pallas as pl
from jax.experimental.pallas import tpu as pltpu

f32 = jnp.float32
bf16 = jnp.bfloat16

D_MODEL = 1024
D_SSD = 512
HEAD_DIM = 64
HEADS = 8
GROUPS = 2
HEADS_PER_GROUP = HEADS // GROUPS
STATE = 128
CONV_W = 4
CONV_DIM = D_SSD + 2 * GROUPS * STATE
D_S5 = 512
S5_CH = 16
S5_GROUPS = 32
S5_STATE = 64
D_FF = 2816
N_ADA = 6
EPS = 1e-6

LANES = 128
SUBLANES = 8
SSD_CHUNK = 128
S5_SLABS = D_S5 // LANES
S5_SLAB_STATE = (S5_GROUPS // S5_SLABS) * S5_STATE
DT_PAD = LANES
VMEM_LIMIT = 56 * 1024 * 1024

NT_DIMS = (((1,), (1,)), ((), ()))
TN_DIMS = (((0,), (0,)), ((), ()))


def _silu(x):
    return x * jax.nn.sigmoid(x)


def _interleave(*gens, lead=0):
    pending, live, rnd = list(gens), [], 0
    while pending or live:
        while pending and rnd >= lead * (len(gens) - len(pending)):
            live.append(pending.pop(0))
        for g in list(live):
            try:
                next(g)
            except StopIteration:
                live.remove(g)
        rnd += 1


def _by_row(fn, v, *ms):
    r, d = v.shape
    m_rows = ms[0].shape[0]
    if m_rows in (1, r):
        return fn(v, *ms)
    out = fn(v.reshape(r // m_rows, m_rows, d), *[m[None] for m in ms])
    return out.reshape(r, d)


def _rms_mod(x, g, sc, sh):
    y = x * lax.rsqrt(jnp.mean(x * x, axis=-1, keepdims=True) + EPS)
    return _by_row(lambda v, s, t: v * (1.0 + s) + t, y * g, sc, sh)


def _mod_row(mod_ref, i, per_row):
    return mod_ref[i] if per_row else mod_ref[i, pl.ds(pl.program_id(0), 1), :]


def _mod_spec(mod, n_seq, first_seq):
    assert first_seq % n_seq == 0
    return pl.BlockSpec((mod.shape[0], n_seq, mod.shape[2]), lambda *_: (0, first_seq // n_seq, 0))


def _const_spec(shape):
    nd = len(shape)
    return pl.BlockSpec(shape, lambda *_: (0,) * nd)


def _params(*sem):
    return pltpu.CompilerParams(dimension_semantics=sem, vmem_limit_bytes=VMEM_LIMIT)


ADA_K_BLOCK = 256


def _ada_kernel(c_ref, *refs):
    nw = len(refs) // 3
    s = _silu(c_ref[...]).astype(bf16)
    for w_ref, b_ref, o_ref in zip(refs[:nw], refs[nw:2 * nw], refs[2 * nw:]):
        n, _, d = o_ref.shape

        @pl.when(pl.program_id(0) == 0)
        def _():
            for j in range(n):
                o_ref[j] = jnp.broadcast_to(b_ref[:, j * d:(j + 1) * d], o_ref.shape[1:])

        for j in range(n):
            o_ref[j] += jnp.dot(s, w_ref[:, j * d:(j + 1) * d].astype(bf16), preferred_element_type=f32)


def _ada_mod(c, ws, bs):
    m, k = c.shape
    ns = [w.shape[1] // k for w in ws]
    return pl.pallas_call(
        _ada_kernel,
        grid=(k // ADA_K_BLOCK,),
        in_specs=[pl.BlockSpec((m, ADA_K_BLOCK), lambda i: (0, i))]
                 + [pl.BlockSpec((ADA_K_BLOCK, n * k), lambda i: (i, 0)) for n in ns]
                 + [_const_spec((1, n * k)) for n in ns],
        out_specs=[_const_spec((n, m, k)) for n in ns],
        out_shape=[jax.ShapeDtypeStruct((n, m, k), f32) for n in ns],
        compiler_params=_params("arbitrary"),
        name="ada_mod",
    )(c, *ws, *[b.reshape(1, n * k) for b, n in zip(bs, ns)])


def _rows_in(x_ref, seq_major):
    if not seq_major:
        return x_ref[0]
    s, t, d = x_ref.shape
    return jnp.swapaxes(x_ref[...], 0, 1).reshape(s * t, d)


def _conv_silu(ext, cw_ref, cb_ref):
    w = [cw_ref[k:k + 1, :] for k in range(CONV_W)]
    back1 = pltpu.roll(ext, 1, 0)
    older = pltpu.roll(ext * w[1] + back1 * w[0], 2, 0)
    conv = cb_ref[...] + ext[SUBLANES:, :] * w[3] + back1[SUBLANES:, :] * w[2] + older[SUBLANES:, :]
    return _silu(conv)


def _inproj_stages(x, sh, sc, g, outs, after=()):
    u = _rms_mod(x, g, sc, sh).astype(bf16)
    yield
    for write, w16 in outs:
        write(lax.dot_general(u, w16, NT_DIMS, preferred_element_type=f32))
        yield
    for stage in after:
        stage()
        yield


def _inproj_kernel(x_ref, mod_ref, g_ref, w_ref, *rest, seq_major, sub, conv):
    if conv:
        cw_ref, cb_ref, z_ref, act_ref, u5_ref, dt_ref, cn_ref, ext_ref = rest

        @pl.when(pl.program_id(1) == 0)
        def _():
            ext_ref[0:SUBLANES, :] = jnp.zeros((SUBLANES, CONV_DIM), f32)
    else:
        z_ref, xbc_ref, u5_ref, dt_ref = rest
    sh = _mod_row(mod_ref, 0, seq_major)
    sc = _mod_row(mod_ref, 1, seq_major)
    x = _rows_in(x_ref, seq_major)
    rows = x.shape[0]
    T = SSD_CHUNK
    o_dt = D_SSD + CONV_DIM
    w16 = lambda o, width: w_ref[o:o + width, :].astype(bf16)
    w_z, w_xbc, w_u5, w_dt = w16(0, D_SSD), w16(D_SSD, CONV_DIM), w16(o_dt + HEADS, D_S5), w16(o_dt, DT_PAD)

    def stages(r0):
        def to(ref):
            def write(v):
                ref[0, r0:r0 + sub, :] = v
            return write

        def to_ext(v):
            ext_ref[SUBLANES + r0:SUBLANES + r0 + sub, :] = v

        def conv_chunk(c0):
            def stage():
                act_ref[0, c0:c0 + T, :] = _conv_silu(ext_ref[c0:c0 + SUBLANES + T, :], cw_ref, cb_ref)
            return stage

        outs = [(to(z_ref), w_z), (to_ext if conv else to(xbc_ref), w_xbc), (to(u5_ref), w_u5), (to(dt_ref), w_dt)]
        after = [conv_chunk(c0) for c0 in range(r0, r0 + sub, T)] if conv else ()
        return _inproj_stages(x[r0:r0 + sub], sh, sc, g_ref[...], outs, after)

    n_matmul_stages = 5
    _interleave(*[stages(r0) for r0 in range(0, rows, sub)], lead=n_matmul_stages if conv else 0)
    if conv:
        cn_ref[0] = ext_ref[SUBLANES + rows - (CONV_W - 1):SUBLANES + rows, :]
        ext_ref[0:SUBLANES, :] = ext_ref[rows:rows + SUBLANES, :]


def _inproj(x, mod, g, w, conv_wb=None, *, tm, sub, seq_major, n_seq, first_seq):
    if seq_major:
        x_spec = _const_spec(x.shape)
        nb, rows, d = 1, x.shape[0] * x.shape[1], x.shape[2]
        assert tm == rows
    else:
        nb, rows, d = x.shape
        x_spec = pl.BlockSpec((1, tm, d), lambda i, j: (i, j, 0))
    nt = rows // tm
    widths = (D_SSD, CONV_DIM, D_S5, DT_PAD)
    conv = conv_wb is not None
    in_specs = [x_spec, _mod_spec(mod, n_seq, first_seq), _const_spec((1, d)),
                pl.BlockSpec(w.shape, lambda i, j: (0, 0), pipeline_mode=pl.Buffered(1))]
    out_specs = [pl.BlockSpec((1, tm, wd), lambda i, j: (i, j, 0)) for wd in widths]
    out_shape = [jax.ShapeDtypeStruct((nb, rows, wd), f32) for wd in widths]
    scratch = []
    if conv:
        in_specs += [_const_spec((CONV_W, CONV_DIM)), _const_spec((1, CONV_DIM))]
        out_specs.append(pl.BlockSpec((1, CONV_W - 1, CONV_DIM), lambda i, j: (i, 0, 0)))
        out_shape.append(jax.ShapeDtypeStruct((nb, CONV_W - 1, CONV_DIM), f32))
        scratch.append(pltpu.VMEM((SUBLANES + tm, CONV_DIM), f32))
    return pl.pallas_call(
        functools.partial(_inproj_kernel, seq_major=seq_major, sub=sub, conv=conv),
        grid=(nb, nt),
        in_specs=in_specs, out_specs=out_specs, out_shape=out_shape, scratch_shapes=scratch,
        compiler_params=_params("parallel", "arbitrary" if conv else "parallel"),
        name="in_proj",
    )(x, mod, g, w, *(conv_wb or ()))


def _split3(x):
    hi = x.astype(bf16)
    r1 = x - hi.astype(f32)
    mid = r1.astype(bf16)
    lo = (r1 - mid.astype(f32)).astype(bf16)
    return hi, mid, lo


def _dot_sel_lhs(sel16, x):
    return sum(jnp.dot(sel16, p, preferred_element_type=f32) for p in _split3(x))


def _dot_sel_rhs(x, sel16):
    hi, mid, _ = _split3(x)
    return jnp.dot(hi, sel16, preferred_element_type=f32) + jnp.dot(mid, sel16, preferred_element_type=f32)


def _gated_group_norm(y, z, ng):
    y = y * _silu(z)
    gw = D_SSD // GROUPS
    parts = []
    for g in range(GROUPS):
        yg = y[:, g * gw:(g + 1) * gw]
        parts.append(yg * lax.rsqrt(jnp.mean(yg * yg, axis=-1, keepdims=True) + EPS))
    return jnp.concatenate(parts, axis=-1) * ng


def _ssd_prompt_kernel(act_ref, z_ref, dt_ref, dtb_ref, alog_ref, dexp_ref, ng_ref, y_ref, st_ref, h_ref, *, cps):
    T = SSD_CHUNK
    gw = HEADS_PER_GROUP * HEAD_DIM

    @pl.when(pl.program_id(1) == 0)
    def _():
        h_ref[...] = jnp.zeros_like(h_ref)

    row = lax.broadcasted_iota(jnp.int32, (T, T), 0)
    col = lax.broadcasted_iota(jnp.int32, (T, T), 1)
    tri = row >= col
    tri16 = jnp.where(tri, 1.0, 0.0).astype(bf16)
    low_half = lax.broadcasted_iota(jnp.int32, (T, LANES), 1) < HEAD_DIM
    a_neg = -jnp.exp(alog_ref[...])

    lane_head = lax.broadcasted_iota(jnp.int32, (DT_PAD, D_SSD), 1) // HEAD_DIM
    sel16 = jnp.where(lax.broadcasted_iota(jnp.int32, (DT_PAD, D_SSD), 0) == lane_head, 1.0, 0.0).astype(bf16)

    def spread(q):
        return _dot_sel_rhs(q, sel16)

    def state_free(ci):
        r0 = ci * T
        act = act_ref[0, r0:r0 + T, :]
        xs = act[:, :D_SSD]
        bm = act[:, D_SSD:D_SSD + GROUPS * STATE]
        cm = act[:, D_SSD + GROUPS * STATE:]
        dtv = jax.nn.softplus(dt_ref[0, r0:r0 + T, :] + dtb_ref[...])
        a_cs = _dot_sel_lhs(tri16, dtv * a_neg)
        bg16s = [bm[:, g * STATE:(g + 1) * STATE].astype(bf16) for g in range(GROUPS)]
        cg16s = [cm[:, g * STATE:(g + 1) * STATE].astype(bf16) for g in range(GROUPS)]
        gmats = [lax.dot_general(cg16s[g], bg16s[g], NT_DIMS, preferred_element_type=f32)
                 for g in range(GROUPS)]
        yield
        a_last = a_cs[T - 1:T, :]
        a_cs_t = a_cs.T
        rep = spread(jnp.concatenate([dtv, jnp.exp(a_last - a_cs), jnp.exp(a_cs)], axis=0))
        yield
        x = xs * rep[:T]
        x16 = x.astype(bf16)
        xd16 = (x * rep[T:2 * T]).astype(bf16)
        e_cs = rep[2 * T:]
        yield
        y_part = []
        for g in range(GROUPS):
            for pr in range(HEADS_PER_GROUP // 2):
                s_pair = []
                for q in range(2):
                    h = g * HEADS_PER_GROUP + 2 * pr + q
                    seg = a_cs[:, h:h + 1] - a_cs_t[h:h + 1, :]
                    lmat = jnp.exp(jnp.where(tri, seg, -jnp.inf))
                    s_pair.append((gmats[g] * lmat).astype(bf16))
                lanes = slice(g * gw + pr * LANES, g * gw + (pr + 1) * LANES)
                both = jnp.dot(jnp.concatenate(s_pair, axis=0), x16[:, lanes], preferred_element_type=f32)
                yield
                y_part.append(jnp.where(low_half, both[:T], both[T:]))
        y_free = jnp.concatenate(y_part, axis=-1) + xs * dexp_ref[...]
        return y_free, e_cs, jnp.exp(a_last), xd16, bg16s, cg16s

    def state_step(ci, y_free, e_cs, e_last, xd16, bg16s, cg16s):
        r0 = ci * T
        y_offs = []
        for g in range(GROUPS):
            hp = h_ref[g]
            y_offs.append(lax.dot_general(cg16s[g], hp.astype(bf16), NT_DIMS, preferred_element_type=f32))
            upd = lax.dot_general(xd16[:, g * gw:(g + 1) * gw], bg16s[g], TN_DIMS, preferred_element_type=f32)
            for hh in range(HEADS_PER_GROUP):
                h = g * HEADS_PER_GROUP + hh
                rows = slice(hh * HEAD_DIM, (hh + 1) * HEAD_DIM)
                h_ref[g, rows, :] = e_last[:, h:h + 1] * hp[rows] + upd[rows]
        y = y_free + jnp.concatenate(y_offs, axis=-1) * e_cs
        y_ref[0, r0:r0 + T, :] = _gated_group_norm(y, z_ref[0, r0:r0 + T, :], ng_ref[...])

    gens = [state_free(ci) for ci in range(cps)]
    free = [None] * cps
    while any(f is None for f in free):
        for ci in range(cps):
            if free[ci] is None:
                try:
                    next(gens[ci])
                except StopIteration as done:
                    free[ci] = done.value
    for ci in range(cps):
        state_step(ci, *free[ci])

    @pl.when(pl.program_id(1) == pl.num_programs(1) - 1)
    def _():
        st_ref[0] = h_ref[...].reshape(HEADS, HEAD_DIM, STATE)


def _ssd_prompt(act, z, dt, dtb, alog, dexp, ng, *, cps=8):
    nb, L, _ = act.shape
    rows = cps * SSD_CHUNK
    blk = lambda wd: pl.BlockSpec((1, rows, wd), lambda b, c: (b, c, 0))
    return pl.pallas_call(
        functools.partial(_ssd_prompt_kernel, cps=cps),
        grid=(nb, L // rows),
        in_specs=[blk(CONV_DIM), blk(D_SSD), blk(DT_PAD), _const_spec((1, DT_PAD)), _const_spec((1, DT_PAD)),
                  _const_spec((1, D_SSD)), _const_spec((1, D_SSD))],
        out_specs=[blk(D_SSD), pl.BlockSpec((1, HEADS, HEAD_DIM, STATE), lambda b, c: (b, 0, 0, 0))],
        out_shape=[jax.ShapeDtypeStruct((nb, L, D_SSD), f32),
                   jax.ShapeDtypeStruct((nb, HEADS, HEAD_DIM, STATE), f32)],
        scratch_shapes=[pltpu.VMEM((GROUPS, HEADS_PER_GROUP * HEAD_DIM, STATE), f32)],
        compiler_params=_params("parallel", "arbitrary"),
        name="ssd_prompt",
    )(act, z, dt, dtb, alog, dexp, ng)


def _ssd_sample_kernel(xbc_ref, z_ref, dt_ref, cbuf_ref, st_ref, cw_ref, cb_ref, dtb_ref, alog_ref,
                       dexp_ref, ng_ref, y_ref, stn_ref, cn_ref, dtot_ref, *, L, nb):
    gw = HEADS_PER_GROUP * HEAD_DIM
    full = [cbuf_ref[j] for j in range(CONV_W - 1)] + [xbc_ref[t] for t in range(L)]
    for j in range(CONV_W - 1):
        cn_ref[j] = full[L + j]
    a_neg = -jnp.exp(alog_ref[...])
    xs, bm, cm, dtv, acs = [], [], [], [], []
    run = None
    for t in range(L):
        conv = cb_ref[...]
        for k in range(CONV_W):
            conv = conv + full[t + k] * cw_ref[k:k + 1, :]
        act = _silu(conv)
        xs.append(act[:, :D_SSD])
        bm.append(act[:, D_SSD:D_SSD + GROUPS * STATE])
        cm.append(act[:, D_SSD + GROUPS * STATE:])
        d = jax.nn.softplus(dt_ref[t] + dtb_ref[...])
        dtv.append(d)
        run = d * a_neg if run is None else run + d * a_neg
        acs.append(run)
    a_tot = acs[L - 1]
    dtot_ref[...] = jnp.exp(a_tot)

    lane_head = lax.broadcasted_iota(jnp.int32, (DT_PAD, D_SSD), 1) // HEAD_DIM
    sel16 = jnp.where(lax.broadcasted_iota(jnp.int32, (DT_PAD, D_SSD), 0) == lane_head, 1.0, 0.0).astype(bf16)
    pairs = [(t, s) for t in range(L) for s in range(t)]
    factors = (dtv + [jnp.exp(a_tot - acs[t]) for t in range(L)] + [jnp.exp(acs[t]) for t in range(L)]
               + [jnp.exp(acs[t] - acs[s]) for t, s in pairs])
    rep = _dot_sel_rhs(jnp.concatenate(factors, axis=0), sel16)
    piece = lambda i: rep[i * nb:(i + 1) * nb]
    x = [xs[t] * piece(t) for t in range(L)]
    xd_stack = jnp.concatenate([x[t] * piece(L + t) for t in range(L)], axis=0)
    e_cs = [piece(2 * L + t) for t in range(L)]
    decay = {ts: piece(3 * L + i) for i, ts in enumerate(pairs)}

    in_group0 = lax.broadcasted_iota(jnp.int32, (nb, D_SSD), 1) < gw
    y_intra = []
    for t in range(L):
        acc = None
        for s in range(t + 1):
            cb_dot = [jnp.sum(cm[t][:, g * STATE:(g + 1) * STATE] * bm[s][:, g * STATE:(g + 1) * STATE],
                              axis=-1, keepdims=True) for g in range(GROUPS)]
            w = jnp.where(in_group0, cb_dot[0], cb_dot[1])
            term = w * x[s] if s == t else w * decay[(t, s)] * x[s]
            acc = term if acc is None else acc + term
        y_intra.append(acc)

    c_stack = [jnp.concatenate([cm[t][:, g * STATE:(g + 1) * STATE] for t in range(L)], axis=0).astype(bf16)
               for g in range(GROUPS)]
    b_stack = [jnp.concatenate([bm[t][:, g * STATE:(g + 1) * STATE] for t in range(L)], axis=0).astype(bf16)
               for g in range(GROUPS)]
    seq_of_row = lax.broadcasted_iota(jnp.int32, (L * nb, gw), 0) & (nb - 1)

    def per_seq(b, yoff):
        mine = seq_of_row == b
        drow = dtot_ref[pl.ds(b, 1), :]
        out = []
        for g in range(GROUPS):
            h0 = st_ref[b, g]
            r = lax.dot_general(c_stack[g], h0.astype(bf16), NT_DIMS, preferred_element_type=f32)
            out.append(yoff[g] + jnp.where(mine, r, 0.0))
            xm = jnp.where(mine, xd_stack[:, g * gw:(g + 1) * gw], 0.0).astype(bf16)
            upd = lax.dot_general(xm, b_stack[g], TN_DIMS, preferred_element_type=f32)
            for hh in range(HEADS_PER_GROUP):
                h = g * HEADS_PER_GROUP + hh
                rows = slice(hh * HEAD_DIM, (hh + 1) * HEAD_DIM)
                stn_ref[b, g, rows, :] = drow[:, h:h + 1] * h0[rows] + upd[rows]
        return tuple(out)

    yoff = lax.fori_loop(0, nb, per_seq, tuple(jnp.zeros((L * nb, gw), f32) for _ in range(GROUPS)),
                         unroll=8)

    for t in range(L):
        y_off = jnp.concatenate([yoff[g][t * nb:(t + 1) * nb] for g in range(GROUPS)], axis=-1)
        y = y_intra[t] + y_off * e_cs[t] + xs[t] * dexp_ref[...]
        y_ref[t] = _gated_group_norm(y, z_ref[t], ng_ref[...])


def _ssd_sample(xbc, z, dt, cbuf, st, cw, cb, dtb, alog, dexp, ng, *, nb=16):
    L, B, _ = xbc.shape
    assert nb & (nb - 1) == 0 and B % nb == 0
    tblk = lambda n, wd: pl.BlockSpec((n, nb, wd), lambda i: (0, i, 0))
    gw = HEADS_PER_GROUP * HEAD_DIM
    st = st.reshape(B, GROUPS, gw, STATE)
    st_spec = pl.BlockSpec((nb, GROUPS, gw, STATE), lambda i: (i, 0, 0, 0))
    return pl.pallas_call(
        functools.partial(_ssd_sample_kernel, L=L, nb=nb),
        grid=(B // nb,),
        in_specs=[tblk(L, CONV_DIM), tblk(L, D_SSD), tblk(L, DT_PAD), tblk(CONV_W - 1, CONV_DIM), st_spec,
                  _const_spec((CONV_W, CONV_DIM)), _const_spec((1, CONV_DIM)),
                  _const_spec((1, DT_PAD)), _const_spec((1, DT_PAD)),
                  _const_spec((1, D_SSD)), _const_spec((1, D_SSD))],
        out_specs=[tblk(L, D_SSD), st_spec, tblk(CONV_W - 1, CONV_DIM)],
        out_shape=[jax.ShapeDtypeStruct((L, B, D_SSD), f32),
                   jax.ShapeDtypeStruct((B, GROUPS, gw, STATE), f32),
                   jax.ShapeDtypeStruct((CONV_W - 1, B, CONV_DIM), f32)],
        scratch_shapes=[pltpu.VMEM((nb, DT_PAD), f32)],
        compiler_params=_params("parallel"),
        name="ssd_sample",
    )(xbc, z, dt, cbuf, st, cw, cb, dtb, alog, dexp, ng)


def _s5_param_kernel(lr_ref, li_ref, ls_ref, br_ref, bi_ref, cr_ref, ci_ref,
                     ar_ref, ai_ref, bblk_ref, cblk_ref, b_scr, c_scr):
    lr, li = lr_ref[...], li_ref[...]
    step = jnp.exp(ls_ref[...])
    mag = jnp.exp(lr * step)
    abr = mag * jnp.cos(li * step)
    abi = mag * jnp.sin(li * step)
    nr, ni = abr - 1.0, abi
    den = lr * lr + li * li
    fr = (nr * lr + ni * li) / den
    fi = (ni * lr - nr * li) / den
    br, bi = br_ref[...], bi_ref[...]
    bbr = fr * br - fi * bi
    bbi = fr * bi + fi * br
    b_scr[...] = jnp.zeros_like(b_scr)
    c_scr[...] = jnp.zeros_like(c_scr)
    gps = S5_GROUPS // S5_SLABS
    ns = S5_SLAB_STATE
    for g in range(S5_GROUPS):
        s, gl = divmod(g, gps)
        ch = slice(gl * S5_CH, (gl + 1) * S5_CH)
        st = slice(gl * S5_STATE, (gl + 1) * S5_STATE)
        st_im = slice(ns + gl * S5_STATE, ns + (gl + 1) * S5_STATE)
        ar_ref[s, :, st] = jnp.broadcast_to(abr[g], (SUBLANES, S5_STATE))
        ai_ref[s, :, st] = jnp.broadcast_to(abi[g], (SUBLANES, S5_STATE))
        b_scr[s, ch, st] = bbr[g]
        b_scr[s, ch, st_im] = bbi[g]
        c_scr[s, st, ch] = cr_ref[g].T
        c_scr[s, st_im, ch] = -ci_ref[g].T
    bblk_ref[...] = b_scr[...].astype(bf16)
    cblk_ref[...] = c_scr[...].astype(bf16)


def _s5_params(a_re, a_im, log_step, b_re_t, b_im_t, c_re, c_im):
    g, p = a_re.shape
    ns = S5_SLAB_STATE
    return pl.pallas_call(
        _s5_param_kernel,
        out_shape=[jax.ShapeDtypeStruct((S5_SLABS, SUBLANES, ns), f32),
                   jax.ShapeDtypeStruct((S5_SLABS, SUBLANES, ns), f32),
                   jax.ShapeDtypeStruct((S5_SLABS, LANES, 2 * ns), bf16),
                   jax.ShapeDtypeStruct((S5_SLABS, 2 * ns, LANES), bf16)],
        scratch_shapes=[pltpu.VMEM((S5_SLABS, LANES, 2 * ns), f32),
                        pltpu.VMEM((S5_SLABS, 2 * ns, LANES), f32)],
        name="s5_params",
    )(a_re.reshape(g, 1, p), a_im.reshape(g, 1, p), log_step.reshape(g, 1, 1), b_re_t, b_im_t, c_re, c_im)


def _gelu(x):
    return 0.5 * x * (1.0 + lax.erf(x * (2.0 ** -0.5)))


def _s5_slabs_interleaved(ut_ref, hbuf_ref, hst_ref, g_ref, ar_ref, ai_ref, bblk_ref, cblk_ref, dsk_ref, tl,
                          row_block=256, n_stages=8):
    ns = S5_SLAB_STATE
    nb = SUBLANES
    rows = nb * tl

    def lanes(s):
        return slice(s * LANES, (s + 1) * LANES)

    def bu_stage(slabs):
        for s in slabs:
            for r in range(0, rows, row_block):
                us = ut_ref[s, r:r + row_block, :]
                hbuf_ref[s, nb + r:nb + r + row_block, :] = jnp.dot(us.astype(bf16), bblk_ref[s],
                                                                    preferred_element_type=f32)
                yield

    def scan_stage(slabs):
        carry = [(hst_ref[s, :, :ns], hst_ref[s, :, ns:]) for s in slabs]
        coef = [(ar_ref[s], ai_ref[s]) for s in slabs]
        for t in range(tl):
            r = nb + nb * t
            for i, s in enumerate(slabs):
                (pr, pi), (ar, ai) = carry[i], coef[i]
                nr = ar * pr - ai * pi + hbuf_ref[s, r:r + nb, :ns]
                ni = ar * pi + ai * pr + hbuf_ref[s, r:r + nb, ns:]
                hbuf_ref[s, r:r + nb, :ns] = nr
                hbuf_ref[s, r:r + nb, ns:] = ni
                carry[i] = (nr, ni)
            if (t + 1) % (tl // n_stages) == 0:
                yield
        for i, s in enumerate(slabs):
            hst_ref[s, :, :ns] = carry[i][0]
            hst_ref[s, :, ns:] = carry[i][1]

    def y_stage(slabs):
        for s in slabs:
            for r in range(0, rows, row_block):
                h16 = hbuf_ref[s, nb + r:nb + r + row_block, :].astype(bf16)
                ys = jnp.dot(h16, cblk_ref[s], preferred_element_type=f32)
                ys = ys + dsk_ref[:, lanes(s)] * ut_ref[s, r:r + row_block, :]
                g_ref[s, r:r + row_block, :] = _gelu(ys)
                yield

    half = S5_SLABS // 2
    first, second = tuple(range(half)), tuple(range(half, S5_SLABS))
    _interleave(bu_stage(first))
    _interleave(scan_stage(first), bu_stage(second))
    _interleave(scan_stage(second), y_stage(first))
    _interleave(y_stage(second))


def _s5_scan_vmem(hbuf_ref, hst_ref, ar_ref, ai_ref, s, nb, rows):
    ns = S5_SLAB_STATE
    hbuf_ref[s, 0:nb, :] = hst_ref[s]
    ar, ai = ar_ref[s], ai_ref[s]

    def body(i, carry):
        j = pl.multiple_of(nb + i * SUBLANES, SUBLANES)
        prev = hbuf_ref[s, pl.ds(j - nb, SUBLANES), :]
        cur = hbuf_ref[s, pl.ds(j, SUBLANES), :]
        pr, pi = prev[:, :ns], prev[:, ns:]
        hbuf_ref[s, pl.ds(j, SUBLANES), :ns] = ar * pr - ai * pi + cur[:, :ns]
        hbuf_ref[s, pl.ds(j, SUBLANES), ns:] = ar * pi + ai * pr + cur[:, ns:]
        return carry

    lax.fori_loop(0, rows // SUBLANES, body, 0, unroll=min(8, nb // SUBLANES))
    hst_ref[s] = hbuf_ref[s, rows:rows + nb, :]


S5_N_IN = 10


def _s5_kernel(*refs, nb, tl, batch_major, n_cast):
    (u_ref, re0_ref, im0_ref, ar_ref, ai_ref, bblk_ref, cblk_ref, dsk_ref, wglu_ref,
     bglu_ref) = refs[:S5_N_IN]
    cast_in = refs[S5_N_IN:S5_N_IN + n_cast]
    y_ref, ren_ref, imn_ref = refs[S5_N_IN + n_cast:S5_N_IN + n_cast + 3]
    cast_out = refs[S5_N_IN + n_cast + 3:S5_N_IN + 2 * n_cast + 3]
    ut_ref, hbuf_ref, hst_ref, g_ref = refs[S5_N_IN + 2 * n_cast + 3:]
    for w_ref, w16_ref in zip(cast_in, cast_out):
        w16_ref[...] = w_ref[...].astype(bf16)
    step = pl.program_id(0)
    rows = nb * tl
    ns = S5_SLAB_STATE

    gps = S5_GROUPS // S5_SLABS

    @pl.when(step == 0)
    def _():
        for g in range(S5_GROUPS):
            s, o = g // gps, (g % gps) * S5_STATE
            hst_ref[s, :, o:o + S5_STATE] = re0_ref[:, g, :]
            hst_ref[s, :, ns + o:ns + o + S5_STATE] = im0_ref[:, g, :]

    for s in range(S5_SLABS):
        sl = slice(s * LANES, (s + 1) * LANES)
        if batch_major:
            for b in range(nb):
                ut_ref[s, pl.ds(b, tl, stride=nb), :] = u_ref[b, :, sl]
        else:
            ut_ref[s] = u_ref[:, :, sl].reshape(rows, LANES)

    if nb == SUBLANES:
        _s5_slabs_interleaved(ut_ref, hbuf_ref, hst_ref, g_ref, ar_ref, ai_ref, bblk_ref, cblk_ref, dsk_ref, tl)
    else:
        for s in range(S5_SLABS):
            sl = slice(s * LANES, (s + 1) * LANES)
            hbuf_ref[s, nb:nb + rows, :] = jnp.dot(ut_ref[s].astype(bf16), bblk_ref[s],
                                                   preferred_element_type=f32)
            _s5_scan_vmem(hbuf_ref, hst_ref, ar_ref, ai_ref, s, nb, rows)
            ys = jnp.dot(hbuf_ref[s, nb:nb + rows, :].astype(bf16), cblk_ref[s], preferred_element_type=f32)
            g_ref[s] = _gelu(ys + dsk_ref[:, sl] * ut_ref[s])

    g = jnp.concatenate([g_ref[s] for s in range(S5_SLABS)], axis=-1)
    gate = jnp.dot(g.astype(bf16), wglu_ref[...], preferred_element_type=f32) + bglu_ref[...]
    out = g * jax.nn.sigmoid(gate)
    if batch_major:
        for s in range(S5_SLABS):
            sl = slice(s * LANES, (s + 1) * LANES)
            ut_ref[s] = out[:, sl]
            for b in range(nb):
                y_ref[b, :, sl] = ut_ref[s, pl.ds(b, tl, stride=nb), :]
    else:
        y_ref[...] = out.reshape(tl, nb, D_S5)

    @pl.when(step == pl.num_programs(0) - 1)
    def _():
        for g in range(S5_GROUPS):
            s, o = g // gps, (g % gps) * S5_STATE
            ren_ref[:, g, :] = hst_ref[s, :, o:o + S5_STATE]
            imn_ref[:, g, :] = hst_ref[s, :, ns + o:ns + o + S5_STATE]


def _s5(u, re0, im0, ar, ai, bblk, cblk, dsk, wglu, bglu, *, tl, batch_major, cast=()):
    if batch_major:
        nb, L, _ = u.shape
        u_spec = pl.BlockSpec((nb, tl, D_S5), lambda i: (0, i, 0))
    else:
        L, nb, _ = u.shape
        u_spec = pl.BlockSpec((tl, nb, D_S5), lambda i: (i, 0, 0))
    rows = nb * tl
    steps = L // tl
    st_spec = _const_spec((nb, S5_GROUPS, S5_STATE))
    cast_specs = [pl.BlockSpec((w.shape[0] // steps, w.shape[1]), lambda i: (i, 0)) for w in cast]
    assert all(w.shape[0] % (steps * 2 * SUBLANES) == 0 for w in cast)
    return pl.pallas_call(
        functools.partial(_s5_kernel, nb=nb, tl=tl, batch_major=batch_major, n_cast=len(cast)),
        grid=(steps,),
        in_specs=[u_spec, st_spec, st_spec,
                  _const_spec((S5_SLABS, SUBLANES, S5_SLAB_STATE)),
                  _const_spec((S5_SLABS, SUBLANES, S5_SLAB_STATE)),
                  _const_spec((S5_SLABS, LANES, 2 * S5_SLAB_STATE)),
                  _const_spec((S5_SLABS, 2 * S5_SLAB_STATE, LANES)),
                  _const_spec((1, D_S5)), _const_spec((D_S5, D_S5)), _const_spec((1, D_S5))] + cast_specs,
        out_specs=[u_spec, st_spec, st_spec] + cast_specs,
        out_shape=[jax.ShapeDtypeStruct(u.shape, f32),
                   jax.ShapeDtypeStruct((nb, S5_GROUPS, S5_STATE), f32),
                   jax.ShapeDtypeStruct((nb, S5_GROUPS, S5_STATE), f32)]
                  + [jax.ShapeDtypeStruct(w.shape, bf16) for w in cast],
        scratch_shapes=[pltpu.VMEM((S5_SLABS, rows, LANES), f32),
                        pltpu.VMEM((S5_SLABS, nb + rows, 2 * S5_SLAB_STATE), f32),
                        pltpu.VMEM((S5_SLABS, nb, 2 * S5_SLAB_STATE), f32),
                        pltpu.VMEM((S5_SLABS, rows, LANES), f32)],
        compiler_params=_params("arbitrary"),
        name="s5_mixer",
    )(u, re0, im0, ar, ai, bblk, cblk, dsk, wglu, bglu, *cast)


def _ffn_stages(x, ys, y5, mods, n2g, nfg, wo_ref, wg_ref, wu_ref, wd_ref, write_out, ff_chunk):
    g1, sh2, sc2, g2, shf, scf = mods
    att = jnp.dot(ys.astype(bf16), wo_ref[:D_SSD, :], preferred_element_type=f32)
    att = att + jnp.dot(y5.astype(bf16), wo_ref[D_SSD:, :], preferred_element_type=f32)
    yield
    x1 = x + _by_row(lambda v, s: v * s, att, g1)
    v = _rms_mod(x1, n2g, sc2, sh2).astype(bf16)
    ff = None
    for o in range(0, D_FF, ff_chunk):
        gate = jnp.dot(v, wg_ref[:, o:o + ff_chunk], preferred_element_type=f32)
        up = jnp.dot(v, wu_ref[:, o:o + ff_chunk], preferred_element_type=f32)
        hid = (_silu(gate) * up).astype(bf16)
        part = jnp.dot(hid, wd_ref[o:o + ff_chunk, :], preferred_element_type=f32)
        ff = part if ff is None else ff + part
        yield
    x2 = x1 + _by_row(lambda v, s: v * s, ff, g2)
    write_out(_rms_mod(x2, nfg, scf, shf))


def _ffn_kernel(x_ref, ys_ref, y5_ref, mod_ref, modf_ref, n2g_ref, nfg_ref, wo_ref, wg_ref, wu_ref, wd_ref,
                o_ref, *, seq_major, ff_chunk, sub):
    mods = (_mod_row(mod_ref, 2, seq_major), _mod_row(mod_ref, 3, seq_major), _mod_row(mod_ref, 4, seq_major),
            _mod_row(mod_ref, 5, seq_major), _mod_row(modf_ref, 0, seq_major), _mod_row(modf_ref, 1, seq_major))
    x = _rows_in(x_ref, seq_major)
    rows = x.shape[0]
    tiles = {}

    def stages(r0):
        def write_out(y):
            if seq_major:
                tiles[r0] = y
            else:
                o_ref[0, r0:r0 + sub, :] = y
        return _ffn_stages(x[r0:r0 + sub], ys_ref[0, r0:r0 + sub, :], y5_ref[0, r0:r0 + sub, :], mods,
                           n2g_ref[...], nfg_ref[...], wo_ref, wg_ref, wu_ref, wd_ref, write_out, ff_chunk)

    _interleave(*[stages(r0) for r0 in range(0, rows, sub)])
    if seq_major:
        s, t, d = o_ref.shape
        y = jnp.concatenate([tiles[r0] for r0 in range(0, rows, sub)], axis=0)
        o_ref[...] = jnp.swapaxes(y.reshape(t, s, d), 0, 1)


def _ffn(x, ys, y5, mod, modf, n2g, nfg, wo, wg, wu, wd, *, tm, sub, seq_major, n_seq, first_seq, ff_chunk=256):
    if seq_major:
        x_spec = _const_spec(x.shape)
        nb, rows, d = 1, x.shape[0] * x.shape[1], x.shape[2]
        assert tm == rows
    else:
        nb, rows, d = x.shape
        x_spec = pl.BlockSpec((1, tm, d), lambda i, j: (i, j, 0))
    nt = rows // tm
    mod_spec = _mod_spec(mod, n_seq, first_seq)
    modf_spec = _mod_spec(modf, n_seq, first_seq)
    blk = lambda wd_: pl.BlockSpec((1, tm, wd_), lambda i, j: (i, j, 0))
    single = dict(pipeline_mode=pl.Buffered(1))
    wspec = lambda shape: pl.BlockSpec(shape, lambda i, j: (0, 0), **single)
    return pl.pallas_call(
        functools.partial(_ffn_kernel, seq_major=seq_major, ff_chunk=ff_chunk, sub=sub),
        grid=(nb, nt),
        in_specs=[x_spec, blk(D_SSD), blk(D_S5), mod_spec, modf_spec,
                  _const_spec((1, d)), _const_spec((1, d)),
                  wspec((d, d)), wspec((d, D_FF)), wspec((d, D_FF)), wspec((D_FF, d))],
        out_specs=x_spec,
        out_shape=jax.ShapeDtypeStruct(x.shape, f32),
        compiler_params=_params("parallel", "parallel"),
        name="out_ffn",
    )(x, ys, y5, mod, modf, n2g, nfg, wo, wg, wu, wd)


def kernel(x_prompt, x_sample, c_prompt, c_sample, state_ssd, state_conv, state_s5_re, state_s5_im, w_ada, b_ada, norm1_g, w_in, conv_w, conv_b, ssd_dt_bias, ssd_A_log, ssd_D, ssd_norm_g, s5_A_re, s5_A_im, s5_log_step, s5_B_re, s5_B_im, s5_C_re, s5_C_im, s5_D, w_glu, b_glu, w_out, norm2_g, w_ffn_gate, w_ffn_up, w_ffn_down, w_ada_f, b_ada_f, normf_g):
    assert w_ada.shape[0] == 1, "single-layer stack"
    bp, seq, d = x_prompt.shape
    bs, dseq, _ = x_sample.shape

    c_all = jnp.concatenate([c_sample, c_prompt], axis=0)
    mod, modf = _ada_mod(c_all, (w_ada[0], w_ada_f), (b_ada[0], b_ada_f))
    seqs_s = dict(n_seq=bs, first_seq=0)
    seqs_p = dict(n_seq=bp, first_seq=bs)

    w_in_p = w_in[0].T
    pad_h = lambda v: jnp.concatenate([v, jnp.zeros((DT_PAD - HEADS,), f32)]).reshape(1, DT_PAD)
    dtb = pad_h(ssd_dt_bias[0])
    alog = pad_h(ssd_A_log[0])
    dexp = jnp.repeat(ssd_D[0], HEAD_DIM).reshape(1, D_SSD)
    ng = ssd_norm_g[0].reshape(1, D_SSD)
    cw, cb = conv_w[0], conv_b[0].reshape(1, CONV_DIM)

    ar, ai, bblk, cblk = _s5_params(s5_A_re[0], s5_A_im[0], s5_log_step[0],
                                    s5_B_re[0].transpose(0, 2, 1), s5_B_im[0].transpose(0, 2, 1),
                                    s5_C_re[0], s5_C_im[0])
    dsk = s5_D[0].reshape(1, D_S5)
    wglu = w_glu[0].astype(bf16)
    bglu = b_glu[0].reshape(1, D_S5)
    n1g, n2g, nfg = norm1_g[0].reshape(1, d), norm2_g[0].reshape(1, d), normf_g.reshape(1, d)

    z, act, u5, dt, conv_p = _inproj(x_prompt, mod, n1g, w_in_p, (cw, cb), tm=1024, sub=512, seq_major=False,
                                     **seqs_p)
    y_ssd, ssd_p = _ssd_prompt(act, z, dt, dtb, alog, dexp, ng)
    zeros_st = jnp.zeros((bp, S5_GROUPS, S5_STATE), f32)
    y_s5, re_p, im_p, wo, wg, wu, wd = _s5(
        u5, zeros_st, zeros_st, ar, ai, bblk, cblk, dsk, wglu, bglu, tl=128, batch_major=True,
        cast=(w_out[0], w_ffn_gate[0], w_ffn_up[0], w_ffn_down[0]))
    y_prompt = _ffn(x_prompt, y_ssd, y_s5, mod, modf, n2g, nfg, wo, wg, wu, wd, tm=1024, sub=512,
                    seq_major=False, **seqs_p)

    rows_s = dseq * bs
    steps = lambda a: a.reshape(dseq, bs, a.shape[-1])
    flat = lambda a: a.reshape(1, rows_s, a.shape[-1])
    z, xbc, u5, dt = _inproj(x_sample, mod, n1g, w_in_p, tm=rows_s, sub=rows_s // 2, seq_major=True, **seqs_s)
    y_ssd, ssd_s, conv_s = _ssd_sample(steps(xbc), steps(z), steps(dt), state_conv[0].transpose(1, 0, 2),
                                       state_ssd[0], cw, cb, dtb, alog, dexp, ng)
    y_s5, re_s, im_s = _s5(steps(u5), state_s5_re[0], state_s5_im[0],
                           ar, ai, bblk, cblk, dsk, wglu, bglu, tl=dseq, batch_major=False)
    y_sample = _ffn(x_sample, flat(y_ssd), flat(y_s5), mod, modf, n2g, nfg, wo, wg, wu, wd,
                    tm=rows_s, sub=rows_s, seq_major=True, **seqs_s)

    return (y_prompt, y_sample,
            ssd_p[None], ssd_s.reshape((1,) + state_ssd.shape[1:]),
            conv_p[None], conv_s.transpose(1, 0, 2)[None],
            re_p[None], re_s[None], im_p[None], im_s[None])
```

```python
import functools

import jax
import jax.numpy as jnp
from jax import lax
from jax.experimental import pallas as pl
from jax.experimental.pallas import tpu as pltpu

f32 = jnp.float32
bf16 = jnp.bfloat16

D_MODEL = 1024
D_SSD = 512
HEAD_DIM = 64
HEADS = 8
GROUPS = 2
HEADS_PER_GROUP = HEADS // GROUPS
STATE = 128
CONV_W = 4
CONV_DIM = D_SSD + 2 * GROUPS * STATE
D_S5 = 512
S5_CH = 16
S5_GROUPS = 32
S5_STATE = 64
D_FF = 2816
N_ADA = 6
EPS = 1e-6

LANES = 128
SUBLANES = 8
SSD_CHUNK = 128
S5_SLABS = D_S5 // LANES
S5_SLAB_STATE = (S5_GROUPS // S5_SLABS) * S5_STATE
DT_PAD = LANES
VMEM_LIMIT = 56 * 1024 * 1024

NT_DIMS = (((1,), (1,)), ((), ()))
TN_DIMS = (((0,), (0,)), ((), ()))


def _silu(x):
    return x * jax.nn.sigmoid(x)


def _interleave(*gens, lead=0):
    pending, live, rnd = list(gens), [], 0
    while pending or live:
        while pending and rnd >= lead * (len(gens) - len(pending)):
            live.append(pending.pop(0))
        for g in list(live):
            try:
                next(g)
            except StopIteration:
                live.remove(g)
        rnd += 1


def _by_row(fn, v, *ms):
    r, d = v.shape
    m_rows = ms[0].shape[0]
    if m_rows in (1, r):
        return fn(v, *ms)
    out = fn(v.reshape(r // m_rows, m_rows, d), *[m[None] for m in ms])
    return out.reshape(r, d)


def _rms_mod(x, g, sc, sh):
    y = x * lax.rsqrt(jnp.mean(x * x, axis=-1, keepdims=True) + EPS)
    return _by_row(lambda v, s, t: v * (1.0 + s) + t, y * g, sc, sh)


def _mod_row(mod_ref, i, per_row):
    return mod_ref[i] if per_row else mod_ref[i, pl.ds(pl.program_id(0), 1), :]


def _mod_spec(mod, n_seq, first_seq):
    assert first_seq % n_seq == 0
    return pl.BlockSpec((mod.shape[0], n_seq, mod.shape[2]), lambda *_: (0, first_seq // n_seq, 0))


def _const_spec(shape):
    nd = len(shape)
    return pl.BlockSpec(shape, lambda *_: (0,) * nd)


def _params(*sem):
    return pltpu.CompilerParams(dimension_semantics=sem, vmem_limit_bytes=VMEM_LIMIT)


ADA_K_BLOCK = 256


def _ada_kernel(ca_ref, cb_ref, *refs):
    nw = len(refs) // 3
    s = _silu(jnp.concatenate([ca_ref[...], cb_ref[...]], axis=0)).astype(bf16)
    for w_ref, b_ref, o_ref in zip(refs[:nw], refs[nw:2 * nw], refs[2 * nw:]):
        n, _, d = o_ref.shape

        @pl.when(pl.program_id(0) == 0)
        def _():
            for j in range(n):
                o_ref[j] = jnp.broadcast_to(b_ref[:, j * d:(j + 1) * d], o_ref.shape[1:])

        for j in range(n):
            o_ref[j] += jnp.dot(s, w_ref[:, j * d:(j + 1) * d].astype(bf16), preferred_element_type=f32)


def _ada_mod(ca, cb, ws, bs):
    k = ca.shape[1]
    m = ca.shape[0] + cb.shape[0]
    ns = [w.shape[1] // k for w in ws]
    return pl.pallas_call(
        _ada_kernel,
        grid=(k // ADA_K_BLOCK,),
        in_specs=[pl.BlockSpec((c.shape[0], ADA_K_BLOCK), lambda i: (0, i)) for c in (ca, cb)]
                 + [pl.BlockSpec((ADA_K_BLOCK, n * k), lambda i: (i, 0)) for n in ns]
                 + [_const_spec((1, n * k)) for n in ns],
        out_specs=[_const_spec((n, m, k)) for n in ns],
        out_shape=[jax.ShapeDtypeStruct((n, m, k), f32) for n in ns],
        compiler_params=_params("arbitrary"),
        name="ada_mod",
    )(ca, cb, *ws, *[b.reshape(1, n * k) for b, n in zip(bs, ns)])


def _rows_in(x_ref, seq_major):
    if not seq_major:
        return x_ref[0]
    s, t, d = x_ref.shape
    return jnp.swapaxes(x_ref[...], 0, 1).reshape(s * t, d)


def _conv_silu(ext, cw_ref, cb_ref):
    w = [cw_ref[k:k + 1, :] for k in range(CONV_W)]
    back1 = pltpu.roll(ext, 1, 0)
    older = pltpu.roll(ext * w[1] + back1 * w[0], 2, 0)
    conv = cb_ref[...] + ext[SUBLANES:, :] * w[3] + back1[SUBLANES:, :] * w[2] + older[SUBLANES:, :]
    return _silu(conv)


def _inproj_stages(x, sh, sc, g, outs, after=()):
    u = _rms_mod(x, g, sc, sh).astype(bf16)
    yield
    for write, w16 in outs:
        write(lax.dot_general(u, w16, NT_DIMS, preferred_element_type=f32))
        yield
    for stage in after:
        stage()
        yield


def _inproj_kernel(x_ref, mod_ref, g_ref, w_ref, *rest, seq_major, sub, conv):
    if conv:
        cw_ref, cb_ref, z_ref, act_ref, u5_ref, dt_ref, cn_ref, ext_ref = rest

        @pl.when(pl.program_id(1) == 0)
        def _():
            ext_ref[0:SUBLANES, :] = jnp.zeros((SUBLANES, CONV_DIM), f32)
    else:
        z_ref, xbc_ref, u5_ref, dt_ref = rest
    sh = _mod_row(mod_ref, 0, seq_major)
    sc = _mod_row(mod_ref, 1, seq_major)
    x = _rows_in(x_ref, seq_major)
    rows = x.shape[0]
    T = SSD_CHUNK
    o_dt = D_SSD + CONV_DIM
    w16 = lambda o, width: w_ref[o:o + width, :].astype(bf16)
    w_z, w_xbc, w_u5, w_dt = w16(0, D_SSD), w16(D_SSD, CONV_DIM), w16(o_dt + HEADS, D_S5), w16(o_dt, DT_PAD)

    def stages(r0):
        def to(ref):
            def write(v):
                ref[0, r0:r0 + sub, :] = v
            return write

        def to_ext(v):
            ext_ref[SUBLANES + r0:SUBLANES + r0 + sub, :] = v

        def conv_chunk(c0):
            def stage():
                act_ref[0, c0:c0 + T, :] = _conv_silu(ext_ref[c0:c0 + SUBLANES + T, :], cw_ref, cb_ref)
            return stage

        outs = [(to(z_ref), w_z), (to_ext if conv else to(xbc_ref), w_xbc), (to(u5_ref), w_u5), (to(dt_ref), w_dt)]
        after = [conv_chunk(c0) for c0 in range(r0, r0 + sub, T)] if conv else ()
        return _inproj_stages(x[r0:r0 + sub], sh, sc, g_ref[...], outs, after)

    n_matmul_stages = 5
    _interleave(*[stages(r0) for r0 in range(0, rows, sub)], lead=n_matmul_stages if conv else 0)
    if conv:
        cn_ref[0] = ext_ref[SUBLANES + rows - (CONV_W - 1):SUBLANES + rows, :]
        ext_ref[0:SUBLANES, :] = ext_ref[rows:rows + SUBLANES, :]


def _inproj(x, mod, g, w, conv_wb=None, *, tm, sub, seq_major, n_seq, first_seq):
    if seq_major:
        x_spec = _const_spec(x.shape)
        nb, rows, d = 1, x.shape[0] * x.shape[1], x.shape[2]
        assert tm == rows
    else:
        nb, rows, d = x.shape
        x_spec = pl.BlockSpec((1, tm, d), lambda i, j: (i, j, 0))
    nt = rows // tm
    widths = (D_SSD, CONV_DIM, D_S5, DT_PAD)
    conv = conv_wb is not None
    in_specs = [x_spec, _mod_spec(mod, n_seq, first_seq), _const_spec((1, d)),
                pl.BlockSpec(w.shape, lambda i, j: (0, 0), pipeline_mode=pl.Buffered(1))]
    out_specs = [pl.BlockSpec((1, tm, wd), lambda i, j: (i, j, 0)) for wd in widths]
    out_shape = [jax.ShapeDtypeStruct((nb, rows, wd), f32) for wd in widths]
    scratch = []
    if conv:
        in_specs += [_const_spec((CONV_W, CONV_DIM)), _const_spec((1, CONV_DIM))]
        out_specs.append(pl.BlockSpec((1, CONV_W - 1, CONV_DIM), lambda i, j: (i, 0, 0)))
        out_shape.append(jax.ShapeDtypeStruct((nb, CONV_W - 1, CONV_DIM), f32))
        scratch.append(pltpu.VMEM((SUBLANES + tm, CONV_DIM), f32))
    return pl.pallas_call(
        functools.partial(_inproj_kernel, seq_major=seq_major, sub=sub, conv=conv),
        grid=(nb, nt),
        in_specs=in_specs, out_specs=out_specs, out_shape=out_shape, scratch_shapes=scratch,
        compiler_params=_params("parallel", "arbitrary" if conv else "parallel"),
        name="in_proj",
    )(x, mod, g, w, *(conv_wb or ()))


def _split3(x):
    hi = x.astype(bf16)
    r1 = x - hi.astype(f32)
    mid = r1.astype(bf16)
    lo = (r1 - mid.astype(f32)).astype(bf16)
    return hi, mid, lo


def _dot_sel_lhs(sel16, x):
    return sum(jnp.dot(sel16, p, preferred_element_type=f32) for p in _split3(x))


def _dot_sel_rhs(x, sel16):
    hi, mid, _ = _split3(x)
    return jnp.dot(hi, sel16, preferred_element_type=f32) + jnp.dot(mid, sel16, preferred_element_type=f32)


def _gated_group_norm(y, z, ng):
    y = y * _silu(z)
    gw = D_SSD // GROUPS
    parts = []
    for g in range(GROUPS):
        yg = y[:, g * gw:(g + 1) * gw]
        parts.append(yg * lax.rsqrt(jnp.mean(yg * yg, axis=-1, keepdims=True) + EPS))
    return jnp.concatenate(parts, axis=-1) * ng


def _ssd_prompt_kernel(act_ref, z_ref, dt_ref, dtb_ref, alog_ref, dexp_ref, ng_ref, y_ref, st_ref, h_ref, *, cps):
    T = SSD_CHUNK
    gw = HEADS_PER_GROUP * HEAD_DIM

    @pl.when(pl.program_id(1) == 0)
    def _():
        h_ref[...] = jnp.zeros_like(h_ref)

    row = lax.broadcasted_iota(jnp.int32, (T, T), 0)
    col = lax.broadcasted_iota(jnp.int32, (T, T), 1)
    tri = row >= col
    tri16 = jnp.where(tri, 1.0, 0.0).astype(bf16)
    low_half = lax.broadcasted_iota(jnp.int32, (T, LANES), 1) < HEAD_DIM
    a_neg = -jnp.exp(alog_ref[...])

    lane_head = lax.broadcasted_iota(jnp.int32, (DT_PAD, D_SSD), 1) // HEAD_DIM
    sel16 = jnp.where(lax.broadcasted_iota(jnp.int32, (DT_PAD, D_SSD), 0) == lane_head, 1.0, 0.0).astype(bf16)

    def spread(q):
        return _dot_sel_rhs(q, sel16)

    def state_free(ci):
        r0 = ci * T
        act = act_ref[0, r0:r0 + T, :]
        xs = act[:, :D_SSD]
        bm = act[:, D_SSD:D_SSD + GROUPS * STATE]
        cm = act[:, D_SSD + GROUPS * STATE:]
        dtv = jax.nn.softplus(dt_ref[0, r0:r0 + T, :] + dtb_ref[...])
        a_cs = _dot_sel_lhs(tri16, dtv * a_neg)
        bg16s = [bm[:, g * STATE:(g + 1) * STATE].astype(bf16) for g in range(GROUPS)]
        cg16s = [cm[:, g * STATE:(g + 1) * STATE].astype(bf16) for g in range(GROUPS)]
        gmats = [lax.dot_general(cg16s[g], bg16s[g], NT_DIMS, preferred_element_type=f32)
                 for g in range(GROUPS)]
        yield
        a_last = a_cs[T - 1:T, :]
        a_cs_t = a_cs.T
        rep = spread(jnp.concatenate([dtv, jnp.exp(a_last - a_cs), jnp.exp(a_cs)], axis=0))
        yield
        x = xs * rep[:T]
        x16 = x.astype(bf16)
        xd16 = (x * rep[T:2 * T]).astype(bf16)
        e_cs = rep[2 * T:]
        yield
        y_part = []
        for g in range(GROUPS):
            for pr in range(HEADS_PER_GROUP // 2):
                s_pair = []
                for q in range(2):
                    h = g * HEADS_PER_GROUP + 2 * pr + q
                    seg = a_cs[:, h:h + 1] - a_cs_t[h:h + 1, :]
                    lmat = jnp.exp(jnp.where(tri, seg, -jnp.inf))
                    s_pair.append((gmats[g] * lmat).astype(bf16))
                lanes = slice(g * gw + pr * LANES, g * gw + (pr + 1) * LANES)
                both = jnp.dot(jnp.concatenate(s_pair, axis=0), x16[:, lanes], preferred_element_type=f32)
                yield
                y_part.append(jnp.where(low_half, both[:T], both[T:]))
        y_free = jnp.concatenate(y_part, axis=-1) + xs * dexp_ref[...]
        return y_free, e_cs, jnp.exp(a_last), xd16, bg16s, cg16s

    def state_step(ci, y_free, e_cs, e_last, xd16, bg16s, cg16s):
        r0 = ci * T
        y_offs = []
        for g in range(GROUPS):
            hp = h_ref[g]
            y_offs.append(lax.dot_general(cg16s[g], hp.astype(bf16), NT_DIMS, preferred_element_type=f32))
            upd = lax.dot_general(xd16[:, g * gw:(g + 1) * gw], bg16s[g], TN_DIMS, preferred_element_type=f32)
            for hh in range(HEADS_PER_GROUP):
                h = g * HEADS_PER_GROUP + hh
                rows = slice(hh * HEAD_DIM, (hh + 1) * HEAD_DIM)
                h_ref[g, rows, :] = e_last[:, h:h + 1] * hp[rows] + upd[rows]
        y = y_free + jnp.concatenate(y_offs, axis=-1) * e_cs
        y_ref[0, r0:r0 + T, :] = _gated_group_norm(y, z_ref[0, r0:r0 + T, :], ng_ref[...])

    gens = [state_free(ci) for ci in range(cps)]
    free = [None] * cps
    while any(f is None for f in free):
        for ci in range(cps):
            if free[ci] is None:
                try:
                    next(gens[ci])
                except StopIteration as done:
                    free[ci] = done.value
    for ci in range(cps):
        state_step(ci, *free[ci])

    @pl.when(pl.program_id(1) == pl.num_programs(1) - 1)
    def _():
        st_ref[0] = h_ref[...].reshape(HEADS, HEAD_DIM, STATE)


def _ssd_prompt(act, z, dt, dtb, alog, dexp, ng, *, cps=8):
    nb, L, _ = act.shape
    rows = cps * SSD_CHUNK
    blk = lambda wd: pl.BlockSpec((1, rows, wd), lambda b, c: (b, c, 0))
    return pl.pallas_call(
        functools.partial(_ssd_prompt_kernel, cps=cps),
        grid=(nb, L // rows),
        in_specs=[blk(CONV_DIM), blk(D_SSD), blk(DT_PAD), _const_spec((1, DT_PAD)), _const_spec((1, DT_PAD)),
                  _const_spec((1, D_SSD)), _const_spec((1, D_SSD))],
        out_specs=[blk(D_SSD), pl.BlockSpec((1, HEADS, HEAD_DIM, STATE), lambda b, c: (b, 0, 0, 0))],
        out_shape=[jax.ShapeDtypeStruct((nb, L, D_SSD), f32),
                   jax.ShapeDtypeStruct((nb, HEADS, HEAD_DIM, STATE), f32)],
        scratch_shapes=[pltpu.VMEM((GROUPS, HEADS_PER_GROUP * HEAD_DIM, STATE), f32)],
        compiler_params=_params("parallel", "arbitrary"),
        name="ssd_prompt",
    )(act, z, dt, dtb, alog, dexp, ng)


def _ssd_sample_kernel(xbc_ref, z_ref, dt_ref, cbuf_ref, st_ref, cw_ref, cb_ref, dtb_ref, alog_ref,
                       dexp_ref, ng_ref, y_ref, stn_ref, cn_ref, dtot_ref, *, L, nb):
    gw = HEADS_PER_GROUP * HEAD_DIM
    full = [cbuf_ref[j] for j in range(CONV_W - 1)] + [xbc_ref[t] for t in range(L)]
    for j in range(CONV_W - 1):
        cn_ref[j] = full[L + j]
    a_neg = -jnp.exp(alog_ref[...])
    xs, bm, cm, dtv, acs = [], [], [], [], []
    run = None
    for t in range(L):
        conv = cb_ref[...]
        for k in range(CONV_W):
            conv = conv + full[t + k] * cw_ref[k:k + 1, :]
        act = _silu(conv)
        xs.append(act[:, :D_SSD])
        bm.append(act[:, D_SSD:D_SSD + GROUPS * STATE])
        cm.append(act[:, D_SSD + GROUPS * STATE:])
        d = jax.nn.softplus(dt_ref[t] + dtb_ref[...])
        dtv.append(d)
        run = d * a_neg if run is None else run + d * a_neg
        acs.append(run)
    a_tot = acs[L - 1]
    dtot_ref[...] = jnp.exp(a_tot)

    lane_head = lax.broadcasted_iota(jnp.int32, (DT_PAD, D_SSD), 1) // HEAD_DIM
    sel16 = jnp.where(lax.broadcasted_iota(jnp.int32, (DT_PAD, D_SSD), 0) == lane_head, 1.0, 0.0).astype(bf16)
    pairs = [(t, s) for t in range(L) for s in range(t)]
    factors = (dtv + [jnp.exp(a_tot - acs[t]) for t in range(L)] + [jnp.exp(acs[t]) for t in range(L)]
               + [jnp.exp(acs[t] - acs[s]) for t, s in pairs])
    rep = _dot_sel_rhs(jnp.concatenate(factors, axis=0), sel16)
    piece = lambda i: rep[i * nb:(i + 1) * nb]
    x = [xs[t] * piece(t) for t in range(L)]
    xd_stack = jnp.concatenate([x[t] * piece(L + t) for t in range(L)], axis=0)
    e_cs = [piece(2 * L + t) for t in range(L)]
    decay = {ts: piece(3 * L + i) for i, ts in enumerate(pairs)}

    in_group0 = lax.broadcasted_iota(jnp.int32, (nb, D_SSD), 1) < gw
    y_intra = []
    for t in range(L):
        acc = None
        for s in range(t + 1):
            cb_dot = [jnp.sum(cm[t][:, g * STATE:(g + 1) * STATE] * bm[s][:, g * STATE:(g + 1) * STATE],
                              axis=-1, keepdims=True) for g in range(GROUPS)]
            w = jnp.where(in_group0, cb_dot[0], cb_dot[1])
            term = w * x[s] if s == t else w * decay[(t, s)] * x[s]
            acc = term if acc is None else acc + term
        y_intra.append(acc)

    c_stack = [jnp.concatenate([cm[t][:, g * STATE:(g + 1) * STATE] for t in range(L)], axis=0).astype(bf16)
               for g in range(GROUPS)]
    b_stack = [jnp.concatenate([bm[t][:, g * STATE:(g + 1) * STATE] for t in range(L)], axis=0).astype(bf16)
               for g in range(GROUPS)]
    seq_of_row = lax.broadcasted_iota(jnp.int32, (L * nb, gw), 0) & (nb - 1)

    def per_seq(b, yoff):
        mine = seq_of_row == b
        drow = dtot_ref[pl.ds(b, 1), :]
        out = []
        for g in range(GROUPS):
            h0 = st_ref[b, g]
            r = lax.dot_general(c_stack[g], h0.astype(bf16), NT_DIMS, preferred_element_type=f32)
            out.append(yoff[g] + jnp.where(mine, r, 0.0))
            xm = jnp.where(mine, xd_stack[:, g * gw:(g + 1) * gw], 0.0).astype(bf16)
            upd = lax.dot_general(xm, b_stack[g], TN_DIMS, preferred_element_type=f32)
            for hh in range(HEADS_PER_GROUP):
                h = g * HEADS_PER_GROUP + hh
                rows = slice(hh * HEAD_DIM, (hh + 1) * HEAD_DIM)
                stn_ref[b, g, rows, :] = drow[:, h:h + 1] * h0[rows] + upd[rows]
        return tuple(out)

    yoff = lax.fori_loop(0, nb, per_seq, tuple(jnp.zeros((L * nb, gw), f32) for _ in range(GROUPS)),
                         unroll=8)

    for t in range(L):
        y_off = jnp.concatenate([yoff[g][t * nb:(t + 1) * nb] for g in range(GROUPS)], axis=-1)
        y = y_intra[t] + y_off * e_cs[t] + xs[t] * dexp_ref[...]
        y_ref[t] = _gated_group_norm(y, z_ref[t], ng_ref[...])


def _ssd_sample(xbc, z, dt, cbuf, st, cw, cb, dtb, alog, dexp, ng, *, nb=16):
    L, B, _ = xbc.shape
    assert nb & (nb - 1) == 0 and B % nb == 0
    tblk = lambda n, wd: pl.BlockSpec((n, nb, wd), lambda i: (0, i, 0))
    gw = HEADS_PER_GROUP * HEAD_DIM
    st = st.reshape(B, GROUPS, gw, STATE)
    st_spec = pl.BlockSpec((nb, GROUPS, gw, STATE), lambda i: (i, 0, 0, 0))
    return pl.pallas_call(
        functools.partial(_ssd_sample_kernel, L=L, nb=nb),
        grid=(B // nb,),
        in_specs=[tblk(L, CONV_DIM), tblk(L, D_SSD), tblk(L, DT_PAD), tblk(CONV_W - 1, CONV_DIM), st_spec,
                  _const_spec((CONV_W, CONV_DIM)), _const_spec((1, CONV_DIM)),
                  _const_spec((1, DT_PAD)), _const_spec((1, DT_PAD)),
                  _const_spec((1, D_SSD)), _const_spec((1, D_SSD))],
        out_specs=[tblk(L, D_SSD), st_spec, tblk(CONV_W - 1, CONV_DIM)],
        out_shape=[jax.ShapeDtypeStruct((L, B, D_SSD), f32),
                   jax.ShapeDtypeStruct((B, GROUPS, gw, STATE), f32),
                   jax.ShapeDtypeStruct((CONV_W - 1, B, CONV_DIM), f32)],
        scratch_shapes=[pltpu.VMEM((nb, DT_PAD), f32)],
        compiler_params=_params("parallel"),
        name="ssd_sample",
    )(xbc, z, dt, cbuf, st, cw, cb, dtb, alog, dexp, ng)


def _s5_param_kernel(lr_ref, li_ref, ls_ref, br_ref, bi_ref, cr_ref, ci_ref,
                     ar_ref, ai_ref, bblk_ref, cblk_ref, b_scr, c_scr):
    lr, li = lr_ref[...], li_ref[...]
    step = jnp.exp(ls_ref[...])
    mag = jnp.exp(lr * step)
    abr = mag * jnp.cos(li * step)
    abi = mag * jnp.sin(li * step)
    nr, ni = abr - 1.0, abi
    den = lr * lr + li * li
    fr = (nr * lr + ni * li) / den
    fi = (ni * lr - nr * li) / den
    br, bi = br_ref[...], bi_ref[...]
    bbr = fr * br - fi * bi
    bbi = fr * bi + fi * br
    b_scr[...] = jnp.zeros_like(b_scr)
    c_scr[...] = jnp.zeros_like(c_scr)
    gps = S5_GROUPS // S5_SLABS
    ns = S5_SLAB_STATE
    for g in range(S5_GROUPS):
        s, gl = divmod(g, gps)
        ch = slice(gl * S5_CH, (gl + 1) * S5_CH)
        st = slice(gl * S5_STATE, (gl + 1) * S5_STATE)
        st_im = slice(ns + gl * S5_STATE, ns + (gl + 1) * S5_STATE)
        ar_ref[s, :, st] = jnp.broadcast_to(abr[g], (SUBLANES, S5_STATE))
        ai_ref[s, :, st] = jnp.broadcast_to(abi[g], (SUBLANES, S5_STATE))
        b_scr[s, ch, st] = bbr[g]
        b_scr[s, ch, st_im] = bbi[g]
        c_scr[s, st, ch] = cr_ref[g].T
        c_scr[s, st_im, ch] = -ci_ref[g].T
    bblk_ref[...] = b_scr[...].astype(bf16)
    cblk_ref[...] = c_scr[...].astype(bf16)


def _s5_params(a_re, a_im, log_step, b_re_t, b_im_t, c_re, c_im):
    g, p = a_re.shape
    ns = S5_SLAB_STATE
    return pl.pallas_call(
        _s5_param_kernel,
        out_shape=[jax.ShapeDtypeStruct((S5_SLABS, SUBLANES, ns), f32),
                   jax.ShapeDtypeStruct((S5_SLABS, SUBLANES, ns), f32),
                   jax.ShapeDtypeStruct((S5_SLABS, LANES, 2 * ns), bf16),
                   jax.ShapeDtypeStruct((S5_SLABS, 2 * ns, LANES), bf16)],
        scratch_shapes=[pltpu.VMEM((S5_SLABS, LANES, 2 * ns), f32),
                        pltpu.VMEM((S5_SLABS, 2 * ns, LANES), f32)],
        name="s5_params",
    )(a_re.reshape(g, 1, p), a_im.reshape(g, 1, p), log_step.reshape(g, 1, 1), b_re_t, b_im_t, c_re, c_im)


def _gelu(x):
    return 0.5 * x * (1.0 + lax.erf(x * (2.0 ** -0.5)))


def _s5_slabs_interleaved(ut_ref, hbuf_ref, hst_ref, g_ref, ar_ref, ai_ref, bblk_ref, cblk_ref, dsk_ref, tl,
                          row_block=256, n_stages=8):
    ns = S5_SLAB_STATE
    nb = SUBLANES
    rows = nb * tl

    def lanes(s):
        return slice(s * LANES, (s + 1) * LANES)

    def bu_stage(slabs):
        for s in slabs:
            for r in range(0, rows, row_block):
                us = ut_ref[s, r:r + row_block, :]
                hbuf_ref[s, nb + r:nb + r + row_block, :] = jnp.dot(us.astype(bf16), bblk_ref[s],
                                                                    preferred_element_type=f32)
                yield

    def scan_stage(slabs):
        carry = [(hst_ref[s, :, :ns], hst_ref[s, :, ns:]) for s in slabs]
        coef = [(ar_ref[s], ai_ref[s]) for s in slabs]
        for t in range(tl):
            r = nb + nb * t
            for i, s in enumerate(slabs):
                (pr, pi), (ar, ai) = carry[i], coef[i]
                nr = ar * pr - ai * pi + hbuf_ref[s, r:r + nb, :ns]
                ni = ar * pi + ai * pr + hbuf_ref[s, r:r + nb, ns:]
                hbuf_ref[s, r:r + nb, :ns] = nr
                hbuf_ref[s, r:r + nb, ns:] = ni
                carry[i] = (nr, ni)
            if (t + 1) % (tl // n_stages) == 0:
                yield
        for i, s in enumerate(slabs):
            hst_ref[s, :, :ns] = carry[i][0]
            hst_ref[s, :, ns:] = carry[i][1]

    def y_stage(slabs):
        for s in slabs:
            for r in range(0, rows, row_block):
                h16 = hbuf_ref[s, nb + r:nb + r + row_block, :].astype(bf16)
                ys = jnp.dot(h16, cblk_ref[s], preferred_element_type=f32)
                ys = ys + dsk_ref[:, lanes(s)] * ut_ref[s, r:r + row_block, :]
                g_ref[s, r:r + row_block, :] = _gelu(ys)
                yield

    half = S5_SLABS // 2
    first, second = tuple(range(half)), tuple(range(half, S5_SLABS))
    _interleave(bu_stage(first))
    _interleave(scan_stage(first), bu_stage(second))
    _interleave(scan_stage(second), y_stage(first))
    _interleave(y_stage(second))


def _s5_scan_vmem(hbuf_ref, hst_ref, ar_ref, ai_ref, s, nb, rows):
    ns = S5_SLAB_STATE
    hbuf_ref[s, 0:nb, :] = hst_ref[s]
    ar, ai = ar_ref[s], ai_ref[s]

    def body(i, carry):
        j = pl.multiple_of(nb + i * SUBLANES, SUBLANES)
        prev = hbuf_ref[s, pl.ds(j - nb, SUBLANES), :]
        cur = hbuf_ref[s, pl.ds(j, SUBLANES), :]
        pr, pi = prev[:, :ns], prev[:, ns:]
        hbuf_ref[s, pl.ds(j, SUBLANES), :ns] = ar * pr - ai * pi + cur[:, :ns]
        hbuf_ref[s, pl.ds(j, SUBLANES), ns:] = ar * pi + ai * pr + cur[:, ns:]
        return carry

    lax.fori_loop(0, rows // SUBLANES, body, 0, unroll=min(8, nb // SUBLANES))
    hst_ref[s] = hbuf_ref[s, rows:rows + nb, :]


S5_N_IN = 10


def _s5_kernel(*refs, nb, tl, batch_major, n_cast):
    (u_ref, re0_ref, im0_ref, ar_ref, ai_ref, bblk_ref, cblk_ref, dsk_ref, wglu_ref,
     bglu_ref) = refs[:S5_N_IN]
    cast_in = refs[S5_N_IN:S5_N_IN + n_cast]
    y_ref, ren_ref, imn_ref = refs[S5_N_IN + n_cast:S5_N_IN + n_cast + 3]
    cast_out = refs[S5_N_IN + n_cast + 3:S5_N_IN + 2 * n_cast + 3]
    ut_ref, hbuf_ref, hst_ref, g_ref = refs[S5_N_IN + 2 * n_cast + 3:]
    for w_ref, w16_ref in zip(cast_in, cast_out):
        w16_ref[...] = w_ref[...].astype(bf16)
    step = pl.program_id(0)
    rows = nb * tl
    ns = S5_SLAB_STATE

    @pl.when(step == 0)
    def _():
        for s in range(S5_SLABS):
            hst_ref[s, :, :ns] = re0_ref[:, s * ns:(s + 1) * ns]
            hst_ref[s, :, ns:] = im0_ref[:, s * ns:(s + 1) * ns]

    for s in range(S5_SLABS):
        sl = slice(s * LANES, (s + 1) * LANES)
        if batch_major:
            for b in range(nb):
                ut_ref[s, pl.ds(b, tl, stride=nb), :] = u_ref[b, :, sl]
        else:
            ut_ref[s] = u_ref[:, :, sl].reshape(rows, LANES)

    if nb == SUBLANES:
        _s5_slabs_interleaved(ut_ref, hbuf_ref, hst_ref, g_ref, ar_ref, ai_ref, bblk_ref, cblk_ref, dsk_ref, tl)
    else:
        for s in range(S5_SLABS):
            sl = slice(s * LANES, (s + 1) * LANES)
            hbuf_ref[s, nb:nb + rows, :] = jnp.dot(ut_ref[s].astype(bf16), bblk_ref[s],
                                                   preferred_element_type=f32)
            _s5_scan_vmem(hbuf_ref, hst_ref, ar_ref, ai_ref, s, nb, rows)
            ys = jnp.dot(hbuf_ref[s, nb:nb + rows, :].astype(bf16), cblk_ref[s], preferred_element_type=f32)
            g_ref[s] = _gelu(ys + dsk_ref[:, sl] * ut_ref[s])

    g = jnp.concatenate([g_ref[s] for s in range(S5_SLABS)], axis=-1)
    gate = jnp.dot(g.astype(bf16), wglu_ref[...], preferred_element_type=f32) + bglu_ref[...]
    out = g * jax.nn.sigmoid(gate)
    if batch_major:
        for s in range(S5_SLABS):
            sl = slice(s * LANES, (s + 1) * LANES)
            ut_ref[s] = out[:, sl]
            for b in range(nb):
                y_ref[b, :, sl] = ut_ref[s, pl.ds(b, tl, stride=nb), :]
    else:
        y_ref[...] = out.reshape(tl, nb, D_S5)

    @pl.when(step == pl.num_programs(0) - 1)
    def _():
        for s in range(S5_SLABS):
            ren_ref[:, s * ns:(s + 1) * ns] = hst_ref[s, :, :ns]
            imn_ref[:, s * ns:(s + 1) * ns] = hst_ref[s, :, ns:]


def _s5(u, re0, im0, ar, ai, bblk, cblk, dsk, wglu, bglu, *, tl, batch_major, cast=()):
    if batch_major:
        nb, L, _ = u.shape
        u_spec = pl.BlockSpec((nb, tl, D_S5), lambda i: (0, i, 0))
    else:
        L, nb, _ = u.shape
        u_spec = pl.BlockSpec((tl, nb, D_S5), lambda i: (i, 0, 0))
    rows = nb * tl
    steps = L // tl
    nstate = S5_GROUPS * S5_STATE
    st_spec = _const_spec((nb, nstate))
    cast_specs = [pl.BlockSpec((w.shape[0] // steps, w.shape[1]), lambda i: (i, 0)) for w in cast]
    assert all(w.shape[0] % (steps * 2 * SUBLANES) == 0 for w in cast)
    return pl.pallas_call(
        functools.partial(_s5_kernel, nb=nb, tl=tl, batch_major=batch_major, n_cast=len(cast)),
        grid=(steps,),
        in_specs=[u_spec, st_spec, st_spec,
                  _const_spec((S5_SLABS, SUBLANES, S5_SLAB_STATE)),
                  _const_spec((S5_SLABS, SUBLANES, S5_SLAB_STATE)),
                  _const_spec((S5_SLABS, LANES, 2 * S5_SLAB_STATE)),
                  _const_spec((S5_SLABS, 2 * S5_SLAB_STATE, LANES)),
                  _const_spec((1, D_S5)), _const_spec((D_S5, D_S5)), _const_spec((1, D_S5))] + cast_specs,
        out_specs=[u_spec, st_spec, st_spec] + cast_specs,
        out_shape=[jax.ShapeDtypeStruct(u.shape, f32),
                   jax.ShapeDtypeStruct((nb, nstate), f32),
                   jax.ShapeDtypeStruct((nb, nstate), f32)]
                  + [jax.ShapeDtypeStruct(w.shape, bf16) for w in cast],
        scratch_shapes=[pltpu.VMEM((S5_SLABS, rows, LANES), f32),
                        pltpu.VMEM((S5_SLABS, nb + rows, 2 * S5_SLAB_STATE), f32),
                        pltpu.VMEM((S5_SLABS, nb, 2 * S5_SLAB_STATE), f32),
                        pltpu.VMEM((S5_SLABS, rows, LANES), f32)],
        compiler_params=_params("arbitrary"),
        name="s5_mixer",
    )(u, re0, im0, ar, ai, bblk, cblk, dsk, wglu, bglu, *cast)


def _ffn_stages(x, ys, y5, mods, n2g, nfg, wo_ref, wg_ref, wu_ref, wd_ref, write_out, ff_chunk):
    g1, sh2, sc2, g2, shf, scf = mods
    att = jnp.dot(ys.astype(bf16), wo_ref[:D_SSD, :], preferred_element_type=f32)
    att = att + jnp.dot(y5.astype(bf16), wo_ref[D_SSD:, :], preferred_element_type=f32)
    yield
    x1 = x + _by_row(lambda v, s: v * s, att, g1)
    v = _rms_mod(x1, n2g, sc2, sh2).astype(bf16)
    ff = None
    for o in range(0, D_FF, ff_chunk):
        gate = jnp.dot(v, wg_ref[:, o:o + ff_chunk], preferred_element_type=f32)
        up = jnp.dot(v, wu_ref[:, o:o + ff_chunk], preferred_element_type=f32)
        hid = (_silu(gate) * up).astype(bf16)
        part = jnp.dot(hid, wd_ref[o:o + ff_chunk, :], preferred_element_type=f32)
        ff = part if ff is None else ff + part
        yield
    x2 = x1 + _by_row(lambda v, s: v * s, ff, g2)
    write_out(_rms_mod(x2, nfg, scf, shf))


def _ffn_kernel(x_ref, ys_ref, y5_ref, mod_ref, modf_ref, n2g_ref, nfg_ref, wo_ref, wg_ref, wu_ref, wd_ref,
                o_ref, *, seq_major, ff_chunk, sub):
    mods = (_mod_row(mod_ref, 2, seq_major), _mod_row(mod_ref, 3, seq_major), _mod_row(mod_ref, 4, seq_major),
            _mod_row(mod_ref, 5, seq_major), _mod_row(modf_ref, 0, seq_major), _mod_row(modf_ref, 1, seq_major))
    x = _rows_in(x_ref, seq_major)
    rows = x.shape[0]
    tiles = {}

    def stages(r0):
        def write_out(y):
            if seq_major:
                tiles[r0] = y
            else:
                o_ref[0, r0:r0 + sub, :] = y
        return _ffn_stages(x[r0:r0 + sub], ys_ref[0, r0:r0 + sub, :], y5_ref[0, r0:r0 + sub, :], mods,
                           n2g_ref[...], nfg_ref[...], wo_ref, wg_ref, wu_ref, wd_ref, write_out, ff_chunk)

    _interleave(*[stages(r0) for r0 in range(0, rows, sub)])
    if seq_major:
        s, t, d = o_ref.shape
        y = jnp.concatenate([tiles[r0] for r0 in range(0, rows, sub)], axis=0)
        o_ref[...] = jnp.swapaxes(y.reshape(t, s, d), 0, 1)


def _ffn(x, ys, y5, mod, modf, n2g, nfg, wo, wg, wu, wd, *, tm, sub, seq_major, n_seq, first_seq, ff_chunk=256):
    if seq_major:
        x_spec = _const_spec(x.shape)
        nb, rows, d = 1, x.shape[0] * x.shape[1], x.shape[2]
        assert tm == rows
    else:
        nb, rows, d = x.shape
        x_spec = pl.BlockSpec((1, tm, d), lambda i, j: (i, j, 0))
    nt = rows // tm
    mod_spec = _mod_spec(mod, n_seq, first_seq)
    modf_spec = _mod_spec(modf, n_seq, first_seq)
    blk = lambda wd_: pl.BlockSpec((1, tm, wd_), lambda i, j: (i, j, 0))
    single = dict(pipeline_mode=pl.Buffered(1))
    wspec = lambda shape: pl.BlockSpec(shape, lambda i, j: (0, 0), **single)
    return pl.pallas_call(
        functools.partial(_ffn_kernel, seq_major=seq_major, ff_chunk=ff_chunk, sub=sub),
        grid=(nb, nt),
        in_specs=[x_spec, blk(D_SSD), blk(D_S5), mod_spec, modf_spec,
                  _const_spec((1, d)), _const_spec((1, d)),
                  wspec((d, d)), wspec((d, D_FF)), wspec((d, D_FF)), wspec((D_FF, d))],
        out_specs=x_spec,
        out_shape=jax.ShapeDtypeStruct(x.shape, f32),
        compiler_params=_params("parallel", "parallel"),
        name="out_ffn",
    )(x, ys, y5, mod, modf, n2g, nfg, wo, wg, wu, wd)


def kernel(x_prompt, x_sample, c_prompt, c_sample, state_ssd, state_conv, state_s5_re, state_s5_im, w_ada, b_ada, norm1_g, w_in, conv_w, conv_b, ssd_dt_bias, ssd_A_log, ssd_D, ssd_norm_g, s5_A_re, s5_A_im, s5_log_step, s5_B_re, s5_B_im, s5_C_re, s5_C_im, s5_D, w_glu, b_glu, w_out, norm2_g, w_ffn_gate, w_ffn_up, w_ffn_down, w_ada_f, b_ada_f, normf_g):
    assert w_ada.shape[0] == 1, "single-layer stack"
    bp, seq, d = x_prompt.shape
    bs, dseq, _ = x_sample.shape

    mod, modf = _ada_mod(c_sample, c_prompt, (w_ada[0], w_ada_f), (b_ada[0], b_ada_f))
    seqs_s = dict(n_seq=bs, first_seq=0)
    seqs_p = dict(n_seq=bp, first_seq=bs)

    w_in_p = w_in[0].T
    pad_h = lambda v: jnp.concatenate([v, jnp.zeros((DT_PAD - HEADS,), f32)]).reshape(1, DT_PAD)
    dtb = pad_h(ssd_dt_bias[0])
    alog = pad_h(ssd_A_log[0])
    dexp = jnp.repeat(ssd_D[0], HEAD_DIM).reshape(1, D_SSD)
    ng = ssd_norm_g[0].reshape(1, D_SSD)
    cw, cb = conv_w[0], conv_b[0].reshape(1, CONV_DIM)

    ar, ai, bblk, cblk = _s5_params(s5_A_re[0], s5_A_im[0], s5_log_step[0],
                                    s5_B_re[0].transpose(0, 2, 1), s5_B_im[0].transpose(0, 2, 1),
                                    s5_C_re[0], s5_C_im[0])
    dsk = s5_D[0].reshape(1, D_S5)
    wglu = w_glu[0].astype(bf16)
    bglu = b_glu[0].reshape(1, D_S5)
    n1g, n2g, nfg = norm1_g[0].reshape(1, d), norm2_g[0].reshape(1, d), normf_g.reshape(1, d)
    nstate = S5_GROUPS * S5_STATE

    z, act, u5, dt, conv_p = _inproj(x_prompt, mod, n1g, w_in_p, (cw, cb), tm=1024, sub=512, seq_major=False,
                                     **seqs_p)
    y_ssd, ssd_p = _ssd_prompt(act, z, dt, dtb, alog, dexp, ng)
    zeros_st = jnp.zeros((bp, nstate), f32)
    y_s5, re_p, im_p, wo, wg, wu, wd = _s5(
        u5, zeros_st, zeros_st, ar, ai, bblk, cblk, dsk, wglu, bglu, tl=128, batch_major=True,
        cast=(w_out[0], w_ffn_gate[0], w_ffn_up[0], w_ffn_down[0]))
    y_prompt = _ffn(x_prompt, y_ssd, y_s5, mod, modf, n2g, nfg, wo, wg, wu, wd, tm=1024, sub=512,
                    seq_major=False, **seqs_p)

    rows_s = dseq * bs
    steps = lambda a: a.reshape(dseq, bs, a.shape[-1])
    flat = lambda a: a.reshape(1, rows_s, a.shape[-1])
    z, xbc, u5, dt = _inproj(x_sample, mod, n1g, w_in_p, tm=rows_s, sub=rows_s // 2, seq_major=True, **seqs_s)
    y_ssd, ssd_s, conv_s = _ssd_sample(steps(xbc), steps(z), steps(dt), state_conv[0].transpose(1, 0, 2),
                                       state_ssd[0], cw, cb, dtb, alog, dexp, ng)
    y_s5, re_s, im_s = _s5(steps(u5), state_s5_re[0].reshape(bs, nstate), state_s5_im[0].reshape(bs, nstate),
                           ar, ai, bblk, cblk, dsk, wglu, bglu, tl=dseq, batch_major=False)
    y_sample = _ffn(x_sample, flat(y_ssd), flat(y_s5), mod, modf, n2g, nfg, wo, wg, wu, wd,
                    tm=rows_s, sub=rows_s, seq_major=True, **seqs_s)

    g5 = (S5_GROUPS, S5_STATE)
    return (y_prompt, y_sample,
            ssd_p[None], ssd_s.reshape((1,) + state_ssd.shape[1:]),
            conv_p[None], conv_s.transpose(1, 0, 2)[None],
            re_p.reshape((1, bp) + g5), re_s.reshape((1, bs) + g5),
            im_p.reshape((1, bp) + g5), im_s.reshape((1, bs) + g5))
```

```python
import functools

import jax
import jax.numpy as jnp
from jax import lax
from jax.experimental import pallas as pl
from jax.experimental.pallas import tpu as pltpu

f32 = jnp.float32
bf16 = jnp.bfloat16

D_MODEL = 1024
D_SSD = 512
HEAD_DIM = 64
HEADS = 8
GROUPS = 2
HEADS_PER_GROUP = HEADS // GROUPS
STATE = 128
CONV_W = 4
CONV_DIM = D_SSD + 2 * GROUPS * STATE
D_S5 = 512
S5_CH = 16
S5_GROUPS = 32
S5_STATE = 64
D_FF = 2816
N_ADA = 6
EPS = 1e-6

LANES = 128
SUBLANES = 8
SSD_CHUNK = 128
S5_SLABS = D_S5 // LANES
S5_SLAB_STATE = (S5_GROUPS // S5_SLABS) * S5_STATE
DT_PAD = LANES
VMEM_LIMIT = 56 * 1024 * 1024

NT_DIMS = (((1,), (1,)), ((), ()))
TN_DIMS = (((0,), (0,)), ((), ()))


def _silu(x):
    return x * jax.nn.sigmoid(x)


def _interleave(*gens, lead=0):
    pending, live, rnd = list(gens), [], 0
    while pending or live:
        while pending and rnd >= lead * (len(gens) - len(pending)):
            live.append(pending.pop(0))
        for g in list(live):
            try:
                next(g)
            except StopIteration:
                live.remove(g)
        rnd += 1


def _by_row(fn, v, *ms):
    r, d = v.shape
    m_rows = ms[0].shape[0]
    if m_rows in (1, r):
        return fn(v, *ms)
    out = fn(v.reshape(r // m_rows, m_rows, d), *[m[None] for m in ms])
    return out.reshape(r, d)


def _rms_mod(x, g, sc, sh):
    y = x * lax.rsqrt(jnp.mean(x * x, axis=-1, keepdims=True) + EPS)
    return _by_row(lambda v, s, t: v * (1.0 + s) + t, y * g, sc, sh)


def _mod_row(mod_ref, i, per_row):
    return mod_ref[i] if per_row else mod_ref[i, pl.ds(pl.program_id(0), 1), :]


def _mod_spec(mod, n_seq, first_seq):
    assert first_seq % n_seq == 0
    return pl.BlockSpec((mod.shape[0], n_seq, mod.shape[2]), lambda *_: (0, first_seq // n_seq, 0))


def _const_spec(shape):
    nd = len(shape)
    return pl.BlockSpec(shape, lambda *_: (0,) * nd)


def _params(*sem):
    return pltpu.CompilerParams(dimension_semantics=sem, vmem_limit_bytes=VMEM_LIMIT)


ADA_K_BLOCK = 256


def _ada_kernel(ca_ref, cb_ref, *refs):
    nw = len(refs) // 3
    s = _silu(jnp.concatenate([ca_ref[...], cb_ref[...]], axis=0)).astype(bf16)
    for w_ref, b_ref, o_ref in zip(refs[:nw], refs[nw:2 * nw], refs[2 * nw:]):
        n, _, d = o_ref.shape

        @pl.when(pl.program_id(0) == 0)
        def _():
            for j in range(n):
                o_ref[j] = jnp.broadcast_to(b_ref[:, j * d:(j + 1) * d], o_ref.shape[1:])

        for j in range(n):
            o_ref[j] += jnp.dot(s, w_ref[:, j * d:(j + 1) * d].astype(bf16), preferred_element_type=f32)


def _ada_mod(ca, cb, ws, bs):
    k = ca.shape[1]
    m = ca.shape[0] + cb.shape[0]
    ns = [w.shape[1] // k for w in ws]
    return pl.pallas_call(
        _ada_kernel,
        grid=(k // ADA_K_BLOCK,),
        in_specs=[pl.BlockSpec((c.shape[0], ADA_K_BLOCK), lambda i: (0, i)) for c in (ca, cb)]
                 + [pl.BlockSpec((ADA_K_BLOCK, n * k), lambda i: (i, 0)) for n in ns]
                 + [_const_spec((1, n * k)) for n in ns],
        out_specs=[_const_spec((n, m, k)) for n in ns],
        out_shape=[jax.ShapeDtypeStruct((n, m, k), f32) for n in ns],
        compiler_params=_params("arbitrary"),
        name="ada_mod",
    )(ca, cb, *ws, *[b.reshape(1, n * k) for b, n in zip(bs, ns)])


def _rows_in(x_ref, seq_major):
    if not seq_major:
        return x_ref[0]
    s, t, d = x_ref.shape
    return jnp.swapaxes(x_ref[...], 0, 1).reshape(s * t, d)


def _conv_silu(ext, cw_ref, cb_ref):
    w = [cw_ref[k:k + 1, :] for k in range(CONV_W)]
    back1 = pltpu.roll(ext, 1, 0)
    older = pltpu.roll(ext * w[1] + back1 * w[0], 2, 0)
    conv = cb_ref[...] + ext[SUBLANES:, :] * w[3] + back1[SUBLANES:, :] * w[2] + older[SUBLANES:, :]
    return _silu(conv)


def _inproj_stages(x, sh, sc, g, outs, after=()):
    u = _rms_mod(x, g, sc, sh).astype(bf16)
    yield
    for write, w16 in outs:
        write(lax.dot_general(u, w16, NT_DIMS, preferred_element_type=f32))
        yield
    for stage in after:
        stage()
        yield


def _inproj_kernel(x_ref, mod_ref, g_ref, w_ref, *rest, seq_major, sub, conv):
    if conv:
        cw_ref, cb_ref, z_ref, act_ref, u5_ref, dt_ref, cn_ref, ext_ref = rest

        @pl.when(pl.program_id(1) == 0)
        def _():
            ext_ref[0:SUBLANES, :] = jnp.zeros((SUBLANES, CONV_DIM), f32)
    else:
        z_ref, xbc_ref, u5_ref, dt_ref = rest
    sh = _mod_row(mod_ref, 0, seq_major)
    sc = _mod_row(mod_ref, 1, seq_major)
    x = _rows_in(x_ref, seq_major)
    rows = x.shape[0]
    T = SSD_CHUNK
    o_dt = D_SSD + CONV_DIM
    w16 = lambda o, width: w_ref[o:o + width, :].astype(bf16)
    w_z, w_xbc, w_u5, w_dt = w16(0, D_SSD), w16(D_SSD, CONV_DIM), w16(o_dt + HEADS, D_S5), w16(o_dt, DT_PAD)

    def stages(r0):
        def to(ref):
            def write(v):
                ref[0, r0:r0 + sub, :] = v
            return write

        def to_ext(v):
            ext_ref[SUBLANES + r0:SUBLANES + r0 + sub, :] = v

        def conv_chunk(c0):
            def stage():
                act_ref[0, c0:c0 + T, :] = _conv_silu(ext_ref[c0:c0 + SUBLANES + T, :], cw_ref, cb_ref)
            return stage

        outs = [(to(z_ref), w_z), (to_ext if conv else to(xbc_ref), w_xbc), (to(u5_ref), w_u5), (to(dt_ref), w_dt)]
        after = [conv_chunk(c0) for c0 in range(r0, r0 + sub, T)] if conv else ()
        return _inproj_stages(x[r0:r0 + sub], sh, sc, g_ref[...], outs, after)

    n_matmul_stages = 5
    _interleave(*[stages(r0) for r0 in range(0, rows, sub)], lead=n_matmul_stages if conv else 0)
    if conv:
        cn_ref[0] = ext_ref[SUBLANES + rows - (CONV_W - 1):SUBLANES + rows, :]
        ext_ref[0:SUBLANES, :] = ext_ref[rows:rows + SUBLANES, :]


def _inproj(x, mod, g, w, conv_wb=None, *, tm, sub, seq_major, n_seq, first_seq):
    if seq_major:
        x_spec = _const_spec(x.shape)
        nb, rows, d = 1, x.shape[0] * x.shape[1], x.shape[2]
        assert tm == rows
    else:
        nb, rows, d = x.shape
        x_spec = pl.BlockSpec((1, tm, d), lambda i, j: (i, j, 0))
    nt = rows // tm
    widths = (D_SSD, CONV_DIM, D_S5, DT_PAD)
    conv = conv_wb is not None
    in_specs = [x_spec, _mod_spec(mod, n_seq, first_seq), _const_spec((1, d)),
                pl.BlockSpec(w.shape, lambda i, j: (0, 0), pipeline_mode=pl.Buffered(1))]
    out_specs = [pl.BlockSpec((1, tm, wd), lambda i, j: (i, j, 0)) for wd in widths]
    out_shape = [jax.ShapeDtypeStruct((nb, rows, wd), f32) for wd in widths]
    scratch = []
    if conv:
        in_specs += [_const_spec((CONV_W, CONV_DIM)), _const_spec((1, CONV_DIM))]
        out_specs.append(pl.BlockSpec((1, CONV_W - 1, CONV_DIM), lambda i, j: (i, 0, 0)))
        out_shape.append(jax.ShapeDtypeStruct((nb, CONV_W - 1, CONV_DIM), f32))
        scratch.append(pltpu.VMEM((SUBLANES + tm, CONV_DIM), f32))
    return pl.pallas_call(
        functools.partial(_inproj_kernel, seq_major=seq_major, sub=sub, conv=conv),
        grid=(nb, nt),
        in_specs=in_specs, out_specs=out_specs, out_shape=out_shape, scratch_shapes=scratch,
        compiler_params=_params("parallel", "arbitrary" if conv else "parallel"),
        name="in_proj",
    )(x, mod, g, w, *(conv_wb or ()))


def _split3(x):
    hi = x.astype(bf16)
    r1 = x - hi.astype(f32)
    mid = r1.astype(bf16)
    lo = (r1 - mid.astype(f32)).astype(bf16)
    return hi, mid, lo


def _dot_sel_lhs(sel16, x):
    return sum(jnp.dot(sel16, p, preferred_element_type=f32) for p in _split3(x))


def _dot_sel_rhs(x, sel16):
    hi, mid, _ = _split3(x)
    return jnp.dot(hi, sel16, preferred_element_type=f32) + jnp.dot(mid, sel16, preferred_element_type=f32)


def _gated_group_norm(y, z, ng):
    y = y * _silu(z)
    gw = D_SSD // GROUPS
    parts = []
    for g in range(GROUPS):
        yg = y[:, g * gw:(g + 1) * gw]
        parts.append(yg * lax.rsqrt(jnp.mean(yg * yg, axis=-1, keepdims=True) + EPS))
    return jnp.concatenate(parts, axis=-1) * ng


def _ssd_prompt_kernel(act_ref, z_ref, dt_ref, dtb_ref, alog_ref, dexp_ref, ng_ref, y_ref, st_ref, h_ref, *, cps):
    T = SSD_CHUNK
    gw = HEADS_PER_GROUP * HEAD_DIM

    @pl.when(pl.program_id(1) == 0)
    def _():
        h_ref[...] = jnp.zeros_like(h_ref)

    row = lax.broadcasted_iota(jnp.int32, (T, T), 0)
    col = lax.broadcasted_iota(jnp.int32, (T, T), 1)
    tri = row >= col
    tri16 = jnp.where(tri, 1.0, 0.0).astype(bf16)
    low_half = lax.broadcasted_iota(jnp.int32, (T, LANES), 1) < HEAD_DIM
    a_neg = -jnp.exp(alog_ref[...])

    lane_head = lax.broadcasted_iota(jnp.int32, (DT_PAD, D_SSD), 1) // HEAD_DIM
    sel16 = jnp.where(lax.broadcasted_iota(jnp.int32, (DT_PAD, D_SSD), 0) == lane_head, 1.0, 0.0).astype(bf16)

    def spread(q):
        return _dot_sel_rhs(q, sel16)

    def state_free(ci):
        r0 = ci * T
        act = act_ref[0, r0:r0 + T, :]
        xs = act[:, :D_SSD]
        bm = act[:, D_SSD:D_SSD + GROUPS * STATE]
        cm = act[:, D_SSD + GROUPS * STATE:]
        dtv = jax.nn.softplus(dt_ref[0, r0:r0 + T, :] + dtb_ref[...])
        a_cs = _dot_sel_lhs(tri16, dtv * a_neg)
        bg16s = [bm[:, g * STATE:(g + 1) * STATE].astype(bf16) for g in range(GROUPS)]
        cg16s = [cm[:, g * STATE:(g + 1) * STATE].astype(bf16) for g in range(GROUPS)]
        gmats = [lax.dot_general(cg16s[g], bg16s[g], NT_DIMS, preferred_element_type=f32)
                 for g in range(GROUPS)]
        yield
        a_last = a_cs[T - 1:T, :]
        a_cs_t = a_cs.T
        rep = spread(jnp.concatenate([dtv, jnp.exp(a_last - a_cs), jnp.exp(a_cs)], axis=0))
        yield
        x = xs * rep[:T]
        x16 = x.astype(bf16)
        xd16 = (x * rep[T:2 * T]).astype(bf16)
        e_cs = rep[2 * T:]
        yield
        y_part = []
        for g in range(GROUPS):
            for pr in range(HEADS_PER_GROUP // 2):
                s_pair = []
                for q in range(2):
                    h = g * HEADS_PER_GROUP + 2 * pr + q
                    seg = a_cs[:, h:h + 1] - a_cs_t[h:h + 1, :]
                    lmat = jnp.exp(jnp.where(tri, seg, -jnp.inf))
                    s_pair.append((gmats[g] * lmat).astype(bf16))
                lanes = slice(g * gw + pr * LANES, g * gw + (pr + 1) * LANES)
                both = jnp.dot(jnp.concatenate(s_pair, axis=0), x16[:, lanes], preferred_element_type=f32)
                yield
                y_part.append(jnp.where(low_half, both[:T], both[T:]))
        y_free = jnp.concatenate(y_part, axis=-1) + xs * dexp_ref[...]
        return y_free, e_cs, jnp.exp(a_last), xd16, bg16s, cg16s

    def state_step(ci, y_free, e_cs, e_last, xd16, bg16s, cg16s):
        r0 = ci * T
        y_offs = []
        for g in range(GROUPS):
            hp = h_ref[g]
            y_offs.append(lax.dot_general(cg16s[g], hp.astype(bf16), NT_DIMS, preferred_element_type=f32))
            upd = lax.dot_general(xd16[:, g * gw:(g + 1) * gw], bg16s[g], TN_DIMS, preferred_element_type=f32)
            for hh in range(HEADS_PER_GROUP):
                h = g * HEADS_PER_GROUP + hh
                rows = slice(hh * HEAD_DIM, (hh + 1) * HEAD_DIM)
                h_ref[g, rows, :] = e_last[:, h:h + 1] * hp[rows] + upd[rows]
        y = y_free + jnp.concatenate(y_offs, axis=-1) * e_cs
        y_ref[0, r0:r0 + T, :] = _gated_group_norm(y, z_ref[0, r0:r0 + T, :], ng_ref[...])

    gens = [state_free(ci) for ci in range(cps)]
    free = [None] * cps
    while any(f is None for f in free):
        for ci in range(cps):
            if free[ci] is None:
                try:
                    next(gens[ci])
                except StopIteration as done:
                    free[ci] = done.value
    for ci in range(cps):
        state_step(ci, *free[ci])

    @pl.when(pl.program_id(1) == pl.num_programs(1) - 1)
    def _():
        st_ref[0] = h_ref[...].reshape(HEADS, HEAD_DIM, STATE)


def _ssd_prompt(act, z, dt, dtb, alog, dexp, ng, *, cps=8):
    nb, L, _ = act.shape
    rows = cps * SSD_CHUNK
    blk = lambda wd: pl.BlockSpec((1, rows, wd), lambda b, c: (b, c, 0))
    return pl.pallas_call(
        functools.partial(_ssd_prompt_kernel, cps=cps),
        grid=(nb, L // rows),
        in_specs=[blk(CONV_DIM), blk(D_SSD), blk(DT_PAD), _const_spec((1, DT_PAD)), _const_spec((1, DT_PAD)),
                  _const_spec((1, D_SSD)), _const_spec((1, D_SSD))],
        out_specs=[blk(D_SSD), pl.BlockSpec((1, HEADS, HEAD_DIM, STATE), lambda b, c: (b, 0, 0, 0))],
        out_shape=[jax.ShapeDtypeStruct((nb, L, D_SSD), f32),
                   jax.ShapeDtypeStruct((nb, HEADS, HEAD_DIM, STATE), f32)],
        scratch_shapes=[pltpu.VMEM((GROUPS, HEADS_PER_GROUP * HEAD_DIM, STATE), f32)],
        compiler_params=_params("parallel", "arbitrary"),
        name="ssd_prompt",
    )(act, z, dt, dtb, alog, dexp, ng)


def _ssd_sample_kernel(xbc_ref, z_ref, dt_ref, cbuf_ref, st_ref, cw_ref, cb_ref, dtb_ref, alog_ref,
                       dexp_ref, ng_ref, y_ref, stn_ref, cn_ref, dtot_ref, *, L, nb):
    gw = HEADS_PER_GROUP * HEAD_DIM
    full = [cbuf_ref[j] for j in range(CONV_W - 1)] + [xbc_ref[t] for t in range(L)]
    for j in range(CONV_W - 1):
        cn_ref[j] = full[L + j]
    a_neg = -jnp.exp(alog_ref[...])
    xs, bm, cm, dtv, acs = [], [], [], [], []
    run = None
    for t in range(L):
        conv = cb_ref[...]
        for k in range(CONV_W):
            conv = conv + full[t + k] * cw_ref[k:k + 1, :]
        act = _silu(conv)
        xs.append(act[:, :D_SSD])
        bm.append(act[:, D_SSD:D_SSD + GROUPS * STATE])
        cm.append(act[:, D_SSD + GROUPS * STATE:])
        d = jax.nn.softplus(dt_ref[t] + dtb_ref[...])
        dtv.append(d)
        run = d * a_neg if run is None else run + d * a_neg
        acs.append(run)
    a_tot = acs[L - 1]
    dtot_ref[...] = jnp.exp(a_tot)

    lane_head = lax.broadcasted_iota(jnp.int32, (DT_PAD, D_SSD), 1) // HEAD_DIM
    sel16 = jnp.where(lax.broadcasted_iota(jnp.int32, (DT_PAD, D_SSD), 0) == lane_head, 1.0, 0.0).astype(bf16)
    pairs = [(t, s) for t in range(L) for s in range(t)]
    factors = (dtv + [jnp.exp(a_tot - acs[t]) for t in range(L)] + [jnp.exp(acs[t]) for t in range(L)]
               + [jnp.exp(acs[t] - acs[s]) for t, s in pairs])
    rep = _dot_sel_rhs(jnp.concatenate(factors, axis=0), sel16)
    piece = lambda i: rep[i * nb:(i + 1) * nb]
    x = [xs[t] * piece(t) for t in range(L)]
    xd_stack = jnp.concatenate([x[t] * piece(L + t) for t in range(L)], axis=0)
    e_cs = [piece(2 * L + t) for t in range(L)]
    decay = {ts: piece(3 * L + i) for i, ts in enumerate(pairs)}

    in_group0 = lax.broadcasted_iota(jnp.int32, (nb, D_SSD), 1) < gw
    y_intra = []
    for t in range(L):
        acc = None
        for s in range(t + 1):
            cb_dot = [jnp.sum(cm[t][:, g * STATE:(g + 1) * STATE] * bm[s][:, g * STATE:(g + 1) * STATE],
                              axis=-1, keepdims=True) for g in range(GROUPS)]
            w = jnp.where(in_group0, cb_dot[0], cb_dot[1])
            term = w * x[s] if s == t else w * decay[(t, s)] * x[s]
            acc = term if acc is None else acc + term
        y_intra.append(acc)

    c_stack = [jnp.concatenate([cm[t][:, g * STATE:(g + 1) * STATE] for t in range(L)], axis=0).astype(bf16)
               for g in range(GROUPS)]
    b_stack = [jnp.concatenate([bm[t][:, g * STATE:(g + 1) * STATE] for t in range(L)], axis=0).astype(bf16)
               for g in range(GROUPS)]
    seq_of_row = lax.broadcasted_iota(jnp.int32, (L * nb, gw), 0) & (nb - 1)

    def per_seq(b, yoff):
        mine = seq_of_row == b
        drow = dtot_ref[pl.ds(b, 1), :]
        out = []
        for g in range(GROUPS):
            h0 = st_ref[b, g]
            r = lax.dot_general(c_stack[g], h0.astype(bf16), NT_DIMS, preferred_element_type=f32)
            out.append(yoff[g] + jnp.where(mine, r, 0.0))
            xm = jnp.where(mine, xd_stack[:, g * gw:(g + 1) * gw], 0.0).astype(bf16)
            upd = lax.dot_general(xm, b_stack[g], TN_DIMS, preferred_element_type=f32)
            for hh in range(HEADS_PER_GROUP):
                h = g * HEADS_PER_GROUP + hh
                rows = slice(hh * HEAD_DIM, (hh + 1) * HEAD_DIM)
                stn_ref[b, g, rows, :] = drow[:, h:h + 1] * h0[rows] + upd[rows]
        return tuple(out)

    yoff = lax.fori_loop(0, nb, per_seq, tuple(jnp.zeros((L * nb, gw), f32) for _ in range(GROUPS)),
                         unroll=8)

    for t in range(L):
        y_off = jnp.concatenate([yoff[g][t * nb:(t + 1) * nb] for g in range(GROUPS)], axis=-1)
        y = y_intra[t] + y_off * e_cs[t] + xs[t] * dexp_ref[...]
        y_ref[t] = _gated_group_norm(y, z_ref[t], ng_ref[...])


def _ssd_sample(xbc, z, dt, cbuf, st, cw, cb, dtb, alog, dexp, ng, *, nb=8):
    L, B, _ = xbc.shape
    assert nb & (nb - 1) == 0 and B % nb == 0
    tblk = lambda n, wd: pl.BlockSpec((n, nb, wd), lambda i: (0, i, 0))
    gw = HEADS_PER_GROUP * HEAD_DIM
    st = st.reshape(B, GROUPS, gw, STATE)
    st_spec = pl.BlockSpec((nb, GROUPS, gw, STATE), lambda i: (i, 0, 0, 0))
    return pl.pallas_call(
        functools.partial(_ssd_sample_kernel, L=L, nb=nb),
        grid=(B // nb,),
        in_specs=[tblk(L, CONV_DIM), tblk(L, D_SSD), tblk(L, DT_PAD), tblk(CONV_W - 1, CONV_DIM), st_spec,
                  _const_spec((CONV_W, CONV_DIM)), _const_spec((1, CONV_DIM)),
                  _const_spec((1, DT_PAD)), _const_spec((1, DT_PAD)),
                  _const_spec((1, D_SSD)), _const_spec((1, D_SSD))],
        out_specs=[tblk(L, D_SSD), st_spec, tblk(CONV_W - 1, CONV_DIM)],
        out_shape=[jax.ShapeDtypeStruct((L, B, D_SSD), f32),
                   jax.ShapeDtypeStruct((B, GROUPS, gw, STATE), f32),
                   jax.ShapeDtypeStruct((CONV_W - 1, B, CONV_DIM), f32)],
        scratch_shapes=[pltpu.VMEM((nb, DT_PAD), f32)],
        compiler_params=_params("parallel"),
        name="ssd_sample",
    )(xbc, z, dt, cbuf, st, cw, cb, dtb, alog, dexp, ng)


def _s5_param_kernel(lr_ref, li_ref, ls_ref, br_ref, bi_ref, cr_ref, ci_ref,
                     ar_ref, ai_ref, bblk_ref, cblk_ref, b_scr, c_scr):
    lr, li = lr_ref[...], li_ref[...]
    step = jnp.exp(ls_ref[...])
    mag = jnp.exp(lr * step)
    abr = mag * jnp.cos(li * step)
    abi = mag * jnp.sin(li * step)
    nr, ni = abr - 1.0, abi
    den = lr * lr + li * li
    fr = (nr * lr + ni * li) / den
    fi = (ni * lr - nr * li) / den
    br, bi = br_ref[...], bi_ref[...]
    bbr = fr * br - fi * bi
    bbi = fr * bi + fi * br
    b_scr[...] = jnp.zeros_like(b_scr)
    c_scr[...] = jnp.zeros_like(c_scr)
    gps = S5_GROUPS // S5_SLABS
    ns = S5_SLAB_STATE
    for g in range(S5_GROUPS):
        s, gl = divmod(g, gps)
        ch = slice(gl * S5_CH, (gl + 1) * S5_CH)
        st = slice(gl * S5_STATE, (gl + 1) * S5_STATE)
        st_im = slice(ns + gl * S5_STATE, ns + (gl + 1) * S5_STATE)
        ar_ref[s, :, st] = jnp.broadcast_to(abr[g], (SUBLANES, S5_STATE))
        ai_ref[s, :, st] = jnp.broadcast_to(abi[g], (SUBLANES, S5_STATE))
        b_scr[s, ch, st] = bbr[g]
        b_scr[s, ch, st_im] = bbi[g]
        c_scr[s, st, ch] = cr_ref[g].T
        c_scr[s, st_im, ch] = -ci_ref[g].T
    bblk_ref[...] = b_scr[...].astype(bf16)
    cblk_ref[...] = c_scr[...].astype(bf16)


def _s5_params(a_re, a_im, log_step, b_re_t, b_im_t, c_re, c_im):
    g, p = a_re.shape
    ns = S5_SLAB_STATE
    return pl.pallas_call(
        _s5_param_kernel,
        out_shape=[jax.ShapeDtypeStruct((S5_SLABS, SUBLANES, ns), f32),
                   jax.ShapeDtypeStruct((S5_SLABS, SUBLANES, ns), f32),
                   jax.ShapeDtypeStruct((S5_SLABS, LANES, 2 * ns), bf16),
                   jax.ShapeDtypeStruct((S5_SLABS, 2 * ns, LANES), bf16)],
        scratch_shapes=[pltpu.VMEM((S5_SLABS, LANES, 2 * ns), f32),
                        pltpu.VMEM((S5_SLABS, 2 * ns, LANES), f32)],
        name="s5_params",
    )(a_re.reshape(g, 1, p), a_im.reshape(g, 1, p), log_step.reshape(g, 1, 1), b_re_t, b_im_t, c_re, c_im)


def _gelu(x):
    return 0.5 * x * (1.0 + lax.erf(x * (2.0 ** -0.5)))


def _s5_slabs_interleaved(ut_ref, hbuf_ref, hst_ref, g_ref, ar_ref, ai_ref, bblk_ref, cblk_ref, dsk_ref, tl,
                          row_block=256, n_stages=8):
    ns = S5_SLAB_STATE
    nb = SUBLANES
    rows = nb * tl

    def lanes(s):
        return slice(s * LANES, (s + 1) * LANES)

    def bu_stage(slabs):
        for s in slabs:
            for r in range(0, rows, row_block):
                us = ut_ref[s, r:r + row_block, :]
                hbuf_ref[s, nb + r:nb + r + row_block, :] = jnp.dot(us.astype(bf16), bblk_ref[s],
                                                                    preferred_element_type=f32)
                yield

    def scan_stage(slabs):
        carry = [(hst_ref[s, :, :ns], hst_ref[s, :, ns:]) for s in slabs]
        coef = [(ar_ref[s], ai_ref[s]) for s in slabs]
        for t in range(tl):
            r = nb + nb * t
            for i, s in enumerate(slabs):
                (pr, pi), (ar, ai) = carry[i], coef[i]
                nr = ar * pr - ai * pi + hbuf_ref[s, r:r + nb, :ns]
                ni = ar * pi + ai * pr + hbuf_ref[s, r:r + nb, ns:]
                hbuf_ref[s, r:r + nb, :ns] = nr
                hbuf_ref[s, r:r + nb, ns:] = ni
                carry[i] = (nr, ni)
            if (t + 1) % (tl // n_stages) == 0:
                yield
        for i, s in enumerate(slabs):
            hst_ref[s, :, :ns] = carry[i][0]
            hst_ref[s, :, ns:] = carry[i][1]

    def y_stage(slabs):
        for s in slabs:
            for r in range(0, rows, row_block):
                h16 = hbuf_ref[s, nb + r:nb + r + row_block, :].astype(bf16)
                ys = jnp.dot(h16, cblk_ref[s], preferred_element_type=f32)
                ys = ys + dsk_ref[:, lanes(s)] * ut_ref[s, r:r + row_block, :]
                g_ref[s, r:r + row_block, :] = _gelu(ys)
                yield

    half = S5_SLABS // 2
    first, second = tuple(range(half)), tuple(range(half, S5_SLABS))
    _interleave(bu_stage(first))
    _interleave(scan_stage(first), bu_stage(second))
    _interleave(scan_stage(second), y_stage(first))
    _interleave(y_stage(second))


def _s5_scan_vmem(hbuf_ref, hst_ref, ar_ref, ai_ref, s, nb, rows):
    ns = S5_SLAB_STATE
    hbuf_ref[s, 0:nb, :] = hst_ref[s]
    ar, ai = ar_ref[s], ai_ref[s]

    def body(i, carry):
        j = pl.multiple_of(nb + i * SUBLANES, SUBLANES)
        prev = hbuf_ref[s, pl.ds(j - nb, SUBLANES), :]
        cur = hbuf_ref[s, pl.ds(j, SUBLANES), :]
        pr, pi = prev[:, :ns], prev[:, ns:]
        hbuf_ref[s, pl.ds(j, SUBLANES), :ns] = ar * pr - ai * pi + cur[:, :ns]
        hbuf_ref[s, pl.ds(j, SUBLANES), ns:] = ar * pi + ai * pr + cur[:, ns:]
        return carry

    lax.fori_loop(0, rows // SUBLANES, body, 0, unroll=min(8, nb // SUBLANES))
    hst_ref[s] = hbuf_ref[s, rows:rows + nb, :]


S5_N_IN = 10


def _s5_kernel(*refs, nb, tl, batch_major, n_cast):
    (u_ref, re0_ref, im0_ref, ar_ref, ai_ref, bblk_ref, cblk_ref, dsk_ref, wglu_ref,
     bglu_ref) = refs[:S5_N_IN]
    cast_in = refs[S5_N_IN:S5_N_IN + n_cast]
    y_ref, ren_ref, imn_ref = refs[S5_N_IN + n_cast:S5_N_IN + n_cast + 3]
    cast_out = refs[S5_N_IN + n_cast + 3:S5_N_IN + 2 * n_cast + 3]
    ut_ref, hbuf_ref, hst_ref, g_ref = refs[S5_N_IN + 2 * n_cast + 3:]
    for w_ref, w16_ref in zip(cast_in, cast_out):
        w16_ref[...] = w_ref[...].astype(bf16)
    step = pl.program_id(0)
    rows = nb * tl
    ns = S5_SLAB_STATE

    @pl.when(step == 0)
    def _():
        for s in range(S5_SLABS):
            hst_ref[s, :, :ns] = re0_ref[:, s * ns:(s + 1) * ns]
            hst_ref[s, :, ns:] = im0_ref[:, s * ns:(s + 1) * ns]

    for s in range(S5_SLABS):
        sl = slice(s * LANES, (s + 1) * LANES)
        if batch_major:
            for b in range(nb):
                ut_ref[s, pl.ds(b, tl, stride=nb), :] = u_ref[b, :, sl]
        else:
            ut_ref[s] = u_ref[:, :, sl].reshape(rows, LANES)

    if nb == SUBLANES:
        _s5_slabs_interleaved(ut_ref, hbuf_ref, hst_ref, g_ref, ar_ref, ai_ref, bblk_ref, cblk_ref, dsk_ref, tl)
    else:
        for s in range(S5_SLABS):
            sl = slice(s * LANES, (s + 1) * LANES)
            hbuf_ref[s, nb:nb + rows, :] = jnp.dot(ut_ref[s].astype(bf16), bblk_ref[s],
                                                   preferred_element_type=f32)
            _s5_scan_vmem(hbuf_ref, hst_ref, ar_ref, ai_ref, s, nb, rows)
            ys = jnp.dot(hbuf_ref[s, nb:nb + rows, :].astype(bf16), cblk_ref[s], preferred_element_type=f32)
            g_ref[s] = _gelu(ys + dsk_ref[:, sl] * ut_ref[s])

    g = jnp.concatenate([g_ref[s] for s in range(S5_SLABS)], axis=-1)
    gate = jnp.dot(g.astype(bf16), wglu_ref[...], preferred_element_type=f32) + bglu_ref[...]
    out = g * jax.nn.sigmoid(gate)
    if batch_major:
        for s in range(S5_SLABS):
            sl = slice(s * LANES, (s + 1) * LANES)
            ut_ref[s] = out[:, sl]
            for b in range(nb):
                y_ref[b, :, sl] = ut_ref[s, pl.ds(b, tl, stride=nb), :]
    else:
        y_ref[...] = out.reshape(tl, nb, D_S5)

    @pl.when(step == pl.num_programs(0) - 1)
    def _():
        for s in range(S5_SLABS):
            ren_ref[:, s * ns:(s + 1) * ns] = hst_ref[s, :, :ns]
            imn_ref[:, s * ns:(s + 1) * ns] = hst_ref[s, :, ns:]


def _s5(u, re0, im0, ar, ai, bblk, cblk, dsk, wglu, bglu, *, tl, batch_major, cast=()):
    if batch_major:
        nb, L, _ = u.shape
        u_spec = pl.BlockSpec((nb, tl, D_S5), lambda i: (0, i, 0))
    else:
        L, nb, _ = u.shape
        u_spec = pl.BlockSpec((tl, nb, D_S5), lambda i: (i, 0, 0))
    rows = nb * tl
    steps = L // tl
    nstate = S5_GROUPS * S5_STATE
    st_spec = _const_spec((nb, nstate))
    cast_specs = [pl.BlockSpec((w.shape[0] // steps, w.shape[1]), lambda i: (i, 0)) for w in cast]
    assert all(w.shape[0] % (steps * 2 * SUBLANES) == 0 for w in cast)
    return pl.pallas_call(
        functools.partial(_s5_kernel, nb=nb, tl=tl, batch_major=batch_major, n_cast=len(cast)),
        grid=(steps,),
        in_specs=[u_spec, st_spec, st_spec,
                  _const_spec((S5_SLABS, SUBLANES, S5_SLAB_STATE)),
                  _const_spec((S5_SLABS, SUBLANES, S5_SLAB_STATE)),
                  _const_spec((S5_SLABS, LANES, 2 * S5_SLAB_STATE)),
                  _const_spec((S5_SLABS, 2 * S5_SLAB_STATE, LANES)),
                  _const_spec((1, D_S5)), _const_spec((D_S5, D_S5)), _const_spec((1, D_S5))] + cast_specs,
        out_specs=[u_spec, st_spec, st_spec] + cast_specs,
        out_shape=[jax.ShapeDtypeStruct(u.shape, f32),
                   jax.ShapeDtypeStruct((nb, nstate), f32),
                   jax.ShapeDtypeStruct((nb, nstate), f32)]
                  + [jax.ShapeDtypeStruct(w.shape, bf16) for w in cast],
        scratch_shapes=[pltpu.VMEM((S5_SLABS, rows, LANES), f32),
                        pltpu.VMEM((S5_SLABS, nb + rows, 2 * S5_SLAB_STATE), f32),
                        pltpu.VMEM((S5_SLABS, nb, 2 * S5_SLAB_STATE), f32),
                        pltpu.VMEM((S5_SLABS, rows, LANES), f32)],
        compiler_params=_params("arbitrary"),
        name="s5_mixer",
    )(u, re0, im0, ar, ai, bblk, cblk, dsk, wglu, bglu, *cast)


def _ffn_stages(x, ys, y5, mods, n2g, nfg, wo_ref, wg_ref, wu_ref, wd_ref, write_out, ff_chunk):
    g1, sh2, sc2, g2, shf, scf = mods
    att = jnp.dot(ys.astype(bf16), wo_ref[:D_SSD, :], preferred_element_type=f32)
    att = att + jnp.dot(y5.astype(bf16), wo_ref[D_SSD:, :], preferred_element_type=f32)
    yield
    x1 = x + _by_row(lambda v, s: v * s, att, g1)
    v = _rms_mod(x1, n2g, sc2, sh2).astype(bf16)
    ff = None
    for o in range(0, D_FF, ff_chunk):
        gate = jnp.dot(v, wg_ref[:, o:o + ff_chunk], preferred_element_type=f32)
        up = jnp.dot(v, wu_ref[:, o:o + ff_chunk], preferred_element_type=f32)
        hid = (_silu(gate) * up).astype(bf16)
        part = jnp.dot(hid, wd_ref[o:o + ff_chunk, :], preferred_element_type=f32)
        ff = part if ff is None else ff + part
        yield
    x2 = x1 + _by_row(lambda v, s: v * s, ff, g2)
    write_out(_rms_mod(x2, nfg, scf, shf))


def _ffn_kernel(x_ref, ys_ref, y5_ref, mod_ref, modf_ref, n2g_ref, nfg_ref, wo_ref, wg_ref, wu_ref, wd_ref,
                o_ref, *, seq_major, ff_chunk, sub):
    mods = (_mod_row(mod_ref, 2, seq_major), _mod_row(mod_ref, 3, seq_major), _mod_row(mod_ref, 4, seq_major),
            _mod_row(mod_ref, 5, seq_major), _mod_row(modf_ref, 0, seq_major), _mod_row(modf_ref, 1, seq_major))
    x = _rows_in(x_ref, seq_major)
    rows = x.shape[0]
    tiles = {}

    def stages(r0):
        def write_out(y):
            if seq_major:
                tiles[r0] = y
            else:
                o_ref[0, r0:r0 + sub, :] = y
        return _ffn_stages(x[r0:r0 + sub], ys_ref[0, r0:r0 + sub, :], y5_ref[0, r0:r0 + sub, :], mods,
                           n2g_ref[...], nfg_ref[...], wo_ref, wg_ref, wu_ref, wd_ref, write_out, ff_chunk)

    _interleave(*[stages(r0) for r0 in range(0, rows, sub)])
    if seq_major:
        s, t, d = o_ref.shape
        y = jnp.concatenate([tiles[r0] for r0 in range(0, rows, sub)], axis=0)
        o_ref[...] = jnp.swapaxes(y.reshape(t, s, d), 0, 1)


def _ffn(x, ys, y5, mod, modf, n2g, nfg, wo, wg, wu, wd, *, tm, sub, seq_major, n_seq, first_seq, ff_chunk=256):
    if seq_major:
        x_spec = _const_spec(x.shape)
        nb, rows, d = 1, x.shape[0] * x.shape[1], x.shape[2]
        assert tm == rows
    else:
        nb, rows, d = x.shape
        x_spec = pl.BlockSpec((1, tm, d), lambda i, j: (i, j, 0))
    nt = rows // tm
    mod_spec = _mod_spec(mod, n_seq, first_seq)
    modf_spec = _mod_spec(modf, n_seq, first_seq)
    blk = lambda wd_: pl.BlockSpec((1, tm, wd_), lambda i, j: (i, j, 0))
    single = dict(pipeline_mode=pl.Buffered(1))
    wspec = lambda shape: pl.BlockSpec(shape, lambda i, j: (0, 0), **single)
    return pl.pallas_call(
        functools.partial(_ffn_kernel, seq_major=seq_major, ff_chunk=ff_chunk, sub=sub),
        grid=(nb, nt),
        in_specs=[x_spec, blk(D_SSD), blk(D_S5), mod_spec, modf_spec,
                  _const_spec((1, d)), _const_spec((1, d)),
                  wspec((d, d)), wspec((d, D_FF)), wspec((d, D_FF)), wspec((D_FF, d))],
        out_specs=x_spec,
        out_shape=jax.ShapeDtypeStruct(x.shape, f32),
        compiler_params=_params("parallel", "parallel"),
        name="out_ffn",
    )(x, ys, y5, mod, modf, n2g, nfg, wo, wg, wu, wd)


def kernel(x_prompt, x_sample, c_prompt, c_sample, state_ssd, state_conv, state_s5_re, state_s5_im, w_ada, b_ada, norm1_g, w_in, conv_w, conv_b, ssd_dt_bias, ssd_A_log, ssd_D, ssd_norm_g, s5_A_re, s5_A_im, s5_log_step, s5_B_re, s5_B_im, s5_C_re, s5_C_im, s5_D, w_glu, b_glu, w_out, norm2_g, w_ffn_gate, w_ffn_up, w_ffn_down, w_ada_f, b_ada_f, normf_g):
    assert w_ada.shape[0] == 1, "single-layer stack"
    bp, seq, d = x_prompt.shape
    bs, dseq, _ = x_sample.shape

    mod, modf = _ada_mod(c_sample, c_prompt, (w_ada[0], w_ada_f), (b_ada[0], b_ada_f))
    seqs_s = dict(n_seq=bs, first_seq=0)
    seqs_p = dict(n_seq=bp, first_seq=bs)

    w_in_p = w_in[0].T
    pad_h = lambda v: jnp.concatenate([v, jnp.zeros((DT_PAD - HEADS,), f32)]).reshape(1, DT_PAD)
    dtb = pad_h(ssd_dt_bias[0])
    alog = pad_h(ssd_A_log[0])
    dexp = jnp.repeat(ssd_D[0], HEAD_DIM).reshape(1, D_SSD)
    ng = ssd_norm_g[0].reshape(1, D_SSD)
    cw, cb = conv_w[0], conv_b[0].reshape(1, CONV_DIM)

    ar, ai, bblk, cblk = _s5_params(s5_A_re[0], s5_A_im[0], s5_log_step[0],
                                    s5_B_re[0].transpose(0, 2, 1), s5_B_im[0].transpose(0, 2, 1),
                                    s5_C_re[0], s5_C_im[0])
    dsk = s5_D[0].reshape(1, D_S5)
    wglu = w_glu[0].astype(bf16)
    bglu = b_glu[0].reshape(1, D_S5)
    n1g, n2g, nfg = norm1_g[0].reshape(1, d), norm2_g[0].reshape(1, d), normf_g.reshape(1, d)
    nstate = S5_GROUPS * S5_STATE

    z, act, u5, dt, conv_p = _inproj(x_prompt, mod, n1g, w_in_p, (cw, cb), tm=1024, sub=512, seq_major=False,
                                     **seqs_p)
    y_ssd, ssd_p = _ssd_prompt(act, z, dt, dtb, alog, dexp, ng)
    zeros_st = jnp.zeros((bp, nstate), f32)
    y_s5, re_p, im_p, wo, wg, wu, wd = _s5(
        u5, zeros_st, zeros_st, ar, ai, bblk, cblk, dsk, wglu, bglu, tl=128, batch_major=True,
        cast=(w_out[0], w_ffn_gate[0], w_ffn_up[0], w_ffn_down[0]))
    y_prompt = _ffn(x_prompt, y_ssd, y_s5, mod, modf, n2g, nfg, wo, wg, wu, wd, tm=1024, sub=512,
                    seq_major=False, **seqs_p)

    rows_s = dseq * bs
    steps = lambda a: a.reshape(dseq, bs, a.shape[-1])
    flat = lambda a: a.reshape(1, rows_s, a.shape[-1])
    z, xbc, u5, dt = _inproj(x_sample, mod, n1g, w_in_p, tm=rows_s, sub=rows_s // 2, seq_major=True, **seqs_s)
    y_ssd, ssd_s, conv_s = _ssd_sample(steps(xbc), steps(z), steps(dt), state_conv[0].transpose(1, 0, 2),
                                       state_ssd[0], cw, cb, dtb, alog, dexp, ng)
    y_s5, re_s, im_s = _s5(steps(u5), state_s5_re[0].reshape(bs, nstate), state_s5_im[0].reshape(bs, nstate),
                           ar, ai, bblk, cblk, dsk, wglu, bglu, tl=dseq, batch_major=False)
    y_sample = _ffn(x_sample, flat(y_ssd), flat(y_s5), mod, modf, n2g, nfg, wo, wg, wu, wd,
                    tm=rows_s, sub=rows_s, seq_major=True, **seqs_s)

    g5 = (S5_GROUPS, S5_STATE)
    return (y_prompt, y_sample,
            ssd_p[None], ssd_s.reshape((1,) + state_ssd.shape[1:]),
            conv_p[None], conv_s.transpose(1, 0, 2)[None],
            re_p.reshape((1, bp) + g5), re_s.reshape((1, bs) + g5),
            im_p.reshape((1, bp) + g5), im_s.reshape((1, bs) + g5))
```

```python
import functools

import jax
import jax.numpy as jnp
from jax import lax
from jax.experimental import pallas as pl
from jax.experimental.pallas import tpu as pltpu

f32 = jnp.float32
bf16 = jnp.bfloat16

D_MODEL = 1024
D_SSD = 512
HEAD_DIM = 64
HEADS = 8
GROUPS = 2
HEADS_PER_GROUP = HEADS // GROUPS
STATE = 128
CONV_W = 4
CONV_DIM = D_SSD + 2 * GROUPS * STATE
D_S5 = 512
S5_CH = 16
S5_GROUPS = 32
S5_STATE = 64
D_FF = 2816
N_ADA = 6
EPS = 1e-6

LANES = 128
SUBLANES = 8
SSD_CHUNK = 128
S5_SLABS = D_S5 // LANES
S5_SLAB_STATE = (S5_GROUPS // S5_SLABS) * S5_STATE
DT_PAD = LANES
VMEM_LIMIT = 56 * 1024 * 1024

NT_DIMS = (((1,), (1,)), ((), ()))
TN_DIMS = (((0,), (0,)), ((), ()))


def _silu(x):
    return x * jax.nn.sigmoid(x)


def _interleave(*gens, lead=0):
    pending, live, rnd = list(gens), [], 0
    while pending or live:
        while pending and rnd >= lead * (len(gens) - len(pending)):
            live.append(pending.pop(0))
        for g in list(live):
            try:
                next(g)
            except StopIteration:
                live.remove(g)
        rnd += 1


def _by_row(fn, v, *ms):
    r, d = v.shape
    m_rows = ms[0].shape[0]
    if m_rows in (1, r):
        return fn(v, *ms)
    out = fn(v.reshape(r // m_rows, m_rows, d), *[m[None] for m in ms])
    return out.reshape(r, d)


def _rms_mod(x, g, sc, sh):
    y = x * lax.rsqrt(jnp.mean(x * x, axis=-1, keepdims=True) + EPS)
    return _by_row(lambda v, s, t: v * (1.0 + s) + t, y * g, sc, sh)


def _mod_row(mod_ref, i, per_row):
    return mod_ref[i] if per_row else mod_ref[i, pl.ds(pl.program_id(0), 1), :]


def _mod_spec(mod, n_seq, first_seq):
    assert first_seq % n_seq == 0
    return pl.BlockSpec((mod.shape[0], n_seq, mod.shape[2]), lambda *_: (0, first_seq // n_seq, 0))


def _const_spec(shape):
    nd = len(shape)
    return pl.BlockSpec(shape, lambda *_: (0,) * nd)


def _params(*sem):
    return pltpu.CompilerParams(dimension_semantics=sem, vmem_limit_bytes=VMEM_LIMIT)


ADA_K_BLOCK = 256


def _ada_kernel(ca_ref, cb_ref, *refs):
    nw = len(refs) // 3
    s = _silu(jnp.concatenate([ca_ref[...], cb_ref[...]], axis=0)).astype(bf16)
    for w_ref, b_ref, o_ref in zip(refs[:nw], refs[nw:2 * nw], refs[2 * nw:]):
        n, _, d = o_ref.shape

        @pl.when(pl.program_id(0) == 0)
        def _():
            for j in range(n):
                o_ref[j] = jnp.broadcast_to(b_ref[:, j * d:(j + 1) * d], o_ref.shape[1:])

        for j in range(n):
            o_ref[j] += jnp.dot(s, w_ref[:, j * d:(j + 1) * d].astype(bf16), preferred_element_type=f32)


def _ada_mod(ca, cb, ws, bs):
    k = ca.shape[1]
    m = ca.shape[0] + cb.shape[0]
    ns = [w.shape[1] // k for w in ws]
    return pl.pallas_call(
        _ada_kernel,
        grid=(k // ADA_K_BLOCK,),
        in_specs=[pl.BlockSpec((c.shape[0], ADA_K_BLOCK), lambda i: (0, i)) for c in (ca, cb)]
                 + [pl.BlockSpec((ADA_K_BLOCK, n * k), lambda i: (i, 0)) for n in ns]
                 + [_const_spec((1, n * k)) for n in ns],
        out_specs=[_const_spec((n, m, k)) for n in ns],
        out_shape=[jax.ShapeDtypeStruct((n, m, k), f32) for n in ns],
        compiler_params=_params("arbitrary"),
        name="ada_mod",
    )(ca, cb, *ws, *[b.reshape(1, n * k) for b, n in zip(bs, ns)])


def _rows_in(x_ref, seq_major):
    if not seq_major:
        return x_ref[0]
    s, t, d = x_ref.shape
    return jnp.swapaxes(x_ref[...], 0, 1).reshape(s * t, d)


def _conv_silu(ext, cw_ref, cb_ref):
    w = [cw_ref[k:k + 1, :] for k in range(CONV_W)]
    back1 = pltpu.roll(ext, 1, 0)
    older = pltpu.roll(ext * w[1] + back1 * w[0], 2, 0)
    conv = cb_ref[...] + ext[SUBLANES:, :] * w[3] + back1[SUBLANES:, :] * w[2] + older[SUBLANES:, :]
    return _silu(conv)


def _inproj_stages(x, sh, sc, g, outs, after=()):
    u = _rms_mod(x, g, sc, sh).astype(bf16)
    yield
    for write, w16 in outs:
        write(lax.dot_general(u, w16, NT_DIMS, preferred_element_type=f32))
        yield
    for stage in after:
        stage()
        yield


def _inproj_kernel(x_ref, mod_ref, g_ref, w_ref, *rest, seq_major, sub, conv):
    if conv:
        cw_ref, cb_ref, z_ref, act_ref, u5_ref, dt_ref, cn_ref, ext_ref = rest

        @pl.when(pl.program_id(1) == 0)
        def _():
            ext_ref[0:SUBLANES, :] = jnp.zeros((SUBLANES, CONV_DIM), f32)
    else:
        z_ref, xbc_ref, u5_ref, dt_ref = rest
    sh = _mod_row(mod_ref, 0, seq_major)
    sc = _mod_row(mod_ref, 1, seq_major)
    x = _rows_in(x_ref, seq_major)
    rows = x.shape[0]
    T = SSD_CHUNK
    o_dt = D_SSD + CONV_DIM
    w16 = lambda o, width: w_ref[o:o + width, :].astype(bf16)
    w_z, w_xbc, w_u5, w_dt = w16(0, D_SSD), w16(D_SSD, CONV_DIM), w16(o_dt + HEADS, D_S5), w16(o_dt, DT_PAD)

    def stages(r0):
        def to(ref):
            def write(v):
                ref[0, r0:r0 + sub, :] = v
            return write

        def to_ext(v):
            ext_ref[SUBLANES + r0:SUBLANES + r0 + sub, :] = v

        def conv_chunk(c0):
            def stage():
                act_ref[0, c0:c0 + T, :] = _conv_silu(ext_ref[c0:c0 + SUBLANES + T, :], cw_ref, cb_ref)
            return stage

        outs = [(to(z_ref), w_z), (to_ext if conv else to(xbc_ref), w_xbc), (to(u5_ref), w_u5), (to(dt_ref), w_dt)]
        after = [conv_chunk(c0) for c0 in range(r0, r0 + sub, T)] if conv else ()
        return _inproj_stages(x[r0:r0 + sub], sh, sc, g_ref[...], outs, after)

    n_matmul_stages = 5
    _interleave(*[stages(r0) for r0 in range(0, rows, sub)], lead=n_matmul_stages if conv else 0)
    if conv:
        cn_ref[0] = ext_ref[SUBLANES + rows - (CONV_W - 1):SUBLANES + rows, :]
        ext_ref[0:SUBLANES, :] = ext_ref[rows:rows + SUBLANES, :]


def _inproj(x, mod, g, w, conv_wb=None, *, tm, sub, seq_major, n_seq, first_seq):
    if seq_major:
        x_spec = _const_spec(x.shape)
        nb, rows, d = 1, x.shape[0] * x.shape[1], x.shape[2]
        assert tm == rows
    else:
        nb, rows, d = x.shape
        x_spec = pl.BlockSpec((1, tm, d), lambda i, j: (i, j, 0))
    nt = rows // tm
    widths = (D_SSD, CONV_DIM, D_S5, DT_PAD)
    conv = conv_wb is not None
    in_specs = [x_spec, _mod_spec(mod, n_seq, first_seq), _const_spec((1, d)),
                pl.BlockSpec(w.shape, lambda i, j: (0, 0), pipeline_mode=pl.Buffered(1))]
    out_specs = [pl.BlockSpec((1, tm, wd), lambda i, j: (i, j, 0)) for wd in widths]
    out_shape = [jax.ShapeDtypeStruct((nb, rows, wd), f32) for wd in widths]
    scratch = []
    if conv:
        in_specs += [_const_spec((CONV_W, CONV_DIM)), _const_spec((1, CONV_DIM))]
        out_specs.append(pl.BlockSpec((1, CONV_W - 1, CONV_DIM), lambda i, j: (i, 0, 0)))
        out_shape.append(jax.ShapeDtypeStruct((nb, CONV_W - 1, CONV_DIM), f32))
        scratch.append(pltpu.VMEM((SUBLANES + tm, CONV_DIM), f32))
    return pl.pallas_call(
        functools.partial(_inproj_kernel, seq_major=seq_major, sub=sub, conv=conv),
        grid=(nb, nt),
        in_specs=in_specs, out_specs=out_specs, out_shape=out_shape, scratch_shapes=scratch,
        compiler_params=_params("parallel", "arbitrary" if conv else "parallel"),
        name="in_proj",
    )(x, mod, g, w, *(conv_wb or ()))


def _split3(x):
    hi = x.astype(bf16)
    r1 = x - hi.astype(f32)
    mid = r1.astype(bf16)
    lo = (r1 - mid.astype(f32)).astype(bf16)
    return hi, mid, lo


def _dot_sel_lhs(sel16, x):
    return sum(jnp.dot(sel16, p, preferred_element_type=f32) for p in _split3(x))


def _dot_sel_rhs(x, sel16):
    hi, mid, _ = _split3(x)
    return jnp.dot(hi, sel16, preferred_element_type=f32) + jnp.dot(mid, sel16, preferred_element_type=f32)


def _gated_group_norm(y, z, ng):
    y = y * _silu(z)
    gw = D_SSD // GROUPS
    parts = []
    for g in range(GROUPS):
        yg = y[:, g * gw:(g + 1) * gw]
        parts.append(yg * lax.rsqrt(jnp.mean(yg * yg, axis=-1, keepdims=True) + EPS))
    return jnp.concatenate(parts, axis=-1) * ng


def _ssd_prompt_kernel(act_ref, z_ref, dt_ref, dtb_ref, alog_ref, dexp_ref, ng_ref, y_ref, st_ref, h_ref, *, cps):
    T = SSD_CHUNK
    gw = HEADS_PER_GROUP * HEAD_DIM

    @pl.when(pl.program_id(1) == 0)
    def _():
        h_ref[...] = jnp.zeros_like(h_ref)

    row = lax.broadcasted_iota(jnp.int32, (T, T), 0)
    col = lax.broadcasted_iota(jnp.int32, (T, T), 1)
    tri = row >= col
    tri16 = jnp.where(tri, 1.0, 0.0).astype(bf16)
    low_half = lax.broadcasted_iota(jnp.int32, (T, LANES), 1) < HEAD_DIM
    a_neg = -jnp.exp(alog_ref[...])

    n_fac = 3
    src = lax.broadcasted_iota(jnp.int32, (2 * DT_PAD, n_fac * D_SSD), 0) % DT_PAD
    dst = lax.broadcasted_iota(jnp.int32, (2 * DT_PAD, n_fac * D_SSD), 1)
    sel16 = jnp.where(src == dst // D_SSD * HEADS + dst % D_SSD // HEAD_DIM, 1.0, 0.0).astype(bf16)
    lane = lax.broadcasted_iota(jnp.int32, (T, DT_PAD), 1)

    def spread(q):
        hi = q.astype(bf16)
        mid = (q - hi.astype(f32)).astype(bf16)
        return jnp.dot(jnp.concatenate([hi, mid], axis=1), sel16, preferred_element_type=f32)

    def state_free(ci):
        r0 = ci * T
        act = act_ref[0, r0:r0 + T, :]
        xs = act[:, :D_SSD]
        bm = act[:, D_SSD:D_SSD + GROUPS * STATE]
        cm = act[:, D_SSD + GROUPS * STATE:]
        dtv = jax.nn.softplus(dt_ref[0, r0:r0 + T, :] + dtb_ref[...])
        a_cs = _dot_sel_lhs(tri16, dtv * a_neg)
        bg16s = [bm[:, g * STATE:(g + 1) * STATE].astype(bf16) for g in range(GROUPS)]
        cg16s = [cm[:, g * STATE:(g + 1) * STATE].astype(bf16) for g in range(GROUPS)]
        gmats = [lax.dot_general(cg16s[g], bg16s[g], NT_DIMS, preferred_element_type=f32)
                 for g in range(GROUPS)]
        yield
        a_last = a_cs[T - 1:T, :]
        a_cs_t = a_cs.T
        a_one = pltpu.roll(a_cs, HEADS, 1)
        a_two = pltpu.roll(a_cs, 2 * HEADS, 1)
        decay = jnp.exp(jnp.where(lane < 2 * HEADS, a_one[T - 1:T, :] - a_one, a_two))
        rep = spread(jnp.where(lane < HEADS, dtv, jnp.where(lane < n_fac * HEADS, decay, 0.0)))
        yield
        x = xs * rep[:, :D_SSD]
        x16 = x.astype(bf16)
        xd16 = (x * rep[:, D_SSD:2 * D_SSD]).astype(bf16)
        e_cs = rep[:, 2 * D_SSD:]
        yield
        y_part = []
        for g in range(GROUPS):
            for pr in range(HEADS_PER_GROUP // 2):
                s_pair = []
                for q in range(2):
                    h = g * HEADS_PER_GROUP + 2 * pr + q
                    seg = a_cs[:, h:h + 1] - a_cs_t[h:h + 1, :]
                    lmat = jnp.exp(jnp.where(tri, seg, -jnp.inf))
                    s_pair.append((gmats[g] * lmat).astype(bf16))
                lanes = slice(g * gw + pr * LANES, g * gw + (pr + 1) * LANES)
                both = jnp.dot(jnp.concatenate(s_pair, axis=0), x16[:, lanes], preferred_element_type=f32)
                yield
                y_part.append(jnp.where(low_half, both[:T], both[T:]))
        y_free = jnp.concatenate(y_part, axis=-1) + xs * dexp_ref[...]
        return y_free, e_cs, jnp.exp(a_last), xd16, bg16s, cg16s

    def state_step(ci, y_free, e_cs, e_last, xd16, bg16s, cg16s):
        r0 = ci * T
        y_offs = []
        for g in range(GROUPS):
            hp = h_ref[g]
            y_offs.append(lax.dot_general(cg16s[g], hp.astype(bf16), NT_DIMS, preferred_element_type=f32))
            upd = lax.dot_general(xd16[:, g * gw:(g + 1) * gw], bg16s[g], TN_DIMS, preferred_element_type=f32)
            for hh in range(HEADS_PER_GROUP):
                h = g * HEADS_PER_GROUP + hh
                rows = slice(hh * HEAD_DIM, (hh + 1) * HEAD_DIM)
                h_ref[g, rows, :] = e_last[:, h:h + 1] * hp[rows] + upd[rows]
        y = y_free + jnp.concatenate(y_offs, axis=-1) * e_cs
        y_ref[0, r0:r0 + T, :] = _gated_group_norm(y, z_ref[0, r0:r0 + T, :], ng_ref[...])

    gens = [state_free(ci) for ci in range(cps)]
    free = [None] * cps
    while any(f is None for f in free):
        for ci in range(cps):
            if free[ci] is None:
                try:
                    next(gens[ci])
                except StopIteration as done:
                    free[ci] = done.value
    for ci in range(cps):
        state_step(ci, *free[ci])

    @pl.when(pl.program_id(1) == pl.num_programs(1) - 1)
    def _():
        st_ref[0] = h_ref[...].reshape(HEADS, HEAD_DIM, STATE)


def _ssd_prompt(act, z, dt, dtb, alog, dexp, ng, *, cps=8):
    nb, L, _ = act.shape
    rows = cps * SSD_CHUNK
    blk = lambda wd: pl.BlockSpec((1, rows, wd), lambda b, c: (b, c, 0))
    return pl.pallas_call(
        functools.partial(_ssd_prompt_kernel, cps=cps),
        grid=(nb, L // rows),
        in_specs=[blk(CONV_DIM), blk(D_SSD), blk(DT_PAD), _const_spec((1, DT_PAD)), _const_spec((1, DT_PAD)),
                  _const_spec((1, D_SSD)), _const_spec((1, D_SSD))],
        out_specs=[blk(D_SSD), pl.BlockSpec((1, HEADS, HEAD_DIM, STATE), lambda b, c: (b, 0, 0, 0))],
        out_shape=[jax.ShapeDtypeStruct((nb, L, D_SSD), f32),
                   jax.ShapeDtypeStruct((nb, HEADS, HEAD_DIM, STATE), f32)],
        scratch_shapes=[pltpu.VMEM((GROUPS, HEADS_PER_GROUP * HEAD_DIM, STATE), f32)],
        compiler_params=_params("parallel", "arbitrary"),
        name="ssd_prompt",
    )(act, z, dt, dtb, alog, dexp, ng)


def _ssd_sample_kernel(xbc_ref, z_ref, dt_ref, cbuf_ref, st_ref, cw_ref, cb_ref, dtb_ref, alog_ref,
                       dexp_ref, ng_ref, y_ref, stn_ref, cn_ref, dtot_ref, *, L, nb):
    gw = HEADS_PER_GROUP * HEAD_DIM
    full = [cbuf_ref[j] for j in range(CONV_W - 1)] + [xbc_ref[t] for t in range(L)]
    for j in range(CONV_W - 1):
        cn_ref[j] = full[L + j]
    a_neg = -jnp.exp(alog_ref[...])
    xs, bm, cm, dtv, acs = [], [], [], [], []
    run = None
    for t in range(L):
        conv = cb_ref[...]
        for k in range(CONV_W):
            conv = conv + full[t + k] * cw_ref[k:k + 1, :]
        act = _silu(conv)
        xs.append(act[:, :D_SSD])
        bm.append(act[:, D_SSD:D_SSD + GROUPS * STATE])
        cm.append(act[:, D_SSD + GROUPS * STATE:])
        d = jax.nn.softplus(dt_ref[t] + dtb_ref[...])
        dtv.append(d)
        run = d * a_neg if run is None else run + d * a_neg
        acs.append(run)
    a_tot = acs[L - 1]
    dtot_ref[...] = jnp.exp(a_tot)

    lane_head = lax.broadcasted_iota(jnp.int32, (DT_PAD, D_SSD), 1) // HEAD_DIM
    sel16 = jnp.where(lax.broadcasted_iota(jnp.int32, (DT_PAD, D_SSD), 0) == lane_head, 1.0, 0.0).astype(bf16)
    pairs = [(t, s) for t in range(L) for s in range(t)]
    factors = (dtv + [jnp.exp(a_tot - acs[t]) for t in range(L)] + [jnp.exp(acs[t]) for t in range(L)]
               + [jnp.exp(acs[t] - acs[s]) for t, s in pairs])
    rep = _dot_sel_rhs(jnp.concatenate(factors, axis=0), sel16)
    piece = lambda i: rep[i * nb:(i + 1) * nb]
    x = [xs[t] * piece(t) for t in range(L)]
    xd_stack = jnp.concatenate([x[t] * piece(L + t) for t in range(L)], axis=0)
    e_cs = [piece(2 * L + t) for t in range(L)]
    decay = {ts: piece(3 * L + i) for i, ts in enumerate(pairs)}

    in_group0 = lax.broadcasted_iota(jnp.int32, (nb, D_SSD), 1) < gw
    y_intra = []
    for t in range(L):
        acc = None
        for s in range(t + 1):
            cb_dot = [jnp.sum(cm[t][:, g * STATE:(g + 1) * STATE] * bm[s][:, g * STATE:(g + 1) * STATE],
                              axis=-1, keepdims=True) for g in range(GROUPS)]
            w = jnp.where(in_group0, cb_dot[0], cb_dot[1])
            term = w * x[s] if s == t else w * decay[(t, s)] * x[s]
            acc = term if acc is None else acc + term
        y_intra.append(acc)

    c_stack = [jnp.concatenate([cm[t][:, g * STATE:(g + 1) * STATE] for t in range(L)], axis=0).astype(bf16)
               for g in range(GROUPS)]
    b_stack = [jnp.concatenate([bm[t][:, g * STATE:(g + 1) * STATE] for t in range(L)], axis=0).astype(bf16)
               for g in range(GROUPS)]
    seq_of_row = lax.broadcasted_iota(jnp.int32, (L * nb, gw), 0) & (nb - 1)

    def per_seq(b, yoff):
        mine = seq_of_row == b
        drow = dtot_ref[pl.ds(b, 1), :]
        out = []
        for g in range(GROUPS):
            h0 = st_ref[b, g]
            r = lax.dot_general(c_stack[g], h0.astype(bf16), NT_DIMS, preferred_element_type=f32)
            out.append(yoff[g] + jnp.where(mine, r, 0.0))
            xm = jnp.where(mine, xd_stack[:, g * gw:(g + 1) * gw], 0.0).astype(bf16)
            upd = lax.dot_general(xm, b_stack[g], TN_DIMS, preferred_element_type=f32)
            for hh in range(HEADS_PER_GROUP):
                h = g * HEADS_PER_GROUP + hh
                rows = slice(hh * HEAD_DIM, (hh + 1) * HEAD_DIM)
                stn_ref[b, g, rows, :] = drow[:, h:h + 1] * h0[rows] + upd[rows]
        return tuple(out)

    yoff = lax.fori_loop(0, nb, per_seq, tuple(jnp.zeros((L * nb, gw), f32) for _ in range(GROUPS)),
                         unroll=8)

    for t in range(L):
        y_off = jnp.concatenate([yoff[g][t * nb:(t + 1) * nb] for g in range(GROUPS)], axis=-1)
        y = y_intra[t] + y_off * e_cs[t] + xs[t] * dexp_ref[...]
        y_ref[t] = _gated_group_norm(y, z_ref[t], ng_ref[...])


def _ssd_sample(xbc, z, dt, cbuf, st, cw, cb, dtb, alog, dexp, ng, *, nb=16):
    L, B, _ = xbc.shape
    assert nb & (nb - 1) == 0 and B % nb == 0
    tblk = lambda n, wd: pl.BlockSpec((n, nb, wd), lambda i: (0, i, 0))
    gw = HEADS_PER_GROUP * HEAD_DIM
    st = st.reshape(B, GROUPS, gw, STATE)
    st_spec = pl.BlockSpec((nb, GROUPS, gw, STATE), lambda i: (i, 0, 0, 0))
    return pl.pallas_call(
        functools.partial(_ssd_sample_kernel, L=L, nb=nb),
        grid=(B // nb,),
        in_specs=[tblk(L, CONV_DIM), tblk(L, D_SSD), tblk(L, DT_PAD), tblk(CONV_W - 1, CONV_DIM), st_spec,
                  _const_spec((CONV_W, CONV_DIM)), _const_spec((1, CONV_DIM)),
                  _const_spec((1, DT_PAD)), _const_spec((1, DT_PAD)),
                  _const_spec((1, D_SSD)), _const_spec((1, D_SSD))],
        out_specs=[tblk(L, D_SSD), st_spec, tblk(CONV_W - 1, CONV_DIM)],
        out_shape=[jax.ShapeDtypeStruct((L, B, D_SSD), f32),
                   jax.ShapeDtypeStruct((B, GROUPS, gw, STATE), f32),
                   jax.ShapeDtypeStruct((CONV_W - 1, B, CONV_DIM), f32)],
        scratch_shapes=[pltpu.VMEM((nb, DT_PAD), f32)],
        compiler_params=_params("parallel"),
        name="ssd_sample",
    )(xbc, z, dt, cbuf, st, cw, cb, dtb, alog, dexp, ng)


def _s5_param_kernel(lr_ref, li_ref, ls_ref, br_ref, bi_ref, cr_ref, ci_ref,
                     ar_ref, ai_ref, bblk_ref, cblk_ref, b_scr, c_scr):
    lr, li = lr_ref[...], li_ref[...]
    step = jnp.exp(ls_ref[...])
    mag = jnp.exp(lr * step)
    abr = mag * jnp.cos(li * step)
    abi = mag * jnp.sin(li * step)
    nr, ni = abr - 1.0, abi
    den = lr * lr + li * li
    fr = (nr * lr + ni * li) / den
    fi = (ni * lr - nr * li) / den
    br, bi = br_ref[...], bi_ref[...]
    bbr = fr * br - fi * bi
    bbi = fr * bi + fi * br
    b_scr[...] = jnp.zeros_like(b_scr)
    c_scr[...] = jnp.zeros_like(c_scr)
    gps = S5_GROUPS // S5_SLABS
    ns = S5_SLAB_STATE
    for g in range(S5_GROUPS):
        s, gl = divmod(g, gps)
        ch = slice(gl * S5_CH, (gl + 1) * S5_CH)
        st = slice(gl * S5_STATE, (gl + 1) * S5_STATE)
        st_im = slice(ns + gl * S5_STATE, ns + (gl + 1) * S5_STATE)
        ar_ref[s, :, st] = jnp.broadcast_to(abr[g], (SUBLANES, S5_STATE))
        ai_ref[s, :, st] = jnp.broadcast_to(abi[g], (SUBLANES, S5_STATE))
        b_scr[s, ch, st] = bbr[g]
        b_scr[s, ch, st_im] = bbi[g]
        c_scr[s, st, ch] = cr_ref[g].T
        c_scr[s, st_im, ch] = -ci_ref[g].T
    bblk_ref[...] = b_scr[...].astype(bf16)
    cblk_ref[...] = c_scr[...].astype(bf16)


def _s5_params(a_re, a_im, log_step, b_re_t, b_im_t, c_re, c_im):
    g, p = a_re.shape
    ns = S5_SLAB_STATE
    return pl.pallas_call(
        _s5_param_kernel,
        out_shape=[jax.ShapeDtypeStruct((S5_SLABS, SUBLANES, ns), f32),
                   jax.ShapeDtypeStruct((S5_SLABS, SUBLANES, ns), f32),
                   jax.ShapeDtypeStruct((S5_SLABS, LANES, 2 * ns), bf16),
                   jax.ShapeDtypeStruct((S5_SLABS, 2 * ns, LANES), bf16)],
        scratch_shapes=[pltpu.VMEM((S5_SLABS, LANES, 2 * ns), f32),
                        pltpu.VMEM((S5_SLABS, 2 * ns, LANES), f32)],
        name="s5_params",
    )(a_re.reshape(g, 1, p), a_im.reshape(g, 1, p), log_step.reshape(g, 1, 1), b_re_t, b_im_t, c_re, c_im)


def _gelu(x):
    return 0.5 * x * (1.0 + lax.erf(x * (2.0 ** -0.5)))


def _s5_slabs_interleaved(ut_ref, hbuf_ref, hst_ref, g_ref, ar_ref, ai_ref, bblk_ref, cblk_ref, dsk_ref, tl,
                          row_block=256, n_stages=8):
    ns = S5_SLAB_STATE
    nb = SUBLANES
    rows = nb * tl

    def lanes(s):
        return slice(s * LANES, (s + 1) * LANES)

    def bu_stage(slabs):
        for s in slabs:
            for r in range(0, rows, row_block):
                us = ut_ref[s, r:r + row_block, :]
                hbuf_ref[s, nb + r:nb + r + row_block, :] = jnp.dot(us.astype(bf16), bblk_ref[s],
                                                                    preferred_element_type=f32)
                yield

    def scan_stage(slabs):
        carry = [(hst_ref[s, :, :ns], hst_ref[s, :, ns:]) for s in slabs]
        coef = [(ar_ref[s], ai_ref[s]) for s in slabs]
        for t in range(tl):
            r = nb + nb * t
            for i, s in enumerate(slabs):
                (pr, pi), (ar, ai) = carry[i], coef[i]
                nr = ar * pr - ai * pi + hbuf_ref[s, r:r + nb, :ns]
                ni = ar * pi + ai * pr + hbuf_ref[s, r:r + nb, ns:]
                hbuf_ref[s, r:r + nb, :ns] = nr
                hbuf_ref[s, r:r + nb, ns:] = ni
                carry[i] = (nr, ni)
            if (t + 1) % (tl // n_stages) == 0:
                yield
        for i, s in enumerate(slabs):
            hst_ref[s, :, :ns] = carry[i][0]
            hst_ref[s, :, ns:] = carry[i][1]

    def y_stage(slabs):
        for s in slabs:
            for r in range(0, rows, row_block):
                h16 = hbuf_ref[s, nb + r:nb + r + row_block, :].astype(bf16)
                ys = jnp.dot(h16, cblk_ref[s], preferred_element_type=f32)
                ys = ys + dsk_ref[:, lanes(s)] * ut_ref[s, r:r + row_block, :]
                g_ref[s, r:r + row_block, :] = _gelu(ys)
                yield

    half = S5_SLABS // 2
    first, second = tuple(range(half)), tuple(range(half, S5_SLABS))
    _interleave(bu_stage(first))
    _interleave(scan_stage(first), bu_stage(second))
    _interleave(scan_stage(second), y_stage(first))
    _interleave(y_stage(second))


def _s5_scan_vmem(hbuf_ref, hst_ref, ar_ref, ai_ref, s, nb, rows):
    ns = S5_SLAB_STATE
    hbuf_ref[s, 0:nb, :] = hst_ref[s]
    ar, ai = ar_ref[s], ai_ref[s]

    def body(i, carry):
        j = pl.multiple_of(nb + i * SUBLANES, SUBLANES)
        prev = hbuf_ref[s, pl.ds(j - nb, SUBLANES), :]
        cur = hbuf_ref[s, pl.ds(j, SUBLANES), :]
        pr, pi = prev[:, :ns], prev[:, ns:]
        hbuf_ref[s, pl.ds(j, SUBLANES), :ns] = ar * pr - ai * pi + cur[:, :ns]
        hbuf_ref[s, pl.ds(j, SUBLANES), ns:] = ar * pi + ai * pr + cur[:, ns:]
        return carry

    lax.fori_loop(0, rows // SUBLANES, body, 0, unroll=min(8, nb // SUBLANES))
    hst_ref[s] = hbuf_ref[s, rows:rows + nb, :]


S5_N_IN = 10


def _s5_kernel(*refs, nb, tl, batch_major, n_cast):
    (u_ref, re0_ref, im0_ref, ar_ref, ai_ref, bblk_ref, cblk_ref, dsk_ref, wglu_ref,
     bglu_ref) = refs[:S5_N_IN]
    cast_in = refs[S5_N_IN:S5_N_IN + n_cast]
    y_ref, ren_ref, imn_ref = refs[S5_N_IN + n_cast:S5_N_IN + n_cast + 3]
    cast_out = refs[S5_N_IN + n_cast + 3:S5_N_IN + 2 * n_cast + 3]
    ut_ref, hbuf_ref, hst_ref, g_ref = refs[S5_N_IN + 2 * n_cast + 3:]
    for w_ref, w16_ref in zip(cast_in, cast_out):
        w16_ref[...] = w_ref[...].astype(bf16)
    step = pl.program_id(0)
    rows = nb * tl
    ns = S5_SLAB_STATE

    @pl.when(step == 0)
    def _():
        for s in range(S5_SLABS):
            hst_ref[s, :, :ns] = re0_ref[:, s * ns:(s + 1) * ns]
            hst_ref[s, :, ns:] = im0_ref[:, s * ns:(s + 1) * ns]

    for s in range(S5_SLABS):
        sl = slice(s * LANES, (s + 1) * LANES)
        if batch_major:
            for b in range(nb):
                ut_ref[s, pl.ds(b, tl, stride=nb), :] = u_ref[b, :, sl]
        else:
            ut_ref[s] = u_ref[:, :, sl].reshape(rows, LANES)

    if nb == SUBLANES:
        _s5_slabs_interleaved(ut_ref, hbuf_ref, hst_ref, g_ref, ar_ref, ai_ref, bblk_ref, cblk_ref, dsk_ref, tl)
    else:
        for s in range(S5_SLABS):
            sl = slice(s * LANES, (s + 1) * LANES)
            hbuf_ref[s, nb:nb + rows, :] = jnp.dot(ut_ref[s].astype(bf16), bblk_ref[s],
                                                   preferred_element_type=f32)
            _s5_scan_vmem(hbuf_ref, hst_ref, ar_ref, ai_ref, s, nb, rows)
            ys = jnp.dot(hbuf_ref[s, nb:nb + rows, :].astype(bf16), cblk_ref[s], preferred_element_type=f32)
            g_ref[s] = _gelu(ys + dsk_ref[:, sl] * ut_ref[s])

    g = jnp.concatenate([g_ref[s] for s in range(S5_SLABS)], axis=-1)
    gate = jnp.dot(g.astype(bf16), wglu_ref[...], preferred_element_type=f32) + bglu_ref[...]
    out = g * jax.nn.sigmoid(gate)
    if batch_major:
        for s in range(S5_SLABS):
            sl = slice(s * LANES, (s + 1) * LANES)
            ut_ref[s] = out[:, sl]
            for b in range(nb):
                y_ref[b, :, sl] = ut_ref[s, pl.ds(b, tl, stride=nb), :]
    else:
        y_ref[...] = out.reshape(tl, nb, D_S5)

    @pl.when(step == pl.num_programs(0) - 1)
    def _():
        for s in range(S5_SLABS):
            ren_ref[:, s * ns:(s + 1) * ns] = hst_ref[s, :, :ns]
            imn_ref[:, s * ns:(s + 1) * ns] = hst_ref[s, :, ns:]


def _s5(u, re0, im0, ar, ai, bblk, cblk, dsk, wglu, bglu, *, tl, batch_major, cast=()):
    if batch_major:
        nb, L, _ = u.shape
        u_spec = pl.BlockSpec((nb, tl, D_S5), lambda i: (0, i, 0))
    else:
        L, nb, _ = u.shape
        u_spec = pl.BlockSpec((tl, nb, D_S5), lambda i: (i, 0, 0))
    rows = nb * tl
    steps = L // tl
    nstate = S5_GROUPS * S5_STATE
    st_spec = _const_spec((nb, nstate))
    cast_specs = [pl.BlockSpec((w.shape[0] // steps, w.shape[1]), lambda i: (i, 0)) for w in cast]
    assert all(w.shape[0] % (steps * 2 * SUBLANES) == 0 for w in cast)
    return pl.pallas_call(
        functools.partial(_s5_kernel, nb=nb, tl=tl, batch_major=batch_major, n_cast=len(cast)),
        grid=(steps,),
        in_specs=[u_spec, st_spec, st_spec,
                  _const_spec((S5_SLABS, SUBLANES, S5_SLAB_STATE)),
                  _const_spec((S5_SLABS, SUBLANES, S5_SLAB_STATE)),
                  _const_spec((S5_SLABS, LANES, 2 * S5_SLAB_STATE)),
                  _const_spec((S5_SLABS, 2 * S5_SLAB_STATE, LANES)),
                  _const_spec((1, D_S5)), _const_spec((D_S5, D_S5)), _const_spec((1, D_S5))] + cast_specs,
        out_specs=[u_spec, st_spec, st_spec] + cast_specs,
        out_shape=[jax.ShapeDtypeStruct(u.shape, f32),
                   jax.ShapeDtypeStruct((nb, nstate), f32),
                   jax.ShapeDtypeStruct((nb, nstate), f32)]
                  + [jax.ShapeDtypeStruct(w.shape, bf16) for w in cast],
        scratch_shapes=[pltpu.VMEM((S5_SLABS, rows, LANES), f32),
                        pltpu.VMEM((S5_SLABS, nb + rows, 2 * S5_SLAB_STATE), f32),
                        pltpu.VMEM((S5_SLABS, nb, 2 * S5_SLAB_STATE), f32),
                        pltpu.VMEM((S5_SLABS, rows, LANES), f32)],
        compiler_params=_params("arbitrary"),
        name="s5_mixer",
    )(u, re0, im0, ar, ai, bblk, cblk, dsk, wglu, bglu, *cast)


def _ffn_stages(x, ys, y5, mods, n2g, nfg, wo_ref, wg_ref, wu_ref, wd_ref, write_out, ff_chunk):
    g1, sh2, sc2, g2, shf, scf = mods
    att = jnp.dot(ys.astype(bf16), wo_ref[:D_SSD, :], preferred_element_type=f32)
    att = att + jnp.dot(y5.astype(bf16), wo_ref[D_SSD:, :], preferred_element_type=f32)
    yield
    x1 = x + _by_row(lambda v, s: v * s, att, g1)
    v = _rms_mod(x1, n2g, sc2, sh2).astype(bf16)
    ff = None
    for o in range(0, D_FF, ff_chunk):
        gate = jnp.dot(v, wg_ref[:, o:o + ff_chunk], preferred_element_type=f32)
        up = jnp.dot(v, wu_ref[:, o:o + ff_chunk], preferred_element_type=f32)
        hid = (_silu(gate) * up).astype(bf16)
        part = jnp.dot(hid, wd_ref[o:o + ff_chunk, :], preferred_element_type=f32)
        ff = part if ff is None else ff + part
        yield
    x2 = x1 + _by_row(lambda v, s: v * s, ff, g2)
    write_out(_rms_mod(x2, nfg, scf, shf))


def _ffn_kernel(x_ref, ys_ref, y5_ref, mod_ref, modf_ref, n2g_ref, nfg_ref, wo_ref, wg_ref, wu_ref, wd_ref,
                o_ref, *, seq_major, ff_chunk, sub):
    mods = (_mod_row(mod_ref, 2, seq_major), _mod_row(mod_ref, 3, seq_major), _mod_row(mod_ref, 4, seq_major),
            _mod_row(mod_ref, 5, seq_major), _mod_row(modf_ref, 0, seq_major), _mod_row(modf_ref, 1, seq_major))
    x = _rows_in(x_ref, seq_major)
    rows = x.shape[0]
    tiles = {}

    def stages(r0):
        def write_out(y):
            if seq_major:
                tiles[r0] = y
            else:
                o_ref[0, r0:r0 + sub, :] = y
        return _ffn_stages(x[r0:r0 + sub], ys_ref[0, r0:r0 + sub, :], y5_ref[0, r0:r0 + sub, :], mods,
                           n2g_ref[...], nfg_ref[...], wo_ref, wg_ref, wu_ref, wd_ref, write_out, ff_chunk)

    _interleave(*[stages(r0) for r0 in range(0, rows, sub)])
    if seq_major:
        s, t, d = o_ref.shape
        y = jnp.concatenate([tiles[r0] for r0 in range(0, rows, sub)], axis=0)
        o_ref[...] = jnp.swapaxes(y.reshape(t, s, d), 0, 1)


def _ffn(x, ys, y5, mod, modf, n2g, nfg, wo, wg, wu, wd, *, tm, sub, seq_major, n_seq, first_seq, ff_chunk=256):
    if seq_major:
        x_spec = _const_spec(x.shape)
        nb, rows, d = 1, x.shape[0] * x.shape[1], x.shape[2]
        assert tm == rows
    else:
        nb, rows, d = x.shape
        x_spec = pl.BlockSpec((1, tm, d), lambda i, j: (i, j, 0))
    nt = rows // tm
    mod_spec = _mod_spec(mod, n_seq, first_seq)
    modf_spec = _mod_spec(modf, n_seq, first_seq)
    blk = lambda wd_: pl.BlockSpec((1, tm, wd_), lambda i, j: (i, j, 0))
    single = dict(pipeline_mode=pl.Buffered(1))
    wspec = lambda shape: pl.BlockSpec(shape, lambda i, j: (0, 0), **single)
    return pl.pallas_call(
        functools.partial(_ffn_kernel, seq_major=seq_major, ff_chunk=ff_chunk, sub=sub),
        grid=(nb, nt),
        in_specs=[x_spec, blk(D_SSD), blk(D_S5), mod_spec, modf_spec,
                  _const_spec((1, d)), _const_spec((1, d)),
                  wspec((d, d)), wspec((d, D_FF)), wspec((d, D_FF)), wspec((D_FF, d))],
        out_specs=x_spec,
        out_shape=jax.ShapeDtypeStruct(x.shape, f32),
        compiler_params=_params("parallel", "parallel"),
        name="out_ffn",
    )(x, ys, y5, mod, modf, n2g, nfg, wo, wg, wu, wd)


def kernel(x_prompt, x_sample, c_prompt, c_sample, state_ssd, state_conv, state_s5_re, state_s5_im, w_ada, b_ada, norm1_g, w_in, conv_w, conv_b, ssd_dt_bias, ssd_A_log, ssd_D, ssd_norm_g, s5_A_re, s5_A_im, s5_log_step, s5_B_re, s5_B_im, s5_C_re, s5_C_im, s5_D, w_glu, b_glu, w_out, norm2_g, w_ffn_gate, w_ffn_up, w_ffn_down, w_ada_f, b_ada_f, normf_g):
    assert w_ada.shape[0] == 1, "single-layer stack"
    bp, seq, d = x_prompt.shape
    bs, dseq, _ = x_sample.shape

    mod, modf = _ada_mod(c_sample, c_prompt, (w_ada[0], w_ada_f), (b_ada[0], b_ada_f))
    seqs_s = dict(n_seq=bs, first_seq=0)
    seqs_p = dict(n_seq=bp, first_seq=bs)

    w_in_p = w_in[0].T
    pad_h = lambda v: jnp.concatenate([v, jnp.zeros((DT_PAD - HEADS,), f32)]).reshape(1, DT_PAD)
    dtb = pad_h(ssd_dt_bias[0])
    alog = pad_h(ssd_A_log[0])
    dexp = jnp.repeat(ssd_D[0], HEAD_DIM).reshape(1, D_SSD)
    ng = ssd_norm_g[0].reshape(1, D_SSD)
    cw, cb = conv_w[0], conv_b[0].reshape(1, CONV_DIM)

    ar, ai, bblk, cblk = _s5_params(s5_A_re[0], s5_A_im[0], s5_log_step[0],
                                    s5_B_re[0].transpose(0, 2, 1), s5_B_im[0].transpose(0, 2, 1),
                                    s5_C_re[0], s5_C_im[0])
    dsk = s5_D[0].reshape(1, D_S5)
    wglu = w_glu[0].astype(bf16)
    bglu = b_glu[0].reshape(1, D_S5)
    n1g, n2g, nfg = norm1_g[0].reshape(1, d), norm2_g[0].reshape(1, d), normf_g.reshape(1, d)
    nstate = S5_GROUPS * S5_STATE

    z, act, u5, dt, conv_p = _inproj(x_prompt, mod, n1g, w_in_p, (cw, cb), tm=1024, sub=512, seq_major=False,
                                     **seqs_p)
    y_ssd, ssd_p = _ssd_prompt(act, z, dt, dtb, alog, dexp, ng)
    zeros_st = jnp.zeros((bp, nstate), f32)
    y_s5, re_p, im_p, wo, wg, wu, wd = _s5(
        u5, zeros_st, zeros_st, ar, ai, bblk, cblk, dsk, wglu, bglu, tl=128, batch_major=True,
        cast=(w_out[0], w_ffn_gate[0], w_ffn_up[0], w_ffn_down[0]))
    y_prompt = _ffn(x_prompt, y_ssd, y_s5, mod, modf, n2g, nfg, wo, wg, wu, wd, tm=1024, sub=512,
                    seq_major=False, **seqs_p)

    rows_s = dseq * bs
    steps = lambda a: a.reshape(dseq, bs, a.shape[-1])
    flat = lambda a: a.reshape(1, rows_s, a.shape[-1])
    z, xbc, u5, dt = _inproj(x_sample, mod, n1g, w_in_p, tm=rows_s, sub=rows_s // 2, seq_major=True, **seqs_s)
    y_ssd, ssd_s, conv_s = _ssd_sample(steps(xbc), steps(z), steps(dt), state_conv[0].transpose(1, 0, 2),
                                       state_ssd[0], cw, cb, dtb, alog, dexp, ng)
    y_s5, re_s, im_s = _s5(steps(u5), state_s5_re[0].reshape(bs, nstate), state_s5_im[0].reshape(bs, nstate),
                           ar, ai, bblk, cblk, dsk, wglu, bglu, tl=dseq, batch_major=False)
    y_sample = _ffn(x_sample, flat(y_ssd), flat(y_s5), mod, modf, n2g, nfg, wo, wg, wu, wd,
                    tm=rows_s, sub=rows_s, seq_major=True, **seqs_s)

    g5 = (S5_GROUPS, S5_STATE)
    return (y_prompt, y_sample,
            ssd_p[None], ssd_s.reshape((1,) + state_ssd.shape[1:]),
            conv_p[None], conv_s.transpose(1, 0, 2)[None],
            re_p.reshape((1, bp) + g5), re_s.reshape((1, bs) + g5),
            im_p.reshape((1, bp) + g5), im_s.reshape((1, bs) + g5))
```

```python
import functools

import jax
import jax.numpy as jnp
from jax import lax
from jax.experimental import pallas as pl
from jax.experimental.pallas import tpu as pltpu

f32 = jnp.float32
bf16 = jnp.bfloat16

D_MODEL = 1024
D_SSD = 512
HEAD_DIM = 64
HEADS = 8
GROUPS = 2
HEADS_PER_GROUP = HEADS // GROUPS
STATE = 128
CONV_W = 4
CONV_DIM = D_SSD + 2 * GROUPS * STATE
D_S5 = 512
S5_CH = 16
S5_GROUPS = 32
S5_STATE = 64
D_FF = 2816
N_ADA = 6
EPS = 1e-6

LANES = 128
SUBLANES = 8
SSD_CHUNK = 128
S5_SLABS = D_S5 // LANES
S5_SLAB_STATE = (S5_GROUPS // S5_SLABS) * S5_STATE
DT_PAD = LANES
VMEM_LIMIT = 56 * 1024 * 1024

NT_DIMS = (((1,), (1,)), ((), ()))
TN_DIMS = (((0,), (0,)), ((), ()))


def _silu(x):
    h = 0.5 * x
    return h + h * jnp.tanh(h)


def _interleave(*gens, lead=0):
    pending, live, rnd = list(gens), [], 0
    while pending or live:
        while pending and rnd >= lead * (len(gens) - len(pending)):
            live.append(pending.pop(0))
        for g in list(live):
            try:
                next(g)
            except StopIteration:
                live.remove(g)
        rnd += 1


def _by_row(fn, v, *ms):
    r, d = v.shape
    m_rows = ms[0].shape[0]
    if m_rows in (1, r):
        return fn(v, *ms)
    out = fn(v.reshape(r // m_rows, m_rows, d), *[m[None] for m in ms])
    return out.reshape(r, d)


def _rms_mod(x, g, sc, sh):
    y = x * lax.rsqrt(jnp.mean(x * x, axis=-1, keepdims=True) + EPS)
    return _by_row(lambda v, s, t: v * (1.0 + s) + t, y * g, sc, sh)


def _mod_row(mod_ref, i, per_row):
    return mod_ref[i] if per_row else mod_ref[i, pl.ds(pl.program_id(0), 1), :]


def _mod_spec(mod, n_seq, first_seq):
    assert first_seq % n_seq == 0
    return pl.BlockSpec((mod.shape[0], n_seq, mod.shape[2]), lambda *_: (0, first_seq // n_seq, 0))


def _const_spec(shape):
    nd = len(shape)
    return pl.BlockSpec(shape, lambda *_: (0,) * nd)


def _params(*sem):
    return pltpu.CompilerParams(dimension_semantics=sem, vmem_limit_bytes=VMEM_LIMIT)


ADA_K_BLOCK = 256


def _ada_kernel(ca_ref, cb_ref, *refs):
    nw = len(refs) // 3
    s = _silu(jnp.concatenate([ca_ref[...], cb_ref[...]], axis=0)).astype(bf16)
    for w_ref, b_ref, o_ref in zip(refs[:nw], refs[nw:2 * nw], refs[2 * nw:]):
        n, _, d = o_ref.shape

        @pl.when(pl.program_id(0) == 0)
        def _():
            for j in range(n):
                o_ref[j] = jnp.broadcast_to(b_ref[:, j * d:(j + 1) * d], o_ref.shape[1:])

        for j in range(n):
            o_ref[j] += jnp.dot(s, w_ref[:, j * d:(j + 1) * d].astype(bf16), preferred_element_type=f32)


def _ada_mod(ca, cb, ws, bs):
    k = ca.shape[1]
    m = ca.shape[0] + cb.shape[0]
    ns = [w.shape[1] // k for w in ws]
    return pl.pallas_call(
        _ada_kernel,
        grid=(k // ADA_K_BLOCK,),
        in_specs=[pl.BlockSpec((c.shape[0], ADA_K_BLOCK), lambda i: (0, i)) for c in (ca, cb)]
                 + [pl.BlockSpec((ADA_K_BLOCK, n * k), lambda i: (i, 0)) for n in ns]
                 + [_const_spec((1, n * k)) for n in ns],
        out_specs=[_const_spec((n, m, k)) for n in ns],
        out_shape=[jax.ShapeDtypeStruct((n, m, k), f32) for n in ns],
        compiler_params=_params("arbitrary"),
        name="ada_mod",
    )(ca, cb, *ws, *[b.reshape(1, n * k) for b, n in zip(bs, ns)])


def _rows_in(x_ref, seq_major):
    if not seq_major:
        return x_ref[0]
    s, t, d = x_ref.shape
    return jnp.swapaxes(x_ref[...], 0, 1).reshape(s * t, d)


def _conv_silu(ext, cw_ref, cb_ref):
    w = [cw_ref[k:k + 1, :] for k in range(CONV_W)]
    back1 = pltpu.roll(ext, 1, 0)
    older = pltpu.roll(ext * w[1] + back1 * w[0], 2, 0)
    conv = cb_ref[...] + ext[SUBLANES:, :] * w[3] + back1[SUBLANES:, :] * w[2] + older[SUBLANES:, :]
    return _silu(conv)


def _inproj_stages(x, sh, sc, g, outs, after=()):
    u = _rms_mod(x, g, sc, sh).astype(bf16)
    yield
    for write, w16 in outs:
        write(lax.dot_general(u, w16, NT_DIMS, preferred_element_type=f32))
        yield
    for stage in after:
        stage()
        yield


def _inproj_kernel(x_ref, mod_ref, g_ref, w_ref, *rest, seq_major, sub, conv):
    if conv:
        cw_ref, cb_ref, z_ref, act_ref, u5_ref, dt_ref, cn_ref, ext_ref = rest

        @pl.when(pl.program_id(1) == 0)
        def _():
            ext_ref[0:SUBLANES, :] = jnp.zeros((SUBLANES, CONV_DIM), f32)
    else:
        z_ref, xbc_ref, u5_ref, dt_ref = rest
    sh = _mod_row(mod_ref, 0, seq_major)
    sc = _mod_row(mod_ref, 1, seq_major)
    x = _rows_in(x_ref, seq_major)
    rows = x.shape[0]
    T = SSD_CHUNK
    o_dt = D_SSD + CONV_DIM
    w16 = lambda o, width: w_ref[o:o + width, :].astype(bf16)
    w_z, w_xbc, w_u5, w_dt = w16(0, D_SSD), w16(D_SSD, CONV_DIM), w16(o_dt + HEADS, D_S5), w16(o_dt, DT_PAD)

    def stages(r0):
        def to(ref):
            def write(v):
                ref[0, r0:r0 + sub, :] = v
            return write

        def to_ext(v):
            ext_ref[SUBLANES + r0:SUBLANES + r0 + sub, :] = v

        def conv_chunk(c0):
            def stage():
                act_ref[0, c0:c0 + T, :] = _conv_silu(ext_ref[c0:c0 + SUBLANES + T, :], cw_ref, cb_ref)
            return stage

        outs = [(to(z_ref), w_z), (to_ext if conv else to(xbc_ref), w_xbc), (to(u5_ref), w_u5), (to(dt_ref), w_dt)]
        after = [conv_chunk(c0) for c0 in range(r0, r0 + sub, T)] if conv else ()
        return _inproj_stages(x[r0:r0 + sub], sh, sc, g_ref[...], outs, after)

    n_matmul_stages = 5
    _interleave(*[stages(r0) for r0 in range(0, rows, sub)], lead=n_matmul_stages if conv else 0)
    if conv:
        cn_ref[0] = ext_ref[SUBLANES + rows - (CONV_W - 1):SUBLANES + rows, :]
        ext_ref[0:SUBLANES, :] = ext_ref[rows:rows + SUBLANES, :]


def _inproj(x, mod, g, w, conv_wb=None, *, tm, sub, seq_major, n_seq, first_seq):
    if seq_major:
        x_spec = _const_spec(x.shape)
        nb, rows, d = 1, x.shape[0] * x.shape[1], x.shape[2]
        assert tm == rows
    else:
        nb, rows, d = x.shape
        x_spec = pl.BlockSpec((1, tm, d), lambda i, j: (i, j, 0))
    nt = rows // tm
    widths = (D_SSD, CONV_DIM, D_S5, DT_PAD)
    conv = conv_wb is not None
    in_specs = [x_spec, _mod_spec(mod, n_seq, first_seq), _const_spec((1, d)),
                pl.BlockSpec(w.shape, lambda i, j: (0, 0), pipeline_mode=pl.Buffered(1))]
    out_specs = [pl.BlockSpec((1, tm, wd), lambda i, j: (i, j, 0)) for wd in widths]
    out_shape = [jax.ShapeDtypeStruct((nb, rows, wd), f32) for wd in widths]
    scratch = []
    if conv:
        in_specs += [_const_spec((CONV_W, CONV_DIM)), _const_spec((1, CONV_DIM))]
        out_specs.append(pl.BlockSpec((1, CONV_W - 1, CONV_DIM), lambda i, j: (i, 0, 0)))
        out_shape.append(jax.ShapeDtypeStruct((nb, CONV_W - 1, CONV_DIM), f32))
        scratch.append(pltpu.VMEM((SUBLANES + tm, CONV_DIM), f32))
    return pl.pallas_call(
        functools.partial(_inproj_kernel, seq_major=seq_major, sub=sub, conv=conv),
        grid=(nb, nt),
        in_specs=in_specs, out_specs=out_specs, out_shape=out_shape, scratch_shapes=scratch,
        compiler_params=_params("parallel", "arbitrary" if conv else "parallel"),
        name="in_proj",
    )(x, mod, g, w, *(conv_wb or ()))


def _split3(x):
    hi = x.astype(bf16)
    r1 = x - hi.astype(f32)
    mid = r1.astype(bf16)
    lo = (r1 - mid.astype(f32)).astype(bf16)
    return hi, mid, lo


def _dot_sel_lhs(sel16, x):
    return sum(jnp.dot(sel16, p, preferred_element_type=f32) for p in _split3(x))


def _dot_sel_rhs(x, sel16):
    hi, mid, _ = _split3(x)
    return jnp.dot(hi, sel16, preferred_element_type=f32) + jnp.dot(mid, sel16, preferred_element_type=f32)


def _gated_group_norm(y, z, ng):
    y = y * _silu(z)
    gw = D_SSD // GROUPS
    parts = []
    for g in range(GROUPS):
        yg = y[:, g * gw:(g + 1) * gw]
        parts.append(yg * lax.rsqrt(jnp.mean(yg * yg, axis=-1, keepdims=True) + EPS))
    return jnp.concatenate(parts, axis=-1) * ng


def _ssd_prompt_kernel(act_ref, z_ref, dt_ref, dtb_ref, alog_ref, dexp_ref, ng_ref, y_ref, st_ref, h_ref, *, cps):
    T = SSD_CHUNK
    gw = HEADS_PER_GROUP * HEAD_DIM

    @pl.when(pl.program_id(1) == 0)
    def _():
        h_ref[...] = jnp.zeros_like(h_ref)

    row = lax.broadcasted_iota(jnp.int32, (T, T), 0)
    col = lax.broadcasted_iota(jnp.int32, (T, T), 1)
    tri = row >= col
    tri16 = jnp.where(tri, 1.0, 0.0).astype(bf16)
    low_half = lax.broadcasted_iota(jnp.int32, (T, LANES), 1) < HEAD_DIM
    a_neg = -jnp.exp(alog_ref[...])

    n_fac = 3
    src = lax.broadcasted_iota(jnp.int32, (2 * DT_PAD, n_fac * D_SSD), 0) % DT_PAD
    dst = lax.broadcasted_iota(jnp.int32, (2 * DT_PAD, n_fac * D_SSD), 1)
    sel16 = jnp.where(src == dst // D_SSD * HEADS + dst % D_SSD // HEAD_DIM, 1.0, 0.0).astype(bf16)
    lane = lax.broadcasted_iota(jnp.int32, (T, DT_PAD), 1)

    def spread(q):
        hi = q.astype(bf16)
        mid = (q - hi.astype(f32)).astype(bf16)
        return jnp.dot(jnp.concatenate([hi, mid], axis=1), sel16, preferred_element_type=f32)

    def state_free(ci):
        r0 = ci * T
        act = act_ref[0, r0:r0 + T, :]
        xs = act[:, :D_SSD]
        bm = act[:, D_SSD:D_SSD + GROUPS * STATE]
        cm = act[:, D_SSD + GROUPS * STATE:]
        dtv = jax.nn.softplus(dt_ref[0, r0:r0 + T, :] + dtb_ref[...])
        a_cs = _dot_sel_lhs(tri16, dtv * a_neg)
        bg16s = [bm[:, g * STATE:(g + 1) * STATE].astype(bf16) for g in range(GROUPS)]
        cg16s = [cm[:, g * STATE:(g + 1) * STATE].astype(bf16) for g in range(GROUPS)]
        gmats = [lax.dot_general(cg16s[g], bg16s[g], NT_DIMS, preferred_element_type=f32)
                 for g in range(GROUPS)]
        yield
        a_last = a_cs[T - 1:T, :]
        a_cs_t = a_cs.T
        a_one = pltpu.roll(a_cs, HEADS, 1)
        a_two = pltpu.roll(a_cs, 2 * HEADS, 1)
        decay = jnp.exp(jnp.where(lane < 2 * HEADS, a_one[T - 1:T, :] - a_one, a_two))
        rep = spread(jnp.where(lane < HEADS, dtv, jnp.where(lane < n_fac * HEADS, decay, 0.0)))
        yield
        x = xs * rep[:, :D_SSD]
        x16 = x.astype(bf16)
        xd16 = (x * rep[:, D_SSD:2 * D_SSD]).astype(bf16)
        e_cs = rep[:, 2 * D_SSD:]
        yield
        y_part = []
        for g in range(GROUPS):
            for pr in range(HEADS_PER_GROUP // 2):
                s_pair = []
                for q in range(2):
                    h = g * HEADS_PER_GROUP + 2 * pr + q
                    seg = a_cs[:, h:h + 1] - a_cs_t[h:h + 1, :]
                    lmat = jnp.exp(jnp.where(tri, seg, -jnp.inf))
                    s_pair.append((gmats[g] * lmat).astype(bf16))
                xp = x16[:, g * gw + pr * LANES:g * gw + (pr + 1) * LANES]
                zero = jnp.zeros_like(xp)
                x_pair = jnp.concatenate([jnp.where(low_half, xp, zero), jnp.where(low_half, zero, xp)], axis=0)
                y_part.append(jnp.dot(jnp.concatenate(s_pair, axis=1), x_pair, preferred_element_type=f32))
                yield
        y_free = jnp.concatenate(y_part, axis=-1) + xs * dexp_ref[...]
        return y_free, e_cs, jnp.exp(a_last), xd16, bg16s, cg16s

    def state_step(ci, y_free, e_cs, e_last, xd16, bg16s, cg16s):
        r0 = ci * T
        y_offs = []
        for g in range(GROUPS):
            hp = h_ref[g]
            y_offs.append(lax.dot_general(cg16s[g], hp.astype(bf16), NT_DIMS, preferred_element_type=f32))
            upd = lax.dot_general(xd16[:, g * gw:(g + 1) * gw], bg16s[g], TN_DIMS, preferred_element_type=f32)
            for hh in range(HEADS_PER_GROUP):
                h = g * HEADS_PER_GROUP + hh
                rows = slice(hh * HEAD_DIM, (hh + 1) * HEAD_DIM)
                h_ref[g, rows, :] = e_last[:, h:h + 1] * hp[rows] + upd[rows]
        y = y_free + jnp.concatenate(y_offs, axis=-1) * e_cs
        y_ref[0, r0:r0 + T, :] = _gated_group_norm(y, z_ref[0, r0:r0 + T, :], ng_ref[...])

    gens = [state_free(ci) for ci in range(cps)]
    free = [None] * cps
    while any(f is None for f in free):
        for ci in range(cps):
            if free[ci] is None:
                try:
                    next(gens[ci])
                except StopIteration as done:
                    free[ci] = done.value
    for ci in range(cps):
        state_step(ci, *free[ci])

    @pl.when(pl.program_id(1) == pl.num_programs(1) - 1)
    def _():
        st_ref[0] = h_ref[...].reshape(HEADS, HEAD_DIM, STATE)


def _ssd_prompt(act, z, dt, dtb, alog, dexp, ng, *, cps=8):
    nb, L, _ = act.shape
    rows = cps * SSD_CHUNK
    blk = lambda wd: pl.BlockSpec((1, rows, wd), lambda b, c: (b, c, 0))
    return pl.pallas_call(
        functools.partial(_ssd_prompt_kernel, cps=cps),
        grid=(nb, L // rows),
        in_specs=[blk(CONV_DIM), blk(D_SSD), blk(DT_PAD), _const_spec((1, DT_PAD)), _const_spec((1, DT_PAD)),
                  _const_spec((1, D_SSD)), _const_spec((1, D_SSD))],
        out_specs=[blk(D_SSD), pl.BlockSpec((1, HEADS, HEAD_DIM, STATE), lambda b, c: (b, 0, 0, 0))],
        out_shape=[jax.ShapeDtypeStruct((nb, L, D_SSD), f32),
                   jax.ShapeDtypeStruct((nb, HEADS, HEAD_DIM, STATE), f32)],
        scratch_shapes=[pltpu.VMEM((GROUPS, HEADS_PER_GROUP * HEAD_DIM, STATE), f32)],
        compiler_params=_params("parallel", "arbitrary"),
        name="ssd_prompt",
    )(act, z, dt, dtb, alog, dexp, ng)


def _ssd_sample_kernel(xbc_ref, z_ref, dt_ref, cbuf_ref, st_ref, cw_ref, cb_ref, dtb_ref, alog_ref,
                       dexp_ref, ng_ref, y_ref, stn_ref, cn_ref, dtot_ref, *, L, nb):
    gw = HEADS_PER_GROUP * HEAD_DIM
    full = [cbuf_ref[j] for j in range(CONV_W - 1)] + [xbc_ref[t] for t in range(L)]
    for j in range(CONV_W - 1):
        cn_ref[j] = full[L + j]
    a_neg = -jnp.exp(alog_ref[...])
    xs, bm, cm, dtv, acs = [], [], [], [], []
    run = None
    for t in range(L):
        conv = cb_ref[...]
        for k in range(CONV_W):
            conv = conv + full[t + k] * cw_ref[k:k + 1, :]
        act = _silu(conv)
        xs.append(act[:, :D_SSD])
        bm.append(act[:, D_SSD:D_SSD + GROUPS * STATE])
        cm.append(act[:, D_SSD + GROUPS * STATE:])
        d = jax.nn.softplus(dt_ref[t] + dtb_ref[...])
        dtv.append(d)
        run = d * a_neg if run is None else run + d * a_neg
        acs.append(run)
    a_tot = acs[L - 1]
    dtot_ref[...] = jnp.exp(a_tot)

    lane_head = lax.broadcasted_iota(jnp.int32, (DT_PAD, D_SSD), 1) // HEAD_DIM
    sel16 = jnp.where(lax.broadcasted_iota(jnp.int32, (DT_PAD, D_SSD), 0) == lane_head, 1.0, 0.0).astype(bf16)
    pairs = [(t, s) for t in range(L) for s in range(t)]
    factors = (dtv + [jnp.exp(a_tot - acs[t]) for t in range(L)] + [jnp.exp(acs[t]) for t in range(L)]
               + [jnp.exp(acs[t] - acs[s]) for t, s in pairs])
    rep = _dot_sel_rhs(jnp.concatenate(factors, axis=0), sel16)
    piece = lambda i: rep[i * nb:(i + 1) * nb]
    x = [xs[t] * piece(t) for t in range(L)]
    xd_stack = jnp.concatenate([x[t] * piece(L + t) for t in range(L)], axis=0)
    e_cs = [piece(2 * L + t) for t in range(L)]
    decay = {ts: piece(3 * L + i) for i, ts in enumerate(pairs)}

    in_group0 = lax.broadcasted_iota(jnp.int32, (nb, D_SSD), 1) < gw
    y_intra = []
    for t in range(L):
        acc = None
        for s in range(t + 1):
            cb_dot = [jnp.sum(cm[t][:, g * STATE:(g + 1) * STATE] * bm[s][:, g * STATE:(g + 1) * STATE],
                              axis=-1, keepdims=True) for g in range(GROUPS)]
            w = jnp.where(in_group0, cb_dot[0], cb_dot[1])
            term = w * x[s] if s == t else w * decay[(t, s)] * x[s]
            acc = term if acc is None else acc + term
        y_intra.append(acc)

    c_stack = [jnp.concatenate([cm[t][:, g * STATE:(g + 1) * STATE] for t in range(L)], axis=0).astype(bf16)
               for g in range(GROUPS)]
    b_stack = [jnp.concatenate([bm[t][:, g * STATE:(g + 1) * STATE] for t in range(L)], axis=0).astype(bf16)
               for g in range(GROUPS)]
    seq_of_row = lax.broadcasted_iota(jnp.int32, (L * nb, gw), 0) & (nb - 1)

    def per_seq(b, yoff):
        mine = seq_of_row == b
        drow = dtot_ref[pl.ds(b, 1), :]
        out = []
        for g in range(GROUPS):
            h0 = st_ref[b, g]
            r = lax.dot_general(c_stack[g], h0.astype(bf16), NT_DIMS, preferred_element_type=f32)
            out.append(yoff[g] + jnp.where(mine, r, 0.0))
            xm = jnp.where(mine, xd_stack[:, g * gw:(g + 1) * gw], 0.0).astype(bf16)
            upd = lax.dot_general(xm, b_stack[g], TN_DIMS, preferred_element_type=f32)
            for hh in range(HEADS_PER_GROUP):
                h = g * HEADS_PER_GROUP + hh
                rows = slice(hh * HEAD_DIM, (hh + 1) * HEAD_DIM)
                stn_ref[b, g, rows, :] = drow[:, h:h + 1] * h0[rows] + upd[rows]
        return tuple(out)

    yoff = lax.fori_loop(0, nb, per_seq, tuple(jnp.zeros((L * nb, gw), f32) for _ in range(GROUPS)),
                         unroll=8)

    for t in range(L):
        y_off = jnp.concatenate([yoff[g][t * nb:(t + 1) * nb] for g in range(GROUPS)], axis=-1)
        y = y_intra[t] + y_off * e_cs[t] + xs[t] * dexp_ref[...]
        y_ref[t] = _gated_group_norm(y, z_ref[t], ng_ref[...])


def _ssd_sample(xbc, z, dt, cbuf, st, cw, cb, dtb, alog, dexp, ng, *, nb=16):
    L, B, _ = xbc.shape
    assert nb & (nb - 1) == 0 and B % nb == 0
    tblk = lambda n, wd: pl.BlockSpec((n, nb, wd), lambda i: (0, i, 0))
    gw = HEADS_PER_GROUP * HEAD_DIM
    st = st.reshape(B, GROUPS, gw, STATE)
    st_spec = pl.BlockSpec((nb, GROUPS, gw, STATE), lambda i: (i, 0, 0, 0))
    return pl.pallas_call(
        functools.partial(_ssd_sample_kernel, L=L, nb=nb),
        grid=(B // nb,),
        in_specs=[tblk(L, CONV_DIM), tblk(L, D_SSD), tblk(L, DT_PAD), tblk(CONV_W - 1, CONV_DIM), st_spec,
                  _const_spec((CONV_W, CONV_DIM)), _const_spec((1, CONV_DIM)),
                  _const_spec((1, DT_PAD)), _const_spec((1, DT_PAD)),
                  _const_spec((1, D_SSD)), _const_spec((1, D_SSD))],
        out_specs=[tblk(L, D_SSD), st_spec, tblk(CONV_W - 1, CONV_DIM)],
        out_shape=[jax.ShapeDtypeStruct((L, B, D_SSD), f32),
                   jax.ShapeDtypeStruct((B, GROUPS, gw, STATE), f32),
                   jax.ShapeDtypeStruct((CONV_W - 1, B, CONV_DIM), f32)],
        scratch_shapes=[pltpu.VMEM((nb, DT_PAD), f32)],
        compiler_params=_params("parallel"),
        name="ssd_sample",
    )(xbc, z, dt, cbuf, st, cw, cb, dtb, alog, dexp, ng)


def _s5_param_kernel(lr_ref, li_ref, ls_ref, br_ref, bi_ref, cr_ref, ci_ref,
                     ar_ref, ai_ref, bblk_ref, cblk_ref, b_scr, c_scr):
    lr, li = lr_ref[...], li_ref[...]
    step = jnp.exp(ls_ref[...])
    mag = jnp.exp(lr * step)
    abr = mag * jnp.cos(li * step)
    abi = mag * jnp.sin(li * step)
    nr, ni = abr - 1.0, abi
    den = lr * lr + li * li
    fr = (nr * lr + ni * li) / den
    fi = (ni * lr - nr * li) / den
    br, bi = br_ref[...], bi_ref[...]
    bbr = fr * br - fi * bi
    bbi = fr * bi + fi * br
    b_scr[...] = jnp.zeros_like(b_scr)
    c_scr[...] = jnp.zeros_like(c_scr)
    gps = S5_GROUPS // S5_SLABS
    ns = S5_SLAB_STATE
    for g in range(S5_GROUPS):
        s, gl = divmod(g, gps)
        ch = slice(gl * S5_CH, (gl + 1) * S5_CH)
        st = slice(gl * S5_STATE, (gl + 1) * S5_STATE)
        st_im = slice(ns + gl * S5_STATE, ns + (gl + 1) * S5_STATE)
        ar_ref[s, :, st] = jnp.broadcast_to(abr[g], (SUBLANES, S5_STATE))
        ai_ref[s, :, st] = jnp.broadcast_to(abi[g], (SUBLANES, S5_STATE))
        b_scr[s, ch, st] = bbr[g]
        b_scr[s, ch, st_im] = bbi[g]
        c_scr[s, st, ch] = cr_ref[g].T
        c_scr[s, st_im, ch] = -ci_ref[g].T
    bblk_ref[...] = b_scr[...].astype(bf16)
    cblk_ref[...] = c_scr[...].astype(bf16)


def _s5_params(a_re, a_im, log_step, b_re_t, b_im_t, c_re, c_im):
    g, p = a_re.shape
    ns = S5_SLAB_STATE
    return pl.pallas_call(
        _s5_param_kernel,
        out_shape=[jax.ShapeDtypeStruct((S5_SLABS, SUBLANES, ns), f32),
                   jax.ShapeDtypeStruct((S5_SLABS, SUBLANES, ns), f32),
                   jax.ShapeDtypeStruct((S5_SLABS, LANES, 2 * ns), bf16),
                   jax.ShapeDtypeStruct((S5_SLABS, 2 * ns, LANES), bf16)],
        scratch_shapes=[pltpu.VMEM((S5_SLABS, LANES, 2 * ns), f32),
                        pltpu.VMEM((S5_SLABS, 2 * ns, LANES), f32)],
        name="s5_params",
    )(a_re.reshape(g, 1, p), a_im.reshape(g, 1, p), log_step.reshape(g, 1, 1), b_re_t, b_im_t, c_re, c_im)


def _gelu(x):
    return 0.5 * x * (1.0 + lax.erf(x * (2.0 ** -0.5)))


def _s5_slabs_interleaved(ut_ref, hbuf_ref, hst_ref, g_ref, ar_ref, ai_ref, bblk_ref, cblk_ref, dsk_ref, tl,
                          row_block=256, n_stages=8):
    ns = S5_SLAB_STATE
    nb = SUBLANES
    rows = nb * tl

    def lanes(s):
        return slice(s * LANES, (s + 1) * LANES)

    def bu_stage(slabs):
        for s in slabs:
            for r in range(0, rows, row_block):
                us = ut_ref[s, r:r + row_block, :]
                hbuf_ref[s, nb + r:nb + r + row_block, :] = jnp.dot(us.astype(bf16), bblk_ref[s],
                                                                    preferred_element_type=f32)
                yield

    def scan_stage(slabs):
        carry = [(hst_ref[s, :, :ns], hst_ref[s, :, ns:]) for s in slabs]
        coef = [(ar_ref[s], ai_ref[s]) for s in slabs]
        for t in range(tl):
            r = nb + nb * t
            for i, s in enumerate(slabs):
                (pr, pi), (ar, ai) = carry[i], coef[i]
                nr = ar * pr - ai * pi + hbuf_ref[s, r:r + nb, :ns]
                ni = ar * pi + ai * pr + hbuf_ref[s, r:r + nb, ns:]
                hbuf_ref[s, r:r + nb, :ns] = nr
                hbuf_ref[s, r:r + nb, ns:] = ni
                carry[i] = (nr, ni)
            if (t + 1) % (tl // n_stages) == 0:
                yield
        for i, s in enumerate(slabs):
            hst_ref[s, :, :ns] = carry[i][0]
            hst_ref[s, :, ns:] = carry[i][1]

    def y_stage(slabs):
        for s in slabs:
            for r in range(0, rows, row_block):
                h16 = hbuf_ref[s, nb + r:nb + r + row_block, :].astype(bf16)
                ys = jnp.dot(h16, cblk_ref[s], preferred_element_type=f32)
                ys = ys + dsk_ref[:, lanes(s)] * ut_ref[s, r:r + row_block, :]
                g_ref[s, r:r + row_block, :] = _gelu(ys)
                yield

    half = S5_SLABS // 2
    first, second = tuple(range(half)), tuple(range(half, S5_SLABS))
    _interleave(bu_stage(first))
    _interleave(scan_stage(first), bu_stage(second))
    _interleave(scan_stage(second), y_stage(first))
    _interleave(y_stage(second))


def _s5_scan_vmem(hbuf_ref, hst_ref, ar_ref, ai_ref, s, nb, rows):
    ns = S5_SLAB_STATE
    hbuf_ref[s, 0:nb, :] = hst_ref[s]
    ar, ai = ar_ref[s], ai_ref[s]

    def body(i, carry):
        j = pl.multiple_of(nb + i * SUBLANES, SUBLANES)
        prev = hbuf_ref[s, pl.ds(j - nb, SUBLANES), :]
        cur = hbuf_ref[s, pl.ds(j, SUBLANES), :]
        pr, pi = prev[:, :ns], prev[:, ns:]
        hbuf_ref[s, pl.ds(j, SUBLANES), :ns] = ar * pr - ai * pi + cur[:, :ns]
        hbuf_ref[s, pl.ds(j, SUBLANES), ns:] = ar * pi + ai * pr + cur[:, ns:]
        return carry

    lax.fori_loop(0, rows // SUBLANES, body, 0, unroll=min(8, nb // SUBLANES))
    hst_ref[s] = hbuf_ref[s, rows:rows + nb, :]


S5_N_IN = 10


def _s5_kernel(*refs, nb, tl, batch_major, n_cast):
    (u_ref, re0_ref, im0_ref, ar_ref, ai_ref, bblk_ref, cblk_ref, dsk_ref, wglu_ref,
     bglu_ref) = refs[:S5_N_IN]
    cast_in = refs[S5_N_IN:S5_N_IN + n_cast]
    y_ref, ren_ref, imn_ref = refs[S5_N_IN + n_cast:S5_N_IN + n_cast + 3]
    cast_out = refs[S5_N_IN + n_cast + 3:S5_N_IN + 2 * n_cast + 3]
    ut_ref, hbuf_ref, hst_ref, g_ref = refs[S5_N_IN + 2 * n_cast + 3:]
    for w_ref, w16_ref in zip(cast_in, cast_out):
        w16_ref[...] = w_ref[...].astype(bf16)
    step = pl.program_id(0)
    rows = nb * tl
    ns = S5_SLAB_STATE

    @pl.when(step == 0)
    def _():
        for s in range(S5_SLABS):
            hst_ref[s, :, :ns] = re0_ref[:, s * ns:(s + 1) * ns]
            hst_ref[s, :, ns:] = im0_ref[:, s * ns:(s + 1) * ns]

    for s in range(S5_SLABS):
        sl = slice(s * LANES, (s + 1) * LANES)
        if batch_major:
            for b in range(nb):
                ut_ref[s, pl.ds(b, tl, stride=nb), :] = u_ref[b, :, sl]
        else:
            ut_ref[s] = u_ref[:, :, sl].reshape(rows, LANES)

    if nb == SUBLANES:
        _s5_slabs_interleaved(ut_ref, hbuf_ref, hst_ref, g_ref, ar_ref, ai_ref, bblk_ref, cblk_ref, dsk_ref, tl)
    else:
        for s in range(S5_SLABS):
            sl = slice(s * LANES, (s + 1) * LANES)
            hbuf_ref[s, nb:nb + rows, :] = jnp.dot(ut_ref[s].astype(bf16), bblk_ref[s],
                                                   preferred_element_type=f32)
            _s5_scan_vmem(hbuf_ref, hst_ref, ar_ref, ai_ref, s, nb, rows)
            ys = jnp.dot(hbuf_ref[s, nb:nb + rows, :].astype(bf16), cblk_ref[s], preferred_element_type=f32)
            g_ref[s] = _gelu(ys + dsk_ref[:, sl] * ut_ref[s])

    g = jnp.concatenate([g_ref[s] for s in range(S5_SLABS)], axis=-1)
    gate = jnp.dot(g.astype(bf16), wglu_ref[...], preferred_element_type=f32) + bglu_ref[...]
    out = g * jax.nn.sigmoid(gate)
    if batch_major:
        for s in range(S5_SLABS):
            sl = slice(s * LANES, (s + 1) * LANES)
            ut_ref[s] = out[:, sl]
            for b in range(nb):
                y_ref[b, :, sl] = ut_ref[s, pl.ds(b, tl, stride=nb), :]
    else:
        y_ref[...] = out.reshape(tl, nb, D_S5)

    @pl.when(step == pl.num_programs(0) - 1)
    def _():
        for s in range(S5_SLABS):
            ren_ref[:, s * ns:(s + 1) * ns] = hst_ref[s, :, :ns]
            imn_ref[:, s * ns:(s + 1) * ns] = hst_ref[s, :, ns:]


def _s5(u, re0, im0, ar, ai, bblk, cblk, dsk, wglu, bglu, *, tl, batch_major, cast=()):
    if batch_major:
        nb, L, _ = u.shape
        u_spec = pl.BlockSpec((nb, tl, D_S5), lambda i: (0, i, 0))
    else:
        L, nb, _ = u.shape
        u_spec = pl.BlockSpec((tl, nb, D_S5), lambda i: (i, 0, 0))
    rows = nb * tl
    steps = L // tl
    nstate = S5_GROUPS * S5_STATE
    st_spec = _const_spec((nb, nstate))
    cast_specs = [pl.BlockSpec((w.shape[0] // steps, w.shape[1]), lambda i: (i, 0)) for w in cast]
    assert all(w.shape[0] % (steps * 2 * SUBLANES) == 0 for w in cast)
    return pl.pallas_call(
        functools.partial(_s5_kernel, nb=nb, tl=tl, batch_major=batch_major, n_cast=len(cast)),
        grid=(steps,),
        in_specs=[u_spec, st_spec, st_spec,
                  _const_spec((S5_SLABS, SUBLANES, S5_SLAB_STATE)),
                  _const_spec((S5_SLABS, SUBLANES, S5_SLAB_STATE)),
                  _const_spec((S5_SLABS, LANES, 2 * S5_SLAB_STATE)),
                  _const_spec((S5_SLABS, 2 * S5_SLAB_STATE, LANES)),
                  _const_spec((1, D_S5)), _const_spec((D_S5, D_S5)), _const_spec((1, D_S5))] + cast_specs,
        out_specs=[u_spec, st_spec, st_spec] + cast_specs,
        out_shape=[jax.ShapeDtypeStruct(u.shape, f32),
                   jax.ShapeDtypeStruct((nb, nstate), f32),
                   jax.ShapeDtypeStruct((nb, nstate), f32)]
                  + [jax.ShapeDtypeStruct(w.shape, bf16) for w in cast],
        scratch_shapes=[pltpu.VMEM((S5_SLABS, rows, LANES), f32),
                        pltpu.VMEM((S5_SLABS, nb + rows, 2 * S5_SLAB_STATE), f32),
                        pltpu.VMEM((S5_SLABS, nb, 2 * S5_SLAB_STATE), f32),
                        pltpu.VMEM((S5_SLABS, rows, LANES), f32)],
        compiler_params=_params("arbitrary"),
        name="s5_mixer",
    )(u, re0, im0, ar, ai, bblk, cblk, dsk, wglu, bglu, *cast)


def _ffn_stages(x, ys, y5, mods, n2g, nfg, wo_ref, wg_ref, wu_ref, wd_ref, write_out, ff_chunk):
    g1, sh2, sc2, g2, shf, scf = mods
    att = jnp.dot(ys.astype(bf16), wo_ref[:D_SSD, :], preferred_element_type=f32)
    att = att + jnp.dot(y5.astype(bf16), wo_ref[D_SSD:, :], preferred_element_type=f32)
    yield
    x1 = x + _by_row(lambda v, s: v * s, att, g1)
    v = _rms_mod(x1, n2g, sc2, sh2).astype(bf16)
    ff = None
    for o in range(0, D_FF, ff_chunk):
        gate = jnp.dot(v, wg_ref[:, o:o + ff_chunk], preferred_element_type=f32)
        up = jnp.dot(v, wu_ref[:, o:o + ff_chunk], preferred_element_type=f32)
        hid = (_silu(gate) * up).astype(bf16)
        part = jnp.dot(hid, wd_ref[o:o + ff_chunk, :], preferred_element_type=f32)
        ff = part if ff is None else ff + part
        yield
    x2 = x1 + _by_row(lambda v, s: v * s, ff, g2)
    write_out(_rms_mod(x2, nfg, scf, shf))


def _ffn_kernel(x_ref, ys_ref, y5_ref, mod_ref, modf_ref, n2g_ref, nfg_ref, wo_ref, wg_ref, wu_ref, wd_ref,
                o_ref, *, seq_major, ff_chunk, sub):
    mods = (_mod_row(mod_ref, 2, seq_major), _mod_row(mod_ref, 3, seq_major), _mod_row(mod_ref, 4, seq_major),
            _mod_row(mod_ref, 5, seq_major), _mod_row(modf_ref, 0, seq_major), _mod_row(modf_ref, 1, seq_major))
    x = _rows_in(x_ref, seq_major)
    rows = x.shape[0]
    tiles = {}

    def stages(r0):
        def write_out(y):
            if seq_major:
                tiles[r0] = y
            else:
                o_ref[0, r0:r0 + sub, :] = y
        return _ffn_stages(x[r0:r0 + sub], ys_ref[0, r0:r0 + sub, :], y5_ref[0, r0:r0 + sub, :], mods,
                           n2g_ref[...], nfg_ref[...], wo_ref, wg_ref, wu_ref, wd_ref, write_out, ff_chunk)

    _interleave(*[stages(r0) for r0 in range(0, rows, sub)])
    if seq_major:
        s, t, d = o_ref.shape
        y = jnp.concatenate([tiles[r0] for r0 in range(0, rows, sub)], axis=0)
        o_ref[...] = jnp.swapaxes(y.reshape(t, s, d), 0, 1)


def _ffn(x, ys, y5, mod, modf, n2g, nfg, wo, wg, wu, wd, *, tm, sub, seq_major, n_seq, first_seq, ff_chunk=256):
    if seq_major:
        x_spec = _const_spec(x.shape)
        nb, rows, d = 1, x.shape[0] * x.shape[1], x.shape[2]
        assert tm == rows
    else:
        nb, rows, d = x.shape
        x_spec = pl.BlockSpec((1, tm, d), lambda i, j: (i, j, 0))
    nt = rows // tm
    mod_spec = _mod_spec(mod, n_seq, first_seq)
    modf_spec = _mod_spec(modf, n_seq, first_seq)
    blk = lambda wd_: pl.BlockSpec((1, tm, wd_), lambda i, j: (i, j, 0))
    single = dict(pipeline_mode=pl.Buffered(1))
    wspec = lambda shape: pl.BlockSpec(shape, lambda i, j: (0, 0), **single)
    return pl.pallas_call(
        functools.partial(_ffn_kernel, seq_major=seq_major, ff_chunk=ff_chunk, sub=sub),
        grid=(nb, nt),
        in_specs=[x_spec, blk(D_SSD), blk(D_S5), mod_spec, modf_spec,
                  _const_spec((1, d)), _const_spec((1, d)),
                  wspec((d, d)), wspec((d, D_FF)), wspec((d, D_FF)), wspec((D_FF, d))],
        out_specs=x_spec,
        out_shape=jax.ShapeDtypeStruct(x.shape, f32),
        compiler_params=_params("parallel", "parallel"),
        name="out_ffn",
    )(x, ys, y5, mod, modf, n2g, nfg, wo, wg, wu, wd)


def kernel(x_prompt, x_sample, c_prompt, c_sample, state_ssd, state_conv, state_s5_re, state_s5_im, w_ada, b_ada, norm1_g, w_in, conv_w, conv_b, ssd_dt_bias, ssd_A_log, ssd_D, ssd_norm_g, s5_A_re, s5_A_im, s5_log_step, s5_B_re, s5_B_im, s5_C_re, s5_C_im, s5_D, w_glu, b_glu, w_out, norm2_g, w_ffn_gate, w_ffn_up, w_ffn_down, w_ada_f, b_ada_f, normf_g):
    assert w_ada.shape[0] == 1, "single-layer stack"
    bp, seq, d = x_prompt.shape
    bs, dseq, _ = x_sample.shape

    mod, modf = _ada_mod(c_sample, c_prompt, (w_ada[0], w_ada_f), (b_ada[0], b_ada_f))
    seqs_s = dict(n_seq=bs, first_seq=0)
    seqs_p = dict(n_seq=bp, first_seq=bs)

    w_in_p = w_in[0].T
    pad_h = lambda v: jnp.concatenate([v, jnp.zeros((DT_PAD - HEADS,), f32)]).reshape(1, DT_PAD)
    dtb = pad_h(ssd_dt_bias[0])
    alog = pad_h(ssd_A_log[0])
    dexp = jnp.repeat(ssd_D[0], HEAD_DIM).reshape(1, D_SSD)
    ng = ssd_norm_g[0].reshape(1, D_SSD)
    cw, cb = conv_w[0], conv_b[0].reshape(1, CONV_DIM)

    ar, ai, bblk, cblk = _s5_params(s5_A_re[0], s5_A_im[0], s5_log_step[0],
                                    s5_B_re[0].transpose(0, 2, 1), s5_B_im[0].transpose(0, 2, 1),
                                    s5_C_re[0], s5_C_im[0])
    dsk = s5_D[0].reshape(1, D_S5)
    wglu = w_glu[0].astype(bf16)
    bglu = b_glu[0].reshape(1, D_S5)
    n1g, n2g, nfg = norm1_g[0].reshape(1, d), norm2_g[0].reshape(1, d), normf_g.reshape(1, d)
    nstate = S5_GROUPS * S5_STATE

    z, act, u5, dt, conv_p = _inproj(x_prompt, mod, n1g, w_in_p, (cw, cb), tm=1024, sub=512, seq_major=False,
                                     **seqs_p)
    y_ssd, ssd_p = _ssd_prompt(act, z, dt, dtb, alog, dexp, ng)
    zeros_st = jnp.zeros((bp, nstate), f32)
    y_s5, re_p, im_p, wo, wg, wu, wd = _s5(
        u5, zeros_st, zeros_st, ar, ai, bblk, cblk, dsk, wglu, bglu, tl=128, batch_major=True,
        cast=(w_out[0], w_ffn_gate[0], w_ffn_up[0], w_ffn_down[0]))
    y_prompt = _ffn(x_prompt, y_ssd, y_s5, mod, modf, n2g, nfg, wo, wg, wu, wd, tm=1024, sub=512,
                    seq_major=False, **seqs_p)

    rows_s = dseq * bs
    steps = lambda a: a.reshape(dseq, bs, a.shape[-1])
    flat = lambda a: a.reshape(1, rows_s, a.shape[-1])
    z, xbc, u5, dt = _inproj(x_sample, mod, n1g, w_in_p, tm=rows_s, sub=rows_s // 2, seq_major=True, **seqs_s)
    y_ssd, ssd_s, conv_s = _ssd_sample(steps(xbc), steps(z), steps(dt), state_conv[0].transpose(1, 0, 2),
                                       state_ssd[0], cw, cb, dtb, alog, dexp, ng)
    y_s5, re_s, im_s = _s5(steps(u5), state_s5_re[0].reshape(bs, nstate), state_s5_im[0].reshape(bs, nstate),
                           ar, ai, bblk, cblk, dsk, wglu, bglu, tl=dseq, batch_major=False)
    y_sample = _ffn(x_sample, flat(y_ssd), flat(y_s5), mod, modf, n2g, nfg, wo, wg, wu, wd,
                    tm=rows_s, sub=rows_s, seq_major=True, **seqs_s)

    g5 = (S5_GROUPS, S5_STATE)
    return (y_prompt, y_sample,
            ssd_p[None], ssd_s.reshape((1,) + state_ssd.shape[1:]),
            conv_p[None], conv_s.transpose(1, 0, 2)[None],
            re_p.reshape((1, bp) + g5), re_s.reshape((1, bs) + g5),
            im_p.reshape((1, bp) + g5), im_s.reshape((1, bs) + g5))
```

```python
import functools

import jax
import jax.numpy as jnp
from jax import lax
from jax.experimental import pallas as pl
from jax.experimental.pallas import tpu as pltpu

f32 = jnp.float32
bf16 = jnp.bfloat16

D_MODEL = 1024
D_SSD = 512
HEAD_DIM = 64
HEADS = 8
GROUPS = 2
HEADS_PER_GROUP = HEADS // GROUPS
STATE = 128
CONV_W = 4
CONV_DIM = D_SSD + 2 * GROUPS * STATE
D_S5 = 512
S5_CH = 16
S5_GROUPS = 32
S5_STATE = 64
D_FF = 2816
N_ADA = 6
EPS = 1e-6

LANES = 128
SUBLANES = 8
SSD_CHUNK = 128
S5_SLABS = D_S5 // LANES
S5_SLAB_STATE = (S5_GROUPS // S5_SLABS) * S5_STATE
DT_PAD = LANES
VMEM_LIMIT = 56 * 1024 * 1024

NT_DIMS = (((1,), (1,)), ((), ()))
TN_DIMS = (((0,), (0,)), ((), ()))


def _silu(x):
    h = 0.5 * x
    return h + h * jnp.tanh(h)


def _interleave(*gens, lead=0):
    pending, live, rnd = list(gens), [], 0
    while pending or live:
        while pending and rnd >= lead * (len(gens) - len(pending)):
            live.append(pending.pop(0))
        for g in list(live):
            try:
                next(g)
            except StopIteration:
                live.remove(g)
        rnd += 1


def _by_row(fn, v, *ms):
    r, d = v.shape
    m_rows = ms[0].shape[0]
    if m_rows in (1, r):
        return fn(v, *ms)
    out = fn(v.reshape(r // m_rows, m_rows, d), *[m[None] for m in ms])
    return out.reshape(r, d)


def _rms_mod(x, g, sc, sh):
    y = x * lax.rsqrt(jnp.mean(x * x, axis=-1, keepdims=True) + EPS)
    return _by_row(lambda v, s, t: v * (1.0 + s) + t, y * g, sc, sh)


def _mod_row(mod_ref, i, per_row):
    return mod_ref[i] if per_row else mod_ref[i, pl.ds(pl.program_id(0), 1), :]


def _mod_spec(mod, n_seq, first_seq):
    assert first_seq % n_seq == 0
    return pl.BlockSpec((mod.shape[0], n_seq, mod.shape[2]), lambda *_: (0, first_seq // n_seq, 0))


def _const_spec(shape):
    nd = len(shape)
    return pl.BlockSpec(shape, lambda *_: (0,) * nd)


def _params(*sem):
    return pltpu.CompilerParams(dimension_semantics=sem, vmem_limit_bytes=VMEM_LIMIT)


ADA_K_BLOCK = 256


def _ada_kernel(ca_ref, cb_ref, *refs):
    nw = len(refs) // 3
    s = _silu(jnp.concatenate([ca_ref[...], cb_ref[...]], axis=0)).astype(bf16)
    for w_ref, b_ref, o_ref in zip(refs[:nw], refs[nw:2 * nw], refs[2 * nw:]):
        n, _, d = o_ref.shape

        @pl.when(pl.program_id(0) == 0)
        def _():
            for j in range(n):
                o_ref[j] = jnp.broadcast_to(b_ref[:, j * d:(j + 1) * d], o_ref.shape[1:])

        for j in range(n):
            o_ref[j] += jnp.dot(s, w_ref[:, j * d:(j + 1) * d].astype(bf16), preferred_element_type=f32)


def _ada_mod(ca, cb, ws, bs):
    k = ca.shape[1]
    m = ca.shape[0] + cb.shape[0]
    ns = [w.shape[1] // k for w in ws]
    return pl.pallas_call(
        _ada_kernel,
        grid=(k // ADA_K_BLOCK,),
        in_specs=[pl.BlockSpec((c.shape[0], ADA_K_BLOCK), lambda i: (0, i)) for c in (ca, cb)]
                 + [pl.BlockSpec((ADA_K_BLOCK, n * k), lambda i: (i, 0)) for n in ns]
                 + [_const_spec((1, n * k)) for n in ns],
        out_specs=[_const_spec((n, m, k)) for n in ns],
        out_shape=[jax.ShapeDtypeStruct((n, m, k), f32) for n in ns],
        compiler_params=_params("arbitrary"),
        name="ada_mod",
    )(ca, cb, *ws, *[b.reshape(1, n * k) for b, n in zip(bs, ns)])


def _rows_in(x_ref, seq_major):
    if not seq_major:
        return x_ref[0]
    s, t, d = x_ref.shape
    return jnp.swapaxes(x_ref[...], 0, 1).reshape(s * t, d)


def _conv_silu(ext, cw_ref, cb_ref):
    w = [cw_ref[k:k + 1, :] for k in range(CONV_W)]
    back1 = pltpu.roll(ext, 1, 0)
    older = pltpu.roll(ext * w[1] + back1 * w[0], 2, 0)
    conv = cb_ref[...] + ext[SUBLANES:, :] * w[3] + back1[SUBLANES:, :] * w[2] + older[SUBLANES:, :]
    return _silu(conv)


def _inproj_stages(x, sh, sc, g, outs, after=()):
    u = _rms_mod(x, g, sc, sh).astype(bf16)
    yield
    for write, w16 in outs:
        write(lax.dot_general(u, w16, NT_DIMS, preferred_element_type=f32))
        yield
    for stage in after:
        stage()
        yield


def _inproj_kernel(x_ref, mod_ref, g_ref, w_ref, *rest, seq_major, sub, conv):
    if conv:
        cw_ref, cb_ref, z_ref, act_ref, u5_ref, dt_ref, cn_ref, ext_ref = rest

        @pl.when(pl.program_id(1) == 0)
        def _():
            ext_ref[0:SUBLANES, :] = jnp.zeros((SUBLANES, CONV_DIM), f32)
    else:
        z_ref, xbc_ref, u5_ref, dt_ref = rest
    sh = _mod_row(mod_ref, 0, seq_major)
    sc = _mod_row(mod_ref, 1, seq_major)
    x = _rows_in(x_ref, seq_major)
    rows = x.shape[0]
    T = SSD_CHUNK
    o_dt = D_SSD + CONV_DIM
    w16 = lambda o, width: w_ref[o:o + width, :].astype(bf16)
    w_z, w_xbc, w_u5, w_dt = w16(0, D_SSD), w16(D_SSD, CONV_DIM), w16(o_dt + HEADS, D_S5), w16(o_dt, DT_PAD)

    def stages(r0):
        def to(ref):
            def write(v):
                ref[0, r0:r0 + sub, :] = v
            return write

        def to_ext(v):
            ext_ref[SUBLANES + r0:SUBLANES + r0 + sub, :] = v

        def conv_chunk(c0):
            def stage():
                act_ref[0, c0:c0 + T, :] = _conv_silu(ext_ref[c0:c0 + SUBLANES + T, :], cw_ref, cb_ref)
            return stage

        outs = [(to(z_ref), w_z), (to_ext if conv else to(xbc_ref), w_xbc), (to(u5_ref), w_u5), (to(dt_ref), w_dt)]
        after = [conv_chunk(c0) for c0 in range(r0, r0 + sub, T)] if conv else ()
        return _inproj_stages(x[r0:r0 + sub], sh, sc, g_ref[...], outs, after)

    n_matmul_stages = 5
    _interleave(*[stages(r0) for r0 in range(0, rows, sub)], lead=n_matmul_stages if conv else 0)
    if conv:
        cn_ref[0] = ext_ref[SUBLANES + rows - (CONV_W - 1):SUBLANES + rows, :]
        ext_ref[0:SUBLANES, :] = ext_ref[rows:rows + SUBLANES, :]


def _inproj(x, mod, g, w, conv_wb=None, *, tm, sub, seq_major, n_seq, first_seq):
    if seq_major:
        x_spec = _const_spec(x.shape)
        nb, rows, d = 1, x.shape[0] * x.shape[1], x.shape[2]
        assert tm == rows
    else:
        nb, rows, d = x.shape
        x_spec = pl.BlockSpec((1, tm, d), lambda i, j: (i, j, 0))
    nt = rows // tm
    widths = (D_SSD, CONV_DIM, D_S5, DT_PAD)
    conv = conv_wb is not None
    in_specs = [x_spec, _mod_spec(mod, n_seq, first_seq), _const_spec((1, d)),
                pl.BlockSpec(w.shape, lambda i, j: (0, 0), pipeline_mode=pl.Buffered(1))]
    out_specs = [pl.BlockSpec((1, tm, wd), lambda i, j: (i, j, 0)) for wd in widths]
    out_shape = [jax.ShapeDtypeStruct((nb, rows, wd), f32) for wd in widths]
    scratch = []
    if conv:
        in_specs += [_const_spec((CONV_W, CONV_DIM)), _const_spec((1, CONV_DIM))]
        out_specs.append(pl.BlockSpec((1, CONV_W - 1, CONV_DIM), lambda i, j: (i, 0, 0)))
        out_shape.append(jax.ShapeDtypeStruct((nb, CONV_W - 1, CONV_DIM), f32))
        scratch.append(pltpu.VMEM((SUBLANES + tm, CONV_DIM), f32))
    return pl.pallas_call(
        functools.partial(_inproj_kernel, seq_major=seq_major, sub=sub, conv=conv),
        grid=(nb, nt),
        in_specs=in_specs, out_specs=out_specs, out_shape=out_shape, scratch_shapes=scratch,
        compiler_params=_params("parallel", "arbitrary" if conv else "parallel"),
        name="in_proj",
    )(x, mod, g, w, *(conv_wb or ()))


def _split3(x):
    hi = x.astype(bf16)
    r1 = x - hi.astype(f32)
    mid = r1.astype(bf16)
    lo = (r1 - mid.astype(f32)).astype(bf16)
    return hi, mid, lo


def _dot_sel_lhs(sel16, x):
    return sum(jnp.dot(sel16, p, preferred_element_type=f32) for p in _split3(x))


def _dot_sel_rhs(x, sel16):
    hi, mid, _ = _split3(x)
    return jnp.dot(hi, sel16, preferred_element_type=f32) + jnp.dot(mid, sel16, preferred_element_type=f32)


def _gated_group_norm(y, z, ng):
    y = y * _silu(z)
    gw = D_SSD // GROUPS
    parts = []
    for g in range(GROUPS):
        yg = y[:, g * gw:(g + 1) * gw]
        parts.append(yg * lax.rsqrt(jnp.mean(yg * yg, axis=-1, keepdims=True) + EPS))
    return jnp.concatenate(parts, axis=-1) * ng


def _ssd_prompt_kernel(act_ref, z_ref, dt_ref, dtb_ref, alog_ref, dexp_ref, ng_ref, y_ref, st_ref, h_ref, *, cps):
    T = SSD_CHUNK
    gw = HEADS_PER_GROUP * HEAD_DIM

    @pl.when(pl.program_id(1) == 0)
    def _():
        h_ref[...] = jnp.zeros_like(h_ref)

    row = lax.broadcasted_iota(jnp.int32, (T, T), 0)
    col = lax.broadcasted_iota(jnp.int32, (T, T), 1)
    tri = row >= col
    tri16 = jnp.where(tri, 1.0, 0.0).astype(bf16)
    low_half = lax.broadcasted_iota(jnp.int32, (T, LANES), 1) < HEAD_DIM
    a_neg = -jnp.exp(alog_ref[...])

    n_fac = 3
    src = lax.broadcasted_iota(jnp.int32, (2 * DT_PAD, n_fac * D_SSD), 0) % DT_PAD
    dst = lax.broadcasted_iota(jnp.int32, (2 * DT_PAD, n_fac * D_SSD), 1)
    sel16 = jnp.where(src == dst // D_SSD * HEADS + dst % D_SSD // HEAD_DIM, 1.0, 0.0).astype(bf16)
    lane = lax.broadcasted_iota(jnp.int32, (T, DT_PAD), 1)

    def spread(q):
        hi = q.astype(bf16)
        mid = (q - hi.astype(f32)).astype(bf16)
        return jnp.dot(jnp.concatenate([hi, mid], axis=1), sel16, preferred_element_type=f32)

    def state_free(ci):
        r0 = ci * T
        act = act_ref[0, r0:r0 + T, :]
        xs = act[:, :D_SSD]
        bm = act[:, D_SSD:D_SSD + GROUPS * STATE]
        cm = act[:, D_SSD + GROUPS * STATE:]
        dtv = jax.nn.softplus(dt_ref[0, r0:r0 + T, :] + dtb_ref[...])
        a_cs = _dot_sel_lhs(tri16, dtv * a_neg)
        bg16s = [bm[:, g * STATE:(g + 1) * STATE].astype(bf16) for g in range(GROUPS)]
        cg16s = [cm[:, g * STATE:(g + 1) * STATE].astype(bf16) for g in range(GROUPS)]
        gmats = [lax.dot_general(cg16s[g], bg16s[g], NT_DIMS, preferred_element_type=f32)
                 for g in range(GROUPS)]
        yield
        a_last = a_cs[T - 1:T, :]
        a_cs_t = a_cs.T
        a_one = pltpu.roll(a_cs, HEADS, 1)
        a_two = pltpu.roll(a_cs, 2 * HEADS, 1)
        decay = jnp.exp(jnp.where(lane < 2 * HEADS, a_one[T - 1:T, :] - a_one, a_two))
        rep = spread(jnp.where(lane < HEADS, dtv, jnp.where(lane < n_fac * HEADS, decay, 0.0)))
        yield
        x = xs * rep[:, :D_SSD]
        x16 = x.astype(bf16)
        xd16 = (x * rep[:, D_SSD:2 * D_SSD]).astype(bf16)
        e_cs = rep[:, 2 * D_SSD:]
        yield
        y_part = []
        for g in range(GROUPS):
            for pr in range(HEADS_PER_GROUP // 2):
                s_pair = []
                for q in range(2):
                    h = g * HEADS_PER_GROUP + 2 * pr + q
                    seg = a_cs[:, h:h + 1] - a_cs_t[h:h + 1, :]
                    lmat = jnp.exp(jnp.where(tri, seg, -jnp.inf))
                    s_pair.append((gmats[g] * lmat).astype(bf16))
                xp = x16[:, g * gw + pr * LANES:g * gw + (pr + 1) * LANES]
                zero = jnp.zeros_like(xp)
                x_pair = jnp.concatenate([jnp.where(low_half, xp, zero), jnp.where(low_half, zero, xp)], axis=0)
                y_part.append(jnp.dot(jnp.concatenate(s_pair, axis=1), x_pair, preferred_element_type=f32))
                yield
        y_free = jnp.concatenate(y_part, axis=-1) + xs * dexp_ref[...]
        return y_free, e_cs, jnp.exp(a_last), xd16, bg16s, cg16s

    def state_step(ci, y_free, e_cs, e_last, xd16, bg16s, cg16s):
        r0 = ci * T
        y_offs = []
        for g in range(GROUPS):
            hp = h_ref[g]
            y_offs.append(lax.dot_general(cg16s[g], hp.astype(bf16), NT_DIMS, preferred_element_type=f32))
            upd = lax.dot_general(xd16[:, g * gw:(g + 1) * gw], bg16s[g], TN_DIMS, preferred_element_type=f32)
            for hh in range(HEADS_PER_GROUP):
                h = g * HEADS_PER_GROUP + hh
                rows = slice(hh * HEAD_DIM, (hh + 1) * HEAD_DIM)
                h_ref[g, rows, :] = e_last[:, h:h + 1] * hp[rows] + upd[rows]
        y = y_free + jnp.concatenate(y_offs, axis=-1) * e_cs
        y_ref[0, r0:r0 + T, :] = _gated_group_norm(y, z_ref[0, r0:r0 + T, :], ng_ref[...])

    gens = [state_free(ci) for ci in range(cps)]
    free = [None] * cps
    while any(f is None for f in free):
        for ci in range(cps):
            if free[ci] is None:
                try:
                    next(gens[ci])
                except StopIteration as done:
                    free[ci] = done.value
    for ci in range(cps):
        state_step(ci, *free[ci])

    @pl.when(pl.program_id(1) == pl.num_programs(1) - 1)
    def _():
        st_ref[0] = h_ref[...].reshape(HEADS, HEAD_DIM, STATE)


def _ssd_prompt(act, z, dt, dtb, alog, dexp, ng, *, cps=8):
    nb, L, _ = act.shape
    rows = cps * SSD_CHUNK
    blk = lambda wd: pl.BlockSpec((1, rows, wd), lambda b, c: (b, c, 0))
    return pl.pallas_call(
        functools.partial(_ssd_prompt_kernel, cps=cps),
        grid=(nb, L // rows),
        in_specs=[blk(CONV_DIM), blk(D_SSD), blk(DT_PAD), _const_spec((1, DT_PAD)), _const_spec((1, DT_PAD)),
                  _const_spec((1, D_SSD)), _const_spec((1, D_SSD))],
        out_specs=[blk(D_SSD), pl.BlockSpec((1, HEADS, HEAD_DIM, STATE), lambda b, c: (b, 0, 0, 0))],
        out_shape=[jax.ShapeDtypeStruct((nb, L, D_SSD), f32),
                   jax.ShapeDtypeStruct((nb, HEADS, HEAD_DIM, STATE), f32)],
        scratch_shapes=[pltpu.VMEM((GROUPS, HEADS_PER_GROUP * HEAD_DIM, STATE), f32)],
        compiler_params=_params("parallel", "arbitrary"),
        name="ssd_prompt",
    )(act, z, dt, dtb, alog, dexp, ng)


def _ssd_sample_kernel(xbc_ref, z_ref, dt_ref, cbuf_ref, st_ref, cw_ref, cb_ref, dtb_ref, alog_ref,
                       dexp_ref, ng_ref, y_ref, stn_ref, cn_ref, dtot_ref, *, L, nb):
    gw = HEADS_PER_GROUP * HEAD_DIM
    full = [cbuf_ref[j] for j in range(CONV_W - 1)] + [xbc_ref[t] for t in range(L)]
    for j in range(CONV_W - 1):
        cn_ref[j] = full[L + j]
    a_neg = -jnp.exp(alog_ref[...])
    xs, bm, cm, dtv, acs = [], [], [], [], []
    run = None
    for t in range(L):
        conv = cb_ref[...]
        for k in range(CONV_W):
            conv = conv + full[t + k] * cw_ref[k:k + 1, :]
        act = _silu(conv)
        xs.append(act[:, :D_SSD])
        bm.append(act[:, D_SSD:D_SSD + GROUPS * STATE])
        cm.append(act[:, D_SSD + GROUPS * STATE:])
        d = jax.nn.softplus(dt_ref[t] + dtb_ref[...])
        dtv.append(d)
        run = d * a_neg if run is None else run + d * a_neg
        acs.append(run)
    a_tot = acs[L - 1]
    dtot_ref[...] = jnp.exp(a_tot)

    lane_head = lax.broadcasted_iota(jnp.int32, (DT_PAD, D_SSD), 1) // HEAD_DIM
    sel16 = jnp.where(lax.broadcasted_iota(jnp.int32, (DT_PAD, D_SSD), 0) == lane_head, 1.0, 0.0).astype(bf16)
    pairs = [(t, s) for t in range(L) for s in range(t)]
    factors = (dtv + [jnp.exp(a_tot - acs[t]) for t in range(L)] + [jnp.exp(acs[t]) for t in range(L)]
               + [jnp.exp(acs[t] - acs[s]) for t, s in pairs])
    rep = _dot_sel_rhs(jnp.concatenate(factors, axis=0), sel16)
    piece = lambda i: rep[i * nb:(i + 1) * nb]
    x = [xs[t] * piece(t) for t in range(L)]
    xd_stack = jnp.concatenate([x[t] * piece(L + t) for t in range(L)], axis=0)
    e_cs = [piece(2 * L + t) for t in range(L)]
    decay = {ts: piece(3 * L + i) for i, ts in enumerate(pairs)}

    in_group0 = lax.broadcasted_iota(jnp.int32, (nb, D_SSD), 1) < gw
    y_intra = []
    for t in range(L):
        acc = None
        for s in range(t + 1):
            cb_dot = [jnp.sum(cm[t][:, g * STATE:(g + 1) * STATE] * bm[s][:, g * STATE:(g + 1) * STATE],
                              axis=-1, keepdims=True) for g in range(GROUPS)]
            w = jnp.where(in_group0, cb_dot[0], cb_dot[1])
            term = w * x[s] if s == t else w * decay[(t, s)] * x[s]
            acc = term if acc is None else acc + term
        y_intra.append(acc)

    c_stack = [jnp.concatenate([cm[t][:, g * STATE:(g + 1) * STATE] for t in range(L)], axis=0).astype(bf16)
               for g in range(GROUPS)]
    b_stack = [jnp.concatenate([bm[t][:, g * STATE:(g + 1) * STATE] for t in range(L)], axis=0).astype(bf16)
               for g in range(GROUPS)]
    seq_of_row = lax.broadcasted_iota(jnp.int32, (L * nb, gw), 0) & (nb - 1)

    def per_seq(b, yoff):
        mine = seq_of_row == b
        drow = dtot_ref[pl.ds(b, 1), :]
        out = []
        for g in range(GROUPS):
            h0 = st_ref[b, g]
            r = lax.dot_general(c_stack[g], h0.astype(bf16), NT_DIMS, preferred_element_type=f32)
            out.append(yoff[g] + jnp.where(mine, r, 0.0))
            xm = jnp.where(mine, xd_stack[:, g * gw:(g + 1) * gw], 0.0).astype(bf16)
            upd = lax.dot_general(xm, b_stack[g], TN_DIMS, preferred_element_type=f32)
            for hh in range(HEADS_PER_GROUP):
                h = g * HEADS_PER_GROUP + hh
                rows = slice(hh * HEAD_DIM, (hh + 1) * HEAD_DIM)
                stn_ref[b, g, rows, :] = drow[:, h:h + 1] * h0[rows] + upd[rows]
        return tuple(out)

    yoff = lax.fori_loop(0, nb, per_seq, tuple(jnp.zeros((L * nb, gw), f32) for _ in range(GROUPS)),
                         unroll=16)

    for t in range(L):
        y_off = jnp.concatenate([yoff[g][t * nb:(t + 1) * nb] for g in range(GROUPS)], axis=-1)
        y = y_intra[t] + y_off * e_cs[t] + xs[t] * dexp_ref[...]
        y_ref[t] = _gated_group_norm(y, z_ref[t], ng_ref[...])


def _ssd_sample(xbc, z, dt, cbuf, st, cw, cb, dtb, alog, dexp, ng, *, nb=16):
    L, B, _ = xbc.shape
    assert nb & (nb - 1) == 0 and B % nb == 0
    tblk = lambda n, wd: pl.BlockSpec((n, nb, wd), lambda i: (0, i, 0))
    gw = HEADS_PER_GROUP * HEAD_DIM
    st = st.reshape(B, GROUPS, gw, STATE)
    st_spec = pl.BlockSpec((nb, GROUPS, gw, STATE), lambda i: (i, 0, 0, 0))
    return pl.pallas_call(
        functools.partial(_ssd_sample_kernel, L=L, nb=nb),
        grid=(B // nb,),
        in_specs=[tblk(L, CONV_DIM), tblk(L, D_SSD), tblk(L, DT_PAD), tblk(CONV_W - 1, CONV_DIM), st_spec,
                  _const_spec((CONV_W, CONV_DIM)), _const_spec((1, CONV_DIM)),
                  _const_spec((1, DT_PAD)), _const_spec((1, DT_PAD)),
                  _const_spec((1, D_SSD)), _const_spec((1, D_SSD))],
        out_specs=[tblk(L, D_SSD), st_spec, tblk(CONV_W - 1, CONV_DIM)],
        out_shape=[jax.ShapeDtypeStruct((L, B, D_SSD), f32),
                   jax.ShapeDtypeStruct((B, GROUPS, gw, STATE), f32),
                   jax.ShapeDtypeStruct((CONV_W - 1, B, CONV_DIM), f32)],
        scratch_shapes=[pltpu.VMEM((nb, DT_PAD), f32)],
        compiler_params=_params("parallel"),
        name="ssd_sample",
    )(xbc, z, dt, cbuf, st, cw, cb, dtb, alog, dexp, ng)


def _s5_param_kernel(lr_ref, li_ref, ls_ref, br_ref, bi_ref, cr_ref, ci_ref,
                     ar_ref, ai_ref, bblk_ref, cblk_ref, b_scr, c_scr):
    lr, li = lr_ref[...], li_ref[...]
    step = jnp.exp(ls_ref[...])
    mag = jnp.exp(lr * step)
    abr = mag * jnp.cos(li * step)
    abi = mag * jnp.sin(li * step)
    nr, ni = abr - 1.0, abi
    den = lr * lr + li * li
    fr = (nr * lr + ni * li) / den
    fi = (ni * lr - nr * li) / den
    br, bi = br_ref[...], bi_ref[...]
    bbr = fr * br - fi * bi
    bbi = fr * bi + fi * br
    b_scr[...] = jnp.zeros_like(b_scr)
    c_scr[...] = jnp.zeros_like(c_scr)
    gps = S5_GROUPS // S5_SLABS
    ns = S5_SLAB_STATE
    for g in range(S5_GROUPS):
        s, gl = divmod(g, gps)
        ch = slice(gl * S5_CH, (gl + 1) * S5_CH)
        st = slice(gl * S5_STATE, (gl + 1) * S5_STATE)
        st_im = slice(ns + gl * S5_STATE, ns + (gl + 1) * S5_STATE)
        ar_ref[s, :, st] = jnp.broadcast_to(abr[g], (SUBLANES, S5_STATE))
        ai_ref[s, :, st] = jnp.broadcast_to(abi[g], (SUBLANES, S5_STATE))
        b_scr[s, ch, st] = bbr[g]
        b_scr[s, ch, st_im] = bbi[g]
        c_scr[s, st, ch] = cr_ref[g].T
        c_scr[s, st_im, ch] = -ci_ref[g].T
    bblk_ref[...] = b_scr[...].astype(bf16)
    cblk_ref[...] = c_scr[...].astype(bf16)


def _s5_params(a_re, a_im, log_step, b_re_t, b_im_t, c_re, c_im):
    g, p = a_re.shape
    ns = S5_SLAB_STATE
    return pl.pallas_call(
        _s5_param_kernel,
        out_shape=[jax.ShapeDtypeStruct((S5_SLABS, SUBLANES, ns), f32),
                   jax.ShapeDtypeStruct((S5_SLABS, SUBLANES, ns), f32),
                   jax.ShapeDtypeStruct((S5_SLABS, LANES, 2 * ns), bf16),
                   jax.ShapeDtypeStruct((S5_SLABS, 2 * ns, LANES), bf16)],
        scratch_shapes=[pltpu.VMEM((S5_SLABS, LANES, 2 * ns), f32),
                        pltpu.VMEM((S5_SLABS, 2 * ns, LANES), f32)],
        name="s5_params",
    )(a_re.reshape(g, 1, p), a_im.reshape(g, 1, p), log_step.reshape(g, 1, 1), b_re_t, b_im_t, c_re, c_im)


def _gelu(x):
    return 0.5 * x * (1.0 + lax.erf(x * (2.0 ** -0.5)))


def _s5_slabs_interleaved(ut_ref, hbuf_ref, hst_ref, g_ref, ar_ref, ai_ref, bblk_ref, cblk_ref, dsk_ref, tl,
                          row_block=256, n_stages=8):
    ns = S5_SLAB_STATE
    nb = SUBLANES
    rows = nb * tl

    def lanes(s):
        return slice(s * LANES, (s + 1) * LANES)

    def bu_stage(slabs):
        for s in slabs:
            for r in range(0, rows, row_block):
                us = ut_ref[s, r:r + row_block, :]
                hbuf_ref[s, nb + r:nb + r + row_block, :] = jnp.dot(us.astype(bf16), bblk_ref[s],
                                                                    preferred_element_type=f32)
                yield

    def scan_stage(slabs):
        carry = [(hst_ref[s, :, :ns], hst_ref[s, :, ns:]) for s in slabs]
        coef = [(ar_ref[s], ai_ref[s]) for s in slabs]
        for t in range(tl):
            r = nb + nb * t
            for i, s in enumerate(slabs):
                (pr, pi), (ar, ai) = carry[i], coef[i]
                nr = ar * pr - ai * pi + hbuf_ref[s, r:r + nb, :ns]
                ni = ar * pi + ai * pr + hbuf_ref[s, r:r + nb, ns:]
                hbuf_ref[s, r:r + nb, :ns] = nr
                hbuf_ref[s, r:r + nb, ns:] = ni
                carry[i] = (nr, ni)
            if (t + 1) % (tl // n_stages) == 0:
                yield
        for i, s in enumerate(slabs):
            hst_ref[s, :, :ns] = carry[i][0]
            hst_ref[s, :, ns:] = carry[i][1]

    def y_stage(slabs):
        for s in slabs:
            for r in range(0, rows, row_block):
                h16 = hbuf_ref[s, nb + r:nb + r + row_block, :].astype(bf16)
                ys = jnp.dot(h16, cblk_ref[s], preferred_element_type=f32)
                ys = ys + dsk_ref[:, lanes(s)] * ut_ref[s, r:r + row_block, :]
                g_ref[s, r:r + row_block, :] = _gelu(ys)
                yield

    half = S5_SLABS // 2
    first, second = tuple(range(half)), tuple(range(half, S5_SLABS))
    _interleave(bu_stage(first))
    _interleave(scan_stage(first), bu_stage(second))
    _interleave(scan_stage(second), y_stage(first))
    _interleave(y_stage(second))


def _s5_scan_vmem(hbuf_ref, hst_ref, ar_ref, ai_ref, s, nb, rows):
    ns = S5_SLAB_STATE
    hbuf_ref[s, 0:nb, :] = hst_ref[s]
    ar, ai = ar_ref[s], ai_ref[s]

    def body(i, carry):
        j = pl.multiple_of(nb + i * SUBLANES, SUBLANES)
        prev = hbuf_ref[s, pl.ds(j - nb, SUBLANES), :]
        cur = hbuf_ref[s, pl.ds(j, SUBLANES), :]
        pr, pi = prev[:, :ns], prev[:, ns:]
        hbuf_ref[s, pl.ds(j, SUBLANES), :ns] = ar * pr - ai * pi + cur[:, :ns]
        hbuf_ref[s, pl.ds(j, SUBLANES), ns:] = ar * pi + ai * pr + cur[:, ns:]
        return carry

    lax.fori_loop(0, rows // SUBLANES, body, 0, unroll=min(8, nb // SUBLANES))
    hst_ref[s] = hbuf_ref[s, rows:rows + nb, :]


S5_N_IN = 10


def _s5_kernel(*refs, nb, tl, batch_major, n_cast):
    (u_ref, re0_ref, im0_ref, ar_ref, ai_ref, bblk_ref, cblk_ref, dsk_ref, wglu_ref,
     bglu_ref) = refs[:S5_N_IN]
    cast_in = refs[S5_N_IN:S5_N_IN + n_cast]
    y_ref, ren_ref, imn_ref = refs[S5_N_IN + n_cast:S5_N_IN + n_cast + 3]
    cast_out = refs[S5_N_IN + n_cast + 3:S5_N_IN + 2 * n_cast + 3]
    ut_ref, hbuf_ref, hst_ref, g_ref = refs[S5_N_IN + 2 * n_cast + 3:]
    for w_ref, w16_ref in zip(cast_in, cast_out):
        w16_ref[...] = w_ref[...].astype(bf16)
    step = pl.program_id(0)
    rows = nb * tl
    ns = S5_SLAB_STATE

    @pl.when(step == 0)
    def _():
        for s in range(S5_SLABS):
            hst_ref[s, :, :ns] = re0_ref[:, s * ns:(s + 1) * ns]
            hst_ref[s, :, ns:] = im0_ref[:, s * ns:(s + 1) * ns]

    for s in range(S5_SLABS):
        sl = slice(s * LANES, (s + 1) * LANES)
        if batch_major:
            for b in range(nb):
                ut_ref[s, pl.ds(b, tl, stride=nb), :] = u_ref[b, :, sl]
        else:
            ut_ref[s] = u_ref[:, :, sl].reshape(rows, LANES)

    if nb == SUBLANES:
        _s5_slabs_interleaved(ut_ref, hbuf_ref, hst_ref, g_ref, ar_ref, ai_ref, bblk_ref, cblk_ref, dsk_ref, tl)
    else:
        for s in range(S5_SLABS):
            sl = slice(s * LANES, (s + 1) * LANES)
            hbuf_ref[s, nb:nb + rows, :] = jnp.dot(ut_ref[s].astype(bf16), bblk_ref[s],
                                                   preferred_element_type=f32)
            _s5_scan_vmem(hbuf_ref, hst_ref, ar_ref, ai_ref, s, nb, rows)
            ys = jnp.dot(hbuf_ref[s, nb:nb + rows, :].astype(bf16), cblk_ref[s], preferred_element_type=f32)
            g_ref[s] = _gelu(ys + dsk_ref[:, sl] * ut_ref[s])

    g = jnp.concatenate([g_ref[s] for s in range(S5_SLABS)], axis=-1)
    gate = jnp.dot(g.astype(bf16), wglu_ref[...], preferred_element_type=f32) + bglu_ref[...]
    out = g * jax.nn.sigmoid(gate)
    if batch_major:
        for s in range(S5_SLABS):
            sl = slice(s * LANES, (s + 1) * LANES)
            ut_ref[s] = out[:, sl]
            for b in range(nb):
                y_ref[b, :, sl] = ut_ref[s, pl.ds(b, tl, stride=nb), :]
    else:
        y_ref[...] = out.reshape(tl, nb, D_S5)

    @pl.when(step == pl.num_programs(0) - 1)
    def _():
        for s in range(S5_SLABS):
            ren_ref[:, s * ns:(s + 1) * ns] = hst_ref[s, :, :ns]
            imn_ref[:, s * ns:(s + 1) * ns] = hst_ref[s, :, ns:]


def _s5(u, re0, im0, ar, ai, bblk, cblk, dsk, wglu, bglu, *, tl, batch_major, cast=()):
    if batch_major:
        nb, L, _ = u.shape
        u_spec = pl.BlockSpec((nb, tl, D_S5), lambda i: (0, i, 0))
    else:
        L, nb, _ = u.shape
        u_spec = pl.BlockSpec((tl, nb, D_S5), lambda i: (i, 0, 0))
    rows = nb * tl
    steps = L // tl
    nstate = S5_GROUPS * S5_STATE
    st_spec = _const_spec((nb, nstate))
    cast_specs = [pl.BlockSpec((w.shape[0] // steps, w.shape[1]), lambda i: (i, 0)) for w in cast]
    assert all(w.shape[0] % (steps * 2 * SUBLANES) == 0 for w in cast)
    return pl.pallas_call(
        functools.partial(_s5_kernel, nb=nb, tl=tl, batch_major=batch_major, n_cast=len(cast)),
        grid=(steps,),
        in_specs=[u_spec, st_spec, st_spec,
                  _const_spec((S5_SLABS, SUBLANES, S5_SLAB_STATE)),
                  _const_spec((S5_SLABS, SUBLANES, S5_SLAB_STATE)),
                  _const_spec((S5_SLABS, LANES, 2 * S5_SLAB_STATE)),
                  _const_spec((S5_SLABS, 2 * S5_SLAB_STATE, LANES)),
                  _const_spec((1, D_S5)), _const_spec((D_S5, D_S5)), _const_spec((1, D_S5))] + cast_specs,
        out_specs=[u_spec, st_spec, st_spec] + cast_specs,
        out_shape=[jax.ShapeDtypeStruct(u.shape, f32),
                   jax.ShapeDtypeStruct((nb, nstate), f32),
                   jax.ShapeDtypeStruct((nb, nstate), f32)]
                  + [jax.ShapeDtypeStruct(w.shape, bf16) for w in cast],
        scratch_shapes=[pltpu.VMEM((S5_SLABS, rows, LANES), f32),
                        pltpu.VMEM((S5_SLABS, nb + rows, 2 * S5_SLAB_STATE), f32),
                        pltpu.VMEM((S5_SLABS, nb, 2 * S5_SLAB_STATE), f32),
                        pltpu.VMEM((S5_SLABS, rows, LANES), f32)],
        compiler_params=_params("arbitrary"),
        name="s5_mixer",
    )(u, re0, im0, ar, ai, bblk, cblk, dsk, wglu, bglu, *cast)


def _ffn_stages(x, ys, y5, mods, n2g, nfg, wo_ref, wg_ref, wu_ref, wd_ref, write_out, ff_chunk):
    g1, sh2, sc2, g2, shf, scf = mods
    att = jnp.dot(ys.astype(bf16), wo_ref[:D_SSD, :], preferred_element_type=f32)
    att = att + jnp.dot(y5.astype(bf16), wo_ref[D_SSD:, :], preferred_element_type=f32)
    yield
    x1 = x + _by_row(lambda v, s: v * s, att, g1)
    v = _rms_mod(x1, n2g, sc2, sh2).astype(bf16)
    ff = None
    for o in range(0, D_FF, ff_chunk):
        gate = jnp.dot(v, wg_ref[:, o:o + ff_chunk], preferred_element_type=f32)
        up = jnp.dot(v, wu_ref[:, o:o + ff_chunk], preferred_element_type=f32)
        hid = (_silu(gate) * up).astype(bf16)
        part = jnp.dot(hid, wd_ref[o:o + ff_chunk, :], preferred_element_type=f32)
        ff = part if ff is None else ff + part
        yield
    x2 = x1 + _by_row(lambda v, s: v * s, ff, g2)
    write_out(_rms_mod(x2, nfg, scf, shf))


def _ffn_kernel(x_ref, ys_ref, y5_ref, mod_ref, modf_ref, n2g_ref, nfg_ref, wo_ref, wg_ref, wu_ref, wd_ref,
                o_ref, *, seq_major, ff_chunk, sub):
    mods = (_mod_row(mod_ref, 2, seq_major), _mod_row(mod_ref, 3, seq_major), _mod_row(mod_ref, 4, seq_major),
            _mod_row(mod_ref, 5, seq_major), _mod_row(modf_ref, 0, seq_major), _mod_row(modf_ref, 1, seq_major))
    x = _rows_in(x_ref, seq_major)
    rows = x.shape[0]
    tiles = {}

    def stages(r0):
        def write_out(y):
            if seq_major:
                tiles[r0] = y
            else:
                o_ref[0, r0:r0 + sub, :] = y
        return _ffn_stages(x[r0:r0 + sub], ys_ref[0, r0:r0 + sub, :], y5_ref[0, r0:r0 + sub, :], mods,
                           n2g_ref[...], nfg_ref[...], wo_ref, wg_ref, wu_ref, wd_ref, write_out, ff_chunk)

    _interleave(*[stages(r0) for r0 in range(0, rows, sub)])
    if seq_major:
        s, t, d = o_ref.shape
        y = jnp.concatenate([tiles[r0] for r0 in range(0, rows, sub)], axis=0)
        o_ref[...] = jnp.swapaxes(y.reshape(t, s, d), 0, 1)


def _ffn(x, ys, y5, mod, modf, n2g, nfg, wo, wg, wu, wd, *, tm, sub, seq_major, n_seq, first_seq, ff_chunk=256):
    if seq_major:
        x_spec = _const_spec(x.shape)
        nb, rows, d = 1, x.shape[0] * x.shape[1], x.shape[2]
        assert tm == rows
    else:
        nb, rows, d = x.shape
        x_spec = pl.BlockSpec((1, tm, d), lambda i, j: (i, j, 0))
    nt = rows // tm
    mod_spec = _mod_spec(mod, n_seq, first_seq)
    modf_spec = _mod_spec(modf, n_seq, first_seq)
    blk = lambda wd_: pl.BlockSpec((1, tm, wd_), lambda i, j: (i, j, 0))
    single = dict(pipeline_mode=pl.Buffered(1))
    wspec = lambda shape: pl.BlockSpec(shape, lambda i, j: (0, 0), **single)
    return pl.pallas_call(
        functools.partial(_ffn_kernel, seq_major=seq_major, ff_chunk=ff_chunk, sub=sub),
        grid=(nb, nt),
        in_specs=[x_spec, blk(D_SSD), blk(D_S5), mod_spec, modf_spec,
                  _const_spec((1, d)), _const_spec((1, d)),
                  wspec((d, d)), wspec((d, D_FF)), wspec((d, D_FF)), wspec((D_FF, d))],
        out_specs=x_spec,
        out_shape=jax.ShapeDtypeStruct(x.shape, f32),
        compiler_params=_params("parallel", "parallel"),
        name="out_ffn",
    )(x, ys, y5, mod, modf, n2g, nfg, wo, wg, wu, wd)


def kernel(x_prompt, x_sample, c_prompt, c_sample, state_ssd, state_conv, state_s5_re, state_s5_im, w_ada, b_ada, norm1_g, w_in, conv_w, conv_b, ssd_dt_bias, ssd_A_log, ssd_D, ssd_norm_g, s5_A_re, s5_A_im, s5_log_step, s5_B_re, s5_B_im, s5_C_re, s5_C_im, s5_D, w_glu, b_glu, w_out, norm2_g, w_ffn_gate, w_ffn_up, w_ffn_down, w_ada_f, b_ada_f, normf_g):
    assert w_ada.shape[0] == 1, "single-layer stack"
    bp, seq, d = x_prompt.shape
    bs, dseq, _ = x_sample.shape

    mod, modf = _ada_mod(c_sample, c_prompt, (w_ada[0], w_ada_f), (b_ada[0], b_ada_f))
    seqs_s = dict(n_seq=bs, first_seq=0)
    seqs_p = dict(n_seq=bp, first_seq=bs)

    w_in_p = w_in[0].T
    pad_h = lambda v: jnp.concatenate([v, jnp.zeros((DT_PAD - HEADS,), f32)]).reshape(1, DT_PAD)
    dtb = pad_h(ssd_dt_bias[0])
    alog = pad_h(ssd_A_log[0])
    dexp = jnp.repeat(ssd_D[0], HEAD_DIM).reshape(1, D_SSD)
    ng = ssd_norm_g[0].reshape(1, D_SSD)
    cw, cb = conv_w[0], conv_b[0].reshape(1, CONV_DIM)

    ar, ai, bblk, cblk = _s5_params(s5_A_re[0], s5_A_im[0], s5_log_step[0],
                                    s5_B_re[0].transpose(0, 2, 1), s5_B_im[0].transpose(0, 2, 1),
                                    s5_C_re[0], s5_C_im[0])
    dsk = s5_D[0].reshape(1, D_S5)
    wglu = w_glu[0].astype(bf16)
    bglu = b_glu[0].reshape(1, D_S5)
    n1g, n2g, nfg = norm1_g[0].reshape(1, d), norm2_g[0].reshape(1, d), normf_g.reshape(1, d)
    nstate = S5_GROUPS * S5_STATE

    z, act, u5, dt, conv_p = _inproj(x_prompt, mod, n1g, w_in_p, (cw, cb), tm=1024, sub=512, seq_major=False,
                                     **seqs_p)
    y_ssd, ssd_p = _ssd_prompt(act, z, dt, dtb, alog, dexp, ng)
    zeros_st = jnp.zeros((bp, nstate), f32)
    y_s5, re_p, im_p, wo, wg, wu, wd = _s5(
        u5, zeros_st, zeros_st, ar, ai, bblk, cblk, dsk, wglu, bglu, tl=128, batch_major=True,
        cast=(w_out[0], w_ffn_gate[0], w_ffn_up[0], w_ffn_down[0]))
    y_prompt = _ffn(x_prompt, y_ssd, y_s5, mod, modf, n2g, nfg, wo, wg, wu, wd, tm=1024, sub=512,
                    seq_major=False, **seqs_p)

    rows_s = dseq * bs
    steps = lambda a: a.reshape(dseq, bs, a.shape[-1])
    flat = lambda a: a.reshape(1, rows_s, a.shape[-1])
    z, xbc, u5, dt = _inproj(x_sample, mod, n1g, w_in_p, tm=rows_s, sub=rows_s // 2, seq_major=True, **seqs_s)
    y_ssd, ssd_s, conv_s = _ssd_sample(steps(xbc), steps(z), steps(dt), state_conv[0].transpose(1, 0, 2),
                                       state_ssd[0], cw, cb, dtb, alog, dexp, ng)
    y_s5, re_s, im_s = _s5(steps(u5), state_s5_re[0].reshape(bs, nstate), state_s5_im[0].reshape(bs, nstate),
                           ar, ai, bblk, cblk, dsk, wglu, bglu, tl=dseq, batch_major=False)
    y_sample = _ffn(x_sample, flat(y_ssd), flat(y_s5), mod, modf, n2g, nfg, wo, wg, wu, wd,
                    tm=rows_s, sub=rows_s, seq_major=True, **seqs_s)

    g5 = (S5_GROUPS, S5_STATE)
    return (y_prompt, y_sample,
            ssd_p[None], ssd_s.reshape((1,) + state_ssd.shape[1:]),
            conv_p[None], conv_s.transpose(1, 0, 2)[None],
            re_p.reshape((1, bp) + g5), re_s.reshape((1, bs) + g5),
            im_p.reshape((1, bp) + g5), im_s.reshape((1, bs) + g5))
```

```python
import functools

import jax
import jax.numpy as jnp
from jax import lax
from jax.experimental import pallas as pl
from jax.experimental.pallas import tpu as pltpu

f32 = jnp.float32
bf16 = jnp.bfloat16

D_MODEL = 1024
D_SSD = 512
HEAD_DIM = 64
HEADS = 8
GROUPS = 2
HEADS_PER_GROUP = HEADS // GROUPS
STATE = 128
CONV_W = 4
CONV_DIM = D_SSD + 2 * GROUPS * STATE
D_S5 = 512
S5_CH = 16
S5_GROUPS = 32
S5_STATE = 64
D_FF = 2816
N_ADA = 6
EPS = 1e-6

LANES = 128
SUBLANES = 8
SSD_CHUNK = 128
S5_SLABS = D_S5 // LANES
S5_SLAB_STATE = (S5_GROUPS // S5_SLABS) * S5_STATE
DT_PAD = LANES
VMEM_LIMIT = 56 * 1024 * 1024

NT_DIMS = (((1,), (1,)), ((), ()))
TN_DIMS = (((0,), (0,)), ((), ()))


def _silu(x):
    h = 0.5 * x
    return h + h * jnp.tanh(h)


def _interleave(*gens, lead=0):
    pending, live, rnd = list(gens), [], 0
    while pending or live:
        while pending and rnd >= lead * (len(gens) - len(pending)):
            live.append(pending.pop(0))
        for g in list(live):
            try:
                next(g)
            except StopIteration:
                live.remove(g)
        rnd += 1


def _by_row(fn, v, *ms):
    r, d = v.shape
    m_rows = ms[0].shape[0]
    if m_rows in (1, r):
        return fn(v, *ms)
    out = fn(v.reshape(r // m_rows, m_rows, d), *[m[None] for m in ms])
    return out.reshape(r, d)


def _rms_mod(x, g, sc, sh):
    y = x * lax.rsqrt(jnp.mean(x * x, axis=-1, keepdims=True) + EPS)
    return _by_row(lambda v, s, t: v * (1.0 + s) + t, y * g, sc, sh)


def _mod_row(mod_ref, i, per_row):
    return mod_ref[i] if per_row else mod_ref[i, pl.ds(pl.program_id(0), 1), :]


def _mod_spec(mod, n_seq, first_seq):
    assert first_seq % n_seq == 0
    return pl.BlockSpec((mod.shape[0], n_seq, mod.shape[2]), lambda *_: (0, first_seq // n_seq, 0))


def _const_spec(shape):
    nd = len(shape)
    return pl.BlockSpec(shape, lambda *_: (0,) * nd)


def _params(*sem):
    return pltpu.CompilerParams(dimension_semantics=sem, vmem_limit_bytes=VMEM_LIMIT)


ADA_K_BLOCK = 256


def _ada_kernel(ca_ref, cb_ref, *refs):
    nw = len(refs) // 3
    s = _silu(jnp.concatenate([ca_ref[...], cb_ref[...]], axis=0)).astype(bf16)
    for w_ref, b_ref, o_ref in zip(refs[:nw], refs[nw:2 * nw], refs[2 * nw:]):
        n, _, d = o_ref.shape

        @pl.when(pl.program_id(0) == 0)
        def _():
            for j in range(n):
                o_ref[j] = jnp.broadcast_to(b_ref[:, j * d:(j + 1) * d], o_ref.shape[1:])

        for j in range(n):
            o_ref[j] += jnp.dot(s, w_ref[:, j * d:(j + 1) * d].astype(bf16), preferred_element_type=f32)


def _ada_mod(ca, cb, ws, bs):
    k = ca.shape[1]
    m = ca.shape[0] + cb.shape[0]
    ns = [w.shape[1] // k for w in ws]
    return pl.pallas_call(
        _ada_kernel,
        grid=(k // ADA_K_BLOCK,),
        in_specs=[pl.BlockSpec((c.shape[0], ADA_K_BLOCK), lambda i: (0, i)) for c in (ca, cb)]
                 + [pl.BlockSpec((ADA_K_BLOCK, n * k), lambda i: (i, 0)) for n in ns]
                 + [_const_spec((1, n * k)) for n in ns],
        out_specs=[_const_spec((n, m, k)) for n in ns],
        out_shape=[jax.ShapeDtypeStruct((n, m, k), f32) for n in ns],
        compiler_params=_params("arbitrary"),
        name="ada_mod",
    )(ca, cb, *ws, *[b.reshape(1, n * k) for b, n in zip(bs, ns)])


def _rows_in(x_ref, seq_major):
    if not seq_major:
        return x_ref[0]
    s, t, d = x_ref.shape
    return jnp.swapaxes(x_ref[...], 0, 1).reshape(s * t, d)


def _conv_silu(ext, cw_ref, cb_ref):
    w = [cw_ref[k:k + 1, :] for k in range(CONV_W)]
    back1 = pltpu.roll(ext, 1, 0)
    older = pltpu.roll(ext * w[1] + back1 * w[0], 2, 0)
    conv = cb_ref[...] + ext[SUBLANES:, :] * w[3] + back1[SUBLANES:, :] * w[2] + older[SUBLANES:, :]
    return _silu(conv)


def _inproj_stages(x, sh, sc, g, outs, after=()):
    u = _rms_mod(x, g, sc, sh).astype(bf16)
    yield
    for write, w16 in outs:
        write(lax.dot_general(u, w16, NT_DIMS, preferred_element_type=f32))
        yield
    for stage in after:
        stage()
        yield


def _inproj_kernel(x_ref, mod_ref, g_ref, w_ref, *rest, seq_major, sub, conv):
    if conv:
        cw_ref, cb_ref, z_ref, act_ref, u5_ref, dt_ref, cn_ref, ext_ref = rest

        @pl.when(pl.program_id(1) == 0)
        def _():
            ext_ref[0:SUBLANES, :] = jnp.zeros((SUBLANES, CONV_DIM), f32)
    else:
        z_ref, xbc_ref, u5_ref, dt_ref = rest
    sh = _mod_row(mod_ref, 0, seq_major)
    sc = _mod_row(mod_ref, 1, seq_major)
    x = _rows_in(x_ref, seq_major)
    rows = x.shape[0]
    T = SSD_CHUNK
    o_dt = D_SSD + CONV_DIM
    w16 = lambda o, width: w_ref[o:o + width, :].astype(bf16)
    w_z, w_xbc, w_u5, w_dt = w16(0, D_SSD), w16(D_SSD, CONV_DIM), w16(o_dt + HEADS, D_S5), w16(o_dt, DT_PAD)

    def stages(r0):
        def to(ref):
            def write(v):
                ref[0, r0:r0 + sub, :] = v
            return write

        def to_ext(v):
            ext_ref[SUBLANES + r0:SUBLANES + r0 + sub, :] = v

        def conv_chunk(c0):
            def stage():
                act_ref[0, c0:c0 + T, :] = _conv_silu(ext_ref[c0:c0 + SUBLANES + T, :], cw_ref, cb_ref)
            return stage

        outs = [(to(z_ref), w_z), (to_ext if conv else to(xbc_ref), w_xbc), (to(u5_ref), w_u5), (to(dt_ref), w_dt)]
        after = [conv_chunk(c0) for c0 in range(r0, r0 + sub, T)] if conv else ()
        return _inproj_stages(x[r0:r0 + sub], sh, sc, g_ref[...], outs, after)

    n_matmul_stages = 5
    _interleave(*[stages(r0) for r0 in range(0, rows, sub)], lead=n_matmul_stages if conv else 0)
    if conv:
        cn_ref[0] = ext_ref[SUBLANES + rows - (CONV_W - 1):SUBLANES + rows, :]
        ext_ref[0:SUBLANES, :] = ext_ref[rows:rows + SUBLANES, :]


def _inproj(x, mod, g, w, conv_wb=None, *, tm, sub, seq_major, n_seq, first_seq):
    if seq_major:
        x_spec = _const_spec(x.shape)
        nb, rows, d = 1, x.shape[0] * x.shape[1], x.shape[2]
        assert tm == rows
    else:
        nb, rows, d = x.shape
        x_spec = pl.BlockSpec((1, tm, d), lambda i, j: (i, j, 0))
    nt = rows // tm
    widths = (D_SSD, CONV_DIM, D_S5, DT_PAD)
    conv = conv_wb is not None
    in_specs = [x_spec, _mod_spec(mod, n_seq, first_seq), _const_spec((1, d)),
                pl.BlockSpec(w.shape, lambda i, j: (0, 0), pipeline_mode=pl.Buffered(1))]
    out_specs = [pl.BlockSpec((1, tm, wd), lambda i, j: (i, j, 0)) for wd in widths]
    out_shape = [jax.ShapeDtypeStruct((nb, rows, wd), f32) for wd in widths]
    scratch = []
    if conv:
        in_specs += [_const_spec((CONV_W, CONV_DIM)), _const_spec((1, CONV_DIM))]
        out_specs.append(pl.BlockSpec((1, CONV_W - 1, CONV_DIM), lambda i, j: (i, 0, 0)))
        out_shape.append(jax.ShapeDtypeStruct((nb, CONV_W - 1, CONV_DIM), f32))
        scratch.append(pltpu.VMEM((SUBLANES + tm, CONV_DIM), f32))
    return pl.pallas_call(
        functools.partial(_inproj_kernel, seq_major=seq_major, sub=sub, conv=conv),
        grid=(nb, nt),
        in_specs=in_specs, out_specs=out_specs, out_shape=out_shape, scratch_shapes=scratch,
        compiler_params=_params("parallel", "arbitrary" if conv else "parallel"),
        name="in_proj",
    )(x, mod, g, w, *(conv_wb or ()))


def _split3(x):
    hi = x.astype(bf16)
    r1 = x - hi.astype(f32)
    mid = r1.astype(bf16)
    lo = (r1 - mid.astype(f32)).astype(bf16)
    return hi, mid, lo


def _dot_sel_lhs(sel16, x):
    return sum(jnp.dot(sel16, p, preferred_element_type=f32) for p in _split3(x))


def _dot_sel_rhs(x, sel16):
    hi, mid, _ = _split3(x)
    return jnp.dot(hi, sel16, preferred_element_type=f32) + jnp.dot(mid, sel16, preferred_element_type=f32)


def _gated_group_norm(y, z, ng):
    y = y * _silu(z)
    gw = D_SSD // GROUPS
    parts = []
    for g in range(GROUPS):
        yg = y[:, g * gw:(g + 1) * gw]
        parts.append(yg * lax.rsqrt(jnp.mean(yg * yg, axis=-1, keepdims=True) + EPS))
    return jnp.concatenate(parts, axis=-1) * ng


def _ssd_prompt_kernel(act_ref, z_ref, dt_ref, dtb_ref, alog_ref, dexp_ref, ng_ref, y_ref, st_ref, h_ref, *, cps):
    T = SSD_CHUNK
    gw = HEADS_PER_GROUP * HEAD_DIM

    @pl.when(pl.program_id(1) == 0)
    def _():
        h_ref[...] = jnp.zeros_like(h_ref)

    row = lax.broadcasted_iota(jnp.int32, (T, T), 0)
    col = lax.broadcasted_iota(jnp.int32, (T, T), 1)
    tri = row >= col
    tri16 = jnp.where(tri, 1.0, 0.0).astype(bf16)
    low_half = lax.broadcasted_iota(jnp.int32, (T, LANES), 1) < HEAD_DIM
    a_neg = -jnp.exp(alog_ref[...])

    n_fac = 3
    src = lax.broadcasted_iota(jnp.int32, (2 * DT_PAD, n_fac * D_SSD), 0) % DT_PAD
    dst = lax.broadcasted_iota(jnp.int32, (2 * DT_PAD, n_fac * D_SSD), 1)
    sel16 = jnp.where(src == dst // D_SSD * HEADS + dst % D_SSD // HEAD_DIM, 1.0, 0.0).astype(bf16)
    lane = lax.broadcasted_iota(jnp.int32, (T, DT_PAD), 1)

    def spread(q):
        hi = q.astype(bf16)
        mid = (q - hi.astype(f32)).astype(bf16)
        return jnp.dot(jnp.concatenate([hi, mid], axis=1), sel16, preferred_element_type=f32)

    def state_free(ci):
        r0 = ci * T
        act = act_ref[0, r0:r0 + T, :]
        xs = act[:, :D_SSD]
        bm = act[:, D_SSD:D_SSD + GROUPS * STATE]
        cm = act[:, D_SSD + GROUPS * STATE:]
        dtv = jax.nn.softplus(dt_ref[0, r0:r0 + T, :] + dtb_ref[...])
        a_cs = _dot_sel_lhs(tri16, dtv * a_neg)
        bg16s = [bm[:, g * STATE:(g + 1) * STATE].astype(bf16) for g in range(GROUPS)]
        cg16s = [cm[:, g * STATE:(g + 1) * STATE].astype(bf16) for g in range(GROUPS)]
        gmats = [lax.dot_general(cg16s[g], bg16s[g], NT_DIMS, preferred_element_type=f32)
                 for g in range(GROUPS)]
        yield
        a_last = a_cs[T - 1:T, :]
        a_cs_t = a_cs.T
        a_one = pltpu.roll(a_cs, HEADS, 1)
        a_two = pltpu.roll(a_cs, 2 * HEADS, 1)
        decay = jnp.exp(jnp.where(lane < 2 * HEADS, a_one[T - 1:T, :] - a_one, a_two))
        rep = spread(jnp.where(lane < HEADS, dtv, jnp.where(lane < n_fac * HEADS, decay, 0.0)))
        yield
        x = xs * rep[:, :D_SSD]
        x16 = x.astype(bf16)
        xd16 = (x * rep[:, D_SSD:2 * D_SSD]).astype(bf16)
        e_cs = rep[:, 2 * D_SSD:]
        yield
        y_part = []
        for g in range(GROUPS):
            for pr in range(HEADS_PER_GROUP // 2):
                s_pair = []
                for q in range(2):
                    h = g * HEADS_PER_GROUP + 2 * pr + q
                    seg = a_cs[:, h:h + 1] - a_cs_t[h:h + 1, :]
                    lmat = jnp.exp(jnp.where(tri, seg, -jnp.inf))
                    s_pair.append((gmats[g] * lmat).astype(bf16))
                xp = x16[:, g * gw + pr * LANES:g * gw + (pr + 1) * LANES]
                zero = jnp.zeros_like(xp)
                x_pair = jnp.concatenate([jnp.where(low_half, xp, zero), jnp.where(low_half, zero, xp)], axis=0)
                y_part.append(jnp.dot(jnp.concatenate(s_pair, axis=1), x_pair, preferred_element_type=f32))
                yield
        y_free = jnp.concatenate(y_part, axis=-1) + xs * dexp_ref[...]
        return y_free, e_cs, jnp.exp(a_last), xd16, bg16s, cg16s

    def state_step(ci, y_free, e_cs, e_last, xd16, bg16s, cg16s):
        r0 = ci * T
        y_offs = []
        for g in range(GROUPS):
            hp = h_ref[g]
            y_offs.append(lax.dot_general(cg16s[g], hp.astype(bf16), NT_DIMS, preferred_element_type=f32))
            upd = lax.dot_general(xd16[:, g * gw:(g + 1) * gw], bg16s[g], TN_DIMS, preferred_element_type=f32)
            for hh in range(HEADS_PER_GROUP):
                h = g * HEADS_PER_GROUP + hh
                rows = slice(hh * HEAD_DIM, (hh + 1) * HEAD_DIM)
                h_ref[g, rows, :] = e_last[:, h:h + 1] * hp[rows] + upd[rows]
        y = y_free + jnp.concatenate(y_offs, axis=-1) * e_cs
        y_ref[0, r0:r0 + T, :] = _gated_group_norm(y, z_ref[0, r0:r0 + T, :], ng_ref[...])

    gens = [state_free(ci) for ci in range(cps)]
    free = [None] * cps
    while any(f is None for f in free):
        for ci in range(cps):
            if free[ci] is None:
                try:
                    next(gens[ci])
                except StopIteration as done:
                    free[ci] = done.value
    for ci in range(cps):
        state_step(ci, *free[ci])

    @pl.when(pl.program_id(1) == pl.num_programs(1) - 1)
    def _():
        st_ref[0] = h_ref[...].reshape(HEADS, HEAD_DIM, STATE)


def _ssd_prompt(act, z, dt, dtb, alog, dexp, ng, *, cps=8):
    nb, L, _ = act.shape
    rows = cps * SSD_CHUNK
    blk = lambda wd: pl.BlockSpec((1, rows, wd), lambda b, c: (b, c, 0))
    return pl.pallas_call(
        functools.partial(_ssd_prompt_kernel, cps=cps),
        grid=(nb, L // rows),
        in_specs=[blk(CONV_DIM), blk(D_SSD), blk(DT_PAD), _const_spec((1, DT_PAD)), _const_spec((1, DT_PAD)),
                  _const_spec((1, D_SSD)), _const_spec((1, D_SSD))],
        out_specs=[blk(D_SSD), pl.BlockSpec((1, HEADS, HEAD_DIM, STATE), lambda b, c: (b, 0, 0, 0))],
        out_shape=[jax.ShapeDtypeStruct((nb, L, D_SSD), f32),
                   jax.ShapeDtypeStruct((nb, HEADS, HEAD_DIM, STATE), f32)],
        scratch_shapes=[pltpu.VMEM((GROUPS, HEADS_PER_GROUP * HEAD_DIM, STATE), f32)],
        compiler_params=_params("parallel", "arbitrary"),
        name="ssd_prompt",
    )(act, z, dt, dtb, alog, dexp, ng)


def _ssd_sample_kernel(xbc_ref, z_ref, dt_ref, cbuf_ref, st_ref, cw_ref, cb_ref, dtb_ref, alog_ref,
                       dexp_ref, ng_ref, y_ref, stn_ref, cn_ref, dtot_ref, *, L, nb):
    gw = HEADS_PER_GROUP * HEAD_DIM
    full = [cbuf_ref[j] for j in range(CONV_W - 1)] + [xbc_ref[t] for t in range(L)]
    for j in range(CONV_W - 1):
        cn_ref[j] = full[L + j]
    a_neg = -jnp.exp(alog_ref[...])
    xs, bm, cm, dtv, acs = [], [], [], [], []
    run = None
    for t in range(L):
        conv = cb_ref[...]
        for k in range(CONV_W):
            conv = conv + full[t + k] * cw_ref[k:k + 1, :]
        act = _silu(conv)
        xs.append(act[:, :D_SSD])
        bm.append(act[:, D_SSD:D_SSD + GROUPS * STATE])
        cm.append(act[:, D_SSD + GROUPS * STATE:])
        d = jax.nn.softplus(dt_ref[t] + dtb_ref[...])
        dtv.append(d)
        run = d * a_neg if run is None else run + d * a_neg
        acs.append(run)
    a_tot = acs[L - 1]
    dtot_ref[...] = jnp.exp(a_tot)

    lane_head = lax.broadcasted_iota(jnp.int32, (DT_PAD, D_SSD), 1) // HEAD_DIM
    sel16 = jnp.where(lax.broadcasted_iota(jnp.int32, (DT_PAD, D_SSD), 0) == lane_head, 1.0, 0.0).astype(bf16)
    pairs = [(t, s) for t in range(L) for s in range(t)]
    factors = (dtv + [jnp.exp(a_tot - acs[t]) for t in range(L)] + [jnp.exp(acs[t]) for t in range(L)]
               + [jnp.exp(acs[t] - acs[s]) for t, s in pairs])
    rep = _dot_sel_rhs(jnp.concatenate(factors, axis=0), sel16)
    piece = lambda i: rep[i * nb:(i + 1) * nb]
    x = [xs[t] * piece(t) for t in range(L)]
    xd_stack = jnp.concatenate([x[t] * piece(L + t) for t in range(L)], axis=0)
    e_cs = [piece(2 * L + t) for t in range(L)]
    decay = {ts: piece(3 * L + i) for i, ts in enumerate(pairs)}

    in_group0 = lax.broadcasted_iota(jnp.int32, (nb, D_SSD), 1) < gw
    y_intra = []
    for t in range(L):
        acc = None
        for s in range(t + 1):
            cb_dot = [jnp.sum(cm[t][:, g * STATE:(g + 1) * STATE] * bm[s][:, g * STATE:(g + 1) * STATE],
                              axis=-1, keepdims=True) for g in range(GROUPS)]
            w = jnp.where(in_group0, cb_dot[0], cb_dot[1])
            term = w * x[s] if s == t else w * decay[(t, s)] * x[s]
            acc = term if acc is None else acc + term
        y_intra.append(acc)

    c_stack = [jnp.concatenate([cm[t][:, g * STATE:(g + 1) * STATE] for t in range(L)], axis=0).astype(bf16)
               for g in range(GROUPS)]
    b_stack = [jnp.concatenate([bm[t][:, g * STATE:(g + 1) * STATE] for t in range(L)], axis=0).astype(bf16)
               for g in range(GROUPS)]
    seq_of_row = lax.broadcasted_iota(jnp.int32, (L * nb, gw), 0) & (nb - 1)

    def per_seq(b, yoff):
        mine = seq_of_row == b
        drow = dtot_ref[pl.ds(b, 1), :]
        out = []
        for g in range(GROUPS):
            h0 = st_ref[b, g]
            r = lax.dot_general(c_stack[g], h0.astype(bf16), NT_DIMS, preferred_element_type=f32)
            out.append(yoff[g] + jnp.where(mine, r, 0.0))
            xm = jnp.where(mine, xd_stack[:, g * gw:(g + 1) * gw], 0.0).astype(bf16)
            upd = lax.dot_general(xm, b_stack[g], TN_DIMS, preferred_element_type=f32)
            for hh in range(HEADS_PER_GROUP):
                h = g * HEADS_PER_GROUP + hh
                rows = slice(hh * HEAD_DIM, (hh + 1) * HEAD_DIM)
                stn_ref[b, g, rows, :] = drow[:, h:h + 1] * h0[rows] + upd[rows]
        return tuple(out)

    yoff = lax.fori_loop(0, nb, per_seq, tuple(jnp.zeros((L * nb, gw), f32) for _ in range(GROUPS)),
                         unroll=True)

    for t in range(L):
        y_off = jnp.concatenate([yoff[g][t * nb:(t + 1) * nb] for g in range(GROUPS)], axis=-1)
        y = y_intra[t] + y_off * e_cs[t] + xs[t] * dexp_ref[...]
        y_ref[t] = _gated_group_norm(y, z_ref[t], ng_ref[...])


def _ssd_sample(xbc, z, dt, cbuf, st, cw, cb, dtb, alog, dexp, ng, *, nb=16):
    L, B, _ = xbc.shape
    assert nb & (nb - 1) == 0 and B % nb == 0
    tblk = lambda n, wd: pl.BlockSpec((n, nb, wd), lambda i: (0, i, 0))
    gw = HEADS_PER_GROUP * HEAD_DIM
    st = st.reshape(B, GROUPS, gw, STATE)
    st_spec = pl.BlockSpec((nb, GROUPS, gw, STATE), lambda i: (i, 0, 0, 0))
    return pl.pallas_call(
        functools.partial(_ssd_sample_kernel, L=L, nb=nb),
        grid=(B // nb,),
        in_specs=[tblk(L, CONV_DIM), tblk(L, D_SSD), tblk(L, DT_PAD), tblk(CONV_W - 1, CONV_DIM), st_spec,
                  _const_spec((CONV_W, CONV_DIM)), _const_spec((1, CONV_DIM)),
                  _const_spec((1, DT_PAD)), _const_spec((1, DT_PAD)),
                  _const_spec((1, D_SSD)), _const_spec((1, D_SSD))],
        out_specs=[tblk(L, D_SSD), st_spec, tblk(CONV_W - 1, CONV_DIM)],
        out_shape=[jax.ShapeDtypeStruct((L, B, D_SSD), f32),
                   jax.ShapeDtypeStruct((B, GROUPS, gw, STATE), f32),
                   jax.ShapeDtypeStruct((CONV_W - 1, B, CONV_DIM), f32)],
        scratch_shapes=[pltpu.VMEM((nb, DT_PAD), f32)],
        compiler_params=_params("parallel"),
        name="ssd_sample",
    )(xbc, z, dt, cbuf, st, cw, cb, dtb, alog, dexp, ng)


def _s5_param_kernel(lr_ref, li_ref, ls_ref, br_ref, bi_ref, cr_ref, ci_ref,
                     ar_ref, ai_ref, bblk_ref, cblk_ref, b_scr, c_scr):
    lr, li = lr_ref[...], li_ref[...]
    step = jnp.exp(ls_ref[...])
    mag = jnp.exp(lr * step)
    abr = mag * jnp.cos(li * step)
    abi = mag * jnp.sin(li * step)
    nr, ni = abr - 1.0, abi
    den = lr * lr + li * li
    fr = (nr * lr + ni * li) / den
    fi = (ni * lr - nr * li) / den
    br, bi = br_ref[...], bi_ref[...]
    bbr = fr * br - fi * bi
    bbi = fr * bi + fi * br
    b_scr[...] = jnp.zeros_like(b_scr)
    c_scr[...] = jnp.zeros_like(c_scr)
    gps = S5_GROUPS // S5_SLABS
    ns = S5_SLAB_STATE
    for g in range(S5_GROUPS):
        s, gl = divmod(g, gps)
        ch = slice(gl * S5_CH, (gl + 1) * S5_CH)
        st = slice(gl * S5_STATE, (gl + 1) * S5_STATE)
        st_im = slice(ns + gl * S5_STATE, ns + (gl + 1) * S5_STATE)
        ar_ref[s, :, st] = jnp.broadcast_to(abr[g], (SUBLANES, S5_STATE))
        ai_ref[s, :, st] = jnp.broadcast_to(abi[g], (SUBLANES, S5_STATE))
        b_scr[s, ch, st] = bbr[g]
        b_scr[s, ch, st_im] = bbi[g]
        c_scr[s, st, ch] = cr_ref[g].T
        c_scr[s, st_im, ch] = -ci_ref[g].T
    bblk_ref[...] = b_scr[...].astype(bf16)
    cblk_ref[...] = c_scr[...].astype(bf16)


def _s5_params(a_re, a_im, log_step, b_re_t, b_im_t, c_re, c_im):
    g, p = a_re.shape
    ns = S5_SLAB_STATE
    return pl.pallas_call(
        _s5_param_kernel,
        out_shape=[jax.ShapeDtypeStruct((S5_SLABS, SUBLANES, ns), f32),
                   jax.ShapeDtypeStruct((S5_SLABS, SUBLANES, ns), f32),
                   jax.ShapeDtypeStruct((S5_SLABS, LANES, 2 * ns), bf16),
                   jax.ShapeDtypeStruct((S5_SLABS, 2 * ns, LANES), bf16)],
        scratch_shapes=[pltpu.VMEM((S5_SLABS, LANES, 2 * ns), f32),
                        pltpu.VMEM((S5_SLABS, 2 * ns, LANES), f32)],
        name="s5_params",
    )(a_re.reshape(g, 1, p), a_im.reshape(g, 1, p), log_step.reshape(g, 1, 1), b_re_t, b_im_t, c_re, c_im)


def _gelu(x):
    return 0.5 * x * (1.0 + lax.erf(x * (2.0 ** -0.5)))


def _s5_slabs_interleaved(ut_ref, hbuf_ref, hst_ref, g_ref, ar_ref, ai_ref, bblk_ref, cblk_ref, dsk_ref, tl,
                          row_block=256, n_stages=8):
    ns = S5_SLAB_STATE
    nb = SUBLANES
    rows = nb * tl

    def lanes(s):
        return slice(s * LANES, (s + 1) * LANES)

    def bu_stage(slabs):
        for s in slabs:
            for r in range(0, rows, row_block):
                us = ut_ref[s, r:r + row_block, :]
                hbuf_ref[s, nb + r:nb + r + row_block, :] = jnp.dot(us.astype(bf16), bblk_ref[s],
                                                                    preferred_element_type=f32)
                yield

    def scan_stage(slabs):
        carry = [(hst_ref[s, :, :ns], hst_ref[s, :, ns:]) for s in slabs]
        coef = [(ar_ref[s], ai_ref[s]) for s in slabs]
        for t in range(tl):
            r = nb + nb * t
            for i, s in enumerate(slabs):
                (pr, pi), (ar, ai) = carry[i], coef[i]
                nr = ar * pr - ai * pi + hbuf_ref[s, r:r + nb, :ns]
                ni = ar * pi + ai * pr + hbuf_ref[s, r:r + nb, ns:]
                hbuf_ref[s, r:r + nb, :ns] = nr
                hbuf_ref[s, r:r + nb, ns:] = ni
                carry[i] = (nr, ni)
            if (t + 1) % (tl // n_stages) == 0:
                yield
        for i, s in enumerate(slabs):
            hst_ref[s, :, :ns] = carry[i][0]
            hst_ref[s, :, ns:] = carry[i][1]

    def y_stage(slabs):
        for s in slabs:
            for r in range(0, rows, row_block):
                h16 = hbuf_ref[s, nb + r:nb + r + row_block, :].astype(bf16)
                ys = jnp.dot(h16, cblk_ref[s], preferred_element_type=f32)
                ys = ys + dsk_ref[:, lanes(s)] * ut_ref[s, r:r + row_block, :]
                g_ref[s, r:r + row_block, :] = _gelu(ys)
                yield

    half = S5_SLABS // 2
    first, second = tuple(range(half)), tuple(range(half, S5_SLABS))
    _interleave(bu_stage(first))
    _interleave(scan_stage(first), bu_stage(second))
    _interleave(scan_stage(second), y_stage(first))
    _interleave(y_stage(second))


def _s5_scan_vmem(hbuf_ref, hst_ref, ar_ref, ai_ref, s, nb, rows):
    ns = S5_SLAB_STATE
    hbuf_ref[s, 0:nb, :] = hst_ref[s]
    ar, ai = ar_ref[s], ai_ref[s]

    def body(i, carry):
        j = pl.multiple_of(nb + i * SUBLANES, SUBLANES)
        prev = hbuf_ref[s, pl.ds(j - nb, SUBLANES), :]
        cur = hbuf_ref[s, pl.ds(j, SUBLANES), :]
        pr, pi = prev[:, :ns], prev[:, ns:]
        hbuf_ref[s, pl.ds(j, SUBLANES), :ns] = ar * pr - ai * pi + cur[:, :ns]
        hbuf_ref[s, pl.ds(j, SUBLANES), ns:] = ar * pi + ai * pr + cur[:, ns:]
        return carry

    lax.fori_loop(0, rows // SUBLANES, body, 0, unroll=min(8, nb // SUBLANES))
    hst_ref[s] = hbuf_ref[s, rows:rows + nb, :]


S5_N_IN = 10


def _s5_kernel(*refs, nb, tl, batch_major, n_cast):
    (u_ref, re0_ref, im0_ref, ar_ref, ai_ref, bblk_ref, cblk_ref, dsk_ref, wglu_ref,
     bglu_ref) = refs[:S5_N_IN]
    cast_in = refs[S5_N_IN:S5_N_IN + n_cast]
    y_ref, ren_ref, imn_ref = refs[S5_N_IN + n_cast:S5_N_IN + n_cast + 3]
    cast_out = refs[S5_N_IN + n_cast + 3:S5_N_IN + 2 * n_cast + 3]
    ut_ref, hbuf_ref, hst_ref, g_ref = refs[S5_N_IN + 2 * n_cast + 3:]
    for w_ref, w16_ref in zip(cast_in, cast_out):
        w16_ref[...] = w_ref[...].astype(bf16)
    step = pl.program_id(0)
    rows = nb * tl
    ns = S5_SLAB_STATE

    @pl.when(step == 0)
    def _():
        for s in range(S5_SLABS):
            hst_ref[s, :, :ns] = re0_ref[:, s * ns:(s + 1) * ns]
            hst_ref[s, :, ns:] = im0_ref[:, s * ns:(s + 1) * ns]

    for s in range(S5_SLABS):
        sl = slice(s * LANES, (s + 1) * LANES)
        if batch_major:
            for b in range(nb):
                ut_ref[s, pl.ds(b, tl, stride=nb), :] = u_ref[b, :, sl]
        else:
            ut_ref[s] = u_ref[:, :, sl].reshape(rows, LANES)

    if nb == SUBLANES:
        _s5_slabs_interleaved(ut_ref, hbuf_ref, hst_ref, g_ref, ar_ref, ai_ref, bblk_ref, cblk_ref, dsk_ref, tl)
    else:
        for s in range(S5_SLABS):
            sl = slice(s * LANES, (s + 1) * LANES)
            hbuf_ref[s, nb:nb + rows, :] = jnp.dot(ut_ref[s].astype(bf16), bblk_ref[s],
                                                   preferred_element_type=f32)
            _s5_scan_vmem(hbuf_ref, hst_ref, ar_ref, ai_ref, s, nb, rows)
            ys = jnp.dot(hbuf_ref[s, nb:nb + rows, :].astype(bf16), cblk_ref[s], preferred_element_type=f32)
            g_ref[s] = _gelu(ys + dsk_ref[:, sl] * ut_ref[s])

    g = jnp.concatenate([g_ref[s] for s in range(S5_SLABS)], axis=-1)
    gate = jnp.dot(g.astype(bf16), wglu_ref[...], preferred_element_type=f32) + bglu_ref[...]
    out = g * jax.nn.sigmoid(gate)
    if batch_major:
        for s in range(S5_SLABS):
            sl = slice(s * LANES, (s + 1) * LANES)
            ut_ref[s] = out[:, sl]
            for b in range(nb):
                y_ref[b, :, sl] = ut_ref[s, pl.ds(b, tl, stride=nb), :]
    else:
        y_ref[...] = out.reshape(tl, nb, D_S5)

    @pl.when(step == pl.num_programs(0) - 1)
    def _():
        for s in range(S5_SLABS):
            ren_ref[:, s * ns:(s + 1) * ns] = hst_ref[s, :, :ns]
            imn_ref[:, s * ns:(s + 1) * ns] = hst_ref[s, :, ns:]


def _s5(u, re0, im0, ar, ai, bblk, cblk, dsk, wglu, bglu, *, tl, batch_major, cast=()):
    if batch_major:
        nb, L, _ = u.shape
        u_spec = pl.BlockSpec((nb, tl, D_S5), lambda i: (0, i, 0))
    else:
        L, nb, _ = u.shape
        u_spec = pl.BlockSpec((tl, nb, D_S5), lambda i: (i, 0, 0))
    rows = nb * tl
    steps = L // tl
    nstate = S5_GROUPS * S5_STATE
    st_spec = _const_spec((nb, nstate))
    cast_specs = [pl.BlockSpec((w.shape[0] // steps, w.shape[1]), lambda i: (i, 0)) for w in cast]
    assert all(w.shape[0] % (steps * 2 * SUBLANES) == 0 for w in cast)
    return pl.pallas_call(
        functools.partial(_s5_kernel, nb=nb, tl=tl, batch_major=batch_major, n_cast=len(cast)),
        grid=(steps,),
        in_specs=[u_spec, st_spec, st_spec,
                  _const_spec((S5_SLABS, SUBLANES, S5_SLAB_STATE)),
                  _const_spec((S5_SLABS, SUBLANES, S5_SLAB_STATE)),
                  _const_spec((S5_SLABS, LANES, 2 * S5_SLAB_STATE)),
                  _const_spec((S5_SLABS, 2 * S5_SLAB_STATE, LANES)),
                  _const_spec((1, D_S5)), _const_spec((D_S5, D_S5)), _const_spec((1, D_S5))] + cast_specs,
        out_specs=[u_spec, st_spec, st_spec] + cast_specs,
        out_shape=[jax.ShapeDtypeStruct(u.shape, f32),
                   jax.ShapeDtypeStruct((nb, nstate), f32),
                   jax.ShapeDtypeStruct((nb, nstate), f32)]
                  + [jax.ShapeDtypeStruct(w.shape, bf16) for w in cast],
        scratch_shapes=[pltpu.VMEM((S5_SLABS, rows, LANES), f32),
                        pltpu.VMEM((S5_SLABS, nb + rows, 2 * S5_SLAB_STATE), f32),
                        pltpu.VMEM((S5_SLABS, nb, 2 * S5_SLAB_STATE), f32),
                        pltpu.VMEM((S5_SLABS, rows, LANES), f32)],
        compiler_params=_params("arbitrary"),
        name="s5_mixer",
    )(u, re0, im0, ar, ai, bblk, cblk, dsk, wglu, bglu, *cast)


def _ffn_stages(x, ys, y5, mods, n2g, nfg, wo_ref, wg_ref, wu_ref, wd_ref, write_out, ff_chunk):
    g1, sh2, sc2, g2, shf, scf = mods
    att = jnp.dot(ys.astype(bf16), wo_ref[:D_SSD, :], preferred_element_type=f32)
    att = att + jnp.dot(y5.astype(bf16), wo_ref[D_SSD:, :], preferred_element_type=f32)
    yield
    x1 = x + _by_row(lambda v, s: v * s, att, g1)
    v = _rms_mod(x1, n2g, sc2, sh2).astype(bf16)
    ff = None
    for o in range(0, D_FF, ff_chunk):
        gate = jnp.dot(v, wg_ref[:, o:o + ff_chunk], preferred_element_type=f32)
        up = jnp.dot(v, wu_ref[:, o:o + ff_chunk], preferred_element_type=f32)
        hid = (_silu(gate) * up).astype(bf16)
        part = jnp.dot(hid, wd_ref[o:o + ff_chunk, :], preferred_element_type=f32)
        ff = part if ff is None else ff + part
        yield
    x2 = x1 + _by_row(lambda v, s: v * s, ff, g2)
    write_out(_rms_mod(x2, nfg, scf, shf))


def _ffn_kernel(x_ref, ys_ref, y5_ref, mod_ref, modf_ref, n2g_ref, nfg_ref, wo_ref, wg_ref, wu_ref, wd_ref,
                o_ref, *, seq_major, ff_chunk, sub):
    mods = (_mod_row(mod_ref, 2, seq_major), _mod_row(mod_ref, 3, seq_major), _mod_row(mod_ref, 4, seq_major),
            _mod_row(mod_ref, 5, seq_major), _mod_row(modf_ref, 0, seq_major), _mod_row(modf_ref, 1, seq_major))
    x = _rows_in(x_ref, seq_major)
    rows = x.shape[0]
    tiles = {}

    def stages(r0):
        def write_out(y):
            if seq_major:
                tiles[r0] = y
            else:
                o_ref[0, r0:r0 + sub, :] = y
        return _ffn_stages(x[r0:r0 + sub], ys_ref[0, r0:r0 + sub, :], y5_ref[0, r0:r0 + sub, :], mods,
                           n2g_ref[...], nfg_ref[...], wo_ref, wg_ref, wu_ref, wd_ref, write_out, ff_chunk)

    _interleave(*[stages(r0) for r0 in range(0, rows, sub)])
    if seq_major:
        s, t, d = o_ref.shape
        y = jnp.concatenate([tiles[r0] for r0 in range(0, rows, sub)], axis=0)
        o_ref[...] = jnp.swapaxes(y.reshape(t, s, d), 0, 1)


def _ffn(x, ys, y5, mod, modf, n2g, nfg, wo, wg, wu, wd, *, tm, sub, seq_major, n_seq, first_seq, ff_chunk=256):
    if seq_major:
        x_spec = _const_spec(x.shape)
        nb, rows, d = 1, x.shape[0] * x.shape[1], x.shape[2]
        assert tm == rows
    else:
        nb, rows, d = x.shape
        x_spec = pl.BlockSpec((1, tm, d), lambda i, j: (i, j, 0))
    nt = rows // tm
    mod_spec = _mod_spec(mod, n_seq, first_seq)
    modf_spec = _mod_spec(modf, n_seq, first_seq)
    blk = lambda wd_: pl.BlockSpec((1, tm, wd_), lambda i, j: (i, j, 0))
    single = dict(pipeline_mode=pl.Buffered(1))
    wspec = lambda shape: pl.BlockSpec(shape, lambda i, j: (0, 0), **single)
    return pl.pallas_call(
        functools.partial(_ffn_kernel, seq_major=seq_major, ff_chunk=ff_chunk, sub=sub),
        grid=(nb, nt),
        in_specs=[x_spec, blk(D_SSD), blk(D_S5), mod_spec, modf_spec,
                  _const_spec((1, d)), _const_spec((1, d)),
                  wspec((d, d)), wspec((d, D_FF)), wspec((d, D_FF)), wspec((D_FF, d))],
        out_specs=x_spec,
        out_shape=jax.ShapeDtypeStruct(x.shape, f32),
        compiler_params=_params("parallel", "parallel"),
        name="out_ffn",
    )(x, ys, y5, mod, modf, n2g, nfg, wo, wg, wu, wd)


def kernel(x_prompt, x_sample, c_prompt, c_sample, state_ssd, state_conv, state_s5_re, state_s5_im, w_ada, b_ada, norm1_g, w_in, conv_w, conv_b, ssd_dt_bias, ssd_A_log, ssd_D, ssd_norm_g, s5_A_re, s5_A_im, s5_log_step, s5_B_re, s5_B_im, s5_C_re, s5_C_im, s5_D, w_glu, b_glu, w_out, norm2_g, w_ffn_gate, w_ffn_up, w_ffn_down, w_ada_f, b_ada_f, normf_g):
    assert w_ada.shape[0] == 1, "single-layer stack"
    bp, seq, d = x_prompt.shape
    bs, dseq, _ = x_sample.shape

    mod, modf = _ada_mod(c_sample, c_prompt, (w_ada[0], w_ada_f), (b_ada[0], b_ada_f))
    seqs_s = dict(n_seq=bs, first_seq=0)
    seqs_p = dict(n_seq=bp, first_seq=bs)

    w_in_p = w_in[0].T
    pad_h = lambda v: jnp.concatenate([v, jnp.zeros((DT_PAD - HEADS,), f32)]).reshape(1, DT_PAD)
    dtb = pad_h(ssd_dt_bias[0])
    alog = pad_h(ssd_A_log[0])
    dexp = jnp.repeat(ssd_D[0], HEAD_DIM).reshape(1, D_SSD)
    ng = ssd_norm_g[0].reshape(1, D_SSD)
    cw, cb = conv_w[0], conv_b[0].reshape(1, CONV_DIM)

    ar, ai, bblk, cblk = _s5_params(s5_A_re[0], s5_A_im[0], s5_log_step[0],
                                    s5_B_re[0].transpose(0, 2, 1), s5_B_im[0].transpose(0, 2, 1),
                                    s5_C_re[0], s5_C_im[0])
    dsk = s5_D[0].reshape(1, D_S5)
    wglu = w_glu[0].astype(bf16)
    bglu = b_glu[0].reshape(1, D_S5)
    n1g, n2g, nfg = norm1_g[0].reshape(1, d), norm2_g[0].reshape(1, d), normf_g.reshape(1, d)
    nstate = S5_GROUPS * S5_STATE

    z, act, u5, dt, conv_p = _inproj(x_prompt, mod, n1g, w_in_p, (cw, cb), tm=1024, sub=512, seq_major=False,
                                     **seqs_p)
    y_ssd, ssd_p = _ssd_prompt(act, z, dt, dtb, alog, dexp, ng)
    zeros_st = jnp.zeros((bp, nstate), f32)
    y_s5, re_p, im_p, wo, wg, wu, wd = _s5(
        u5, zeros_st, zeros_st, ar, ai, bblk, cblk, dsk, wglu, bglu, tl=128, batch_major=True,
        cast=(w_out[0], w_ffn_gate[0], w_ffn_up[0], w_ffn_down[0]))
    y_prompt = _ffn(x_prompt, y_ssd, y_s5, mod, modf, n2g, nfg, wo, wg, wu, wd, tm=1024, sub=512,
                    seq_major=False, **seqs_p)

    rows_s = dseq * bs
    steps = lambda a: a.reshape(dseq, bs, a.shape[-1])
    flat = lambda a: a.reshape(1, rows_s, a.shape[-1])
    z, xbc, u5, dt = _inproj(x_sample, mod, n1g, w_in_p, tm=rows_s, sub=rows_s // 2, seq_major=True, **seqs_s)
    y_ssd, ssd_s, conv_s = _ssd_sample(steps(xbc), steps(z), steps(dt), state_conv[0].transpose(1, 0, 2),
                                       state_ssd[0], cw, cb, dtb, alog, dexp, ng)
    y_s5, re_s, im_s = _s5(steps(u5), state_s5_re[0].reshape(bs, nstate), state_s5_im[0].reshape(bs, nstate),
                           ar, ai, bblk, cblk, dsk, wglu, bglu, tl=dseq, batch_major=False)
    y_sample = _ffn(x_sample, flat(y_ssd), flat(y_s5), mod, modf, n2g, nfg, wo, wg, wu, wd,
                    tm=rows_s, sub=rows_s, seq_major=True, **seqs_s)

    g5 = (S5_GROUPS, S5_STATE)
    return (y_prompt, y_sample,
            ssd_p[None], ssd_s.reshape((1,) + state_ssd.shape[1:]),
            conv_p[None], conv_s.transpose(1, 0, 2)[None],
            re_p.reshape((1, bp) + g5), re_s.reshape((1, bs) + g5),
            im_p.reshape((1, bp) + g5), im_s.reshape((1, bs) + g5))
```

```python
import functools

import jax
import jax.numpy as jnp
from jax import lax
from jax.experimental import pallas as pl
from jax.experimental.pallas import tpu as pltpu

f32 = jnp.float32
bf16 = jnp.bfloat16

D_MODEL = 1024
D_SSD = 512
HEAD_DIM = 64
HEADS = 8
GROUPS = 2
HEADS_PER_GROUP = HEADS // GROUPS
STATE = 128
CONV_W = 4
CONV_DIM = D_SSD + 2 * GROUPS * STATE
D_S5 = 512
S5_CH = 16
S5_GROUPS = 32
S5_STATE = 64
D_FF = 2816
N_ADA = 6
EPS = 1e-6

LANES = 128
SUBLANES = 8
SSD_CHUNK = 128
S5_SLABS = D_S5 // LANES
S5_SLAB_STATE = (S5_GROUPS // S5_SLABS) * S5_STATE
DT_PAD = LANES
VMEM_LIMIT = 56 * 1024 * 1024

NT_DIMS = (((1,), (1,)), ((), ()))
TN_DIMS = (((0,), (0,)), ((), ()))


def _silu(x):
    h = 0.5 * x
    return h + h * jnp.tanh(h)


def _interleave(*gens, lead=0):
    pending, live, rnd = list(gens), [], 0
    while pending or live:
        while pending and rnd >= lead * (len(gens) - len(pending)):
            live.append(pending.pop(0))
        for g in list(live):
            try:
                next(g)
            except StopIteration:
                live.remove(g)
        rnd += 1


def _by_row(fn, v, *ms):
    r, d = v.shape
    m_rows = ms[0].shape[0]
    if m_rows in (1, r):
        return fn(v, *ms)
    out = fn(v.reshape(r // m_rows, m_rows, d), *[m[None] for m in ms])
    return out.reshape(r, d)


def _rms_mod(x, g, sc, sh):
    y = x * lax.rsqrt(jnp.mean(x * x, axis=-1, keepdims=True) + EPS)
    return _by_row(lambda v, s, t: v * (1.0 + s) + t, y * g, sc, sh)


def _mod_row(mod_ref, i, per_row):
    return mod_ref[i] if per_row else mod_ref[i, pl.ds(pl.program_id(0), 1), :]


def _mod_spec(mod, n_seq, first_seq):
    assert first_seq % n_seq == 0
    return pl.BlockSpec((mod.shape[0], n_seq, mod.shape[2]), lambda *_: (0, first_seq // n_seq, 0))


def _const_spec(shape):
    nd = len(shape)
    return pl.BlockSpec(shape, lambda *_: (0,) * nd)


def _params(*sem):
    return pltpu.CompilerParams(dimension_semantics=sem, vmem_limit_bytes=VMEM_LIMIT)


ADA_K_BLOCK = 256


def _ada_kernel(ca_ref, cb_ref, *refs):
    nw = len(refs) // 3
    s = _silu(jnp.concatenate([ca_ref[...], cb_ref[...]], axis=0)).astype(bf16)
    for w_ref, b_ref, o_ref in zip(refs[:nw], refs[nw:2 * nw], refs[2 * nw:]):
        n, _, d = o_ref.shape

        @pl.when(pl.program_id(0) == 0)
        def _():
            for j in range(n):
                o_ref[j] = jnp.broadcast_to(b_ref[:, j * d:(j + 1) * d], o_ref.shape[1:])

        for j in range(n):
            o_ref[j] += jnp.dot(s, w_ref[:, j * d:(j + 1) * d].astype(bf16), preferred_element_type=f32)


def _ada_mod(ca, cb, ws, bs):
    k = ca.shape[1]
    m = ca.shape[0] + cb.shape[0]
    ns = [w.shape[1] // k for w in ws]
    return pl.pallas_call(
        _ada_kernel,
        grid=(k // ADA_K_BLOCK,),
        in_specs=[pl.BlockSpec((c.shape[0], ADA_K_BLOCK), lambda i: (0, i)) for c in (ca, cb)]
                 + [pl.BlockSpec((ADA_K_BLOCK, n * k), lambda i: (i, 0)) for n in ns]
                 + [_const_spec((1, n * k)) for n in ns],
        out_specs=[_const_spec((n, m, k)) for n in ns],
        out_shape=[jax.ShapeDtypeStruct((n, m, k), f32) for n in ns],
        compiler_params=_params("arbitrary"),
        name="ada_mod",
    )(ca, cb, *ws, *[b.reshape(1, n * k) for b, n in zip(bs, ns)])


def _rows_in(x_ref, seq_major):
    if not seq_major:
        return x_ref[0]
    s, t, d = x_ref.shape
    return jnp.swapaxes(x_ref[...], 0, 1).reshape(s * t, d)


def _conv_silu(ext, cw_ref, cb_ref):
    w = [cw_ref[k:k + 1, :] for k in range(CONV_W)]
    back1 = pltpu.roll(ext, 1, 0)
    older = pltpu.roll(ext * w[1] + back1 * w[0], 2, 0)
    conv = cb_ref[...] + ext[SUBLANES:, :] * w[3] + back1[SUBLANES:, :] * w[2] + older[SUBLANES:, :]
    return _silu(conv)


def _inproj_stages(x, sh, sc, g, outs, after=()):
    u = _rms_mod(x, g, sc, sh).astype(bf16)
    yield
    for write, w16 in outs:
        write(lax.dot_general(u, w16, NT_DIMS, preferred_element_type=f32))
        yield
    for stage in after:
        stage()
        yield


def _inproj_kernel(x_ref, mod_ref, g_ref, w_ref, *rest, seq_major, sub, conv):
    if conv:
        cw_ref, cb_ref, z_ref, act_ref, u5_ref, dt_ref, cn_ref, ext_ref = rest

        @pl.when(pl.program_id(1) == 0)
        def _():
            ext_ref[0:SUBLANES, :] = jnp.zeros((SUBLANES, CONV_DIM), f32)
    else:
        z_ref, xbc_ref, u5_ref, dt_ref = rest
    sh = _mod_row(mod_ref, 0, seq_major)
    sc = _mod_row(mod_ref, 1, seq_major)
    x = _rows_in(x_ref, seq_major)
    rows = x.shape[0]
    T = SSD_CHUNK
    o_dt = D_SSD + CONV_DIM
    w16 = lambda o, width: w_ref[o:o + width, :].astype(bf16)
    w_z, w_xbc, w_u5, w_dt = w16(0, D_SSD), w16(D_SSD, CONV_DIM), w16(o_dt + HEADS, D_S5), w16(o_dt, DT_PAD)

    def stages(r0):
        def to(ref):
            def write(v):
                ref[0, r0:r0 + sub, :] = v
            return write

        def to_ext(v):
            ext_ref[SUBLANES + r0:SUBLANES + r0 + sub, :] = v

        def conv_chunk(c0):
            def stage():
                act_ref[0, c0:c0 + T, :] = _conv_silu(ext_ref[c0:c0 + SUBLANES + T, :], cw_ref, cb_ref)
            return stage

        outs = [(to(z_ref), w_z), (to_ext if conv else to(xbc_ref), w_xbc), (to(u5_ref), w_u5), (to(dt_ref), w_dt)]
        after = [conv_chunk(c0) for c0 in range(r0, r0 + sub, T)] if conv else ()
        return _inproj_stages(x[r0:r0 + sub], sh, sc, g_ref[...], outs, after)

    n_matmul_stages = 5
    _interleave(*[stages(r0) for r0 in range(0, rows, sub)], lead=n_matmul_stages if conv else 0)
    if conv:
        cn_ref[0] = ext_ref[SUBLANES + rows - (CONV_W - 1):SUBLANES + rows, :]
        ext_ref[0:SUBLANES, :] = ext_ref[rows:rows + SUBLANES, :]


def _inproj(x, mod, g, w, conv_wb=None, *, tm, sub, seq_major, n_seq, first_seq):
    if seq_major:
        x_spec = _const_spec(x.shape)
        nb, rows, d = 1, x.shape[0] * x.shape[1], x.shape[2]
        assert tm == rows
    else:
        nb, rows, d = x.shape
        x_spec = pl.BlockSpec((1, tm, d), lambda i, j: (i, j, 0))
    nt = rows // tm
    widths = (D_SSD, CONV_DIM, D_S5, DT_PAD)
    conv = conv_wb is not None
    in_specs = [x_spec, _mod_spec(mod, n_seq, first_seq), _const_spec((1, d)),
                pl.BlockSpec(w.shape, lambda i, j: (0, 0), pipeline_mode=pl.Buffered(1))]
    out_specs = [pl.BlockSpec((1, tm, wd), lambda i, j: (i, j, 0)) for wd in widths]
    out_shape = [jax.ShapeDtypeStruct((nb, rows, wd), f32) for wd in widths]
    scratch = []
    if conv:
        in_specs += [_const_spec((CONV_W, CONV_DIM)), _const_spec((1, CONV_DIM))]
        out_specs.append(pl.BlockSpec((1, CONV_W - 1, CONV_DIM), lambda i, j: (i, 0, 0)))
        out_shape.append(jax.ShapeDtypeStruct((nb, CONV_W - 1, CONV_DIM), f32))
        scratch.append(pltpu.VMEM((SUBLANES + tm, CONV_DIM), f32))
    return pl.pallas_call(
        functools.partial(_inproj_kernel, seq_major=seq_major, sub=sub, conv=conv),
        grid=(nb, nt),
        in_specs=in_specs, out_specs=out_specs, out_shape=out_shape, scratch_shapes=scratch,
        compiler_params=_params("parallel", "arbitrary" if conv else "parallel"),
        name="in_proj",
    )(x, mod, g, w, *(conv_wb or ()))


def _split3(x):
    hi = x.astype(bf16)
    r1 = x - hi.astype(f32)
    mid = r1.astype(bf16)
    lo = (r1 - mid.astype(f32)).astype(bf16)
    return hi, mid, lo


def _dot_sel_lhs(sel16, x):
    return sum(jnp.dot(sel16, p, preferred_element_type=f32) for p in _split3(x))


def _dot_sel_rhs(x, sel16):
    hi, mid, _ = _split3(x)
    return jnp.dot(hi, sel16, preferred_element_type=f32) + jnp.dot(mid, sel16, preferred_element_type=f32)


def _gated_group_norm(y, z, ng):
    y = y * _silu(z)
    gw = D_SSD // GROUPS
    parts = []
    for g in range(GROUPS):
        yg = y[:, g * gw:(g + 1) * gw]
        parts.append(yg * lax.rsqrt(jnp.mean(yg * yg, axis=-1, keepdims=True) + EPS))
    return jnp.concatenate(parts, axis=-1) * ng


def _ssd_prompt_kernel(act_ref, z_ref, dt_ref, dtb_ref, alog_ref, dexp_ref, ng_ref, y_ref, st_ref, h_ref, *, cps):
    T = SSD_CHUNK
    gw = HEADS_PER_GROUP * HEAD_DIM

    @pl.when(pl.program_id(1) == 0)
    def _():
        h_ref[...] = jnp.zeros_like(h_ref)

    row = lax.broadcasted_iota(jnp.int32, (T, T), 0)
    col = lax.broadcasted_iota(jnp.int32, (T, T), 1)
    tri = row >= col
    tri16 = jnp.where(tri, 1.0, 0.0).astype(bf16)
    low_half = lax.broadcasted_iota(jnp.int32, (T, LANES), 1) < HEAD_DIM
    a_neg = -jnp.exp(alog_ref[...])

    n_fac = 3
    src = lax.broadcasted_iota(jnp.int32, (2 * DT_PAD, n_fac * D_SSD), 0) % DT_PAD
    dst = lax.broadcasted_iota(jnp.int32, (2 * DT_PAD, n_fac * D_SSD), 1)
    sel16 = jnp.where(src == dst // D_SSD * HEADS + dst % D_SSD // HEAD_DIM, 1.0, 0.0).astype(bf16)
    lane = lax.broadcasted_iota(jnp.int32, (T, DT_PAD), 1)

    def spread(q):
        hi = q.astype(bf16)
        mid = (q - hi.astype(f32)).astype(bf16)
        return jnp.dot(jnp.concatenate([hi, mid], axis=1), sel16, preferred_element_type=f32)

    def state_free(ci):
        r0 = ci * T
        act = act_ref[0, r0:r0 + T, :]
        xs = act[:, :D_SSD]
        bm = act[:, D_SSD:D_SSD + GROUPS * STATE]
        cm = act[:, D_SSD + GROUPS * STATE:]
        dtv = jax.nn.softplus(dt_ref[0, r0:r0 + T, :] + dtb_ref[...])
        a_cs = _dot_sel_lhs(tri16, dtv * a_neg)
        bg16s = [bm[:, g * STATE:(g + 1) * STATE].astype(bf16) for g in range(GROUPS)]
        cg16s = [cm[:, g * STATE:(g + 1) * STATE].astype(bf16) for g in range(GROUPS)]
        gmats = [lax.dot_general(cg16s[g], bg16s[g], NT_DIMS, preferred_element_type=f32)
                 for g in range(GROUPS)]
        yield
        a_last = a_cs[T - 1:T, :]
        a_cs_t = a_cs.T
        a_one = pltpu.roll(a_cs, HEADS, 1)
        a_two = pltpu.roll(a_cs, 2 * HEADS, 1)
        decay = jnp.exp(jnp.where(lane < 2 * HEADS, a_one[T - 1:T, :] - a_one, a_two))
        rep = spread(jnp.where(lane < HEADS, dtv, jnp.where(lane < n_fac * HEADS, decay, 0.0)))
        yield
        x = xs * rep[:, :D_SSD]
        x16 = x.astype(bf16)
        xd16 = (x * rep[:, D_SSD:2 * D_SSD]).astype(bf16)
        e_cs = rep[:, 2 * D_SSD:]
        yield
        y_part = []
        for g in range(GROUPS):
            for pr in range(HEADS_PER_GROUP // 2):
                s_pair = []
                for q in range(2):
                    h = g * HEADS_PER_GROUP + 2 * pr + q
                    seg = a_cs[:, h:h + 1] - a_cs_t[h:h + 1, :]
                    lmat = jnp.exp(jnp.where(tri, seg, -jnp.inf))
                    s_pair.append((gmats[g] * lmat).astype(bf16))
                xp = x16[:, g * gw + pr * LANES:g * gw + (pr + 1) * LANES]
                zero = jnp.zeros_like(xp)
                x_pair = jnp.concatenate([jnp.where(low_half, xp, zero), jnp.where(low_half, zero, xp)], axis=0)
                y_part.append(jnp.dot(jnp.concatenate(s_pair, axis=1), x_pair, preferred_element_type=f32))
                yield
        y_free = jnp.concatenate(y_part, axis=-1) + xs * dexp_ref[...]
        return y_free, e_cs, jnp.exp(a_last), xd16, bg16s, cg16s

    def state_step(ci, y_free, e_cs, e_last, xd16, bg16s, cg16s):
        r0 = ci * T
        y_offs = []
        for g in range(GROUPS):
            hp = h_ref[g]
            y_offs.append(lax.dot_general(cg16s[g], hp.astype(bf16), NT_DIMS, preferred_element_type=f32))
            upd = lax.dot_general(xd16[:, g * gw:(g + 1) * gw], bg16s[g], TN_DIMS, preferred_element_type=f32)
            for hh in range(HEADS_PER_GROUP):
                h = g * HEADS_PER_GROUP + hh
                rows = slice(hh * HEAD_DIM, (hh + 1) * HEAD_DIM)
                h_ref[g, rows, :] = e_last[:, h:h + 1] * hp[rows] + upd[rows]
        y = y_free + jnp.concatenate(y_offs, axis=-1) * e_cs
        y_ref[0, r0:r0 + T, :] = _gated_group_norm(y, z_ref[0, r0:r0 + T, :], ng_ref[...])

    gens = [state_free(ci) for ci in range(cps)]
    free = [None] * cps
    while any(f is None for f in free):
        for ci in range(cps):
            if free[ci] is None:
                try:
                    next(gens[ci])
                except StopIteration as done:
                    free[ci] = done.value
    for ci in range(cps):
        state_step(ci, *free[ci])

    @pl.when(pl.program_id(1) == pl.num_programs(1) - 1)
    def _():
        st_ref[0] = h_ref[...].reshape(HEADS, HEAD_DIM, STATE)


def _ssd_prompt(act, z, dt, dtb, alog, dexp, ng, *, cps=8):
    nb, L, _ = act.shape
    rows = cps * SSD_CHUNK
    blk = lambda wd: pl.BlockSpec((1, rows, wd), lambda b, c: (b, c, 0))
    return pl.pallas_call(
        functools.partial(_ssd_prompt_kernel, cps=cps),
        grid=(nb, L // rows),
        in_specs=[blk(CONV_DIM), blk(D_SSD), blk(DT_PAD), _const_spec((1, DT_PAD)), _const_spec((1, DT_PAD)),
                  _const_spec((1, D_SSD)), _const_spec((1, D_SSD))],
        out_specs=[blk(D_SSD), pl.BlockSpec((1, HEADS, HEAD_DIM, STATE), lambda b, c: (b, 0, 0, 0))],
        out_shape=[jax.ShapeDtypeStruct((nb, L, D_SSD), f32),
                   jax.ShapeDtypeStruct((nb, HEADS, HEAD_DIM, STATE), f32)],
        scratch_shapes=[pltpu.VMEM((GROUPS, HEADS_PER_GROUP * HEAD_DIM, STATE), f32)],
        compiler_params=_params("parallel", "arbitrary"),
        name="ssd_prompt",
    )(act, z, dt, dtb, alog, dexp, ng)


def _ssd_sample_kernel(xbc_ref, z_ref, dt_ref, cbuf_ref, st_ref, cw_ref, cb_ref, dtb_ref, alog_ref,
                       dexp_ref, ng_ref, y_ref, stn_ref, cn_ref, dtot_ref, *, L, nb):
    gw = HEADS_PER_GROUP * HEAD_DIM
    full = [cbuf_ref[j] for j in range(CONV_W - 1)] + [xbc_ref[t] for t in range(L)]
    for j in range(CONV_W - 1):
        cn_ref[j] = full[L + j]
    a_neg = -jnp.exp(alog_ref[...])
    xs, bm, cm, dtv, acs = [], [], [], [], []
    run = None
    for t in range(L):
        conv = cb_ref[...]
        for k in range(CONV_W):
            conv = conv + full[t + k] * cw_ref[k:k + 1, :]
        act = _silu(conv)
        xs.append(act[:, :D_SSD])
        bm.append(act[:, D_SSD:D_SSD + GROUPS * STATE])
        cm.append(act[:, D_SSD + GROUPS * STATE:])
        d = jax.nn.softplus(dt_ref[t] + dtb_ref[...])
        dtv.append(d)
        run = d * a_neg if run is None else run + d * a_neg
        acs.append(run)
    a_tot = acs[L - 1]
    dtot_ref[...] = jnp.exp(a_tot)

    lane_head = lax.broadcasted_iota(jnp.int32, (DT_PAD, D_SSD), 1) // HEAD_DIM
    sel16 = jnp.where(lax.broadcasted_iota(jnp.int32, (DT_PAD, D_SSD), 0) == lane_head, 1.0, 0.0).astype(bf16)
    pairs = [(t, s) for t in range(L) for s in range(t)]
    factors = (dtv + [jnp.exp(a_tot - acs[t]) for t in range(L)] + [jnp.exp(acs[t]) for t in range(L)]
               + [jnp.exp(acs[t] - acs[s]) for t, s in pairs])
    rep = _dot_sel_rhs(jnp.concatenate(factors, axis=0), sel16)
    piece = lambda i: rep[i * nb:(i + 1) * nb]
    x = [xs[t] * piece(t) for t in range(L)]
    xd_stack = jnp.concatenate([x[t] * piece(L + t) for t in range(L)], axis=0)
    e_cs = [piece(2 * L + t) for t in range(L)]
    decay = {ts: piece(3 * L + i) for i, ts in enumerate(pairs)}

    in_group0 = lax.broadcasted_iota(jnp.int32, (nb, D_SSD), 1) < gw
    y_intra = []
    for t in range(L):
        acc = None
        for s in range(t + 1):
            cb_dot = [jnp.sum(cm[t][:, g * STATE:(g + 1) * STATE] * bm[s][:, g * STATE:(g + 1) * STATE],
                              axis=-1, keepdims=True) for g in range(GROUPS)]
            w = jnp.where(in_group0, cb_dot[0], cb_dot[1])
            term = w * x[s] if s == t else w * decay[(t, s)] * x[s]
            acc = term if acc is None else acc + term
        y_intra.append(acc)

    c_stack = [jnp.concatenate([cm[t][:, g * STATE:(g + 1) * STATE] for t in range(L)], axis=0).astype(bf16)
               for g in range(GROUPS)]
    b_stack = [jnp.concatenate([bm[t][:, g * STATE:(g + 1) * STATE] for t in range(L)], axis=0).astype(bf16)
               for g in range(GROUPS)]
    seq_of_row = lax.broadcasted_iota(jnp.int32, (L * nb, gw), 0) & (nb - 1)

    def per_seq(b, yoff):
        mine = seq_of_row == b
        drow = dtot_ref[pl.ds(b, 1), :]
        out = []
        for g in range(GROUPS):
            h0 = st_ref[b, g]
            r = lax.dot_general(c_stack[g], h0.astype(bf16), NT_DIMS, preferred_element_type=f32)
            out.append(yoff[g] + jnp.where(mine, r, 0.0))
            xm = jnp.where(mine, xd_stack[:, g * gw:(g + 1) * gw], 0.0).astype(bf16)
            upd = lax.dot_general(xm, b_stack[g], TN_DIMS, preferred_element_type=f32)
            for hh in range(HEADS_PER_GROUP):
                h = g * HEADS_PER_GROUP + hh
                rows = slice(hh * HEAD_DIM, (hh + 1) * HEAD_DIM)
                stn_ref[b, g, rows, :] = drow[:, h:h + 1] * h0[rows] + upd[rows]
        return tuple(out)

    yoff = lax.fori_loop(0, nb, per_seq, tuple(jnp.zeros((L * nb, gw), f32) for _ in range(GROUPS)),
                         unroll=True)

    for t in range(L):
        y_off = jnp.concatenate([yoff[g][t * nb:(t + 1) * nb] for g in range(GROUPS)], axis=-1)
        y = y_intra[t] + y_off * e_cs[t] + xs[t] * dexp_ref[...]
        y_ref[t] = _gated_group_norm(y, z_ref[t], ng_ref[...])


def _ssd_sample(xbc, z, dt, cbuf, st, cw, cb, dtb, alog, dexp, ng, *, nb=16):
    L, B, _ = xbc.shape
    assert nb & (nb - 1) == 0 and B % nb == 0
    tblk = lambda n, wd: pl.BlockSpec((n, nb, wd), lambda i: (0, i, 0))
    gw = HEADS_PER_GROUP * HEAD_DIM
    st = st.reshape(B, GROUPS, gw, STATE)
    st_spec = pl.BlockSpec((nb, GROUPS, gw, STATE), lambda i: (i, 0, 0, 0))
    return pl.pallas_call(
        functools.partial(_ssd_sample_kernel, L=L, nb=nb),
        grid=(B // nb,),
        in_specs=[tblk(L, CONV_DIM), tblk(L, D_SSD), tblk(L, DT_PAD), tblk(CONV_W - 1, CONV_DIM), st_spec,
                  _const_spec((CONV_W, CONV_DIM)), _const_spec((1, CONV_DIM)),
                  _const_spec((1, DT_PAD)), _const_spec((1, DT_PAD)),
                  _const_spec((1, D_SSD)), _const_spec((1, D_SSD))],
        out_specs=[tblk(L, D_SSD), st_spec, tblk(CONV_W - 1, CONV_DIM)],
        out_shape=[jax.ShapeDtypeStruct((L, B, D_SSD), f32),
                   jax.ShapeDtypeStruct((B, GROUPS, gw, STATE), f32),
                   jax.ShapeDtypeStruct((CONV_W - 1, B, CONV_DIM), f32)],
        scratch_shapes=[pltpu.VMEM((nb, DT_PAD), f32)],
        compiler_params=_params("parallel"),
        name="ssd_sample",
    )(xbc, z, dt, cbuf, st, cw, cb, dtb, alog, dexp, ng)


def _s5_param_kernel(lr_ref, li_ref, ls_ref, br_ref, bi_ref, cr_ref, ci_ref,
                     ar_ref, ai_ref, bblk_ref, cblk_ref, b_scr, c_scr):
    lr, li = lr_ref[...], li_ref[...]
    step = jnp.exp(ls_ref[...])
    mag = jnp.exp(lr * step)
    abr = mag * jnp.cos(li * step)
    abi = mag * jnp.sin(li * step)
    nr, ni = abr - 1.0, abi
    den = lr * lr + li * li
    fr = (nr * lr + ni * li) / den
    fi = (ni * lr - nr * li) / den
    br, bi = br_ref[...], bi_ref[...]
    bbr = fr * br - fi * bi
    bbi = fr * bi + fi * br
    b_scr[...] = jnp.zeros_like(b_scr)
    c_scr[...] = jnp.zeros_like(c_scr)
    gps = S5_GROUPS // S5_SLABS
    ns = S5_SLAB_STATE
    for g in range(S5_GROUPS):
        s, gl = divmod(g, gps)
        ch = slice(gl * S5_CH, (gl + 1) * S5_CH)
        st = slice(gl * S5_STATE, (gl + 1) * S5_STATE)
        st_im = slice(ns + gl * S5_STATE, ns + (gl + 1) * S5_STATE)
        ar_ref[s, :, st] = jnp.broadcast_to(abr[g], (SUBLANES, S5_STATE))
        ai_ref[s, :, st] = jnp.broadcast_to(abi[g], (SUBLANES, S5_STATE))
        b_scr[s, ch, st] = bbr[g]
        b_scr[s, ch, st_im] = bbi[g]
        c_scr[s, st, ch] = cr_ref[g].T
        c_scr[s, st_im, ch] = -ci_ref[g].T
    bblk_ref[...] = b_scr[...].astype(bf16)
    cblk_ref[...] = c_scr[...].astype(bf16)


def _s5_params(a_re, a_im, log_step, b_re_t, b_im_t, c_re, c_im):
    g, p = a_re.shape
    ns = S5_SLAB_STATE
    return pl.pallas_call(
        _s5_param_kernel,
        out_shape=[jax.ShapeDtypeStruct((S5_SLABS, SUBLANES, ns), f32),
                   jax.ShapeDtypeStruct((S5_SLABS, SUBLANES, ns), f32),
                   jax.ShapeDtypeStruct((S5_SLABS, LANES, 2 * ns), bf16),
                   jax.ShapeDtypeStruct((S5_SLABS, 2 * ns, LANES), bf16)],
        scratch_shapes=[pltpu.VMEM((S5_SLABS, LANES, 2 * ns), f32),
                        pltpu.VMEM((S5_SLABS, 2 * ns, LANES), f32)],
        name="s5_params",
    )(a_re.reshape(g, 1, p), a_im.reshape(g, 1, p), log_step.reshape(g, 1, 1), b_re_t, b_im_t, c_re, c_im)


def _gelu(x):
    return 0.5 * x * (1.0 + lax.erf(x * (2.0 ** -0.5)))


def _s5_slabs_interleaved(ut_ref, hbuf_ref, hst_ref, g_ref, ar_ref, ai_ref, bblk_ref, cblk_ref, dsk_ref, tl,
                          row_block=256, n_stages=8):
    ns = S5_SLAB_STATE
    nb = SUBLANES
    rows = nb * tl

    def lanes(s):
        return slice(s * LANES, (s + 1) * LANES)

    def bu_stage(slabs):
        for s in slabs:
            for r in range(0, rows, row_block):
                us = ut_ref[s, r:r + row_block, :]
                hbuf_ref[s, nb + r:nb + r + row_block, :] = jnp.dot(us.astype(bf16), bblk_ref[s],
                                                                    preferred_element_type=f32)
                yield

    def scan_stage(slabs):
        carry = [(hst_ref[s, :, :ns], hst_ref[s, :, ns:]) for s in slabs]
        coef = [(ar_ref[s], ai_ref[s]) for s in slabs]
        for t in range(tl):
            r = nb + nb * t
            for i, s in enumerate(slabs):
                (pr, pi), (ar, ai) = carry[i], coef[i]
                nr = ar * pr - ai * pi + hbuf_ref[s, r:r + nb, :ns]
                ni = ar * pi + ai * pr + hbuf_ref[s, r:r + nb, ns:]
                hbuf_ref[s, r:r + nb, :ns] = nr
                hbuf_ref[s, r:r + nb, ns:] = ni
                carry[i] = (nr, ni)
            if (t + 1) % (tl // n_stages) == 0:
                yield
        for i, s in enumerate(slabs):
            hst_ref[s, :, :ns] = carry[i][0]
            hst_ref[s, :, ns:] = carry[i][1]

    def y_stage(slabs):
        for s in slabs:
            for r in range(0, rows, row_block):
                h16 = hbuf_ref[s, nb + r:nb + r + row_block, :].astype(bf16)
                ys = jnp.dot(h16, cblk_ref[s], preferred_element_type=f32)
                ys = ys + dsk_ref[:, lanes(s)] * ut_ref[s, r:r + row_block, :]
                g_ref[s, r:r + row_block, :] = _gelu(ys)
                yield

    half = S5_SLABS // 2
    first, second = tuple(range(half)), tuple(range(half, S5_SLABS))
    _interleave(bu_stage(first))
    _interleave(scan_stage(first), bu_stage(second))
    _interleave(scan_stage(second), y_stage(first))
    _interleave(y_stage(second))


def _s5_scan_vmem(hbuf_ref, hst_ref, ar_ref, ai_ref, s, nb, rows):
    ns = S5_SLAB_STATE
    hbuf_ref[s, 0:nb, :] = hst_ref[s]
    ar, ai = ar_ref[s], ai_ref[s]

    def body(i, carry):
        j = pl.multiple_of(nb + i * SUBLANES, SUBLANES)
        prev = hbuf_ref[s, pl.ds(j - nb, SUBLANES), :]
        cur = hbuf_ref[s, pl.ds(j, SUBLANES), :]
        pr, pi = prev[:, :ns], prev[:, ns:]
        hbuf_ref[s, pl.ds(j, SUBLANES), :ns] = ar * pr - ai * pi + cur[:, :ns]
        hbuf_ref[s, pl.ds(j, SUBLANES), ns:] = ar * pi + ai * pr + cur[:, ns:]
        return carry

    lax.fori_loop(0, rows // SUBLANES, body, 0, unroll=True)
    hst_ref[s] = hbuf_ref[s, rows:rows + nb, :]


S5_N_IN = 10


def _s5_kernel(*refs, nb, tl, batch_major, n_cast):
    (u_ref, re0_ref, im0_ref, ar_ref, ai_ref, bblk_ref, cblk_ref, dsk_ref, wglu_ref,
     bglu_ref) = refs[:S5_N_IN]
    cast_in = refs[S5_N_IN:S5_N_IN + n_cast]
    y_ref, ren_ref, imn_ref = refs[S5_N_IN + n_cast:S5_N_IN + n_cast + 3]
    cast_out = refs[S5_N_IN + n_cast + 3:S5_N_IN + 2 * n_cast + 3]
    ut_ref, hbuf_ref, hst_ref, g_ref = refs[S5_N_IN + 2 * n_cast + 3:]
    for w_ref, w16_ref in zip(cast_in, cast_out):
        w16_ref[...] = w_ref[...].astype(bf16)
    step = pl.program_id(0)
    rows = nb * tl
    ns = S5_SLAB_STATE

    @pl.when(step == 0)
    def _():
        for s in range(S5_SLABS):
            hst_ref[s, :, :ns] = re0_ref[:, s * ns:(s + 1) * ns]
            hst_ref[s, :, ns:] = im0_ref[:, s * ns:(s + 1) * ns]

    for s in range(S5_SLABS):
        sl = slice(s * LANES, (s + 1) * LANES)
        if batch_major:
            for b in range(nb):
                ut_ref[s, pl.ds(b, tl, stride=nb), :] = u_ref[b, :, sl]
        else:
            ut_ref[s] = u_ref[:, :, sl].reshape(rows, LANES)

    if nb == SUBLANES:
        _s5_slabs_interleaved(ut_ref, hbuf_ref, hst_ref, g_ref, ar_ref, ai_ref, bblk_ref, cblk_ref, dsk_ref, tl)
    else:
        for s in range(S5_SLABS):
            sl = slice(s * LANES, (s + 1) * LANES)
            hbuf_ref[s, nb:nb + rows, :] = jnp.dot(ut_ref[s].astype(bf16), bblk_ref[s],
                                                   preferred_element_type=f32)
            _s5_scan_vmem(hbuf_ref, hst_ref, ar_ref, ai_ref, s, nb, rows)
            ys = jnp.dot(hbuf_ref[s, nb:nb + rows, :].astype(bf16), cblk_ref[s], preferred_element_type=f32)
            g_ref[s] = _gelu(ys + dsk_ref[:, sl] * ut_ref[s])

    g = jnp.concatenate([g_ref[s] for s in range(S5_SLABS)], axis=-1)
    gate = jnp.dot(g.astype(bf16), wglu_ref[...], preferred_element_type=f32) + bglu_ref[...]
    out = g * jax.nn.sigmoid(gate)
    if batch_major:
        for s in range(S5_SLABS):
            sl = slice(s * LANES, (s + 1) * LANES)
            ut_ref[s] = out[:, sl]
            for b in range(nb):
                y_ref[b, :, sl] = ut_ref[s, pl.ds(b, tl, stride=nb), :]
    else:
        y_ref[...] = out.reshape(tl, nb, D_S5)

    @pl.when(step == pl.num_programs(0) - 1)
    def _():
        for s in range(S5_SLABS):
            ren_ref[:, s * ns:(s + 1) * ns] = hst_ref[s, :, :ns]
            imn_ref[:, s * ns:(s + 1) * ns] = hst_ref[s, :, ns:]


def _s5(u, re0, im0, ar, ai, bblk, cblk, dsk, wglu, bglu, *, tl, batch_major, cast=()):
    if batch_major:
        nb, L, _ = u.shape
        u_spec = pl.BlockSpec((nb, tl, D_S5), lambda i: (0, i, 0))
    else:
        L, nb, _ = u.shape
        u_spec = pl.BlockSpec((tl, nb, D_S5), lambda i: (i, 0, 0))
    rows = nb * tl
    steps = L // tl
    nstate = S5_GROUPS * S5_STATE
    st_spec = _const_spec((nb, nstate))
    cast_specs = [pl.BlockSpec((w.shape[0] // steps, w.shape[1]), lambda i: (i, 0)) for w in cast]
    assert all(w.shape[0] % (steps * 2 * SUBLANES) == 0 for w in cast)
    return pl.pallas_call(
        functools.partial(_s5_kernel, nb=nb, tl=tl, batch_major=batch_major, n_cast=len(cast)),
        grid=(steps,),
        in_specs=[u_spec, st_spec, st_spec,
                  _const_spec((S5_SLABS, SUBLANES, S5_SLAB_STATE)),
                  _const_spec((S5_SLABS, SUBLANES, S5_SLAB_STATE)),
                  _const_spec((S5_SLABS, LANES, 2 * S5_SLAB_STATE)),
                  _const_spec((S5_SLABS, 2 * S5_SLAB_STATE, LANES)),
                  _const_spec((1, D_S5)), _const_spec((D_S5, D_S5)), _const_spec((1, D_S5))] + cast_specs,
        out_specs=[u_spec, st_spec, st_spec] + cast_specs,
        out_shape=[jax.ShapeDtypeStruct(u.shape, f32),
                   jax.ShapeDtypeStruct((nb, nstate), f32),
                   jax.ShapeDtypeStruct((nb, nstate), f32)]
                  + [jax.ShapeDtypeStruct(w.shape, bf16) for w in cast],
        scratch_shapes=[pltpu.VMEM((S5_SLABS, rows, LANES), f32),
                        pltpu.VMEM((S5_SLABS, nb + rows, 2 * S5_SLAB_STATE), f32),
                        pltpu.VMEM((S5_SLABS, nb, 2 * S5_SLAB_STATE), f32),
                        pltpu.VMEM((S5_SLABS, rows, LANES), f32)],
        compiler_params=_params("arbitrary"),
        name="s5_mixer",
    )(u, re0, im0, ar, ai, bblk, cblk, dsk, wglu, bglu, *cast)


def _ffn_stages(x, ys, y5, mods, n2g, nfg, wo_ref, wg_ref, wu_ref, wd_ref, write_out, ff_chunk):
    g1, sh2, sc2, g2, shf, scf = mods
    att = jnp.dot(ys.astype(bf16), wo_ref[:D_SSD, :], preferred_element_type=f32)
    att = att + jnp.dot(y5.astype(bf16), wo_ref[D_SSD:, :], preferred_element_type=f32)
    yield
    x1 = x + _by_row(lambda v, s: v * s, att, g1)
    v = _rms_mod(x1, n2g, sc2, sh2).astype(bf16)
    ff = None
    for o in range(0, D_FF, ff_chunk):
        gate = jnp.dot(v, wg_ref[:, o:o + ff_chunk], preferred_element_type=f32)
        up = jnp.dot(v, wu_ref[:, o:o + ff_chunk], preferred_element_type=f32)
        hid = (_silu(gate) * up).astype(bf16)
        part = jnp.dot(hid, wd_ref[o:o + ff_chunk, :], preferred_element_type=f32)
        ff = part if ff is None else ff + part
        yield
    x2 = x1 + _by_row(lambda v, s: v * s, ff, g2)
    write_out(_rms_mod(x2, nfg, scf, shf))


def _ffn_kernel(x_ref, ys_ref, y5_ref, mod_ref, modf_ref, n2g_ref, nfg_ref, wo_ref, wg_ref, wu_ref, wd_ref,
                o_ref, *, seq_major, ff_chunk, sub):
    mods = (_mod_row(mod_ref, 2, seq_major), _mod_row(mod_ref, 3, seq_major), _mod_row(mod_ref, 4, seq_major),
            _mod_row(mod_ref, 5, seq_major), _mod_row(modf_ref, 0, seq_major), _mod_row(modf_ref, 1, seq_major))
    x = _rows_in(x_ref, seq_major)
    rows = x.shape[0]
    tiles = {}

    def stages(r0):
        def write_out(y):
            if seq_major:
                tiles[r0] = y
            else:
                o_ref[0, r0:r0 + sub, :] = y
        return _ffn_stages(x[r0:r0 + sub], ys_ref[0, r0:r0 + sub, :], y5_ref[0, r0:r0 + sub, :], mods,
                           n2g_ref[...], nfg_ref[...], wo_ref, wg_ref, wu_ref, wd_ref, write_out, ff_chunk)

    _interleave(*[stages(r0) for r0 in range(0, rows, sub)])
    if seq_major:
        s, t, d = o_ref.shape
        y = jnp.concatenate([tiles[r0] for r0 in range(0, rows, sub)], axis=0)
        o_ref[...] = jnp.swapaxes(y.reshape(t, s, d), 0, 1)


def _ffn(x, ys, y5, mod, modf, n2g, nfg, wo, wg, wu, wd, *, tm, sub, seq_major, n_seq, first_seq, ff_chunk=256):
    if seq_major:
        x_spec = _const_spec(x.shape)
        nb, rows, d = 1, x.shape[0] * x.shape[1], x.shape[2]
        assert tm == rows
    else:
        nb, rows, d = x.shape
        x_spec = pl.BlockSpec((1, tm, d), lambda i, j: (i, j, 0))
    nt = rows // tm
    mod_spec = _mod_spec(mod, n_seq, first_seq)
    modf_spec = _mod_spec(modf, n_seq, first_seq)
    blk = lambda wd_: pl.BlockSpec((1, tm, wd_), lambda i, j: (i, j, 0))
    single = dict(pipeline_mode=pl.Buffered(1))
    wspec = lambda shape: pl.BlockSpec(shape, lambda i, j: (0, 0), **single)
    return pl.pallas_call(
        functools.partial(_ffn_kernel, seq_major=seq_major, ff_chunk=ff_chunk, sub=sub),
        grid=(nb, nt),
        in_specs=[x_spec, blk(D_SSD), blk(D_S5), mod_spec, modf_spec,
                  _const_spec((1, d)), _const_spec((1, d)),
                  wspec((d, d)), wspec((d, D_FF)), wspec((d, D_FF)), wspec((D_FF, d))],
        out_specs=x_spec,
        out_shape=jax.ShapeDtypeStruct(x.shape, f32),
        compiler_params=_params("parallel", "parallel"),
        name="out_ffn",
    )(x, ys, y5, mod, modf, n2g, nfg, wo, wg, wu, wd)


def kernel(x_prompt, x_sample, c_prompt, c_sample, state_ssd, state_conv, state_s5_re, state_s5_im, w_ada, b_ada, norm1_g, w_in, conv_w, conv_b, ssd_dt_bias, ssd_A_log, ssd_D, ssd_norm_g, s5_A_re, s5_A_im, s5_log_step, s5_B_re, s5_B_im, s5_C_re, s5_C_im, s5_D, w_glu, b_glu, w_out, norm2_g, w_ffn_gate, w_ffn_up, w_ffn_down, w_ada_f, b_ada_f, normf_g):
    assert w_ada.shape[0] == 1, "single-layer stack"
    bp, seq, d = x_prompt.shape
    bs, dseq, _ = x_sample.shape

    mod, modf = _ada_mod(c_sample, c_prompt, (w_ada[0], w_ada_f), (b_ada[0], b_ada_f))
    seqs_s = dict(n_seq=bs, first_seq=0)
    seqs_p = dict(n_seq=bp, first_seq=bs)

    w_in_p = w_in[0].T
    pad_h = lambda v: jnp.concatenate([v, jnp.zeros((DT_PAD - HEADS,), f32)]).reshape(1, DT_PAD)
    dtb = pad_h(ssd_dt_bias[0])
    alog = pad_h(ssd_A_log[0])
    dexp = jnp.repeat(ssd_D[0], HEAD_DIM).reshape(1, D_SSD)
    ng = ssd_norm_g[0].reshape(1, D_SSD)
    cw, cb = conv_w[0], conv_b[0].reshape(1, CONV_DIM)

    ar, ai, bblk, cblk = _s5_params(s5_A_re[0], s5_A_im[0], s5_log_step[0],
                                    s5_B_re[0].transpose(0, 2, 1), s5_B_im[0].transpose(0, 2, 1),
                                    s5_C_re[0], s5_C_im[0])
    dsk = s5_D[0].reshape(1, D_S5)
    wglu = w_glu[0].astype(bf16)
    bglu = b_glu[0].reshape(1, D_S5)
    n1g, n2g, nfg = norm1_g[0].reshape(1, d), norm2_g[0].reshape(1, d), normf_g.reshape(1, d)
    nstate = S5_GROUPS * S5_STATE

    z, act, u5, dt, conv_p = _inproj(x_prompt, mod, n1g, w_in_p, (cw, cb), tm=1024, sub=512, seq_major=False,
                                     **seqs_p)
    y_ssd, ssd_p = _ssd_prompt(act, z, dt, dtb, alog, dexp, ng)
    zeros_st = jnp.zeros((bp, nstate), f32)
    y_s5, re_p, im_p, wo, wg, wu, wd = _s5(
        u5, zeros_st, zeros_st, ar, ai, bblk, cblk, dsk, wglu, bglu, tl=128, batch_major=True,
        cast=(w_out[0], w_ffn_gate[0], w_ffn_up[0], w_ffn_down[0]))
    y_prompt = _ffn(x_prompt, y_ssd, y_s5, mod, modf, n2g, nfg, wo, wg, wu, wd, tm=1024, sub=512,
                    seq_major=False, **seqs_p)

    rows_s = dseq * bs
    steps = lambda a: a.reshape(dseq, bs, a.shape[-1])
    flat = lambda a: a.reshape(1, rows_s, a.shape[-1])
    z, xbc, u5, dt = _inproj(x_sample, mod, n1g, w_in_p, tm=rows_s, sub=rows_s // 2, seq_major=True, **seqs_s)
    y_ssd, ssd_s, conv_s = _ssd_sample(steps(xbc), steps(z), steps(dt), state_conv[0].transpose(1, 0, 2),
                                       state_ssd[0], cw, cb, dtb, alog, dexp, ng)
    y_s5, re_s, im_s = _s5(steps(u5), state_s5_re[0].reshape(bs, nstate), state_s5_im[0].reshape(bs, nstate),
                           ar, ai, bblk, cblk, dsk, wglu, bglu, tl=dseq, batch_major=False)
    y_sample = _ffn(x_sample, flat(y_ssd), flat(y_s5), mod, modf, n2g, nfg, wo, wg, wu, wd,
                    tm=rows_s, sub=rows_s, seq_major=True, **seqs_s)

    g5 = (S5_GROUPS, S5_STATE)
    return (y_prompt, y_sample,
            ssd_p[None], ssd_s.reshape((1,) + state_ssd.shape[1:]),
            conv_p[None], conv_s.transpose(1, 0, 2)[None],
            re_p.reshape((1, bp) + g5), re_s.reshape((1, bs) + g5),
            im_p.reshape((1, bp) + g5), im_s.reshape((1, bs) + g5))
```

```python
import functools

import jax
import jax.numpy as jnp
from jax import lax
from jax.experimental import pallas as pl
from jax.experimental.pallas import tpu as pltpu

f32 = jnp.float32
bf16 = jnp.bfloat16

D_MODEL = 1024
D_SSD = 512
HEAD_DIM = 64
HEADS = 8
GROUPS = 2
HEADS_PER_GROUP = HEADS // GROUPS
STATE = 128
CONV_W = 4
CONV_DIM = D_SSD + 2 * GROUPS * STATE
D_S5 = 512
S5_CH = 16
S5_GROUPS = 32
S5_STATE = 64
D_FF = 2816
N_ADA = 6
EPS = 1e-6

LANES = 128
SUBLANES = 8
SSD_CHUNK = 128
S5_SLABS = D_S5 // LANES
S5_SLAB_STATE = (S5_GROUPS // S5_SLABS) * S5_STATE
DT_PAD = LANES
VMEM_LIMIT = 56 * 1024 * 1024

NT_DIMS = (((1,), (1,)), ((), ()))
TN_DIMS = (((0,), (0,)), ((), ()))


def _silu(x):
    h = 0.5 * x
    return h + h * jnp.tanh(h)


def _interleave(*gens, lead=0):
    pending, live, rnd = list(gens), [], 0
    while pending or live:
        while pending and rnd >= lead * (len(gens) - len(pending)):
            live.append(pending.pop(0))
        for g in list(live):
            try:
                next(g)
            except StopIteration:
                live.remove(g)
        rnd += 1


def _by_row(fn, v, *ms):
    r, d = v.shape
    m_rows = ms[0].shape[0]
    if m_rows in (1, r):
        return fn(v, *ms)
    out = fn(v.reshape(r // m_rows, m_rows, d), *[m[None] for m in ms])
    return out.reshape(r, d)


def _rms_mod(x, g, sc, sh):
    y = x * lax.rsqrt(jnp.mean(x * x, axis=-1, keepdims=True) + EPS)
    return _by_row(lambda v, s, t: v * (1.0 + s) + t, y * g, sc, sh)


def _mod_row(mod_ref, i, per_row):
    return mod_ref[i] if per_row else mod_ref[i, pl.ds(pl.program_id(0), 1), :]


def _mod_spec(mod, n_seq, first_seq):
    assert first_seq % n_seq == 0
    return pl.BlockSpec((mod.shape[0], n_seq, mod.shape[2]), lambda *_: (0, first_seq // n_seq, 0))


def _const_spec(shape):
    nd = len(shape)
    return pl.BlockSpec(shape, lambda *_: (0,) * nd)


def _params(*sem):
    return pltpu.CompilerParams(dimension_semantics=sem, vmem_limit_bytes=VMEM_LIMIT)


ADA_K_BLOCK = 512


def _ada_kernel(ca_ref, cb_ref, *refs):
    nw = len(refs) // 3
    s = _silu(jnp.concatenate([ca_ref[...], cb_ref[...]], axis=0)).astype(bf16)
    for w_ref, b_ref, o_ref in zip(refs[:nw], refs[nw:2 * nw], refs[2 * nw:]):
        n, _, d = o_ref.shape

        @pl.when(pl.program_id(0) == 0)
        def _():
            for j in range(n):
                o_ref[j] = jnp.broadcast_to(b_ref[:, j * d:(j + 1) * d], o_ref.shape[1:])

        for j in range(n):
            o_ref[j] += jnp.dot(s, w_ref[:, j * d:(j + 1) * d].astype(bf16), preferred_element_type=f32)


def _ada_mod(ca, cb, ws, bs):
    k = ca.shape[1]
    m = ca.shape[0] + cb.shape[0]
    ns = [w.shape[1] // k for w in ws]
    return pl.pallas_call(
        _ada_kernel,
        grid=(k // ADA_K_BLOCK,),
        in_specs=[pl.BlockSpec((c.shape[0], ADA_K_BLOCK), lambda i: (0, i)) for c in (ca, cb)]
                 + [pl.BlockSpec((ADA_K_BLOCK, n * k), lambda i: (i, 0)) for n in ns]
                 + [_const_spec((1, n * k)) for n in ns],
        out_specs=[_const_spec((n, m, k)) for n in ns],
        out_shape=[jax.ShapeDtypeStruct((n, m, k), f32) for n in ns],
        compiler_params=_params("arbitrary"),
        name="ada_mod",
    )(ca, cb, *ws, *[b.reshape(1, n * k) for b, n in zip(bs, ns)])


def _rows_in(x_ref, seq_major):
    if not seq_major:
        return x_ref[0]
    s, t, d = x_ref.shape
    return jnp.swapaxes(x_ref[...], 0, 1).reshape(s * t, d)


def _conv_silu(ext, cw_ref, cb_ref):
    w = [cw_ref[k:k + 1, :] for k in range(CONV_W)]
    back1 = pltpu.roll(ext, 1, 0)
    older = pltpu.roll(ext * w[1] + back1 * w[0], 2, 0)
    conv = cb_ref[...] + ext[SUBLANES:, :] * w[3] + back1[SUBLANES:, :] * w[2] + older[SUBLANES:, :]
    return _silu(conv)


def _inproj_stages(x, sh, sc, g, outs, after=()):
    u = _rms_mod(x, g, sc, sh).astype(bf16)
    yield
    for write, w16 in outs:
        write(lax.dot_general(u, w16, NT_DIMS, preferred_element_type=f32))
        yield
    for stage in after:
        stage()
        yield


def _inproj_kernel(x_ref, mod_ref, g_ref, w_ref, *rest, seq_major, sub, conv):
    if conv:
        cw_ref, cb_ref, z_ref, act_ref, u5_ref, dt_ref, cn_ref, ext_ref = rest

        @pl.when(pl.program_id(1) == 0)
        def _():
            ext_ref[0:SUBLANES, :] = jnp.zeros((SUBLANES, CONV_DIM), f32)
    else:
        z_ref, xbc_ref, u5_ref, dt_ref = rest
    sh = _mod_row(mod_ref, 0, seq_major)
    sc = _mod_row(mod_ref, 1, seq_major)
    x = _rows_in(x_ref, seq_major)
    rows = x.shape[0]
    T = SSD_CHUNK
    o_dt = D_SSD + CONV_DIM
    w16 = lambda o, width: w_ref[o:o + width, :].astype(bf16)
    w_z, w_xbc, w_u5, w_dt = w16(0, D_SSD), w16(D_SSD, CONV_DIM), w16(o_dt + HEADS, D_S5), w16(o_dt, DT_PAD)

    def stages(r0):
        def to(ref):
            def write(v):
                ref[0, r0:r0 + sub, :] = v
            return write

        def to_ext(v):
            ext_ref[SUBLANES + r0:SUBLANES + r0 + sub, :] = v

        def conv_chunk(c0):
            def stage():
                act_ref[0, c0:c0 + T, :] = _conv_silu(ext_ref[c0:c0 + SUBLANES + T, :], cw_ref, cb_ref)
            return stage

        outs = [(to(z_ref), w_z), (to_ext if conv else to(xbc_ref), w_xbc), (to(u5_ref), w_u5), (to(dt_ref), w_dt)]
        after = [conv_chunk(c0) for c0 in range(r0, r0 + sub, T)] if conv else ()
        return _inproj_stages(x[r0:r0 + sub], sh, sc, g_ref[...], outs, after)

    n_matmul_stages = 5
    _interleave(*[stages(r0) for r0 in range(0, rows, sub)], lead=n_matmul_stages if conv else 0)
    if conv:
        cn_ref[0] = ext_ref[SUBLANES + rows - (CONV_W - 1):SUBLANES + rows, :]
        ext_ref[0:SUBLANES, :] = ext_ref[rows:rows + SUBLANES, :]


def _inproj(x, mod, g, w, conv_wb=None, *, tm, sub, seq_major, n_seq, first_seq):
    if seq_major:
        x_spec = _const_spec(x.shape)
        nb, rows, d = 1, x.shape[0] * x.shape[1], x.shape[2]
        assert tm == rows
    else:
        nb, rows, d = x.shape
        x_spec = pl.BlockSpec((1, tm, d), lambda i, j: (i, j, 0))
    nt = rows // tm
    widths = (D_SSD, CONV_DIM, D_S5, DT_PAD)
    conv = conv_wb is not None
    in_specs = [x_spec, _mod_spec(mod, n_seq, first_seq), _const_spec((1, d)),
                pl.BlockSpec(w.shape, lambda i, j: (0, 0), pipeline_mode=pl.Buffered(1))]
    out_specs = [pl.BlockSpec((1, tm, wd), lambda i, j: (i, j, 0)) for wd in widths]
    out_shape = [jax.ShapeDtypeStruct((nb, rows, wd), f32) for wd in widths]
    scratch = []
    if conv:
        in_specs += [_const_spec((CONV_W, CONV_DIM)), _const_spec((1, CONV_DIM))]
        out_specs.append(pl.BlockSpec((1, CONV_W - 1, CONV_DIM), lambda i, j: (i, 0, 0)))
        out_shape.append(jax.ShapeDtypeStruct((nb, CONV_W - 1, CONV_DIM), f32))
        scratch.append(pltpu.VMEM((SUBLANES + tm, CONV_DIM), f32))
    return pl.pallas_call(
        functools.partial(_inproj_kernel, seq_major=seq_major, sub=sub, conv=conv),
        grid=(nb, nt),
        in_specs=in_specs, out_specs=out_specs, out_shape=out_shape, scratch_shapes=scratch,
        compiler_params=_params("parallel", "arbitrary" if conv else "parallel"),
        name="in_proj",
    )(x, mod, g, w, *(conv_wb or ()))


def _split3(x):
    hi = x.astype(bf16)
    r1 = x - hi.astype(f32)
    mid = r1.astype(bf16)
    lo = (r1 - mid.astype(f32)).astype(bf16)
    return hi, mid, lo


def _dot_sel_lhs(sel16, x):
    return sum(jnp.dot(sel16, p, preferred_element_type=f32) for p in _split3(x))


def _dot_sel_rhs(x, sel16):
    hi, mid, _ = _split3(x)
    return jnp.dot(hi, sel16, preferred_element_type=f32) + jnp.dot(mid, sel16, preferred_element_type=f32)


def _gated_group_norm(y, z, ng):
    y = y * _silu(z)
    gw = D_SSD // GROUPS
    parts = []
    for g in range(GROUPS):
        yg = y[:, g * gw:(g + 1) * gw]
        parts.append(yg * lax.rsqrt(jnp.mean(yg * yg, axis=-1, keepdims=True) + EPS))
    return jnp.concatenate(parts, axis=-1) * ng


def _ssd_prompt_kernel(act_ref, z_ref, dt_ref, dtb_ref, alog_ref, dexp_ref, ng_ref, y_ref, st_ref, h_ref, *, cps):
    T = SSD_CHUNK
    gw = HEADS_PER_GROUP * HEAD_DIM

    @pl.when(pl.program_id(1) == 0)
    def _():
        h_ref[...] = jnp.zeros_like(h_ref)

    row = lax.broadcasted_iota(jnp.int32, (T, T), 0)
    col = lax.broadcasted_iota(jnp.int32, (T, T), 1)
    tri = row >= col
    tri16 = jnp.where(tri, 1.0, 0.0).astype(bf16)
    low_half = lax.broadcasted_iota(jnp.int32, (T, LANES), 1) < HEAD_DIM
    a_neg = -jnp.exp(alog_ref[...])

    n_fac = 3
    src = lax.broadcasted_iota(jnp.int32, (2 * DT_PAD, n_fac * D_SSD), 0) % DT_PAD
    dst = lax.broadcasted_iota(jnp.int32, (2 * DT_PAD, n_fac * D_SSD), 1)
    sel16 = jnp.where(src == dst // D_SSD * HEADS + dst % D_SSD // HEAD_DIM, 1.0, 0.0).astype(bf16)
    lane = lax.broadcasted_iota(jnp.int32, (T, DT_PAD), 1)

    def spread(q):
        hi = q.astype(bf16)
        mid = (q - hi.astype(f32)).astype(bf16)
        return jnp.dot(jnp.concatenate([hi, mid], axis=1), sel16, preferred_element_type=f32)

    def state_free(ci):
        r0 = ci * T
        act = act_ref[0, r0:r0 + T, :]
        xs = act[:, :D_SSD]
        bm = act[:, D_SSD:D_SSD + GROUPS * STATE]
        cm = act[:, D_SSD + GROUPS * STATE:]
        dtv = jax.nn.softplus(dt_ref[0, r0:r0 + T, :] + dtb_ref[...])
        a_cs = _dot_sel_lhs(tri16, dtv * a_neg)
        bg16s = [bm[:, g * STATE:(g + 1) * STATE].astype(bf16) for g in range(GROUPS)]
        cg16s = [cm[:, g * STATE:(g + 1) * STATE].astype(bf16) for g in range(GROUPS)]
        gmats = [lax.dot_general(cg16s[g], bg16s[g], NT_DIMS, preferred_element_type=f32)
                 for g in range(GROUPS)]
        yield
        a_last = a_cs[T - 1:T, :]
        a_cs_t = a_cs.T
        a_one = pltpu.roll(a_cs, HEADS, 1)
        a_two = pltpu.roll(a_cs, 2 * HEADS, 1)
        decay = jnp.exp(jnp.where(lane < 2 * HEADS, a_one[T - 1:T, :] - a_one, a_two))
        rep = spread(jnp.where(lane < HEADS, dtv, jnp.where(lane < n_fac * HEADS, decay, 0.0)))
        yield
        x = xs * rep[:, :D_SSD]
        x16 = x.astype(bf16)
        xd16 = (x * rep[:, D_SSD:2 * D_SSD]).astype(bf16)
        e_cs = rep[:, 2 * D_SSD:]
        yield
        y_part = []
        for g in range(GROUPS):
            for pr in range(HEADS_PER_GROUP // 2):
                s_pair = []
                for q in range(2):
                    h = g * HEADS_PER_GROUP + 2 * pr + q
                    seg = a_cs[:, h:h + 1] - a_cs_t[h:h + 1, :]
                    lmat = jnp.exp(jnp.where(tri, seg, -jnp.inf))
                    s_pair.append((gmats[g] * lmat).astype(bf16))
                xp = x16[:, g * gw + pr * LANES:g * gw + (pr + 1) * LANES]
                zero = jnp.zeros_like(xp)
                x_pair = jnp.concatenate([jnp.where(low_half, xp, zero), jnp.where(low_half, zero, xp)], axis=0)
                y_part.append(jnp.dot(jnp.concatenate(s_pair, axis=1), x_pair, preferred_element_type=f32))
                yield
        y_free = jnp.concatenate(y_part, axis=-1) + xs * dexp_ref[...]
        return y_free, e_cs, jnp.exp(a_last), xd16, bg16s, cg16s

    def state_step(ci, y_free, e_cs, e_last, xd16, bg16s, cg16s):
        r0 = ci * T
        y_offs = []
        for g in range(GROUPS):
            hp = h_ref[g]
            y_offs.append(lax.dot_general(cg16s[g], hp.astype(bf16), NT_DIMS, preferred_element_type=f32))
            upd = lax.dot_general(xd16[:, g * gw:(g + 1) * gw], bg16s[g], TN_DIMS, preferred_element_type=f32)
            for hh in range(HEADS_PER_GROUP):
                h = g * HEADS_PER_GROUP + hh
                rows = slice(hh * HEAD_DIM, (hh + 1) * HEAD_DIM)
                h_ref[g, rows, :] = e_last[:, h:h + 1] * hp[rows] + upd[rows]
        y = y_free + jnp.concatenate(y_offs, axis=-1) * e_cs
        y_ref[0, r0:r0 + T, :] = _gated_group_norm(y, z_ref[0, r0:r0 + T, :], ng_ref[...])

    gens = [state_free(ci) for ci in range(cps)]
    free = [None] * cps
    while any(f is None for f in free):
        for ci in range(cps):
            if free[ci] is None:
                try:
                    next(gens[ci])
                except StopIteration as done:
                    free[ci] = done.value
    for ci in range(cps):
        state_step(ci, *free[ci])

    @pl.when(pl.program_id(1) == pl.num_programs(1) - 1)
    def _():
        st_ref[0] = h_ref[...].reshape(HEADS, HEAD_DIM, STATE)


def _ssd_prompt(act, z, dt, dtb, alog, dexp, ng, *, cps=8):
    nb, L, _ = act.shape
    rows = cps * SSD_CHUNK
    blk = lambda wd: pl.BlockSpec((1, rows, wd), lambda b, c: (b, c, 0))
    return pl.pallas_call(
        functools.partial(_ssd_prompt_kernel, cps=cps),
        grid=(nb, L // rows),
        in_specs=[blk(CONV_DIM), blk(D_SSD), blk(DT_PAD), _const_spec((1, DT_PAD)), _const_spec((1, DT_PAD)),
                  _const_spec((1, D_SSD)), _const_spec((1, D_SSD))],
        out_specs=[blk(D_SSD), pl.BlockSpec((1, HEADS, HEAD_DIM, STATE), lambda b, c: (b, 0, 0, 0))],
        out_shape=[jax.ShapeDtypeStruct((nb, L, D_SSD), f32),
                   jax.ShapeDtypeStruct((nb, HEADS, HEAD_DIM, STATE), f32)],
        scratch_shapes=[pltpu.VMEM((GROUPS, HEADS_PER_GROUP * HEAD_DIM, STATE), f32)],
        compiler_params=_params("parallel", "arbitrary"),
        name="ssd_prompt",
    )(act, z, dt, dtb, alog, dexp, ng)


def _ssd_sample_kernel(xbc_ref, z_ref, dt_ref, cbuf_ref, st_ref, cw_ref, cb_ref, dtb_ref, alog_ref,
                       dexp_ref, ng_ref, y_ref, stn_ref, cn_ref, dtot_ref, *, L, nb):
    gw = HEADS_PER_GROUP * HEAD_DIM
    full = [cbuf_ref[j] for j in range(CONV_W - 1)] + [xbc_ref[t] for t in range(L)]
    for j in range(CONV_W - 1):
        cn_ref[j] = full[L + j]
    a_neg = -jnp.exp(alog_ref[...])
    xs, bm, cm, dtv, acs = [], [], [], [], []
    run = None
    for t in range(L):
        conv = cb_ref[...]
        for k in range(CONV_W):
            conv = conv + full[t + k] * cw_ref[k:k + 1, :]
        act = _silu(conv)
        xs.append(act[:, :D_SSD])
        bm.append(act[:, D_SSD:D_SSD + GROUPS * STATE])
        cm.append(act[:, D_SSD + GROUPS * STATE:])
        d = jax.nn.softplus(dt_ref[t] + dtb_ref[...])
        dtv.append(d)
        run = d * a_neg if run is None else run + d * a_neg
        acs.append(run)
    a_tot = acs[L - 1]
    dtot_ref[...] = jnp.exp(a_tot)

    lane_head = lax.broadcasted_iota(jnp.int32, (DT_PAD, D_SSD), 1) // HEAD_DIM
    sel16 = jnp.where(lax.broadcasted_iota(jnp.int32, (DT_PAD, D_SSD), 0) == lane_head, 1.0, 0.0).astype(bf16)
    pairs = [(t, s) for t in range(L) for s in range(t)]
    factors = (dtv + [jnp.exp(a_tot - acs[t]) for t in range(L)] + [jnp.exp(acs[t]) for t in range(L)]
               + [jnp.exp(acs[t] - acs[s]) for t, s in pairs])
    rep = _dot_sel_rhs(jnp.concatenate(factors, axis=0), sel16)
    piece = lambda i: rep[i * nb:(i + 1) * nb]
    x = [xs[t] * piece(t) for t in range(L)]
    xd_stack = jnp.concatenate([x[t] * piece(L + t) for t in range(L)], axis=0)
    e_cs = [piece(2 * L + t) for t in range(L)]
    decay = {ts: piece(3 * L + i) for i, ts in enumerate(pairs)}

    in_group0 = lax.broadcasted_iota(jnp.int32, (nb, D_SSD), 1) < gw
    y_intra = []
    for t in range(L):
        acc = None
        for s in range(t + 1):
            cb_dot = [jnp.sum(cm[t][:, g * STATE:(g + 1) * STATE] * bm[s][:, g * STATE:(g + 1) * STATE],
                              axis=-1, keepdims=True) for g in range(GROUPS)]
            w = jnp.where(in_group0, cb_dot[0], cb_dot[1])
            term = w * x[s] if s == t else w * decay[(t, s)] * x[s]
            acc = term if acc is None else acc + term
        y_intra.append(acc)

    c_stack = [jnp.concatenate([cm[t][:, g * STATE:(g + 1) * STATE] for t in range(L)], axis=0).astype(bf16)
               for g in range(GROUPS)]
    b_stack = [jnp.concatenate([bm[t][:, g * STATE:(g + 1) * STATE] for t in range(L)], axis=0).astype(bf16)
               for g in range(GROUPS)]
    seq_of_row = lax.broadcasted_iota(jnp.int32, (L * nb, gw), 0) & (nb - 1)

    def per_seq(b, yoff):
        mine = seq_of_row == b
        drow = dtot_ref[pl.ds(b, 1), :]
        out = []
        for g in range(GROUPS):
            h0 = st_ref[b, g]
            r = lax.dot_general(c_stack[g], h0.astype(bf16), NT_DIMS, preferred_element_type=f32)
            out.append(yoff[g] + jnp.where(mine, r, 0.0))
            xm = jnp.where(mine, xd_stack[:, g * gw:(g + 1) * gw], 0.0).astype(bf16)
            upd = lax.dot_general(xm, b_stack[g], TN_DIMS, preferred_element_type=f32)
            for hh in range(HEADS_PER_GROUP):
                h = g * HEADS_PER_GROUP + hh
                rows = slice(hh * HEAD_DIM, (hh + 1) * HEAD_DIM)
                stn_ref[b, g, rows, :] = drow[:, h:h + 1] * h0[rows] + upd[rows]
        return tuple(out)

    yoff = lax.fori_loop(0, nb, per_seq, tuple(jnp.zeros((L * nb, gw), f32) for _ in range(GROUPS)),
                         unroll=True)

    for t in range(L):
        y_off = jnp.concatenate([yoff[g][t * nb:(t + 1) * nb] for g in range(GROUPS)], axis=-1)
        y = y_intra[t] + y_off * e_cs[t] + xs[t] * dexp_ref[...]
        y_ref[t] = _gated_group_norm(y, z_ref[t], ng_ref[...])


def _ssd_sample(xbc, z, dt, cbuf, st, cw, cb, dtb, alog, dexp, ng, *, nb=16):
    L, B, _ = xbc.shape
    assert nb & (nb - 1) == 0 and B % nb == 0
    tblk = lambda n, wd: pl.BlockSpec((n, nb, wd), lambda i: (0, i, 0))
    gw = HEADS_PER_GROUP * HEAD_DIM
    st = st.reshape(B, GROUPS, gw, STATE)
    st_spec = pl.BlockSpec((nb, GROUPS, gw, STATE), lambda i: (i, 0, 0, 0))
    return pl.pallas_call(
        functools.partial(_ssd_sample_kernel, L=L, nb=nb),
        grid=(B // nb,),
        in_specs=[tblk(L, CONV_DIM), tblk(L, D_SSD), tblk(L, DT_PAD), tblk(CONV_W - 1, CONV_DIM), st_spec,
                  _const_spec((CONV_W, CONV_DIM)), _const_spec((1, CONV_DIM)),
                  _const_spec((1, DT_PAD)), _const_spec((1, DT_PAD)),
                  _const_spec((1, D_SSD)), _const_spec((1, D_SSD))],
        out_specs=[tblk(L, D_SSD), st_spec, tblk(CONV_W - 1, CONV_DIM)],
        out_shape=[jax.ShapeDtypeStruct((L, B, D_SSD), f32),
                   jax.ShapeDtypeStruct((B, GROUPS, gw, STATE), f32),
                   jax.ShapeDtypeStruct((CONV_W - 1, B, CONV_DIM), f32)],
        scratch_shapes=[pltpu.VMEM((nb, DT_PAD), f32)],
        compiler_params=_params("parallel"),
        name="ssd_sample",
    )(xbc, z, dt, cbuf, st, cw, cb, dtb, alog, dexp, ng)


def _s5_param_kernel(lr_ref, li_ref, ls_ref, br_ref, bi_ref, cr_ref, ci_ref,
                     ar_ref, ai_ref, bblk_ref, cblk_ref, b_scr, c_scr):
    lr, li = lr_ref[...], li_ref[...]
    step = jnp.exp(ls_ref[...])
    mag = jnp.exp(lr * step)
    abr = mag * jnp.cos(li * step)
    abi = mag * jnp.sin(li * step)
    nr, ni = abr - 1.0, abi
    den = lr * lr + li * li
    fr = (nr * lr + ni * li) / den
    fi = (ni * lr - nr * li) / den
    br, bi = br_ref[...], bi_ref[...]
    bbr = fr * br - fi * bi
    bbi = fr * bi + fi * br
    b_scr[...] = jnp.zeros_like(b_scr)
    c_scr[...] = jnp.zeros_like(c_scr)
    gps = S5_GROUPS // S5_SLABS
    ns = S5_SLAB_STATE
    for g in range(S5_GROUPS):
        s, gl = divmod(g, gps)
        ch = slice(gl * S5_CH, (gl + 1) * S5_CH)
        st = slice(gl * S5_STATE, (gl + 1) * S5_STATE)
        st_im = slice(ns + gl * S5_STATE, ns + (gl + 1) * S5_STATE)
        ar_ref[s, :, st] = jnp.broadcast_to(abr[g], (SUBLANES, S5_STATE))
        ai_ref[s, :, st] = jnp.broadcast_to(abi[g], (SUBLANES, S5_STATE))
        b_scr[s, ch, st] = bbr[g]
        b_scr[s, ch, st_im] = bbi[g]
        c_scr[s, st, ch] = cr_ref[g].T
        c_scr[s, st_im, ch] = -ci_ref[g].T
    bblk_ref[...] = b_scr[...].astype(bf16)
    cblk_ref[...] = c_scr[...].astype(bf16)


def _s5_params(a_re, a_im, log_step, b_re_t, b_im_t, c_re, c_im):
    g, p = a_re.shape
    ns = S5_SLAB_STATE
    return pl.pallas_call(
        _s5_param_kernel,
        out_shape=[jax.ShapeDtypeStruct((S5_SLABS, SUBLANES, ns), f32),
                   jax.ShapeDtypeStruct((S5_SLABS, SUBLANES, ns), f32),
                   jax.ShapeDtypeStruct((S5_SLABS, LANES, 2 * ns), bf16),
                   jax.ShapeDtypeStruct((S5_SLABS, 2 * ns, LANES), bf16)],
        scratch_shapes=[pltpu.VMEM((S5_SLABS, LANES, 2 * ns), f32),
                        pltpu.VMEM((S5_SLABS, 2 * ns, LANES), f32)],
        name="s5_params",
    )(a_re.reshape(g, 1, p), a_im.reshape(g, 1, p), log_step.reshape(g, 1, 1), b_re_t, b_im_t, c_re, c_im)


def _gelu(x):
    return 0.5 * x * (1.0 + lax.erf(x * (2.0 ** -0.5)))


def _s5_slabs_interleaved(ut_ref, hbuf_ref, hst_ref, g_ref, ar_ref, ai_ref, bblk_ref, cblk_ref, dsk_ref, tl,
                          row_block=256, n_stages=8):
    ns = S5_SLAB_STATE
    nb = SUBLANES
    rows = nb * tl

    def lanes(s):
        return slice(s * LANES, (s + 1) * LANES)

    def bu_stage(slabs):
        for s in slabs:
            for r in range(0, rows, row_block):
                us = ut_ref[s, r:r + row_block, :]
                hbuf_ref[s, nb + r:nb + r + row_block, :] = jnp.dot(us.astype(bf16), bblk_ref[s],
                                                                    preferred_element_type=f32)
                yield

    def scan_stage(slabs):
        carry = [(hst_ref[s, :, :ns], hst_ref[s, :, ns:]) for s in slabs]
        coef = [(ar_ref[s], ai_ref[s]) for s in slabs]
        for t in range(tl):
            r = nb + nb * t
            for i, s in enumerate(slabs):
                (pr, pi), (ar, ai) = carry[i], coef[i]
                nr = ar * pr - ai * pi + hbuf_ref[s, r:r + nb, :ns]
                ni = ar * pi + ai * pr + hbuf_ref[s, r:r + nb, ns:]
                hbuf_ref[s, r:r + nb, :ns] = nr
                hbuf_ref[s, r:r + nb, ns:] = ni
                carry[i] = (nr, ni)
            if (t + 1) % (tl // n_stages) == 0:
                yield
        for i, s in enumerate(slabs):
            hst_ref[s, :, :ns] = carry[i][0]
            hst_ref[s, :, ns:] = carry[i][1]

    def y_stage(slabs):
        for s in slabs:
            for r in range(0, rows, row_block):
                h16 = hbuf_ref[s, nb + r:nb + r + row_block, :].astype(bf16)
                ys = jnp.dot(h16, cblk_ref[s], preferred_element_type=f32)
                ys = ys + dsk_ref[:, lanes(s)] * ut_ref[s, r:r + row_block, :]
                g_ref[s, r:r + row_block, :] = _gelu(ys)
                yield

    half = S5_SLABS // 2
    first, second = tuple(range(half)), tuple(range(half, S5_SLABS))
    _interleave(bu_stage(first))
    _interleave(scan_stage(first), bu_stage(second))
    _interleave(scan_stage(second), y_stage(first))
    _interleave(y_stage(second))


def _s5_scan_vmem(hbuf_ref, hst_ref, ar_ref, ai_ref, s, nb, rows):
    ns = S5_SLAB_STATE
    hbuf_ref[s, 0:nb, :] = hst_ref[s]
    ar, ai = ar_ref[s], ai_ref[s]

    def body(i, carry):
        j = pl.multiple_of(nb + i * SUBLANES, SUBLANES)
        prev = hbuf_ref[s, pl.ds(j - nb, SUBLANES), :]
        cur = hbuf_ref[s, pl.ds(j, SUBLANES), :]
        pr, pi = prev[:, :ns], prev[:, ns:]
        hbuf_ref[s, pl.ds(j, SUBLANES), :ns] = ar * pr - ai * pi + cur[:, :ns]
        hbuf_ref[s, pl.ds(j, SUBLANES), ns:] = ar * pi + ai * pr + cur[:, ns:]
        return carry

    lax.fori_loop(0, rows // SUBLANES, body, 0, unroll=True)
    hst_ref[s] = hbuf_ref[s, rows:rows + nb, :]


S5_N_IN = 10


def _s5_kernel(*refs, nb, tl, batch_major, n_cast):
    (u_ref, re0_ref, im0_ref, ar_ref, ai_ref, bblk_ref, cblk_ref, dsk_ref, wglu_ref,
     bglu_ref) = refs[:S5_N_IN]
    cast_in = refs[S5_N_IN:S5_N_IN + n_cast]
    y_ref, ren_ref, imn_ref = refs[S5_N_IN + n_cast:S5_N_IN + n_cast + 3]
    cast_out = refs[S5_N_IN + n_cast + 3:S5_N_IN + 2 * n_cast + 3]
    ut_ref, hbuf_ref, hst_ref, g_ref = refs[S5_N_IN + 2 * n_cast + 3:]
    for w_ref, w16_ref in zip(cast_in, cast_out):
        w16_ref[...] = w_ref[...].astype(bf16)
    step = pl.program_id(0)
    rows = nb * tl
    ns = S5_SLAB_STATE

    @pl.when(step == 0)
    def _():
        for s in range(S5_SLABS):
            hst_ref[s, :, :ns] = re0_ref[:, s * ns:(s + 1) * ns]
            hst_ref[s, :, ns:] = im0_ref[:, s * ns:(s + 1) * ns]

    for s in range(S5_SLABS):
        sl = slice(s * LANES, (s + 1) * LANES)
        if batch_major:
            for b in range(nb):
                ut_ref[s, pl.ds(b, tl, stride=nb), :] = u_ref[b, :, sl]
        else:
            ut_ref[s] = u_ref[:, :, sl].reshape(rows, LANES)

    if nb == SUBLANES:
        _s5_slabs_interleaved(ut_ref, hbuf_ref, hst_ref, g_ref, ar_ref, ai_ref, bblk_ref, cblk_ref, dsk_ref, tl)
    else:
        for s in range(S5_SLABS):
            sl = slice(s * LANES, (s + 1) * LANES)
            hbuf_ref[s, nb:nb + rows, :] = jnp.dot(ut_ref[s].astype(bf16), bblk_ref[s],
                                                   preferred_element_type=f32)
            _s5_scan_vmem(hbuf_ref, hst_ref, ar_ref, ai_ref, s, nb, rows)
            ys = jnp.dot(hbuf_ref[s, nb:nb + rows, :].astype(bf16), cblk_ref[s], preferred_element_type=f32)
            g_ref[s] = _gelu(ys + dsk_ref[:, sl] * ut_ref[s])

    g = jnp.concatenate([g_ref[s] for s in range(S5_SLABS)], axis=-1)
    gate = jnp.dot(g.astype(bf16), wglu_ref[...], preferred_element_type=f32) + bglu_ref[...]
    out = g * jax.nn.sigmoid(gate)
    if batch_major:
        for s in range(S5_SLABS):
            sl = slice(s * LANES, (s + 1) * LANES)
            ut_ref[s] = out[:, sl]
            for b in range(nb):
                y_ref[b, :, sl] = ut_ref[s, pl.ds(b, tl, stride=nb), :]
    else:
        y_ref[...] = out.reshape(tl, nb, D_S5)

    @pl.when(step == pl.num_programs(0) - 1)
    def _():
        for s in range(S5_SLABS):
            ren_ref[:, s * ns:(s + 1) * ns] = hst_ref[s, :, :ns]
            imn_ref[:, s * ns:(s + 1) * ns] = hst_ref[s, :, ns:]


def _s5(u, re0, im0, ar, ai, bblk, cblk, dsk, wglu, bglu, *, tl, batch_major, cast=()):
    if batch_major:
        nb, L, _ = u.shape
        u_spec = pl.BlockSpec((nb, tl, D_S5), lambda i: (0, i, 0))
    else:
        L, nb, _ = u.shape
        u_spec = pl.BlockSpec((tl, nb, D_S5), lambda i: (i, 0, 0))
    rows = nb * tl
    steps = L // tl
    nstate = S5_GROUPS * S5_STATE
    st_spec = _const_spec((nb, nstate))
    cast_specs = [pl.BlockSpec((w.shape[0] // steps, w.shape[1]), lambda i: (i, 0)) for w in cast]
    assert all(w.shape[0] % (steps * 2 * SUBLANES) == 0 for w in cast)
    return pl.pallas_call(
        functools.partial(_s5_kernel, nb=nb, tl=tl, batch_major=batch_major, n_cast=len(cast)),
        grid=(steps,),
        in_specs=[u_spec, st_spec, st_spec,
                  _const_spec((S5_SLABS, SUBLANES, S5_SLAB_STATE)),
                  _const_spec((S5_SLABS, SUBLANES, S5_SLAB_STATE)),
                  _const_spec((S5_SLABS, LANES, 2 * S5_SLAB_STATE)),
                  _const_spec((S5_SLABS, 2 * S5_SLAB_STATE, LANES)),
                  _const_spec((1, D_S5)), _const_spec((D_S5, D_S5)), _const_spec((1, D_S5))] + cast_specs,
        out_specs=[u_spec, st_spec, st_spec] + cast_specs,
        out_shape=[jax.ShapeDtypeStruct(u.shape, f32),
                   jax.ShapeDtypeStruct((nb, nstate), f32),
                   jax.ShapeDtypeStruct((nb, nstate), f32)]
                  + [jax.ShapeDtypeStruct(w.shape, bf16) for w in cast],
        scratch_shapes=[pltpu.VMEM((S5_SLABS, rows, LANES), f32),
                        pltpu.VMEM((S5_SLABS, nb + rows, 2 * S5_SLAB_STATE), f32),
                        pltpu.VMEM((S5_SLABS, nb, 2 * S5_SLAB_STATE), f32),
                        pltpu.VMEM((S5_SLABS, rows, LANES), f32)],
        compiler_params=_params("arbitrary"),
        name="s5_mixer",
    )(u, re0, im0, ar, ai, bblk, cblk, dsk, wglu, bglu, *cast)


def _ffn_stages(x, ys, y5, mods, n2g, nfg, wo_ref, wg_ref, wu_ref, wd_ref, write_out, ff_chunk):
    g1, sh2, sc2, g2, shf, scf = mods
    att = jnp.dot(ys.astype(bf16), wo_ref[:D_SSD, :], preferred_element_type=f32)
    att = att + jnp.dot(y5.astype(bf16), wo_ref[D_SSD:, :], preferred_element_type=f32)
    yield
    x1 = x + _by_row(lambda v, s: v * s, att, g1)
    v = _rms_mod(x1, n2g, sc2, sh2).astype(bf16)
    ff = None
    for o in range(0, D_FF, ff_chunk):
        gate = jnp.dot(v, wg_ref[:, o:o + ff_chunk], preferred_element_type=f32)
        up = jnp.dot(v, wu_ref[:, o:o + ff_chunk], preferred_element_type=f32)
        hid = (_silu(gate) * up).astype(bf16)
        part = jnp.dot(hid, wd_ref[o:o + ff_chunk, :], preferred_element_type=f32)
        ff = part if ff is None else ff + part
        yield
    x2 = x1 + _by_row(lambda v, s: v * s, ff, g2)
    write_out(_rms_mod(x2, nfg, scf, shf))


def _ffn_kernel(x_ref, ys_ref, y5_ref, mod_ref, modf_ref, n2g_ref, nfg_ref, wo_ref, wg_ref, wu_ref, wd_ref,
                o_ref, *, seq_major, ff_chunk, sub):
    mods = (_mod_row(mod_ref, 2, seq_major), _mod_row(mod_ref, 3, seq_major), _mod_row(mod_ref, 4, seq_major),
            _mod_row(mod_ref, 5, seq_major), _mod_row(modf_ref, 0, seq_major), _mod_row(modf_ref, 1, seq_major))
    x = _rows_in(x_ref, seq_major)
    rows = x.shape[0]
    tiles = {}

    def stages(r0):
        def write_out(y):
            if seq_major:
                tiles[r0] = y
            else:
                o_ref[0, r0:r0 + sub, :] = y
        return _ffn_stages(x[r0:r0 + sub], ys_ref[0, r0:r0 + sub, :], y5_ref[0, r0:r0 + sub, :], mods,
                           n2g_ref[...], nfg_ref[...], wo_ref, wg_ref, wu_ref, wd_ref, write_out, ff_chunk)

    _interleave(*[stages(r0) for r0 in range(0, rows, sub)])
    if seq_major:
        s, t, d = o_ref.shape
        y = jnp.concatenate([tiles[r0] for r0 in range(0, rows, sub)], axis=0)
        o_ref[...] = jnp.swapaxes(y.reshape(t, s, d), 0, 1)


def _ffn(x, ys, y5, mod, modf, n2g, nfg, wo, wg, wu, wd, *, tm, sub, seq_major, n_seq, first_seq, ff_chunk=256):
    if seq_major:
        x_spec = _const_spec(x.shape)
        nb, rows, d = 1, x.shape[0] * x.shape[1], x.shape[2]
        assert tm == rows
    else:
        nb, rows, d = x.shape
        x_spec = pl.BlockSpec((1, tm, d), lambda i, j: (i, j, 0))
    nt = rows // tm
    mod_spec = _mod_spec(mod, n_seq, first_seq)
    modf_spec = _mod_spec(modf, n_seq, first_seq)
    blk = lambda wd_: pl.BlockSpec((1, tm, wd_), lambda i, j: (i, j, 0))
    single = dict(pipeline_mode=pl.Buffered(1))
    wspec = lambda shape: pl.BlockSpec(shape, lambda i, j: (0, 0), **single)
    return pl.pallas_call(
        functools.partial(_ffn_kernel, seq_major=seq_major, ff_chunk=ff_chunk, sub=sub),
        grid=(nb, nt),
        in_specs=[x_spec, blk(D_SSD), blk(D_S5), mod_spec, modf_spec,
                  _const_spec((1, d)), _const_spec((1, d)),
                  wspec((d, d)), wspec((d, D_FF)), wspec((d, D_FF)), wspec((D_FF, d))],
        out_specs=x_spec,
        out_shape=jax.ShapeDtypeStruct(x.shape, f32),
        compiler_params=_params("parallel", "parallel"),
        name="out_ffn",
    )(x, ys, y5, mod, modf, n2g, nfg, wo, wg, wu, wd)


def kernel(x_prompt, x_sample, c_prompt, c_sample, state_ssd, state_conv, state_s5_re, state_s5_im, w_ada, b_ada, norm1_g, w_in, conv_w, conv_b, ssd_dt_bias, ssd_A_log, ssd_D, ssd_norm_g, s5_A_re, s5_A_im, s5_log_step, s5_B_re, s5_B_im, s5_C_re, s5_C_im, s5_D, w_glu, b_glu, w_out, norm2_g, w_ffn_gate, w_ffn_up, w_ffn_down, w_ada_f, b_ada_f, normf_g):
    assert w_ada.shape[0] == 1, "single-layer stack"
    bp, seq, d = x_prompt.shape
    bs, dseq, _ = x_sample.shape

    mod, modf = _ada_mod(c_sample, c_prompt, (w_ada[0], w_ada_f), (b_ada[0], b_ada_f))
    seqs_s = dict(n_seq=bs, first_seq=0)
    seqs_p = dict(n_seq=bp, first_seq=bs)

    w_in_p = w_in[0].T
    pad_h = lambda v: jnp.concatenate([v, jnp.zeros((DT_PAD - HEADS,), f32)]).reshape(1, DT_PAD)
    dtb = pad_h(ssd_dt_bias[0])
    alog = pad_h(ssd_A_log[0])
    dexp = jnp.repeat(ssd_D[0], HEAD_DIM).reshape(1, D_SSD)
    ng = ssd_norm_g[0].reshape(1, D_SSD)
    cw, cb = conv_w[0], conv_b[0].reshape(1, CONV_DIM)

    ar, ai, bblk, cblk = _s5_params(s5_A_re[0], s5_A_im[0], s5_log_step[0],
                                    s5_B_re[0].transpose(0, 2, 1), s5_B_im[0].transpose(0, 2, 1),
                                    s5_C_re[0], s5_C_im[0])
    dsk = s5_D[0].reshape(1, D_S5)
    wglu = w_glu[0].astype(bf16)
    bglu = b_glu[0].reshape(1, D_S5)
    n1g, n2g, nfg = norm1_g[0].reshape(1, d), norm2_g[0].reshape(1, d), normf_g.reshape(1, d)
    nstate = S5_GROUPS * S5_STATE

    z, act, u5, dt, conv_p = _inproj(x_prompt, mod, n1g, w_in_p, (cw, cb), tm=1024, sub=512, seq_major=False,
                                     **seqs_p)
    y_ssd, ssd_p = _ssd_prompt(act, z, dt, dtb, alog, dexp, ng)
    zeros_st = jnp.zeros((bp, nstate), f32)
    y_s5, re_p, im_p, wo, wg, wu, wd = _s5(
        u5, zeros_st, zeros_st, ar, ai, bblk, cblk, dsk, wglu, bglu, tl=128, batch_major=True,
        cast=(w_out[0], w_ffn_gate[0], w_ffn_up[0], w_ffn_down[0]))
    y_prompt = _ffn(x_prompt, y_ssd, y_s5, mod, modf, n2g, nfg, wo, wg, wu, wd, tm=1024, sub=512,
                    seq_major=False, **seqs_p)

    rows_s = dseq * bs
    steps = lambda a: a.reshape(dseq, bs, a.shape[-1])
    flat = lambda a: a.reshape(1, rows_s, a.shape[-1])
    z, xbc, u5, dt = _inproj(x_sample, mod, n1g, w_in_p, tm=rows_s, sub=rows_s // 2, seq_major=True, **seqs_s)
    y_ssd, ssd_s, conv_s = _ssd_sample(steps(xbc), steps(z), steps(dt), state_conv[0].transpose(1, 0, 2),
                                       state_ssd[0], cw, cb, dtb, alog, dexp, ng)
    y_s5, re_s, im_s = _s5(steps(u5), state_s5_re[0].reshape(bs, nstate), state_s5_im[0].reshape(bs, nstate),
                           ar, ai, bblk, cblk, dsk, wglu, bglu, tl=dseq, batch_major=False)
    y_sample = _ffn(x_sample, flat(y_ssd), flat(y_s5), mod, modf, n2g, nfg, wo, wg, wu, wd,
                    tm=rows_s, sub=rows_s, seq_major=True, **seqs_s)

    g5 = (S5_GROUPS, S5_STATE)
    return (y_prompt, y_sample,
            ssd_p[None], ssd_s.reshape((1,) + state_ssd.shape[1:]),
            conv_p[None], conv_s.transpose(1, 0, 2)[None],
            re_p.reshape((1, bp) + g5), re_s.reshape((1, bs) + g5),
            im_p.reshape((1, bp) + g5), im_s.reshape((1, bs) + g5))
```
